```python
import jax, jax.numpy as jnp
from jax import lax
import numpy as np

D_MODEL = 1024
BATCH = 8
SEQ = 2048
DEPTH = 4

N_MIXERS = 2
N_DN_LAYERS = (DEPTH + 1) // 2
N_SB_LAYERS = DEPTH // 2
DN_HEADS = 8
DN_HEAD_DIM = D_MODEL // DN_HEADS
DN_CONV = 4
DN_CHUNK = 64
SB_HEADS = 16
SB_HEAD_DIM = D_MODEL // SB_HEADS
SB_BLOCK = 128
D_FF = -(-8 * D_MODEL // (3 * 256)) * 256
N_MOD = 6
EPS = 1e-6
ADA_INIT = 0.25

kernel_name = 'hybrid_deltanet_stickbreaking_adaln_trunk'


def rms_norm(x, g):
    xf = x.astype(jnp.float32)
    y = xf * lax.rsqrt(jnp.mean(xf * xf, axis=-1, keepdims=True) + EPS)
    return (y * g.astype(jnp.float32)).astype(x.dtype)


def l2_norm(x):
    return x * lax.rsqrt(jnp.sum(x * x, axis=-1, keepdims=True) + EPS)


def causal_dwconv(x, w):
    width = w.shape[0]
    return lax.conv_general_dilated(
        x, w[:, None, :].astype(x.dtype), window_strides=(1,), padding=[(width - 1, 0)],
        dimension_numbers=('NWC', 'WIO', 'NWC'), feature_group_count=x.shape[-1])


def chunk_gated_delta_rule(q, k, v, g, beta):
    B, T, H, dk = q.shape
    dv = v.shape[-1]
    C = DN_CHUNK
    N = T // C
    to_c = lambda t: t.transpose(0, 2, 1, 3).reshape(B, H, N, C, t.shape[-1])
    q = to_c(q) * (dk ** -0.5)
    k = to_c(k)
    v = to_c(v)
    g = g.transpose(0, 2, 1).reshape(B, H, N, C)
    beta = beta.transpose(0, 2, 1).reshape(B, H, N, C)
    G = jnp.cumsum(g, axis=-1)
    idx = jnp.arange(C)
    incl = idx[:, None] >= idx[None, :]
    strict = idx[:, None] > idx[None, :]
    decay = jnp.exp(jnp.where(incl, G[..., :, None] - G[..., None, :], -jnp.inf))
    k_beta = k * beta[..., None]
    v_beta = v * beta[..., None]
    A = jnp.where(strict, jnp.einsum('bhnid,bhnjd->bhnij', k_beta, k) * decay, 0.0)
    tri = jnp.eye(C, dtype=A.dtype) + A
    U = lax.linalg.triangular_solve(tri, v_beta, left_side=True, lower=True, unit_diagonal=True)
    W = lax.linalg.triangular_solve(tri, k_beta * jnp.exp(G)[..., None], left_side=True,
                                    lower=True, unit_diagonal=True)
    attn_intra = jnp.einsum('bhnid,bhnjd->bhnij', q, k) * decay
    q_dec = q * jnp.exp(G)[..., None]
    k_tail = k * jnp.exp(G[..., -1:] - G)[..., None]
    g_last = jnp.exp(G[..., -1])
    xs = tuple(jnp.moveaxis(t, 2, 0) for t in (q_dec, k_tail, U, W, attn_intra, g_last))

    def step(S, inp):
        qd, kt, u, w, a, gl = inp
        v_new = u - jnp.einsum('bhck,bhkv->bhcv', w, S)
        o = jnp.einsum('bhck,bhkv->bhcv', qd, S) + jnp.einsum('bhij,bhjv->bhiv', a, v_new)
        S = S * gl[..., None, None] + jnp.einsum('bhck,bhcv->bhkv', kt, v_new)
        return S, o

    S0 = jnp.zeros((B, H, dk, dv), jnp.float32)
    _, o = lax.scan(step, S0, xs)
    return jnp.moveaxis(o, 0, 2).reshape(B, H, T, dv).transpose(0, 2, 1, 3)


def gated_deltanet_mixer(h, w_in, conv_w, a_log, dt_bias, onorm_g, w_out):
    B, T, _ = h.shape
    H, d = DN_HEADS, DN_HEAD_DIM
    proj = h @ w_in
    qkv, z, a, b = jnp.split(proj, [3 * H * d, 4 * H * d, 4 * H * d + H], axis=-1)
    qkv = jax.nn.silu(causal_dwconv(qkv, conv_w)).astype(jnp.float32)
    q, k, v = [t.reshape(B, T, H, d) for t in jnp.split(qkv, 3, axis=-1)]
    q = l2_norm(q)
    k = l2_norm(k)
    beta = jax.nn.sigmoid(b.astype(jnp.float32))
    g = -jnp.exp(a_log.astype(jnp.float32)) * jax.nn.softplus(
        a.astype(jnp.float32) + dt_bias.astype(jnp.float32))
    o = chunk_gated_delta_rule(q, k, v, g, beta)
    o = rms_norm(o, onorm_g) * jax.nn.silu(z.reshape(B, T, H, d).astype(jnp.float32))
    return o.reshape(B, T, H * d).astype(h.dtype) @ w_out


def stick_breaking_mixer(h, w_qkv, q_norm_g, k_norm_g, w_out):
    B, T, _ = h.shape
    H, d = SB_HEADS, SB_HEAD_DIM
    qkv = (h @ w_qkv).reshape(B, T, 3, H, d)
    q = rms_norm(qkv[:, :, 0], q_norm_g).astype(jnp.float32).transpose(0, 2, 1, 3)
    k = rms_norm(qkv[:, :, 1], k_norm_g).astype(jnp.float32).transpose(0, 2, 1, 3)
    v = qkv[:, :, 2].astype(jnp.float32).transpose(0, 2, 1, 3)
    scale = d ** -0.5
    outs = []
    for blk in range(T // SB_BLOCK):
        q0 = blk * SB_BLOCK
        kv_len = q0 + SB_BLOCK
        z = jnp.einsum('bhqd,bhkd->bhqk', q[:, :, q0:kv_len], k[:, :, :kv_len]) * scale
        t_pos = q0 + jnp.arange(SB_BLOCK)
        s_pos = jnp.arange(kv_len)
        causal = s_pos[None, :] < t_pos[:, None]
        log_1m = jnp.where(causal, jax.nn.log_sigmoid(-z), 0.0)
        log_stick = lax.cumsum(log_1m, axis=3, reverse=True) - log_1m
        a = jnp.where(causal, jnp.exp(jax.nn.log_sigmoid(z) + log_stick), 0.0)
        outs.append(jnp.einsum('bhqk,bhkd->bhqd', a, v[:, :, :kv_len]))
    o = jnp.concatenate(outs, axis=2).transpose(0, 2, 1, 3).reshape(B, T, H * d)
    return o.astype(h.dtype) @ w_out


def swiglu(h, w_in, w_out):
    gate, up = jnp.split(h @ w_in, 2, axis=-1)
    return (jax.nn.silu(gate) * up) @ w_out


def _fwd_setup_inputs(seed: int = 0) -> dict:
    key = jax.random.key(seed)
    ks = jax.random.split(key, 20)
    D, H, d = D_MODEL, DN_HEADS, DN_HEAD_DIM
    nrm = lambda k, shape, s: jax.random.normal(k, shape, jnp.float32) * s
    dn_in_cols = 4 * H * d + 2 * H
    dt = jnp.exp(jax.random.uniform(ks[8], (N_DN_LAYERS, H), jnp.float32, np.log(1e-3), np.log(1e-1)))
    return {
        'x': nrm(ks[0], (BATCH, SEQ, D), 1.0),
        'c': nrm(ks[1], (BATCH, D), 1.0),
        'ada_w': nrm(ks[2], (DEPTH, D, N_MOD * D), ADA_INIT * D ** -0.5),
        'ada_b': nrm(ks[3], (DEPTH, N_MOD * D), 0.01),
        'norm1_g': 1.0 + nrm(ks[4], (DEPTH, D), 0.02),
        'norm2_g': 1.0 + nrm(ks[5], (DEPTH, D), 0.02),
        'dn_w_in': nrm(ks[6], (N_DN_LAYERS, D, dn_in_cols), D ** -0.5),
        'dn_conv_w': nrm(ks[7], (N_DN_LAYERS, DN_CONV, 3 * H * d), DN_CONV ** -0.5),
        'dn_a_log': jnp.log(jax.random.uniform(ks[9], (N_DN_LAYERS, H), jnp.float32, 1.0, 16.0)),
        'dn_dt_bias': dt + jnp.log(-jnp.expm1(-dt)),
        'dn_onorm_g': 1.0 + nrm(ks[10], (N_DN_LAYERS, d), 0.02),
        'dn_w_out': nrm(ks[11], (N_DN_LAYERS, H * d, D), (H * d) ** -0.5),
        'sb_w_qkv': nrm(ks[12], (N_SB_LAYERS, D, 3 * SB_HEADS * SB_HEAD_DIM), D ** -0.5),
        'sb_q_norm_g': 1.0 + nrm(ks[13], (N_SB_LAYERS, SB_HEAD_DIM), 0.02),
        'sb_k_norm_g': 1.0 + nrm(ks[14], (N_SB_LAYERS, SB_HEAD_DIM), 0.02),
        'sb_w_out': nrm(ks[15], (N_SB_LAYERS, SB_HEADS * SB_HEAD_DIM, D), (SB_HEADS * SB_HEAD_DIM) ** -0.5),
        'ffn_w_in': nrm(ks[16], (DEPTH, D, 2 * D_FF), D ** -0.5),
        'ffn_w_out': nrm(ks[17], (DEPTH, D_FF, D), D_FF ** -0.5),
    }


def _fwd_reference(x, c, ada_w, ada_b, norm1_g, norm2_g, dn_w_in, dn_conv_w, dn_a_log, dn_dt_bias,
              dn_onorm_g, dn_w_out, sb_w_qkv, sb_q_norm_g, sb_k_norm_g, sb_w_out, ffn_w_in, ffn_w_out):
    cond = jax.nn.silu(c)
    for i in range(DEPTH):
        mod = (cond @ ada_w[i] + ada_b[i])[:, None, :]
        sh1, sc1, gt1, sh2, sc2, gt2 = jnp.split(mod, N_MOD, axis=-1)
        h = rms_norm(x, norm1_g[i]) * (1.0 + sc1) + sh1
        j = i // N_MIXERS
        if i % N_MIXERS == 0:
            y = gated_deltanet_mixer(h, dn_w_in[j], dn_conv_w[j], dn_a_log[j], dn_dt_bias[j],
                                     dn_onorm_g[j], dn_w_out[j])
        else:
            y = stick_breaking_mixer(h, sb_w_qkv[j], sb_q_norm_g[j], sb_k_norm_g[j], sb_w_out[j])
        x = x + gt1 * y
        h = rms_norm(x, norm2_g[i]) * (1.0 + sc2) + sh2
        x = x + gt2 * swiglu(h, ffn_w_in[i], ffn_w_out[i])
    return x


import jax as _jax
import jax.numpy as _jnp

TWIN_FORMAT = 'train_step'
FWD_PARAMS = ['x', 'c', 'ada_w', 'ada_b', 'norm1_g', 'norm2_g', 'dn_w_in', 'dn_conv_w', 'dn_a_log', 'dn_dt_bias', 'dn_onorm_g', 'dn_w_out', 'sb_w_qkv', 'sb_q_norm_g', 'sb_k_norm_g', 'sb_w_out', 'ffn_w_in', 'ffn_w_out']
TWIN_WEIGHTS = ['ada_w', 'ada_b', 'norm1_g', 'norm2_g', 'dn_w_in', 'dn_conv_w', 'dn_a_log', 'dn_dt_bias', 'dn_onorm_g', 'dn_w_out', 'sb_w_qkv', 'sb_q_norm_g', 'sb_k_norm_g', 'sb_w_out', 'ffn_w_in', 'ffn_w_out']
TWIN_DIFF_INPUT = 'x'
TWIN_INPUTS = ['x', 'c', 'ada_w', 'ada_b', 'norm1_g', 'norm2_g', 'dn_w_in', 'dn_conv_w', 'dn_a_log', 'dn_dt_bias', 'dn_onorm_g', 'dn_w_out', 'sb_w_qkv', 'sb_q_norm_g', 'sb_k_norm_g', 'sb_w_out', 'ffn_w_in', 'ffn_w_out', 'loss_target', 'm_ada_w', 'm_ada_b', 'm_norm1_g', 'm_norm2_g', 'm_dn_w_in', 'm_dn_conv_w', 'm_dn_a_log', 'm_dn_dt_bias', 'm_dn_onorm_g', 'm_dn_w_out', 'm_sb_w_qkv', 'm_sb_q_norm_g', 'm_sb_k_norm_g', 'm_sb_w_out', 'm_ffn_w_in', 'm_ffn_w_out', 'v_ada_w', 'v_ada_b', 'v_norm1_g', 'v_norm2_g', 'v_dn_w_in', 'v_dn_conv_w', 'v_dn_a_log', 'v_dn_dt_bias', 'v_dn_onorm_g', 'v_dn_w_out', 'v_sb_w_qkv', 'v_sb_q_norm_g', 'v_sb_k_norm_g', 'v_sb_w_out', 'v_ffn_w_in', 'v_ffn_w_out']
TWIN_OUTPUTS = ['loss', 'grad_x', 'grad_ada_w', 'grad_ada_b', 'grad_norm1_g', 'grad_norm2_g', 'grad_dn_w_in', 'grad_dn_conv_w', 'grad_dn_a_log', 'grad_dn_dt_bias', 'grad_dn_onorm_g', 'grad_dn_w_out', 'grad_sb_w_qkv', 'grad_sb_q_norm_g', 'grad_sb_k_norm_g', 'grad_sb_w_out', 'grad_ffn_w_in', 'grad_ffn_w_out', 'delta_ada_w', 'delta_ada_b', 'delta_norm1_g', 'delta_norm2_g', 'delta_dn_w_in', 'delta_dn_conv_w', 'delta_dn_a_log', 'delta_dn_dt_bias', 'delta_dn_onorm_g', 'delta_dn_w_out', 'delta_sb_w_qkv', 'delta_sb_q_norm_g', 'delta_sb_k_norm_g', 'delta_sb_w_out', 'delta_ffn_w_in', 'delta_ffn_w_out', 'new_m_ada_w', 'new_m_ada_b', 'new_m_norm1_g', 'new_m_norm2_g', 'new_m_dn_w_in', 'new_m_dn_conv_w', 'new_m_dn_a_log', 'new_m_dn_dt_bias', 'new_m_dn_onorm_g', 'new_m_dn_w_out', 'new_m_sb_w_qkv', 'new_m_sb_q_norm_g', 'new_m_sb_k_norm_g', 'new_m_sb_w_out', 'new_m_ffn_w_in', 'new_m_ffn_w_out', 'new_v_ada_w', 'new_v_ada_b', 'new_v_norm1_g', 'new_v_norm2_g', 'new_v_dn_w_in', 'new_v_dn_conv_w', 'new_v_dn_a_log', 'new_v_dn_dt_bias', 'new_v_dn_onorm_g', 'new_v_dn_w_out', 'new_v_sb_w_qkv', 'new_v_sb_q_norm_g', 'new_v_sb_k_norm_g', 'new_v_sb_w_out', 'new_v_ffn_w_in', 'new_v_ffn_w_out']
TWIN_LEAF_KINDS = {'loss': 'loss', 'grad_x': 'grad_x', 'grad_ada_w': 'grad_w', 'grad_ada_b': 'grad_w', 'grad_norm1_g': 'grad_w', 'grad_norm2_g': 'grad_w', 'grad_dn_w_in': 'grad_w', 'grad_dn_conv_w': 'grad_w', 'grad_dn_a_log': 'grad_w', 'grad_dn_dt_bias': 'grad_w', 'grad_dn_onorm_g': 'grad_w', 'grad_dn_w_out': 'grad_w', 'grad_sb_w_qkv': 'grad_w', 'grad_sb_q_norm_g': 'grad_w', 'grad_sb_k_norm_g': 'grad_w', 'grad_sb_w_out': 'grad_w', 'grad_ffn_w_in': 'grad_w', 'grad_ffn_w_out': 'grad_w', 'delta_ada_w': 'delta_w', 'delta_ada_b': 'delta_w', 'delta_norm1_g': 'delta_w', 'delta_norm2_g': 'delta_w', 'delta_dn_w_in': 'delta_w', 'delta_dn_conv_w': 'delta_w', 'delta_dn_a_log': 'delta_w', 'delta_dn_dt_bias': 'delta_w', 'delta_dn_onorm_g': 'delta_w', 'delta_dn_w_out': 'delta_w', 'delta_sb_w_qkv': 'delta_w', 'delta_sb_q_norm_g': 'delta_w', 'delta_sb_k_norm_g': 'delta_w', 'delta_sb_w_out': 'delta_w', 'delta_ffn_w_in': 'delta_w', 'delta_ffn_w_out': 'delta_w', 'new_m_ada_w': 'new_m', 'new_m_ada_b': 'new_m', 'new_m_norm1_g': 'new_m', 'new_m_norm2_g': 'new_m', 'new_m_dn_w_in': 'new_m', 'new_m_dn_conv_w': 'new_m', 'new_m_dn_a_log': 'new_m', 'new_m_dn_dt_bias': 'new_m', 'new_m_dn_onorm_g': 'new_m', 'new_m_dn_w_out': 'new_m', 'new_m_sb_w_qkv': 'new_m', 'new_m_sb_q_norm_g': 'new_m', 'new_m_sb_k_norm_g': 'new_m', 'new_m_sb_w_out': 'new_m', 'new_m_ffn_w_in': 'new_m', 'new_m_ffn_w_out': 'new_m', 'new_v_ada_w': 'new_v', 'new_v_ada_b': 'new_v', 'new_v_norm1_g': 'new_v', 'new_v_norm2_g': 'new_v', 'new_v_dn_w_in': 'new_v', 'new_v_dn_conv_w': 'new_v', 'new_v_dn_a_log': 'new_v', 'new_v_dn_dt_bias': 'new_v', 'new_v_dn_onorm_g': 'new_v', 'new_v_dn_w_out': 'new_v', 'new_v_sb_w_qkv': 'new_v', 'new_v_sb_q_norm_g': 'new_v', 'new_v_sb_k_norm_g': 'new_v', 'new_v_sb_w_out': 'new_v', 'new_v_ffn_w_in': 'new_v', 'new_v_ffn_w_out': 'new_v'}


def _forward(args):
    return _fwd_reference(*[args[k] for k in FWD_PARAMS])


def _output_shape():
    out = _jax.eval_shape(lambda: _forward(_fwd_setup_inputs(0)))
    return out.shape, out.dtype

N_MICROBATCH = 1
ADAM_LR = 0.001
ADAM_B1 = 0.9
ADAM_B2 = 0.999
ADAM_EPS = 1e-08
ADAM_WD = 0.01
ADAM_STEP = 10
PER_EXAMPLE_BATCH_AXIS = {'x': 0, 'c': 0, 'loss_target': 0}
SHARED_INPUTS = []
_WEIGHT_DTYPES = {'ada_w': _jnp.float32, 'ada_b': _jnp.float32, 'norm1_g': _jnp.float32, 'norm2_g': _jnp.float32, 'dn_w_in': _jnp.float32, 'dn_conv_w': _jnp.float32, 'dn_a_log': _jnp.float32, 'dn_dt_bias': _jnp.float32, 'dn_onorm_g': _jnp.float32, 'dn_w_out': _jnp.float32, 'sb_w_qkv': _jnp.float32, 'sb_q_norm_g': _jnp.float32, 'sb_k_norm_g': _jnp.float32, 'sb_w_out': _jnp.float32, 'ffn_w_in': _jnp.float32, 'ffn_w_out': _jnp.float32}
MOMENT_SCALE = {'ada_w': 1.753481e-01, 'ada_b': 3.515143e-01, 'norm1_g': 1.602181e-01, 'norm2_g': 3.178280e-01, 'dn_w_in': 1.747388e-02, 'dn_conv_w': 1.689731e-02, 'dn_a_log': 2.622069e-01, 'dn_dt_bias': 2.491016e-01, 'dn_onorm_g': 1.181583e+00, 'dn_w_out': 2.317761e-02, 'sb_w_qkv': 1.509088e-02, 'sb_q_norm_g': 3.984054e-01, 'sb_k_norm_g': 3.979770e-01, 'sb_w_out': 2.238193e-02, 'ffn_w_in': 1.198699e-02, 'ffn_w_out': 1.990587e-02}


def _to_microbatches(a, axis):
    t = _jnp.moveaxis(a, axis, 0)
    t = t.reshape((N_MICROBATCH, t.shape[0] // N_MICROBATCH) + t.shape[1:])
    return _jnp.moveaxis(t, 1, axis + 1)


def setup_inputs(seed: int = 0) -> dict:
    inp = _fwd_setup_inputs(seed)
    key = _jax.random.fold_in(_jax.random.key(seed), 7919)
    shape, _ = _output_shape()
    out = dict(inp)
    out["loss_target"] = _jax.random.normal(_jax.random.fold_in(key, 0), shape, _jnp.float32)
    for i, name in enumerate(TWIN_WEIGHTS):
        w = inp[name].astype(_jnp.float32)
        if MOMENT_SCALE is None:
            s = _jnp.sqrt(_jnp.mean(_jnp.square(w)) + 1e-30)
        else:
            s = MOMENT_SCALE[name]
        km, kv = _jax.random.split(_jax.random.fold_in(key, i + 1))
        out[name] = w
        out["m_" + name] = s * _jax.random.normal(km, w.shape, _jnp.float32)
        out["v_" + name] = (s * s) * _jax.random.uniform(kv, w.shape, _jnp.float32, 0.5, 1.5)
    if N_MICROBATCH > 1:
        for name, axis in PER_EXAMPLE_BATCH_AXIS.items():
            out[name] = _to_microbatches(out[name], axis)
    return {'x': out['x'], 'c': out['c'], 'ada_w': out['ada_w'], 'ada_b': out['ada_b'], 'norm1_g': out['norm1_g'], 'norm2_g': out['norm2_g'], 'dn_w_in': out['dn_w_in'], 'dn_conv_w': out['dn_conv_w'], 'dn_a_log': out['dn_a_log'], 'dn_dt_bias': out['dn_dt_bias'], 'dn_onorm_g': out['dn_onorm_g'], 'dn_w_out': out['dn_w_out'], 'sb_w_qkv': out['sb_w_qkv'], 'sb_q_norm_g': out['sb_q_norm_g'], 'sb_k_norm_g': out['sb_k_norm_g'], 'sb_w_out': out['sb_w_out'], 'ffn_w_in': out['ffn_w_in'], 'ffn_w_out': out['ffn_w_out'], 'loss_target': out['loss_target'], 'm_ada_w': out['m_ada_w'], 'm_ada_b': out['m_ada_b'], 'm_norm1_g': out['m_norm1_g'], 'm_norm2_g': out['m_norm2_g'], 'm_dn_w_in': out['m_dn_w_in'], 'm_dn_conv_w': out['m_dn_conv_w'], 'm_dn_a_log': out['m_dn_a_log'], 'm_dn_dt_bias': out['m_dn_dt_bias'], 'm_dn_onorm_g': out['m_dn_onorm_g'], 'm_dn_w_out': out['m_dn_w_out'], 'm_sb_w_qkv': out['m_sb_w_qkv'], 'm_sb_q_norm_g': out['m_sb_q_norm_g'], 'm_sb_k_norm_g': out['m_sb_k_norm_g'], 'm_sb_w_out': out['m_sb_w_out'], 'm_ffn_w_in': out['m_ffn_w_in'], 'm_ffn_w_out': out['m_ffn_w_out'], 'v_ada_w': out['v_ada_w'], 'v_ada_b': out['v_ada_b'], 'v_norm1_g': out['v_norm1_g'], 'v_norm2_g': out['v_norm2_g'], 'v_dn_w_in': out['v_dn_w_in'], 'v_dn_conv_w': out['v_dn_conv_w'], 'v_dn_a_log': out['v_dn_a_log'], 'v_dn_dt_bias': out['v_dn_dt_bias'], 'v_dn_onorm_g': out['v_dn_onorm_g'], 'v_dn_w_out': out['v_dn_w_out'], 'v_sb_w_qkv': out['v_sb_w_qkv'], 'v_sb_q_norm_g': out['v_sb_q_norm_g'], 'v_sb_k_norm_g': out['v_sb_k_norm_g'], 'v_sb_w_out': out['v_sb_w_out'], 'v_ffn_w_in': out['v_ffn_w_in'], 'v_ffn_w_out': out['v_ffn_w_out']}


def _loss(weights, diff, rest, loss_target):
    with _jax.named_scope("forward"):
        args = {**rest, TWIN_DIFF_INPUT: diff, **{k: w.astype(_WEIGHT_DTYPES[k]) for k, w in weights.items()}}
        y = _forward(args)
    with _jax.named_scope("loss_head"):
        err = _jnp.square(y.astype(_jnp.float32) - loss_target)
        return 0.5 * _jnp.sum(_jnp.mean(err, axis=-1)) if err.ndim else 0.5 * err


def _adamw(w, g, m, v):
    m = ADAM_B1 * m + (1.0 - ADAM_B1) * g
    v = ADAM_B2 * v + (1.0 - ADAM_B2) * _jnp.square(g)
    m_hat = m / (1.0 - ADAM_B1 ** ADAM_STEP)
    v_hat = v / (1.0 - ADAM_B2 ** ADAM_STEP)
    delta = -ADAM_LR * (m_hat / (_jnp.sqrt(v_hat) + ADAM_EPS) + ADAM_WD * w)
    return delta, m, v


def reference(x, c, ada_w, ada_b, norm1_g, norm2_g, dn_w_in, dn_conv_w, dn_a_log, dn_dt_bias, dn_onorm_g, dn_w_out, sb_w_qkv, sb_q_norm_g, sb_k_norm_g, sb_w_out, ffn_w_in, ffn_w_out, loss_target, m_ada_w, m_ada_b, m_norm1_g, m_norm2_g, m_dn_w_in, m_dn_conv_w, m_dn_a_log, m_dn_dt_bias, m_dn_onorm_g, m_dn_w_out, m_sb_w_qkv, m_sb_q_norm_g, m_sb_k_norm_g, m_sb_w_out, m_ffn_w_in, m_ffn_w_out, v_ada_w, v_ada_b, v_norm1_g, v_norm2_g, v_dn_w_in, v_dn_conv_w, v_dn_a_log, v_dn_dt_bias, v_dn_onorm_g, v_dn_w_out, v_sb_w_qkv, v_sb_q_norm_g, v_sb_k_norm_g, v_sb_w_out, v_ffn_w_in, v_ffn_w_out):
    given = dict(x=x, c=c, ada_w=ada_w, ada_b=ada_b, norm1_g=norm1_g, norm2_g=norm2_g, dn_w_in=dn_w_in, dn_conv_w=dn_conv_w, dn_a_log=dn_a_log, dn_dt_bias=dn_dt_bias, dn_onorm_g=dn_onorm_g, dn_w_out=dn_w_out, sb_w_qkv=sb_w_qkv, sb_q_norm_g=sb_q_norm_g, sb_k_norm_g=sb_k_norm_g, sb_w_out=sb_w_out, ffn_w_in=ffn_w_in, ffn_w_out=ffn_w_out, loss_target=loss_target, m_ada_w=m_ada_w, m_ada_b=m_ada_b, m_norm1_g=m_norm1_g, m_norm2_g=m_norm2_g, m_dn_w_in=m_dn_w_in, m_dn_conv_w=m_dn_conv_w, m_dn_a_log=m_dn_a_log, m_dn_dt_bias=m_dn_dt_bias, m_dn_onorm_g=m_dn_onorm_g, m_dn_w_out=m_dn_w_out, m_sb_w_qkv=m_sb_w_qkv, m_sb_q_norm_g=m_sb_q_norm_g, m_sb_k_norm_g=m_sb_k_norm_g, m_sb_w_out=m_sb_w_out, m_ffn_w_in=m_ffn_w_in, m_ffn_w_out=m_ffn_w_out, v_ada_w=v_ada_w, v_ada_b=v_ada_b, v_norm1_g=v_norm1_g, v_norm2_g=v_norm2_g, v_dn_w_in=v_dn_w_in, v_dn_conv_w=v_dn_conv_w, v_dn_a_log=v_dn_a_log, v_dn_dt_bias=v_dn_dt_bias, v_dn_onorm_g=v_dn_onorm_g, v_dn_w_out=v_dn_w_out, v_sb_w_qkv=v_sb_w_qkv, v_sb_q_norm_g=v_sb_q_norm_g, v_sb_k_norm_g=v_sb_k_norm_g, v_sb_w_out=v_sb_w_out, v_ffn_w_in=v_ffn_w_in, v_ffn_w_out=v_ffn_w_out)
    weights = {n: given[n] for n in TWIN_WEIGHTS}
    shared = {n: given[n] for n in SHARED_INPUTS}
    per_example = {n: given[n] for n in ['x', 'c']}
    grad_fn = _jax.value_and_grad(_loss, argnums=(0, 1))

    def one_microbatch(ex, loss_target):
        ex = dict(ex)
        diff = ex.pop(TWIN_DIFF_INPUT)
        return grad_fn(weights, diff, {**shared, **ex}, loss_target)

    if N_MICROBATCH == 1:
        loss, (grad_w, grad_x) = one_microbatch(per_example, given["loss_target"])
    else:
        def body(carry, xs):
            loss_sum, grad_sum = carry
            l_k, (gw_k, gx_k) = one_microbatch(xs[0], xs[1])
            with _jax.named_scope("update"):
                return (loss_sum + l_k, _jax.tree.map(_jnp.add, grad_sum, gw_k)), gx_k

        init = (_jnp.zeros((), _jnp.float32), _jax.tree.map(_jnp.zeros_like, weights))
        (loss, grad_w), grad_x = _jax.lax.scan(body, init, (per_example, given["loss_target"]))
    with _jax.named_scope("update"):
        delta_w, new_m, new_v = {}, {}, {}
        for n in TWIN_WEIGHTS:
            delta_w[n], new_m[n], new_v[n] = _adamw(weights[n], grad_w[n], given["m_" + n], given["v_" + n])
    return (loss, grad_x, *[grad_w[n] for n in TWIN_WEIGHTS], *[delta_w[n] for n in TWIN_WEIGHTS],
            *[new_m[n] for n in TWIN_WEIGHTS], *[new_v[n] for n in TWIN_WEIGHTS])
```

```python
import functools

import jax
import jax.numpy as jnp
from jax import lax
from jax.experimental import pallas as pl
from jax.experimental.pallas import tpu as pltpu

F32 = jnp.float32
BF16 = jnp.bfloat16
HI = lax.Precision.HIGHEST
MESH = pl.DeviceIdType.MESH

EPS = 1e-6
N_MOD = 6
DN_HEAD_DIM = 128
DN_CONV = 4
DN_CHUNK = 64
DN_GROUP = 8
SB_HEAD_DIM = 64
SB_BLOCK = 128
LANES = 128
PACK_W = 1024
PACK_ALIGN = 256
VMEM_LIMIT = 56 * 1024 * 1024

ADAM_LR = 0.001
ADAM_B1 = 0.9
ADAM_B2 = 0.999
ADAM_EPS = 1e-08
ADAM_WD = 0.01
ADAM_STEP = 10


def _cparams(sem=None):
    return pltpu.CompilerParams(dimension_semantics=sem, vmem_limit_bytes=VMEM_LIMIT)


def _sigmoid(x):
    return 1.0 / (1.0 + jnp.exp(-x))


def _silu(x):
    return x * _sigmoid(x)


def _dsilu(x):
    s = _sigmoid(x)
    return s * (1.0 + x * (1.0 - s))


def _softplus(x):
    return jnp.maximum(x, 0.0) + jnp.log1p(jnp.exp(-jnp.abs(x)))


def _pick_tile(n, prefs):
    for t in prefs:
        if n % t == 0:
            return t
    return n


def _mm(a, b, *, ta=False, tb=False, out_dtype=F32, name):
    m = a.shape[1] if ta else a.shape[0]
    k = a.shape[0] if ta else a.shape[1]
    n = b.shape[0] if tb else b.shape[1]
    assert (b.shape[1] if tb else b.shape[0]) == k
    tm = _pick_tile(m, (512, 256, 128))
    tn = _pick_tile(n, (512, 256, 128))
    dims = (((0 if ta else 1,), (1 if tb else 0,)), ((), ()))

    def body(a_ref, b_ref, o_ref):
        av = a_ref[...].astype(BF16)
        bv = b_ref[...].astype(BF16)
        o_ref[...] = lax.dot_general(av, bv, dims, preferred_element_type=F32).astype(o_ref.dtype)

    a_spec = pl.BlockSpec((k, tm), lambda i, j: (0, i)) if ta else pl.BlockSpec((tm, k), lambda i, j: (i, 0))
    b_spec = pl.BlockSpec((tn, k), lambda i, j: (j, 0)) if tb else pl.BlockSpec((k, tn), lambda i, j: (0, j))
    return pl.pallas_call(
        body, grid=(m // tm, n // tn), in_specs=[a_spec, b_spec],
        out_specs=pl.BlockSpec((tm, tn), lambda i, j: (i, j)),
        out_shape=jax.ShapeDtypeStruct((m, n), out_dtype), name=name,
        compiler_params=_cparams(("parallel", "parallel")))(a, b)


def _rowwise(body, rows, vecs, outs, accs, *, name, tr=256):
    rows = [r if isinstance(r, tuple) else (r, r.shape[1], 0) for r in rows]
    nrows = rows[0][0].shape[0]
    tr = tr if nrows % tr == 0 else nrows
    nr, nv, no = len(rows), len(vecs), len(outs)

    def kern(*refs):
        r_in, v_in = refs[:nr], refs[nr:nr + nv]
        o_refs, a_refs = refs[nr + nv:nr + nv + no], refs[nr + nv + no:]
        res_o, res_a = body([r[...] for r in r_in], [v[...] for v in v_in])
        for o, val in zip(o_refs, res_o):
            o[...] = val.astype(o.dtype)
        if a_refs:
            @pl.when(pl.program_id(0) == 0)
            def _():
                for a in a_refs:
                    a[...] = jnp.zeros(a.shape, a.dtype)
            for a, val in zip(a_refs, res_a):
                a[...] += val

    in_specs = [pl.BlockSpec((tr, w), functools.partial(lambda i, cb: (i, cb), cb=cb)) for _, w, cb in rows]
    in_specs += [pl.BlockSpec(v.shape, functools.partial(lambda i, nd: (0,) * nd, nd=v.ndim)) for v in vecs]
    out_specs = [pl.BlockSpec((tr, w), lambda i: (i, 0)) for w, _ in outs]
    out_specs += [pl.BlockSpec(s, lambda i: (0, 0)) for s in accs]
    out_shape = [jax.ShapeDtypeStruct((nrows, w), dt) for w, dt in outs]
    out_shape += [jax.ShapeDtypeStruct(s, F32) for s in accs]
    res = pl.pallas_call(
        kern, grid=(nrows // tr,), in_specs=in_specs, out_specs=out_specs, out_shape=out_shape, name=name,
        compiler_params=_cparams(("arbitrary",) if accs else ("parallel",)))(*[r[0] for r in rows], *vecs)
    return res


def _colsum(v):
    return jnp.sum(v, axis=0, keepdims=True)


def _adaln_fwd(x, g, sc, sh, name):
    def body(r, v):
        (xv,), (gv, scv, shv) = r, v
        rs = lax.rsqrt(jnp.mean(xv * xv, axis=-1, keepdims=True) + EPS)
        return [(xv * rs) * gv * (1.0 + scv) + shv], []
    return _rowwise(body, [x], [g, sc, sh], [(x.shape[1], BF16)], [], name=name)[0]


def _adaln_bwd(x, dh, dxr, g, sc, name):
    d = x.shape[1]

    def body(r, v):
        (xv, dhv, dxv), (gv, scv) = r, v
        rs = lax.rsqrt(jnp.mean(xv * xv, axis=-1, keepdims=True) + EPS)
        nv = xv * rs
        dn = dhv * (gv * (1.0 + scv))
        dx = rs * (dn - nv * jnp.mean(dn * nv, axis=-1, keepdims=True)) + dxv
        dhn = dhv * nv
        return [dx], [_colsum(dhn * (1.0 + scv)), _colsum(dhn * gv), _colsum(dhv)]
    return _rowwise(body, [x, dh, dxr], [g, sc], [(d, F32)], [(1, d)] * 3, name=name)


def _resid_fwd(x, y, gt, name):
    def body(r, v):
        return [r[0] + v[0] * r[1]], []
    return _rowwise(body, [x, y], [gt], [(x.shape[1], F32)], [], name=name)[0]


def _resid_bwd(dx, y, gt, name):
    d = dx.shape[1]

    def body(r, v):
        return [v[0] * r[0]], [_colsum(r[0] * r[1])]
    return _rowwise(body, [dx, y], [gt], [(d, BF16)], [(1, d)], name=name)


def _swiglu_fwd(u, name):
    f = u.shape[1] // 2

    def body(r, v):
        uv = r[0]
        return [_silu(uv[:, :f]) * uv[:, f:]], []
    return _rowwise(body, [u], [], [(f, BF16)], [], name=name)[0]


def _swiglu_bwd(u, da, name):
    f = u.shape[1] // 2

    def body(r, v):
        uv, dav = r
        gate, up = uv[:, :f], uv[:, f:]
        return [jnp.concatenate([dav * up * _dsilu(gate), dav * _silu(gate)], axis=1)], []
    return _rowwise(body, [u, da], [], [(2 * f, BF16)], [], name=name)[0]


def _loss_fwd_bwd(y, tgt, name):
    d = y.shape[1]

    def body(r, v):
        err = r[0] - r[1]
        part = jnp.sum(jnp.sum(err * err, axis=1, keepdims=True), axis=0, keepdims=True) * (0.5 / d)
        return [err * (1.0 / d)], [jnp.broadcast_to(part, (1, LANES))]
    dy, acc = _rowwise(body, [y, tgt], [], [(d, F32)], [(1, LANES)], name=name)
    return acc[0, 0], dy


def _rms_rows_fwd(x, g, name):
    def body(r, v):
        xv = r[0]
        rs = lax.rsqrt(jnp.mean(xv * xv, axis=-1, keepdims=True) + EPS)
        return [xv * rs * v[0]], []
    return _rowwise(body, [x], [g], [(x.shape[1], F32)], [], name=name, tr=2048)[0]


def _rms_rows_bwd(x, dy, g, name):
    d = x.shape[1]

    def body(r, v):
        xv, dyv = r
        rs = lax.rsqrt(jnp.mean(xv * xv, axis=-1, keepdims=True) + EPS)
        nv = xv * rs
        dn = dyv * v[0]
        return [rs * (dn - nv * jnp.mean(dn * nv, axis=-1, keepdims=True))], [_colsum(dyv * nv)]
    return _rowwise(body, [x, dy], [g], [(d, F32)], [(1, d)], name=name, tr=2048)


def _dn_post_fwd(o, z, g, name):
    def body(r, v):
        ov, zv = r
        rs = lax.rsqrt(jnp.mean(ov * ov, axis=-1, keepdims=True) + EPS)
        return [ov * rs * v[0] * _silu(zv)], []
    return _rowwise(body, [o, z], [g], [(o.shape[1], BF16)], [], name=name, tr=2048)[0]


def _dn_post_bwd(o, z, don, g, name):
    d = o.shape[1]

    def body(r, v):
        ov, zv, dv = r
        rs = lax.rsqrt(jnp.mean(ov * ov, axis=-1, keepdims=True) + EPS)
        nv = ov * rs
        s = _silu(zv)
        dn = dv * v[0] * s
        do = rs * (dn - nv * jnp.mean(dn * nv, axis=-1, keepdims=True))
        dz = dv * nv * v[0] * _dsilu(zv)
        return [do, dz], [_colsum(dv * nv * s)]
    return _rowwise(body, [o, z, don], [g], [(d, F32), (d, BF16)], [(1, d)], name=name, tr=2048)


def _chunk_tri(tr, upper):
    row = lax.broadcasted_iota(jnp.int32, (tr, tr), 0)
    col = lax.broadcasted_iota(jnp.int32, (tr, tr), 1)
    same = (row // DN_CHUNK) == (col // DN_CHUNK)
    return (same & ((row <= col) if upper else (row >= col))).astype(F32)


def _dn_gate_fwd(proj, colblk, a_log, dt_bias, name):
    def body(r, v):
        ab = r[0]
        a, b = ab[:, :LANES], ab[:, LANES:]
        g = -jnp.exp(v[0]) * _softplus(a + v[1])
        big_g = jnp.dot(_chunk_tri(g.shape[0], False), g, precision=HI, preferred_element_type=F32)
        return [big_g, _sigmoid(b)], []
    return _rowwise(body, [(proj, 2 * LANES, colblk)], [a_log, dt_bias], [(LANES, F32), (LANES, F32)], [], name=name)


def _dn_gate_bwd(proj, colblk, d_big_g, dbeta, a_log, dt_bias, name):
    def body(r, v):
        ab, dgc, dbt = r
        a, b = ab[:, :LANES], ab[:, LANES:]
        dg = jnp.dot(_chunk_tri(dgc.shape[0], True), dgc, precision=HI, preferred_element_type=F32)
        na = -jnp.exp(v[0])
        pre = a + v[1]
        da = dg * na * _sigmoid(pre)
        beta = _sigmoid(b)
        db = dbt * beta * (1.0 - beta)
        return [jnp.concatenate([da, db], axis=1)], [_colsum(dg * na * _softplus(pre)), _colsum(da)]
    return _rowwise(body, [(proj, 2 * LANES, colblk), d_big_g, dbeta], [a_log, dt_bias],
                    [(2 * LANES, BF16)], [(1, LANES), (1, LANES)], name=name)


def _shift_rows(x, s):
    if s == 0:
        return x
    t = x.shape[0]
    row = lax.broadcasted_iota(jnp.int32, x.shape, 0)
    rolled = pltpu.roll(x, s % t, axis=0)
    return jnp.where((row >= s) if s > 0 else (row < t + s), rolled, 0.0)


def _rnd(x):
    return x.astype(BF16).astype(F32)


def _conv(x, c_ref):
    c = x * _rnd(c_ref[DN_CONV - 1:DN_CONV, :])
    for s in range(1, DN_CONV):
        c = c + _shift_rows(x, s) * _rnd(c_ref[DN_CONV - 1 - s:DN_CONV - s, :])
    return c


def _dn_prep_fwd(proj, conv_w, heads, name):
    t = proj.shape[0]
    d = heads * DN_HEAD_DIM

    def body(xq, xk, xv, cq, ck, cv, q_ref, k_ref, v_ref):
        for x_ref, c_ref, o_ref, norm in ((xq, cq, q_ref, True), (xk, ck, k_ref, True), (xv, cv, v_ref, False)):
            y = _silu(_conv(_rnd(x_ref[...]), c_ref))
            if norm:
                y = y * lax.rsqrt(jnp.sum(y * y, axis=-1, keepdims=True) + EPS)
            o_ref[...] = y

    xs = [pl.BlockSpec((t, DN_HEAD_DIM), functools.partial(lambda h, o: (0, h + o), o=o * heads)) for o in range(3)]
    cs = [pl.BlockSpec((DN_CONV, DN_HEAD_DIM), functools.partial(lambda h, o: (0, h + o), o=o * heads)) for o in range(3)]
    return pl.pallas_call(
        body, grid=(heads,), in_specs=xs + cs,
        out_specs=[pl.BlockSpec((t, DN_HEAD_DIM), lambda h: (0, h))] * 3,
        out_shape=[jax.ShapeDtypeStruct((t, d), F32)] * 3, name=name,
        compiler_params=_cparams(("parallel",)))(proj, proj, proj, conv_w, conv_w, conv_w)


def _dn_prep_bwd(proj, conv_w, dq, dk, dv, heads, name):
    t = proj.shape[0]
    d = heads * DN_HEAD_DIM

    def body(xq, xk, xv, cq, ck, cv, gq, gk, gv, oq, ok, ov, wq, wk, wv):
        for x_ref, c_ref, g_ref, o_ref, w_ref, norm in ((xq, cq, gq, oq, wq, True), (xk, ck, gk, ok, wk, True),
                                                        (xv, cv, gv, ov, wv, False)):
            x, dy = _rnd(x_ref[...]), g_ref[...]
            c = _conv(x, c_ref)
            if norm:
                y = _silu(c)
                rs = lax.rsqrt(jnp.sum(y * y, axis=-1, keepdims=True) + EPS)
                yn = y * rs
                dy = rs * (dy - yn * jnp.sum(dy * yn, axis=-1, keepdims=True))
            dc = _rnd(dy * _dsilu(c))
            dx = dc * _rnd(c_ref[DN_CONV - 1:DN_CONV, :])
            w_ref[DN_CONV - 1:DN_CONV, :] = _colsum(dc * x)
            for s in range(1, DN_CONV):
                dx = dx + _shift_rows(dc, -s) * _rnd(c_ref[DN_CONV - 1 - s:DN_CONV - s, :])
                w_ref[DN_CONV - 1 - s:DN_CONV - s, :] = _colsum(dc * _shift_rows(x, s))
            o_ref[...] = dx.astype(o_ref.dtype)

    xs = [pl.BlockSpec((t, DN_HEAD_DIM), functools.partial(lambda h, o: (0, h + o), o=o * heads)) for o in range(3)]
    cs = [pl.BlockSpec((DN_CONV, DN_HEAD_DIM), functools.partial(lambda h, o: (0, h + o), o=o * heads)) for o in range(3)]
    hs = pl.BlockSpec((t, DN_HEAD_DIM), lambda h: (0, h))
    ws = pl.BlockSpec((DN_CONV, DN_HEAD_DIM), lambda h: (0, h))
    return pl.pallas_call(
        body, grid=(heads,), in_specs=xs + cs + [hs] * 3, out_specs=[hs] * 3 + [ws] * 3,
        out_shape=[jax.ShapeDtypeStruct((t, d), BF16)] * 3 + [jax.ShapeDtypeStruct((DN_CONV, d), F32)] * 3, name=name,
        compiler_params=_cparams(("parallel",)))(proj, proj, proj, conv_w, conv_w, conv_w, dq, dk, dv)


def _bmm(a, b, prec=None):
    return lax.dot_general(a, b, (((2,), (1,)), ((0,), (0,))), precision=prec, preferred_element_type=F32)


def _bmm_nt(a, b, prec=None):
    return lax.dot_general(a, b, (((2,), (2,)), ((0,), (0,))), precision=prec, preferred_element_type=F32)


def _bmm_tn(a, b, prec=None):
    return lax.dot_general(a, b, (((1,), (1,)), ((0,), (0,))), precision=prec, preferred_element_type=F32)


def _dn_local(qg, kg, vg, gc, gr, bt):
    c = qg.shape[1]
    row = lax.broadcasted_iota(jnp.int32, (c, c), 0)
    col = lax.broadcasted_iota(jnp.int32, (c, c), 1)
    incl, strict = (row >= col)[None], (row > col)[None]
    decay = jnp.where(incl, jnp.exp(jnp.where(incl, gc - gr, 0.0)), 0.0)
    kb = kg * bt
    vb = vg * bt
    m = _bmm_nt(kb.astype(BF16), kg.astype(BF16))
    a = jnp.where(strict, m * decay, 0.0)
    bp = -a
    tm = jnp.where((row == col)[None], 1.0, 0.0) + bp
    steps = max(1, (c - 1).bit_length()) - 1
    for _ in range(steps):
        bp = _bmm(bp, bp, HI)
        tm = tm + _bmm(tm, bp, HI)
    eg = jnp.exp(gc)
    glast = gc[:, c - 1:c, :]
    erel = jnp.exp(glast - gc)
    kbg = kb * eg
    qm = _bmm_nt(qg.astype(BF16), kg.astype(BF16))
    return dict(decay=decay, strict=strict, kb=kb, vb=vb, m=m, tm=tm, eg=eg, erel=erel, kbg=kbg, qm=qm,
                u=_bmm(tm, vb, HI), w=_bmm(tm, kbg, HI), qd=qg * eg, kt=kg * erel, gl=jnp.exp(glast))


def _dot(a, b):
    return jnp.dot(a.astype(BF16), b.astype(BF16), preferred_element_type=F32)


def _dot_nt(a, b):
    return lax.dot_general(a.astype(BF16), b.astype(BF16), (((1,), (1,)), ((), ())), preferred_element_type=F32)


def _dot_tn(a, b):
    return lax.dot_general(a.astype(BF16), b.astype(BF16), (((0,), (0,)), ((), ())), preferred_element_type=F32)


def _dn_chunk_specs(t, heads):
    n = t // DN_CHUNK
    hs = pl.BlockSpec((t, DN_HEAD_DIM), lambda h: (0, h))
    gcs = pl.BlockSpec((1, n, DN_CHUNK, 1), lambda h: (h, 0, 0, 0))
    grs = pl.BlockSpec((1, n, 1, DN_CHUNK), lambda h: (h, 0, 0, 0))
    ss = pl.BlockSpec((1, n, DN_HEAD_DIM, DN_HEAD_DIM), lambda h: (h, 0, 0, 0))
    return n, hs, gcs, grs, ss


def _dn_chunk_fwd(q, k, v, gc, gr, bc, name):
    t, d = q.shape
    heads = d // DN_HEAD_DIM
    c, dk = DN_CHUNK, DN_HEAD_DIM
    n, hs, gcs, grs, ss = _dn_chunk_specs(t, heads)
    nb = min(DN_GROUP, n)
    scale = dk ** -0.5

    def body(q_ref, k_ref, v_ref, gc_ref, gr_ref, b_ref, o_ref, s_ref, u_s, w_s, at_s, qd_s, kt_s, gl_s):
        def group(gi, carry):
            r0 = pl.multiple_of(gi * (nb * c), nb * c)
            n0 = gi * nb
            ld = lambda ref: ref[pl.ds(r0, nb * c), :].reshape(nb, c, dk)
            loc = _dn_local(ld(q_ref) * scale, ld(k_ref), ld(v_ref), gc_ref[0, pl.ds(n0, nb)],
                            gr_ref[0, pl.ds(n0, nb)], b_ref[0, pl.ds(n0, nb)])
            u_s[pl.ds(n0, nb)] = loc["u"]
            w_s[pl.ds(n0, nb)] = loc["w"]
            at_s[pl.ds(n0, nb)] = loc["qm"] * loc["decay"]
            qd_s[pl.ds(n0, nb)] = loc["qd"]
            kt_s[pl.ds(n0, nb)] = loc["kt"]
            gl_s[pl.ds(n0, nb)] = loc["gl"]
            return carry
        lax.fori_loop(0, n // nb, group, 0)

        def chunk(i, s):
            s_ref[0, i] = s
            vnew = u_s[i] - _dot(w_s[i], s)
            o = _dot(qd_s[i], s) + _dot(at_s[i], vnew)
            o_ref[pl.ds(pl.multiple_of(i * c, c), c), :] = o
            return s * gl_s[i] + _dot_tn(kt_s[i], vnew)
        lax.fori_loop(0, n, chunk, jnp.zeros((dk, dk), F32))

    scratch = [pltpu.VMEM((n, c, dk), F32), pltpu.VMEM((n, c, dk), F32), pltpu.VMEM((n, c, c), F32),
               pltpu.VMEM((n, c, dk), F32), pltpu.VMEM((n, c, dk), F32), pltpu.VMEM((n, 1, 1), F32)]
    return pl.pallas_call(
        body, grid=(heads,), in_specs=[hs, hs, hs, gcs, grs, gcs], out_specs=[hs, ss],
        out_shape=[jax.ShapeDtypeStruct((t, d), F32), jax.ShapeDtypeStruct((heads, n, dk, dk), F32)],
        scratch_shapes=scratch, name=name, compiler_params=_cparams(("parallel",)))(q, k, v, gc, gr, bc)


def _dn_chunk_bwd(q, k, v, gc, gr, bc, s_all, do, name):
    t, d = q.shape
    heads = d // DN_HEAD_DIM
    c, dk = DN_CHUNK, DN_HEAD_DIM
    n, hs, gcs, grs, ss = _dn_chunk_specs(t, heads)
    nb = min(DN_GROUP, n)
    scale = dk ** -0.5

    def body(q_ref, k_ref, v_ref, gc_ref, gr_ref, b_ref, s_ref, do_ref,
             dq_ref, dk_ref, dv_ref, dgc_ref, dgr_ref, db_ref,
             u_s, w_s, att_s, qd_s, kt_s, gl_s, du_s, dw_s, dat_s, dqd_s, dkt_s, dgl_s):
        def load_group(gi):
            r0 = pl.multiple_of(gi * (nb * c), nb * c)
            n0 = gi * nb
            ld = lambda ref: ref[pl.ds(r0, nb * c), :].reshape(nb, c, dk)
            qg, kg, vg = ld(q_ref) * scale, ld(k_ref), ld(v_ref)
            gcv, grv, bt = gc_ref[0, pl.ds(n0, nb)], gr_ref[0, pl.ds(n0, nb)], b_ref[0, pl.ds(n0, nb)]
            return r0, n0, qg, kg, vg, gcv, grv, bt, _dn_local(qg, kg, vg, gcv, grv, bt)

        def group_a(gi, carry):
            _, n0, qg, kg, _, gcv, grv, _, loc = load_group(gi)
            row = lax.broadcasted_iota(jnp.int32, (c, c), 0)
            col = lax.broadcasted_iota(jnp.int32, (c, c), 1)
            upper = (col >= row)[None]
            decay_t = jnp.where(upper, jnp.exp(jnp.where(upper, grv - gcv, 0.0)), 0.0)
            u_s[pl.ds(n0, nb)] = loc["u"]
            w_s[pl.ds(n0, nb)] = loc["w"]
            att_s[pl.ds(n0, nb)] = _bmm_nt(kg.astype(BF16), qg.astype(BF16)) * decay_t
            qd_s[pl.ds(n0, nb)] = loc["qd"]
            kt_s[pl.ds(n0, nb)] = loc["kt"]
            gl_s[pl.ds(n0, nb)] = loc["gl"]
            return carry
        lax.fori_loop(0, n // nb, group_a, 0)

        def chunk_b(it, ds_next):
            i = n - 1 - it
            s = s_ref[0, i]
            dov = do_ref[pl.ds(pl.multiple_of(i * c, c), c), :]
            w, kt = w_s[i], kt_s[i]
            vnew = u_s[i] - _dot(w, s)
            dvnew = _dot(att_s[i], dov) + _dot(kt, ds_next)
            du_s[i] = dvnew
            dw_s[i] = -_dot_nt(dvnew, s)
            dat_s[i] = _dot_nt(dov, vnew)
            dqd_s[i] = _dot_nt(dov, s)
            dkt_s[i] = _dot_nt(vnew, ds_next)
            dgl_s[i] = jnp.sum(jnp.sum(ds_next * s, axis=1, keepdims=True), axis=0, keepdims=True)
            return ds_next * gl_s[i] + _dot_tn(qd_s[i], dov) - _dot_tn(w, dvnew)
        lax.fori_loop(0, n, chunk_b, jnp.zeros((dk, dk), F32))

        def group_c(gi, carry):
            r0, n0, qg, kg, vg, gcv, grv, bt, loc = load_group(gi)
            sl = pl.ds(n0, nb)
            du, dw, dat, dqd, dkt, dgl = du_s[sl], dw_s[sl], dat_s[sl], dqd_s[sl], dkt_s[sl], dgl_s[sl]
            tm, decay, kb, kbg = loc["tm"], loc["decay"], loc["kb"], loc["kbg"]
            dvb = _bmm_tn(tm, du, HI)
            dkbg = _bmm_tn(tm, dw, HI)
            dt = _bmm_nt(du, loc["vb"], HI) + _bmm_nt(dw, kbg, HI)
            da = jnp.where(loc["strict"], -_bmm_tn(tm, _bmm_nt(dt, tm, HI), HI), 0.0)
            dms = (da * decay).astype(BF16)
            dqs = (dat * decay).astype(BF16)
            kgb = kg.astype(BF16)
            dkb = _bmm(dms, kgb) + dkbg * loc["eg"]
            dqt = _bmm(dqs, kgb) + dqd * loc["eg"]
            dkk = _bmm_tn(dms, kb.astype(BF16)) + _bmm_tn(dqs, qg.astype(BF16)) + dkt * loc["erel"] + dkb * bt
            e = (da * loc["m"] + dat * loc["qm"]) * decay
            lsum = lambda x: jnp.sum(x, axis=2, keepdims=True)
            dkt_kt = lsum(dkt * loc["kt"])
            dgcv = lsum(e) + lsum(dqd * loc["qd"]) - dkt_kt + lsum(dkbg * kbg)
            dglast = jnp.sum(dkt_kt, axis=1, keepdims=True) + dgl * loc["gl"]
            rowc = lax.broadcasted_iota(jnp.int32, (1, c, 1), 1)
            dgc_ref[0, sl] = dgcv + jnp.where(rowc == c - 1, dglast, 0.0)
            dgr_ref[0, sl] = -jnp.sum(e, axis=1, keepdims=True)
            db_ref[0, sl] = lsum(dkb * kg) + lsum(dvb * vg)
            rows = pl.ds(r0, nb * c)
            dq_ref[rows, :] = (dqt * scale).reshape(nb * c, dk)
            dk_ref[rows, :] = dkk.reshape(nb * c, dk)
            dv_ref[rows, :] = (dvb * bt).reshape(nb * c, dk)
            return carry
        lax.fori_loop(0, n // nb, group_c, 0)

    big = lambda: pltpu.VMEM((n, c, dk), F32)
    sq = lambda: pltpu.VMEM((n, c, c), F32)
    one = lambda: pltpu.VMEM((n, 1, 1), F32)
    scratch = [big(), big(), sq(), big(), big(), one(), big(), big(), sq(), big(), big(), one()]
    return pl.pallas_call(
        body, grid=(heads,), in_specs=[hs, hs, hs, gcs, grs, gcs, ss, hs], out_specs=[hs, hs, hs, gcs, grs, gcs],
        out_shape=[jax.ShapeDtypeStruct((t, d), F32)] * 3 + [
            jax.ShapeDtypeStruct((heads, n, c, 1), F32), jax.ShapeDtypeStruct((heads, n, 1, c), F32),
            jax.ShapeDtypeStruct((heads, n, c, 1), F32)],
        scratch_shapes=scratch, name=name, compiler_params=_cparams(("parallel",)))(q, k, v, gc, gr, bc, s_all, do)


def _sb_scores(q, kj, qi, j, blk, r):
    row = lax.broadcasted_iota(jnp.int32, (blk, blk), 0)
    col = lax.broadcasted_iota(jnp.int32, (blk, blk), 1)
    z = lax.dot_general(q, kj, (((1,), (1,)), ((), ())), preferred_element_type=F32)
    mask = (col + j * blk) < (row + qi * blk)
    lsm = -_softplus(z)
    lm = jnp.where(mask, lsm, 0.0)
    later = (row > col).astype(F32)
    cs = jnp.dot(lm, later, precision=HI, preferred_element_type=F32)
    ls = z + lsm
    a = jnp.where(mask, jnp.exp(ls + cs + r), 0.0)
    return a, mask, ls, lsm, r + jnp.sum(lm, axis=1, keepdims=True)


def _sb_fwd(qn, kn, v, name):
    heads, t, d = qn.shape
    blk = min(SB_BLOCK, t)
    scale = d ** -0.5

    def body(q_ref, k_ref, v_ref, o_ref):
        qi = pl.program_id(1)
        q = (q_ref[0] * scale).astype(BF16)

        def step(it, carry):
            acc, r = carry
            j = qi - it
            rows = pl.ds(pl.multiple_of(j * blk, blk), blk)
            a, _, _, _, r = _sb_scores(q, k_ref[0, rows, :].astype(BF16), qi, j, blk, r)
            return acc + _dot(a, v_ref[0, rows, :]), r
        acc, _ = lax.fori_loop(0, qi + 1, step, (jnp.zeros((blk, d), F32), jnp.zeros((blk, 1), F32)))
        o_ref[0] = acc

    qs = pl.BlockSpec((1, blk, d), lambda h, i: (h, i, 0))
    ks = pl.BlockSpec((1, t, d), lambda h, i: (h, 0, 0))
    return pl.pallas_call(
        body, grid=(heads, t // blk), in_specs=[qs, ks, ks], out_specs=qs,
        out_shape=jax.ShapeDtypeStruct((heads, t, d), F32), name=name,
        compiler_params=_cparams(("parallel", "parallel")))(qn, kn, v)


def _sb_bwd(qn, kn, v, do, name):
    heads, t, d = qn.shape
    blk = min(SB_BLOCK, t)
    scale = d ** -0.5

    def body(q_ref, k_ref, v_ref, do_ref, dq_ref, dk_ref, dv_ref, p_s, sg_s):
        qi = pl.program_id(1)

        @pl.when(qi == 0)
        def _():
            dk_ref[...] = jnp.zeros(dk_ref.shape, F32)
            dv_ref[...] = jnp.zeros(dv_ref.shape, F32)

        qs = q_ref[0] * scale
        q = qs.astype(BF16)
        dov = do_ref[0]
        row = lax.broadcasted_iota(jnp.int32, (blk, blk), 0)
        col = lax.broadcasted_iota(jnp.int32, (blk, blk), 1)
        later_incl = (row >= col).astype(F32)
        zero = jnp.zeros((blk, 1), F32)

        def sweep1(it, carry):
            r, sp = carry
            j = qi - it
            rows = pl.ds(pl.multiple_of(j * blk, blk), blk)
            a, _, ls, _, r = _sb_scores(q, k_ref[0, rows, :].astype(BF16), qi, j, blk, r)
            p = a * _dot_nt(dov, v_ref[0, rows, :])
            p_s[j] = p
            sg_s[j] = jnp.exp(ls)
            dv_ref[0, rows, :] += _dot_tn(a, dov)
            return r, sp + jnp.sum(p, axis=1, keepdims=True)
        _, total = lax.fori_loop(0, qi + 1, sweep1, (zero, zero))

        def sweep2(it, carry):
            dq, sp = carry
            j = qi - it
            rows = pl.ds(pl.multiple_of(j * blk, blk), blk)
            p, sg = p_s[j], sg_s[j]
            pref = total - sp - jnp.dot(p, later_incl, precision=HI, preferred_element_type=F32)
            mask = (col + j * blk) < (row + qi * blk)
            dz = jnp.where(mask, p * (1.0 - sg) - pref * sg, 0.0)
            dk_ref[0, rows, :] += _dot_tn(dz, qs)
            return dq + _dot(dz, k_ref[0, rows, :]), sp + jnp.sum(p, axis=1, keepdims=True)
        dq, _ = lax.fori_loop(0, qi + 1, sweep2, (jnp.zeros((blk, d), F32), zero))
        dq_ref[0] = dq * scale

    qs_ = pl.BlockSpec((1, blk, d), lambda h, i: (h, i, 0))
    ks_ = pl.BlockSpec((1, t, d), lambda h, i: (h, 0, 0))
    shp = jax.ShapeDtypeStruct((heads, t, d), F32)
    scratch = [pltpu.VMEM((t // blk, blk, blk), F32), pltpu.VMEM((t // blk, blk, blk), F32)]
    return pl.pallas_call(
        body, grid=(heads, t // blk), in_specs=[qs_, ks_, ks_, qs_], out_specs=[qs_, ks_, ks_],
        out_shape=[shp, shp, shp], scratch_shapes=scratch, name=name,
        compiler_params=_cparams(("parallel", "arbitrary")))(qn, kn, v, do)


def _adamw(w, g, m, v, name):
    shape = w.shape
    two_d = lambda a: a.reshape(-1, shape[-1])
    c1 = 1.0 - ADAM_B1 ** ADAM_STEP
    c2 = 1.0 - ADAM_B2 ** ADAM_STEP

    def body(r, _):
        wv, gv, mv, vv = r
        mn = ADAM_B1 * mv + (1.0 - ADAM_B1) * gv
        vn = ADAM_B2 * vv + (1.0 - ADAM_B2) * (gv * gv)
        delta = -ADAM_LR * ((mn / c1) / (jnp.sqrt(vn / c2) + ADAM_EPS) + ADAM_WD * wv)
        return [delta, mn, vn], []
    width = shape[-1]
    res = _rowwise(body, [two_d(w), two_d(g), two_d(m), two_d(v)], [], [(width, F32)] * 3, [], name=name)
    return [r.reshape(shape) for r in res]


def _dn_layer_fwd(h1, w, conv_w, a_log, dt_bias, onorm_g, tag):
    t, d = h1.shape
    heads = d // DN_HEAD_DIM
    n = t // DN_CHUNK
    proj = _mm(h1, w["all"], name=tag + "_proj")
    qn, kn, vv = _dn_prep_fwd(proj, conv_w, heads, tag + "_prep")
    big_g, beta = _dn_gate_fwd(proj, 4 * d // (2 * LANES), a_log, dt_bias, tag + "_gate")
    gt_ = big_g[:, :heads].T.reshape(heads, n, DN_CHUNK)
    gc, gr = gt_[..., None], gt_[:, :, None, :]
    bc = beta[:, :heads].T.reshape(heads, n, DN_CHUNK)[..., None]
    o, s_all = _dn_chunk_fwd(qn, kn, vv, gc, gr, bc, tag + "_chunk")
    z = proj[:, 3 * d:4 * d].reshape(t * heads, DN_HEAD_DIM)
    on = _dn_post_fwd(o.reshape(t * heads, DN_HEAD_DIM), z, onorm_g, tag + "_post").reshape(t, d)
    y = _mm(on, w["out"], name=tag + "_out")
    return y, dict(proj=proj, qn=qn, kn=kn, v=vv, gc=gc, gr=gr, bc=bc, o=o, s_all=s_all, z=z, on=on)


def _dn_layer_bwd(dy, h1, w, conv_w, a_log, dt_bias, onorm_g, sv, tag):
    t, d = h1.shape
    heads = d // DN_HEAD_DIM
    don = _mm(dy, w["out"], tb=True, name=tag + "_dout")
    g_out = _mm(sv["on"], dy, ta=True, name=tag + "_gwout")
    do, dz, g_on = _dn_post_bwd(sv["o"].reshape(t * heads, DN_HEAD_DIM), sv["z"],
                                don.reshape(t * heads, DN_HEAD_DIM), onorm_g, tag + "_dpost")
    dq, dk, dv, dgc, dgr, dbc = _dn_chunk_bwd(sv["qn"], sv["kn"], sv["v"], sv["gc"], sv["gr"], sv["bc"], sv["s_all"],
                                              do.reshape(t, d), tag + "_dchunk")
    pad = lambda a: jnp.pad(a.reshape(heads, t).T, ((0, 0), (0, LANES - heads)))
    d_big_g = pad(dgc) + pad(dgr)
    dab, g_alog, g_dt = _dn_gate_bwd(sv["proj"], 4 * d // (2 * LANES), d_big_g, pad(dbc), a_log, dt_bias, tag + "_dgate")
    dxq, dxk, dxv, wq, wk, wv = _dn_prep_bwd(sv["proj"], conv_w, dq, dk, dv, heads, tag + "_dprep")
    dproj = jnp.concatenate([dxq, dxk, dxv, dz.reshape(t, d), dab], axis=1)
    dh1 = _mm(dproj, w["all"], tb=True, name=tag + "_dh")
    g_all = _mm(h1, dproj, ta=True, name=tag + "_gwin")
    grads = dict(w_all=g_all, w_out=g_out, conv_w=jnp.concatenate([wq, wk, wv], axis=1), a_log=g_alog, dt_bias=g_dt,
                 onorm_g=g_on)
    return dh1, grads


def _sb_layer_fwd(h1, w, q_g, k_g, tag):
    t, d = h1.shape
    heads = d // SB_HEAD_DIM
    qkv = _mm(h1, w["qkv"], name=tag + "_proj")
    rows = lambda i: qkv[:, i * d:(i + 1) * d].reshape(t * heads, SB_HEAD_DIM)
    to_heads = lambda a: a.reshape(t, heads, SB_HEAD_DIM).transpose(1, 0, 2)
    q_raw, k_raw = rows(0), rows(1)
    qn = to_heads(_rms_rows_fwd(q_raw, q_g, tag + "_qnorm"))
    kn = to_heads(_rms_rows_fwd(k_raw, k_g, tag + "_knorm"))
    vv = to_heads(rows(2))
    o = _sb_fwd(qn, kn, vv, tag + "_attn")
    ob = o.transpose(1, 0, 2).reshape(t, d).astype(BF16)
    y = _mm(ob, w["out"], name=tag + "_out")
    return y, dict(q_raw=q_raw, k_raw=k_raw, qn=qn, kn=kn, v=vv, o=o, ob=ob)


def _sb_layer_bwd(dy, h1, w, q_g, k_g, sv, tag):
    t, d = h1.shape
    heads = d // SB_HEAD_DIM
    do = _mm(dy, w["out"], tb=True, name=tag + "_dout")
    g_out = _mm(sv["ob"], dy, ta=True, name=tag + "_gwout")
    to_heads = lambda a: a.reshape(t, heads, SB_HEAD_DIM).transpose(1, 0, 2)
    to_rows = lambda a: a.transpose(1, 0, 2).reshape(t * heads, SB_HEAD_DIM)
    dqn, dkn, dv = _sb_bwd(sv["qn"], sv["kn"], sv["v"], to_heads(do), tag + "_dattn")
    dq, g_q = _rms_rows_bwd(sv["q_raw"], to_rows(dqn), q_g, tag + "_dqnorm")
    dk, g_k = _rms_rows_bwd(sv["k_raw"], to_rows(dkn), k_g, tag + "_dknorm")
    dqkv = jnp.concatenate([dq.reshape(t, d), dk.reshape(t, d), to_rows(dv).reshape(t, d)], axis=1).astype(BF16)
    dh1 = _mm(dqkv, w["qkv"], tb=True, name=tag + "_dh")
    g_qkv = _mm(h1, dqkv, ta=True, name=tag + "_gwin")
    return dh1, dict(w_qkv=g_qkv, w_out=g_out, q_norm_g=g_q, k_norm_g=g_k)


def _local_step(x, tgt, mod, norm1_g, norm2_g, dn, sb, ffn):
    depth = mod.shape[0]
    d = x.shape[1]
    saved = []
    for i in range(depth):
        mv = [mod[i:i + 1, j * d:(j + 1) * d] for j in range(N_MOD)]
        sh1, sc1, gt1, sh2, sc2, gt2 = mv
        tag = "l%d" % i
        h1 = _adaln_fwd(x, norm1_g[i:i + 1], sc1, sh1, tag + "_ln1")
        if i % 2 == 0:
            p = dn[i // 2]
            y, sv = _dn_layer_fwd(h1, p, p["conv_w"], p["a_log"], p["dt_bias"], p["onorm_g"], tag + "_dn")
        else:
            p = sb[i // 2]
            y, sv = _sb_layer_fwd(h1, p, p["q_g"], p["k_g"], tag + "_sb")
        x1 = _resid_fwd(x, y, gt1, tag + "_res1")
        h2 = _adaln_fwd(x1, norm2_g[i:i + 1], sc2, sh2, tag + "_ln2")
        u = _mm(h2, ffn[i]["w_in"], name=tag + "_ffn_in")
        a = _swiglu_fwd(u, tag + "_swiglu")
        y2 = _mm(a, ffn[i]["w_out"], name=tag + "_ffn_out")
        x2 = _resid_fwd(x1, y2, gt2, tag + "_res2")
        saved.append(dict(x0=x, h1=h1, y=y, mix=sv, x1=x1, h2=h2, u=u, a=a, y2=y2))
        x = x2

    loss, dx = _loss_fwd_bwd(x, tgt, "loss")

    dmod, dn1, dn2 = [None] * depth, [None] * depth, [None] * depth
    g_dn, g_sb, g_ffn = [None] * len(dn), [None] * len(sb), [None] * depth
    for i in reversed(range(depth)):
        s = saved[i]
        mv = [mod[i:i + 1, j * d:(j + 1) * d] for j in range(N_MOD)]
        sh1, sc1, gt1, sh2, sc2, gt2 = mv
        tag = "l%d" % i
        dy2, dgt2 = _resid_bwd(dx, s["y2"], gt2, tag + "_dres2")
        da = _mm(dy2, ffn[i]["w_out"], tb=True, name=tag + "_dffn_a")
        g_wout = _mm(s["a"], dy2, ta=True, name=tag + "_gffn_out")
        du = _swiglu_bwd(s["u"], da, tag + "_dswiglu")
        dh2 = _mm(du, ffn[i]["w_in"], tb=True, name=tag + "_dffn_h")
        g_win = _mm(s["h2"], du, ta=True, name=tag + "_gffn_in")
        g_ffn[i] = dict(w_in=g_win, w_out=g_wout)
        dx, dg2, dsc2, dsh2 = _adaln_bwd(s["x1"], dh2, dx, norm2_g[i:i + 1], sc2, tag + "_dln2")
        dy, dgt1 = _resid_bwd(dx, s["y"], gt1, tag + "_dres1")
        if i % 2 == 0:
            p = dn[i // 2]
            dh1, g_dn[i // 2] = _dn_layer_bwd(dy, s["h1"], p, p["conv_w"], p["a_log"], p["dt_bias"], p["onorm_g"],
                                               s["mix"], tag + "_dn")
        else:
            p = sb[i // 2]
            dh1, g_sb[i // 2] = _sb_layer_bwd(dy, s["h1"], p, p["q_g"], p["k_g"], s["mix"], tag + "_sb")
        dx, dg1, dsc1, dsh1 = _adaln_bwd(s["x0"], dh1, dx, norm1_g[i:i + 1], sc1, tag + "_dln1")
        dmod[i] = jnp.concatenate([dsh1, dsc1, dgt1, dsh2, dsc2, dgt2], axis=1)
        dn1[i], dn2[i] = dg1, dg2
    return (loss, dx, jnp.concatenate(dmod, axis=0), jnp.concatenate(dn1, axis=0), jnp.concatenate(dn2, axis=0),
            g_dn, g_sb, g_ffn)


def _axes():
    return lax.axis_index("x"), lax.axis_index("y"), lax.axis_index("c")


def _remote(src, dst, send_sem, recv_sem, dev):
    return pltpu.make_async_remote_copy(src_ref=src, dst_ref=dst, send_sem=send_sem, recv_sem=recv_sem,
                                        device_id=dev, device_id_type=MESH)


def _other_chips(x, y):
    return [(1 - x, y), (x, 1 - y), (1 - x, 1 - y)]


def _allgather_small(v, name):
    m, n = v.shape

    def body(x_ref, out_ref, send_sems, recv_sems, local_sem):
        x, y, c = _axes()
        me, sibling = (x, y, c), (x, y, 1 - c)
        chips = _other_chips(x, y)

        def rows(px, py, pc):
            return out_ref.at[pl.ds((4 * px + 2 * py + pc) * m, m), :]

        def copy(k, block, to, src=None):
            return _remote(rows(*block) if src is None else src, rows(*block), send_sems.at[k], recv_sems.at[k], to)

        mine = pltpu.make_async_copy(x_ref, rows(*me), local_sem)
        mine.start()
        first = [copy(0, me, sibling, src=x_ref)]
        first += [copy(1 + j, me, (*chip, c), src=x_ref) for j, chip in enumerate(chips)]
        for cp in first:
            cp.start()
        passed = [copy(4 + j, (*chip, c), sibling) for j, chip in enumerate(chips)]
        for j, chip in enumerate(chips):
            copy(1 + j, (*chip, c), me).wait_recv()
            passed[j].start()
        copy(0, sibling, me).wait_recv()
        for j, chip in enumerate(chips):
            copy(4 + j, (*chip, 1 - c), me).wait_recv()
        for cp in first + passed:
            cp.wait_send()
        mine.wait()

    return pl.pallas_call(
        body, out_shape=jax.ShapeDtypeStruct((8 * m, n), v.dtype),
        in_specs=[pl.BlockSpec(memory_space=pltpu.VMEM)], out_specs=pl.BlockSpec(memory_space=pltpu.VMEM),
        scratch_shapes=[pltpu.SemaphoreType.DMA((7,)), pltpu.SemaphoreType.DMA((7,)), pltpu.SemaphoreType.DMA],
        name=name, compiler_params=pltpu.CompilerParams(vmem_limit_bytes=VMEM_LIMIT))(v)


def _allgather_weights(packed, name):
    _, rh, w = packed.shape

    def body(p_ref, out_ref, send_sems, recv_sems, local_sem):
        x, y, c = _axes()
        sibling = (x, y, 1 - c)
        chips = _other_chips(x, y)

        def blk(cx, cy, h):
            return out_ref.at[2 * cx + cy, h]

        mine = pltpu.make_async_copy(p_ref, out_ref.at[2 * x + y], local_sem)
        mine.start()
        first = [_remote(p_ref.at[c], blk(x, y, c), send_sems.at[j], recv_sems.at[j], (*chip, c))
                 for j, chip in enumerate(chips)]
        for cp in first:
            cp.start()
        passed = [_remote(blk(*chip, c), blk(*chip, c), send_sems.at[3 + j], recv_sems.at[3 + j], sibling)
                  for j, chip in enumerate(chips)]
        for j, chip in enumerate(chips):
            _remote(p_ref.at[c], blk(*chip, c), send_sems.at[j], recv_sems.at[j], sibling).wait_recv()
            passed[j].start()
        for j, chip in enumerate(chips):
            _remote(p_ref.at[c], blk(*chip, 1 - c), send_sems.at[3 + j], recv_sems.at[3 + j], sibling).wait_recv()
        for cp in first + passed:
            cp.wait_send()
        mine.wait()

    hbm = pl.BlockSpec(memory_space=pltpu.HBM)
    return pl.pallas_call(
        body, out_shape=jax.ShapeDtypeStruct((4, 2, rh, w), packed.dtype), in_specs=[hbm], out_specs=hbm,
        scratch_shapes=[pltpu.SemaphoreType.DMA((6,)), pltpu.SemaphoreType.DMA((6,)), pltpu.SemaphoreType.DMA],
        name=name)(packed)


def _sibling_swap(v, pick_other_half, name):
    shape = v.shape[1:] if pick_other_half else v.shape

    def body(v_ref, out_ref, send_sem, recv_sem):
        x, y, c = _axes()
        cp = _remote(v_ref.at[1 - c] if pick_other_half else v_ref, out_ref, send_sem, recv_sem, (x, y, 1 - c))
        cp.start()
        cp.wait()

    hbm = pl.BlockSpec(memory_space=pltpu.HBM)
    return pl.pallas_call(
        body, out_shape=jax.ShapeDtypeStruct(shape, v.dtype), in_specs=[hbm], out_specs=hbm,
        scratch_shapes=[pltpu.SemaphoreType.DMA, pltpu.SemaphoreType.DMA], name=name)(v)


def _chip_scatter(v, name):
    def body(v_ref, out_ref, send_sems, recv_sems, local_sem):
        x, y, c = _axes()
        me = 2 * x + y
        chips = _other_chips(x, y)
        mine = pltpu.make_async_copy(v_ref.at[me], out_ref.at[me], local_sem)
        mine.start()
        sends = [_remote(v_ref.at[2 * cx + cy], out_ref.at[me], send_sems.at[j], recv_sems.at[j], (cx, cy, c))
                 for j, (cx, cy) in enumerate(chips)]
        for cp in sends:
            cp.start()
        for j, (cx, cy) in enumerate(chips):
            _remote(v_ref.at[me], out_ref.at[2 * cx + cy], send_sems.at[j], recv_sems.at[j], (cx, cy, c)).wait_recv()
        for cp in sends:
            cp.wait_send()
        mine.wait()

    hbm = pl.BlockSpec(memory_space=pltpu.HBM)
    return pl.pallas_call(
        body, out_shape=jax.ShapeDtypeStruct(v.shape, v.dtype), in_specs=[hbm], out_specs=hbm,
        scratch_shapes=[pltpu.SemaphoreType.DMA((3,)), pltpu.SemaphoreType.DMA((3,)), pltpu.SemaphoreType.DMA],
        name=name)(v)


def _sum_lead(v, name, tr=256):
    k, r, w = v.shape

    def body(v_ref, o_ref):
        acc = v_ref[0].astype(F32)
        for i in range(1, k):
            acc = acc + v_ref[i].astype(F32)
        o_ref[...] = acc

    return pl.pallas_call(
        body, grid=(r // tr,), in_specs=[pl.BlockSpec((k, tr, w), lambda i: (0, i, 0))],
        out_specs=pl.BlockSpec((tr, w), lambda i: (i, 0)), out_shape=jax.ShapeDtypeStruct((r, w), F32), name=name,
        compiler_params=_cparams(("parallel",)))(v)


_BIG = (("dn_w_in", 2), ("dn_w_out", 1), ("sb_w_qkv", 2), ("sb_w_out", 1), ("ffn_w_in", 2), ("ffn_w_out", 1))


def _pack_rows(shard_shapes):
    counts = [l * (r // 2) * cs // PACK_W for l, r, cs in shard_shapes]
    total = sum(counts)
    return counts, -(-total // PACK_ALIGN) * PACK_ALIGN


def _pack(parts, rh):
    lead = parts[0].shape[:-4]
    nl = len(lead)
    flat = []
    for p in parts:
        p = jnp.moveaxis(p, nl + 1, 0)
        flat.append(p.reshape((2,) + lead + (-1, PACK_W)))
    used = sum(f.shape[-2] for f in flat)
    flat.append(jnp.zeros((2,) + lead + (rh - used, PACK_W), flat[0].dtype))
    return jnp.concatenate(flat, axis=nl + 1)


def _unpack(buf, shard_shapes, counts):
    lead = buf.shape[:-3]
    out, off = [], 0
    for (l, r, cs), n in zip(shard_shapes, counts):
        p = buf[..., off:off + n, :].reshape(lead + (2, l, r // 2, cs))
        out.append(jnp.moveaxis(p, len(lead), len(lead) + 1).reshape(lead + (l, r, cs)))
        off += n
    return out


def kernel(x, c, ada_w, ada_b, norm1_g, norm2_g, dn_w_in, dn_conv_w, dn_a_log, dn_dt_bias, dn_onorm_g, dn_w_out, sb_w_qkv, sb_q_norm_g, sb_k_norm_g, sb_w_out, ffn_w_in, ffn_w_out, loss_target, m_ada_w, m_ada_b, m_norm1_g, m_norm2_g, m_dn_w_in, m_dn_conv_w, m_dn_a_log, m_dn_dt_bias, m_dn_onorm_g, m_dn_w_out, m_sb_w_qkv, m_sb_q_norm_g, m_sb_k_norm_g, m_sb_w_out, m_ffn_w_in, m_ffn_w_out, v_ada_w, v_ada_b, v_norm1_g, v_norm2_g, v_dn_w_in, v_dn_conv_w, v_dn_a_log, v_dn_dt_bias, v_dn_onorm_g, v_dn_w_out, v_sb_w_qkv, v_sb_q_norm_g, v_sb_k_norm_g, v_sb_w_out, v_ffn_w_in, v_ffn_w_out):
    names = ("ada_w", "ada_b", "norm1_g", "norm2_g", "dn_w_in", "dn_conv_w", "dn_a_log", "dn_dt_bias", "dn_onorm_g",
             "dn_w_out", "sb_w_qkv", "sb_q_norm_g", "sb_k_norm_g", "sb_w_out", "ffn_w_in", "ffn_w_out")
    w = dict(zip(names, (ada_w, ada_b, norm1_g, norm2_g, dn_w_in, dn_conv_w, dn_a_log, dn_dt_bias, dn_onorm_g,
                         dn_w_out, sb_w_qkv, sb_q_norm_g, sb_k_norm_g, sb_w_out, ffn_w_in, ffn_w_out)))
    mom = dict(zip(names, (m_ada_w, m_ada_b, m_norm1_g, m_norm2_g, m_dn_w_in, m_dn_conv_w, m_dn_a_log, m_dn_dt_bias,
                           m_dn_onorm_g, m_dn_w_out, m_sb_w_qkv, m_sb_q_norm_g, m_sb_k_norm_g, m_sb_w_out, m_ffn_w_in,
                           m_ffn_w_out)))
    var = dict(zip(names, (v_ada_w, v_ada_b, v_norm1_g, v_norm2_g, v_dn_w_in, v_dn_conv_w, v_dn_a_log, v_dn_dt_bias,
                           v_dn_onorm_g, v_dn_w_out, v_sb_w_qkv, v_sb_q_norm_g, v_sb_k_norm_g, v_sb_w_out, v_ffn_w_in,
                           v_ffn_w_out)))
    ax, ay, ac = _axes()
    chip = 2 * ax + ay
    dev = 2 * chip + ac
    t, d = x.shape[1], x.shape[2]
    depth, ndn, nsb = ada_w.shape[0], dn_w_in.shape[0], sb_w_qkv.shape[0]
    heads = d // DN_HEAD_DIM
    mod_cols = ada_w.shape[2]
    conv_cols = dn_conv_w.shape[2]
    nchips, ndev = 4, 8

    conv_rows = ndn * DN_CONV * conv_cols // d
    pay1 = jnp.concatenate([c, dn_conv_w.reshape(conv_rows, d), jnp.zeros((8 - 1 - conv_rows, d), F32)], axis=0)
    g1 = _allgather_small(pay1, "ag_cond").reshape(ndev, 8, d)
    c_all = g1[:, 0]
    conv_full = g1[::2, 1:1 + conv_rows].reshape(nchips, ndn, DN_CONV, conv_cols).transpose(1, 2, 0, 3)
    conv_full = conv_full.reshape(ndn, DN_CONV, nchips * conv_cols)

    c16 = jnp.pad(c_all, ((0, 16 - ndev), (0, 0)))
    cond16 = _rowwise(lambda r, v: ([_silu(r[0])], []), [c16], [], [(d, F32)], [], name="cond_silu")[0]
    pay2 = jnp.concatenate([_mm(cond16, ada_w[i], name="ada_mod%d" % i)[:ndev] for i in range(depth)], axis=0)
    g2 = _allgather_small(pay2, "ag_mod").reshape(ndev, depth, ndev, mod_cols)[::2]
    mod_raw = lax.dynamic_index_in_dim(g2, dev, axis=2, keepdims=False)
    mod_raw = mod_raw.transpose(1, 0, 2).reshape(depth, nchips * mod_cols)
    mod = _rowwise(lambda r, v: ([r[0] + r[1]], []), [mod_raw, ada_b], [], [(nchips * mod_cols, F32)], [],
                   name="ada_bias")[0]

    shard_shapes = [w[n].shape for n, _ in _BIG]
    counts, rh = _pack_rows(shard_shapes)
    halves = lambda a: a.reshape(a.shape[:-2] + (2, a.shape[-2] // 2, a.shape[-1]))
    packed = _pack([halves(w[n].astype(BF16)) for n, _ in _BIG], rh)
    gathered = _allgather_weights(packed, "ag_weights")
    full = {}
    for (n, axis), a in zip(_BIG, _unpack(gathered, shard_shapes, counts)):
        l, r, cs = a.shape[1:]
        if axis == 2:
            full[n] = a.transpose(1, 2, 0, 3).reshape(l, r, nchips * cs)
        else:
            full[n] = a.transpose(1, 0, 2, 3).reshape(l, nchips * r, cs)

    padl = lambda v: jnp.pad(v[None, :], ((0, 0), (0, LANES - v.shape[0])))
    padc = lambda a: jnp.pad(a, ((0, 0), (0, LANES - a.shape[1])))
    dn = []
    for j in range(ndn):
        wi = full["dn_w_in"][j]
        w_all = jnp.concatenate([wi[:, :4 * d], padc(wi[:, 4 * d:4 * d + heads]), padc(wi[:, 4 * d + heads:])], axis=1)
        dn.append(dict(all=w_all, out=full["dn_w_out"][j], conv_w=conv_full[j], a_log=padl(dn_a_log[j]),
                       dt_bias=padl(dn_dt_bias[j]), onorm_g=dn_onorm_g[j][None]))
    sb = [dict(qkv=full["sb_w_qkv"][j], out=full["sb_w_out"][j], q_g=sb_q_norm_g[j][None], k_g=sb_k_norm_g[j][None])
          for j in range(nsb)]
    ffn = [dict(w_in=full["ffn_w_in"][i], w_out=full["ffn_w_out"][i]) for i in range(depth)]

    loss_local, grad_x, dmod, g_n1, g_n2, g_dn, g_sb, g_ffn = _local_step(x[0], loss_target[0], mod, norm1_g, norm2_g,
                                                                           dn, sb, ffn)
    loss = lax.psum(loss_local, ("x", "y", "c"))

    g_conv = jnp.stack([g["conv_w"] for g in g_dn])
    misc = jnp.concatenate([jnp.concatenate([g["onorm_g"] for g in g_dn], axis=1),
                            jnp.concatenate([g["a_log"] for g in g_dn], axis=1),
                            jnp.concatenate([g["dt_bias"] for g in g_dn], axis=1),
                            jnp.concatenate([g["q_norm_g"] for g in g_sb], axis=1),
                            jnp.concatenate([g["k_norm_g"] for g in g_sb], axis=1)], axis=1)
    misc = jnp.pad(misc, ((0, 0), (0, -misc.shape[1] % d))).reshape(-1, d)
    small = [dmod.reshape(-1, d), g_n1, g_n2, g_conv.reshape(-1, d), misc]
    small_rows = [s.shape[0] for s in small]
    pad_rows = -sum(small_rows) % 8
    pay3 = jnp.concatenate(small + [jnp.zeros((pad_rows, d), F32)], axis=0)
    nrow3 = pay3.shape[0]
    g3 = _allgather_small(pay3, "ag_small")
    summed = _rowwise(lambda r, v: ([], [_colsum(r[0])]), [g3.reshape(ndev, nrow3 * d)], [], [], [(1, nrow3 * d)],
                      name="small_sum")[0].reshape(nrow3, d)
    offs = [0]
    for n_ in small_rows:
        offs.append(offs[-1] + n_)
    grads = {}
    grads["ada_b"] = summed[offs[0]:offs[1]].reshape(depth, N_MOD * d)
    grads["norm1_g"] = summed[offs[1]:offs[2]]
    grads["norm2_g"] = summed[offs[2]:offs[3]]
    conv_sum = summed[offs[3]:offs[4]].reshape(ndn, DN_CONV, nchips * conv_cols)
    grads["dn_conv_w"] = lax.dynamic_slice_in_dim(conv_sum, chip * conv_cols, conv_cols, axis=2)
    mrow = summed[offs[4]:offs[5]].reshape(-1)
    o = 0
    grads["dn_onorm_g"] = mrow[o:o + ndn * DN_HEAD_DIM].reshape(ndn, DN_HEAD_DIM)
    o += ndn * DN_HEAD_DIM
    grads["dn_a_log"] = mrow[o:o + ndn * LANES].reshape(ndn, LANES)[:, :heads]
    o += ndn * LANES
    grads["dn_dt_bias"] = mrow[o:o + ndn * LANES].reshape(ndn, LANES)[:, :heads]
    o += ndn * LANES
    grads["sb_q_norm_g"] = mrow[o:o + nsb * SB_HEAD_DIM].reshape(nsb, SB_HEAD_DIM)
    o += nsb * SB_HEAD_DIM
    grads["sb_k_norm_g"] = mrow[o:o + nsb * SB_HEAD_DIM].reshape(nsb, SB_HEAD_DIM)
    dmod_all = g3.reshape(ndev, nrow3, d)[:, :small_rows[0]].reshape(ndev, depth, N_MOD * d)
    dmod_mine = lax.dynamic_slice_in_dim(dmod_all, chip * mod_cols, mod_cols, axis=2)
    dmod16 = jnp.pad(dmod_mine, ((0, 16 - ndev), (0, 0), (0, 0)))
    grads["ada_w"] = jnp.stack([_mm(cond16, dmod16[:, i], ta=True, name="ada_gw%d" % i) for i in range(depth)])

    def gw_in(g):
        ga = g["w_all"]
        return jnp.concatenate([ga[:, :4 * d], ga[:, 4 * d:4 * d + heads], ga[:, 4 * d + LANES:4 * d + LANES + heads]],
                               axis=1)
    gfull = dict(dn_w_in=jnp.stack([gw_in(g) for g in g_dn]), dn_w_out=jnp.stack([g["w_out"] for g in g_dn]),
                 sb_w_qkv=jnp.stack([g["w_qkv"] for g in g_sb]), sb_w_out=jnp.stack([g["w_out"] for g in g_sb]),
                 ffn_w_in=jnp.stack([g["w_in"] for g in g_ffn]), ffn_w_out=jnp.stack([g["w_out"] for g in g_ffn]))
    parts = []
    for (n, axis), (l, r, cs) in zip(_BIG, shard_shapes):
        a = gfull[n].astype(BF16)
        if axis == 2:
            a = a.reshape(l, r, nchips, cs).transpose(2, 0, 1, 3)
        else:
            a = a.reshape(l, nchips, r, cs).transpose(1, 0, 2, 3)
        parts.append(halves(a))
    gp = _pack(parts, rh)
    from_sibling = _sibling_swap(gp, True, "gr_pair")
    own_half = lax.dynamic_index_in_dim(gp, ac, axis=0, keepdims=False)
    pair = _rowwise(lambda r, v: ([r[0].astype(F32) + r[1].astype(F32)], []),
                    [own_half.reshape(nchips * rh, PACK_W), from_sibling.reshape(nchips * rh, PACK_W)], [],
                    [(PACK_W, BF16)], [], name="gr_pair_add")[0]
    from_chips = _chip_scatter(pair.reshape(nchips, rh, PACK_W), "gr_chips")
    reduced = _sum_lead(from_chips, "gr_chip_add")
    other = _sibling_swap(reduced, False, "gr_share")
    both = jnp.where(ac == 0, jnp.stack([reduced, other]), jnp.stack([other, reduced]))
    for (n, _), g in zip(_BIG, _unpack(both, shard_shapes, counts)):
        grads[n] = g

    delta, new_m, new_v = {}, {}, {}
    for n in names:
        delta[n], new_m[n], new_v[n] = _adamw(w[n], grads[n], mom[n], var[n], "adamw_" + n)
    return (loss, grad_x[None], *[grads[n] for n in names], *[delta[n] for n in names], *[new_m[n] for n in names],
            *[new_v[n] for n in names])
```

```python
import functools

import jax
import jax.numpy as jnp
from jax import lax
from jax.experimental import pallas as pl
from jax.experimental.pallas import tpu as pltpu

F32 = jnp.float32
BF16 = jnp.bfloat16
HI = lax.Precision.HIGHEST
MESH = pl.DeviceIdType.MESH

EPS = 1e-6
N_MOD = 6
DN_HEAD_DIM = 128
DN_CONV = 4
DN_CHUNK = 64
DN_GROUP = 8
SB_HEAD_DIM = 64
SB_BLOCK = 128
SB_QBLOCK = 256
LANES = 128
PACK_W = 1024
PACK_ALIGN = 256
VMEM_LIMIT = 56 * 1024 * 1024

ADAM_LR = 0.001
ADAM_B1 = 0.9
ADAM_B2 = 0.999
ADAM_EPS = 1e-08
ADAM_WD = 0.01
ADAM_STEP = 10


def _cparams(sem=None):
    return pltpu.CompilerParams(dimension_semantics=sem, vmem_limit_bytes=VMEM_LIMIT)


def _sigmoid(x):
    return 1.0 / (1.0 + jnp.exp(-x))


def _silu(x):
    return x * _sigmoid(x)


def _dsilu(x):
    s = _sigmoid(x)
    return s * (1.0 + x * (1.0 - s))


def _softplus(x):
    return jnp.maximum(x, 0.0) + jnp.log1p(jnp.exp(-jnp.abs(x)))


def _pick_tile(n, prefs):
    for t in prefs:
        if n % t == 0:
            return t
    return n


def _mm(a, b, *, ta=False, tb=False, out_dtype=F32, name):
    m = a.shape[1] if ta else a.shape[0]
    k = a.shape[0] if ta else a.shape[1]
    n = b.shape[0] if tb else b.shape[1]
    assert (b.shape[1] if tb else b.shape[0]) == k
    tm = _pick_tile(m, (512, 256, 128))
    tn = _pick_tile(n, (512, 256, 128))
    dims = (((0 if ta else 1,), (1 if tb else 0,)), ((), ()))

    def body(a_ref, b_ref, o_ref):
        av = a_ref[...].astype(BF16)
        bv = b_ref[...].astype(BF16)
        o_ref[...] = lax.dot_general(av, bv, dims, preferred_element_type=F32).astype(o_ref.dtype)

    a_spec = pl.BlockSpec((k, tm), lambda i, j: (0, i)) if ta else pl.BlockSpec((tm, k), lambda i, j: (i, 0))
    b_spec = pl.BlockSpec((tn, k), lambda i, j: (j, 0)) if tb else pl.BlockSpec((k, tn), lambda i, j: (0, j))
    return pl.pallas_call(
        body, grid=(m // tm, n // tn), in_specs=[a_spec, b_spec],
        out_specs=pl.BlockSpec((tm, tn), lambda i, j: (i, j)),
        out_shape=jax.ShapeDtypeStruct((m, n), out_dtype), name=name,
        compiler_params=_cparams(("parallel", "parallel")))(a, b)


def _rowwise(body, rows, vecs, outs, accs, *, name, tr=256):
    rows = [r if isinstance(r, tuple) else (r, r.shape[1], 0) for r in rows]
    nrows = rows[0][0].shape[0]
    tr = tr if nrows % tr == 0 else nrows
    nr, nv, no = len(rows), len(vecs), len(outs)

    def kern(*refs):
        r_in, v_in = refs[:nr], refs[nr:nr + nv]
        o_refs, a_refs = refs[nr + nv:nr + nv + no], refs[nr + nv + no:]
        res_o, res_a = body([r[...] for r in r_in], [v[...] for v in v_in])
        for o, val in zip(o_refs, res_o):
            o[...] = val.astype(o.dtype)
        if a_refs:
            @pl.when(pl.program_id(0) == 0)
            def _():
                for a in a_refs:
                    a[...] = jnp.zeros(a.shape, a.dtype)
            for a, val in zip(a_refs, res_a):
                a[...] += val

    in_specs = [pl.BlockSpec((tr, w), functools.partial(lambda i, cb: (i, cb), cb=cb)) for _, w, cb in rows]
    in_specs += [pl.BlockSpec(v.shape, functools.partial(lambda i, nd: (0,) * nd, nd=v.ndim)) for v in vecs]
    out_specs = [pl.BlockSpec((tr, w), lambda i: (i, 0)) for w, _ in outs]
    out_specs += [pl.BlockSpec(s, lambda i: (0, 0)) for s in accs]
    out_shape = [jax.ShapeDtypeStruct((nrows, w), dt) for w, dt in outs]
    out_shape += [jax.ShapeDtypeStruct(s, F32) for s in accs]
    res = pl.pallas_call(
        kern, grid=(nrows // tr,), in_specs=in_specs, out_specs=out_specs, out_shape=out_shape, name=name,
        compiler_params=_cparams(("arbitrary",) if accs else ("parallel",)))(*[r[0] for r in rows], *vecs)
    return res


def _colsum(v):
    return jnp.sum(v, axis=0, keepdims=True)


def _adaln_fwd(x, g, sc, sh, name):
    def body(r, v):
        (xv,), (gv, scv, shv) = r, v
        rs = lax.rsqrt(jnp.mean(xv * xv, axis=-1, keepdims=True) + EPS)
        return [(xv * rs) * gv * (1.0 + scv) + shv], []
    return _rowwise(body, [x], [g, sc, sh], [(x.shape[1], BF16)], [], name=name)[0]


def _adaln_bwd(x, dh, dxr, g, sc, name):
    d = x.shape[1]

    def body(r, v):
        (xv, dhv, dxv), (gv, scv) = r, v
        rs = lax.rsqrt(jnp.mean(xv * xv, axis=-1, keepdims=True) + EPS)
        nv = xv * rs
        dn = dhv * (gv * (1.0 + scv))
        dx = rs * (dn - nv * jnp.mean(dn * nv, axis=-1, keepdims=True)) + dxv
        dhn = dhv * nv
        return [dx], [_colsum(dhn * (1.0 + scv)), _colsum(dhn * gv), _colsum(dhv)]
    return _rowwise(body, [x, dh, dxr], [g, sc], [(d, F32)], [(1, d)] * 3, name=name)


def _resid_fwd(x, y, gt, name):
    def body(r, v):
        return [r[0] + v[0] * r[1]], []
    return _rowwise(body, [x, y], [gt], [(x.shape[1], F32)], [], name=name)[0]


def _resid_bwd(dx, y, gt, name):
    d = dx.shape[1]

    def body(r, v):
        return [v[0] * r[0]], [_colsum(r[0] * r[1])]
    return _rowwise(body, [dx, y], [gt], [(d, BF16)], [(1, d)], name=name)


def _swiglu_fwd(u, name):
    f = u.shape[1] // 2

    def body(r, v):
        uv = r[0]
        return [_silu(uv[:, :f]) * uv[:, f:]], []
    return _rowwise(body, [u], [], [(f, BF16)], [], name=name)[0]


def _swiglu_bwd(u, da, name):
    f = u.shape[1] // 2

    def body(r, v):
        uv, dav = r
        gate, up = uv[:, :f], uv[:, f:]
        return [jnp.concatenate([dav * up * _dsilu(gate), dav * _silu(gate)], axis=1)], []
    return _rowwise(body, [u, da], [], [(2 * f, BF16)], [], name=name)[0]


def _loss_fwd_bwd(y, tgt, name):
    d = y.shape[1]

    def body(r, v):
        err = r[0] - r[1]
        part = jnp.sum(jnp.sum(err * err, axis=1, keepdims=True), axis=0, keepdims=True) * (0.5 / d)
        return [err * (1.0 / d)], [jnp.broadcast_to(part, (1, LANES))]
    dy, acc = _rowwise(body, [y, tgt], [], [(d, F32)], [(1, LANES)], name=name)
    return acc[0, 0], dy


def _head_means(v):
    row = lax.broadcasted_iota(jnp.int32, (LANES, LANES), 0)
    col = lax.broadcasted_iota(jnp.int32, (LANES, LANES), 1)
    same = ((row // SB_HEAD_DIM) == (col // SB_HEAD_DIM)).astype(F32)
    parts = [jnp.dot(v[:, g * LANES:(g + 1) * LANES], same, precision=HI, preferred_element_type=F32)
             for g in range(v.shape[1] // LANES)]
    return jnp.concatenate(parts, axis=1) * (1.0 / SB_HEAD_DIM)


def _sb_norm_fwd(qkv, gq, gk, name):
    d = qkv.shape[1] // 3

    def body(r, v):
        return [x * lax.rsqrt(_head_means(x * x) + EPS) * g for x, g in zip(r, v)], []
    return _rowwise(body, [(qkv, d, 0), (qkv, d, 1)], [gq, gk], [(d, F32), (d, F32)], [], name=name)


def _sb_norm_bwd(qkv, dqn, dkn, dv, gq, gk, name):
    d = qkv.shape[1] // 3

    def body(r, v):
        outs, accs = [], []
        for x, dy, g in ((r[0], r[2], v[0]), (r[1], r[3], v[1])):
            rs = lax.rsqrt(_head_means(x * x) + EPS)
            nv = x * rs
            dn = dy * g
            outs.append(rs * (dn - nv * _head_means(dn * nv)))
            accs.append(_colsum(dy * nv))
        return [jnp.concatenate(outs + [r[4]], axis=1)], accs
    return _rowwise(body, [(qkv, d, 0), (qkv, d, 1), dqn, dkn, dv], [gq, gk], [(3 * d, BF16)], [(1, d), (1, d)],
                    name=name)


def _dn_post_fwd(o, z, g, name):
    def body(r, v):
        ov, zv = r
        rs = lax.rsqrt(jnp.mean(ov * ov, axis=-1, keepdims=True) + EPS)
        return [ov * rs * v[0] * _silu(zv)], []
    return _rowwise(body, [o, z], [g], [(o.shape[1], BF16)], [], name=name, tr=2048)[0]


def _dn_post_bwd(o, z, don, g, name):
    d = o.shape[1]

    def body(r, v):
        ov, zv, dv = r
        rs = lax.rsqrt(jnp.mean(ov * ov, axis=-1, keepdims=True) + EPS)
        nv = ov * rs
        s = _silu(zv)
        dn = dv * v[0] * s
        do = rs * (dn - nv * jnp.mean(dn * nv, axis=-1, keepdims=True))
        dz = dv * nv * v[0] * _dsilu(zv)
        return [do, dz], [_colsum(dv * nv * s)]
    return _rowwise(body, [o, z, don], [g], [(d, F32), (d, BF16)], [(1, d)], name=name, tr=2048)


def _chunk_tri(tr, upper):
    row = lax.broadcasted_iota(jnp.int32, (tr, tr), 0)
    col = lax.broadcasted_iota(jnp.int32, (tr, tr), 1)
    same = (row // DN_CHUNK) == (col // DN_CHUNK)
    return (same & ((row <= col) if upper else (row >= col))).astype(F32)


def _dn_gate_fwd(proj, colblk, a_log, dt_bias, name):
    def body(r, v):
        ab = r[0]
        a, b = ab[:, :LANES], ab[:, LANES:]
        g = -jnp.exp(v[0]) * _softplus(a + v[1])
        big_g = jnp.dot(_chunk_tri(g.shape[0], False), g, precision=HI, preferred_element_type=F32)
        return [big_g, _sigmoid(b)], []
    return _rowwise(body, [(proj, 2 * LANES, colblk)], [a_log, dt_bias], [(LANES, F32), (LANES, F32)], [], name=name)


def _dn_gate_bwd(proj, colblk, d_big_g, dbeta, a_log, dt_bias, name):
    def body(r, v):
        ab, dgc, dbt = r
        a, b = ab[:, :LANES], ab[:, LANES:]
        dg = jnp.dot(_chunk_tri(dgc.shape[0], True), dgc, precision=HI, preferred_element_type=F32)
        na = -jnp.exp(v[0])
        pre = a + v[1]
        da = dg * na * _sigmoid(pre)
        beta = _sigmoid(b)
        db = dbt * beta * (1.0 - beta)
        return [jnp.concatenate([da, db], axis=1)], [_colsum(dg * na * _softplus(pre)), _colsum(da)]
    return _rowwise(body, [(proj, 2 * LANES, colblk), d_big_g, dbeta], [a_log, dt_bias],
                    [(2 * LANES, BF16)], [(1, LANES), (1, LANES)], name=name)


def _shift_rows(x, s):
    if s == 0:
        return x
    t = x.shape[0]
    row = lax.broadcasted_iota(jnp.int32, x.shape, 0)
    rolled = pltpu.roll(x, s % t, axis=0)
    return jnp.where((row >= s) if s > 0 else (row < t + s), rolled, 0.0)


def _rnd(x):
    return x.astype(BF16).astype(F32)


def _conv(x, c_ref):
    c = x * _rnd(c_ref[DN_CONV - 1:DN_CONV, :])
    for s in range(1, DN_CONV):
        c = c + _shift_rows(x, s) * _rnd(c_ref[DN_CONV - 1 - s:DN_CONV - s, :])
    return c


def _dn_prep_fwd(proj, conv_w, heads, name):
    t = proj.shape[0]
    d = heads * DN_HEAD_DIM

    def body(xq, xk, xv, cq, ck, cv, q_ref, k_ref, v_ref):
        for x_ref, c_ref, o_ref, norm in ((xq, cq, q_ref, True), (xk, ck, k_ref, True), (xv, cv, v_ref, False)):
            y = _silu(_conv(_rnd(x_ref[...]), c_ref))
            if norm:
                y = y * lax.rsqrt(jnp.sum(y * y, axis=-1, keepdims=True) + EPS)
            o_ref[...] = y

    xs = [pl.BlockSpec((t, DN_HEAD_DIM), functools.partial(lambda h, o: (0, h + o), o=o * heads)) for o in range(3)]
    cs = [pl.BlockSpec((DN_CONV, DN_HEAD_DIM), functools.partial(lambda h, o: (0, h + o), o=o * heads)) for o in range(3)]
    return pl.pallas_call(
        body, grid=(heads,), in_specs=xs + cs,
        out_specs=[pl.BlockSpec((t, DN_HEAD_DIM), lambda h: (0, h))] * 3,
        out_shape=[jax.ShapeDtypeStruct((t, d), F32)] * 3, name=name,
        compiler_params=_cparams(("parallel",)))(proj, proj, proj, conv_w, conv_w, conv_w)


def _dn_prep_bwd(proj, conv_w, dq, dk, dv, heads, name):
    t = proj.shape[0]
    d = heads * DN_HEAD_DIM

    def body(xq, xk, xv, cq, ck, cv, gq, gk, gv, oq, ok, ov, wq, wk, wv):
        for x_ref, c_ref, g_ref, o_ref, w_ref, norm in ((xq, cq, gq, oq, wq, True), (xk, ck, gk, ok, wk, True),
                                                        (xv, cv, gv, ov, wv, False)):
            x, dy = _rnd(x_ref[...]), g_ref[...]
            c = _conv(x, c_ref)
            if norm:
                y = _silu(c)
                rs = lax.rsqrt(jnp.sum(y * y, axis=-1, keepdims=True) + EPS)
                yn = y * rs
                dy = rs * (dy - yn * jnp.sum(dy * yn, axis=-1, keepdims=True))
            dc = _rnd(dy * _dsilu(c))
            dx = dc * _rnd(c_ref[DN_CONV - 1:DN_CONV, :])
            w_ref[DN_CONV - 1:DN_CONV, :] = _colsum(dc * x)
            for s in range(1, DN_CONV):
                dx = dx + _shift_rows(dc, -s) * _rnd(c_ref[DN_CONV - 1 - s:DN_CONV - s, :])
                w_ref[DN_CONV - 1 - s:DN_CONV - s, :] = _colsum(dc * _shift_rows(x, s))
            o_ref[...] = dx.astype(o_ref.dtype)

    xs = [pl.BlockSpec((t, DN_HEAD_DIM), functools.partial(lambda h, o: (0, h + o), o=o * heads)) for o in range(3)]
    cs = [pl.BlockSpec((DN_CONV, DN_HEAD_DIM), functools.partial(lambda h, o: (0, h + o), o=o * heads)) for o in range(3)]
    hs = pl.BlockSpec((t, DN_HEAD_DIM), lambda h: (0, h))
    ws = pl.BlockSpec((DN_CONV, DN_HEAD_DIM), lambda h: (0, h))
    return pl.pallas_call(
        body, grid=(heads,), in_specs=xs + cs + [hs] * 3, out_specs=[hs] * 3 + [ws] * 3,
        out_shape=[jax.ShapeDtypeStruct((t, d), BF16)] * 3 + [jax.ShapeDtypeStruct((DN_CONV, d), F32)] * 3, name=name,
        compiler_params=_cparams(("parallel",)))(proj, proj, proj, conv_w, conv_w, conv_w, dq, dk, dv)


def _bmm(a, b, prec=None):
    return lax.dot_general(a, b, (((2,), (1,)), ((0,), (0,))), precision=prec, preferred_element_type=F32)


def _bmm_nt(a, b, prec=None):
    return lax.dot_general(a, b, (((2,), (2,)), ((0,), (0,))), precision=prec, preferred_element_type=F32)


def _bmm_tn(a, b, prec=None):
    return lax.dot_general(a, b, (((1,), (1,)), ((0,), (0,))), precision=prec, preferred_element_type=F32)


def _dn_local(qg, kg, vg, gc, gr, bt):
    c = qg.shape[1]
    row = lax.broadcasted_iota(jnp.int32, (c, c), 0)
    col = lax.broadcasted_iota(jnp.int32, (c, c), 1)
    incl, strict = (row >= col)[None], (row > col)[None]
    decay = jnp.where(incl, jnp.exp(jnp.where(incl, gc - gr, 0.0)), 0.0)
    kb = kg * bt
    vb = vg * bt
    m = _bmm_nt(kb.astype(BF16), kg.astype(BF16))
    a = jnp.where(strict, m * decay, 0.0)
    bp = -a
    tm = jnp.where((row == col)[None], 1.0, 0.0) + bp
    steps = max(1, (c - 1).bit_length()) - 1
    for _ in range(steps):
        bp = _bmm(bp, bp, HI)
        tm = tm + _bmm(tm, bp, HI)
    eg = jnp.exp(gc)
    glast = gc[:, c - 1:c, :]
    erel = jnp.exp(glast - gc)
    kbg = kb * eg
    qm = _bmm_nt(qg.astype(BF16), kg.astype(BF16))
    return dict(decay=decay, strict=strict, kb=kb, vb=vb, m=m, tm=tm, eg=eg, erel=erel, kbg=kbg, qm=qm,
                u=_bmm(tm, vb, HI), w=_bmm(tm, kbg, HI), qd=qg * eg, kt=kg * erel, gl=jnp.exp(glast))


def _dot(a, b):
    return jnp.dot(a.astype(BF16), b.astype(BF16), preferred_element_type=F32)


def _dot_nt(a, b):
    return lax.dot_general(a.astype(BF16), b.astype(BF16), (((1,), (1,)), ((), ())), preferred_element_type=F32)


def _dot_tn(a, b):
    return lax.dot_general(a.astype(BF16), b.astype(BF16), (((0,), (0,)), ((), ())), preferred_element_type=F32)


def _dn_chunk_specs(t, heads):
    n = t // DN_CHUNK
    hs = pl.BlockSpec((t, DN_HEAD_DIM), lambda h: (0, h))
    gcs = pl.BlockSpec((1, n, DN_CHUNK, 1), lambda h: (h, 0, 0, 0))
    grs = pl.BlockSpec((1, n, 1, DN_CHUNK), lambda h: (h, 0, 0, 0))
    ss = pl.BlockSpec((1, n, DN_HEAD_DIM, DN_HEAD_DIM), lambda h: (h, 0, 0, 0))
    return n, hs, gcs, grs, ss


def _dn_chunk_fwd(q, k, v, gc, gr, bc, name):
    t, d = q.shape
    heads = d // DN_HEAD_DIM
    c, dk = DN_CHUNK, DN_HEAD_DIM
    n, hs, gcs, grs, ss = _dn_chunk_specs(t, heads)
    nb = min(DN_GROUP, n)
    scale = dk ** -0.5

    def body(q_ref, k_ref, v_ref, gc_ref, gr_ref, b_ref, o_ref, s_ref, u_s, w_s, at_s, qd_s, kt_s, gl_s):
        def group(gi, carry):
            r0 = pl.multiple_of(gi * (nb * c), nb * c)
            n0 = gi * nb
            ld = lambda ref: ref[pl.ds(r0, nb * c), :].reshape(nb, c, dk)
            loc = _dn_local(ld(q_ref) * scale, ld(k_ref), ld(v_ref), gc_ref[0, pl.ds(n0, nb)],
                            gr_ref[0, pl.ds(n0, nb)], b_ref[0, pl.ds(n0, nb)])
            u_s[pl.ds(n0, nb)] = loc["u"]
            w_s[pl.ds(n0, nb)] = loc["w"]
            at_s[pl.ds(n0, nb)] = loc["qm"] * loc["decay"]
            qd_s[pl.ds(n0, nb)] = loc["qd"]
            kt_s[pl.ds(n0, nb)] = loc["kt"]
            gl_s[pl.ds(n0, nb)] = loc["gl"]
            return carry
        lax.fori_loop(0, n // nb, group, 0)

        def chunk(i, s):
            s_ref[0, i] = s
            vnew = u_s[i] - _dot(w_s[i], s)
            o = _dot(qd_s[i], s) + _dot(at_s[i], vnew)
            o_ref[pl.ds(pl.multiple_of(i * c, c), c), :] = o
            return s * gl_s[i] + _dot_tn(kt_s[i], vnew)
        lax.fori_loop(0, n, chunk, jnp.zeros((dk, dk), F32))

    scratch = [pltpu.VMEM((n, c, dk), F32), pltpu.VMEM((n, c, dk), F32), pltpu.VMEM((n, c, c), F32),
               pltpu.VMEM((n, c, dk), F32), pltpu.VMEM((n, c, dk), F32), pltpu.VMEM((n, 1, 1), F32)]
    return pl.pallas_call(
        body, grid=(heads,), in_specs=[hs, hs, hs, gcs, grs, gcs], out_specs=[hs, ss],
        out_shape=[jax.ShapeDtypeStruct((t, d), F32), jax.ShapeDtypeStruct((heads, n, dk, dk), F32)],
        scratch_shapes=scratch, name=name, compiler_params=_cparams(("parallel",)))(q, k, v, gc, gr, bc)


def _dn_chunk_bwd(q, k, v, gc, gr, bc, s_all, do, name):
    t, d = q.shape
    heads = d // DN_HEAD_DIM
    c, dk = DN_CHUNK, DN_HEAD_DIM
    n, hs, gcs, grs, ss = _dn_chunk_specs(t, heads)
    nb = min(DN_GROUP, n)
    scale = dk ** -0.5

    def body(q_ref, k_ref, v_ref, gc_ref, gr_ref, b_ref, s_ref, do_ref,
             dq_ref, dk_ref, dv_ref, dgc_ref, dgr_ref, db_ref,
             u_s, w_s, att_s, qd_s, kt_s, gl_s, du_s, dw_s, dat_s, dqd_s, dkt_s, dgl_s):
        def load_group(gi):
            r0 = pl.multiple_of(gi * (nb * c), nb * c)
            n0 = gi * nb
            ld = lambda ref: ref[pl.ds(r0, nb * c), :].reshape(nb, c, dk)
            qg, kg, vg = ld(q_ref) * scale, ld(k_ref), ld(v_ref)
            gcv, grv, bt = gc_ref[0, pl.ds(n0, nb)], gr_ref[0, pl.ds(n0, nb)], b_ref[0, pl.ds(n0, nb)]
            return r0, n0, qg, kg, vg, gcv, grv, bt, _dn_local(qg, kg, vg, gcv, grv, bt)

        def group_a(gi, carry):
            _, n0, qg, kg, _, gcv, grv, _, loc = load_group(gi)
            row = lax.broadcasted_iota(jnp.int32, (c, c), 0)
            col = lax.broadcasted_iota(jnp.int32, (c, c), 1)
            upper = (col >= row)[None]
            decay_t = jnp.where(upper, jnp.exp(jnp.where(upper, grv - gcv, 0.0)), 0.0)
            u_s[pl.ds(n0, nb)] = loc["u"]
            w_s[pl.ds(n0, nb)] = loc["w"]
            att_s[pl.ds(n0, nb)] = _bmm_nt(kg.astype(BF16), qg.astype(BF16)) * decay_t
            qd_s[pl.ds(n0, nb)] = loc["qd"]
            kt_s[pl.ds(n0, nb)] = loc["kt"]
            gl_s[pl.ds(n0, nb)] = loc["gl"]
            return carry
        lax.fori_loop(0, n // nb, group_a, 0)

        def chunk_b(it, ds_next):
            i = n - 1 - it
            s = s_ref[0, i]
            dov = do_ref[pl.ds(pl.multiple_of(i * c, c), c), :]
            w, kt = w_s[i], kt_s[i]
            vnew = u_s[i] - _dot(w, s)
            dvnew = _dot(att_s[i], dov) + _dot(kt, ds_next)
            du_s[i] = dvnew
            dw_s[i] = -_dot_nt(dvnew, s)
            dat_s[i] = _dot_nt(dov, vnew)
            dqd_s[i] = _dot_nt(dov, s)
            dkt_s[i] = _dot_nt(vnew, ds_next)
            dgl_s[i] = jnp.sum(jnp.sum(ds_next * s, axis=1, keepdims=True), axis=0, keepdims=True)
            return ds_next * gl_s[i] + _dot_tn(qd_s[i], dov) - _dot_tn(w, dvnew)
        lax.fori_loop(0, n, chunk_b, jnp.zeros((dk, dk), F32))

        def group_c(gi, carry):
            r0, n0, qg, kg, vg, gcv, grv, bt, loc = load_group(gi)
            sl = pl.ds(n0, nb)
            du, dw, dat, dqd, dkt, dgl = du_s[sl], dw_s[sl], dat_s[sl], dqd_s[sl], dkt_s[sl], dgl_s[sl]
            tm, decay, kb, kbg = loc["tm"], loc["decay"], loc["kb"], loc["kbg"]
            dvb = _bmm_tn(tm, du, HI)
            dkbg = _bmm_tn(tm, dw, HI)
            dt = _bmm_nt(du, loc["vb"], HI) + _bmm_nt(dw, kbg, HI)
            da = jnp.where(loc["strict"], -_bmm_tn(tm, _bmm_nt(dt, tm, HI), HI), 0.0)
            dms = (da * decay).astype(BF16)
            dqs = (dat * decay).astype(BF16)
            kgb = kg.astype(BF16)
            dkb = _bmm(dms, kgb) + dkbg * loc["eg"]
            dqt = _bmm(dqs, kgb) + dqd * loc["eg"]
            dkk = _bmm_tn(dms, kb.astype(BF16)) + _bmm_tn(dqs, qg.astype(BF16)) + dkt * loc["erel"] + dkb * bt
            e = (da * loc["m"] + dat * loc["qm"]) * decay
            lsum = lambda x: jnp.sum(x, axis=2, keepdims=True)
            dkt_kt = lsum(dkt * loc["kt"])
            dgcv = lsum(e) + lsum(dqd * loc["qd"]) - dkt_kt + lsum(dkbg * kbg)
            dglast = jnp.sum(dkt_kt, axis=1, keepdims=True) + dgl * loc["gl"]
            rowc = lax.broadcasted_iota(jnp.int32, (1, c, 1), 1)
            dgc_ref[0, sl] = dgcv + jnp.where(rowc == c - 1, dglast, 0.0)
            dgr_ref[0, sl] = -jnp.sum(e, axis=1, keepdims=True)
            db_ref[0, sl] = lsum(dkb * kg) + lsum(dvb * vg)
            rows = pl.ds(r0, nb * c)
            dq_ref[rows, :] = (dqt * scale).reshape(nb * c, dk)
            dk_ref[rows, :] = dkk.reshape(nb * c, dk)
            dv_ref[rows, :] = (dvb * bt).reshape(nb * c, dk)
            return carry
        lax.fori_loop(0, n // nb, group_c, 0)

    big = lambda: pltpu.VMEM((n, c, dk), F32)
    sq = lambda: pltpu.VMEM((n, c, c), F32)
    one = lambda: pltpu.VMEM((n, 1, 1), F32)
    scratch = [big(), big(), sq(), big(), big(), one(), big(), big(), sq(), big(), big(), one()]
    return pl.pallas_call(
        body, grid=(heads,), in_specs=[hs, hs, hs, gcs, grs, gcs, ss, hs], out_specs=[hs, hs, hs, gcs, grs, gcs],
        out_shape=[jax.ShapeDtypeStruct((t, d), F32)] * 3 + [
            jax.ShapeDtypeStruct((heads, n, c, 1), F32), jax.ShapeDtypeStruct((heads, n, 1, c), F32),
            jax.ShapeDtypeStruct((heads, n, c, 1), F32)],
        scratch_shapes=scratch, name=name, compiler_params=_cparams(("parallel",)))(q, k, v, gc, gr, bc, s_all, do)


def _sb_tile(z, mask, r):
    bk = z.shape[1]
    row = lax.broadcasted_iota(jnp.int32, (bk, bk), 0)
    col = lax.broadcasted_iota(jnp.int32, (bk, bk), 1)
    lsm = -_softplus(z)
    lm = lsm if mask is None else jnp.where(mask, lsm, 0.0)
    cs = jnp.dot(lm, (row > col).astype(F32), precision=HI, preferred_element_type=F32)
    ls = z + lsm
    a = jnp.exp(ls + cs + r)
    if mask is not None:
        a = jnp.where(mask, a, 0.0)
    return a, jnp.exp(ls), r + jnp.sum(lm, axis=1, keepdims=True)


def _sb_sweep(bq, bk, qi, visit, carry):
    nd = bq // bk
    row = lax.broadcasted_iota(jnp.int32, (bq, bk), 0)
    col = lax.broadcasted_iota(jnp.int32, (bq, bk), 1)
    for dd in reversed(range(nd)):
        carry = visit(qi * nd + dd, (col + dd * bk) < row, carry)
    return lax.fori_loop(0, qi * nd, lambda it, c: visit(qi * nd - 1 - it, None, c), carry)


def _sb_specs(t, d, bq, vcol):
    qs = pl.BlockSpec((bq, LANES), lambda g, i: (i, g))
    ks = pl.BlockSpec((t, LANES), lambda g, i: (0, g))
    vs = pl.BlockSpec((t, LANES), lambda g, i: (0, g + vcol))
    return qs, ks, vs


def _sb_fwd(qn, kn, qkv, name):
    t, d = qn.shape
    bq, bk = min(SB_QBLOCK, t), min(SB_BLOCK, t)
    scale = SB_HEAD_DIM ** -0.5
    nt = (((1,), (1,)), ((), ()))

    def body(q_ref, k_ref, v_ref, o_ref):
        qi = pl.program_id(1)
        h0 = lax.broadcasted_iota(jnp.int32, (1, LANES), 1) < SB_HEAD_DIM
        qf = q_ref[...] * scale
        qh = [jnp.where(h0, qf, 0.0).astype(BF16), jnp.where(h0, 0.0, qf).astype(BF16)]

        def visit(j, mask, carry):
            acc, r = carry
            rows = pl.ds(pl.multiple_of(j * bk, bk), bk)
            kj = k_ref[rows, :].astype(BF16)
            vj = v_ref[rows, :]
            vh = [jnp.where(h0, vj, 0.0).astype(BF16), jnp.where(h0, 0.0, vj).astype(BF16)]
            r = list(r)
            for h in range(2):
                z = lax.dot_general(qh[h], kj, nt, preferred_element_type=F32)
                a, _, r[h] = _sb_tile(z, mask, r[h])
                acc = acc + jnp.dot(a.astype(BF16), vh[h], preferred_element_type=F32)
            return acc, tuple(r)
        zero = jnp.zeros((bq, 1), F32)
        acc, _ = _sb_sweep(bq, bk, qi, visit, (jnp.zeros((bq, LANES), F32), (zero, zero)))
        o_ref[...] = acc.astype(o_ref.dtype)

    qs, ks, vs = _sb_specs(t, d, bq, 2 * d // LANES)
    return pl.pallas_call(
        body, grid=(d // LANES, t // bq), in_specs=[qs, ks, vs], out_specs=qs,
        out_shape=jax.ShapeDtypeStruct((t, d), BF16), name=name,
        compiler_params=_cparams(("parallel", "parallel")))(qn, kn, qkv)


def _sb_bwd(qn, kn, qkv, do, name):
    t, d = qn.shape
    bq, bk = min(SB_QBLOCK, t), min(SB_BLOCK, t)
    scale = SB_HEAD_DIM ** -0.5
    nt = (((1,), (1,)), ((), ()))

    def body(q_ref, k_ref, v_ref, do_ref, dq_ref, dk_ref, dv_ref, p_s, sg_s):
        qi = pl.program_id(1)

        @pl.when(qi == 0)
        def _():
            dk_ref[...] = jnp.zeros(dk_ref.shape, F32)
            dv_ref[...] = jnp.zeros(dv_ref.shape, F32)

        h0 = lax.broadcasted_iota(jnp.int32, (1, LANES), 1) < SB_HEAD_DIM
        split = lambda a: [jnp.where(h0, a, 0.0).astype(BF16), jnp.where(h0, 0.0, a).astype(BF16)]
        qh = split(q_ref[...] * scale)
        doh = split(do_ref[...])
        row = lax.broadcasted_iota(jnp.int32, (bk, bk), 0)
        col = lax.broadcasted_iota(jnp.int32, (bk, bk), 1)
        later_incl = (row >= col).astype(F32)
        zero = jnp.zeros((bq, 1), F32)

        def visit1(j, mask, carry):
            rows = pl.ds(pl.multiple_of(j * bk, bk), bk)
            kj = k_ref[rows, :].astype(BF16)
            vj = v_ref[rows, :].astype(BF16)
            r, sp = list(carry[0]), list(carry[1])
            dv = jnp.zeros((bk, LANES), F32)
            for h in range(2):
                z = lax.dot_general(qh[h], kj, nt, preferred_element_type=F32)
                a, sg, r[h] = _sb_tile(z, mask, r[h])
                p = a * lax.dot_general(doh[h], vj, nt, preferred_element_type=F32)
                p_s[h, j] = p
                sg_s[h, j] = sg
                sp[h] = sp[h] + jnp.sum(p, axis=1, keepdims=True)
                dv = dv + _dot_tn(a, doh[h])
            dv_ref[rows, :] += dv
            return tuple(r), tuple(sp)
        _, total = _sb_sweep(bq, bk, qi, visit1, ((zero, zero), (zero, zero)))

        def visit2(j, mask, carry):
            rows = pl.ds(pl.multiple_of(j * bk, bk), bk)
            kh = split(k_ref[rows, :])
            dq, sp = carry[0], list(carry[1])
            dk = jnp.zeros((bk, LANES), F32)
            for h in range(2):
                p, sg = p_s[h, j], sg_s[h, j]
                pref = total[h] - sp[h] - jnp.dot(p, later_incl, precision=HI, preferred_element_type=F32)
                dz = p * (1.0 - sg) - pref * sg
                if mask is not None:
                    dz = jnp.where(mask, dz, 0.0)
                dz = dz.astype(BF16)
                sp[h] = sp[h] + jnp.sum(p, axis=1, keepdims=True)
                dk = dk + lax.dot_general(dz, qh[h], (((0,), (0,)), ((), ())), preferred_element_type=F32)
                dq = dq + jnp.dot(dz, kh[h], preferred_element_type=F32)
            dk_ref[rows, :] += dk
            return dq, tuple(sp)
        dq, _ = _sb_sweep(bq, bk, qi, visit2, (jnp.zeros((bq, LANES), F32), (zero, zero)))
        dq_ref[...] = dq * scale

    qs, ks, vs = _sb_specs(t, d, bq, 2 * d // LANES)
    shp = jax.ShapeDtypeStruct((t, d), F32)
    scratch = [pltpu.VMEM((2, t // bk, bq, bk), F32), pltpu.VMEM((2, t // bk, bq, bk), F32)]
    return pl.pallas_call(
        body, grid=(d // LANES, t // bq), in_specs=[qs, ks, vs, qs], out_specs=[qs, ks, ks],
        out_shape=[shp, shp, shp], scratch_shapes=scratch, name=name,
        compiler_params=_cparams(("parallel", "arbitrary")))(qn, kn, qkv, do)


def _adamw(w, g, m, v, name):
    shape = w.shape
    two_d = lambda a: a.reshape(-1, shape[-1])
    c1 = 1.0 - ADAM_B1 ** ADAM_STEP
    c2 = 1.0 - ADAM_B2 ** ADAM_STEP

    def body(r, _):
        wv, gv, mv, vv = r
        mn = ADAM_B1 * mv + (1.0 - ADAM_B1) * gv
        vn = ADAM_B2 * vv + (1.0 - ADAM_B2) * (gv * gv)
        delta = -ADAM_LR * ((mn / c1) / (jnp.sqrt(vn / c2) + ADAM_EPS) + ADAM_WD * wv)
        return [delta, mn, vn], []
    width = shape[-1]
    res = _rowwise(body, [two_d(w), two_d(g), two_d(m), two_d(v)], [], [(width, F32)] * 3, [], name=name)
    return [r.reshape(shape) for r in res]


def _dn_layer_fwd(h1, w, conv_w, a_log, dt_bias, onorm_g, tag):
    t, d = h1.shape
    heads = d // DN_HEAD_DIM
    n = t // DN_CHUNK
    proj = _mm(h1, w["all"], name=tag + "_proj")
    qn, kn, vv = _dn_prep_fwd(proj, conv_w, heads, tag + "_prep")
    big_g, beta = _dn_gate_fwd(proj, 4 * d // (2 * LANES), a_log, dt_bias, tag + "_gate")
    gt_ = big_g[:, :heads].T.reshape(heads, n, DN_CHUNK)
    gc, gr = gt_[..., None], gt_[:, :, None, :]
    bc = beta[:, :heads].T.reshape(heads, n, DN_CHUNK)[..., None]
    o, s_all = _dn_chunk_fwd(qn, kn, vv, gc, gr, bc, tag + "_chunk")
    z = proj[:, 3 * d:4 * d].reshape(t * heads, DN_HEAD_DIM)
    on = _dn_post_fwd(o.reshape(t * heads, DN_HEAD_DIM), z, onorm_g, tag + "_post").reshape(t, d)
    y = _mm(on, w["out"], name=tag + "_out")
    return y, dict(proj=proj, qn=qn, kn=kn, v=vv, gc=gc, gr=gr, bc=bc, o=o, s_all=s_all, z=z, on=on)


def _dn_layer_bwd(dy, h1, w, conv_w, a_log, dt_bias, onorm_g, sv, tag):
    t, d = h1.shape
    heads = d // DN_HEAD_DIM
    don = _mm(dy, w["out"], tb=True, name=tag + "_dout")
    g_out = _mm(sv["on"], dy, ta=True, name=tag + "_gwout")
    do, dz, g_on = _dn_post_bwd(sv["o"].reshape(t * heads, DN_HEAD_DIM), sv["z"],
                                don.reshape(t * heads, DN_HEAD_DIM), onorm_g, tag + "_dpost")
    dq, dk, dv, dgc, dgr, dbc = _dn_chunk_bwd(sv["qn"], sv["kn"], sv["v"], sv["gc"], sv["gr"], sv["bc"], sv["s_all"],
                                              do.reshape(t, d), tag + "_dchunk")
    pad = lambda a: jnp.pad(a.reshape(heads, t).T, ((0, 0), (0, LANES - heads)))
    d_big_g = pad(dgc) + pad(dgr)
    dab, g_alog, g_dt = _dn_gate_bwd(sv["proj"], 4 * d // (2 * LANES), d_big_g, pad(dbc), a_log, dt_bias, tag + "_dgate")
    dxq, dxk, dxv, wq, wk, wv = _dn_prep_bwd(sv["proj"], conv_w, dq, dk, dv, heads, tag + "_dprep")
    dproj = jnp.concatenate([dxq, dxk, dxv, dz.reshape(t, d), dab], axis=1)
    dh1 = _mm(dproj, w["all"], tb=True, name=tag + "_dh")
    g_all = _mm(h1, dproj, ta=True, name=tag + "_gwin")
    grads = dict(w_all=g_all, w_out=g_out, conv_w=jnp.concatenate([wq, wk, wv], axis=1), a_log=g_alog, dt_bias=g_dt,
                 onorm_g=g_on)
    return dh1, grads


def _sb_layer_fwd(h1, w, q_g, k_g, tag):
    t, d = h1.shape
    heads = d // SB_HEAD_DIM
    qkv = _mm(h1, w["qkv"], name=tag + "_proj")
    gq, gk = jnp.tile(q_g, (1, heads)), jnp.tile(k_g, (1, heads))
    qn, kn = _sb_norm_fwd(qkv, gq, gk, tag + "_norm")
    o = _sb_fwd(qn, kn, qkv, tag + "_attn")
    y = _mm(o, w["out"], name=tag + "_out")
    return y, dict(qkv=qkv, qn=qn, kn=kn, o=o, gq=gq, gk=gk)


def _sb_layer_bwd(dy, h1, w, q_g, k_g, sv, tag):
    t, d = h1.shape
    heads = d // SB_HEAD_DIM
    do = _mm(dy, w["out"], tb=True, name=tag + "_dout")
    g_out = _mm(sv["o"], dy, ta=True, name=tag + "_gwout")
    dqn, dkn, dv = _sb_bwd(sv["qn"], sv["kn"], sv["qkv"], do, tag + "_dattn")
    dqkv, g_q, g_k = _sb_norm_bwd(sv["qkv"], dqn, dkn, dv, sv["gq"], sv["gk"], tag + "_dnorm")
    fold = lambda g: jnp.sum(g.reshape(heads, SB_HEAD_DIM), axis=0, keepdims=True)
    dh1 = _mm(dqkv, w["qkv"], tb=True, name=tag + "_dh")
    g_qkv = _mm(h1, dqkv, ta=True, name=tag + "_gwin")
    return dh1, dict(w_qkv=g_qkv, w_out=g_out, q_norm_g=fold(g_q), k_norm_g=fold(g_k))


def _local_step(x, tgt, mod, norm1_g, norm2_g, dn, sb, ffn):
    depth = mod.shape[0]
    d = x.shape[1]
    saved = []
    for i in range(depth):
        mv = [mod[i:i + 1, j * d:(j + 1) * d] for j in range(N_MOD)]
        sh1, sc1, gt1, sh2, sc2, gt2 = mv
        tag = "l%d" % i
        h1 = _adaln_fwd(x, norm1_g[i:i + 1], sc1, sh1, tag + "_ln1")
        if i % 2 == 0:
            p = dn[i // 2]
            y, sv = _dn_layer_fwd(h1, p, p["conv_w"], p["a_log"], p["dt_bias"], p["onorm_g"], tag + "_dn")
        else:
            p = sb[i // 2]
            y, sv = _sb_layer_fwd(h1, p, p["q_g"], p["k_g"], tag + "_sb")
        x1 = _resid_fwd(x, y, gt1, tag + "_res1")
        h2 = _adaln_fwd(x1, norm2_g[i:i + 1], sc2, sh2, tag + "_ln2")
        u = _mm(h2, ffn[i]["w_in"], name=tag + "_ffn_in")
        a = _swiglu_fwd(u, tag + "_swiglu")
        y2 = _mm(a, ffn[i]["w_out"], name=tag + "_ffn_out")
        x2 = _resid_fwd(x1, y2, gt2, tag + "_res2")
        saved.append(dict(x0=x, h1=h1, y=y, mix=sv, x1=x1, h2=h2, u=u, a=a, y2=y2))
        x = x2

    loss, dx = _loss_fwd_bwd(x, tgt, "loss")

    dmod, dn1, dn2 = [None] * depth, [None] * depth, [None] * depth
    g_dn, g_sb, g_ffn = [None] * len(dn), [None] * len(sb), [None] * depth
    for i in reversed(range(depth)):
        s = saved[i]
        mv = [mod[i:i + 1, j * d:(j + 1) * d] for j in range(N_MOD)]
        sh1, sc1, gt1, sh2, sc2, gt2 = mv
        tag = "l%d" % i
        dy2, dgt2 = _resid_bwd(dx, s["y2"], gt2, tag + "_dres2")
        da = _mm(dy2, ffn[i]["w_out"], tb=True, name=tag + "_dffn_a")
        g_wout = _mm(s["a"], dy2, ta=True, name=tag + "_gffn_out")
        du = _swiglu_bwd(s["u"], da, tag + "_dswiglu")
        dh2 = _mm(du, ffn[i]["w_in"], tb=True, name=tag + "_dffn_h")
        g_win = _mm(s["h2"], du, ta=True, name=tag + "_gffn_in")
        g_ffn[i] = dict(w_in=g_win, w_out=g_wout)
        dx, dg2, dsc2, dsh2 = _adaln_bwd(s["x1"], dh2, dx, norm2_g[i:i + 1], sc2, tag + "_dln2")
        dy, dgt1 = _resid_bwd(dx, s["y"], gt1, tag + "_dres1")
        if i % 2 == 0:
            p = dn[i // 2]
            dh1, g_dn[i // 2] = _dn_layer_bwd(dy, s["h1"], p, p["conv_w"], p["a_log"], p["dt_bias"], p["onorm_g"],
                                               s["mix"], tag + "_dn")
        else:
            p = sb[i // 2]
            dh1, g_sb[i // 2] = _sb_layer_bwd(dy, s["h1"], p, p["q_g"], p["k_g"], s["mix"], tag + "_sb")
        dx, dg1, dsc1, dsh1 = _adaln_bwd(s["x0"], dh1, dx, norm1_g[i:i + 1], sc1, tag + "_dln1")
        dmod[i] = jnp.concatenate([dsh1, dsc1, dgt1, dsh2, dsc2, dgt2], axis=1)
        dn1[i], dn2[i] = dg1, dg2
    return (loss, dx, jnp.concatenate(dmod, axis=0), jnp.concatenate(dn1, axis=0), jnp.concatenate(dn2, axis=0),
            g_dn, g_sb, g_ffn)


def _axes():
    return lax.axis_index("x"), lax.axis_index("y"), lax.axis_index("c")


def _remote(src, dst, send_sem, recv_sem, dev):
    return pltpu.make_async_remote_copy(src_ref=src, dst_ref=dst, send_sem=send_sem, recv_sem=recv_sem,
                                        device_id=dev, device_id_type=MESH)


def _other_chips(x, y):
    return [(1 - x, y), (x, 1 - y), (1 - x, 1 - y)]


def _allgather_small(v, name):
    m, n = v.shape

    def body(x_ref, out_ref, send_sems, recv_sems, local_sem):
        x, y, c = _axes()
        me, sibling = (x, y, c), (x, y, 1 - c)
        chips = _other_chips(x, y)

        def rows(px, py, pc):
            return out_ref.at[pl.ds((4 * px + 2 * py + pc) * m, m), :]

        def copy(k, block, to, src=None):
            return _remote(rows(*block) if src is None else src, rows(*block), send_sems.at[k], recv_sems.at[k], to)

        mine = pltpu.make_async_copy(x_ref, rows(*me), local_sem)
        mine.start()
        first = [copy(0, me, sibling, src=x_ref)]
        first += [copy(1 + j, me, (*chip, c), src=x_ref) for j, chip in enumerate(chips)]
        for cp in first:
            cp.start()
        passed = [copy(4 + j, (*chip, c), sibling) for j, chip in enumerate(chips)]
        for j, chip in enumerate(chips):
            copy(1 + j, (*chip, c), me).wait_recv()
            passed[j].start()
        copy(0, sibling, me).wait_recv()
        for j, chip in enumerate(chips):
            copy(4 + j, (*chip, 1 - c), me).wait_recv()
        for cp in first + passed:
            cp.wait_send()
        mine.wait()

    return pl.pallas_call(
        body, out_shape=jax.ShapeDtypeStruct((8 * m, n), v.dtype),
        in_specs=[pl.BlockSpec(memory_space=pltpu.VMEM)], out_specs=pl.BlockSpec(memory_space=pltpu.VMEM),
        scratch_shapes=[pltpu.SemaphoreType.DMA((7,)), pltpu.SemaphoreType.DMA((7,)), pltpu.SemaphoreType.DMA],
        name=name, compiler_params=pltpu.CompilerParams(vmem_limit_bytes=VMEM_LIMIT))(v)


def _allgather_weights(packed, name):
    _, rh, w = packed.shape

    def body(p_ref, out_ref, send_sems, recv_sems, local_sem):
        x, y, c = _axes()
        sibling = (x, y, 1 - c)
        chips = _other_chips(x, y)

        def blk(cx, cy, h):
            return out_ref.at[2 * cx + cy, h]

        mine = pltpu.make_async_copy(p_ref, out_ref.at[2 * x + y], local_sem)
        mine.start()
        first = [_remote(p_ref.at[c], blk(x, y, c), send_sems.at[j], recv_sems.at[j], (*chip, c))
                 for j, chip in enumerate(chips)]
        for cp in first:
            cp.start()
        passed = [_remote(blk(*chip, c), blk(*chip, c), send_sems.at[3 + j], recv_sems.at[3 + j], sibling)
                  for j, chip in enumerate(chips)]
        for j, chip in enumerate(chips):
            _remote(p_ref.at[c], blk(*chip, c), send_sems.at[j], recv_sems.at[j], sibling).wait_recv()
            passed[j].start()
        for j, chip in enumerate(chips):
            _remote(p_ref.at[c], blk(*chip, 1 - c), send_sems.at[3 + j], recv_sems.at[3 + j], sibling).wait_recv()
        for cp in first + passed:
            cp.wait_send()
        mine.wait()

    hbm = pl.BlockSpec(memory_space=pltpu.HBM)
    return pl.pallas_call(
        body, out_shape=jax.ShapeDtypeStruct((4, 2, rh, w), packed.dtype), in_specs=[hbm], out_specs=hbm,
        scratch_shapes=[pltpu.SemaphoreType.DMA((6,)), pltpu.SemaphoreType.DMA((6,)), pltpu.SemaphoreType.DMA],
        name=name)(packed)


def _sibling_swap(v, pick_other_half, name):
    shape = v.shape[1:] if pick_other_half else v.shape

    def body(v_ref, out_ref, send_sem, recv_sem):
        x, y, c = _axes()
        cp = _remote(v_ref.at[1 - c] if pick_other_half else v_ref, out_ref, send_sem, recv_sem, (x, y, 1 - c))
        cp.start()
        cp.wait()

    hbm = pl.BlockSpec(memory_space=pltpu.HBM)
    return pl.pallas_call(
        body, out_shape=jax.ShapeDtypeStruct(shape, v.dtype), in_specs=[hbm], out_specs=hbm,
        scratch_shapes=[pltpu.SemaphoreType.DMA, pltpu.SemaphoreType.DMA], name=name)(v)


def _chip_scatter(v, name):
    def body(v_ref, out_ref, send_sems, recv_sems, local_sem):
        x, y, c = _axes()
        me = 2 * x + y
        chips = _other_chips(x, y)
        mine = pltpu.make_async_copy(v_ref.at[me], out_ref.at[me], local_sem)
        mine.start()
        sends = [_remote(v_ref.at[2 * cx + cy], out_ref.at[me], send_sems.at[j], recv_sems.at[j], (cx, cy, c))
                 for j, (cx, cy) in enumerate(chips)]
        for cp in sends:
            cp.start()
        for j, (cx, cy) in enumerate(chips):
            _remote(v_ref.at[me], out_ref.at[2 * cx + cy], send_sems.at[j], recv_sems.at[j], (cx, cy, c)).wait_recv()
        for cp in sends:
            cp.wait_send()
        mine.wait()

    hbm = pl.BlockSpec(memory_space=pltpu.HBM)
    return pl.pallas_call(
        body, out_shape=jax.ShapeDtypeStruct(v.shape, v.dtype), in_specs=[hbm], out_specs=hbm,
        scratch_shapes=[pltpu.SemaphoreType.DMA((3,)), pltpu.SemaphoreType.DMA((3,)), pltpu.SemaphoreType.DMA],
        name=name)(v)


def _sum_lead(v, name, tr=256):
    k, r, w = v.shape

    def body(v_ref, o_ref):
        acc = v_ref[0].astype(F32)
        for i in range(1, k):
            acc = acc + v_ref[i].astype(F32)
        o_ref[...] = acc

    return pl.pallas_call(
        body, grid=(r // tr,), in_specs=[pl.BlockSpec((k, tr, w), lambda i: (0, i, 0))],
        out_specs=pl.BlockSpec((tr, w), lambda i: (i, 0)), out_shape=jax.ShapeDtypeStruct((r, w), F32), name=name,
        compiler_params=_cparams(("parallel",)))(v)


_BIG = (("dn_w_in", 2), ("dn_w_out", 1), ("sb_w_qkv", 2), ("sb_w_out", 1), ("ffn_w_in", 2), ("ffn_w_out", 1))


def _pack_rows(shard_shapes):
    counts = [l * (r // 2) * cs // PACK_W for l, r, cs in shard_shapes]
    total = sum(counts)
    return counts, -(-total // PACK_ALIGN) * PACK_ALIGN


def _pack(parts, rh):
    lead = parts[0].shape[:-4]
    nl = len(lead)
    flat = []
    for p in parts:
        p = jnp.moveaxis(p, nl + 1, 0)
        flat.append(p.reshape((2,) + lead + (-1, PACK_W)))
    used = sum(f.shape[-2] for f in flat)
    flat.append(jnp.zeros((2,) + lead + (rh - used, PACK_W), flat[0].dtype))
    return jnp.concatenate(flat, axis=nl + 1)


def _unpack(buf, shard_shapes, counts):
    lead = buf.shape[:-3]
    out, off = [], 0
    for (l, r, cs), n in zip(shard_shapes, counts):
        p = buf[..., off:off + n, :].reshape(lead + (2, l, r // 2, cs))
        out.append(jnp.moveaxis(p, len(lead), len(lead) + 1).reshape(lead + (l, r, cs)))
        off += n
    return out


def kernel(x, c, ada_w, ada_b, norm1_g, norm2_g, dn_w_in, dn_conv_w, dn_a_log, dn_dt_bias, dn_onorm_g, dn_w_out, sb_w_qkv, sb_q_norm_g, sb_k_norm_g, sb_w_out, ffn_w_in, ffn_w_out, loss_target, m_ada_w, m_ada_b, m_norm1_g, m_norm2_g, m_dn_w_in, m_dn_conv_w, m_dn_a_log, m_dn_dt_bias, m_dn_onorm_g, m_dn_w_out, m_sb_w_qkv, m_sb_q_norm_g, m_sb_k_norm_g, m_sb_w_out, m_ffn_w_in, m_ffn_w_out, v_ada_w, v_ada_b, v_norm1_g, v_norm2_g, v_dn_w_in, v_dn_conv_w, v_dn_a_log, v_dn_dt_bias, v_dn_onorm_g, v_dn_w_out, v_sb_w_qkv, v_sb_q_norm_g, v_sb_k_norm_g, v_sb_w_out, v_ffn_w_in, v_ffn_w_out):
    names = ("ada_w", "ada_b", "norm1_g", "norm2_g", "dn_w_in", "dn_conv_w", "dn_a_log", "dn_dt_bias", "dn_onorm_g",
             "dn_w_out", "sb_w_qkv", "sb_q_norm_g", "sb_k_norm_g", "sb_w_out", "ffn_w_in", "ffn_w_out")
    w = dict(zip(names, (ada_w, ada_b, norm1_g, norm2_g, dn_w_in, dn_conv_w, dn_a_log, dn_dt_bias, dn_onorm_g,
                         dn_w_out, sb_w_qkv, sb_q_norm_g, sb_k_norm_g, sb_w_out, ffn_w_in, ffn_w_out)))
    mom = dict(zip(names, (m_ada_w, m_ada_b, m_norm1_g, m_norm2_g, m_dn_w_in, m_dn_conv_w, m_dn_a_log, m_dn_dt_bias,
                           m_dn_onorm_g, m_dn_w_out, m_sb_w_qkv, m_sb_q_norm_g, m_sb_k_norm_g, m_sb_w_out, m_ffn_w_in,
                           m_ffn_w_out)))
    var = dict(zip(names, (v_ada_w, v_ada_b, v_norm1_g, v_norm2_g, v_dn_w_in, v_dn_conv_w, v_dn_a_log, v_dn_dt_bias,
                           v_dn_onorm_g, v_dn_w_out, v_sb_w_qkv, v_sb_q_norm_g, v_sb_k_norm_g, v_sb_w_out, v_ffn_w_in,
                           v_ffn_w_out)))
    ax, ay, ac = _axes()
    chip = 2 * ax + ay
    dev = 2 * chip + ac
    t, d = x.shape[1], x.shape[2]
    depth, ndn, nsb = ada_w.shape[0], dn_w_in.shape[0], sb_w_qkv.shape[0]
    heads = d // DN_HEAD_DIM
    mod_cols = ada_w.shape[2]
    conv_cols = dn_conv_w.shape[2]
    nchips, ndev = 4, 8

    conv_rows = ndn * DN_CONV * conv_cols // d
    pay1 = jnp.concatenate([c, dn_conv_w.reshape(conv_rows, d), jnp.zeros((8 - 1 - conv_rows, d), F32)], axis=0)
    g1 = _allgather_small(pay1, "ag_cond").reshape(ndev, 8, d)
    c_all = g1[:, 0]
    conv_full = g1[::2, 1:1 + conv_rows].reshape(nchips, ndn, DN_CONV, conv_cols).transpose(1, 2, 0, 3)
    conv_full = conv_full.reshape(ndn, DN_CONV, nchips * conv_cols)

    c16 = jnp.pad(c_all, ((0, 16 - ndev), (0, 0)))
    cond16 = _rowwise(lambda r, v: ([_silu(r[0])], []), [c16], [], [(d, F32)], [], name="cond_silu")[0]
    pay2 = jnp.concatenate([_mm(cond16, ada_w[i], name="ada_mod%d" % i)[:ndev] for i in range(depth)], axis=0)
    g2 = _allgather_small(pay2, "ag_mod").reshape(ndev, depth, ndev, mod_cols)[::2]
    mod_raw = lax.dynamic_index_in_dim(g2, dev, axis=2, keepdims=False)
    mod_raw = mod_raw.transpose(1, 0, 2).reshape(depth, nchips * mod_cols)
    mod = _rowwise(lambda r, v: ([r[0] + r[1]], []), [mod_raw, ada_b], [], [(nchips * mod_cols, F32)], [],
                   name="ada_bias")[0]

    shard_shapes = [w[n].shape for n, _ in _BIG]
    counts, rh = _pack_rows(shard_shapes)
    halves = lambda a: a.reshape(a.shape[:-2] + (2, a.shape[-2] // 2, a.shape[-1]))
    packed = _pack([halves(w[n].astype(BF16)) for n, _ in _BIG], rh)
    gathered = _allgather_weights(packed, "ag_weights")
    full = {}
    for (n, axis), a in zip(_BIG, _unpack(gathered, shard_shapes, counts)):
        l, r, cs = a.shape[1:]
        if axis == 2:
            full[n] = a.transpose(1, 2, 0, 3).reshape(l, r, nchips * cs)
        else:
            full[n] = a.transpose(1, 0, 2, 3).reshape(l, nchips * r, cs)

    padl = lambda v: jnp.pad(v[None, :], ((0, 0), (0, LANES - v.shape[0])))
    padc = lambda a: jnp.pad(a, ((0, 0), (0, LANES - a.shape[1])))
    dn = []
    for j in range(ndn):
        wi = full["dn_w_in"][j]
        w_all = jnp.concatenate([wi[:, :4 * d], padc(wi[:, 4 * d:4 * d + heads]), padc(wi[:, 4 * d + heads:])], axis=1)
        dn.append(dict(all=w_all, out=full["dn_w_out"][j], conv_w=conv_full[j], a_log=padl(dn_a_log[j]),
                       dt_bias=padl(dn_dt_bias[j]), onorm_g=dn_onorm_g[j][None]))
    sb = [dict(qkv=full["sb_w_qkv"][j], out=full["sb_w_out"][j], q_g=sb_q_norm_g[j][None], k_g=sb_k_norm_g[j][None])
          for j in range(nsb)]
    ffn = [dict(w_in=full["ffn_w_in"][i], w_out=full["ffn_w_out"][i]) for i in range(depth)]

    loss_local, grad_x, dmod, g_n1, g_n2, g_dn, g_sb, g_ffn = _local_step(x[0], loss_target[0], mod, norm1_g, norm2_g,
                                                                           dn, sb, ffn)
    loss = lax.psum(loss_local, ("x", "y", "c"))

    g_conv = jnp.stack([g["conv_w"] for g in g_dn])
    misc = jnp.concatenate([jnp.concatenate([g["onorm_g"] for g in g_dn], axis=1),
                            jnp.concatenate([g["a_log"] for g in g_dn], axis=1),
                            jnp.concatenate([g["dt_bias"] for g in g_dn], axis=1),
                            jnp.concatenate([g["q_norm_g"] for g in g_sb], axis=1),
                            jnp.concatenate([g["k_norm_g"] for g in g_sb], axis=1)], axis=1)
    misc = jnp.pad(misc, ((0, 0), (0, -misc.shape[1] % d))).reshape(-1, d)
    small = [dmod.reshape(-1, d), g_n1, g_n2, g_conv.reshape(-1, d), misc]
    small_rows = [s.shape[0] for s in small]
    pad_rows = -sum(small_rows) % 8
    pay3 = jnp.concatenate(small + [jnp.zeros((pad_rows, d), F32)], axis=0)
    nrow3 = pay3.shape[0]
    g3 = _allgather_small(pay3, "ag_small")
    summed = _rowwise(lambda r, v: ([], [_colsum(r[0])]), [g3.reshape(ndev, nrow3 * d)], [], [], [(1, nrow3 * d)],
                      name="small_sum")[0].reshape(nrow3, d)
    offs = [0]
    for n_ in small_rows:
        offs.append(offs[-1] + n_)
    grads = {}
    grads["ada_b"] = summed[offs[0]:offs[1]].reshape(depth, N_MOD * d)
    grads["norm1_g"] = summed[offs[1]:offs[2]]
    grads["norm2_g"] = summed[offs[2]:offs[3]]
    conv_sum = summed[offs[3]:offs[4]].reshape(ndn, DN_CONV, nchips * conv_cols)
    grads["dn_conv_w"] = lax.dynamic_slice_in_dim(conv_sum, chip * conv_cols, conv_cols, axis=2)
    mrow = summed[offs[4]:offs[5]].reshape(-1)
    o = 0
    grads["dn_onorm_g"] = mrow[o:o + ndn * DN_HEAD_DIM].reshape(ndn, DN_HEAD_DIM)
    o += ndn * DN_HEAD_DIM
    grads["dn_a_log"] = mrow[o:o + ndn * LANES].reshape(ndn, LANES)[:, :heads]
    o += ndn * LANES
    grads["dn_dt_bias"] = mrow[o:o + ndn * LANES].reshape(ndn, LANES)[:, :heads]
    o += ndn * LANES
    grads["sb_q_norm_g"] = mrow[o:o + nsb * SB_HEAD_DIM].reshape(nsb, SB_HEAD_DIM)
    o += nsb * SB_HEAD_DIM
    grads["sb_k_norm_g"] = mrow[o:o + nsb * SB_HEAD_DIM].reshape(nsb, SB_HEAD_DIM)
    dmod_all = g3.reshape(ndev, nrow3, d)[:, :small_rows[0]].reshape(ndev, depth, N_MOD * d)
    dmod_mine = lax.dynamic_slice_in_dim(dmod_all, chip * mod_cols, mod_cols, axis=2)
    dmod16 = jnp.pad(dmod_mine, ((0, 16 - ndev), (0, 0), (0, 0)))
    grads["ada_w"] = jnp.stack([_mm(cond16, dmod16[:, i], ta=True, name="ada_gw%d" % i) for i in range(depth)])

    def gw_in(g):
        ga = g["w_all"]
        return jnp.concatenate([ga[:, :4 * d], ga[:, 4 * d:4 * d + heads], ga[:, 4 * d + LANES:4 * d + LANES + heads]],
                               axis=1)
    gfull = dict(dn_w_in=jnp.stack([gw_in(g) for g in g_dn]), dn_w_out=jnp.stack([g["w_out"] for g in g_dn]),
                 sb_w_qkv=jnp.stack([g["w_qkv"] for g in g_sb]), sb_w_out=jnp.stack([g["w_out"] for g in g_sb]),
                 ffn_w_in=jnp.stack([g["w_in"] for g in g_ffn]), ffn_w_out=jnp.stack([g["w_out"] for g in g_ffn]))
    parts = []
    for (n, axis), (l, r, cs) in zip(_BIG, shard_shapes):
        a = gfull[n].astype(BF16)
        if axis == 2:
            a = a.reshape(l, r, nchips, cs).transpose(2, 0, 1, 3)
        else:
            a = a.reshape(l, nchips, r, cs).transpose(1, 0, 2, 3)
        parts.append(halves(a))
    gp = _pack(parts, rh)
    from_sibling = _sibling_swap(gp, True, "gr_pair")
    own_half = lax.dynamic_index_in_dim(gp, ac, axis=0, keepdims=False)
    pair = _rowwise(lambda r, v: ([r[0].astype(F32) + r[1].astype(F32)], []),
                    [own_half.reshape(nchips * rh, PACK_W), from_sibling.reshape(nchips * rh, PACK_W)], [],
                    [(PACK_W, BF16)], [], name="gr_pair_add")[0]
    from_chips = _chip_scatter(pair.reshape(nchips, rh, PACK_W), "gr_chips")
    reduced = _sum_lead(from_chips, "gr_chip_add")
    other = _sibling_swap(reduced, False, "gr_share")
    both = jnp.where(ac == 0, jnp.stack([reduced, other]), jnp.stack([other, reduced]))
    for (n, _), g in zip(_BIG, _unpack(both, shard_shapes, counts)):
        grads[n] = g

    delta, new_m, new_v = {}, {}, {}
    for n in names:
        delta[n], new_m[n], new_v[n] = _adamw(w[n], grads[n], mom[n], var[n], "adamw_" + n)
    return (loss, grad_x[None], *[grads[n] for n in names], *[delta[n] for n in names], *[new_m[n] for n in names],
            *[new_v[n] for n in names])
```

```python
import functools

import jax
import jax.numpy as jnp
from jax import lax
from jax.experimental import pallas as pl
from jax.experimental.pallas import tpu as pltpu

F32 = jnp.float32
BF16 = jnp.bfloat16
HI = lax.Precision.HIGHEST
MESH = pl.DeviceIdType.MESH

EPS = 1e-6
N_MOD = 6
DN_HEAD_DIM = 128
DN_CONV = 4
DN_CHUNK = 64
DN_GROUP = 8
SB_HEAD_DIM = 64
SB_BLOCK = 128
SB_QBLOCK = 256
LANES = 128
PACK_W = 1024
PACK_ALIGN = 256
VMEM_LIMIT = 56 * 1024 * 1024

ADAM_LR = 0.001
ADAM_B1 = 0.9
ADAM_B2 = 0.999
ADAM_EPS = 1e-08
ADAM_WD = 0.01
ADAM_STEP = 10


def _cparams(sem=None):
    return pltpu.CompilerParams(dimension_semantics=sem, vmem_limit_bytes=VMEM_LIMIT)


def _sigmoid(x):
    return 1.0 / (1.0 + jnp.exp(-x))


def _silu(x):
    return x * _sigmoid(x)


def _dsilu(x):
    s = _sigmoid(x)
    return s * (1.0 + x * (1.0 - s))


def _softplus(x):
    return jnp.maximum(x, 0.0) + jnp.log1p(jnp.exp(-jnp.abs(x)))


def _pick_tile(n, prefs):
    for t in prefs:
        if n % t == 0:
            return t
    return n


def _mm(a, b, *, ta=False, tb=False, out_dtype=F32, name):
    m = a.shape[1] if ta else a.shape[0]
    k = a.shape[0] if ta else a.shape[1]
    n = b.shape[0] if tb else b.shape[1]
    assert (b.shape[1] if tb else b.shape[0]) == k
    tm = _pick_tile(m, (512, 256, 128))
    tn = _pick_tile(n, (512, 256, 128))
    dims = (((0 if ta else 1,), (1 if tb else 0,)), ((), ()))

    def body(a_ref, b_ref, o_ref):
        av = a_ref[...].astype(BF16)
        bv = b_ref[...].astype(BF16)
        o_ref[...] = lax.dot_general(av, bv, dims, preferred_element_type=F32).astype(o_ref.dtype)

    a_spec = pl.BlockSpec((k, tm), lambda i, j: (0, i)) if ta else pl.BlockSpec((tm, k), lambda i, j: (i, 0))
    b_spec = pl.BlockSpec((tn, k), lambda i, j: (j, 0)) if tb else pl.BlockSpec((k, tn), lambda i, j: (0, j))
    return pl.pallas_call(
        body, grid=(m // tm, n // tn), in_specs=[a_spec, b_spec],
        out_specs=pl.BlockSpec((tm, tn), lambda i, j: (i, j)),
        out_shape=jax.ShapeDtypeStruct((m, n), out_dtype), name=name,
        compiler_params=_cparams(("parallel", "parallel")))(a, b)


def _rowwise(body, rows, vecs, outs, accs, *, name, tr=256):
    rows = [r if isinstance(r, tuple) else (r, r.shape[1], 0) for r in rows]
    nrows = rows[0][0].shape[0]
    tr = tr if nrows % tr == 0 else nrows
    nr, nv, no = len(rows), len(vecs), len(outs)

    def kern(*refs):
        r_in, v_in = refs[:nr], refs[nr:nr + nv]
        o_refs, a_refs = refs[nr + nv:nr + nv + no], refs[nr + nv + no:]
        res_o, res_a = body([r[...] for r in r_in], [v[...] for v in v_in])
        for o, val in zip(o_refs, res_o):
            o[...] = val.astype(o.dtype)
        if a_refs:
            @pl.when(pl.program_id(0) == 0)
            def _():
                for a in a_refs:
                    a[...] = jnp.zeros(a.shape, a.dtype)
            for a, val in zip(a_refs, res_a):
                a[...] += val

    in_specs = [pl.BlockSpec((tr, w), functools.partial(lambda i, cb: (i, cb), cb=cb)) for _, w, cb in rows]
    in_specs += [pl.BlockSpec(v.shape, functools.partial(lambda i, nd: (0,) * nd, nd=v.ndim)) for v in vecs]
    out_specs = [pl.BlockSpec((tr, w), lambda i: (i, 0)) for w, _ in outs]
    out_specs += [pl.BlockSpec(s, lambda i: (0, 0)) for s in accs]
    out_shape = [jax.ShapeDtypeStruct((nrows, w), dt) for w, dt in outs]
    out_shape += [jax.ShapeDtypeStruct(s, F32) for s in accs]
    res = pl.pallas_call(
        kern, grid=(nrows // tr,), in_specs=in_specs, out_specs=out_specs, out_shape=out_shape, name=name,
        compiler_params=_cparams(("arbitrary",) if accs else ("parallel",)))(*[r[0] for r in rows], *vecs)
    return res


def _colsum(v):
    return jnp.sum(v, axis=0, keepdims=True)


def _adaln_fwd(x, g, sc, sh, name):
    def body(r, v):
        (xv,), (gv, scv, shv) = r, v
        rs = lax.rsqrt(jnp.mean(xv * xv, axis=-1, keepdims=True) + EPS)
        return [(xv * rs) * gv * (1.0 + scv) + shv], []
    return _rowwise(body, [x], [g, sc, sh], [(x.shape[1], BF16)], [], name=name)[0]


def _adaln_bwd(x, dh, dxr, g, sc, name):
    d = x.shape[1]

    def body(r, v):
        (xv, dhv, dxv), (gv, scv) = r, v
        rs = lax.rsqrt(jnp.mean(xv * xv, axis=-1, keepdims=True) + EPS)
        nv = xv * rs
        dn = dhv * (gv * (1.0 + scv))
        dx = rs * (dn - nv * jnp.mean(dn * nv, axis=-1, keepdims=True)) + dxv
        dhn = dhv * nv
        return [dx], [_colsum(dhn * (1.0 + scv)), _colsum(dhn * gv), _colsum(dhv)]
    return _rowwise(body, [x, dh, dxr], [g, sc], [(d, F32)], [(1, d)] * 3, name=name)


def _resid_fwd(x, y, gt, name):
    def body(r, v):
        return [r[0] + v[0] * r[1]], []
    return _rowwise(body, [x, y], [gt], [(x.shape[1], F32)], [], name=name)[0]


def _resid_bwd(dx, y, gt, name):
    d = dx.shape[1]

    def body(r, v):
        return [v[0] * r[0]], [_colsum(r[0] * r[1])]
    return _rowwise(body, [dx, y], [gt], [(d, BF16)], [(1, d)], name=name)


def _swiglu_fwd(u, name):
    f = u.shape[1] // 2

    def body(r, v):
        uv = r[0]
        return [_silu(uv[:, :f]) * uv[:, f:]], []
    return _rowwise(body, [u], [], [(f, BF16)], [], name=name)[0]


def _swiglu_bwd(u, da, name):
    f = u.shape[1] // 2

    def body(r, v):
        uv, dav = r
        gate, up = uv[:, :f], uv[:, f:]
        return [jnp.concatenate([dav * up * _dsilu(gate), dav * _silu(gate)], axis=1)], []
    return _rowwise(body, [u, da], [], [(2 * f, BF16)], [], name=name)[0]


def _loss_fwd_bwd(y, tgt, name):
    d = y.shape[1]

    def body(r, v):
        err = r[0] - r[1]
        part = jnp.sum(jnp.sum(err * err, axis=1, keepdims=True), axis=0, keepdims=True) * (0.5 / d)
        return [err * (1.0 / d)], [jnp.broadcast_to(part, (1, LANES))]
    dy, acc = _rowwise(body, [y, tgt], [], [(d, F32)], [(1, LANES)], name=name)
    return acc[0, 0], dy


def _head_means(v):
    row = lax.broadcasted_iota(jnp.int32, (LANES, LANES), 0)
    col = lax.broadcasted_iota(jnp.int32, (LANES, LANES), 1)
    same = ((row // SB_HEAD_DIM) == (col // SB_HEAD_DIM)).astype(F32)
    parts = [jnp.dot(v[:, g * LANES:(g + 1) * LANES], same, precision=HI, preferred_element_type=F32)
             for g in range(v.shape[1] // LANES)]
    return jnp.concatenate(parts, axis=1) * (1.0 / SB_HEAD_DIM)


def _sb_norm_fwd(qkv, gq, gk, name):
    d = qkv.shape[1] // 3

    def body(r, v):
        return [x * lax.rsqrt(_head_means(x * x) + EPS) * g for x, g in zip(r, v)], []
    return _rowwise(body, [(qkv, d, 0), (qkv, d, 1)], [gq, gk], [(d, F32), (d, F32)], [], name=name)


def _sb_norm_bwd(qkv, dqn, dkn, dv, gq, gk, name):
    d = qkv.shape[1] // 3

    def body(r, v):
        outs, accs = [], []
        for x, dy, g in ((r[0], r[2], v[0]), (r[1], r[3], v[1])):
            rs = lax.rsqrt(_head_means(x * x) + EPS)
            nv = x * rs
            dn = dy * g
            outs.append(rs * (dn - nv * _head_means(dn * nv)))
            accs.append(_colsum(dy * nv))
        return [jnp.concatenate(outs + [r[4]], axis=1)], accs
    return _rowwise(body, [(qkv, d, 0), (qkv, d, 1), dqn, dkn, dv], [gq, gk], [(3 * d, BF16)], [(1, d), (1, d)],
                    name=name)


def _dn_post_fwd(o, z, g, name):
    def body(r, v):
        ov, zv = r
        rs = lax.rsqrt(jnp.mean(ov * ov, axis=-1, keepdims=True) + EPS)
        return [ov * rs * v[0] * _silu(zv)], []
    return _rowwise(body, [o, z], [g], [(o.shape[1], BF16)], [], name=name, tr=2048)[0]


def _dn_post_bwd(o, z, don, g, name):
    d = o.shape[1]

    def body(r, v):
        ov, zv, dv = r
        rs = lax.rsqrt(jnp.mean(ov * ov, axis=-1, keepdims=True) + EPS)
        nv = ov * rs
        s = _silu(zv)
        dn = dv * v[0] * s
        do = rs * (dn - nv * jnp.mean(dn * nv, axis=-1, keepdims=True))
        dz = dv * nv * v[0] * _dsilu(zv)
        return [do, dz], [_colsum(dv * nv * s)]
    return _rowwise(body, [o, z, don], [g], [(d, F32), (d, BF16)], [(1, d)], name=name, tr=2048)


def _chunk_tri(tr, upper):
    row = lax.broadcasted_iota(jnp.int32, (tr, tr), 0)
    col = lax.broadcasted_iota(jnp.int32, (tr, tr), 1)
    same = (row // DN_CHUNK) == (col // DN_CHUNK)
    return (same & ((row <= col) if upper else (row >= col))).astype(F32)


def _dn_gate_fwd(proj, colblk, a_log, dt_bias, name):
    def body(r, v):
        ab = r[0]
        a, b = ab[:, :LANES], ab[:, LANES:]
        g = -jnp.exp(v[0]) * _softplus(a + v[1])
        big_g = jnp.dot(_chunk_tri(g.shape[0], False), g, precision=HI, preferred_element_type=F32)
        return [big_g, _sigmoid(b)], []
    return _rowwise(body, [(proj, 2 * LANES, colblk)], [a_log, dt_bias], [(LANES, F32), (LANES, F32)], [], name=name)


def _dn_gate_bwd(proj, colblk, d_big_g, dbeta, a_log, dt_bias, name):
    def body(r, v):
        ab, dgc, dbt = r
        a, b = ab[:, :LANES], ab[:, LANES:]
        dg = jnp.dot(_chunk_tri(dgc.shape[0], True), dgc, precision=HI, preferred_element_type=F32)
        na = -jnp.exp(v[0])
        pre = a + v[1]
        da = dg * na * _sigmoid(pre)
        beta = _sigmoid(b)
        db = dbt * beta * (1.0 - beta)
        return [jnp.concatenate([da, db], axis=1)], [_colsum(dg * na * _softplus(pre)), _colsum(da)]
    return _rowwise(body, [(proj, 2 * LANES, colblk), d_big_g, dbeta], [a_log, dt_bias],
                    [(2 * LANES, BF16)], [(1, LANES), (1, LANES)], name=name)


def _shift_rows(x, s):
    if s == 0:
        return x
    t = x.shape[0]
    row = lax.broadcasted_iota(jnp.int32, x.shape, 0)
    rolled = pltpu.roll(x, s % t, axis=0)
    return jnp.where((row >= s) if s > 0 else (row < t + s), rolled, 0.0)


def _rnd(x):
    return x.astype(BF16).astype(F32)


def _conv(x, c_ref):
    c = x * _rnd(c_ref[DN_CONV - 1:DN_CONV, :])
    for s in range(1, DN_CONV):
        c = c + _shift_rows(x, s) * _rnd(c_ref[DN_CONV - 1 - s:DN_CONV - s, :])
    return c


def _dn_prep_fwd(proj, conv_w, heads, name):
    t = proj.shape[0]
    d = heads * DN_HEAD_DIM

    def body(xq, xk, xv, cq, ck, cv, q_ref, k_ref, v_ref):
        for x_ref, c_ref, o_ref, norm in ((xq, cq, q_ref, True), (xk, ck, k_ref, True), (xv, cv, v_ref, False)):
            y = _silu(_conv(_rnd(x_ref[...]), c_ref))
            if norm:
                y = y * lax.rsqrt(jnp.sum(y * y, axis=-1, keepdims=True) + EPS)
            o_ref[...] = y

    xs = [pl.BlockSpec((t, DN_HEAD_DIM), functools.partial(lambda h, o: (0, h + o), o=o * heads)) for o in range(3)]
    cs = [pl.BlockSpec((DN_CONV, DN_HEAD_DIM), functools.partial(lambda h, o: (0, h + o), o=o * heads)) for o in range(3)]
    return pl.pallas_call(
        body, grid=(heads,), in_specs=xs + cs,
        out_specs=[pl.BlockSpec((t, DN_HEAD_DIM), lambda h: (0, h))] * 3,
        out_shape=[jax.ShapeDtypeStruct((t, d), F32)] * 3, name=name,
        compiler_params=_cparams(("parallel",)))(proj, proj, proj, conv_w, conv_w, conv_w)


def _dn_prep_bwd(proj, conv_w, dq, dk, dv, heads, name):
    t = proj.shape[0]
    d = heads * DN_HEAD_DIM

    def body(xq, xk, xv, cq, ck, cv, gq, gk, gv, oq, ok, ov, wq, wk, wv):
        for x_ref, c_ref, g_ref, o_ref, w_ref, norm in ((xq, cq, gq, oq, wq, True), (xk, ck, gk, ok, wk, True),
                                                        (xv, cv, gv, ov, wv, False)):
            x, dy = _rnd(x_ref[...]), g_ref[...]
            c = _conv(x, c_ref)
            if norm:
                y = _silu(c)
                rs = lax.rsqrt(jnp.sum(y * y, axis=-1, keepdims=True) + EPS)
                yn = y * rs
                dy = rs * (dy - yn * jnp.sum(dy * yn, axis=-1, keepdims=True))
            dc = _rnd(dy * _dsilu(c))
            dx = dc * _rnd(c_ref[DN_CONV - 1:DN_CONV, :])
            w_ref[DN_CONV - 1:DN_CONV, :] = _colsum(dc * x)
            for s in range(1, DN_CONV):
                dx = dx + _shift_rows(dc, -s) * _rnd(c_ref[DN_CONV - 1 - s:DN_CONV - s, :])
                w_ref[DN_CONV - 1 - s:DN_CONV - s, :] = _colsum(dc * _shift_rows(x, s))
            o_ref[...] = dx.astype(o_ref.dtype)

    xs = [pl.BlockSpec((t, DN_HEAD_DIM), functools.partial(lambda h, o: (0, h + o), o=o * heads)) for o in range(3)]
    cs = [pl.BlockSpec((DN_CONV, DN_HEAD_DIM), functools.partial(lambda h, o: (0, h + o), o=o * heads)) for o in range(3)]
    hs = pl.BlockSpec((t, DN_HEAD_DIM), lambda h: (0, h))
    ws = pl.BlockSpec((DN_CONV, DN_HEAD_DIM), lambda h: (0, h))
    return pl.pallas_call(
        body, grid=(heads,), in_specs=xs + cs + [hs] * 3, out_specs=[hs] * 3 + [ws] * 3,
        out_shape=[jax.ShapeDtypeStruct((t, d), BF16)] * 3 + [jax.ShapeDtypeStruct((DN_CONV, d), F32)] * 3, name=name,
        compiler_params=_cparams(("parallel",)))(proj, proj, proj, conv_w, conv_w, conv_w, dq, dk, dv)


def _bmm(a, b, prec=None):
    return lax.dot_general(a, b, (((2,), (1,)), ((0,), (0,))), precision=prec, preferred_element_type=F32)


def _bmm_nt(a, b, prec=None):
    return lax.dot_general(a, b, (((2,), (2,)), ((0,), (0,))), precision=prec, preferred_element_type=F32)


def _bmm_tn(a, b, prec=None):
    return lax.dot_general(a, b, (((1,), (1,)), ((0,), (0,))), precision=prec, preferred_element_type=F32)


def _dn_local(qg, kg, vg, gc, gr, bt):
    c = qg.shape[1]
    row = lax.broadcasted_iota(jnp.int32, (c, c), 0)
    col = lax.broadcasted_iota(jnp.int32, (c, c), 1)
    incl, strict = (row >= col)[None], (row > col)[None]
    decay = jnp.where(incl, jnp.exp(jnp.where(incl, gc - gr, 0.0)), 0.0)
    kb = kg * bt
    vb = vg * bt
    m = _bmm_nt(kb.astype(BF16), kg.astype(BF16))
    a = jnp.where(strict, m * decay, 0.0)
    bp = -a
    tm = jnp.where((row == col)[None], 1.0, 0.0) + bp
    steps = max(1, (c - 1).bit_length()) - 1
    for _ in range(steps):
        bp = _bmm(bp, bp, HI)
        tm = tm + _bmm(tm, bp, HI)
    eg = jnp.exp(gc)
    glast = gc[:, c - 1:c, :]
    erel = jnp.exp(glast - gc)
    kbg = kb * eg
    qm = _bmm_nt(qg.astype(BF16), kg.astype(BF16))
    return dict(decay=decay, strict=strict, kb=kb, vb=vb, m=m, tm=tm, eg=eg, erel=erel, kbg=kbg, qm=qm,
                u=_bmm(tm, vb, HI), w=_bmm(tm, kbg, HI), qd=qg * eg, kt=kg * erel, gl=jnp.exp(glast))


def _dot(a, b):
    return jnp.dot(a.astype(BF16), b.astype(BF16), preferred_element_type=F32)


def _dot_nt(a, b):
    return lax.dot_general(a.astype(BF16), b.astype(BF16), (((1,), (1,)), ((), ())), preferred_element_type=F32)


def _dot_tn(a, b):
    return lax.dot_general(a.astype(BF16), b.astype(BF16), (((0,), (0,)), ((), ())), preferred_element_type=F32)


def _dn_chunk_specs(t, heads):
    n = t // DN_CHUNK
    hs = pl.BlockSpec((t, DN_HEAD_DIM), lambda h: (0, h))
    gcs = pl.BlockSpec((1, n, DN_CHUNK, 1), lambda h: (h, 0, 0, 0))
    grs = pl.BlockSpec((1, n, 1, DN_CHUNK), lambda h: (h, 0, 0, 0))
    ss = pl.BlockSpec((1, n, DN_HEAD_DIM, DN_HEAD_DIM), lambda h: (h, 0, 0, 0))
    return n, hs, gcs, grs, ss


def _dn_chunk_fwd(q, k, v, gc, gr, bc, name):
    t, d = q.shape
    heads = d // DN_HEAD_DIM
    c, dk = DN_CHUNK, DN_HEAD_DIM
    n, hs, gcs, grs, ss = _dn_chunk_specs(t, heads)
    nb = min(DN_GROUP, n)
    scale = dk ** -0.5

    def body(q_ref, k_ref, v_ref, gc_ref, gr_ref, b_ref, o_ref, s_ref, u_s, w_s, at_s, qd_s, kt_s, gl_s):
        def group(gi, carry):
            r0 = pl.multiple_of(gi * (nb * c), nb * c)
            n0 = gi * nb
            ld = lambda ref: ref[pl.ds(r0, nb * c), :].reshape(nb, c, dk)
            loc = _dn_local(ld(q_ref) * scale, ld(k_ref), ld(v_ref), gc_ref[0, pl.ds(n0, nb)],
                            gr_ref[0, pl.ds(n0, nb)], b_ref[0, pl.ds(n0, nb)])
            u_s[pl.ds(n0, nb)] = loc["u"]
            w_s[pl.ds(n0, nb)] = loc["w"]
            at_s[pl.ds(n0, nb)] = loc["qm"] * loc["decay"]
            qd_s[pl.ds(n0, nb)] = loc["qd"]
            kt_s[pl.ds(n0, nb)] = loc["kt"]
            gl_s[pl.ds(n0, nb)] = loc["gl"]
            return carry
        lax.fori_loop(0, n // nb, group, 0)

        def chunk(i, s):
            s_ref[0, i] = s
            vnew = u_s[i] - _dot(w_s[i], s)
            o = _dot(qd_s[i], s) + _dot(at_s[i], vnew)
            o_ref[pl.ds(pl.multiple_of(i * c, c), c), :] = o
            return s * gl_s[i] + _dot_tn(kt_s[i], vnew)
        lax.fori_loop(0, n, chunk, jnp.zeros((dk, dk), F32))

    scratch = [pltpu.VMEM((n, c, dk), F32), pltpu.VMEM((n, c, dk), F32), pltpu.VMEM((n, c, c), F32),
               pltpu.VMEM((n, c, dk), F32), pltpu.VMEM((n, c, dk), F32), pltpu.VMEM((n, 1, 1), F32)]
    return pl.pallas_call(
        body, grid=(heads,), in_specs=[hs, hs, hs, gcs, grs, gcs], out_specs=[hs, ss],
        out_shape=[jax.ShapeDtypeStruct((t, d), F32), jax.ShapeDtypeStruct((heads, n, dk, dk), F32)],
        scratch_shapes=scratch, name=name, compiler_params=_cparams(("parallel",)))(q, k, v, gc, gr, bc)


def _dn_chunk_bwd(q, k, v, gc, gr, bc, s_all, do, name):
    t, d = q.shape
    heads = d // DN_HEAD_DIM
    c, dk = DN_CHUNK, DN_HEAD_DIM
    n, hs, gcs, grs, ss = _dn_chunk_specs(t, heads)
    nb = min(DN_GROUP, n)
    scale = dk ** -0.5

    def body(q_ref, k_ref, v_ref, gc_ref, gr_ref, b_ref, s_ref, do_ref,
             dq_ref, dk_ref, dv_ref, dgc_ref, dgr_ref, db_ref,
             u_s, w_s, att_s, qd_s, kt_s, gl_s, du_s, dw_s, dat_s, dqd_s, dkt_s, dgl_s):
        def load_group(gi):
            r0 = pl.multiple_of(gi * (nb * c), nb * c)
            n0 = gi * nb
            ld = lambda ref: ref[pl.ds(r0, nb * c), :].reshape(nb, c, dk)
            qg, kg, vg = ld(q_ref) * scale, ld(k_ref), ld(v_ref)
            gcv, grv, bt = gc_ref[0, pl.ds(n0, nb)], gr_ref[0, pl.ds(n0, nb)], b_ref[0, pl.ds(n0, nb)]
            return r0, n0, qg, kg, vg, gcv, grv, bt, _dn_local(qg, kg, vg, gcv, grv, bt)

        def group_a(gi, carry):
            _, n0, qg, kg, _, gcv, grv, _, loc = load_group(gi)
            row = lax.broadcasted_iota(jnp.int32, (c, c), 0)
            col = lax.broadcasted_iota(jnp.int32, (c, c), 1)
            upper = (col >= row)[None]
            decay_t = jnp.where(upper, jnp.exp(jnp.where(upper, grv - gcv, 0.0)), 0.0)
            u_s[pl.ds(n0, nb)] = loc["u"]
            w_s[pl.ds(n0, nb)] = loc["w"]
            att_s[pl.ds(n0, nb)] = _bmm_nt(kg.astype(BF16), qg.astype(BF16)) * decay_t
            qd_s[pl.ds(n0, nb)] = loc["qd"]
            kt_s[pl.ds(n0, nb)] = loc["kt"]
            gl_s[pl.ds(n0, nb)] = loc["gl"]
            return carry
        lax.fori_loop(0, n // nb, group_a, 0)

        def chunk_b(it, ds_next):
            i = n - 1 - it
            s = s_ref[0, i]
            dov = do_ref[pl.ds(pl.multiple_of(i * c, c), c), :]
            w, kt = w_s[i], kt_s[i]
            vnew = u_s[i] - _dot(w, s)
            dvnew = _dot(att_s[i], dov) + _dot(kt, ds_next)
            du_s[i] = dvnew
            dw_s[i] = -_dot_nt(dvnew, s)
            dat_s[i] = _dot_nt(dov, vnew)
            dqd_s[i] = _dot_nt(dov, s)
            dkt_s[i] = _dot_nt(vnew, ds_next)
            dgl_s[i] = jnp.sum(jnp.sum(ds_next * s, axis=1, keepdims=True), axis=0, keepdims=True)
            return ds_next * gl_s[i] + _dot_tn(qd_s[i], dov) - _dot_tn(w, dvnew)
        lax.fori_loop(0, n, chunk_b, jnp.zeros((dk, dk), F32))

        def group_c(gi, carry):
            r0, n0, qg, kg, vg, gcv, grv, bt, loc = load_group(gi)
            sl = pl.ds(n0, nb)
            du, dw, dat, dqd, dkt, dgl = du_s[sl], dw_s[sl], dat_s[sl], dqd_s[sl], dkt_s[sl], dgl_s[sl]
            tm, decay, kb, kbg = loc["tm"], loc["decay"], loc["kb"], loc["kbg"]
            dvb = _bmm_tn(tm, du, HI)
            dkbg = _bmm_tn(tm, dw, HI)
            dt = _bmm_nt(du, loc["vb"], HI) + _bmm_nt(dw, kbg, HI)
            da = jnp.where(loc["strict"], -_bmm_tn(tm, _bmm_nt(dt, tm, HI), HI), 0.0)
            dms = (da * decay).astype(BF16)
            dqs = (dat * decay).astype(BF16)
            kgb = kg.astype(BF16)
            dkb = _bmm(dms, kgb) + dkbg * loc["eg"]
            dqt = _bmm(dqs, kgb) + dqd * loc["eg"]
            dkk = _bmm_tn(dms, kb.astype(BF16)) + _bmm_tn(dqs, qg.astype(BF16)) + dkt * loc["erel"] + dkb * bt
            e = (da * loc["m"] + dat * loc["qm"]) * decay
            lsum = lambda x: jnp.sum(x, axis=2, keepdims=True)
            dkt_kt = lsum(dkt * loc["kt"])
            dgcv = lsum(e) + lsum(dqd * loc["qd"]) - dkt_kt + lsum(dkbg * kbg)
            dglast = jnp.sum(dkt_kt, axis=1, keepdims=True) + dgl * loc["gl"]
            rowc = lax.broadcasted_iota(jnp.int32, (1, c, 1), 1)
            dgc_ref[0, sl] = dgcv + jnp.where(rowc == c - 1, dglast, 0.0)
            dgr_ref[0, sl] = -jnp.sum(e, axis=1, keepdims=True)
            db_ref[0, sl] = lsum(dkb * kg) + lsum(dvb * vg)
            rows = pl.ds(r0, nb * c)
            dq_ref[rows, :] = (dqt * scale).reshape(nb * c, dk)
            dk_ref[rows, :] = dkk.reshape(nb * c, dk)
            dv_ref[rows, :] = (dvb * bt).reshape(nb * c, dk)
            return carry
        lax.fori_loop(0, n // nb, group_c, 0)

    big = lambda: pltpu.VMEM((n, c, dk), F32)
    sq = lambda: pltpu.VMEM((n, c, c), F32)
    one = lambda: pltpu.VMEM((n, 1, 1), F32)
    scratch = [big(), big(), sq(), big(), big(), one(), big(), big(), sq(), big(), big(), one()]
    return pl.pallas_call(
        body, grid=(heads,), in_specs=[hs, hs, hs, gcs, grs, gcs, ss, hs], out_specs=[hs, hs, hs, gcs, grs, gcs],
        out_shape=[jax.ShapeDtypeStruct((t, d), F32)] * 3 + [
            jax.ShapeDtypeStruct((heads, n, c, 1), F32), jax.ShapeDtypeStruct((heads, n, 1, c), F32),
            jax.ShapeDtypeStruct((heads, n, c, 1), F32)],
        scratch_shapes=scratch, name=name, compiler_params=_cparams(("parallel",)))(q, k, v, gc, gr, bc, s_all, do)


def _dot01(x, m01):
    x1 = x.astype(BF16)
    r1 = x - x1.astype(F32)
    x2 = r1.astype(BF16)
    x3 = (r1 - x2.astype(F32)).astype(BF16)
    dot = lambda a: jnp.dot(a, m01, preferred_element_type=F32)
    return (dot(x1) + dot(x2)) + dot(x3)


def _sb_tile(z, mask, r):
    bk = z.shape[1]
    row = lax.broadcasted_iota(jnp.int32, (bk, bk), 0)
    col = lax.broadcasted_iota(jnp.int32, (bk, bk), 1)
    lsm = -_softplus(z)
    lm = lsm if mask is None else jnp.where(mask, lsm, 0.0)
    cs = _dot01(lm, (row > col).astype(BF16))
    ls = z + lsm
    a = jnp.exp(ls + cs + r)
    if mask is not None:
        a = jnp.where(mask, a, 0.0)
    return a, jnp.exp(ls), r + jnp.sum(lm, axis=1, keepdims=True)


def _sb_sweep(bq, bk, qi, visit, carry):
    nd = bq // bk
    row = lax.broadcasted_iota(jnp.int32, (2 * bq, bk), 0)
    row = jnp.where(row >= bq, row - bq, row)
    col = lax.broadcasted_iota(jnp.int32, (2 * bq, bk), 1)
    for dd in reversed(range(nd)):
        carry = visit(qi * nd + dd, (col + dd * bk) < row, carry)
    return lax.fori_loop(0, qi * nd, lambda it, c: visit(qi * nd - 1 - it, None, c), carry)


def _stack_heads(a, h0):
    return jnp.concatenate([jnp.where(h0, a, 0.0), jnp.where(h0, 0.0, a)], axis=0).astype(BF16)


def _side_by_side(a, bq):
    return jnp.concatenate([a[:bq], a[bq:]], axis=1)


def _sb_specs(t, d, bq, vcol):
    qs = pl.BlockSpec((bq, LANES), lambda g, i: (i, g))
    ks = pl.BlockSpec((t, LANES), lambda g, i: (0, g))
    vs = pl.BlockSpec((t, LANES), lambda g, i: (0, g + vcol))
    return qs, ks, vs


def _sb_fwd(qn, kn, qkv, name):
    t, d = qn.shape
    bq, bk = min(SB_QBLOCK, t), min(SB_BLOCK, t)
    scale = SB_HEAD_DIM ** -0.5
    nt = (((1,), (1,)), ((), ()))

    def body(q_ref, k_ref, v_ref, o_ref):
        qi = pl.program_id(1)
        h0 = lax.broadcasted_iota(jnp.int32, (1, LANES), 1) < SB_HEAD_DIM
        q2 = _stack_heads(q_ref[...] * scale, h0)

        def visit(j, mask, carry):
            acc, r = carry
            rows = pl.ds(pl.multiple_of(j * bk, bk), bk)
            z = lax.dot_general(q2, k_ref[rows, :].astype(BF16), nt, preferred_element_type=F32)
            a, _, r = _sb_tile(z, mask, r)
            a2 = _side_by_side(a.astype(BF16), bq)
            return acc + jnp.dot(a2, _stack_heads(v_ref[rows, :], h0), preferred_element_type=F32), r
        acc, _ = _sb_sweep(bq, bk, qi, visit, (jnp.zeros((bq, LANES), F32), jnp.zeros((2 * bq, 1), F32)))
        o_ref[...] = acc.astype(o_ref.dtype)

    qs, ks, vs = _sb_specs(t, d, bq, 2 * d // LANES)
    return pl.pallas_call(
        body, grid=(d // LANES, t // bq), in_specs=[qs, ks, vs], out_specs=qs,
        out_shape=jax.ShapeDtypeStruct((t, d), BF16), name=name,
        compiler_params=_cparams(("parallel", "parallel")))(qn, kn, qkv)


def _sb_bwd(qn, kn, qkv, do, name):
    t, d = qn.shape
    bq, bk = min(SB_QBLOCK, t), min(SB_BLOCK, t)
    scale = SB_HEAD_DIM ** -0.5
    nt = (((1,), (1,)), ((), ()))

    def body(q_ref, k_ref, v_ref, do_ref, dq_ref, dk_ref, dv_ref, p_s, sg_s):
        qi = pl.program_id(1)

        @pl.when(qi == 0)
        def _():
            dk_ref[...] = jnp.zeros(dk_ref.shape, F32)
            dv_ref[...] = jnp.zeros(dv_ref.shape, F32)

        h0 = lax.broadcasted_iota(jnp.int32, (1, LANES), 1) < SB_HEAD_DIM
        q2 = _stack_heads(q_ref[...] * scale, h0)
        do2 = _stack_heads(do_ref[...], h0)
        row = lax.broadcasted_iota(jnp.int32, (bk, bk), 0)
        col = lax.broadcasted_iota(jnp.int32, (bk, bk), 1)
        later_incl = (row >= col).astype(BF16)
        zero = jnp.zeros((2 * bq, 1), F32)
        tn = (((0,), (0,)), ((), ()))

        def visit1(j, mask, carry):
            r, sp = carry
            rows = pl.ds(pl.multiple_of(j * bk, bk), bk)
            z = lax.dot_general(q2, k_ref[rows, :].astype(BF16), nt, preferred_element_type=F32)
            da = lax.dot_general(do2, v_ref[rows, :].astype(BF16), nt, preferred_element_type=F32)
            a, sg, r = _sb_tile(z, mask, r)
            p = a * da
            p_s[j] = p
            sg_s[j] = sg
            dv_ref[rows, :] += lax.dot_general(a.astype(BF16), do2, tn, preferred_element_type=F32)
            return r, sp + jnp.sum(p, axis=1, keepdims=True)
        _, total = _sb_sweep(bq, bk, qi, visit1, (zero, zero))

        def visit2(j, mask, carry):
            dq, sp = carry
            rows = pl.ds(pl.multiple_of(j * bk, bk), bk)
            p, sg = p_s[j], sg_s[j]
            pref = total - sp - _dot01(p, later_incl)
            dz = p * (1.0 - sg) - pref * sg
            if mask is not None:
                dz = jnp.where(mask, dz, 0.0)
            dz = dz.astype(BF16)
            dk_ref[rows, :] += lax.dot_general(dz, q2, tn, preferred_element_type=F32)
            dq = dq + jnp.dot(_side_by_side(dz, bq), _stack_heads(k_ref[rows, :], h0), preferred_element_type=F32)
            return dq, sp + jnp.sum(p, axis=1, keepdims=True)
        dq, _ = _sb_sweep(bq, bk, qi, visit2, (jnp.zeros((bq, LANES), F32), zero))
        dq_ref[...] = dq * scale

    qs, ks, vs = _sb_specs(t, d, bq, 2 * d // LANES)
    shp = jax.ShapeDtypeStruct((t, d), F32)
    scratch = [pltpu.VMEM((t // bk, 2 * bq, bk), F32), pltpu.VMEM((t // bk, 2 * bq, bk), F32)]
    return pl.pallas_call(
        body, grid=(d // LANES, t // bq), in_specs=[qs, ks, vs, qs], out_specs=[qs, ks, ks],
        out_shape=[shp, shp, shp], scratch_shapes=scratch, name=name,
        compiler_params=_cparams(("parallel", "arbitrary")))(qn, kn, qkv, do)


def _adamw(w, g, m, v, name):
    shape = w.shape
    two_d = lambda a: a.reshape(-1, shape[-1])
    c1 = 1.0 - ADAM_B1 ** ADAM_STEP
    c2 = 1.0 - ADAM_B2 ** ADAM_STEP

    def body(r, _):
        wv, gv, mv, vv = r
        mn = ADAM_B1 * mv + (1.0 - ADAM_B1) * gv
        vn = ADAM_B2 * vv + (1.0 - ADAM_B2) * (gv * gv)
        delta = -ADAM_LR * ((mn / c1) / (jnp.sqrt(vn / c2) + ADAM_EPS) + ADAM_WD * wv)
        return [delta, mn, vn], []
    width = shape[-1]
    res = _rowwise(body, [two_d(w), two_d(g), two_d(m), two_d(v)], [], [(width, F32)] * 3, [], name=name)
    return [r.reshape(shape) for r in res]


def _dn_layer_fwd(h1, w, conv_w, a_log, dt_bias, onorm_g, tag):
    t, d = h1.shape
    heads = d // DN_HEAD_DIM
    n = t // DN_CHUNK
    proj = _mm(h1, w["all"], name=tag + "_proj")
    qn, kn, vv = _dn_prep_fwd(proj, conv_w, heads, tag + "_prep")
    big_g, beta = _dn_gate_fwd(proj, 4 * d // (2 * LANES), a_log, dt_bias, tag + "_gate")
    gt_ = big_g[:, :heads].T.reshape(heads, n, DN_CHUNK)
    gc, gr = gt_[..., None], gt_[:, :, None, :]
    bc = beta[:, :heads].T.reshape(heads, n, DN_CHUNK)[..., None]
    o, s_all = _dn_chunk_fwd(qn, kn, vv, gc, gr, bc, tag + "_chunk")
    z = proj[:, 3 * d:4 * d].reshape(t * heads, DN_HEAD_DIM)
    on = _dn_post_fwd(o.reshape(t * heads, DN_HEAD_DIM), z, onorm_g, tag + "_post").reshape(t, d)
    y = _mm(on, w["out"], name=tag + "_out")
    return y, dict(proj=proj, qn=qn, kn=kn, v=vv, gc=gc, gr=gr, bc=bc, o=o, s_all=s_all, z=z, on=on)


def _dn_layer_bwd(dy, h1, w, conv_w, a_log, dt_bias, onorm_g, sv, tag):
    t, d = h1.shape
    heads = d // DN_HEAD_DIM
    don = _mm(dy, w["out"], tb=True, name=tag + "_dout")
    g_out = _mm(sv["on"], dy, ta=True, name=tag + "_gwout")
    do, dz, g_on = _dn_post_bwd(sv["o"].reshape(t * heads, DN_HEAD_DIM), sv["z"],
                                don.reshape(t * heads, DN_HEAD_DIM), onorm_g, tag + "_dpost")
    dq, dk, dv, dgc, dgr, dbc = _dn_chunk_bwd(sv["qn"], sv["kn"], sv["v"], sv["gc"], sv["gr"], sv["bc"], sv["s_all"],
                                              do.reshape(t, d), tag + "_dchunk")
    pad = lambda a: jnp.pad(a.reshape(heads, t).T, ((0, 0), (0, LANES - heads)))
    d_big_g = pad(dgc) + pad(dgr)
    dab, g_alog, g_dt = _dn_gate_bwd(sv["proj"], 4 * d // (2 * LANES), d_big_g, pad(dbc), a_log, dt_bias, tag + "_dgate")
    dxq, dxk, dxv, wq, wk, wv = _dn_prep_bwd(sv["proj"], conv_w, dq, dk, dv, heads, tag + "_dprep")
    dproj = jnp.concatenate([dxq, dxk, dxv, dz.reshape(t, d), dab], axis=1)
    dh1 = _mm(dproj, w["all"], tb=True, name=tag + "_dh")
    g_all = _mm(h1, dproj, ta=True, name=tag + "_gwin")
    grads = dict(w_all=g_all, w_out=g_out, conv_w=jnp.concatenate([wq, wk, wv], axis=1), a_log=g_alog, dt_bias=g_dt,
                 onorm_g=g_on)
    return dh1, grads


def _sb_layer_fwd(h1, w, q_g, k_g, tag):
    t, d = h1.shape
    heads = d // SB_HEAD_DIM
    qkv = _mm(h1, w["qkv"], name=tag + "_proj")
    gq, gk = jnp.tile(q_g, (1, heads)), jnp.tile(k_g, (1, heads))
    qn, kn = _sb_norm_fwd(qkv, gq, gk, tag + "_norm")
    o = _sb_fwd(qn, kn, qkv, tag + "_attn")
    y = _mm(o, w["out"], name=tag + "_out")
    return y, dict(qkv=qkv, qn=qn, kn=kn, o=o, gq=gq, gk=gk)


def _sb_layer_bwd(dy, h1, w, q_g, k_g, sv, tag):
    t, d = h1.shape
    heads = d // SB_HEAD_DIM
    do = _mm(dy, w["out"], tb=True, name=tag + "_dout")
    g_out = _mm(sv["o"], dy, ta=True, name=tag + "_gwout")
    dqn, dkn, dv = _sb_bwd(sv["qn"], sv["kn"], sv["qkv"], do, tag + "_dattn")
    dqkv, g_q, g_k = _sb_norm_bwd(sv["qkv"], dqn, dkn, dv, sv["gq"], sv["gk"], tag + "_dnorm")
    fold = lambda g: jnp.sum(g.reshape(heads, SB_HEAD_DIM), axis=0, keepdims=True)
    dh1 = _mm(dqkv, w["qkv"], tb=True, name=tag + "_dh")
    g_qkv = _mm(h1, dqkv, ta=True, name=tag + "_gwin")
    return dh1, dict(w_qkv=g_qkv, w_out=g_out, q_norm_g=fold(g_q), k_norm_g=fold(g_k))


def _local_step(x, tgt, mod, norm1_g, norm2_g, dn, sb, ffn):
    depth = mod.shape[0]
    d = x.shape[1]
    saved = []
    for i in range(depth):
        mv = [mod[i:i + 1, j * d:(j + 1) * d] for j in range(N_MOD)]
        sh1, sc1, gt1, sh2, sc2, gt2 = mv
        tag = "l%d" % i
        h1 = _adaln_fwd(x, norm1_g[i:i + 1], sc1, sh1, tag + "_ln1")
        if i % 2 == 0:
            p = dn[i // 2]
            y, sv = _dn_layer_fwd(h1, p, p["conv_w"], p["a_log"], p["dt_bias"], p["onorm_g"], tag + "_dn")
        else:
            p = sb[i // 2]
            y, sv = _sb_layer_fwd(h1, p, p["q_g"], p["k_g"], tag + "_sb")
        x1 = _resid_fwd(x, y, gt1, tag + "_res1")
        h2 = _adaln_fwd(x1, norm2_g[i:i + 1], sc2, sh2, tag + "_ln2")
        u = _mm(h2, ffn[i]["w_in"], name=tag + "_ffn_in")
        a = _swiglu_fwd(u, tag + "_swiglu")
        y2 = _mm(a, ffn[i]["w_out"], name=tag + "_ffn_out")
        x2 = _resid_fwd(x1, y2, gt2, tag + "_res2")
        saved.append(dict(x0=x, h1=h1, y=y, mix=sv, x1=x1, h2=h2, u=u, a=a, y2=y2))
        x = x2

    loss, dx = _loss_fwd_bwd(x, tgt, "loss")

    dmod, dn1, dn2 = [None] * depth, [None] * depth, [None] * depth
    g_dn, g_sb, g_ffn = [None] * len(dn), [None] * len(sb), [None] * depth
    for i in reversed(range(depth)):
        s = saved[i]
        mv = [mod[i:i + 1, j * d:(j + 1) * d] for j in range(N_MOD)]
        sh1, sc1, gt1, sh2, sc2, gt2 = mv
        tag = "l%d" % i
        dy2, dgt2 = _resid_bwd(dx, s["y2"], gt2, tag + "_dres2")
        da = _mm(dy2, ffn[i]["w_out"], tb=True, name=tag + "_dffn_a")
        g_wout = _mm(s["a"], dy2, ta=True, name=tag + "_gffn_out")
        du = _swiglu_bwd(s["u"], da, tag + "_dswiglu")
        dh2 = _mm(du, ffn[i]["w_in"], tb=True, name=tag + "_dffn_h")
        g_win = _mm(s["h2"], du, ta=True, name=tag + "_gffn_in")
        g_ffn[i] = dict(w_in=g_win, w_out=g_wout)
        dx, dg2, dsc2, dsh2 = _adaln_bwd(s["x1"], dh2, dx, norm2_g[i:i + 1], sc2, tag + "_dln2")
        dy, dgt1 = _resid_bwd(dx, s["y"], gt1, tag + "_dres1")
        if i % 2 == 0:
            p = dn[i // 2]
            dh1, g_dn[i // 2] = _dn_layer_bwd(dy, s["h1"], p, p["conv_w"], p["a_log"], p["dt_bias"], p["onorm_g"],
                                               s["mix"], tag + "_dn")
        else:
            p = sb[i // 2]
            dh1, g_sb[i // 2] = _sb_layer_bwd(dy, s["h1"], p, p["q_g"], p["k_g"], s["mix"], tag + "_sb")
        dx, dg1, dsc1, dsh1 = _adaln_bwd(s["x0"], dh1, dx, norm1_g[i:i + 1], sc1, tag + "_dln1")
        dmod[i] = jnp.concatenate([dsh1, dsc1, dgt1, dsh2, dsc2, dgt2], axis=1)
        dn1[i], dn2[i] = dg1, dg2
    return (loss, dx, jnp.concatenate(dmod, axis=0), jnp.concatenate(dn1, axis=0), jnp.concatenate(dn2, axis=0),
            g_dn, g_sb, g_ffn)


def _axes():
    return lax.axis_index("x"), lax.axis_index("y"), lax.axis_index("c")


def _remote(src, dst, send_sem, recv_sem, dev):
    return pltpu.make_async_remote_copy(src_ref=src, dst_ref=dst, send_sem=send_sem, recv_sem=recv_sem,
                                        device_id=dev, device_id_type=MESH)


def _other_chips(x, y):
    return [(1 - x, y), (x, 1 - y), (1 - x, 1 - y)]


def _allgather_small(v, name):
    m, n = v.shape

    def body(x_ref, out_ref, send_sems, recv_sems, local_sem):
        x, y, c = _axes()
        me, sibling = (x, y, c), (x, y, 1 - c)
        chips = _other_chips(x, y)

        def rows(px, py, pc):
            return out_ref.at[pl.ds((4 * px + 2 * py + pc) * m, m), :]

        def copy(k, block, to, src=None):
            return _remote(rows(*block) if src is None else src, rows(*block), send_sems.at[k], recv_sems.at[k], to)

        mine = pltpu.make_async_copy(x_ref, rows(*me), local_sem)
        mine.start()
        first = [copy(0, me, sibling, src=x_ref)]
        first += [copy(1 + j, me, (*chip, c), src=x_ref) for j, chip in enumerate(chips)]
        for cp in first:
            cp.start()
        passed = [copy(4 + j, (*chip, c), sibling) for j, chip in enumerate(chips)]
        for j, chip in enumerate(chips):
            copy(1 + j, (*chip, c), me).wait_recv()
            passed[j].start()
        copy(0, sibling, me).wait_recv()
        for j, chip in enumerate(chips):
            copy(4 + j, (*chip, 1 - c), me).wait_recv()
        for cp in first + passed:
            cp.wait_send()
        mine.wait()

    return pl.pallas_call(
        body, out_shape=jax.ShapeDtypeStruct((8 * m, n), v.dtype),
        in_specs=[pl.BlockSpec(memory_space=pltpu.VMEM)], out_specs=pl.BlockSpec(memory_space=pltpu.VMEM),
        scratch_shapes=[pltpu.SemaphoreType.DMA((7,)), pltpu.SemaphoreType.DMA((7,)), pltpu.SemaphoreType.DMA],
        name=name, compiler_params=pltpu.CompilerParams(vmem_limit_bytes=VMEM_LIMIT))(v)


def _allgather_weights(packed, name):
    _, rh, w = packed.shape

    def body(p_ref, out_ref, send_sems, recv_sems, local_sem):
        x, y, c = _axes()
        sibling = (x, y, 1 - c)
        chips = _other_chips(x, y)

        def blk(cx, cy, h):
            return out_ref.at[2 * cx + cy, h]

        mine = pltpu.make_async_copy(p_ref, out_ref.at[2 * x + y], local_sem)
        mine.start()
        first = [_remote(p_ref.at[c], blk(x, y, c), send_sems.at[j], recv_sems.at[j], (*chip, c))
                 for j, chip in enumerate(chips)]
        for cp in first:
            cp.start()
        passed = [_remote(blk(*chip, c), blk(*chip, c), send_sems.at[3 + j], recv_sems.at[3 + j], sibling)
                  for j, chip in enumerate(chips)]
        for j, chip in enumerate(chips):
            _remote(p_ref.at[c], blk(*chip, c), send_sems.at[j], recv_sems.at[j], sibling).wait_recv()
            passed[j].start()
        for j, chip in enumerate(chips):
            _remote(p_ref.at[c], blk(*chip, 1 - c), send_sems.at[3 + j], recv_sems.at[3 + j], sibling).wait_recv()
        for cp in first + passed:
            cp.wait_send()
        mine.wait()

    hbm = pl.BlockSpec(memory_space=pltpu.HBM)
    return pl.pallas_call(
        body, out_shape=jax.ShapeDtypeStruct((4, 2, rh, w), packed.dtype), in_specs=[hbm], out_specs=hbm,
        scratch_shapes=[pltpu.SemaphoreType.DMA((6,)), pltpu.SemaphoreType.DMA((6,)), pltpu.SemaphoreType.DMA],
        name=name)(packed)


def _sibling_swap(v, pick_other_half, name):
    shape = v.shape[1:] if pick_other_half else v.shape

    def body(v_ref, out_ref, send_sem, recv_sem):
        x, y, c = _axes()
        cp = _remote(v_ref.at[1 - c] if pick_other_half else v_ref, out_ref, send_sem, recv_sem, (x, y, 1 - c))
        cp.start()
        cp.wait()

    hbm = pl.BlockSpec(memory_space=pltpu.HBM)
    return pl.pallas_call(
        body, out_shape=jax.ShapeDtypeStruct(shape, v.dtype), in_specs=[hbm], out_specs=hbm,
        scratch_shapes=[pltpu.SemaphoreType.DMA, pltpu.SemaphoreType.DMA], name=name)(v)


def _chip_scatter(v, name):
    def body(v_ref, out_ref, send_sems, recv_sems, local_sem):
        x, y, c = _axes()
        me = 2 * x + y
        chips = _other_chips(x, y)
        mine = pltpu.make_async_copy(v_ref.at[me], out_ref.at[me], local_sem)
        mine.start()
        sends = [_remote(v_ref.at[2 * cx + cy], out_ref.at[me], send_sems.at[j], recv_sems.at[j], (cx, cy, c))
                 for j, (cx, cy) in enumerate(chips)]
        for cp in sends:
            cp.start()
        for j, (cx, cy) in enumerate(chips):
            _remote(v_ref.at[me], out_ref.at[2 * cx + cy], send_sems.at[j], recv_sems.at[j], (cx, cy, c)).wait_recv()
        for cp in sends:
            cp.wait_send()
        mine.wait()

    hbm = pl.BlockSpec(memory_space=pltpu.HBM)
    return pl.pallas_call(
        body, out_shape=jax.ShapeDtypeStruct(v.shape, v.dtype), in_specs=[hbm], out_specs=hbm,
        scratch_shapes=[pltpu.SemaphoreType.DMA((3,)), pltpu.SemaphoreType.DMA((3,)), pltpu.SemaphoreType.DMA],
        name=name)(v)


def _sum_lead(v, name, tr=256):
    k, r, w = v.shape

    def body(v_ref, o_ref):
        acc = v_ref[0].astype(F32)
        for i in range(1, k):
            acc = acc + v_ref[i].astype(F32)
        o_ref[...] = acc

    return pl.pallas_call(
        body, grid=(r // tr,), in_specs=[pl.BlockSpec((k, tr, w), lambda i: (0, i, 0))],
        out_specs=pl.BlockSpec((tr, w), lambda i: (i, 0)), out_shape=jax.ShapeDtypeStruct((r, w), F32), name=name,
        compiler_params=_cparams(("parallel",)))(v)


_BIG = (("dn_w_in", 2), ("dn_w_out", 1), ("sb_w_qkv", 2), ("sb_w_out", 1), ("ffn_w_in", 2), ("ffn_w_out", 1))


def _pack_rows(shard_shapes):
    counts = [l * (r // 2) * cs // PACK_W for l, r, cs in shard_shapes]
    total = sum(counts)
    return counts, -(-total // PACK_ALIGN) * PACK_ALIGN


def _pack(parts, rh):
    lead = parts[0].shape[:-4]
    nl = len(lead)
    flat = []
    for p in parts:
        p = jnp.moveaxis(p, nl + 1, 0)
        flat.append(p.reshape((2,) + lead + (-1, PACK_W)))
    used = sum(f.shape[-2] for f in flat)
    flat.append(jnp.zeros((2,) + lead + (rh - used, PACK_W), flat[0].dtype))
    return jnp.concatenate(flat, axis=nl + 1)


def _unpack(buf, shard_shapes, counts):
    lead = buf.shape[:-3]
    out, off = [], 0
    for (l, r, cs), n in zip(shard_shapes, counts):
        p = buf[..., off:off + n, :].reshape(lead + (2, l, r // 2, cs))
        out.append(jnp.moveaxis(p, len(lead), len(lead) + 1).reshape(lead + (l, r, cs)))
        off += n
    return out


def kernel(x, c, ada_w, ada_b, norm1_g, norm2_g, dn_w_in, dn_conv_w, dn_a_log, dn_dt_bias, dn_onorm_g, dn_w_out, sb_w_qkv, sb_q_norm_g, sb_k_norm_g, sb_w_out, ffn_w_in, ffn_w_out, loss_target, m_ada_w, m_ada_b, m_norm1_g, m_norm2_g, m_dn_w_in, m_dn_conv_w, m_dn_a_log, m_dn_dt_bias, m_dn_onorm_g, m_dn_w_out, m_sb_w_qkv, m_sb_q_norm_g, m_sb_k_norm_g, m_sb_w_out, m_ffn_w_in, m_ffn_w_out, v_ada_w, v_ada_b, v_norm1_g, v_norm2_g, v_dn_w_in, v_dn_conv_w, v_dn_a_log, v_dn_dt_bias, v_dn_onorm_g, v_dn_w_out, v_sb_w_qkv, v_sb_q_norm_g, v_sb_k_norm_g, v_sb_w_out, v_ffn_w_in, v_ffn_w_out):
    names = ("ada_w", "ada_b", "norm1_g", "norm2_g", "dn_w_in", "dn_conv_w", "dn_a_log", "dn_dt_bias", "dn_onorm_g",
             "dn_w_out", "sb_w_qkv", "sb_q_norm_g", "sb_k_norm_g", "sb_w_out", "ffn_w_in", "ffn_w_out")
    w = dict(zip(names, (ada_w, ada_b, norm1_g, norm2_g, dn_w_in, dn_conv_w, dn_a_log, dn_dt_bias, dn_onorm_g,
                         dn_w_out, sb_w_qkv, sb_q_norm_g, sb_k_norm_g, sb_w_out, ffn_w_in, ffn_w_out)))
    mom = dict(zip(names, (m_ada_w, m_ada_b, m_norm1_g, m_norm2_g, m_dn_w_in, m_dn_conv_w, m_dn_a_log, m_dn_dt_bias,
                           m_dn_onorm_g, m_dn_w_out, m_sb_w_qkv, m_sb_q_norm_g, m_sb_k_norm_g, m_sb_w_out, m_ffn_w_in,
                           m_ffn_w_out)))
    var = dict(zip(names, (v_ada_w, v_ada_b, v_norm1_g, v_norm2_g, v_dn_w_in, v_dn_conv_w, v_dn_a_log, v_dn_dt_bias,
                           v_dn_onorm_g, v_dn_w_out, v_sb_w_qkv, v_sb_q_norm_g, v_sb_k_norm_g, v_sb_w_out, v_ffn_w_in,
                           v_ffn_w_out)))
    ax, ay, ac = _axes()
    chip = 2 * ax + ay
    dev = 2 * chip + ac
    t, d = x.shape[1], x.shape[2]
    depth, ndn, nsb = ada_w.shape[0], dn_w_in.shape[0], sb_w_qkv.shape[0]
    heads = d // DN_HEAD_DIM
    mod_cols = ada_w.shape[2]
    conv_cols = dn_conv_w.shape[2]
    nchips, ndev = 4, 8

    conv_rows = ndn * DN_CONV * conv_cols // d
    pay1 = jnp.concatenate([c, dn_conv_w.reshape(conv_rows, d), jnp.zeros((8 - 1 - conv_rows, d), F32)], axis=0)
    g1 = _allgather_small(pay1, "ag_cond").reshape(ndev, 8, d)
    c_all = g1[:, 0]
    conv_full = g1[::2, 1:1 + conv_rows].reshape(nchips, ndn, DN_CONV, conv_cols).transpose(1, 2, 0, 3)
    conv_full = conv_full.reshape(ndn, DN_CONV, nchips * conv_cols)

    c16 = jnp.pad(c_all, ((0, 16 - ndev), (0, 0)))
    cond16 = _rowwise(lambda r, v: ([_silu(r[0])], []), [c16], [], [(d, F32)], [], name="cond_silu")[0]
    pay2 = jnp.concatenate([_mm(cond16, ada_w[i], name="ada_mod%d" % i)[:ndev] for i in range(depth)], axis=0)
    g2 = _allgather_small(pay2, "ag_mod").reshape(ndev, depth, ndev, mod_cols)[::2]
    mod_raw = lax.dynamic_index_in_dim(g2, dev, axis=2, keepdims=False)
    mod_raw = mod_raw.transpose(1, 0, 2).reshape(depth, nchips * mod_cols)
    mod = _rowwise(lambda r, v: ([r[0] + r[1]], []), [mod_raw, ada_b], [], [(nchips * mod_cols, F32)], [],
                   name="ada_bias")[0]

    shard_shapes = [w[n].shape for n, _ in _BIG]
    counts, rh = _pack_rows(shard_shapes)
    halves = lambda a: a.reshape(a.shape[:-2] + (2, a.shape[-2] // 2, a.shape[-1]))
    packed = _pack([halves(w[n].astype(BF16)) for n, _ in _BIG], rh)
    gathered = _allgather_weights(packed, "ag_weights")
    full = {}
    for (n, axis), a in zip(_BIG, _unpack(gathered, shard_shapes, counts)):
        l, r, cs = a.shape[1:]
        if axis == 2:
            full[n] = a.transpose(1, 2, 0, 3).reshape(l, r, nchips * cs)
        else:
            full[n] = a.transpose(1, 0, 2, 3).reshape(l, nchips * r, cs)

    padl = lambda v: jnp.pad(v[None, :], ((0, 0), (0, LANES - v.shape[0])))
    padc = lambda a: jnp.pad(a, ((0, 0), (0, LANES - a.shape[1])))
    dn = []
    for j in range(ndn):
        wi = full["dn_w_in"][j]
        w_all = jnp.concatenate([wi[:, :4 * d], padc(wi[:, 4 * d:4 * d + heads]), padc(wi[:, 4 * d + heads:])], axis=1)
        dn.append(dict(all=w_all, out=full["dn_w_out"][j], conv_w=conv_full[j], a_log=padl(dn_a_log[j]),
                       dt_bias=padl(dn_dt_bias[j]), onorm_g=dn_onorm_g[j][None]))
    sb = [dict(qkv=full["sb_w_qkv"][j], out=full["sb_w_out"][j], q_g=sb_q_norm_g[j][None], k_g=sb_k_norm_g[j][None])
          for j in range(nsb)]
    ffn = [dict(w_in=full["ffn_w_in"][i], w_out=full["ffn_w_out"][i]) for i in range(depth)]

    loss_local, grad_x, dmod, g_n1, g_n2, g_dn, g_sb, g_ffn = _local_step(x[0], loss_target[0], mod, norm1_g, norm2_g,
                                                                           dn, sb, ffn)
    loss = lax.psum(loss_local, ("x", "y", "c"))

    g_conv = jnp.stack([g["conv_w"] for g in g_dn])
    misc = jnp.concatenate([jnp.concatenate([g["onorm_g"] for g in g_dn], axis=1),
                            jnp.concatenate([g["a_log"] for g in g_dn], axis=1),
                            jnp.concatenate([g["dt_bias"] for g in g_dn], axis=1),
                            jnp.concatenate([g["q_norm_g"] for g in g_sb], axis=1),
                            jnp.concatenate([g["k_norm_g"] for g in g_sb], axis=1)], axis=1)
    misc = jnp.pad(misc, ((0, 0), (0, -misc.shape[1] % d))).reshape(-1, d)
    small = [dmod.reshape(-1, d), g_n1, g_n2, g_conv.reshape(-1, d), misc]
    small_rows = [s.shape[0] for s in small]
    pad_rows = -sum(small_rows) % 8
    pay3 = jnp.concatenate(small + [jnp.zeros((pad_rows, d), F32)], axis=0)
    nrow3 = pay3.shape[0]
    g3 = _allgather_small(pay3, "ag_small")
    summed = _rowwise(lambda r, v: ([], [_colsum(r[0])]), [g3.reshape(ndev, nrow3 * d)], [], [], [(1, nrow3 * d)],
                      name="small_sum")[0].reshape(nrow3, d)
    offs = [0]
    for n_ in small_rows:
        offs.append(offs[-1] + n_)
    grads = {}
    grads["ada_b"] = summed[offs[0]:offs[1]].reshape(depth, N_MOD * d)
    grads["norm1_g"] = summed[offs[1]:offs[2]]
    grads["norm2_g"] = summed[offs[2]:offs[3]]
    conv_sum = summed[offs[3]:offs[4]].reshape(ndn, DN_CONV, nchips * conv_cols)
    grads["dn_conv_w"] = lax.dynamic_slice_in_dim(conv_sum, chip * conv_cols, conv_cols, axis=2)
    mrow = summed[offs[4]:offs[5]].reshape(-1)
    o = 0
    grads["dn_onorm_g"] = mrow[o:o + ndn * DN_HEAD_DIM].reshape(ndn, DN_HEAD_DIM)
    o += ndn * DN_HEAD_DIM
    grads["dn_a_log"] = mrow[o:o + ndn * LANES].reshape(ndn, LANES)[:, :heads]
    o += ndn * LANES
    grads["dn_dt_bias"] = mrow[o:o + ndn * LANES].reshape(ndn, LANES)[:, :heads]
    o += ndn * LANES
    grads["sb_q_norm_g"] = mrow[o:o + nsb * SB_HEAD_DIM].reshape(nsb, SB_HEAD_DIM)
    o += nsb * SB_HEAD_DIM
    grads["sb_k_norm_g"] = mrow[o:o + nsb * SB_HEAD_DIM].reshape(nsb, SB_HEAD_DIM)
    dmod_all = g3.reshape(ndev, nrow3, d)[:, :small_rows[0]].reshape(ndev, depth, N_MOD * d)
    dmod_mine = lax.dynamic_slice_in_dim(dmod_all, chip * mod_cols, mod_cols, axis=2)
    dmod16 = jnp.pad(dmod_mine, ((0, 16 - ndev), (0, 0), (0, 0)))
    grads["ada_w"] = jnp.stack([_mm(cond16, dmod16[:, i], ta=True, name="ada_gw%d" % i) for i in range(depth)])

    def gw_in(g):
        ga = g["w_all"]
        return jnp.concatenate([ga[:, :4 * d], ga[:, 4 * d:4 * d + heads], ga[:, 4 * d + LANES:4 * d + LANES + heads]],
                               axis=1)
    gfull = dict(dn_w_in=jnp.stack([gw_in(g) for g in g_dn]), dn_w_out=jnp.stack([g["w_out"] for g in g_dn]),
                 sb_w_qkv=jnp.stack([g["w_qkv"] for g in g_sb]), sb_w_out=jnp.stack([g["w_out"] for g in g_sb]),
                 ffn_w_in=jnp.stack([g["w_in"] for g in g_ffn]), ffn_w_out=jnp.stack([g["w_out"] for g in g_ffn]))
    parts = []
    for (n, axis), (l, r, cs) in zip(_BIG, shard_shapes):
        a = gfull[n].astype(BF16)
        if axis == 2:
            a = a.reshape(l, r, nchips, cs).transpose(2, 0, 1, 3)
        else:
            a = a.reshape(l, nchips, r, cs).transpose(1, 0, 2, 3)
        parts.append(halves(a))
    gp = _pack(parts, rh)
    from_sibling = _sibling_swap(gp, True, "gr_pair")
    own_half = lax.dynamic_index_in_dim(gp, ac, axis=0, keepdims=False)
    pair = _rowwise(lambda r, v: ([r[0].astype(F32) + r[1].astype(F32)], []),
                    [own_half.reshape(nchips * rh, PACK_W), from_sibling.reshape(nchips * rh, PACK_W)], [],
                    [(PACK_W, BF16)], [], name="gr_pair_add")[0]
    from_chips = _chip_scatter(pair.reshape(nchips, rh, PACK_W), "gr_chips")
    reduced = _sum_lead(from_chips, "gr_chip_add")
    other = _sibling_swap(reduced, False, "gr_share")
    both = jnp.where(ac == 0, jnp.stack([reduced, other]), jnp.stack([other, reduced]))
    for (n, _), g in zip(_BIG, _unpack(both, shard_shapes, counts)):
        grads[n] = g

    delta, new_m, new_v = {}, {}, {}
    for n in names:
        delta[n], new_m[n], new_v[n] = _adamw(w[n], grads[n], mom[n], var[n], "adamw_" + n)
    return (loss, grad_x[None], *[grads[n] for n in names], *[delta[n] for n in names], *[new_m[n] for n in names],
            *[new_v[n] for n in names])
```

```python
import functools

import jax
import jax.numpy as jnp
from jax import lax
from jax.experimental import pallas as pl
from jax.experimental.pallas import tpu as pltpu

F32 = jnp.float32
BF16 = jnp.bfloat16
HI = lax.Precision.HIGHEST
MESH = pl.DeviceIdType.MESH

EPS = 1e-6
N_MOD = 6
DN_HEAD_DIM = 128
DN_CONV = 4
DN_CHUNK = 64
DN_GROUP = 8
DN_GATE_ROWS = 256
SB_HEAD_DIM = 64
SB_BLOCK = 128
SB_QBLOCK = 256
LANES = 128
VMEM_LIMIT = 56 * 1024 * 1024
MM_BLOCK_BUDGET = 36 * 1024 * 1024
ROW_BLOCK_BUDGET = 20 * 1024 * 1024

ADAM_LR = 0.001
ADAM_B1 = 0.9
ADAM_B2 = 0.999
ADAM_EPS = 1e-08
ADAM_WD = 0.01
ADAM_STEP = 10


def _cparams(sem=None):
    return pltpu.CompilerParams(dimension_semantics=sem, vmem_limit_bytes=VMEM_LIMIT)


def _sigmoid(x):
    return 1.0 / (1.0 + jnp.exp(-x))


def _silu(x):
    return x * _sigmoid(x)


def _dsilu(x):
    s = _sigmoid(x)
    return s * (1.0 + x * (1.0 - s))


def _softplus(x):
    return jnp.maximum(x, 0.0) + jnp.log1p(jnp.exp(-jnp.abs(x)))


def _pick_tile(n, prefs):
    for t in prefs:
        if n % t == 0:
            return t
    return n


def _mm(a, b, *, ta=False, tb=False, out_dtype=F32, name):
    m = a.shape[1] if ta else a.shape[0]
    k = a.shape[0] if ta else a.shape[1]
    n = b.shape[0] if tb else b.shape[1]
    assert (b.shape[1] if tb else b.shape[0]) == k
    size = lambda dt: jnp.dtype(dt).itemsize
    best = None
    for tm in sorted({t for t in (m, 2048, 1024, 512, 256, 128) if m % t == 0 and (t % 128 == 0 or t == m)}):
        for tn in sorted({t for t in (n, 2816, 1408, 1024, 768, 512, 256, 128) if n % t == 0 and (t % 128 == 0 or t == n)}):
            need = 2 * (tm * k * size(a.dtype) + tn * k * size(b.dtype) + tm * tn * size(out_dtype))
            if need <= MM_BLOCK_BUDGET and (best is None or tm * tn > best[0] * best[1]):
                best = (tm, tn)
    tm, tn = best
    dims = (((0 if ta else 1,), (1 if tb else 0,)), ((), ()))

    def body(a_ref, b_ref, o_ref):
        av = a_ref[...].astype(BF16)
        bv = b_ref[...].astype(BF16)
        o_ref[...] = lax.dot_general(av, bv, dims, preferred_element_type=F32).astype(o_ref.dtype)

    a_spec = pl.BlockSpec((k, tm), lambda i, j: (0, i)) if ta else pl.BlockSpec((tm, k), lambda i, j: (i, 0))
    b_spec = pl.BlockSpec((tn, k), lambda i, j: (j, 0)) if tb else pl.BlockSpec((k, tn), lambda i, j: (0, j))
    return pl.pallas_call(
        body, grid=(m // tm, n // tn), in_specs=[a_spec, b_spec],
        out_specs=pl.BlockSpec((tm, tn), lambda i, j: (i, j)),
        out_shape=jax.ShapeDtypeStruct((m, n), out_dtype), name=name,
        compiler_params=_cparams(("parallel", "parallel")))(a, b)


def _rowwise(body, rows, vecs, outs, accs, *, name, tr=None):
    rows = [r if isinstance(r, tuple) else (r, r.shape[1], 0) for r in rows]
    nrows = rows[0][0].shape[0]
    if tr is None:
        per_row = sum(w * jnp.dtype(a.dtype).itemsize for a, w, _ in rows) + sum(
            w * jnp.dtype(dt).itemsize for w, dt in outs)
        tr = next((t for t in (2048, 1024, 512, 256) if nrows % t == 0 and 2 * t * per_row <= ROW_BLOCK_BUDGET), 256)
    tr = tr if nrows % tr == 0 else nrows
    nr, nv, no = len(rows), len(vecs), len(outs)

    def kern(*refs):
        r_in, v_in = refs[:nr], refs[nr:nr + nv]
        o_refs, a_refs = refs[nr + nv:nr + nv + no], refs[nr + nv + no:]
        res_o, res_a = body([r[...] for r in r_in], [v[...] for v in v_in])
        for o, val in zip(o_refs, res_o):
            o[...] = val.astype(o.dtype)
        if a_refs:
            @pl.when(pl.program_id(0) == 0)
            def _():
                for a in a_refs:
                    a[...] = jnp.zeros(a.shape, a.dtype)
            for a, val in zip(a_refs, res_a):
                a[...] += val

    in_specs = [pl.BlockSpec((tr, w), functools.partial(lambda i, cb: (i, cb), cb=cb)) for _, w, cb in rows]
    in_specs += [pl.BlockSpec(v.shape, functools.partial(lambda i, nd: (0,) * nd, nd=v.ndim)) for v in vecs]
    out_specs = [pl.BlockSpec((tr, w), lambda i: (i, 0)) for w, _ in outs]
    out_specs += [pl.BlockSpec(s, lambda i: (0, 0)) for s in accs]
    out_shape = [jax.ShapeDtypeStruct((nrows, w), dt) for w, dt in outs]
    out_shape += [jax.ShapeDtypeStruct(s, F32) for s in accs]
    res = pl.pallas_call(
        kern, grid=(nrows // tr,), in_specs=in_specs, out_specs=out_specs, out_shape=out_shape, name=name,
        compiler_params=_cparams(("arbitrary",) if accs else ("parallel",)))(*[r[0] for r in rows], *vecs)
    return res


def _colsum(v):
    return jnp.sum(v, axis=0, keepdims=True)


def _adaln_fwd(x, g, sc, sh, name):
    def body(r, v):
        (xv,), (gv, scv, shv) = r, v
        rs = lax.rsqrt(jnp.mean(xv * xv, axis=-1, keepdims=True) + EPS)
        return [(xv * rs) * gv * (1.0 + scv) + shv], []
    return _rowwise(body, [x], [g, sc, sh], [(x.shape[1], BF16)], [], name=name)[0]


def _adaln_bwd(x, dh, dxr, g, sc, name):
    d = x.shape[1]

    def body(r, v):
        (xv, dhv, dxv), (gv, scv) = r, v
        rs = lax.rsqrt(jnp.mean(xv * xv, axis=-1, keepdims=True) + EPS)
        nv = xv * rs
        dn = dhv * (gv * (1.0 + scv))
        dx = rs * (dn - nv * jnp.mean(dn * nv, axis=-1, keepdims=True)) + dxv
        dhn = dhv * nv
        return [dx], [_colsum(dhn * (1.0 + scv)), _colsum(dhn * gv), _colsum(dhv)]
    return _rowwise(body, [x, dh, dxr], [g, sc], [(d, F32)], [(1, d)] * 3, name=name)


def _resid_fwd(x, y, gt, name):
    def body(r, v):
        return [r[0] + v[0] * r[1]], []
    return _rowwise(body, [x, y], [gt], [(x.shape[1], F32)], [], name=name)[0]


def _resid_bwd(dx, y, gt, name):
    d = dx.shape[1]

    def body(r, v):
        return [v[0] * r[0]], [_colsum(r[0] * r[1])]
    return _rowwise(body, [dx, y], [gt], [(d, BF16)], [(1, d)], name=name)


def _swiglu_fwd(u, name):
    f = u.shape[1] // 2

    def body(r, v):
        uv = r[0]
        return [_silu(uv[:, :f]) * uv[:, f:]], []
    return _rowwise(body, [u], [], [(f, BF16)], [], name=name)[0]


def _swiglu_bwd(u, da, name):
    f = u.shape[1] // 2

    def body(r, v):
        uv, dav = r
        gate, up = uv[:, :f], uv[:, f:]
        return [jnp.concatenate([dav * up * _dsilu(gate), dav * _silu(gate)], axis=1)], []
    return _rowwise(body, [u, da], [], [(2 * f, BF16)], [], name=name)[0]


def _loss_fwd_bwd(y, tgt, name):
    d = y.shape[1]

    def body(r, v):
        err = r[0] - r[1]
        part = jnp.sum(jnp.sum(err * err, axis=1, keepdims=True), axis=0, keepdims=True) * (0.5 / d)
        return [err * (1.0 / d)], [jnp.broadcast_to(part, (1, LANES))]
    dy, acc = _rowwise(body, [y, tgt], [], [(d, F32)], [(1, LANES)], name=name)
    return acc[0, 0], dy


def _head_means(v):
    row = lax.broadcasted_iota(jnp.int32, (LANES, LANES), 0)
    col = lax.broadcasted_iota(jnp.int32, (LANES, LANES), 1)
    same = ((row // SB_HEAD_DIM) == (col // SB_HEAD_DIM)).astype(F32)
    parts = [jnp.dot(v[:, g * LANES:(g + 1) * LANES], same, precision=HI, preferred_element_type=F32)
             for g in range(v.shape[1] // LANES)]
    return jnp.concatenate(parts, axis=1) * (1.0 / SB_HEAD_DIM)


def _sb_norm_fwd(qkv, gq, gk, name):
    d = qkv.shape[1] // 3

    def body(r, v):
        return [x * lax.rsqrt(_head_means(x * x) + EPS) * g for x, g in zip(r, v)], []
    return _rowwise(body, [(qkv, d, 0), (qkv, d, 1)], [gq, gk], [(d, F32), (d, F32)], [], name=name)


def _sb_norm_bwd(qkv, dqn, dkn, dv, gq, gk, name):
    d = qkv.shape[1] // 3

    def body(r, v):
        outs, accs = [], []
        for x, dy, g in ((r[0], r[2], v[0]), (r[1], r[3], v[1])):
            rs = lax.rsqrt(_head_means(x * x) + EPS)
            nv = x * rs
            dn = dy * g
            outs.append(rs * (dn - nv * _head_means(dn * nv)))
            accs.append(_colsum(dy * nv))
        return [jnp.concatenate(outs + [r[4]], axis=1)], accs
    return _rowwise(body, [(qkv, d, 0), (qkv, d, 1), dqn, dkn, dv], [gq, gk], [(3 * d, BF16)], [(1, d), (1, d)],
                    name=name)


def _head_tiles(a):
    return [a[:, h * DN_HEAD_DIM:(h + 1) * DN_HEAD_DIM] for h in range(a.shape[1] // DN_HEAD_DIM)]


def _dn_post_fwd(o, proj, g, name):
    d = o.shape[1]

    def body(r, v):
        outs = []
        for ov, zv in zip(_head_tiles(r[0]), _head_tiles(r[1])):
            rs = lax.rsqrt(jnp.mean(ov * ov, axis=-1, keepdims=True) + EPS)
            outs.append(ov * rs * v[0] * _silu(zv))
        return [jnp.concatenate(outs, axis=1)], []
    return _rowwise(body, [o, (proj, d, 3)], [g], [(d, BF16)], [], name=name)[0]


def _dn_post_bwd(o, proj, don, g, name):
    d = o.shape[1]

    def body(r, v):
        dos, dzs, dg = [], [], jnp.zeros((1, DN_HEAD_DIM), F32)
        for ov, zv, dv in zip(_head_tiles(r[0]), _head_tiles(r[1]), _head_tiles(r[2])):
            rs = lax.rsqrt(jnp.mean(ov * ov, axis=-1, keepdims=True) + EPS)
            nv = ov * rs
            s = _silu(zv)
            dn = dv * v[0] * s
            dos.append(rs * (dn - nv * jnp.mean(dn * nv, axis=-1, keepdims=True)))
            dzs.append(dv * nv * v[0] * _dsilu(zv))
            dg = dg + _colsum(dv * nv * s)
        return [jnp.concatenate(dos, axis=1), jnp.concatenate(dzs, axis=1)], [dg]
    return _rowwise(body, [o, (proj, d, 3), don], [g], [(d, F32), (d, BF16)], [(1, DN_HEAD_DIM)], name=name)


def _chunk_tri(tr, upper):
    row = lax.broadcasted_iota(jnp.int32, (tr, tr), 0)
    col = lax.broadcasted_iota(jnp.int32, (tr, tr), 1)
    same = (row // DN_CHUNK) == (col // DN_CHUNK)
    return (same & ((row <= col) if upper else (row >= col))).astype(F32)


def _dn_gate_fwd(proj, colblk, a_log, dt_bias, name):
    def body(r, v):
        ab = r[0]
        a, b = ab[:, :LANES], ab[:, LANES:]
        g = -jnp.exp(v[0]) * _softplus(a + v[1])
        big_g = jnp.dot(_chunk_tri(g.shape[0], False), g, precision=HI, preferred_element_type=F32)
        return [big_g, _sigmoid(b)], []
    return _rowwise(body, [(proj, 2 * LANES, colblk)], [a_log, dt_bias], [(LANES, F32), (LANES, F32)], [], name=name,
                    tr=DN_GATE_ROWS)


def _dn_gate_bwd(proj, colblk, d_big_g, dbeta, a_log, dt_bias, name):
    def body(r, v):
        ab, dgc, dbt = r
        a, b = ab[:, :LANES], ab[:, LANES:]
        dg = jnp.dot(_chunk_tri(dgc.shape[0], True), dgc, precision=HI, preferred_element_type=F32)
        na = -jnp.exp(v[0])
        pre = a + v[1]
        da = dg * na * _sigmoid(pre)
        beta = _sigmoid(b)
        db = dbt * beta * (1.0 - beta)
        return [jnp.concatenate([da, db], axis=1)], [_colsum(dg * na * _softplus(pre)), _colsum(da)]
    return _rowwise(body, [(proj, 2 * LANES, colblk), d_big_g, dbeta], [a_log, dt_bias],
                    [(2 * LANES, BF16)], [(1, LANES), (1, LANES)], name=name, tr=DN_GATE_ROWS)


def _shift_rows(x, s):
    if s == 0:
        return x
    t = x.shape[0]
    row = lax.broadcasted_iota(jnp.int32, x.shape, 0)
    rolled = pltpu.roll(x, s % t, axis=0)
    return jnp.where((row >= s) if s > 0 else (row < t + s), rolled, 0.0)


def _rnd(x):
    return x.astype(BF16).astype(F32)


def _conv(x, c_ref):
    c = x * _rnd(c_ref[DN_CONV - 1:DN_CONV, :])
    for s in range(1, DN_CONV):
        c = c + _shift_rows(x, s) * _rnd(c_ref[DN_CONV - 1 - s:DN_CONV - s, :])
    return c


def _dn_prep_fwd(proj, conv_w, heads, name):
    t = proj.shape[0]
    d = heads * DN_HEAD_DIM

    def body(xq, xk, xv, cq, ck, cv, q_ref, k_ref, v_ref):
        for x_ref, c_ref, o_ref, norm in ((xq, cq, q_ref, True), (xk, ck, k_ref, True), (xv, cv, v_ref, False)):
            y = _silu(_conv(_rnd(x_ref[...]), c_ref))
            if norm:
                y = y * lax.rsqrt(jnp.sum(y * y, axis=-1, keepdims=True) + EPS)
            o_ref[...] = y

    xs = [pl.BlockSpec((t, DN_HEAD_DIM), functools.partial(lambda h, o: (0, h + o), o=o * heads)) for o in range(3)]
    cs = [pl.BlockSpec((DN_CONV, DN_HEAD_DIM), functools.partial(lambda h, o: (0, h + o), o=o * heads)) for o in range(3)]
    return pl.pallas_call(
        body, grid=(heads,), in_specs=xs + cs,
        out_specs=[pl.BlockSpec((t, DN_HEAD_DIM), lambda h: (0, h))] * 3,
        out_shape=[jax.ShapeDtypeStruct((t, d), F32)] * 3, name=name,
        compiler_params=_cparams(("parallel",)))(proj, proj, proj, conv_w, conv_w, conv_w)


def _dn_prep_bwd(proj, conv_w, dq, dk, dv, heads, name):
    t = proj.shape[0]
    d = heads * DN_HEAD_DIM

    def body(xq, xk, xv, cq, ck, cv, gq, gk, gv, oq, ok, ov, wq, wk, wv):
        for x_ref, c_ref, g_ref, o_ref, w_ref, norm in ((xq, cq, gq, oq, wq, True), (xk, ck, gk, ok, wk, True),
                                                        (xv, cv, gv, ov, wv, False)):
            x, dy = _rnd(x_ref[...]), g_ref[...]
            c = _conv(x, c_ref)
            if norm:
                y = _silu(c)
                rs = lax.rsqrt(jnp.sum(y * y, axis=-1, keepdims=True) + EPS)
                yn = y * rs
                dy = rs * (dy - yn * jnp.sum(dy * yn, axis=-1, keepdims=True))
            dc = _rnd(dy * _dsilu(c))
            dx = dc * _rnd(c_ref[DN_CONV - 1:DN_CONV, :])
            w_ref[DN_CONV - 1:DN_CONV, :] = _colsum(dc * x)
            for s in range(1, DN_CONV):
                dx = dx + _shift_rows(dc, -s) * _rnd(c_ref[DN_CONV - 1 - s:DN_CONV - s, :])
                w_ref[DN_CONV - 1 - s:DN_CONV - s, :] = _colsum(dc * _shift_rows(x, s))
            o_ref[...] = dx.astype(o_ref.dtype)

    xs = [pl.BlockSpec((t, DN_HEAD_DIM), functools.partial(lambda h, o: (0, h + o), o=o * heads)) for o in range(3)]
    cs = [pl.BlockSpec((DN_CONV, DN_HEAD_DIM), functools.partial(lambda h, o: (0, h + o), o=o * heads)) for o in range(3)]
    hs = pl.BlockSpec((t, DN_HEAD_DIM), lambda h: (0, h))
    ws = pl.BlockSpec((DN_CONV, DN_HEAD_DIM), lambda h: (0, h))
    return pl.pallas_call(
        body, grid=(heads,), in_specs=xs + cs + [hs] * 3, out_specs=[hs] * 3 + [ws] * 3,
        out_shape=[jax.ShapeDtypeStruct((t, d), BF16)] * 3 + [jax.ShapeDtypeStruct((DN_CONV, d), F32)] * 3, name=name,
        compiler_params=_cparams(("parallel",)))(proj, proj, proj, conv_w, conv_w, conv_w, dq, dk, dv)


def _bmm(a, b, prec=None):
    return lax.dot_general(a, b, (((2,), (1,)), ((0,), (0,))), precision=prec, preferred_element_type=F32)


def _bmm_nt(a, b, prec=None):
    return lax.dot_general(a, b, (((2,), (2,)), ((0,), (0,))), precision=prec, preferred_element_type=F32)


def _bmm_tn(a, b, prec=None):
    return lax.dot_general(a, b, (((1,), (1,)), ((0,), (0,))), precision=prec, preferred_element_type=F32)


def _dn_local(qg, kg, vg, gc, gr, bt):
    c = qg.shape[1]
    row = lax.broadcasted_iota(jnp.int32, (c, c), 0)
    col = lax.broadcasted_iota(jnp.int32, (c, c), 1)
    incl, strict = (row >= col)[None], (row > col)[None]
    decay = jnp.where(incl, jnp.exp(jnp.where(incl, gc - gr, 0.0)), 0.0)
    kb = kg * bt
    vb = vg * bt
    m = _bmm_nt(kb.astype(BF16), kg.astype(BF16))
    a = jnp.where(strict, m * decay, 0.0)
    bp = -a
    tm = jnp.where((row == col)[None], 1.0, 0.0) + bp
    steps = max(1, (c - 1).bit_length()) - 1
    for _ in range(steps):
        bp = _bmm(bp, bp, HI)
        tm = tm + _bmm(tm, bp, HI)
    eg = jnp.exp(gc)
    glast = gc[:, c - 1:c, :]
    erel = jnp.exp(glast - gc)
    kbg = kb * eg
    qm = _bmm_nt(qg.astype(BF16), kg.astype(BF16))
    return dict(decay=decay, strict=strict, kb=kb, vb=vb, m=m, tm=tm, eg=eg, erel=erel, kbg=kbg, qm=qm,
                u=_bmm(tm, vb, HI), w=_bmm(tm, kbg, HI), qd=qg * eg, kt=kg * erel, gl=jnp.exp(glast))


def _dot(a, b):
    return jnp.dot(a.astype(BF16), b.astype(BF16), preferred_element_type=F32)


def _dot_nt(a, b):
    return lax.dot_general(a.astype(BF16), b.astype(BF16), (((1,), (1,)), ((), ())), preferred_element_type=F32)


def _dot_tn(a, b):
    return lax.dot_general(a.astype(BF16), b.astype(BF16), (((0,), (0,)), ((), ())), preferred_element_type=F32)


def _dn_chunk_specs(t, heads):
    n = t // DN_CHUNK
    hs = pl.BlockSpec((t, DN_HEAD_DIM), lambda h: (0, h))
    gcs = pl.BlockSpec((1, n, DN_CHUNK, 1), lambda h: (h, 0, 0, 0))
    grs = pl.BlockSpec((1, n, 1, DN_CHUNK), lambda h: (h, 0, 0, 0))
    ss = pl.BlockSpec((1, n, DN_HEAD_DIM, DN_HEAD_DIM), lambda h: (h, 0, 0, 0))
    return n, hs, gcs, grs, ss


def _dn_chunk_fwd(q, k, v, gc, gr, bc, name):
    t, d = q.shape
    heads = d // DN_HEAD_DIM
    c, dk = DN_CHUNK, DN_HEAD_DIM
    n, hs, gcs, grs, ss = _dn_chunk_specs(t, heads)
    nb = min(DN_GROUP, n)
    scale = dk ** -0.5

    def body(q_ref, k_ref, v_ref, gc_ref, gr_ref, b_ref, o_ref, s_ref, u_s, w_s, at_s, qd_s, kt_s, gl_s):
        def group(gi, carry):
            r0 = pl.multiple_of(gi * (nb * c), nb * c)
            n0 = gi * nb
            ld = lambda ref: ref[pl.ds(r0, nb * c), :].reshape(nb, c, dk)
            loc = _dn_local(ld(q_ref) * scale, ld(k_ref), ld(v_ref), gc_ref[0, pl.ds(n0, nb)],
                            gr_ref[0, pl.ds(n0, nb)], b_ref[0, pl.ds(n0, nb)])
            u_s[pl.ds(n0, nb)] = loc["u"]
            w_s[pl.ds(n0, nb)] = loc["w"]
            at_s[pl.ds(n0, nb)] = loc["qm"] * loc["decay"]
            qd_s[pl.ds(n0, nb)] = loc["qd"]
            kt_s[pl.ds(n0, nb)] = loc["kt"]
            gl_s[pl.ds(n0, nb)] = loc["gl"]
            return carry
        lax.fori_loop(0, n // nb, group, 0)

        def chunk(i, s):
            s_ref[0, i] = s
            vnew = u_s[i] - _dot(w_s[i], s)
            o = _dot(qd_s[i], s) + _dot(at_s[i], vnew)
            o_ref[pl.ds(pl.multiple_of(i * c, c), c), :] = o
            return s * gl_s[i] + _dot_tn(kt_s[i], vnew)
        lax.fori_loop(0, n, chunk, jnp.zeros((dk, dk), F32))

    scratch = [pltpu.VMEM((n, c, dk), F32), pltpu.VMEM((n, c, dk), F32), pltpu.VMEM((n, c, c), F32),
               pltpu.VMEM((n, c, dk), F32), pltpu.VMEM((n, c, dk), F32), pltpu.VMEM((n, 1, 1), F32)]
    return pl.pallas_call(
        body, grid=(heads,), in_specs=[hs, hs, hs, gcs, grs, gcs], out_specs=[hs, ss],
        out_shape=[jax.ShapeDtypeStruct((t, d), F32), jax.ShapeDtypeStruct((heads, n, dk, dk), F32)],
        scratch_shapes=scratch, name=name, compiler_params=_cparams(("parallel",)))(q, k, v, gc, gr, bc)


def _dn_chunk_bwd(q, k, v, gc, gr, bc, s_all, do, name):
    t, d = q.shape
    heads = d // DN_HEAD_DIM
    c, dk = DN_CHUNK, DN_HEAD_DIM
    n, hs, gcs, grs, ss = _dn_chunk_specs(t, heads)
    nb = min(DN_GROUP, n)
    scale = dk ** -0.5

    def body(q_ref, k_ref, v_ref, gc_ref, gr_ref, b_ref, s_ref, do_ref,
             dq_ref, dk_ref, dv_ref, dgc_ref, dgr_ref, db_ref,
             u_s, w_s, att_s, qd_s, kt_s, gl_s, du_s, dw_s, dat_s, dqd_s, dkt_s, dgl_s):
        def load_group(gi):
            r0 = pl.multiple_of(gi * (nb * c), nb * c)
            n0 = gi * nb
            ld = lambda ref: ref[pl.ds(r0, nb * c), :].reshape(nb, c, dk)
            qg, kg, vg = ld(q_ref) * scale, ld(k_ref), ld(v_ref)
            gcv, grv, bt = gc_ref[0, pl.ds(n0, nb)], gr_ref[0, pl.ds(n0, nb)], b_ref[0, pl.ds(n0, nb)]
            return r0, n0, qg, kg, vg, gcv, grv, bt, _dn_local(qg, kg, vg, gcv, grv, bt)

        def group_a(gi, carry):
            _, n0, qg, kg, _, gcv, grv, _, loc = load_group(gi)
            row = lax.broadcasted_iota(jnp.int32, (c, c), 0)
            col = lax.broadcasted_iota(jnp.int32, (c, c), 1)
            upper = (col >= row)[None]
            decay_t = jnp.where(upper, jnp.exp(jnp.where(upper, grv - gcv, 0.0)), 0.0)
            u_s[pl.ds(n0, nb)] = loc["u"]
            w_s[pl.ds(n0, nb)] = loc["w"]
            att_s[pl.ds(n0, nb)] = _bmm_nt(kg.astype(BF16), qg.astype(BF16)) * decay_t
            qd_s[pl.ds(n0, nb)] = loc["qd"]
            kt_s[pl.ds(n0, nb)] = loc["kt"]
            gl_s[pl.ds(n0, nb)] = loc["gl"]
            return carry
        lax.fori_loop(0, n // nb, group_a, 0)

        def chunk_b(it, ds_next):
            i = n - 1 - it
            s = s_ref[0, i]
            dov = do_ref[pl.ds(pl.multiple_of(i * c, c), c), :]
            w, kt = w_s[i], kt_s[i]
            vnew = u_s[i] - _dot(w, s)
            dvnew = _dot(att_s[i], dov) + _dot(kt, ds_next)
            du_s[i] = dvnew
            dw_s[i] = -_dot_nt(dvnew, s)
            dat_s[i] = _dot_nt(dov, vnew)
            dqd_s[i] = _dot_nt(dov, s)
            dkt_s[i] = _dot_nt(vnew, ds_next)
            dgl_s[i] = jnp.sum(jnp.sum(ds_next * s, axis=1, keepdims=True), axis=0, keepdims=True)
            return ds_next * gl_s[i] + _dot_tn(qd_s[i], dov) - _dot_tn(w, dvnew)
        lax.fori_loop(0, n, chunk_b, jnp.zeros((dk, dk), F32))

        def group_c(gi, carry):
            r0, n0, qg, kg, vg, gcv, grv, bt, loc = load_group(gi)
            sl = pl.ds(n0, nb)
            du, dw, dat, dqd, dkt, dgl = du_s[sl], dw_s[sl], dat_s[sl], dqd_s[sl], dkt_s[sl], dgl_s[sl]
            tm, decay, kb, kbg = loc["tm"], loc["decay"], loc["kb"], loc["kbg"]
            dvb = _bmm_tn(tm, du, HI)
            dkbg = _bmm_tn(tm, dw, HI)
            dt = _bmm_nt(du, loc["vb"], HI) + _bmm_nt(dw, kbg, HI)
            da = jnp.where(loc["strict"], -_bmm_tn(tm, _bmm_nt(dt, tm, HI), HI), 0.0)
            dms = (da * decay).astype(BF16)
            dqs = (dat * decay).astype(BF16)
            kgb = kg.astype(BF16)
            dkb = _bmm(dms, kgb) + dkbg * loc["eg"]
            dqt = _bmm(dqs, kgb) + dqd * loc["eg"]
            dkk = _bmm_tn(dms, kb.astype(BF16)) + _bmm_tn(dqs, qg.astype(BF16)) + dkt * loc["erel"] + dkb * bt
            e = (da * loc["m"] + dat * loc["qm"]) * decay
            lsum = lambda x: jnp.sum(x, axis=2, keepdims=True)
            dkt_kt = lsum(dkt * loc["kt"])
            dgcv = lsum(e) + lsum(dqd * loc["qd"]) - dkt_kt + lsum(dkbg * kbg)
            dglast = jnp.sum(dkt_kt, axis=1, keepdims=True) + dgl * loc["gl"]
            rowc = lax.broadcasted_iota(jnp.int32, (1, c, 1), 1)
            dgc_ref[0, sl] = dgcv + jnp.where(rowc == c - 1, dglast, 0.0)
            dgr_ref[0, sl] = -jnp.sum(e, axis=1, keepdims=True)
            db_ref[0, sl] = lsum(dkb * kg) + lsum(dvb * vg)
            rows = pl.ds(r0, nb * c)
            dq_ref[rows, :] = (dqt * scale).reshape(nb * c, dk)
            dk_ref[rows, :] = dkk.reshape(nb * c, dk)
            dv_ref[rows, :] = (dvb * bt).reshape(nb * c, dk)
            return carry
        lax.fori_loop(0, n // nb, group_c, 0)

    big = lambda: pltpu.VMEM((n, c, dk), F32)
    sq = lambda: pltpu.VMEM((n, c, c), F32)
    one = lambda: pltpu.VMEM((n, 1, 1), F32)
    scratch = [big(), big(), sq(), big(), big(), one(), big(), big(), sq(), big(), big(), one()]
    return pl.pallas_call(
        body, grid=(heads,), in_specs=[hs, hs, hs, gcs, grs, gcs, ss, hs], out_specs=[hs, hs, hs, gcs, grs, gcs],
        out_shape=[jax.ShapeDtypeStruct((t, d), F32)] * 3 + [
            jax.ShapeDtypeStruct((heads, n, c, 1), F32), jax.ShapeDtypeStruct((heads, n, 1, c), F32),
            jax.ShapeDtypeStruct((heads, n, c, 1), F32)],
        scratch_shapes=scratch, name=name, compiler_params=_cparams(("parallel",)))(q, k, v, gc, gr, bc, s_all, do)


def _dot01(x, m01):
    x1 = x.astype(BF16)
    r1 = x - x1.astype(F32)
    x2 = r1.astype(BF16)
    x3 = (r1 - x2.astype(F32)).astype(BF16)
    dot = lambda a: jnp.dot(a, m01, preferred_element_type=F32)
    return (dot(x1) + dot(x2)) + dot(x3)


def _sb_tile(z, mask, r):
    bk = z.shape[1]
    row = lax.broadcasted_iota(jnp.int32, (bk, bk), 0)
    col = lax.broadcasted_iota(jnp.int32, (bk, bk), 1)
    lsm = -_softplus(z)
    lm = lsm if mask is None else jnp.where(mask, lsm, 0.0)
    cs = _dot01(lm, (row > col).astype(BF16))
    ls = z + lsm
    a = jnp.exp(ls + cs + r)
    if mask is not None:
        a = jnp.where(mask, a, 0.0)
    return a, jnp.exp(ls), r + jnp.sum(lm, axis=1, keepdims=True)


def _sb_sweep(bq, bk, qi, visit, carry):
    nd = bq // bk
    row = lax.broadcasted_iota(jnp.int32, (2 * bq, bk), 0)
    row = jnp.where(row >= bq, row - bq, row)
    col = lax.broadcasted_iota(jnp.int32, (2 * bq, bk), 1)
    for dd in reversed(range(nd)):
        carry = visit(qi * nd + dd, (col + dd * bk) < row, carry)
    return lax.fori_loop(0, qi * nd, lambda it, c: visit(qi * nd - 1 - it, None, c), carry)


def _stack_heads(a, h0):
    return jnp.concatenate([jnp.where(h0, a, 0.0), jnp.where(h0, 0.0, a)], axis=0).astype(BF16)


def _side_by_side(a, bq):
    return jnp.concatenate([a[:bq], a[bq:]], axis=1)


def _sb_specs(t, d, bq, vcol):
    qs = pl.BlockSpec((bq, LANES), lambda g, i: (i, g))
    ks = pl.BlockSpec((t, LANES), lambda g, i: (0, g))
    vs = pl.BlockSpec((t, LANES), lambda g, i: (0, g + vcol))
    return qs, ks, vs


def _sb_fwd(qn, kn, qkv, name):
    t, d = qn.shape
    bq, bk = min(SB_QBLOCK, t), min(SB_BLOCK, t)
    scale = SB_HEAD_DIM ** -0.5
    nt = (((1,), (1,)), ((), ()))

    def body(q_ref, k_ref, v_ref, o_ref):
        qi = pl.program_id(1)
        h0 = lax.broadcasted_iota(jnp.int32, (1, LANES), 1) < SB_HEAD_DIM
        q2 = _stack_heads(q_ref[...] * scale, h0)

        def visit(j, mask, carry):
            acc, r = carry
            rows = pl.ds(pl.multiple_of(j * bk, bk), bk)
            z = lax.dot_general(q2, k_ref[rows, :].astype(BF16), nt, preferred_element_type=F32)
            a, _, r = _sb_tile(z, mask, r)
            a2 = _side_by_side(a.astype(BF16), bq)
            return acc + jnp.dot(a2, _stack_heads(v_ref[rows, :], h0), preferred_element_type=F32), r
        acc, _ = _sb_sweep(bq, bk, qi, visit, (jnp.zeros((bq, LANES), F32), jnp.zeros((2 * bq, 1), F32)))
        o_ref[...] = acc.astype(o_ref.dtype)

    qs, ks, vs = _sb_specs(t, d, bq, 2 * d // LANES)
    return pl.pallas_call(
        body, grid=(d // LANES, t // bq), in_specs=[qs, ks, vs], out_specs=qs,
        out_shape=jax.ShapeDtypeStruct((t, d), BF16), name=name,
        compiler_params=_cparams(("parallel", "parallel")))(qn, kn, qkv)


def _sb_bwd(qn, kn, qkv, do, name):
    t, d = qn.shape
    bq, bk = min(SB_QBLOCK, t), min(SB_BLOCK, t)
    scale = SB_HEAD_DIM ** -0.5
    nt = (((1,), (1,)), ((), ()))

    def body(q_ref, k_ref, v_ref, do_ref, dq_ref, dk_ref, dv_ref, p_s, sg_s):
        qi = pl.program_id(1)

        @pl.when(qi == 0)
        def _():
            dk_ref[...] = jnp.zeros(dk_ref.shape, F32)
            dv_ref[...] = jnp.zeros(dv_ref.shape, F32)

        h0 = lax.broadcasted_iota(jnp.int32, (1, LANES), 1) < SB_HEAD_DIM
        q2 = _stack_heads(q_ref[...] * scale, h0)
        do2 = _stack_heads(do_ref[...], h0)
        row = lax.broadcasted_iota(jnp.int32, (bk, bk), 0)
        col = lax.broadcasted_iota(jnp.int32, (bk, bk), 1)
        later_incl = (row >= col).astype(BF16)
        zero = jnp.zeros((2 * bq, 1), F32)
        tn = (((0,), (0,)), ((), ()))

        def visit1(j, mask, carry):
            r, sp = carry
            rows = pl.ds(pl.multiple_of(j * bk, bk), bk)
            z = lax.dot_general(q2, k_ref[rows, :].astype(BF16), nt, preferred_element_type=F32)
            da = lax.dot_general(do2, v_ref[rows, :].astype(BF16), nt, preferred_element_type=F32)
            a, sg, r = _sb_tile(z, mask, r)
            p = a * da
            p_s[j] = p
            sg_s[j] = sg
            dv_ref[rows, :] += lax.dot_general(a.astype(BF16), do2, tn, preferred_element_type=F32)
            return r, sp + jnp.sum(p, axis=1, keepdims=True)
        _, total = _sb_sweep(bq, bk, qi, visit1, (zero, zero))

        def visit2(j, mask, carry):
            dq, sp = carry
            rows = pl.ds(pl.multiple_of(j * bk, bk), bk)
            p, sg = p_s[j], sg_s[j]
            pref = total - sp - _dot01(p, later_incl)
            dz = p * (1.0 - sg) - pref * sg
            if mask is not None:
                dz = jnp.where(mask, dz, 0.0)
            dz = dz.astype(BF16)
            dk_ref[rows, :] += lax.dot_general(dz, q2, tn, preferred_element_type=F32)
            dq = dq + jnp.dot(_side_by_side(dz, bq), _stack_heads(k_ref[rows, :], h0), preferred_element_type=F32)
            return dq, sp + jnp.sum(p, axis=1, keepdims=True)
        dq, _ = _sb_sweep(bq, bk, qi, visit2, (jnp.zeros((bq, LANES), F32), zero))
        dq_ref[...] = dq * scale

    qs, ks, vs = _sb_specs(t, d, bq, 2 * d // LANES)
    shp = jax.ShapeDtypeStruct((t, d), F32)
    scratch = [pltpu.VMEM((t // bk, 2 * bq, bk), F32), pltpu.VMEM((t // bk, 2 * bq, bk), F32)]
    return pl.pallas_call(
        body, grid=(d // LANES, t // bq), in_specs=[qs, ks, vs, qs], out_specs=[qs, ks, ks],
        out_shape=[shp, shp, shp], scratch_shapes=scratch, name=name,
        compiler_params=_cparams(("parallel", "arbitrary")))(qn, kn, qkv, do)


def _adamw(w, g, m, v, name):
    shape = w.shape
    two_d = lambda a: a.reshape(-1, shape[-1])
    c1 = 1.0 - ADAM_B1 ** ADAM_STEP
    c2 = 1.0 - ADAM_B2 ** ADAM_STEP

    def body(r, _):
        wv, gv, mv, vv = r
        mn = ADAM_B1 * mv + (1.0 - ADAM_B1) * gv
        vn = ADAM_B2 * vv + (1.0 - ADAM_B2) * (gv * gv)
        delta = -ADAM_LR * ((mn / c1) / (jnp.sqrt(vn / c2) + ADAM_EPS) + ADAM_WD * wv)
        return [delta, mn, vn], []
    width = shape[-1]
    res = _rowwise(body, [two_d(w), two_d(g), two_d(m), two_d(v)], [], [(width, F32)] * 3, [], name=name)
    return [r.reshape(shape) for r in res]


def _dn_layer_fwd(h1, w, conv_w, a_log, dt_bias, onorm_g, tag):
    t, d = h1.shape
    heads = d // DN_HEAD_DIM
    n = t // DN_CHUNK
    proj = _mm(h1, w["all"], name=tag + "_proj")
    qn, kn, vv = _dn_prep_fwd(proj, conv_w, heads, tag + "_prep")
    big_g, beta = _dn_gate_fwd(proj, 4 * d // (2 * LANES), a_log, dt_bias, tag + "_gate")
    gt_ = big_g[:, :heads].T.reshape(heads, n, DN_CHUNK)
    gc, gr = gt_[..., None], gt_[:, :, None, :]
    bc = beta[:, :heads].T.reshape(heads, n, DN_CHUNK)[..., None]
    o, s_all = _dn_chunk_fwd(qn, kn, vv, gc, gr, bc, tag + "_chunk")
    on = _dn_post_fwd(o, proj, onorm_g, tag + "_post")
    y = _mm(on, w["out"], name=tag + "_out")
    return y, dict(proj=proj, qn=qn, kn=kn, v=vv, gc=gc, gr=gr, bc=bc, o=o, s_all=s_all, on=on)


def _dn_layer_bwd(dy, h1, w, conv_w, a_log, dt_bias, onorm_g, sv, tag):
    t, d = h1.shape
    heads = d // DN_HEAD_DIM
    don = _mm(dy, w["out"], tb=True, name=tag + "_dout")
    g_out = _mm(sv["on"], dy, ta=True, out_dtype=BF16, name=tag + "_gwout")
    do, dz, g_on = _dn_post_bwd(sv["o"], sv["proj"], don, onorm_g, tag + "_dpost")
    dq, dk, dv, dgc, dgr, dbc = _dn_chunk_bwd(sv["qn"], sv["kn"], sv["v"], sv["gc"], sv["gr"], sv["bc"], sv["s_all"],
                                              do, tag + "_dchunk")
    pad = lambda a: jnp.pad(a.reshape(heads, t).T, ((0, 0), (0, LANES - heads)))
    d_big_g = pad(dgc) + pad(dgr)
    dab, g_alog, g_dt = _dn_gate_bwd(sv["proj"], 4 * d // (2 * LANES), d_big_g, pad(dbc), a_log, dt_bias, tag + "_dgate")
    dxq, dxk, dxv, wq, wk, wv = _dn_prep_bwd(sv["proj"], conv_w, dq, dk, dv, heads, tag + "_dprep")
    dproj = jnp.concatenate([dxq, dxk, dxv, dz, dab], axis=1)
    dh1 = _mm(dproj, w["all"], tb=True, name=tag + "_dh")
    g_all = _mm(h1, dproj, ta=True, out_dtype=BF16, name=tag + "_gwin")
    grads = dict(w_all=g_all, w_out=g_out, conv_w=jnp.concatenate([wq, wk, wv], axis=1), a_log=g_alog, dt_bias=g_dt,
                 onorm_g=g_on)
    return dh1, grads


def _sb_layer_fwd(h1, w, q_g, k_g, tag):
    t, d = h1.shape
    heads = d // SB_HEAD_DIM
    qkv = _mm(h1, w["qkv"], name=tag + "_proj")
    gq, gk = jnp.tile(q_g, (1, heads)), jnp.tile(k_g, (1, heads))
    qn, kn = _sb_norm_fwd(qkv, gq, gk, tag + "_norm")
    o = _sb_fwd(qn, kn, qkv, tag + "_attn")
    y = _mm(o, w["out"], name=tag + "_out")
    return y, dict(qkv=qkv, qn=qn, kn=kn, o=o, gq=gq, gk=gk)


def _sb_layer_bwd(dy, h1, w, q_g, k_g, sv, tag):
    t, d = h1.shape
    heads = d // SB_HEAD_DIM
    do = _mm(dy, w["out"], tb=True, name=tag + "_dout")
    g_out = _mm(sv["o"], dy, ta=True, out_dtype=BF16, name=tag + "_gwout")
    dqn, dkn, dv = _sb_bwd(sv["qn"], sv["kn"], sv["qkv"], do, tag + "_dattn")
    dqkv, g_q, g_k = _sb_norm_bwd(sv["qkv"], dqn, dkn, dv, sv["gq"], sv["gk"], tag + "_dnorm")
    fold = lambda g: jnp.sum(g.reshape(heads, SB_HEAD_DIM), axis=0, keepdims=True)
    dh1 = _mm(dqkv, w["qkv"], tb=True, name=tag + "_dh")
    g_qkv = _mm(h1, dqkv, ta=True, out_dtype=BF16, name=tag + "_gwin")
    return dh1, dict(w_qkv=g_qkv, w_out=g_out, q_norm_g=fold(g_q), k_norm_g=fold(g_k))


def _local_step(x, tgt, mod, norm1_g, norm2_g, dn, sb, ffn):
    depth = mod.shape[0]
    d = x.shape[1]
    saved = []
    for i in range(depth):
        mv = [mod[i:i + 1, j * d:(j + 1) * d] for j in range(N_MOD)]
        sh1, sc1, gt1, sh2, sc2, gt2 = mv
        tag = "l%d" % i
        h1 = _adaln_fwd(x, norm1_g[i:i + 1], sc1, sh1, tag + "_ln1")
        if i % 2 == 0:
            p = dn[i // 2]
            y, sv = _dn_layer_fwd(h1, p, p["conv_w"], p["a_log"], p["dt_bias"], p["onorm_g"], tag + "_dn")
        else:
            p = sb[i // 2]
            y, sv = _sb_layer_fwd(h1, p, p["q_g"], p["k_g"], tag + "_sb")
        x1 = _resid_fwd(x, y, gt1, tag + "_res1")
        h2 = _adaln_fwd(x1, norm2_g[i:i + 1], sc2, sh2, tag + "_ln2")
        u = _mm(h2, ffn[i]["w_in"], name=tag + "_ffn_in")
        a = _swiglu_fwd(u, tag + "_swiglu")
        y2 = _mm(a, ffn[i]["w_out"], name=tag + "_ffn_out")
        x2 = _resid_fwd(x1, y2, gt2, tag + "_res2")
        saved.append(dict(x0=x, h1=h1, y=y, mix=sv, x1=x1, h2=h2, u=u, a=a, y2=y2))
        x = x2

    loss, dx = _loss_fwd_bwd(x, tgt, "loss")

    dmod, dn1, dn2 = [None] * depth, [None] * depth, [None] * depth
    g_dn, g_sb, g_ffn = [None] * len(dn), [None] * len(sb), [None] * depth
    for i in reversed(range(depth)):
        s = saved[i]
        mv = [mod[i:i + 1, j * d:(j + 1) * d] for j in range(N_MOD)]
        sh1, sc1, gt1, sh2, sc2, gt2 = mv
        tag = "l%d" % i
        dy2, dgt2 = _resid_bwd(dx, s["y2"], gt2, tag + "_dres2")
        da = _mm(dy2, ffn[i]["w_out"], tb=True, name=tag + "_dffn_a")
        g_wout = _mm(s["a"], dy2, ta=True, out_dtype=BF16, name=tag + "_gffn_out")
        du = _swiglu_bwd(s["u"], da, tag + "_dswiglu")
        dh2 = _mm(du, ffn[i]["w_in"], tb=True, name=tag + "_dffn_h")
        g_win = _mm(s["h2"], du, ta=True, out_dtype=BF16, name=tag + "_gffn_in")
        g_ffn[i] = dict(w_in=g_win, w_out=g_wout)
        dx, dg2, dsc2, dsh2 = _adaln_bwd(s["x1"], dh2, dx, norm2_g[i:i + 1], sc2, tag + "_dln2")
        dy, dgt1 = _resid_bwd(dx, s["y"], gt1, tag + "_dres1")
        if i % 2 == 0:
            p = dn[i // 2]
            dh1, g_dn[i // 2] = _dn_layer_bwd(dy, s["h1"], p, p["conv_w"], p["a_log"], p["dt_bias"], p["onorm_g"],
                                               s["mix"], tag + "_dn")
        else:
            p = sb[i // 2]
            dh1, g_sb[i // 2] = _sb_layer_bwd(dy, s["h1"], p, p["q_g"], p["k_g"], s["mix"], tag + "_sb")
        dx, dg1, dsc1, dsh1 = _adaln_bwd(s["x0"], dh1, dx, norm1_g[i:i + 1], sc1, tag + "_dln1")
        dmod[i] = jnp.concatenate([dsh1, dsc1, dgt1, dsh2, dsc2, dgt2], axis=1)
        dn1[i], dn2[i] = dg1, dg2
    return (loss, dx, jnp.concatenate(dmod, axis=0), jnp.concatenate(dn1, axis=0), jnp.concatenate(dn2, axis=0),
            g_dn, g_sb, g_ffn)


def _axes():
    return lax.axis_index("x"), lax.axis_index("y"), lax.axis_index("c")


def _remote(src, dst, send_sem, recv_sem, dev):
    return pltpu.make_async_remote_copy(src_ref=src, dst_ref=dst, send_sem=send_sem, recv_sem=recv_sem,
                                        device_id=dev, device_id_type=MESH)


def _other_chips(x, y):
    return [(1 - x, y), (x, 1 - y), (1 - x, 1 - y)]


def _allgather_small(v, name):
    m, n = v.shape

    def body(x_ref, out_ref, send_sems, recv_sems, local_sem):
        x, y, c = _axes()
        me, sibling = (x, y, c), (x, y, 1 - c)
        chips = _other_chips(x, y)

        def rows(px, py, pc):
            return out_ref.at[pl.ds((4 * px + 2 * py + pc) * m, m), :]

        def copy(k, block, to, src=None):
            return _remote(rows(*block) if src is None else src, rows(*block), send_sems.at[k], recv_sems.at[k], to)

        mine = pltpu.make_async_copy(x_ref, rows(*me), local_sem)
        mine.start()
        first = [copy(0, me, sibling, src=x_ref)]
        first += [copy(1 + j, me, (*chip, c), src=x_ref) for j, chip in enumerate(chips)]
        for cp in first:
            cp.start()
        passed = [copy(4 + j, (*chip, c), sibling) for j, chip in enumerate(chips)]
        for j, chip in enumerate(chips):
            copy(1 + j, (*chip, c), me).wait_recv()
            passed[j].start()
        copy(0, sibling, me).wait_recv()
        for j, chip in enumerate(chips):
            copy(4 + j, (*chip, 1 - c), me).wait_recv()
        for cp in first + passed:
            cp.wait_send()
        mine.wait()

    return pl.pallas_call(
        body, out_shape=jax.ShapeDtypeStruct((8 * m, n), v.dtype),
        in_specs=[pl.BlockSpec(memory_space=pltpu.VMEM)], out_specs=pl.BlockSpec(memory_space=pltpu.VMEM),
        scratch_shapes=[pltpu.SemaphoreType.DMA((7,)), pltpu.SemaphoreType.DMA((7,)), pltpu.SemaphoreType.DMA],
        name=name, compiler_params=pltpu.CompilerParams(vmem_limit_bytes=VMEM_LIMIT))(v)


def _half(ref, h, rh):
    return ref.at[(slice(None),) * (len(ref.shape) - 2) + (pl.ds(h * rh, rh), slice(None))]


def _hbm_call(body, ins, out_shapes, n_sems, n_local, name):
    hbm = pl.BlockSpec(memory_space=pltpu.HBM)
    scratch = [pltpu.SemaphoreType.DMA((n_sems,)), pltpu.SemaphoreType.DMA((n_sems,))]
    if n_local:
        scratch.append(pltpu.SemaphoreType.DMA((n_local,)))
    return pl.pallas_call(body, out_shape=out_shapes, in_specs=[hbm] * len(ins), out_specs=[hbm] * len(out_shapes),
                          scratch_shapes=scratch, name=name)(*ins)


def _allgather_weights(shards, name):
    n = len(shards)

    def body(*refs):
        p_refs, out_refs, (send_sems, recv_sems, local_sems) = refs[:n], refs[n:2 * n], refs[2 * n:]
        x, y, c = _axes()
        sibling = (x, y, 1 - c)
        chips = _other_chips(x, y)
        rh = [s.shape[1] // 2 for s in shards]
        blk = lambda i, cx, cy, h: _half(out_refs[i].at[2 * cx + cy], h, rh[i])
        mine = [pltpu.make_async_copy(p_refs[i], out_refs[i].at[2 * x + y], local_sems.at[i]) for i in range(n)]
        for cp in mine:
            cp.start()
        first = [_remote(_half(p_refs[i], c, rh[i]), blk(i, x, y, c), send_sems.at[6 * i + j], recv_sems.at[6 * i + j],
                         (*chip, c)) for i in range(n) for j, chip in enumerate(chips)]
        for cp in first:
            cp.start()
        passed = []
        for j, chip in enumerate(chips):
            for i in range(n):
                _remote(_half(p_refs[i], c, rh[i]), blk(i, *chip, c), send_sems.at[6 * i + j], recv_sems.at[6 * i + j],
                        sibling).wait_recv()
                passed.append(_remote(blk(i, *chip, c), blk(i, *chip, c), send_sems.at[6 * i + 3 + j],
                                      recv_sems.at[6 * i + 3 + j], sibling))
                passed[-1].start()
        for j, chip in enumerate(chips):
            for i in range(n):
                _remote(_half(p_refs[i], c, rh[i]), blk(i, *chip, 1 - c), send_sems.at[6 * i + 3 + j],
                        recv_sems.at[6 * i + 3 + j], sibling).wait_recv()
        for cp in first + passed:
            cp.wait_send()
        for cp in mine:
            cp.wait()

    outs = [jax.ShapeDtypeStruct((4,) + s.shape, s.dtype) for s in shards]
    return _hbm_call(body, shards, outs, 6 * n, n, name)


def _sibling_swap_halves(arrs, name):
    n = len(arrs)

    def body(*refs):
        v_refs, out_refs, (send_sems, recv_sems) = refs[:n], refs[n:2 * n], refs[2 * n:]
        x, y, c = _axes()
        cps = [_remote(_half(v_refs[i], 1 - c, arrs[i].shape[-2] // 2), out_refs[i], send_sems.at[i], recv_sems.at[i],
                       (x, y, 1 - c)) for i in range(n)]
        for cp in cps:
            cp.start()
        for cp in cps:
            cp.wait()

    outs = [jax.ShapeDtypeStruct(a.shape[:-2] + (a.shape[-2] // 2, a.shape[-1]), a.dtype) for a in arrs]
    return _hbm_call(body, arrs, outs, n, 0, name)


def _chip_scatter(arrs, name):
    n = len(arrs)

    def body(*refs):
        v_refs, out_refs, (send_sems, recv_sems, local_sems) = refs[:n], refs[n:2 * n], refs[2 * n:]
        x, y, c = _axes()
        me = 2 * x + y
        chips = _other_chips(x, y)
        mine = [pltpu.make_async_copy(v_refs[i].at[me], out_refs[i].at[me], local_sems.at[i]) for i in range(n)]
        for cp in mine:
            cp.start()
        sends = [_remote(v_refs[i].at[2 * cx + cy], out_refs[i].at[me], send_sems.at[3 * i + j], recv_sems.at[3 * i + j],
                         (cx, cy, c)) for i in range(n) for j, (cx, cy) in enumerate(chips)]
        for cp in sends:
            cp.start()
        for i in range(n):
            for j, (cx, cy) in enumerate(chips):
                _remote(v_refs[i].at[me], out_refs[i].at[2 * cx + cy], send_sems.at[3 * i + j], recv_sems.at[3 * i + j],
                        (cx, cy, c)).wait_recv()
        for cp in sends:
            cp.wait_send()
        for cp in mine:
            cp.wait()

    outs = [jax.ShapeDtypeStruct(a.shape, a.dtype) for a in arrs]
    return _hbm_call(body, arrs, outs, 3 * n, n, name)


def _share_halves(arrs, name):
    n = len(arrs)

    def body(*refs):
        v_refs, out_refs, (send_sems, recv_sems, local_sems) = refs[:n], refs[n:2 * n], refs[2 * n:]
        x, y, c = _axes()
        rh = [a.shape[-2] for a in arrs]
        mine = [pltpu.make_async_copy(v_refs[i], _half(out_refs[i], c, rh[i]), local_sems.at[i]) for i in range(n)]
        for cp in mine:
            cp.start()
        cps = [_remote(v_refs[i], _half(out_refs[i], c, rh[i]), send_sems.at[i], recv_sems.at[i], (x, y, 1 - c))
               for i in range(n)]
        for cp in cps:
            cp.start()
        for i in range(n):
            _remote(v_refs[i], _half(out_refs[i], 1 - c, rh[i]), send_sems.at[i], recv_sems.at[i],
                    (x, y, 1 - c)).wait_recv()
        for cp in cps:
            cp.wait_send()
        for cp in mine:
            cp.wait()

    outs = [jax.ShapeDtypeStruct(a.shape[:-2] + (2 * a.shape[-2], a.shape[-1]), a.dtype) for a in arrs]
    return _hbm_call(body, arrs, outs, n, n, name)


def _sum_lead(v, name):
    k, r, w = v.shape
    tr = _pick_tile(r, (256, 128, 64, 32, 16))

    def body(v_ref, o_ref):
        acc = v_ref[0].astype(F32)
        for i in range(1, k):
            acc = acc + v_ref[i].astype(F32)
        o_ref[...] = acc

    return pl.pallas_call(
        body, grid=(r // tr,), in_specs=[pl.BlockSpec((k, tr, w), lambda i: (0, i, 0))],
        out_specs=pl.BlockSpec((tr, w), lambda i: (i, 0)), out_shape=jax.ShapeDtypeStruct((r, w), F32), name=name,
        compiler_params=_cparams(("parallel",)))(v)


_BIG = (("dn_w_in", 2), ("dn_w_out", 1), ("sb_w_qkv", 2), ("sb_w_out", 1), ("ffn_w_in", 2), ("ffn_w_out", 1))


def _to_chip_major(a, axis, nchips):
    l, r, c = a.shape
    if axis == 2:
        return a.reshape(l, r, nchips, c // nchips).transpose(2, 0, 1, 3)
    return a.reshape(l, nchips, r // nchips, c).transpose(1, 0, 2, 3)


def _from_chip_major(a, axis):
    n, l, r, c = a.shape
    if axis == 2:
        return a.transpose(1, 2, 0, 3).reshape(l, r, n * c)
    return a.transpose(1, 0, 2, 3).reshape(l, n * r, c)


def kernel(x, c, ada_w, ada_b, norm1_g, norm2_g, dn_w_in, dn_conv_w, dn_a_log, dn_dt_bias, dn_onorm_g, dn_w_out, sb_w_qkv, sb_q_norm_g, sb_k_norm_g, sb_w_out, ffn_w_in, ffn_w_out, loss_target, m_ada_w, m_ada_b, m_norm1_g, m_norm2_g, m_dn_w_in, m_dn_conv_w, m_dn_a_log, m_dn_dt_bias, m_dn_onorm_g, m_dn_w_out, m_sb_w_qkv, m_sb_q_norm_g, m_sb_k_norm_g, m_sb_w_out, m_ffn_w_in, m_ffn_w_out, v_ada_w, v_ada_b, v_norm1_g, v_norm2_g, v_dn_w_in, v_dn_conv_w, v_dn_a_log, v_dn_dt_bias, v_dn_onorm_g, v_dn_w_out, v_sb_w_qkv, v_sb_q_norm_g, v_sb_k_norm_g, v_sb_w_out, v_ffn_w_in, v_ffn_w_out):
    names = ("ada_w", "ada_b", "norm1_g", "norm2_g", "dn_w_in", "dn_conv_w", "dn_a_log", "dn_dt_bias", "dn_onorm_g",
             "dn_w_out", "sb_w_qkv", "sb_q_norm_g", "sb_k_norm_g", "sb_w_out", "ffn_w_in", "ffn_w_out")
    w = dict(zip(names, (ada_w, ada_b, norm1_g, norm2_g, dn_w_in, dn_conv_w, dn_a_log, dn_dt_bias, dn_onorm_g,
                         dn_w_out, sb_w_qkv, sb_q_norm_g, sb_k_norm_g, sb_w_out, ffn_w_in, ffn_w_out)))
    mom = dict(zip(names, (m_ada_w, m_ada_b, m_norm1_g, m_norm2_g, m_dn_w_in, m_dn_conv_w, m_dn_a_log, m_dn_dt_bias,
                           m_dn_onorm_g, m_dn_w_out, m_sb_w_qkv, m_sb_q_norm_g, m_sb_k_norm_g, m_sb_w_out, m_ffn_w_in,
                           m_ffn_w_out)))
    var = dict(zip(names, (v_ada_w, v_ada_b, v_norm1_g, v_norm2_g, v_dn_w_in, v_dn_conv_w, v_dn_a_log, v_dn_dt_bias,
                           v_dn_onorm_g, v_dn_w_out, v_sb_w_qkv, v_sb_q_norm_g, v_sb_k_norm_g, v_sb_w_out, v_ffn_w_in,
                           v_ffn_w_out)))
    ax, ay, ac = _axes()
    chip = 2 * ax + ay
    dev = 2 * chip + ac
    t, d = x.shape[1], x.shape[2]
    depth, ndn, nsb = ada_w.shape[0], dn_w_in.shape[0], sb_w_qkv.shape[0]
    heads = d // DN_HEAD_DIM
    mod_cols = ada_w.shape[2]
    conv_cols = dn_conv_w.shape[2]
    nchips, ndev = 4, 8

    conv_rows = ndn * DN_CONV * conv_cols // d
    pay1 = jnp.concatenate([c, dn_conv_w.reshape(conv_rows, d), jnp.zeros((8 - 1 - conv_rows, d), F32)], axis=0)
    g1 = _allgather_small(pay1, "ag_cond").reshape(ndev, 8, d)
    c_all = g1[:, 0]
    conv_full = g1[::2, 1:1 + conv_rows].reshape(nchips, ndn, DN_CONV, conv_cols).transpose(1, 2, 0, 3)
    conv_full = conv_full.reshape(ndn, DN_CONV, nchips * conv_cols)

    c16 = jnp.pad(c_all, ((0, 16 - ndev), (0, 0)))
    cond16 = _rowwise(lambda r, v: ([_silu(r[0])], []), [c16], [], [(d, F32)], [], name="cond_silu")[0]
    pay2 = jnp.concatenate([_mm(cond16, ada_w[i], name="ada_mod%d" % i)[:ndev] for i in range(depth)], axis=0)
    g2 = _allgather_small(pay2, "ag_mod").reshape(ndev, depth, ndev, mod_cols)[::2]
    mod_raw = lax.dynamic_index_in_dim(g2, dev, axis=2, keepdims=False)
    mod_raw = mod_raw.transpose(1, 0, 2).reshape(depth, nchips * mod_cols)
    mod = _rowwise(lambda r, v: ([r[0] + r[1]], []), [mod_raw, ada_b], [], [(nchips * mod_cols, F32)], [],
                   name="ada_bias")[0]

    gathered = _allgather_weights([w[n].astype(BF16) for n, _ in _BIG], "ag_weights")
    full = {n: _from_chip_major(a, axis) for (n, axis), a in zip(_BIG, gathered)}

    padl = lambda v: jnp.pad(v[None, :], ((0, 0), (0, LANES - v.shape[0])))
    padc = lambda a: jnp.pad(a, ((0, 0), (0, LANES - a.shape[1])))
    dn = []
    for j in range(ndn):
        wi = full["dn_w_in"][j]
        w_all = jnp.concatenate([wi[:, :4 * d], padc(wi[:, 4 * d:4 * d + heads]), padc(wi[:, 4 * d + heads:])], axis=1)
        dn.append(dict(all=w_all, out=full["dn_w_out"][j], conv_w=conv_full[j], a_log=padl(dn_a_log[j]),
                       dt_bias=padl(dn_dt_bias[j]), onorm_g=dn_onorm_g[j][None]))
    sb = [dict(qkv=full["sb_w_qkv"][j], out=full["sb_w_out"][j], q_g=sb_q_norm_g[j][None], k_g=sb_k_norm_g[j][None])
          for j in range(nsb)]
    ffn = [dict(w_in=full["ffn_w_in"][i], w_out=full["ffn_w_out"][i]) for i in range(depth)]

    loss_local, grad_x, dmod, g_n1, g_n2, g_dn, g_sb, g_ffn = _local_step(x[0], loss_target[0], mod, norm1_g, norm2_g,
                                                                           dn, sb, ffn)
    loss = lax.psum(loss_local, ("x", "y", "c"))

    g_conv = jnp.stack([g["conv_w"] for g in g_dn])
    misc = jnp.concatenate([jnp.concatenate([g["onorm_g"] for g in g_dn], axis=1),
                            jnp.concatenate([g["a_log"] for g in g_dn], axis=1),
                            jnp.concatenate([g["dt_bias"] for g in g_dn], axis=1),
                            jnp.concatenate([g["q_norm_g"] for g in g_sb], axis=1),
                            jnp.concatenate([g["k_norm_g"] for g in g_sb], axis=1)], axis=1)
    misc = jnp.pad(misc, ((0, 0), (0, -misc.shape[1] % d))).reshape(-1, d)
    small = [dmod.reshape(-1, d), g_n1, g_n2, g_conv.reshape(-1, d), misc]
    small_rows = [s.shape[0] for s in small]
    pad_rows = -sum(small_rows) % 8
    pay3 = jnp.concatenate(small + [jnp.zeros((pad_rows, d), F32)], axis=0)
    nrow3 = pay3.shape[0]
    g3 = _allgather_small(pay3, "ag_small")
    summed = _rowwise(lambda r, v: ([], [_colsum(r[0])]), [g3.reshape(ndev, nrow3 * d)], [], [], [(1, nrow3 * d)],
                      name="small_sum")[0].reshape(nrow3, d)
    offs = [0]
    for n_ in small_rows:
        offs.append(offs[-1] + n_)
    grads = {}
    grads["ada_b"] = summed[offs[0]:offs[1]].reshape(depth, N_MOD * d)
    grads["norm1_g"] = summed[offs[1]:offs[2]]
    grads["norm2_g"] = summed[offs[2]:offs[3]]
    conv_sum = summed[offs[3]:offs[4]].reshape(ndn, DN_CONV, nchips * conv_cols)
    grads["dn_conv_w"] = lax.dynamic_slice_in_dim(conv_sum, chip * conv_cols, conv_cols, axis=2)
    mrow = summed[offs[4]:offs[5]].reshape(-1)
    o = 0
    grads["dn_onorm_g"] = mrow[o:o + ndn * DN_HEAD_DIM].reshape(ndn, DN_HEAD_DIM)
    o += ndn * DN_HEAD_DIM
    grads["dn_a_log"] = mrow[o:o + ndn * LANES].reshape(ndn, LANES)[:, :heads]
    o += ndn * LANES
    grads["dn_dt_bias"] = mrow[o:o + ndn * LANES].reshape(ndn, LANES)[:, :heads]
    o += ndn * LANES
    grads["sb_q_norm_g"] = mrow[o:o + nsb * SB_HEAD_DIM].reshape(nsb, SB_HEAD_DIM)
    o += nsb * SB_HEAD_DIM
    grads["sb_k_norm_g"] = mrow[o:o + nsb * SB_HEAD_DIM].reshape(nsb, SB_HEAD_DIM)
    dmod_all = g3.reshape(ndev, nrow3, d)[:, :small_rows[0]].reshape(ndev, depth, N_MOD * d)
    dmod_mine = lax.dynamic_slice_in_dim(dmod_all, chip * mod_cols, mod_cols, axis=2)
    dmod16 = jnp.pad(dmod_mine, ((0, 16 - ndev), (0, 0), (0, 0)))
    grads["ada_w"] = jnp.stack([_mm(cond16, dmod16[:, i], ta=True, name="ada_gw%d" % i) for i in range(depth)])

    def gw_in(g):
        ga = g["w_all"]
        return jnp.concatenate([ga[:, :4 * d], ga[:, 4 * d:4 * d + heads], ga[:, 4 * d + LANES:4 * d + LANES + heads]],
                               axis=1)
    gfull = dict(dn_w_in=jnp.stack([gw_in(g) for g in g_dn]), dn_w_out=jnp.stack([g["w_out"] for g in g_dn]),
                 sb_w_qkv=jnp.stack([g["w_qkv"] for g in g_sb]), sb_w_out=jnp.stack([g["w_out"] for g in g_sb]),
                 ffn_w_in=jnp.stack([g["w_in"] for g in g_ffn]), ffn_w_out=jnp.stack([g["w_out"] for g in g_ffn]))
    parts = [_to_chip_major(gfull[n], axis, nchips) for n, axis in _BIG]
    from_sibling = _sibling_swap_halves(parts, "gr_pair")
    two_d = lambda a: a.reshape(-1, a.shape[-1])
    pairs = []
    for (n, _), p, fs in zip(_BIG, parts, from_sibling):
        rh = fs.shape[2]
        own = lax.dynamic_slice_in_dim(p, ac * rh, rh, axis=2)
        pairs.append(_rowwise(lambda r, v: ([r[0].astype(F32) + r[1].astype(F32)], []), [two_d(own), two_d(fs)], [],
                              [(fs.shape[-1], BF16)], [], name="gr_pair_add_" + n)[0].reshape(fs.shape))
    from_chips = _chip_scatter(pairs, "gr_chips")
    reduced = [_sum_lead(fc.reshape(nchips, -1, fc.shape[-1]), "gr_chip_add_" + n).reshape(fc.shape[1:])
               for (n, _), fc in zip(_BIG, from_chips)]
    for (n, _), g in zip(_BIG, _share_halves(reduced, "gr_share")):
        grads[n] = g

    delta, new_m, new_v = {}, {}, {}
    for n in names:
        delta[n], new_m[n], new_v[n] = _adamw(w[n], grads[n], mom[n], var[n], "adamw_" + n)
    return (loss, grad_x[None], *[grads[n] for n in names], *[delta[n] for n in names], *[new_m[n] for n in names],
            *[new_v[n] for n in names])
```

```python
import functools

import jax
import jax.numpy as jnp
from jax import lax
from jax.experimental import pallas as pl
from jax.experimental.pallas import tpu as pltpu

F32 = jnp.float32
BF16 = jnp.bfloat16
HI = lax.Precision.HIGHEST
MESH = pl.DeviceIdType.MESH

EPS = 1e-6
N_MOD = 6
DN_HEAD_DIM = 128
DN_CONV = 4
DN_CHUNK = 64
DN_GROUP = 8
DN_GATE_ROWS = 256
SB_HEAD_DIM = 64
SB_BLOCK = 128
SB_QBLOCK = 256
SB_PART = 128
LANES = 128
VMEM_LIMIT = 56 * 1024 * 1024
MM_BLOCK_BUDGET = 36 * 1024 * 1024
ROW_BLOCK_BUDGET = 20 * 1024 * 1024

ADAM_LR = 0.001
ADAM_B1 = 0.9
ADAM_B2 = 0.999
ADAM_EPS = 1e-08
ADAM_WD = 0.01
ADAM_STEP = 10


def _cparams(sem=None):
    return pltpu.CompilerParams(dimension_semantics=sem, vmem_limit_bytes=VMEM_LIMIT)


def _sigmoid(x):
    return 1.0 / (1.0 + jnp.exp(-x))


def _silu(x):
    return x * _sigmoid(x)


def _dsilu(x):
    s = _sigmoid(x)
    return s * (1.0 + x * (1.0 - s))


def _softplus(x):
    return jnp.maximum(x, 0.0) + jnp.log1p(jnp.exp(-jnp.abs(x)))


def _pick_tile(n, prefs):
    for t in prefs:
        if n % t == 0:
            return t
    return n


def _mm(a, b, *, ta=False, tb=False, out_dtype=F32, name):
    m = a.shape[1] if ta else a.shape[0]
    k = a.shape[0] if ta else a.shape[1]
    n = b.shape[0] if tb else b.shape[1]
    assert (b.shape[1] if tb else b.shape[0]) == k
    size = lambda dt: jnp.dtype(dt).itemsize
    best = None
    for tm in sorted({t for t in (m, 2048, 1024, 512, 256, 128) if m % t == 0 and (t % 128 == 0 or t == m)}):
        for tn in sorted({t for t in (n, 2816, 1408, 1024, 768, 512, 256, 128) if n % t == 0 and (t % 128 == 0 or t == n)}):
            need = 2 * (tm * k * size(a.dtype) + tn * k * size(b.dtype) + tm * tn * size(out_dtype))
            if need <= MM_BLOCK_BUDGET and (best is None or tm * tn > best[0] * best[1]):
                best = (tm, tn)
    tm, tn = best
    dims = (((0 if ta else 1,), (1 if tb else 0,)), ((), ()))

    def body(a_ref, b_ref, o_ref):
        av = a_ref[...].astype(BF16)
        bv = b_ref[...].astype(BF16)
        o_ref[...] = lax.dot_general(av, bv, dims, preferred_element_type=F32).astype(o_ref.dtype)

    a_spec = pl.BlockSpec((k, tm), lambda i, j: (0, i)) if ta else pl.BlockSpec((tm, k), lambda i, j: (i, 0))
    b_spec = pl.BlockSpec((tn, k), lambda i, j: (j, 0)) if tb else pl.BlockSpec((k, tn), lambda i, j: (0, j))
    return pl.pallas_call(
        body, grid=(m // tm, n // tn), in_specs=[a_spec, b_spec],
        out_specs=pl.BlockSpec((tm, tn), lambda i, j: (i, j)),
        out_shape=jax.ShapeDtypeStruct((m, n), out_dtype), name=name,
        compiler_params=_cparams(("parallel", "parallel")))(a, b)


def _rowwise(body, rows, vecs, outs, accs, *, name, tr=None):
    rows = [r if isinstance(r, tuple) else (r, r.shape[1], 0) for r in rows]
    nrows = rows[0][0].shape[0]
    if tr is None:
        per_row = sum(w * jnp.dtype(a.dtype).itemsize for a, w, _ in rows) + sum(
            w * jnp.dtype(dt).itemsize for w, dt in outs)
        tr = next((t for t in (2048, 1024, 512, 256) if nrows % t == 0 and 2 * t * per_row <= ROW_BLOCK_BUDGET), 256)
    tr = tr if nrows % tr == 0 else nrows
    nr, nv, no = len(rows), len(vecs), len(outs)

    def kern(*refs):
        r_in, v_in = refs[:nr], refs[nr:nr + nv]
        o_refs, a_refs = refs[nr + nv:nr + nv + no], refs[nr + nv + no:]
        res_o, res_a = body([r[...] for r in r_in], [v[...] for v in v_in])
        for o, val in zip(o_refs, res_o):
            o[...] = val.astype(o.dtype)
        if a_refs:
            @pl.when(pl.program_id(0) == 0)
            def _():
                for a in a_refs:
                    a[...] = jnp.zeros(a.shape, a.dtype)
            for a, val in zip(a_refs, res_a):
                a[...] += val

    in_specs = [pl.BlockSpec((tr, w), functools.partial(lambda i, cb: (i, cb), cb=cb)) for _, w, cb in rows]
    in_specs += [pl.BlockSpec(v.shape, functools.partial(lambda i, nd: (0,) * nd, nd=v.ndim)) for v in vecs]
    out_specs = [pl.BlockSpec((tr, w), lambda i: (i, 0)) for w, _ in outs]
    out_specs += [pl.BlockSpec(s, lambda i: (0, 0)) for s in accs]
    out_shape = [jax.ShapeDtypeStruct((nrows, w), dt) for w, dt in outs]
    out_shape += [jax.ShapeDtypeStruct(s, F32) for s in accs]
    res = pl.pallas_call(
        kern, grid=(nrows // tr,), in_specs=in_specs, out_specs=out_specs, out_shape=out_shape, name=name,
        compiler_params=_cparams(("arbitrary",) if accs else ("parallel",)))(*[r[0] for r in rows], *vecs)
    return res


def _colsum(v):
    return jnp.sum(v, axis=0, keepdims=True)


def _adaln_fwd(x, g, sc, sh, name):
    def body(r, v):
        (xv,), (gv, scv, shv) = r, v
        rs = lax.rsqrt(jnp.mean(xv * xv, axis=-1, keepdims=True) + EPS)
        return [(xv * rs) * gv * (1.0 + scv) + shv], []
    return _rowwise(body, [x], [g, sc, sh], [(x.shape[1], BF16)], [], name=name)[0]


def _adaln_bwd(x, dh, dxr, g, sc, name):
    d = x.shape[1]

    def body(r, v):
        (xv, dhv, dxv), (gv, scv) = r, v
        rs = lax.rsqrt(jnp.mean(xv * xv, axis=-1, keepdims=True) + EPS)
        nv = xv * rs
        dn = dhv * (gv * (1.0 + scv))
        dx = rs * (dn - nv * jnp.mean(dn * nv, axis=-1, keepdims=True)) + dxv
        dhn = dhv * nv
        return [dx], [_colsum(dhn * (1.0 + scv)), _colsum(dhn * gv), _colsum(dhv)]
    return _rowwise(body, [x, dh, dxr], [g, sc], [(d, F32)], [(1, d)] * 3, name=name)


def _resid_fwd(x, y, gt, name):
    def body(r, v):
        return [r[0] + v[0] * r[1]], []
    return _rowwise(body, [x, y], [gt], [(x.shape[1], F32)], [], name=name)[0]


def _resid_bwd(dx, y, gt, name):
    d = dx.shape[1]

    def body(r, v):
        return [v[0] * r[0]], [_colsum(r[0] * r[1])]
    return _rowwise(body, [dx, y], [gt], [(d, BF16)], [(1, d)], name=name)


def _swiglu_fwd(u, name):
    f = u.shape[1] // 2

    def body(r, v):
        uv = r[0]
        return [_silu(uv[:, :f]) * uv[:, f:]], []
    return _rowwise(body, [u], [], [(f, BF16)], [], name=name)[0]


def _swiglu_bwd(u, da, name):
    f = u.shape[1] // 2

    def body(r, v):
        uv, dav = r
        gate, up = uv[:, :f], uv[:, f:]
        return [jnp.concatenate([dav * up * _dsilu(gate), dav * _silu(gate)], axis=1)], []
    return _rowwise(body, [u, da], [], [(2 * f, BF16)], [], name=name)[0]


def _loss_fwd_bwd(y, tgt, name):
    d = y.shape[1]

    def body(r, v):
        err = r[0] - r[1]
        part = jnp.sum(jnp.sum(err * err, axis=1, keepdims=True), axis=0, keepdims=True) * (0.5 / d)
        return [err * (1.0 / d)], [jnp.broadcast_to(part, (1, LANES))]
    dy, acc = _rowwise(body, [y, tgt], [], [(d, F32)], [(1, LANES)], name=name)
    return acc[0, 0], dy


def _head_means(v):
    row = lax.broadcasted_iota(jnp.int32, (LANES, LANES), 0)
    col = lax.broadcasted_iota(jnp.int32, (LANES, LANES), 1)
    same = ((row // SB_HEAD_DIM) == (col // SB_HEAD_DIM)).astype(F32)
    parts = [jnp.dot(v[:, g * LANES:(g + 1) * LANES], same, precision=HI, preferred_element_type=F32)
             for g in range(v.shape[1] // LANES)]
    return jnp.concatenate(parts, axis=1) * (1.0 / SB_HEAD_DIM)


def _sb_norm_fwd(qkv, gq, gk, name):
    d = qkv.shape[1] // 3

    def body(r, v):
        return [x * lax.rsqrt(_head_means(x * x) + EPS) * g for x, g in zip(r, v)], []
    return _rowwise(body, [(qkv, d, 0), (qkv, d, 1)], [gq, gk], [(d, F32), (d, F32)], [], name=name)


def _sb_norm_bwd(qkv, dqn, dkn, dv, gq, gk, name):
    d = qkv.shape[1] // 3

    def body(r, v):
        outs, accs = [], []
        for x, dy, g in ((r[0], r[2], v[0]), (r[1], r[3], v[1])):
            rs = lax.rsqrt(_head_means(x * x) + EPS)
            nv = x * rs
            dn = dy * g
            outs.append(rs * (dn - nv * _head_means(dn * nv)))
            accs.append(_colsum(dy * nv))
        return [jnp.concatenate(outs + [r[4]], axis=1)], accs
    return _rowwise(body, [(qkv, d, 0), (qkv, d, 1), dqn, dkn, dv], [gq, gk], [(3 * d, BF16)], [(1, d), (1, d)],
                    name=name)


def _head_tiles(a):
    return [a[:, h * DN_HEAD_DIM:(h + 1) * DN_HEAD_DIM] for h in range(a.shape[1] // DN_HEAD_DIM)]


def _dn_post_fwd(o, proj, g, name):
    d = o.shape[1]

    def body(r, v):
        outs = []
        for ov, zv in zip(_head_tiles(r[0]), _head_tiles(r[1])):
            rs = lax.rsqrt(jnp.mean(ov * ov, axis=-1, keepdims=True) + EPS)
            outs.append(ov * rs * v[0] * _silu(zv))
        return [jnp.concatenate(outs, axis=1)], []
    return _rowwise(body, [o, (proj, d, 3)], [g], [(d, BF16)], [], name=name)[0]


def _dn_post_bwd(o, proj, don, g, name):
    d = o.shape[1]

    def body(r, v):
        dos, dzs, dg = [], [], jnp.zeros((1, DN_HEAD_DIM), F32)
        for ov, zv, dv in zip(_head_tiles(r[0]), _head_tiles(r[1]), _head_tiles(r[2])):
            rs = lax.rsqrt(jnp.mean(ov * ov, axis=-1, keepdims=True) + EPS)
            nv = ov * rs
            s = _silu(zv)
            dn = dv * v[0] * s
            dos.append(rs * (dn - nv * jnp.mean(dn * nv, axis=-1, keepdims=True)))
            dzs.append(dv * nv * v[0] * _dsilu(zv))
            dg = dg + _colsum(dv * nv * s)
        return [jnp.concatenate(dos, axis=1), jnp.concatenate(dzs, axis=1)], [dg]
    return _rowwise(body, [o, (proj, d, 3), don], [g], [(d, F32), (d, BF16)], [(1, DN_HEAD_DIM)], name=name)


def _chunk_tri(tr, upper):
    row = lax.broadcasted_iota(jnp.int32, (tr, tr), 0)
    col = lax.broadcasted_iota(jnp.int32, (tr, tr), 1)
    same = (row // DN_CHUNK) == (col // DN_CHUNK)
    return (same & ((row <= col) if upper else (row >= col))).astype(F32)


def _dn_gate_fwd(proj, colblk, a_log, dt_bias, name):
    def body(r, v):
        ab = r[0]
        a, b = ab[:, :LANES], ab[:, LANES:]
        g = -jnp.exp(v[0]) * _softplus(a + v[1])
        big_g = jnp.dot(_chunk_tri(g.shape[0], False), g, precision=HI, preferred_element_type=F32)
        return [big_g, _sigmoid(b)], []
    return _rowwise(body, [(proj, 2 * LANES, colblk)], [a_log, dt_bias], [(LANES, F32), (LANES, F32)], [], name=name,
                    tr=DN_GATE_ROWS)


def _dn_gate_bwd(proj, colblk, d_big_g, dbeta, a_log, dt_bias, name):
    def body(r, v):
        ab, dgc, dbt = r
        a, b = ab[:, :LANES], ab[:, LANES:]
        dg = jnp.dot(_chunk_tri(dgc.shape[0], True), dgc, precision=HI, preferred_element_type=F32)
        na = -jnp.exp(v[0])
        pre = a + v[1]
        da = dg * na * _sigmoid(pre)
        beta = _sigmoid(b)
        db = dbt * beta * (1.0 - beta)
        return [jnp.concatenate([da, db], axis=1)], [_colsum(dg * na * _softplus(pre)), _colsum(da)]
    return _rowwise(body, [(proj, 2 * LANES, colblk), d_big_g, dbeta], [a_log, dt_bias],
                    [(2 * LANES, BF16)], [(1, LANES), (1, LANES)], name=name, tr=DN_GATE_ROWS)


def _shift_rows(x, s):
    if s == 0:
        return x
    t = x.shape[0]
    row = lax.broadcasted_iota(jnp.int32, x.shape, 0)
    rolled = pltpu.roll(x, s % t, axis=0)
    return jnp.where((row >= s) if s > 0 else (row < t + s), rolled, 0.0)


def _rnd(x):
    return x.astype(BF16).astype(F32)


def _conv(x, c_ref):
    c = x * _rnd(c_ref[DN_CONV - 1:DN_CONV, :])
    for s in range(1, DN_CONV):
        c = c + _shift_rows(x, s) * _rnd(c_ref[DN_CONV - 1 - s:DN_CONV - s, :])
    return c


def _dn_prep_fwd(proj, conv_w, heads, name):
    t = proj.shape[0]
    d = heads * DN_HEAD_DIM

    def body(xq, xk, xv, cq, ck, cv, q_ref, k_ref, v_ref):
        for x_ref, c_ref, o_ref, norm in ((xq, cq, q_ref, True), (xk, ck, k_ref, True), (xv, cv, v_ref, False)):
            y = _silu(_conv(_rnd(x_ref[...]), c_ref))
            if norm:
                y = y * lax.rsqrt(jnp.sum(y * y, axis=-1, keepdims=True) + EPS)
            o_ref[...] = y

    xs = [pl.BlockSpec((t, DN_HEAD_DIM), functools.partial(lambda h, o: (0, h + o), o=o * heads)) for o in range(3)]
    cs = [pl.BlockSpec((DN_CONV, DN_HEAD_DIM), functools.partial(lambda h, o: (0, h + o), o=o * heads)) for o in range(3)]
    return pl.pallas_call(
        body, grid=(heads,), in_specs=xs + cs,
        out_specs=[pl.BlockSpec((t, DN_HEAD_DIM), lambda h: (0, h))] * 3,
        out_shape=[jax.ShapeDtypeStruct((t, d), F32)] * 3, name=name,
        compiler_params=_cparams(("parallel",)))(proj, proj, proj, conv_w, conv_w, conv_w)


def _dn_prep_bwd(proj, conv_w, dq, dk, dv, heads, name):
    t = proj.shape[0]
    d = heads * DN_HEAD_DIM

    def body(xq, xk, xv, cq, ck, cv, gq, gk, gv, oq, ok, ov, wq, wk, wv):
        for x_ref, c_ref, g_ref, o_ref, w_ref, norm in ((xq, cq, gq, oq, wq, True), (xk, ck, gk, ok, wk, True),
                                                        (xv, cv, gv, ov, wv, False)):
            x, dy = _rnd(x_ref[...]), g_ref[...]
            c = _conv(x, c_ref)
            if norm:
                y = _silu(c)
                rs = lax.rsqrt(jnp.sum(y * y, axis=-1, keepdims=True) + EPS)
                yn = y * rs
                dy = rs * (dy - yn * jnp.sum(dy * yn, axis=-1, keepdims=True))
            dc = _rnd(dy * _dsilu(c))
            dx = dc * _rnd(c_ref[DN_CONV - 1:DN_CONV, :])
            w_ref[DN_CONV - 1:DN_CONV, :] = _colsum(dc * x)
            for s in range(1, DN_CONV):
                dx = dx + _shift_rows(dc, -s) * _rnd(c_ref[DN_CONV - 1 - s:DN_CONV - s, :])
                w_ref[DN_CONV - 1 - s:DN_CONV - s, :] = _colsum(dc * _shift_rows(x, s))
            o_ref[...] = dx.astype(o_ref.dtype)

    xs = [pl.BlockSpec((t, DN_HEAD_DIM), functools.partial(lambda h, o: (0, h + o), o=o * heads)) for o in range(3)]
    cs = [pl.BlockSpec((DN_CONV, DN_HEAD_DIM), functools.partial(lambda h, o: (0, h + o), o=o * heads)) for o in range(3)]
    hs = pl.BlockSpec((t, DN_HEAD_DIM), lambda h: (0, h))
    ws = pl.BlockSpec((DN_CONV, DN_HEAD_DIM), lambda h: (0, h))
    return pl.pallas_call(
        body, grid=(heads,), in_specs=xs + cs + [hs] * 3, out_specs=[hs] * 3 + [ws] * 3,
        out_shape=[jax.ShapeDtypeStruct((t, d), BF16)] * 3 + [jax.ShapeDtypeStruct((DN_CONV, d), F32)] * 3, name=name,
        compiler_params=_cparams(("parallel",)))(proj, proj, proj, conv_w, conv_w, conv_w, dq, dk, dv)


def _bmm(a, b, prec=None):
    return lax.dot_general(a, b, (((2,), (1,)), ((0,), (0,))), precision=prec, preferred_element_type=F32)


def _bmm_nt(a, b, prec=None):
    return lax.dot_general(a, b, (((2,), (2,)), ((0,), (0,))), precision=prec, preferred_element_type=F32)


def _bmm_tn(a, b, prec=None):
    return lax.dot_general(a, b, (((1,), (1,)), ((0,), (0,))), precision=prec, preferred_element_type=F32)


def _dn_local(qg, kg, vg, gc, gr, bt):
    c = qg.shape[1]
    row = lax.broadcasted_iota(jnp.int32, (c, c), 0)
    col = lax.broadcasted_iota(jnp.int32, (c, c), 1)
    incl, strict = (row >= col)[None], (row > col)[None]
    decay = jnp.where(incl, jnp.exp(jnp.where(incl, gc - gr, 0.0)), 0.0)
    kb = kg * bt
    vb = vg * bt
    m = _bmm_nt(kb.astype(BF16), kg.astype(BF16))
    a = jnp.where(strict, m * decay, 0.0)
    bp = -a
    tm = jnp.where((row == col)[None], 1.0, 0.0) + bp
    steps = max(1, (c - 1).bit_length()) - 1
    for _ in range(steps):
        bp = _bmm(bp, bp, HI)
        tm = tm + _bmm(tm, bp, HI)
    eg = jnp.exp(gc)
    glast = gc[:, c - 1:c, :]
    erel = jnp.exp(glast - gc)
    kbg = kb * eg
    qm = _bmm_nt(qg.astype(BF16), kg.astype(BF16))
    return dict(decay=decay, strict=strict, kb=kb, vb=vb, m=m, tm=tm, eg=eg, erel=erel, kbg=kbg, qm=qm,
                u=_bmm(tm, vb, HI), w=_bmm(tm, kbg, HI), qd=qg * eg, kt=kg * erel, gl=jnp.exp(glast))


def _dot(a, b):
    return jnp.dot(a.astype(BF16), b.astype(BF16), preferred_element_type=F32)


def _dot_nt(a, b):
    return lax.dot_general(a.astype(BF16), b.astype(BF16), (((1,), (1,)), ((), ())), preferred_element_type=F32)


def _dot_tn(a, b):
    return lax.dot_general(a.astype(BF16), b.astype(BF16), (((0,), (0,)), ((), ())), preferred_element_type=F32)


def _dn_chunk_specs(t, heads):
    n = t // DN_CHUNK
    hs = pl.BlockSpec((t, DN_HEAD_DIM), lambda h: (0, h))
    gcs = pl.BlockSpec((1, n, DN_CHUNK, 1), lambda h: (h, 0, 0, 0))
    grs = pl.BlockSpec((1, n, 1, DN_CHUNK), lambda h: (h, 0, 0, 0))
    ss = pl.BlockSpec((1, n, DN_HEAD_DIM, DN_HEAD_DIM), lambda h: (h, 0, 0, 0))
    return n, hs, gcs, grs, ss


def _dn_chunk_fwd(q, k, v, gc, gr, bc, name):
    t, d = q.shape
    heads = d // DN_HEAD_DIM
    c, dk = DN_CHUNK, DN_HEAD_DIM
    n, hs, gcs, grs, ss = _dn_chunk_specs(t, heads)
    nb = min(DN_GROUP, n)
    scale = dk ** -0.5

    def body(q_ref, k_ref, v_ref, gc_ref, gr_ref, b_ref, o_ref, s_ref, u_s, w_s, at_s, qd_s, kt_s, gl_s):
        def group(gi, carry):
            r0 = pl.multiple_of(gi * (nb * c), nb * c)
            n0 = gi * nb
            ld = lambda ref: ref[pl.ds(r0, nb * c), :].reshape(nb, c, dk)
            loc = _dn_local(ld(q_ref) * scale, ld(k_ref), ld(v_ref), gc_ref[0, pl.ds(n0, nb)],
                            gr_ref[0, pl.ds(n0, nb)], b_ref[0, pl.ds(n0, nb)])
            u_s[pl.ds(n0, nb)] = loc["u"]
            w_s[pl.ds(n0, nb)] = loc["w"]
            at_s[pl.ds(n0, nb)] = loc["qm"] * loc["decay"]
            qd_s[pl.ds(n0, nb)] = loc["qd"]
            kt_s[pl.ds(n0, nb)] = loc["kt"]
            gl_s[pl.ds(n0, nb)] = loc["gl"]
            return carry
        lax.fori_loop(0, n // nb, group, 0)

        def chunk(i, s):
            s_ref[0, i] = s
            vnew = u_s[i] - _dot(w_s[i], s)
            o = _dot(qd_s[i], s) + _dot(at_s[i], vnew)
            o_ref[pl.ds(pl.multiple_of(i * c, c), c), :] = o
            return s * gl_s[i] + _dot_tn(kt_s[i], vnew)
        lax.fori_loop(0, n, chunk, jnp.zeros((dk, dk), F32))

    scratch = [pltpu.VMEM((n, c, dk), F32), pltpu.VMEM((n, c, dk), F32), pltpu.VMEM((n, c, c), F32),
               pltpu.VMEM((n, c, dk), F32), pltpu.VMEM((n, c, dk), F32), pltpu.VMEM((n, 1, 1), F32)]
    return pl.pallas_call(
        body, grid=(heads,), in_specs=[hs, hs, hs, gcs, grs, gcs], out_specs=[hs, ss],
        out_shape=[jax.ShapeDtypeStruct((t, d), F32), jax.ShapeDtypeStruct((heads, n, dk, dk), F32)],
        scratch_shapes=scratch, name=name, compiler_params=_cparams(("parallel",)))(q, k, v, gc, gr, bc)


def _dn_chunk_bwd(q, k, v, gc, gr, bc, s_all, do, name):
    t, d = q.shape
    heads = d // DN_HEAD_DIM
    c, dk = DN_CHUNK, DN_HEAD_DIM
    n, hs, gcs, grs, ss = _dn_chunk_specs(t, heads)
    nb = min(DN_GROUP, n)
    scale = dk ** -0.5

    def body(q_ref, k_ref, v_ref, gc_ref, gr_ref, b_ref, s_ref, do_ref,
             dq_ref, dk_ref, dv_ref, dgc_ref, dgr_ref, db_ref,
             u_s, w_s, att_s, qd_s, kt_s, gl_s, du_s, dw_s, dat_s, dqd_s, dkt_s, dgl_s):
        def load_group(gi):
            r0 = pl.multiple_of(gi * (nb * c), nb * c)
            n0 = gi * nb
            ld = lambda ref: ref[pl.ds(r0, nb * c), :].reshape(nb, c, dk)
            qg, kg, vg = ld(q_ref) * scale, ld(k_ref), ld(v_ref)
            gcv, grv, bt = gc_ref[0, pl.ds(n0, nb)], gr_ref[0, pl.ds(n0, nb)], b_ref[0, pl.ds(n0, nb)]
            return r0, n0, qg, kg, vg, gcv, grv, bt, _dn_local(qg, kg, vg, gcv, grv, bt)

        def group_a(gi, carry):
            _, n0, qg, kg, _, gcv, grv, _, loc = load_group(gi)
            row = lax.broadcasted_iota(jnp.int32, (c, c), 0)
            col = lax.broadcasted_iota(jnp.int32, (c, c), 1)
            upper = (col >= row)[None]
            decay_t = jnp.where(upper, jnp.exp(jnp.where(upper, grv - gcv, 0.0)), 0.0)
            u_s[pl.ds(n0, nb)] = loc["u"]
            w_s[pl.ds(n0, nb)] = loc["w"]
            att_s[pl.ds(n0, nb)] = _bmm_nt(kg.astype(BF16), qg.astype(BF16)) * decay_t
            qd_s[pl.ds(n0, nb)] = loc["qd"]
            kt_s[pl.ds(n0, nb)] = loc["kt"]
            gl_s[pl.ds(n0, nb)] = loc["gl"]
            return carry
        lax.fori_loop(0, n // nb, group_a, 0)

        def chunk_b(it, ds_next):
            i = n - 1 - it
            s = s_ref[0, i]
            dov = do_ref[pl.ds(pl.multiple_of(i * c, c), c), :]
            w, kt = w_s[i], kt_s[i]
            vnew = u_s[i] - _dot(w, s)
            dvnew = _dot(att_s[i], dov) + _dot(kt, ds_next)
            du_s[i] = dvnew
            dw_s[i] = -_dot_nt(dvnew, s)
            dat_s[i] = _dot_nt(dov, vnew)
            dqd_s[i] = _dot_nt(dov, s)
            dkt_s[i] = _dot_nt(vnew, ds_next)
            dgl_s[i] = jnp.sum(jnp.sum(ds_next * s, axis=1, keepdims=True), axis=0, keepdims=True)
            return ds_next * gl_s[i] + _dot_tn(qd_s[i], dov) - _dot_tn(w, dvnew)
        lax.fori_loop(0, n, chunk_b, jnp.zeros((dk, dk), F32))

        def group_c(gi, carry):
            r0, n0, qg, kg, vg, gcv, grv, bt, loc = load_group(gi)
            sl = pl.ds(n0, nb)
            du, dw, dat, dqd, dkt, dgl = du_s[sl], dw_s[sl], dat_s[sl], dqd_s[sl], dkt_s[sl], dgl_s[sl]
            tm, decay, kb, kbg = loc["tm"], loc["decay"], loc["kb"], loc["kbg"]
            dvb = _bmm_tn(tm, du, HI)
            dkbg = _bmm_tn(tm, dw, HI)
            dt = _bmm_nt(du, loc["vb"], HI) + _bmm_nt(dw, kbg, HI)
            da = jnp.where(loc["strict"], -_bmm_tn(tm, _bmm_nt(dt, tm, HI), HI), 0.0)
            dms = (da * decay).astype(BF16)
            dqs = (dat * decay).astype(BF16)
            kgb = kg.astype(BF16)
            dkb = _bmm(dms, kgb) + dkbg * loc["eg"]
            dqt = _bmm(dqs, kgb) + dqd * loc["eg"]
            dkk = _bmm_tn(dms, kb.astype(BF16)) + _bmm_tn(dqs, qg.astype(BF16)) + dkt * loc["erel"] + dkb * bt
            e = (da * loc["m"] + dat * loc["qm"]) * decay
            lsum = lambda x: jnp.sum(x, axis=2, keepdims=True)
            dkt_kt = lsum(dkt * loc["kt"])
            dgcv = lsum(e) + lsum(dqd * loc["qd"]) - dkt_kt + lsum(dkbg * kbg)
            dglast = jnp.sum(dkt_kt, axis=1, keepdims=True) + dgl * loc["gl"]
            rowc = lax.broadcasted_iota(jnp.int32, (1, c, 1), 1)
            dgc_ref[0, sl] = dgcv + jnp.where(rowc == c - 1, dglast, 0.0)
            dgr_ref[0, sl] = -jnp.sum(e, axis=1, keepdims=True)
            db_ref[0, sl] = lsum(dkb * kg) + lsum(dvb * vg)
            rows = pl.ds(r0, nb * c)
            dq_ref[rows, :] = (dqt * scale).reshape(nb * c, dk)
            dk_ref[rows, :] = dkk.reshape(nb * c, dk)
            dv_ref[rows, :] = (dvb * bt).reshape(nb * c, dk)
            return carry
        lax.fori_loop(0, n // nb, group_c, 0)

    big = lambda: pltpu.VMEM((n, c, dk), F32)
    sq = lambda: pltpu.VMEM((n, c, c), F32)
    one = lambda: pltpu.VMEM((n, 1, 1), F32)
    scratch = [big(), big(), sq(), big(), big(), one(), big(), big(), sq(), big(), big(), one()]
    return pl.pallas_call(
        body, grid=(heads,), in_specs=[hs, hs, hs, gcs, grs, gcs, ss, hs], out_specs=[hs, hs, hs, gcs, grs, gcs],
        out_shape=[jax.ShapeDtypeStruct((t, d), F32)] * 3 + [
            jax.ShapeDtypeStruct((heads, n, c, 1), F32), jax.ShapeDtypeStruct((heads, n, 1, c), F32),
            jax.ShapeDtypeStruct((heads, n, c, 1), F32)],
        scratch_shapes=scratch, name=name, compiler_params=_cparams(("parallel",)))(q, k, v, gc, gr, bc, s_all, do)


def _dot01(x, m01):
    x1 = x.astype(BF16)
    r1 = x - x1.astype(F32)
    x2 = r1.astype(BF16)
    x3 = (r1 - x2.astype(F32)).astype(BF16)
    dot = lambda a: jnp.dot(a, m01, preferred_element_type=F32)
    return (dot(x1) + dot(x2)) + dot(x3)


def _sb_tile(z, mask, r):
    rows, bk = z.shape
    row = lax.broadcasted_iota(jnp.int32, (bk, bk), 0)
    col = lax.broadcasted_iota(jnp.int32, (bk, bk), 1)
    later = (row > col).astype(BF16)
    parts = [slice(p, p + SB_PART) for p in range(0, rows, SB_PART)]
    lm, ls, cs = [], [], []
    for p in parts:
        lsm = -_softplus(z[p])
        lm.append(lsm if mask is None else jnp.where(mask[p], lsm, 0.0))
        ls.append(z[p] + lsm)
        cs.append(_dot01(lm[-1], later))
    a = []
    for i, p in enumerate(parts):
        ap = jnp.exp(ls[i] + cs[i] + r[p])
        a.append(ap if mask is None else jnp.where(mask[p], ap, 0.0))
    cat = lambda xs: jnp.concatenate(xs, axis=0)
    return cat(a), jnp.exp(cat(ls)), r + jnp.sum(cat(lm), axis=1, keepdims=True)


def _sb_sweep(bq, bk, qi, visit, carry):
    nd = bq // bk
    row = lax.broadcasted_iota(jnp.int32, (2 * bq, bk), 0)
    row = jnp.where(row >= bq, row - bq, row)
    col = lax.broadcasted_iota(jnp.int32, (2 * bq, bk), 1)
    for dd in reversed(range(nd)):
        carry = visit(qi * nd + dd, (col + dd * bk) < row, carry)
    return lax.fori_loop(0, qi * nd, lambda it, c: visit(qi * nd - 1 - it, None, c), carry)


def _stack_heads(a, h0):
    return jnp.concatenate([jnp.where(h0, a, 0.0), jnp.where(h0, 0.0, a)], axis=0).astype(BF16)


def _side_by_side(a, bq):
    return jnp.concatenate([a[:bq], a[bq:]], axis=1)


def _sb_specs(t, d, bq, vcol):
    qs = pl.BlockSpec((bq, LANES), lambda g, i: (i, g))
    ks = pl.BlockSpec((t, LANES), lambda g, i: (0, g))
    vs = pl.BlockSpec((t, LANES), lambda g, i: (0, g + vcol))
    return qs, ks, vs


def _sb_fwd(qn, kn, qkv, name):
    t, d = qn.shape
    bq, bk = min(SB_QBLOCK, t), min(SB_BLOCK, t)
    scale = SB_HEAD_DIM ** -0.5
    nt = (((1,), (1,)), ((), ()))

    def body(q_ref, k_ref, v_ref, o_ref):
        qi = pl.program_id(1)
        h0 = lax.broadcasted_iota(jnp.int32, (1, LANES), 1) < SB_HEAD_DIM
        q2 = _stack_heads(q_ref[...] * scale, h0)

        def visit(j, mask, carry):
            acc, r = carry
            rows = pl.ds(pl.multiple_of(j * bk, bk), bk)
            z = lax.dot_general(q2, k_ref[rows, :].astype(BF16), nt, preferred_element_type=F32)
            a, _, r = _sb_tile(z, mask, r)
            a2 = _side_by_side(a.astype(BF16), bq)
            return acc + jnp.dot(a2, _stack_heads(v_ref[rows, :], h0), preferred_element_type=F32), r
        acc, _ = _sb_sweep(bq, bk, qi, visit, (jnp.zeros((bq, LANES), F32), jnp.zeros((2 * bq, 1), F32)))
        o_ref[...] = acc.astype(o_ref.dtype)

    qs, ks, vs = _sb_specs(t, d, bq, 2 * d // LANES)
    return pl.pallas_call(
        body, grid=(d // LANES, t // bq), in_specs=[qs, ks, vs], out_specs=qs,
        out_shape=jax.ShapeDtypeStruct((t, d), BF16), name=name,
        compiler_params=_cparams(("parallel", "parallel")))(qn, kn, qkv)


def _sb_bwd(qn, kn, qkv, do, name):
    t, d = qn.shape
    bq, bk = min(SB_QBLOCK, t), min(SB_BLOCK, t)
    scale = SB_HEAD_DIM ** -0.5
    nt = (((1,), (1,)), ((), ()))

    def body(q_ref, k_ref, v_ref, do_ref, dq_ref, dk_ref, dv_ref, p_s, sg_s):
        qi = pl.program_id(1)

        @pl.when(qi == 0)
        def _():
            dk_ref[...] = jnp.zeros(dk_ref.shape, F32)
            dv_ref[...] = jnp.zeros(dv_ref.shape, F32)

        h0 = lax.broadcasted_iota(jnp.int32, (1, LANES), 1) < SB_HEAD_DIM
        q2 = _stack_heads(q_ref[...] * scale, h0)
        do2 = _stack_heads(do_ref[...], h0)
        row = lax.broadcasted_iota(jnp.int32, (bk, bk), 0)
        col = lax.broadcasted_iota(jnp.int32, (bk, bk), 1)
        later_incl = (row >= col).astype(BF16)
        zero = jnp.zeros((2 * bq, 1), F32)
        tn = (((0,), (0,)), ((), ()))

        def visit1(j, mask, carry):
            r, sp = carry
            rows = pl.ds(pl.multiple_of(j * bk, bk), bk)
            z = lax.dot_general(q2, k_ref[rows, :].astype(BF16), nt, preferred_element_type=F32)
            da = lax.dot_general(do2, v_ref[rows, :].astype(BF16), nt, preferred_element_type=F32)
            a, sg, r = _sb_tile(z, mask, r)
            p = a * da
            p_s[j] = p
            sg_s[j] = sg
            dv_ref[rows, :] += lax.dot_general(a.astype(BF16), do2, tn, preferred_element_type=F32)
            return r, sp + jnp.sum(p, axis=1, keepdims=True)
        _, total = _sb_sweep(bq, bk, qi, visit1, (zero, zero))

        def visit2(j, mask, carry):
            dq, sp = carry
            rows = pl.ds(pl.multiple_of(j * bk, bk), bk)
            p, sg = p_s[j], sg_s[j]
            pref = total - sp - _dot01(p, later_incl)
            dz = p * (1.0 - sg) - pref * sg
            if mask is not None:
                dz = jnp.where(mask, dz, 0.0)
            dz = dz.astype(BF16)
            dk_ref[rows, :] += lax.dot_general(dz, q2, tn, preferred_element_type=F32)
            dq = dq + jnp.dot(_side_by_side(dz, bq), _stack_heads(k_ref[rows, :], h0), preferred_element_type=F32)
            return dq, sp + jnp.sum(p, axis=1, keepdims=True)
        dq, _ = _sb_sweep(bq, bk, qi, visit2, (jnp.zeros((bq, LANES), F32), zero))
        dq_ref[...] = dq * scale

    qs, ks, vs = _sb_specs(t, d, bq, 2 * d // LANES)
    shp = jax.ShapeDtypeStruct((t, d), F32)
    scratch = [pltpu.VMEM((t // bk, 2 * bq, bk), F32), pltpu.VMEM((t // bk, 2 * bq, bk), F32)]
    return pl.pallas_call(
        body, grid=(d // LANES, t // bq), in_specs=[qs, ks, vs, qs], out_specs=[qs, ks, ks],
        out_shape=[shp, shp, shp], scratch_shapes=scratch, name=name,
        compiler_params=_cparams(("parallel", "arbitrary")))(qn, kn, qkv, do)


def _adamw(w, g, m, v, name):
    shape = w.shape
    two_d = lambda a: a.reshape(-1, shape[-1])
    c1 = 1.0 - ADAM_B1 ** ADAM_STEP
    c2 = 1.0 - ADAM_B2 ** ADAM_STEP

    def body(r, _):
        wv, gv, mv, vv = r
        mn = ADAM_B1 * mv + (1.0 - ADAM_B1) * gv
        vn = ADAM_B2 * vv + (1.0 - ADAM_B2) * (gv * gv)
        delta = -ADAM_LR * ((mn / c1) / (jnp.sqrt(vn / c2) + ADAM_EPS) + ADAM_WD * wv)
        return [delta, mn, vn], []
    width = shape[-1]
    res = _rowwise(body, [two_d(w), two_d(g), two_d(m), two_d(v)], [], [(width, F32)] * 3, [], name=name)
    return [r.reshape(shape) for r in res]


def _dn_layer_fwd(h1, w, conv_w, a_log, dt_bias, onorm_g, tag):
    t, d = h1.shape
    heads = d // DN_HEAD_DIM
    n = t // DN_CHUNK
    proj = _mm(h1, w["all"], name=tag + "_proj")
    qn, kn, vv = _dn_prep_fwd(proj, conv_w, heads, tag + "_prep")
    big_g, beta = _dn_gate_fwd(proj, 4 * d // (2 * LANES), a_log, dt_bias, tag + "_gate")
    gt_ = big_g[:, :heads].T.reshape(heads, n, DN_CHUNK)
    gc, gr = gt_[..., None], gt_[:, :, None, :]
    bc = beta[:, :heads].T.reshape(heads, n, DN_CHUNK)[..., None]
    o, s_all = _dn_chunk_fwd(qn, kn, vv, gc, gr, bc, tag + "_chunk")
    on = _dn_post_fwd(o, proj, onorm_g, tag + "_post")
    y = _mm(on, w["out"], name=tag + "_out")
    return y, dict(proj=proj, qn=qn, kn=kn, v=vv, gc=gc, gr=gr, bc=bc, o=o, s_all=s_all, on=on)


def _dn_layer_bwd(dy, h1, w, conv_w, a_log, dt_bias, onorm_g, sv, tag):
    t, d = h1.shape
    heads = d // DN_HEAD_DIM
    don = _mm(dy, w["out"], tb=True, name=tag + "_dout")
    g_out = _mm(sv["on"], dy, ta=True, out_dtype=BF16, name=tag + "_gwout")
    do, dz, g_on = _dn_post_bwd(sv["o"], sv["proj"], don, onorm_g, tag + "_dpost")
    dq, dk, dv, dgc, dgr, dbc = _dn_chunk_bwd(sv["qn"], sv["kn"], sv["v"], sv["gc"], sv["gr"], sv["bc"], sv["s_all"],
                                              do, tag + "_dchunk")
    pad = lambda a: jnp.pad(a.reshape(heads, t).T, ((0, 0), (0, LANES - heads)))
    d_big_g = pad(dgc) + pad(dgr)
    dab, g_alog, g_dt = _dn_gate_bwd(sv["proj"], 4 * d // (2 * LANES), d_big_g, pad(dbc), a_log, dt_bias, tag + "_dgate")
    dxq, dxk, dxv, wq, wk, wv = _dn_prep_bwd(sv["proj"], conv_w, dq, dk, dv, heads, tag + "_dprep")
    dproj = jnp.concatenate([dxq, dxk, dxv, dz, dab], axis=1)
    dh1 = _mm(dproj, w["all"], tb=True, name=tag + "_dh")
    g_all = _mm(h1, dproj, ta=True, out_dtype=BF16, name=tag + "_gwin")
    grads = dict(w_all=g_all, w_out=g_out, conv_w=jnp.concatenate([wq, wk, wv], axis=1), a_log=g_alog, dt_bias=g_dt,
                 onorm_g=g_on)
    return dh1, grads


def _sb_layer_fwd(h1, w, q_g, k_g, tag):
    t, d = h1.shape
    heads = d // SB_HEAD_DIM
    qkv = _mm(h1, w["qkv"], name=tag + "_proj")
    gq, gk = jnp.tile(q_g, (1, heads)), jnp.tile(k_g, (1, heads))
    qn, kn = _sb_norm_fwd(qkv, gq, gk, tag + "_norm")
    o = _sb_fwd(qn, kn, qkv, tag + "_attn")
    y = _mm(o, w["out"], name=tag + "_out")
    return y, dict(qkv=qkv, qn=qn, kn=kn, o=o, gq=gq, gk=gk)


def _sb_layer_bwd(dy, h1, w, q_g, k_g, sv, tag):
    t, d = h1.shape
    heads = d // SB_HEAD_DIM
    do = _mm(dy, w["out"], tb=True, name=tag + "_dout")
    g_out = _mm(sv["o"], dy, ta=True, out_dtype=BF16, name=tag + "_gwout")
    dqn, dkn, dv = _sb_bwd(sv["qn"], sv["kn"], sv["qkv"], do, tag + "_dattn")
    dqkv, g_q, g_k = _sb_norm_bwd(sv["qkv"], dqn, dkn, dv, sv["gq"], sv["gk"], tag + "_dnorm")
    fold = lambda g: jnp.sum(g.reshape(heads, SB_HEAD_DIM), axis=0, keepdims=True)
    dh1 = _mm(dqkv, w["qkv"], tb=True, name=tag + "_dh")
    g_qkv = _mm(h1, dqkv, ta=True, out_dtype=BF16, name=tag + "_gwin")
    return dh1, dict(w_qkv=g_qkv, w_out=g_out, q_norm_g=fold(g_q), k_norm_g=fold(g_k))


def _local_step(x, tgt, mod, norm1_g, norm2_g, dn, sb, ffn):
    depth = mod.shape[0]
    d = x.shape[1]
    saved = []
    for i in range(depth):
        mv = [mod[i:i + 1, j * d:(j + 1) * d] for j in range(N_MOD)]
        sh1, sc1, gt1, sh2, sc2, gt2 = mv
        tag = "l%d" % i
        h1 = _adaln_fwd(x, norm1_g[i:i + 1], sc1, sh1, tag + "_ln1")
        if i % 2 == 0:
            p = dn[i // 2]
            y, sv = _dn_layer_fwd(h1, p, p["conv_w"], p["a_log"], p["dt_bias"], p["onorm_g"], tag + "_dn")
        else:
            p = sb[i // 2]
            y, sv = _sb_layer_fwd(h1, p, p["q_g"], p["k_g"], tag + "_sb")
        x1 = _resid_fwd(x, y, gt1, tag + "_res1")
        h2 = _adaln_fwd(x1, norm2_g[i:i + 1], sc2, sh2, tag + "_ln2")
        u = _mm(h2, ffn[i]["w_in"], name=tag + "_ffn_in")
        a = _swiglu_fwd(u, tag + "_swiglu")
        y2 = _mm(a, ffn[i]["w_out"], name=tag + "_ffn_out")
        x2 = _resid_fwd(x1, y2, gt2, tag + "_res2")
        saved.append(dict(x0=x, h1=h1, y=y, mix=sv, x1=x1, h2=h2, u=u, a=a, y2=y2))
        x = x2

    loss, dx = _loss_fwd_bwd(x, tgt, "loss")

    dmod, dn1, dn2 = [None] * depth, [None] * depth, [None] * depth
    g_dn, g_sb, g_ffn = [None] * len(dn), [None] * len(sb), [None] * depth
    for i in reversed(range(depth)):
        s = saved[i]
        mv = [mod[i:i + 1, j * d:(j + 1) * d] for j in range(N_MOD)]
        sh1, sc1, gt1, sh2, sc2, gt2 = mv
        tag = "l%d" % i
        dy2, dgt2 = _resid_bwd(dx, s["y2"], gt2, tag + "_dres2")
        da = _mm(dy2, ffn[i]["w_out"], tb=True, name=tag + "_dffn_a")
        g_wout = _mm(s["a"], dy2, ta=True, out_dtype=BF16, name=tag + "_gffn_out")
        du = _swiglu_bwd(s["u"], da, tag + "_dswiglu")
        dh2 = _mm(du, ffn[i]["w_in"], tb=True, name=tag + "_dffn_h")
        g_win = _mm(s["h2"], du, ta=True, out_dtype=BF16, name=tag + "_gffn_in")
        g_ffn[i] = dict(w_in=g_win, w_out=g_wout)
        dx, dg2, dsc2, dsh2 = _adaln_bwd(s["x1"], dh2, dx, norm2_g[i:i + 1], sc2, tag + "_dln2")
        dy, dgt1 = _resid_bwd(dx, s["y"], gt1, tag + "_dres1")
        if i % 2 == 0:
            p = dn[i // 2]
            dh1, g_dn[i // 2] = _dn_layer_bwd(dy, s["h1"], p, p["conv_w"], p["a_log"], p["dt_bias"], p["onorm_g"],
                                               s["mix"], tag + "_dn")
        else:
            p = sb[i // 2]
            dh1, g_sb[i // 2] = _sb_layer_bwd(dy, s["h1"], p, p["q_g"], p["k_g"], s["mix"], tag + "_sb")
        dx, dg1, dsc1, dsh1 = _adaln_bwd(s["x0"], dh1, dx, norm1_g[i:i + 1], sc1, tag + "_dln1")
        dmod[i] = jnp.concatenate([dsh1, dsc1, dgt1, dsh2, dsc2, dgt2], axis=1)
        dn1[i], dn2[i] = dg1, dg2
    return (loss, dx, jnp.concatenate(dmod, axis=0), jnp.concatenate(dn1, axis=0), jnp.concatenate(dn2, axis=0),
            g_dn, g_sb, g_ffn)


def _axes():
    return lax.axis_index("x"), lax.axis_index("y"), lax.axis_index("c")


def _remote(src, dst, send_sem, recv_sem, dev):
    return pltpu.make_async_remote_copy(src_ref=src, dst_ref=dst, send_sem=send_sem, recv_sem=recv_sem,
                                        device_id=dev, device_id_type=MESH)


def _other_chips(x, y):
    return [(1 - x, y), (x, 1 - y), (1 - x, 1 - y)]


def _allgather_small(v, name):
    m, n = v.shape

    def body(x_ref, out_ref, send_sems, recv_sems, local_sem):
        x, y, c = _axes()
        me, sibling = (x, y, c), (x, y, 1 - c)
        chips = _other_chips(x, y)

        def rows(px, py, pc):
            return out_ref.at[pl.ds((4 * px + 2 * py + pc) * m, m), :]

        def copy(k, block, to, src=None):
            return _remote(rows(*block) if src is None else src, rows(*block), send_sems.at[k], recv_sems.at[k], to)

        mine = pltpu.make_async_copy(x_ref, rows(*me), local_sem)
        mine.start()
        first = [copy(0, me, sibling, src=x_ref)]
        first += [copy(1 + j, me, (*chip, c), src=x_ref) for j, chip in enumerate(chips)]
        for cp in first:
            cp.start()
        passed = [copy(4 + j, (*chip, c), sibling) for j, chip in enumerate(chips)]
        for j, chip in enumerate(chips):
            copy(1 + j, (*chip, c), me).wait_recv()
            passed[j].start()
        copy(0, sibling, me).wait_recv()
        for j, chip in enumerate(chips):
            copy(4 + j, (*chip, 1 - c), me).wait_recv()
        for cp in first + passed:
            cp.wait_send()
        mine.wait()

    return pl.pallas_call(
        body, out_shape=jax.ShapeDtypeStruct((8 * m, n), v.dtype),
        in_specs=[pl.BlockSpec(memory_space=pltpu.VMEM)], out_specs=pl.BlockSpec(memory_space=pltpu.VMEM),
        scratch_shapes=[pltpu.SemaphoreType.DMA((7,)), pltpu.SemaphoreType.DMA((7,)), pltpu.SemaphoreType.DMA],
        name=name, compiler_params=pltpu.CompilerParams(vmem_limit_bytes=VMEM_LIMIT))(v)


def _half(ref, h, rh):
    return ref.at[(slice(None),) * (len(ref.shape) - 2) + (pl.ds(h * rh, rh), slice(None))]


def _hbm_call(body, ins, out_shapes, n_sems, n_local, name):
    hbm = pl.BlockSpec(memory_space=pltpu.HBM)
    scratch = [pltpu.SemaphoreType.DMA((n_sems,)), pltpu.SemaphoreType.DMA((n_sems,))]
    if n_local:
        scratch.append(pltpu.SemaphoreType.DMA((n_local,)))
    return pl.pallas_call(body, out_shape=out_shapes, in_specs=[hbm] * len(ins), out_specs=[hbm] * len(out_shapes),
                          scratch_shapes=scratch, name=name)(*ins)


def _allgather_weights(shards, name):
    n = len(shards)

    def body(*refs):
        p_refs, out_refs, (send_sems, recv_sems, local_sems) = refs[:n], refs[n:2 * n], refs[2 * n:]
        x, y, c = _axes()
        sibling = (x, y, 1 - c)
        chips = _other_chips(x, y)
        rh = [s.shape[1] // 2 for s in shards]
        blk = lambda i, cx, cy, h: _half(out_refs[i].at[2 * cx + cy], h, rh[i])
        mine = [pltpu.make_async_copy(p_refs[i], out_refs[i].at[2 * x + y], local_sems.at[i]) for i in range(n)]
        for cp in mine:
            cp.start()
        first = [_remote(_half(p_refs[i], c, rh[i]), blk(i, x, y, c), send_sems.at[6 * i + j], recv_sems.at[6 * i + j],
                         (*chip, c)) for i in range(n) for j, chip in enumerate(chips)]
        for cp in first:
            cp.start()
        passed = []
        for j, chip in enumerate(chips):
            for i in range(n):
                _remote(_half(p_refs[i], c, rh[i]), blk(i, *chip, c), send_sems.at[6 * i + j], recv_sems.at[6 * i + j],
                        sibling).wait_recv()
                passed.append(_remote(blk(i, *chip, c), blk(i, *chip, c), send_sems.at[6 * i + 3 + j],
                                      recv_sems.at[6 * i + 3 + j], sibling))
                passed[-1].start()
        for j, chip in enumerate(chips):
            for i in range(n):
                _remote(_half(p_refs[i], c, rh[i]), blk(i, *chip, 1 - c), send_sems.at[6 * i + 3 + j],
                        recv_sems.at[6 * i + 3 + j], sibling).wait_recv()
        for cp in first + passed:
            cp.wait_send()
        for cp in mine:
            cp.wait()

    outs = [jax.ShapeDtypeStruct((4,) + s.shape, s.dtype) for s in shards]
    return _hbm_call(body, shards, outs, 6 * n, n, name)


def _sibling_swap_halves(arrs, name):
    n = len(arrs)

    def body(*refs):
        v_refs, out_refs, (send_sems, recv_sems) = refs[:n], refs[n:2 * n], refs[2 * n:]
        x, y, c = _axes()
        cps = [_remote(_half(v_refs[i], 1 - c, arrs[i].shape[-2] // 2), out_refs[i], send_sems.at[i], recv_sems.at[i],
                       (x, y, 1 - c)) for i in range(n)]
        for cp in cps:
            cp.start()
        for cp in cps:
            cp.wait()

    outs = [jax.ShapeDtypeStruct(a.shape[:-2] + (a.shape[-2] // 2, a.shape[-1]), a.dtype) for a in arrs]
    return _hbm_call(body, arrs, outs, n, 0, name)


def _chip_scatter(arrs, name):
    n = len(arrs)

    def body(*refs):
        v_refs, out_refs, (send_sems, recv_sems, local_sems) = refs[:n], refs[n:2 * n], refs[2 * n:]
        x, y, c = _axes()
        me = 2 * x + y
        chips = _other_chips(x, y)
        mine = [pltpu.make_async_copy(v_refs[i].at[me], out_refs[i].at[me], local_sems.at[i]) for i in range(n)]
        for cp in mine:
            cp.start()
        sends = [_remote(v_refs[i].at[2 * cx + cy], out_refs[i].at[me], send_sems.at[3 * i + j], recv_sems.at[3 * i + j],
                         (cx, cy, c)) for i in range(n) for j, (cx, cy) in enumerate(chips)]
        for cp in sends:
            cp.start()
        for i in range(n):
            for j, (cx, cy) in enumerate(chips):
                _remote(v_refs[i].at[me], out_refs[i].at[2 * cx + cy], send_sems.at[3 * i + j], recv_sems.at[3 * i + j],
                        (cx, cy, c)).wait_recv()
        for cp in sends:
            cp.wait_send()
        for cp in mine:
            cp.wait()

    outs = [jax.ShapeDtypeStruct(a.shape, a.dtype) for a in arrs]
    return _hbm_call(body, arrs, outs, 3 * n, n, name)


def _share_halves(arrs, name):
    n = len(arrs)

    def body(*refs):
        v_refs, out_refs, (send_sems, recv_sems, local_sems) = refs[:n], refs[n:2 * n], refs[2 * n:]
        x, y, c = _axes()
        mine = [pltpu.make_async_copy(v_refs[i], out_refs[i].at[c], local_sems.at[i]) for i in range(n)]
        for cp in mine:
            cp.start()
        cps = [_remote(v_refs[i], out_refs[i].at[c], send_sems.at[i], recv_sems.at[i], (x, y, 1 - c)) for i in range(n)]
        for cp in cps:
            cp.start()
        for i in range(n):
            _remote(v_refs[i], out_refs[i].at[1 - c], send_sems.at[i], recv_sems.at[i], (x, y, 1 - c)).wait_recv()
        for cp in cps:
            cp.wait_send()
        for cp in mine:
            cp.wait()

    outs = [jax.ShapeDtypeStruct((2,) + a.shape, a.dtype) for a in arrs]
    return _hbm_call(body, arrs, outs, n, n, name)


def _sum_lead(v, name):
    k, r, w = v.shape
    tr = _pick_tile(r, (256, 128, 64, 32, 16))

    def body(v_ref, o_ref):
        acc = v_ref[0].astype(F32)
        for i in range(1, k):
            acc = acc + v_ref[i].astype(F32)
        o_ref[...] = acc

    return pl.pallas_call(
        body, grid=(r // tr,), in_specs=[pl.BlockSpec((k, tr, w), lambda i: (0, i, 0))],
        out_specs=pl.BlockSpec((tr, w), lambda i: (i, 0)), out_shape=jax.ShapeDtypeStruct((r, w), F32), name=name,
        compiler_params=_cparams(("parallel",)))(v)


_BIG = (("dn_w_in", 2), ("dn_w_out", 1), ("sb_w_qkv", 2), ("sb_w_out", 1), ("ffn_w_in", 2), ("ffn_w_out", 1))


def _to_chip_major(a, axis, nchips):
    l, r, c = a.shape
    if axis == 2:
        return a.reshape(l, r, nchips, c // nchips).transpose(2, 0, 1, 3)
    return a.reshape(l, nchips, r // nchips, c).transpose(1, 0, 2, 3)


def _from_chip_major(a, axis):
    n, l, r, c = a.shape
    if axis == 2:
        return a.transpose(1, 2, 0, 3).reshape(l, r, n * c)
    return a.transpose(1, 0, 2, 3).reshape(l, n * r, c)


def kernel(x, c, ada_w, ada_b, norm1_g, norm2_g, dn_w_in, dn_conv_w, dn_a_log, dn_dt_bias, dn_onorm_g, dn_w_out, sb_w_qkv, sb_q_norm_g, sb_k_norm_g, sb_w_out, ffn_w_in, ffn_w_out, loss_target, m_ada_w, m_ada_b, m_norm1_g, m_norm2_g, m_dn_w_in, m_dn_conv_w, m_dn_a_log, m_dn_dt_bias, m_dn_onorm_g, m_dn_w_out, m_sb_w_qkv, m_sb_q_norm_g, m_sb_k_norm_g, m_sb_w_out, m_ffn_w_in, m_ffn_w_out, v_ada_w, v_ada_b, v_norm1_g, v_norm2_g, v_dn_w_in, v_dn_conv_w, v_dn_a_log, v_dn_dt_bias, v_dn_onorm_g, v_dn_w_out, v_sb_w_qkv, v_sb_q_norm_g, v_sb_k_norm_g, v_sb_w_out, v_ffn_w_in, v_ffn_w_out):
    names = ("ada_w", "ada_b", "norm1_g", "norm2_g", "dn_w_in", "dn_conv_w", "dn_a_log", "dn_dt_bias", "dn_onorm_g",
             "dn_w_out", "sb_w_qkv", "sb_q_norm_g", "sb_k_norm_g", "sb_w_out", "ffn_w_in", "ffn_w_out")
    w = dict(zip(names, (ada_w, ada_b, norm1_g, norm2_g, dn_w_in, dn_conv_w, dn_a_log, dn_dt_bias, dn_onorm_g,
                         dn_w_out, sb_w_qkv, sb_q_norm_g, sb_k_norm_g, sb_w_out, ffn_w_in, ffn_w_out)))
    mom = dict(zip(names, (m_ada_w, m_ada_b, m_norm1_g, m_norm2_g, m_dn_w_in, m_dn_conv_w, m_dn_a_log, m_dn_dt_bias,
                           m_dn_onorm_g, m_dn_w_out, m_sb_w_qkv, m_sb_q_norm_g, m_sb_k_norm_g, m_sb_w_out, m_ffn_w_in,
                           m_ffn_w_out)))
    var = dict(zip(names, (v_ada_w, v_ada_b, v_norm1_g, v_norm2_g, v_dn_w_in, v_dn_conv_w, v_dn_a_log, v_dn_dt_bias,
                           v_dn_onorm_g, v_dn_w_out, v_sb_w_qkv, v_sb_q_norm_g, v_sb_k_norm_g, v_sb_w_out, v_ffn_w_in,
                           v_ffn_w_out)))
    ax, ay, ac = _axes()
    chip = 2 * ax + ay
    dev = 2 * chip + ac
    t, d = x.shape[1], x.shape[2]
    depth, ndn, nsb = ada_w.shape[0], dn_w_in.shape[0], sb_w_qkv.shape[0]
    heads = d // DN_HEAD_DIM
    mod_cols = ada_w.shape[2]
    conv_cols = dn_conv_w.shape[2]
    nchips, ndev = 4, 8

    conv_rows = ndn * DN_CONV * conv_cols // d
    pay1 = jnp.concatenate([c, dn_conv_w.reshape(conv_rows, d), jnp.zeros((8 - 1 - conv_rows, d), F32)], axis=0)
    g1 = _allgather_small(pay1, "ag_cond").reshape(ndev, 8, d)
    c_all = g1[:, 0]
    conv_full = g1[::2, 1:1 + conv_rows].reshape(nchips, ndn, DN_CONV, conv_cols).transpose(1, 2, 0, 3)
    conv_full = conv_full.reshape(ndn, DN_CONV, nchips * conv_cols)

    c16 = jnp.pad(c_all, ((0, 16 - ndev), (0, 0)))
    cond16 = _rowwise(lambda r, v: ([_silu(r[0])], []), [c16], [], [(d, F32)], [], name="cond_silu")[0]
    pay2 = jnp.concatenate([_mm(cond16, ada_w[i], name="ada_mod%d" % i)[:ndev] for i in range(depth)], axis=0)
    g2 = _allgather_small(pay2, "ag_mod").reshape(ndev, depth, ndev, mod_cols)[::2]
    mod_raw = lax.dynamic_index_in_dim(g2, dev, axis=2, keepdims=False)
    mod_raw = mod_raw.transpose(1, 0, 2).reshape(depth, nchips * mod_cols)
    mod = _rowwise(lambda r, v: ([r[0] + r[1]], []), [mod_raw, ada_b], [], [(nchips * mod_cols, F32)], [],
                   name="ada_bias")[0]

    gathered = _allgather_weights([w[n].astype(BF16) for n, _ in _BIG], "ag_weights")
    full = {n: _from_chip_major(a, axis) for (n, axis), a in zip(_BIG, gathered)}

    padl = lambda v: jnp.pad(v[None, :], ((0, 0), (0, LANES - v.shape[0])))
    padc = lambda a: jnp.pad(a, ((0, 0), (0, LANES - a.shape[1])))
    dn = []
    for j in range(ndn):
        wi = full["dn_w_in"][j]
        w_all = jnp.concatenate([wi[:, :4 * d], padc(wi[:, 4 * d:4 * d + heads]), padc(wi[:, 4 * d + heads:])], axis=1)
        dn.append(dict(all=w_all, out=full["dn_w_out"][j], conv_w=conv_full[j], a_log=padl(dn_a_log[j]),
                       dt_bias=padl(dn_dt_bias[j]), onorm_g=dn_onorm_g[j][None]))
    sb = [dict(qkv=full["sb_w_qkv"][j], out=full["sb_w_out"][j], q_g=sb_q_norm_g[j][None], k_g=sb_k_norm_g[j][None])
          for j in range(nsb)]
    ffn = [dict(w_in=full["ffn_w_in"][i], w_out=full["ffn_w_out"][i]) for i in range(depth)]

    loss_local, grad_x, dmod, g_n1, g_n2, g_dn, g_sb, g_ffn = _local_step(x[0], loss_target[0], mod, norm1_g, norm2_g,
                                                                           dn, sb, ffn)
    loss = lax.psum(loss_local, ("x", "y", "c"))

    g_conv = jnp.stack([g["conv_w"] for g in g_dn])
    misc = jnp.concatenate([jnp.concatenate([g["onorm_g"] for g in g_dn], axis=1),
                            jnp.concatenate([g["a_log"] for g in g_dn], axis=1),
                            jnp.concatenate([g["dt_bias"] for g in g_dn], axis=1),
                            jnp.concatenate([g["q_norm_g"] for g in g_sb], axis=1),
                            jnp.concatenate([g["k_norm_g"] for g in g_sb], axis=1)], axis=1)
    misc = jnp.pad(misc, ((0, 0), (0, -misc.shape[1] % d))).reshape(-1, d)
    small = [dmod.reshape(-1, d), g_n1, g_n2, g_conv.reshape(-1, d), misc]
    small_rows = [s.shape[0] for s in small]
    pad_rows = -sum(small_rows) % 8
    pay3 = jnp.concatenate(small + [jnp.zeros((pad_rows, d), F32)], axis=0)
    nrow3 = pay3.shape[0]
    g3 = _allgather_small(pay3, "ag_small")
    summed = _rowwise(lambda r, v: ([], [_colsum(r[0])]), [g3.reshape(ndev, nrow3 * d)], [], [], [(1, nrow3 * d)],
                      name="small_sum")[0].reshape(nrow3, d)
    offs = [0]
    for n_ in small_rows:
        offs.append(offs[-1] + n_)
    grads = {}
    grads["ada_b"] = summed[offs[0]:offs[1]].reshape(depth, N_MOD * d)
    grads["norm1_g"] = summed[offs[1]:offs[2]]
    grads["norm2_g"] = summed[offs[2]:offs[3]]
    conv_sum = summed[offs[3]:offs[4]].reshape(ndn, DN_CONV, nchips * conv_cols)
    grads["dn_conv_w"] = lax.dynamic_slice_in_dim(conv_sum, chip * conv_cols, conv_cols, axis=2)
    mrow = summed[offs[4]:offs[5]].reshape(-1)
    o = 0
    grads["dn_onorm_g"] = mrow[o:o + ndn * DN_HEAD_DIM].reshape(ndn, DN_HEAD_DIM)
    o += ndn * DN_HEAD_DIM
    grads["dn_a_log"] = mrow[o:o + ndn * LANES].reshape(ndn, LANES)[:, :heads]
    o += ndn * LANES
    grads["dn_dt_bias"] = mrow[o:o + ndn * LANES].reshape(ndn, LANES)[:, :heads]
    o += ndn * LANES
    grads["sb_q_norm_g"] = mrow[o:o + nsb * SB_HEAD_DIM].reshape(nsb, SB_HEAD_DIM)
    o += nsb * SB_HEAD_DIM
    grads["sb_k_norm_g"] = mrow[o:o + nsb * SB_HEAD_DIM].reshape(nsb, SB_HEAD_DIM)
    dmod_all = g3.reshape(ndev, nrow3, d)[:, :small_rows[0]].reshape(ndev, depth, N_MOD * d)
    dmod_mine = lax.dynamic_slice_in_dim(dmod_all, chip * mod_cols, mod_cols, axis=2)
    dmod16 = jnp.pad(dmod_mine, ((0, 16 - ndev), (0, 0), (0, 0)))
    grads["ada_w"] = jnp.stack([_mm(cond16, dmod16[:, i], ta=True, name="ada_gw%d" % i) for i in range(depth)])

    def gw_in(g):
        ga = g["w_all"]
        return jnp.concatenate([ga[:, :4 * d], ga[:, 4 * d:4 * d + heads], ga[:, 4 * d + LANES:4 * d + LANES + heads]],
                               axis=1)
    gfull = dict(dn_w_in=jnp.stack([gw_in(g) for g in g_dn]), dn_w_out=jnp.stack([g["w_out"] for g in g_dn]),
                 sb_w_qkv=jnp.stack([g["w_qkv"] for g in g_sb]), sb_w_out=jnp.stack([g["w_out"] for g in g_sb]),
                 ffn_w_in=jnp.stack([g["w_in"] for g in g_ffn]), ffn_w_out=jnp.stack([g["w_out"] for g in g_ffn]))
    parts = [_to_chip_major(gfull[n], axis, nchips) for n, axis in _BIG]
    from_sibling = _sibling_swap_halves(parts, "gr_pair")
    two_d = lambda a: a.reshape(-1, a.shape[-1])
    pairs = []
    for (n, _), p, fs in zip(_BIG, parts, from_sibling):
        rh = fs.shape[2]
        own = lax.dynamic_slice_in_dim(p, ac * rh, rh, axis=2)
        pairs.append(_rowwise(lambda r, v: ([r[0].astype(F32) + r[1].astype(F32)], []), [two_d(own), two_d(fs)], [],
                              [(fs.shape[-1], BF16)], [], name="gr_pair_add_" + n)[0].reshape(fs.shape))
    from_chips = _chip_scatter(pairs, "gr_chips")
    reduced = [_sum_lead(fc.reshape(nchips, -1, fc.shape[-1]), "gr_chip_add_" + n).reshape(fc.shape[1:])
               for (n, _), fc in zip(_BIG, from_chips)]
    for (n, _), g in zip(_BIG, _share_halves(reduced, "gr_share")):
        grads[n] = jnp.moveaxis(g, 0, 1).reshape(g.shape[1], 2 * g.shape[2], g.shape[3])

    delta, new_m, new_v = {}, {}, {}
    for n in names:
        delta[n], new_m[n], new_v[n] = _adamw(w[n], grads[n], mom[n], var[n], "adamw_" + n)
    return (loss, grad_x[None], *[grads[n] for n in names], *[delta[n] for n in names], *[new_m[n] for n in names],
            *[new_v[n] for n in names])
```

```python
import functools

import jax
import jax.numpy as jnp
from jax import lax
from jax.experimental import pallas as pl
from jax.experimental.pallas import tpu as pltpu

F32 = jnp.float32
BF16 = jnp.bfloat16
HI = lax.Precision.HIGHEST
MESH = pl.DeviceIdType.MESH

EPS = 1e-6
N_MOD = 6
DN_HEAD_DIM = 128
DN_CONV = 4
DN_CHUNK = 64
DN_GROUP = 8
DN_GATE_ROWS = 256
SB_HEAD_DIM = 64
SB_BLOCK = 128
SB_QBLOCK = 256
SB_PART = 128
LANES = 128
VMEM_LIMIT = 56 * 1024 * 1024
MM_BLOCK_BUDGET = 36 * 1024 * 1024
ROW_BLOCK_BUDGET = 20 * 1024 * 1024

ADAM_LR = 0.001
ADAM_B1 = 0.9
ADAM_B2 = 0.999
ADAM_EPS = 1e-08
ADAM_WD = 0.01
ADAM_STEP = 10


def _cparams(sem=None):
    return pltpu.CompilerParams(dimension_semantics=sem, vmem_limit_bytes=VMEM_LIMIT)


def _sigmoid(x):
    return 1.0 / (1.0 + jnp.exp(-x))


def _silu(x):
    return x * _sigmoid(x)


def _dsilu(x):
    s = _sigmoid(x)
    return s * (1.0 + x * (1.0 - s))


def _softplus(x):
    return jnp.maximum(x, 0.0) + jnp.log1p(jnp.exp(-jnp.abs(x)))


def _pick_tile(n, prefs):
    for t in prefs:
        if n % t == 0:
            return t
    return n


def _mm(a, b, *, ta=False, tb=False, out_dtype=F32, name):
    m = a.shape[1] if ta else a.shape[0]
    k = a.shape[0] if ta else a.shape[1]
    n = b.shape[0] if tb else b.shape[1]
    assert (b.shape[1] if tb else b.shape[0]) == k
    size = lambda dt: jnp.dtype(dt).itemsize
    best = None
    for tm in sorted({t for t in (m, 2048, 1024, 512, 256, 128) if m % t == 0 and (t % 128 == 0 or t == m)}):
        for tn in sorted({t for t in (n, 2816, 1408, 1024, 768, 512, 256, 128) if n % t == 0 and (t % 128 == 0 or t == n)}):
            need = 2 * (tm * k * size(a.dtype) + tn * k * size(b.dtype) + tm * tn * size(out_dtype))
            if need <= MM_BLOCK_BUDGET and (best is None or tm * tn > best[0] * best[1]):
                best = (tm, tn)
    tm, tn = best
    dims = (((0 if ta else 1,), (1 if tb else 0,)), ((), ()))

    def body(a_ref, b_ref, o_ref):
        av = a_ref[...].astype(BF16)
        bv = b_ref[...].astype(BF16)
        o_ref[...] = lax.dot_general(av, bv, dims, preferred_element_type=F32).astype(o_ref.dtype)

    a_spec = pl.BlockSpec((k, tm), lambda i, j: (0, i)) if ta else pl.BlockSpec((tm, k), lambda i, j: (i, 0))
    b_spec = pl.BlockSpec((tn, k), lambda i, j: (j, 0)) if tb else pl.BlockSpec((k, tn), lambda i, j: (0, j))
    return pl.pallas_call(
        body, grid=(m // tm, n // tn), in_specs=[a_spec, b_spec],
        out_specs=pl.BlockSpec((tm, tn), lambda i, j: (i, j)),
        out_shape=jax.ShapeDtypeStruct((m, n), out_dtype), name=name,
        compiler_params=_cparams(("parallel", "parallel")))(a, b)


def _rowwise(body, rows, vecs, outs, accs, *, name, tr=None):
    rows = [r if isinstance(r, tuple) else (r, r.shape[1], 0) for r in rows]
    nrows = rows[0][0].shape[0]
    if tr is None:
        per_row = sum(w * jnp.dtype(a.dtype).itemsize for a, w, _ in rows) + sum(
            w * jnp.dtype(dt).itemsize for w, dt in outs)
        tr = next((t for t in (2048, 1024, 512, 256) if nrows % t == 0 and 2 * t * per_row <= ROW_BLOCK_BUDGET), 256)
    tr = tr if nrows % tr == 0 else nrows
    nr, nv, no = len(rows), len(vecs), len(outs)

    def kern(*refs):
        r_in, v_in = refs[:nr], refs[nr:nr + nv]
        o_refs, a_refs = refs[nr + nv:nr + nv + no], refs[nr + nv + no:]
        res_o, res_a = body([r[...] for r in r_in], [v[...] for v in v_in])
        for o, val in zip(o_refs, res_o):
            o[...] = val.astype(o.dtype)
        if a_refs:
            @pl.when(pl.program_id(0) == 0)
            def _():
                for a in a_refs:
                    a[...] = jnp.zeros(a.shape, a.dtype)
            for a, val in zip(a_refs, res_a):
                a[...] += val

    in_specs = [pl.BlockSpec((tr, w), functools.partial(lambda i, cb: (i, cb), cb=cb)) for _, w, cb in rows]
    in_specs += [pl.BlockSpec(v.shape, functools.partial(lambda i, nd: (0,) * nd, nd=v.ndim)) for v in vecs]
    out_specs = [pl.BlockSpec((tr, w), lambda i: (i, 0)) for w, _ in outs]
    out_specs += [pl.BlockSpec(s, lambda i: (0, 0)) for s in accs]
    out_shape = [jax.ShapeDtypeStruct((nrows, w), dt) for w, dt in outs]
    out_shape += [jax.ShapeDtypeStruct(s, F32) for s in accs]
    res = pl.pallas_call(
        kern, grid=(nrows // tr,), in_specs=in_specs, out_specs=out_specs, out_shape=out_shape, name=name,
        compiler_params=_cparams(("arbitrary",) if accs else ("parallel",)))(*[r[0] for r in rows], *vecs)
    return res


def _colsum(v):
    return jnp.sum(v, axis=0, keepdims=True)


def _adaln_fwd(x, g, sc, sh, name):
    def body(r, v):
        (xv,), (gv, scv, shv) = r, v
        rs = lax.rsqrt(jnp.mean(xv * xv, axis=-1, keepdims=True) + EPS)
        return [(xv * rs) * gv * (1.0 + scv) + shv], []
    return _rowwise(body, [x], [g, sc, sh], [(x.shape[1], BF16)], [], name=name)[0]


def _adaln_bwd(x, dh, dxr, g, sc, name):
    d = x.shape[1]

    def body(r, v):
        (xv, dhv, dxv), (gv, scv) = r, v
        rs = lax.rsqrt(jnp.mean(xv * xv, axis=-1, keepdims=True) + EPS)
        nv = xv * rs
        dn = dhv * (gv * (1.0 + scv))
        dx = rs * (dn - nv * jnp.mean(dn * nv, axis=-1, keepdims=True)) + dxv
        dhn = dhv * nv
        return [dx], [_colsum(dhn * (1.0 + scv)), _colsum(dhn * gv), _colsum(dhv)]
    return _rowwise(body, [x, dh, dxr], [g, sc], [(d, F32)], [(1, d)] * 3, name=name)


def _resid_fwd(x, y, gt, name):
    def body(r, v):
        return [r[0] + v[0] * r[1]], []
    return _rowwise(body, [x, y], [gt], [(x.shape[1], F32)], [], name=name)[0]


def _resid_bwd(dx, y, gt, name):
    d = dx.shape[1]

    def body(r, v):
        return [v[0] * r[0]], [_colsum(r[0] * r[1])]
    return _rowwise(body, [dx, y], [gt], [(d, BF16)], [(1, d)], name=name)


def _swiglu_fwd(u, name):
    f = u.shape[1] // 2

    def body(r, v):
        uv = r[0]
        return [_silu(uv[:, :f]) * uv[:, f:]], []
    return _rowwise(body, [u], [], [(f, BF16)], [], name=name)[0]


def _swiglu_bwd(u, da, name):
    f = u.shape[1] // 2

    def body(r, v):
        uv, dav = r
        gate, up = uv[:, :f], uv[:, f:]
        return [jnp.concatenate([dav * up * _dsilu(gate), dav * _silu(gate)], axis=1)], []
    return _rowwise(body, [u, da], [], [(2 * f, BF16)], [], name=name)[0]


def _loss_fwd_bwd(y, tgt, name):
    d = y.shape[1]

    def body(r, v):
        err = r[0] - r[1]
        part = jnp.sum(jnp.sum(err * err, axis=1, keepdims=True), axis=0, keepdims=True) * (0.5 / d)
        return [err * (1.0 / d)], [jnp.broadcast_to(part, (1, LANES))]
    dy, acc = _rowwise(body, [y, tgt], [], [(d, F32)], [(1, LANES)], name=name)
    return acc[0, 0], dy


def _head_means(v):
    row = lax.broadcasted_iota(jnp.int32, (LANES, LANES), 0)
    col = lax.broadcasted_iota(jnp.int32, (LANES, LANES), 1)
    same = ((row // SB_HEAD_DIM) == (col // SB_HEAD_DIM)).astype(F32)
    parts = [jnp.dot(v[:, g * LANES:(g + 1) * LANES], same, precision=HI, preferred_element_type=F32)
             for g in range(v.shape[1] // LANES)]
    return jnp.concatenate(parts, axis=1) * (1.0 / SB_HEAD_DIM)


def _sb_norm_fwd(qkv, gq, gk, name):
    d = qkv.shape[1] // 3

    def body(r, v):
        return [x * lax.rsqrt(_head_means(x * x) + EPS) * g for x, g in zip(r, v)], []
    return _rowwise(body, [(qkv, d, 0), (qkv, d, 1)], [gq, gk], [(d, F32), (d, F32)], [], name=name)


def _sb_norm_bwd(qkv, dqn, dkn, dv, gq, gk, name):
    d = qkv.shape[1] // 3

    def body(r, v):
        outs, accs = [], []
        for x, dy, g in ((r[0], r[2], v[0]), (r[1], r[3], v[1])):
            rs = lax.rsqrt(_head_means(x * x) + EPS)
            nv = x * rs
            dn = dy * g
            outs.append(rs * (dn - nv * _head_means(dn * nv)))
            accs.append(_colsum(dy * nv))
        return [jnp.concatenate(outs + [r[4]], axis=1)], accs
    return _rowwise(body, [(qkv, d, 0), (qkv, d, 1), dqn, dkn, dv], [gq, gk], [(3 * d, BF16)], [(1, d), (1, d)],
                    name=name)


def _head_tiles(a):
    return [a[:, h * DN_HEAD_DIM:(h + 1) * DN_HEAD_DIM] for h in range(a.shape[1] // DN_HEAD_DIM)]


def _dn_post_fwd(o, proj, g, name):
    d = o.shape[1]

    def body(r, v):
        outs = []
        for ov, zv in zip(_head_tiles(r[0]), _head_tiles(r[1])):
            rs = lax.rsqrt(jnp.mean(ov * ov, axis=-1, keepdims=True) + EPS)
            outs.append(ov * rs * v[0] * _silu(zv))
        return [jnp.concatenate(outs, axis=1)], []
    return _rowwise(body, [o, (proj, d, 3)], [g], [(d, BF16)], [], name=name)[0]


def _dn_post_bwd(o, proj, don, g, name):
    d = o.shape[1]

    def body(r, v):
        dos, dzs, dg = [], [], jnp.zeros((1, DN_HEAD_DIM), F32)
        for ov, zv, dv in zip(_head_tiles(r[0]), _head_tiles(r[1]), _head_tiles(r[2])):
            rs = lax.rsqrt(jnp.mean(ov * ov, axis=-1, keepdims=True) + EPS)
            nv = ov * rs
            s = _silu(zv)
            dn = dv * v[0] * s
            dos.append(rs * (dn - nv * jnp.mean(dn * nv, axis=-1, keepdims=True)))
            dzs.append(dv * nv * v[0] * _dsilu(zv))
            dg = dg + _colsum(dv * nv * s)
        return [jnp.concatenate(dos, axis=1), jnp.concatenate(dzs, axis=1)], [dg]
    return _rowwise(body, [o, (proj, d, 3), don], [g], [(d, F32), (d, BF16)], [(1, DN_HEAD_DIM)], name=name)


def _chunk_tri(tr, upper):
    row = lax.broadcasted_iota(jnp.int32, (tr, tr), 0)
    col = lax.broadcasted_iota(jnp.int32, (tr, tr), 1)
    same = (row // DN_CHUNK) == (col // DN_CHUNK)
    return (same & ((row <= col) if upper else (row >= col))).astype(F32)


def _dn_gate_fwd(proj, colblk, a_log, dt_bias, name):
    def body(r, v):
        ab = r[0]
        a, b = ab[:, :LANES], ab[:, LANES:]
        g = -jnp.exp(v[0]) * _softplus(a + v[1])
        big_g = jnp.dot(_chunk_tri(g.shape[0], False), g, precision=HI, preferred_element_type=F32)
        return [big_g, _sigmoid(b)], []
    return _rowwise(body, [(proj, 2 * LANES, colblk)], [a_log, dt_bias], [(LANES, F32), (LANES, F32)], [], name=name,
                    tr=DN_GATE_ROWS)


def _dn_gate_bwd(proj, colblk, d_big_g, dbeta, a_log, dt_bias, name):
    def body(r, v):
        ab, dgc, dbt = r
        a, b = ab[:, :LANES], ab[:, LANES:]
        dg = jnp.dot(_chunk_tri(dgc.shape[0], True), dgc, precision=HI, preferred_element_type=F32)
        na = -jnp.exp(v[0])
        pre = a + v[1]
        da = dg * na * _sigmoid(pre)
        beta = _sigmoid(b)
        db = dbt * beta * (1.0 - beta)
        return [jnp.concatenate([da, db], axis=1)], [_colsum(dg * na * _softplus(pre)), _colsum(da)]
    return _rowwise(body, [(proj, 2 * LANES, colblk), d_big_g, dbeta], [a_log, dt_bias],
                    [(2 * LANES, BF16)], [(1, LANES), (1, LANES)], name=name, tr=DN_GATE_ROWS)


def _shift_rows(x, s):
    if s == 0:
        return x
    t = x.shape[0]
    row = lax.broadcasted_iota(jnp.int32, x.shape, 0)
    rolled = pltpu.roll(x, s % t, axis=0)
    return jnp.where((row >= s) if s > 0 else (row < t + s), rolled, 0.0)


def _rnd(x):
    return x.astype(BF16).astype(F32)


def _conv(x, c_ref):
    c = x * _rnd(c_ref[DN_CONV - 1:DN_CONV, :])
    for s in range(1, DN_CONV):
        c = c + _shift_rows(x, s) * _rnd(c_ref[DN_CONV - 1 - s:DN_CONV - s, :])
    return c


def _dn_prep_fwd(proj, conv_w, heads, name):
    t = proj.shape[0]
    d = heads * DN_HEAD_DIM

    def body(xq, xk, xv, cq, ck, cv, q_ref, k_ref, v_ref):
        for x_ref, c_ref, o_ref, norm in ((xq, cq, q_ref, True), (xk, ck, k_ref, True), (xv, cv, v_ref, False)):
            y = _silu(_conv(_rnd(x_ref[...]), c_ref))
            if norm:
                y = y * lax.rsqrt(jnp.sum(y * y, axis=-1, keepdims=True) + EPS)
            o_ref[...] = y

    xs = [pl.BlockSpec((t, DN_HEAD_DIM), functools.partial(lambda h, o: (0, h + o), o=o * heads)) for o in range(3)]
    cs = [pl.BlockSpec((DN_CONV, DN_HEAD_DIM), functools.partial(lambda h, o: (0, h + o), o=o * heads)) for o in range(3)]
    return pl.pallas_call(
        body, grid=(heads,), in_specs=xs + cs,
        out_specs=[pl.BlockSpec((t, DN_HEAD_DIM), lambda h: (0, h))] * 3,
        out_shape=[jax.ShapeDtypeStruct((t, d), F32)] * 3, name=name,
        compiler_params=_cparams(("parallel",)))(proj, proj, proj, conv_w, conv_w, conv_w)


def _dn_prep_bwd(proj, conv_w, dq, dk, dv, heads, name):
    t = proj.shape[0]
    d = heads * DN_HEAD_DIM

    def body(xq, xk, xv, cq, ck, cv, gq, gk, gv, oq, ok, ov, wq, wk, wv):
        for x_ref, c_ref, g_ref, o_ref, w_ref, norm in ((xq, cq, gq, oq, wq, True), (xk, ck, gk, ok, wk, True),
                                                        (xv, cv, gv, ov, wv, False)):
            x, dy = _rnd(x_ref[...]), g_ref[...]
            c = _conv(x, c_ref)
            if norm:
                y = _silu(c)
                rs = lax.rsqrt(jnp.sum(y * y, axis=-1, keepdims=True) + EPS)
                yn = y * rs
                dy = rs * (dy - yn * jnp.sum(dy * yn, axis=-1, keepdims=True))
            dc = _rnd(dy * _dsilu(c))
            dx = dc * _rnd(c_ref[DN_CONV - 1:DN_CONV, :])
            w_ref[DN_CONV - 1:DN_CONV, :] = _colsum(dc * x)
            for s in range(1, DN_CONV):
                dx = dx + _shift_rows(dc, -s) * _rnd(c_ref[DN_CONV - 1 - s:DN_CONV - s, :])
                w_ref[DN_CONV - 1 - s:DN_CONV - s, :] = _colsum(dc * _shift_rows(x, s))
            o_ref[...] = dx.astype(o_ref.dtype)

    xs = [pl.BlockSpec((t, DN_HEAD_DIM), functools.partial(lambda h, o: (0, h + o), o=o * heads)) for o in range(3)]
    cs = [pl.BlockSpec((DN_CONV, DN_HEAD_DIM), functools.partial(lambda h, o: (0, h + o), o=o * heads)) for o in range(3)]
    hs = pl.BlockSpec((t, DN_HEAD_DIM), lambda h: (0, h))
    ws = pl.BlockSpec((DN_CONV, DN_HEAD_DIM), lambda h: (0, h))
    return pl.pallas_call(
        body, grid=(heads,), in_specs=xs + cs + [hs] * 3, out_specs=[hs] * 3 + [ws] * 3,
        out_shape=[jax.ShapeDtypeStruct((t, d), BF16)] * 3 + [jax.ShapeDtypeStruct((DN_CONV, d), F32)] * 3, name=name,
        compiler_params=_cparams(("parallel",)))(proj, proj, proj, conv_w, conv_w, conv_w, dq, dk, dv)


X3 = "three bf16 passes"


def _bdot(a, b, dims, prec):
    if prec != X3:
        return lax.dot_general(a, b, dims, precision=prec, preferred_element_type=F32)
    a1, b1 = a.astype(BF16), b.astype(BF16)
    a2, b2 = (a - a1.astype(F32)).astype(BF16), (b - b1.astype(F32)).astype(BF16)
    dot = lambda x, y: lax.dot_general(x, y, dims, preferred_element_type=F32)
    return dot(a1, b1) + (dot(a1, b2) + dot(a2, b1))


def _bmm(a, b, prec=None):
    return _bdot(a, b, (((2,), (1,)), ((0,), (0,))), prec)


def _bmm_nt(a, b, prec=None):
    return _bdot(a, b, (((2,), (2,)), ((0,), (0,))), prec)


def _bmm_tn(a, b, prec=None):
    return _bdot(a, b, (((1,), (1,)), ((0,), (0,))), prec)


def _dn_local(qg, kg, vg, gc, gr, bt):
    c = qg.shape[1]
    row = lax.broadcasted_iota(jnp.int32, (c, c), 0)
    col = lax.broadcasted_iota(jnp.int32, (c, c), 1)
    incl, strict = (row >= col)[None], (row > col)[None]
    decay = jnp.where(incl, jnp.exp(jnp.where(incl, gc - gr, 0.0)), 0.0)
    kb = kg * bt
    vb = vg * bt
    m = _bmm_nt(kb.astype(BF16), kg.astype(BF16))
    a = jnp.where(strict, m * decay, 0.0)
    bp = -a
    tm = jnp.where((row == col)[None], 1.0, 0.0) + bp
    steps = max(1, (c - 1).bit_length()) - 1
    for _ in range(steps):
        bp = _bmm(bp, bp, X3)
        tm = tm + _bmm(tm, bp, X3)
    eg = jnp.exp(gc)
    glast = gc[:, c - 1:c, :]
    erel = jnp.exp(glast - gc)
    kbg = kb * eg
    qm = _bmm_nt(qg.astype(BF16), kg.astype(BF16))
    return dict(decay=decay, strict=strict, kb=kb, vb=vb, m=m, tm=tm, eg=eg, erel=erel, kbg=kbg, qm=qm,
                u=_bmm(tm, vb, X3), w=_bmm(tm, kbg, X3), qd=qg * eg, kt=kg * erel, gl=jnp.exp(glast))


def _dot(a, b):
    return jnp.dot(a.astype(BF16), b.astype(BF16), preferred_element_type=F32)


def _dot_nt(a, b):
    return lax.dot_general(a.astype(BF16), b.astype(BF16), (((1,), (1,)), ((), ())), preferred_element_type=F32)


def _dot_tn(a, b):
    return lax.dot_general(a.astype(BF16), b.astype(BF16), (((0,), (0,)), ((), ())), preferred_element_type=F32)


def _dn_chunk_specs(t, heads):
    n = t // DN_CHUNK
    hs = pl.BlockSpec((t, DN_HEAD_DIM), lambda h: (0, h))
    gcs = pl.BlockSpec((1, n, DN_CHUNK, 1), lambda h: (h, 0, 0, 0))
    grs = pl.BlockSpec((1, n, 1, DN_CHUNK), lambda h: (h, 0, 0, 0))
    ss = pl.BlockSpec((1, n, DN_HEAD_DIM, DN_HEAD_DIM), lambda h: (h, 0, 0, 0))
    return n, hs, gcs, grs, ss


def _dn_chunk_fwd(q, k, v, gc, gr, bc, name):
    t, d = q.shape
    heads = d // DN_HEAD_DIM
    c, dk = DN_CHUNK, DN_HEAD_DIM
    n, hs, gcs, grs, ss = _dn_chunk_specs(t, heads)
    nb = min(DN_GROUP, n)
    scale = dk ** -0.5

    def body(q_ref, k_ref, v_ref, gc_ref, gr_ref, b_ref, o_ref, s_ref, u_s, w_s, at_s, qd_s, kt_s, gl_s):
        def group(gi, carry):
            r0 = pl.multiple_of(gi * (nb * c), nb * c)
            n0 = gi * nb
            ld = lambda ref: ref[pl.ds(r0, nb * c), :].reshape(nb, c, dk)
            loc = _dn_local(ld(q_ref) * scale, ld(k_ref), ld(v_ref), gc_ref[0, pl.ds(n0, nb)],
                            gr_ref[0, pl.ds(n0, nb)], b_ref[0, pl.ds(n0, nb)])
            u_s[pl.ds(n0, nb)] = loc["u"]
            w_s[pl.ds(n0, nb)] = loc["w"]
            at_s[pl.ds(n0, nb)] = loc["qm"] * loc["decay"]
            qd_s[pl.ds(n0, nb)] = loc["qd"]
            kt_s[pl.ds(n0, nb)] = loc["kt"]
            gl_s[pl.ds(n0, nb)] = loc["gl"]
            return carry
        lax.fori_loop(0, n // nb, group, 0)

        def chunk(i, s):
            s_ref[0, i] = s
            vnew = u_s[i] - _dot(w_s[i], s)
            o = _dot(qd_s[i], s) + _dot(at_s[i], vnew)
            o_ref[pl.ds(pl.multiple_of(i * c, c), c), :] = o
            return s * gl_s[i] + _dot_tn(kt_s[i], vnew)
        lax.fori_loop(0, n, chunk, jnp.zeros((dk, dk), F32))

    scratch = [pltpu.VMEM((n, c, dk), F32), pltpu.VMEM((n, c, dk), F32), pltpu.VMEM((n, c, c), F32),
               pltpu.VMEM((n, c, dk), F32), pltpu.VMEM((n, c, dk), F32), pltpu.VMEM((n, 1, 1), F32)]
    return pl.pallas_call(
        body, grid=(heads,), in_specs=[hs, hs, hs, gcs, grs, gcs], out_specs=[hs, ss],
        out_shape=[jax.ShapeDtypeStruct((t, d), F32), jax.ShapeDtypeStruct((heads, n, dk, dk), F32)],
        scratch_shapes=scratch, name=name, compiler_params=_cparams(("parallel",)))(q, k, v, gc, gr, bc)


def _dn_chunk_bwd(q, k, v, gc, gr, bc, s_all, do, name):
    t, d = q.shape
    heads = d // DN_HEAD_DIM
    c, dk = DN_CHUNK, DN_HEAD_DIM
    n, hs, gcs, grs, ss = _dn_chunk_specs(t, heads)
    nb = min(DN_GROUP, n)
    scale = dk ** -0.5

    def body(q_ref, k_ref, v_ref, gc_ref, gr_ref, b_ref, s_ref, do_ref,
             dq_ref, dk_ref, dv_ref, dgc_ref, dgr_ref, db_ref,
             u_s, w_s, att_s, qd_s, kt_s, gl_s, du_s, dw_s, dat_s, dqd_s, dkt_s, dgl_s):
        def load_group(gi):
            r0 = pl.multiple_of(gi * (nb * c), nb * c)
            n0 = gi * nb
            ld = lambda ref: ref[pl.ds(r0, nb * c), :].reshape(nb, c, dk)
            qg, kg, vg = ld(q_ref) * scale, ld(k_ref), ld(v_ref)
            gcv, grv, bt = gc_ref[0, pl.ds(n0, nb)], gr_ref[0, pl.ds(n0, nb)], b_ref[0, pl.ds(n0, nb)]
            return r0, n0, qg, kg, vg, gcv, grv, bt, _dn_local(qg, kg, vg, gcv, grv, bt)

        def group_a(gi, carry):
            _, n0, qg, kg, _, gcv, grv, _, loc = load_group(gi)
            row = lax.broadcasted_iota(jnp.int32, (c, c), 0)
            col = lax.broadcasted_iota(jnp.int32, (c, c), 1)
            upper = (col >= row)[None]
            decay_t = jnp.where(upper, jnp.exp(jnp.where(upper, grv - gcv, 0.0)), 0.0)
            u_s[pl.ds(n0, nb)] = loc["u"]
            w_s[pl.ds(n0, nb)] = loc["w"]
            att_s[pl.ds(n0, nb)] = _bmm_nt(kg.astype(BF16), qg.astype(BF16)) * decay_t
            qd_s[pl.ds(n0, nb)] = loc["qd"]
            kt_s[pl.ds(n0, nb)] = loc["kt"]
            gl_s[pl.ds(n0, nb)] = loc["gl"]
            return carry
        lax.fori_loop(0, n // nb, group_a, 0)

        def chunk_b(it, ds_next):
            i = n - 1 - it
            s = s_ref[0, i]
            dov = do_ref[pl.ds(pl.multiple_of(i * c, c), c), :]
            w, kt = w_s[i], kt_s[i]
            vnew = u_s[i] - _dot(w, s)
            dvnew = _dot(att_s[i], dov) + _dot(kt, ds_next)
            du_s[i] = dvnew
            dw_s[i] = -_dot_nt(dvnew, s)
            dat_s[i] = _dot_nt(dov, vnew)
            dqd_s[i] = _dot_nt(dov, s)
            dkt_s[i] = _dot_nt(vnew, ds_next)
            dgl_s[i] = jnp.sum(jnp.sum(ds_next * s, axis=1, keepdims=True), axis=0, keepdims=True)
            return ds_next * gl_s[i] + _dot_tn(qd_s[i], dov) - _dot_tn(w, dvnew)
        lax.fori_loop(0, n, chunk_b, jnp.zeros((dk, dk), F32))

        def group_c(gi, carry):
            r0, n0, qg, kg, vg, gcv, grv, bt, loc = load_group(gi)
            sl = pl.ds(n0, nb)
            du, dw, dat, dqd, dkt, dgl = du_s[sl], dw_s[sl], dat_s[sl], dqd_s[sl], dkt_s[sl], dgl_s[sl]
            tm, decay, kb, kbg = loc["tm"], loc["decay"], loc["kb"], loc["kbg"]
            dvb = _bmm_tn(tm, du, X3)
            dkbg = _bmm_tn(tm, dw, X3)
            dt = _bmm_nt(du, loc["vb"], X3) + _bmm_nt(dw, kbg, X3)
            da = jnp.where(loc["strict"], -_bmm_tn(tm, _bmm_nt(dt, tm, X3), X3), 0.0)
            dms = (da * decay).astype(BF16)
            dqs = (dat * decay).astype(BF16)
            kgb = kg.astype(BF16)
            dkb = _bmm(dms, kgb) + dkbg * loc["eg"]
            dqt = _bmm(dqs, kgb) + dqd * loc["eg"]
            dkk = _bmm_tn(dms, kb.astype(BF16)) + _bmm_tn(dqs, qg.astype(BF16)) + dkt * loc["erel"] + dkb * bt
            e = (da * loc["m"] + dat * loc["qm"]) * decay
            lsum = lambda x: jnp.sum(x, axis=2, keepdims=True)
            dkt_kt = lsum(dkt * loc["kt"])
            dgcv = lsum(e) + lsum(dqd * loc["qd"]) - dkt_kt + lsum(dkbg * kbg)
            dglast = jnp.sum(dkt_kt, axis=1, keepdims=True) + dgl * loc["gl"]
            rowc = lax.broadcasted_iota(jnp.int32, (1, c, 1), 1)
            dgc_ref[0, sl] = dgcv + jnp.where(rowc == c - 1, dglast, 0.0)
            dgr_ref[0, sl] = -jnp.sum(e, axis=1, keepdims=True)
            db_ref[0, sl] = lsum(dkb * kg) + lsum(dvb * vg)
            rows = pl.ds(r0, nb * c)
            dq_ref[rows, :] = (dqt * scale).reshape(nb * c, dk)
            dk_ref[rows, :] = dkk.reshape(nb * c, dk)
            dv_ref[rows, :] = (dvb * bt).reshape(nb * c, dk)
            return carry
        lax.fori_loop(0, n // nb, group_c, 0)

    big = lambda: pltpu.VMEM((n, c, dk), F32)
    sq = lambda: pltpu.VMEM((n, c, c), F32)
    one = lambda: pltpu.VMEM((n, 1, 1), F32)
    scratch = [big(), big(), sq(), big(), big(), one(), big(), big(), sq(), big(), big(), one()]
    return pl.pallas_call(
        body, grid=(heads,), in_specs=[hs, hs, hs, gcs, grs, gcs, ss, hs], out_specs=[hs, hs, hs, gcs, grs, gcs],
        out_shape=[jax.ShapeDtypeStruct((t, d), F32)] * 3 + [
            jax.ShapeDtypeStruct((heads, n, c, 1), F32), jax.ShapeDtypeStruct((heads, n, 1, c), F32),
            jax.ShapeDtypeStruct((heads, n, c, 1), F32)],
        scratch_shapes=scratch, name=name, compiler_params=_cparams(("parallel",)))(q, k, v, gc, gr, bc, s_all, do)


def _dot01(x, m01):
    x1 = x.astype(BF16)
    r1 = x - x1.astype(F32)
    x2 = r1.astype(BF16)
    x3 = (r1 - x2.astype(F32)).astype(BF16)
    dot = lambda a: jnp.dot(a, m01, preferred_element_type=F32)
    return (dot(x1) + dot(x2)) + dot(x3)


def _sb_tile(z, mask, r):
    rows, bk = z.shape
    row = lax.broadcasted_iota(jnp.int32, (bk, bk), 0)
    col = lax.broadcasted_iota(jnp.int32, (bk, bk), 1)
    later = (row > col).astype(BF16)
    parts = [slice(p, p + SB_PART) for p in range(0, rows, SB_PART)]
    lm, ls, cs = [], [], []
    for p in parts:
        lsm = -_softplus(z[p])
        lm.append(lsm if mask is None else jnp.where(mask[p], lsm, 0.0))
        ls.append(z[p] + lsm)
        cs.append(_dot01(lm[-1], later))
    a = []
    for i, p in enumerate(parts):
        ap = jnp.exp(ls[i] + cs[i] + r[p])
        a.append(ap if mask is None else jnp.where(mask[p], ap, 0.0))
    cat = lambda xs: jnp.concatenate(xs, axis=0)
    return cat(a), jnp.exp(cat(ls)), r + jnp.sum(cat(lm), axis=1, keepdims=True)


def _sb_sweep(bq, bk, qi, fetch, visit, carry):
    nd = bq // bk
    row = lax.broadcasted_iota(jnp.int32, (2 * bq, bk), 0)
    row = jnp.where(row >= bq, row - bq, row)
    col = lax.broadcasted_iota(jnp.int32, (2 * bq, bk), 1)
    ahead = lambda j: fetch(jnp.maximum(j - 1, 0))
    pre = fetch(qi * nd + nd - 1)
    for dd in reversed(range(nd)):
        nxt = ahead(qi * nd + dd)
        carry = visit(qi * nd + dd, (col + dd * bk) < row, carry, pre)
        pre = nxt

    def below(it, c):
        j = qi * nd - 1 - it
        nxt = ahead(j)
        return visit(j, None, c[0], c[1]), nxt
    return lax.fori_loop(0, qi * nd, below, (carry, pre))[0]


def _stack_heads(a, h0):
    return jnp.concatenate([jnp.where(h0, a, 0.0), jnp.where(h0, 0.0, a)], axis=0).astype(BF16)


def _side_by_side(a, bq):
    return jnp.concatenate([a[:bq], a[bq:]], axis=1)


def _sb_specs(t, d, bq, vcol):
    qs = pl.BlockSpec((bq, LANES), lambda g, i: (i, g))
    ks = pl.BlockSpec((t, LANES), lambda g, i: (0, g))
    vs = pl.BlockSpec((t, LANES), lambda g, i: (0, g + vcol))
    return qs, ks, vs


def _sb_fwd(qn, kn, qkv, name):
    t, d = qn.shape
    bq, bk = min(SB_QBLOCK, t), min(SB_BLOCK, t)
    scale = SB_HEAD_DIM ** -0.5
    nt = (((1,), (1,)), ((), ()))

    def body(q_ref, k_ref, v_ref, o_ref):
        qi = pl.program_id(1)
        h0 = lax.broadcasted_iota(jnp.int32, (1, LANES), 1) < SB_HEAD_DIM
        q2 = _stack_heads(q_ref[...] * scale, h0)

        tile_rows = lambda j: pl.ds(pl.multiple_of(j * bk, bk), bk)

        def fetch(j):
            return lax.dot_general(q2, k_ref[tile_rows(j), :].astype(BF16), nt, preferred_element_type=F32)

        def visit(j, mask, carry, z):
            acc, r = carry
            a, _, r = _sb_tile(z, mask, r)
            a2 = _side_by_side(a.astype(BF16), bq)
            return acc + jnp.dot(a2, _stack_heads(v_ref[tile_rows(j), :], h0), preferred_element_type=F32), r
        acc, _ = _sb_sweep(bq, bk, qi, fetch, visit, (jnp.zeros((bq, LANES), F32), jnp.zeros((2 * bq, 1), F32)))
        o_ref[...] = acc.astype(o_ref.dtype)

    qs, ks, vs = _sb_specs(t, d, bq, 2 * d // LANES)
    return pl.pallas_call(
        body, grid=(d // LANES, t // bq), in_specs=[qs, ks, vs], out_specs=qs,
        out_shape=jax.ShapeDtypeStruct((t, d), BF16), name=name,
        compiler_params=_cparams(("parallel", "parallel")))(qn, kn, qkv)


def _sb_bwd(qn, kn, qkv, do, name):
    t, d = qn.shape
    bq, bk = min(SB_QBLOCK, t), min(SB_BLOCK, t)
    scale = SB_HEAD_DIM ** -0.5
    nt = (((1,), (1,)), ((), ()))

    def body(q_ref, k_ref, v_ref, do_ref, dq_ref, dk_ref, dv_ref, p_s, sg_s):
        qi = pl.program_id(1)

        @pl.when(qi == 0)
        def _():
            dk_ref[...] = jnp.zeros(dk_ref.shape, F32)
            dv_ref[...] = jnp.zeros(dv_ref.shape, F32)

        h0 = lax.broadcasted_iota(jnp.int32, (1, LANES), 1) < SB_HEAD_DIM
        q2 = _stack_heads(q_ref[...] * scale, h0)
        do2 = _stack_heads(do_ref[...], h0)
        row = lax.broadcasted_iota(jnp.int32, (bk, bk), 0)
        col = lax.broadcasted_iota(jnp.int32, (bk, bk), 1)
        later_incl = (row >= col).astype(BF16)
        zero = jnp.zeros((2 * bq, 1), F32)
        tn = (((0,), (0,)), ((), ()))

        tile_rows = lambda j: pl.ds(pl.multiple_of(j * bk, bk), bk)

        def fetch1(j):
            rows = tile_rows(j)
            return (lax.dot_general(q2, k_ref[rows, :].astype(BF16), nt, preferred_element_type=F32),
                    lax.dot_general(do2, v_ref[rows, :].astype(BF16), nt, preferred_element_type=F32))

        def visit1(j, mask, carry, z_da):
            r, sp = carry
            a, sg, r = _sb_tile(z_da[0], mask, r)
            p = a * z_da[1]
            p_s[j] = p
            sg_s[j] = sg
            dv_ref[tile_rows(j), :] += lax.dot_general(a.astype(BF16), do2, tn, preferred_element_type=F32)
            return r, sp + jnp.sum(p, axis=1, keepdims=True)
        _, total = _sb_sweep(bq, bk, qi, fetch1, visit1, (zero, zero))

        def fetch2(j):
            return _dot01(p_s[j], later_incl)

        def visit2(j, mask, carry, later_p):
            dq, sp = carry
            rows = tile_rows(j)
            p, sg = p_s[j], sg_s[j]
            pref = total - sp - later_p
            dz = p * (1.0 - sg) - pref * sg
            if mask is not None:
                dz = jnp.where(mask, dz, 0.0)
            dz = dz.astype(BF16)
            dk_ref[rows, :] += lax.dot_general(dz, q2, tn, preferred_element_type=F32)
            dq = dq + jnp.dot(_side_by_side(dz, bq), _stack_heads(k_ref[rows, :], h0), preferred_element_type=F32)
            return dq, sp + jnp.sum(p, axis=1, keepdims=True)
        dq, _ = _sb_sweep(bq, bk, qi, fetch2, visit2, (jnp.zeros((bq, LANES), F32), zero))
        dq_ref[...] = dq * scale

    qs, ks, vs = _sb_specs(t, d, bq, 2 * d // LANES)
    shp = jax.ShapeDtypeStruct((t, d), F32)
    scratch = [pltpu.VMEM((t // bk, 2 * bq, bk), F32), pltpu.VMEM((t // bk, 2 * bq, bk), F32)]
    return pl.pallas_call(
        body, grid=(d // LANES, t // bq), in_specs=[qs, ks, vs, qs], out_specs=[qs, ks, ks],
        out_shape=[shp, shp, shp], scratch_shapes=scratch, name=name,
        compiler_params=_cparams(("parallel", "arbitrary")))(qn, kn, qkv, do)


def _adamw(w, g, m, v, name):
    shape = w.shape
    two_d = lambda a: a.reshape(-1, shape[-1])
    c1 = 1.0 - ADAM_B1 ** ADAM_STEP
    c2 = 1.0 - ADAM_B2 ** ADAM_STEP

    def body(r, _):
        wv, gv, mv, vv = r
        mn = ADAM_B1 * mv + (1.0 - ADAM_B1) * gv
        vn = ADAM_B2 * vv + (1.0 - ADAM_B2) * (gv * gv)
        delta = -ADAM_LR * ((mn / c1) / (jnp.sqrt(vn / c2) + ADAM_EPS) + ADAM_WD * wv)
        return [delta, mn, vn], []
    width = shape[-1]
    res = _rowwise(body, [two_d(w), two_d(g), two_d(m), two_d(v)], [], [(width, F32)] * 3, [], name=name)
    return [r.reshape(shape) for r in res]


def _dn_layer_fwd(h1, w, conv_w, a_log, dt_bias, onorm_g, tag):
    t, d = h1.shape
    heads = d // DN_HEAD_DIM
    n = t // DN_CHUNK
    proj = _mm(h1, w["all"], name=tag + "_proj")
    qn, kn, vv = _dn_prep_fwd(proj, conv_w, heads, tag + "_prep")
    big_g, beta = _dn_gate_fwd(proj, 4 * d // (2 * LANES), a_log, dt_bias, tag + "_gate")
    gt_ = big_g[:, :heads].T.reshape(heads, n, DN_CHUNK)
    gc, gr = gt_[..., None], gt_[:, :, None, :]
    bc = beta[:, :heads].T.reshape(heads, n, DN_CHUNK)[..., None]
    o, s_all = _dn_chunk_fwd(qn, kn, vv, gc, gr, bc, tag + "_chunk")
    on = _dn_post_fwd(o, proj, onorm_g, tag + "_post")
    y = _mm(on, w["out"], name=tag + "_out")
    return y, dict(proj=proj, qn=qn, kn=kn, v=vv, gc=gc, gr=gr, bc=bc, o=o, s_all=s_all, on=on)


def _dn_layer_bwd(dy, h1, w, conv_w, a_log, dt_bias, onorm_g, sv, tag):
    t, d = h1.shape
    heads = d // DN_HEAD_DIM
    don = _mm(dy, w["out"], tb=True, name=tag + "_dout")
    g_out = _mm(sv["on"], dy, ta=True, out_dtype=BF16, name=tag + "_gwout")
    do, dz, g_on = _dn_post_bwd(sv["o"], sv["proj"], don, onorm_g, tag + "_dpost")
    dq, dk, dv, dgc, dgr, dbc = _dn_chunk_bwd(sv["qn"], sv["kn"], sv["v"], sv["gc"], sv["gr"], sv["bc"], sv["s_all"],
                                              do, tag + "_dchunk")
    pad = lambda a: jnp.pad(a.reshape(heads, t).T, ((0, 0), (0, LANES - heads)))
    d_big_g = pad(dgc) + pad(dgr)
    dab, g_alog, g_dt = _dn_gate_bwd(sv["proj"], 4 * d // (2 * LANES), d_big_g, pad(dbc), a_log, dt_bias, tag + "_dgate")
    dxq, dxk, dxv, wq, wk, wv = _dn_prep_bwd(sv["proj"], conv_w, dq, dk, dv, heads, tag + "_dprep")
    dproj = jnp.concatenate([dxq, dxk, dxv, dz, dab], axis=1)
    dh1 = _mm(dproj, w["all"], tb=True, name=tag + "_dh")
    g_all = _mm(h1, dproj, ta=True, out_dtype=BF16, name=tag + "_gwin")
    grads = dict(w_all=g_all, w_out=g_out, conv_w=jnp.concatenate([wq, wk, wv], axis=1), a_log=g_alog, dt_bias=g_dt,
                 onorm_g=g_on)
    return dh1, grads


def _sb_layer_fwd(h1, w, q_g, k_g, tag):
    t, d = h1.shape
    heads = d // SB_HEAD_DIM
    qkv = _mm(h1, w["qkv"], name=tag + "_proj")
    gq, gk = jnp.tile(q_g, (1, heads)), jnp.tile(k_g, (1, heads))
    qn, kn = _sb_norm_fwd(qkv, gq, gk, tag + "_norm")
    o = _sb_fwd(qn, kn, qkv, tag + "_attn")
    y = _mm(o, w["out"], name=tag + "_out")
    return y, dict(qkv=qkv, qn=qn, kn=kn, o=o, gq=gq, gk=gk)


def _sb_layer_bwd(dy, h1, w, q_g, k_g, sv, tag):
    t, d = h1.shape
    heads = d // SB_HEAD_DIM
    do = _mm(dy, w["out"], tb=True, name=tag + "_dout")
    g_out = _mm(sv["o"], dy, ta=True, out_dtype=BF16, name=tag + "_gwout")
    dqn, dkn, dv = _sb_bwd(sv["qn"], sv["kn"], sv["qkv"], do, tag + "_dattn")
    dqkv, g_q, g_k = _sb_norm_bwd(sv["qkv"], dqn, dkn, dv, sv["gq"], sv["gk"], tag + "_dnorm")
    fold = lambda g: jnp.sum(g.reshape(heads, SB_HEAD_DIM), axis=0, keepdims=True)
    dh1 = _mm(dqkv, w["qkv"], tb=True, name=tag + "_dh")
    g_qkv = _mm(h1, dqkv, ta=True, out_dtype=BF16, name=tag + "_gwin")
    return dh1, dict(w_qkv=g_qkv, w_out=g_out, q_norm_g=fold(g_q), k_norm_g=fold(g_k))


def _local_step(x, tgt, mod, norm1_g, norm2_g, dn, sb, ffn):
    depth = mod.shape[0]
    d = x.shape[1]
    saved = []
    for i in range(depth):
        mv = [mod[i:i + 1, j * d:(j + 1) * d] for j in range(N_MOD)]
        sh1, sc1, gt1, sh2, sc2, gt2 = mv
        tag = "l%d" % i
        h1 = _adaln_fwd(x, norm1_g[i:i + 1], sc1, sh1, tag + "_ln1")
        if i % 2 == 0:
            p = dn[i // 2]
            y, sv = _dn_layer_fwd(h1, p, p["conv_w"], p["a_log"], p["dt_bias"], p["onorm_g"], tag + "_dn")
        else:
            p = sb[i // 2]
            y, sv = _sb_layer_fwd(h1, p, p["q_g"], p["k_g"], tag + "_sb")
        x1 = _resid_fwd(x, y, gt1, tag + "_res1")
        h2 = _adaln_fwd(x1, norm2_g[i:i + 1], sc2, sh2, tag + "_ln2")
        u = _mm(h2, ffn[i]["w_in"], name=tag + "_ffn_in")
        a = _swiglu_fwd(u, tag + "_swiglu")
        y2 = _mm(a, ffn[i]["w_out"], name=tag + "_ffn_out")
        x2 = _resid_fwd(x1, y2, gt2, tag + "_res2")
        saved.append(dict(x0=x, h1=h1, y=y, mix=sv, x1=x1, h2=h2, u=u, a=a, y2=y2))
        x = x2

    loss, dx = _loss_fwd_bwd(x, tgt, "loss")

    dmod, dn1, dn2 = [None] * depth, [None] * depth, [None] * depth
    g_dn, g_sb, g_ffn = [None] * len(dn), [None] * len(sb), [None] * depth
    for i in reversed(range(depth)):
        s = saved[i]
        mv = [mod[i:i + 1, j * d:(j + 1) * d] for j in range(N_MOD)]
        sh1, sc1, gt1, sh2, sc2, gt2 = mv
        tag = "l%d" % i
        dy2, dgt2 = _resid_bwd(dx, s["y2"], gt2, tag + "_dres2")
        da = _mm(dy2, ffn[i]["w_out"], tb=True, name=tag + "_dffn_a")
        g_wout = _mm(s["a"], dy2, ta=True, out_dtype=BF16, name=tag + "_gffn_out")
        du = _swiglu_bwd(s["u"], da, tag + "_dswiglu")
        dh2 = _mm(du, ffn[i]["w_in"], tb=True, name=tag + "_dffn_h")
        g_win = _mm(s["h2"], du, ta=True, out_dtype=BF16, name=tag + "_gffn_in")
        g_ffn[i] = dict(w_in=g_win, w_out=g_wout)
        dx, dg2, dsc2, dsh2 = _adaln_bwd(s["x1"], dh2, dx, norm2_g[i:i + 1], sc2, tag + "_dln2")
        dy, dgt1 = _resid_bwd(dx, s["y"], gt1, tag + "_dres1")
        if i % 2 == 0:
            p = dn[i // 2]
            dh1, g_dn[i // 2] = _dn_layer_bwd(dy, s["h1"], p, p["conv_w"], p["a_log"], p["dt_bias"], p["onorm_g"],
                                               s["mix"], tag + "_dn")
        else:
            p = sb[i // 2]
            dh1, g_sb[i // 2] = _sb_layer_bwd(dy, s["h1"], p, p["q_g"], p["k_g"], s["mix"], tag + "_sb")
        dx, dg1, dsc1, dsh1 = _adaln_bwd(s["x0"], dh1, dx, norm1_g[i:i + 1], sc1, tag + "_dln1")
        dmod[i] = jnp.concatenate([dsh1, dsc1, dgt1, dsh2, dsc2, dgt2], axis=1)
        dn1[i], dn2[i] = dg1, dg2
    return (loss, dx, jnp.concatenate(dmod, axis=0), jnp.concatenate(dn1, axis=0), jnp.concatenate(dn2, axis=0),
            g_dn, g_sb, g_ffn)


def _axes():
    return lax.axis_index("x"), lax.axis_index("y"), lax.axis_index("c")


def _remote(src, dst, send_sem, recv_sem, dev):
    return pltpu.make_async_remote_copy(src_ref=src, dst_ref=dst, send_sem=send_sem, recv_sem=recv_sem,
                                        device_id=dev, device_id_type=MESH)


def _other_chips(x, y):
    return [(1 - x, y), (x, 1 - y), (1 - x, 1 - y)]


def _allgather_small(v, name):
    m, n = v.shape

    def body(x_ref, out_ref, send_sems, recv_sems, local_sem):
        x, y, c = _axes()
        me, sibling = (x, y, c), (x, y, 1 - c)
        chips = _other_chips(x, y)

        def rows(px, py, pc):
            return out_ref.at[pl.ds((4 * px + 2 * py + pc) * m, m), :]

        def copy(k, block, to, src=None):
            return _remote(rows(*block) if src is None else src, rows(*block), send_sems.at[k], recv_sems.at[k], to)

        mine = pltpu.make_async_copy(x_ref, rows(*me), local_sem)
        mine.start()
        first = [copy(0, me, sibling, src=x_ref)]
        first += [copy(1 + j, me, (*chip, c), src=x_ref) for j, chip in enumerate(chips)]
        for cp in first:
            cp.start()
        passed = [copy(4 + j, (*chip, c), sibling) for j, chip in enumerate(chips)]
        for j, chip in enumerate(chips):
            copy(1 + j, (*chip, c), me).wait_recv()
            passed[j].start()
        copy(0, sibling, me).wait_recv()
        for j, chip in enumerate(chips):
            copy(4 + j, (*chip, 1 - c), me).wait_recv()
        for cp in first + passed:
            cp.wait_send()
        mine.wait()

    return pl.pallas_call(
        body, out_shape=jax.ShapeDtypeStruct((8 * m, n), v.dtype),
        in_specs=[pl.BlockSpec(memory_space=pltpu.VMEM)], out_specs=pl.BlockSpec(memory_space=pltpu.VMEM),
        scratch_shapes=[pltpu.SemaphoreType.DMA((7,)), pltpu.SemaphoreType.DMA((7,)), pltpu.SemaphoreType.DMA],
        name=name, compiler_params=pltpu.CompilerParams(vmem_limit_bytes=VMEM_LIMIT))(v)


def _half(ref, h, rh):
    return ref.at[(slice(None),) * (len(ref.shape) - 2) + (pl.ds(h * rh, rh), slice(None))]


def _hbm_call(body, ins, out_shapes, n_sems, n_local, name):
    hbm = pl.BlockSpec(memory_space=pltpu.HBM)
    scratch = [pltpu.SemaphoreType.DMA((n_sems,)), pltpu.SemaphoreType.DMA((n_sems,))]
    if n_local:
        scratch.append(pltpu.SemaphoreType.DMA((n_local,)))
    return pl.pallas_call(body, out_shape=out_shapes, in_specs=[hbm] * len(ins), out_specs=[hbm] * len(out_shapes),
                          scratch_shapes=scratch, name=name)(*ins)


def _allgather_weights(shards, name):
    n = len(shards)

    def body(*refs):
        p_refs, out_refs, (send_sems, recv_sems, local_sems) = refs[:n], refs[n:2 * n], refs[2 * n:]
        x, y, c = _axes()
        sibling = (x, y, 1 - c)
        chips = _other_chips(x, y)
        rh = [s.shape[1] // 2 for s in shards]
        blk = lambda i, cx, cy, h: _half(out_refs[i].at[2 * cx + cy], h, rh[i])
        mine = [pltpu.make_async_copy(p_refs[i], out_refs[i].at[2 * x + y], local_sems.at[i]) for i in range(n)]
        for cp in mine:
            cp.start()
        first = [_remote(_half(p_refs[i], c, rh[i]), blk(i, x, y, c), send_sems.at[6 * i + j], recv_sems.at[6 * i + j],
                         (*chip, c)) for i in range(n) for j, chip in enumerate(chips)]
        for cp in first:
            cp.start()
        passed = []
        for j, chip in enumerate(chips):
            for i in range(n):
                _remote(_half(p_refs[i], c, rh[i]), blk(i, *chip, c), send_sems.at[6 * i + j], recv_sems.at[6 * i + j],
                        sibling).wait_recv()
                passed.append(_remote(blk(i, *chip, c), blk(i, *chip, c), send_sems.at[6 * i + 3 + j],
                                      recv_sems.at[6 * i + 3 + j], sibling))
                passed[-1].start()
        for j, chip in enumerate(chips):
            for i in range(n):
                _remote(_half(p_refs[i], c, rh[i]), blk(i, *chip, 1 - c), send_sems.at[6 * i + 3 + j],
                        recv_sems.at[6 * i + 3 + j], sibling).wait_recv()
        for cp in first + passed:
            cp.wait_send()
        for cp in mine:
            cp.wait()

    outs = [jax.ShapeDtypeStruct((4,) + s.shape, s.dtype) for s in shards]
    return _hbm_call(body, shards, outs, 6 * n, n, name)


def _sibling_swap_halves(arrs, name):
    n = len(arrs)

    def body(*refs):
        v_refs, out_refs, (send_sems, recv_sems) = refs[:n], refs[n:2 * n], refs[2 * n:]
        x, y, c = _axes()
        cps = [_remote(_half(v_refs[i], 1 - c, arrs[i].shape[-2] // 2), out_refs[i], send_sems.at[i], recv_sems.at[i],
                       (x, y, 1 - c)) for i in range(n)]
        for cp in cps:
            cp.start()
        for cp in cps:
            cp.wait()

    outs = [jax.ShapeDtypeStruct(a.shape[:-2] + (a.shape[-2] // 2, a.shape[-1]), a.dtype) for a in arrs]
    return _hbm_call(body, arrs, outs, n, 0, name)


def _chip_scatter(arrs, name):
    n = len(arrs)

    def body(*refs):
        v_refs, out_refs, (send_sems, recv_sems, local_sems) = refs[:n], refs[n:2 * n], refs[2 * n:]
        x, y, c = _axes()
        me = 2 * x + y
        chips = _other_chips(x, y)
        mine = [pltpu.make_async_copy(v_refs[i].at[me], out_refs[i].at[me], local_sems.at[i]) for i in range(n)]
        for cp in mine:
            cp.start()
        sends = [_remote(v_refs[i].at[2 * cx + cy], out_refs[i].at[me], send_sems.at[3 * i + j], recv_sems.at[3 * i + j],
                         (cx, cy, c)) for i in range(n) for j, (cx, cy) in enumerate(chips)]
        for cp in sends:
            cp.start()
        for i in range(n):
            for j, (cx, cy) in enumerate(chips):
                _remote(v_refs[i].at[me], out_refs[i].at[2 * cx + cy], send_sems.at[3 * i + j], recv_sems.at[3 * i + j],
                        (cx, cy, c)).wait_recv()
        for cp in sends:
            cp.wait_send()
        for cp in mine:
            cp.wait()

    outs = [jax.ShapeDtypeStruct(a.shape, a.dtype) for a in arrs]
    return _hbm_call(body, arrs, outs, 3 * n, n, name)


def _share_halves(arrs, name):
    n = len(arrs)

    def body(*refs):
        v_refs, out_refs, (send_sems, recv_sems) = refs[:n], refs[n:2 * n], refs[2 * n:]
        x, y, c = _axes()
        cps = [_remote(v_refs[i], out_refs[i], send_sems.at[i], recv_sems.at[i], (x, y, 1 - c)) for i in range(n)]
        for cp in cps:
            cp.start()
        for cp in cps:
            cp.wait()

    outs = [jax.ShapeDtypeStruct(a.shape, a.dtype) for a in arrs]
    return _hbm_call(body, arrs, outs, n, 0, name)


def _sum_lead(v, name):
    k, r, w = v.shape
    tr = _pick_tile(r, (256, 128, 64, 32, 16))

    def body(v_ref, o_ref):
        acc = v_ref[0].astype(F32)
        for i in range(1, k):
            acc = acc + v_ref[i].astype(F32)
        o_ref[...] = acc

    return pl.pallas_call(
        body, grid=(r // tr,), in_specs=[pl.BlockSpec((k, tr, w), lambda i: (0, i, 0))],
        out_specs=pl.BlockSpec((tr, w), lambda i: (i, 0)), out_shape=jax.ShapeDtypeStruct((r, w), F32), name=name,
        compiler_params=_cparams(("parallel",)))(v)


_BIG = (("dn_w_in", 2), ("dn_w_out", 1), ("sb_w_qkv", 2), ("sb_w_out", 1), ("ffn_w_in", 2), ("ffn_w_out", 1))


def _to_chip_major(a, axis, nchips):
    l, r, c = a.shape
    if axis == 2:
        return a.reshape(l, r, nchips, c // nchips).transpose(2, 0, 1, 3)
    return a.reshape(l, nchips, r // nchips, c).transpose(1, 0, 2, 3)


def _from_chip_major(a, axis):
    n, l, r, c = a.shape
    if axis == 2:
        return a.transpose(1, 2, 0, 3).reshape(l, r, n * c)
    return a.transpose(1, 0, 2, 3).reshape(l, n * r, c)


def kernel(x, c, ada_w, ada_b, norm1_g, norm2_g, dn_w_in, dn_conv_w, dn_a_log, dn_dt_bias, dn_onorm_g, dn_w_out, sb_w_qkv, sb_q_norm_g, sb_k_norm_g, sb_w_out, ffn_w_in, ffn_w_out, loss_target, m_ada_w, m_ada_b, m_norm1_g, m_norm2_g, m_dn_w_in, m_dn_conv_w, m_dn_a_log, m_dn_dt_bias, m_dn_onorm_g, m_dn_w_out, m_sb_w_qkv, m_sb_q_norm_g, m_sb_k_norm_g, m_sb_w_out, m_ffn_w_in, m_ffn_w_out, v_ada_w, v_ada_b, v_norm1_g, v_norm2_g, v_dn_w_in, v_dn_conv_w, v_dn_a_log, v_dn_dt_bias, v_dn_onorm_g, v_dn_w_out, v_sb_w_qkv, v_sb_q_norm_g, v_sb_k_norm_g, v_sb_w_out, v_ffn_w_in, v_ffn_w_out):
    names = ("ada_w", "ada_b", "norm1_g", "norm2_g", "dn_w_in", "dn_conv_w", "dn_a_log", "dn_dt_bias", "dn_onorm_g",
             "dn_w_out", "sb_w_qkv", "sb_q_norm_g", "sb_k_norm_g", "sb_w_out", "ffn_w_in", "ffn_w_out")
    w = dict(zip(names, (ada_w, ada_b, norm1_g, norm2_g, dn_w_in, dn_conv_w, dn_a_log, dn_dt_bias, dn_onorm_g,
                         dn_w_out, sb_w_qkv, sb_q_norm_g, sb_k_norm_g, sb_w_out, ffn_w_in, ffn_w_out)))
    mom = dict(zip(names, (m_ada_w, m_ada_b, m_norm1_g, m_norm2_g, m_dn_w_in, m_dn_conv_w, m_dn_a_log, m_dn_dt_bias,
                           m_dn_onorm_g, m_dn_w_out, m_sb_w_qkv, m_sb_q_norm_g, m_sb_k_norm_g, m_sb_w_out, m_ffn_w_in,
                           m_ffn_w_out)))
    var = dict(zip(names, (v_ada_w, v_ada_b, v_norm1_g, v_norm2_g, v_dn_w_in, v_dn_conv_w, v_dn_a_log, v_dn_dt_bias,
                           v_dn_onorm_g, v_dn_w_out, v_sb_w_qkv, v_sb_q_norm_g, v_sb_k_norm_g, v_sb_w_out, v_ffn_w_in,
                           v_ffn_w_out)))
    ax, ay, ac = _axes()
    chip = 2 * ax + ay
    dev = 2 * chip + ac
    t, d = x.shape[1], x.shape[2]
    depth, ndn, nsb = ada_w.shape[0], dn_w_in.shape[0], sb_w_qkv.shape[0]
    heads = d // DN_HEAD_DIM
    mod_cols = ada_w.shape[2]
    conv_cols = dn_conv_w.shape[2]
    nchips, ndev = 4, 8

    conv_rows = ndn * DN_CONV * conv_cols // d
    pay1 = jnp.concatenate([c, dn_conv_w.reshape(conv_rows, d), jnp.zeros((8 - 1 - conv_rows, d), F32)], axis=0)
    g1 = _allgather_small(pay1, "ag_cond").reshape(ndev, 8, d)
    c_all = g1[:, 0]
    conv_full = g1[::2, 1:1 + conv_rows].reshape(nchips, ndn, DN_CONV, conv_cols).transpose(1, 2, 0, 3)
    conv_full = conv_full.reshape(ndn, DN_CONV, nchips * conv_cols)

    c16 = jnp.pad(c_all, ((0, 16 - ndev), (0, 0)))
    cond16 = _rowwise(lambda r, v: ([_silu(r[0])], []), [c16], [], [(d, F32)], [], name="cond_silu")[0]
    pay2 = jnp.concatenate([_mm(cond16, ada_w[i], name="ada_mod%d" % i)[:ndev] for i in range(depth)], axis=0)
    g2 = _allgather_small(pay2, "ag_mod").reshape(ndev, depth, ndev, mod_cols)[::2]
    mod_raw = lax.dynamic_index_in_dim(g2, dev, axis=2, keepdims=False)
    mod_raw = mod_raw.transpose(1, 0, 2).reshape(depth, nchips * mod_cols)
    mod = _rowwise(lambda r, v: ([r[0] + r[1]], []), [mod_raw, ada_b], [], [(nchips * mod_cols, F32)], [],
                   name="ada_bias")[0]

    gathered = _allgather_weights([w[n].astype(BF16) for n, _ in _BIG], "ag_weights")
    full = {n: _from_chip_major(a, axis) for (n, axis), a in zip(_BIG, gathered)}

    padl = lambda v: jnp.pad(v[None, :], ((0, 0), (0, LANES - v.shape[0])))
    padc = lambda a: jnp.pad(a, ((0, 0), (0, LANES - a.shape[1])))
    dn = []
    for j in range(ndn):
        wi = full["dn_w_in"][j]
        w_all = jnp.concatenate([wi[:, :4 * d], padc(wi[:, 4 * d:4 * d + heads]), padc(wi[:, 4 * d + heads:])], axis=1)
        dn.append(dict(all=w_all, out=full["dn_w_out"][j], conv_w=conv_full[j], a_log=padl(dn_a_log[j]),
                       dt_bias=padl(dn_dt_bias[j]), onorm_g=dn_onorm_g[j][None]))
    sb = [dict(qkv=full["sb_w_qkv"][j], out=full["sb_w_out"][j], q_g=sb_q_norm_g[j][None], k_g=sb_k_norm_g[j][None])
          for j in range(nsb)]
    ffn = [dict(w_in=full["ffn_w_in"][i], w_out=full["ffn_w_out"][i]) for i in range(depth)]

    loss_local, grad_x, dmod, g_n1, g_n2, g_dn, g_sb, g_ffn = _local_step(x[0], loss_target[0], mod, norm1_g, norm2_g,
                                                                           dn, sb, ffn)
    loss = lax.psum(loss_local, ("x", "y", "c"))

    g_conv = jnp.stack([g["conv_w"] for g in g_dn])
    misc = jnp.concatenate([jnp.concatenate([g["onorm_g"] for g in g_dn], axis=1),
                            jnp.concatenate([g["a_log"] for g in g_dn], axis=1),
                            jnp.concatenate([g["dt_bias"] for g in g_dn], axis=1),
                            jnp.concatenate([g["q_norm_g"] for g in g_sb], axis=1),
                            jnp.concatenate([g["k_norm_g"] for g in g_sb], axis=1)], axis=1)
    misc = jnp.pad(misc, ((0, 0), (0, -misc.shape[1] % d))).reshape(-1, d)
    small = [dmod.reshape(-1, d), g_n1, g_n2, g_conv.reshape(-1, d), misc]
    small_rows = [s.shape[0] for s in small]
    pad_rows = -sum(small_rows) % 8
    pay3 = jnp.concatenate(small + [jnp.zeros((pad_rows, d), F32)], axis=0)
    nrow3 = pay3.shape[0]
    g3 = _allgather_small(pay3, "ag_small")
    summed = _rowwise(lambda r, v: ([], [_colsum(r[0])]), [g3.reshape(ndev, nrow3 * d)], [], [], [(1, nrow3 * d)],
                      name="small_sum")[0].reshape(nrow3, d)
    offs = [0]
    for n_ in small_rows:
        offs.append(offs[-1] + n_)
    grads = {}
    grads["ada_b"] = summed[offs[0]:offs[1]].reshape(depth, N_MOD * d)
    grads["norm1_g"] = summed[offs[1]:offs[2]]
    grads["norm2_g"] = summed[offs[2]:offs[3]]
    conv_sum = summed[offs[3]:offs[4]].reshape(ndn, DN_CONV, nchips * conv_cols)
    grads["dn_conv_w"] = lax.dynamic_slice_in_dim(conv_sum, chip * conv_cols, conv_cols, axis=2)
    mrow = summed[offs[4]:offs[5]].reshape(-1)
    o = 0
    grads["dn_onorm_g"] = mrow[o:o + ndn * DN_HEAD_DIM].reshape(ndn, DN_HEAD_DIM)
    o += ndn * DN_HEAD_DIM
    grads["dn_a_log"] = mrow[o:o + ndn * LANES].reshape(ndn, LANES)[:, :heads]
    o += ndn * LANES
    grads["dn_dt_bias"] = mrow[o:o + ndn * LANES].reshape(ndn, LANES)[:, :heads]
    o += ndn * LANES
    grads["sb_q_norm_g"] = mrow[o:o + nsb * SB_HEAD_DIM].reshape(nsb, SB_HEAD_DIM)
    o += nsb * SB_HEAD_DIM
    grads["sb_k_norm_g"] = mrow[o:o + nsb * SB_HEAD_DIM].reshape(nsb, SB_HEAD_DIM)
    dmod_all = g3.reshape(ndev, nrow3, d)[:, :small_rows[0]].reshape(ndev, depth, N_MOD * d)
    dmod_mine = lax.dynamic_slice_in_dim(dmod_all, chip * mod_cols, mod_cols, axis=2)
    dmod16 = jnp.pad(dmod_mine, ((0, 16 - ndev), (0, 0), (0, 0)))
    grads["ada_w"] = jnp.stack([_mm(cond16, dmod16[:, i], ta=True, name="ada_gw%d" % i) for i in range(depth)])

    def gw_in(g):
        ga = g["w_all"]
        return jnp.concatenate([ga[:, :4 * d], ga[:, 4 * d:4 * d + heads], ga[:, 4 * d + LANES:4 * d + LANES + heads]],
                               axis=1)
    gfull = dict(dn_w_in=jnp.stack([gw_in(g) for g in g_dn]), dn_w_out=jnp.stack([g["w_out"] for g in g_dn]),
                 sb_w_qkv=jnp.stack([g["w_qkv"] for g in g_sb]), sb_w_out=jnp.stack([g["w_out"] for g in g_sb]),
                 ffn_w_in=jnp.stack([g["w_in"] for g in g_ffn]), ffn_w_out=jnp.stack([g["w_out"] for g in g_ffn]))
    parts = [_to_chip_major(gfull[n], axis, nchips) for n, axis in _BIG]
    from_sibling = _sibling_swap_halves(parts, "gr_pair")
    two_d = lambda a: a.reshape(-1, a.shape[-1])
    pairs = []
    for (n, _), p, fs in zip(_BIG, parts, from_sibling):
        rh = fs.shape[2]
        own = lax.dynamic_slice_in_dim(p, ac * rh, rh, axis=2)
        pairs.append(_rowwise(lambda r, v: ([r[0].astype(F32) + r[1].astype(F32)], []), [two_d(own), two_d(fs)], [],
                              [(fs.shape[-1], BF16)], [], name="gr_pair_add_" + n)[0].reshape(fs.shape))
    from_chips = _chip_scatter(pairs, "gr_chips")
    reduced = [_sum_lead(fc.reshape(nchips, -1, fc.shape[-1]), "gr_chip_add_" + n).reshape(fc.shape[1:])
               for (n, _), fc in zip(_BIG, from_chips)]
    reduced = [r.reshape(-1, r.shape[-1]) for r in reduced]
    for (n, _), mine, other in zip(_BIG, reduced, _share_halves(reduced, "gr_share")):
        l = w[n].shape[0]
        mine, other = mine.reshape(l, -1, mine.shape[-1]), other.reshape(l, -1, other.shape[-1])
        grads[n] = jnp.where(ac == 0, jnp.concatenate([mine, other], axis=1), jnp.concatenate([other, mine], axis=1))

    delta, new_m, new_v = {}, {}, {}
    for n in names:
        delta[n], new_m[n], new_v[n] = _adamw(w[n], grads[n], mom[n], var[n], "adamw_" + n)
    return (loss, grad_x[None], *[grads[n] for n in names], *[delta[n] for n in names], *[new_m[n] for n in names],
            *[new_v[n] for n in names])
```

```python
import functools

import jax
import jax.numpy as jnp
from jax import lax
from jax.experimental import pallas as pl
from jax.experimental.pallas import tpu as pltpu

F32 = jnp.float32
BF16 = jnp.bfloat16
HI = lax.Precision.HIGHEST
MESH = pl.DeviceIdType.MESH

EPS = 1e-6
N_MOD = 6
DN_HEAD_DIM = 128
DN_CONV = 4
DN_CHUNK = 64
DN_GROUP = 8
DN_GATE_ROWS = 256
SB_HEAD_DIM = 64
SB_BLOCK = 128
SB_QBLOCK = 256
SB_PART = 128
LANES = 128
VMEM_LIMIT = 56 * 1024 * 1024
MM_BLOCK_BUDGET = 36 * 1024 * 1024
ROW_BLOCK_BUDGET = 20 * 1024 * 1024

ADAM_LR = 0.001
ADAM_B1 = 0.9
ADAM_B2 = 0.999
ADAM_EPS = 1e-08
ADAM_WD = 0.01
ADAM_STEP = 10


def _cparams(sem=None):
    return pltpu.CompilerParams(dimension_semantics=sem, vmem_limit_bytes=VMEM_LIMIT)


def _sigmoid(x):
    return 1.0 / (1.0 + jnp.exp(-x))


def _silu(x):
    return x * _sigmoid(x)


def _dsilu(x):
    s = _sigmoid(x)
    return s * (1.0 + x * (1.0 - s))


def _softplus(x):
    return jnp.maximum(x, 0.0) + jnp.log1p(jnp.exp(-jnp.abs(x)))


def _pick_tile(n, prefs):
    for t in prefs:
        if n % t == 0:
            return t
    return n


def _mm(a, b, *, ta=False, tb=False, out_dtype=F32, name):
    m = a.shape[1] if ta else a.shape[0]
    k = a.shape[0] if ta else a.shape[1]
    n = b.shape[0] if tb else b.shape[1]
    assert (b.shape[1] if tb else b.shape[0]) == k
    size = lambda dt: jnp.dtype(dt).itemsize
    best = None
    for tm in sorted({t for t in (m, 2048, 1024, 512, 256, 128) if m % t == 0 and (t % 128 == 0 or t == m)}):
        for tn in sorted({t for t in (n, 2816, 1408, 1024, 768, 512, 256, 128) if n % t == 0 and (t % 128 == 0 or t == n)}):
            need = 2 * (tm * k * size(a.dtype) + tn * k * size(b.dtype) + tm * tn * size(out_dtype))
            if need <= MM_BLOCK_BUDGET and (best is None or tm * tn > best[0] * best[1]):
                best = (tm, tn)
    tm, tn = best
    dims = (((0 if ta else 1,), (1 if tb else 0,)), ((), ()))

    def body(a_ref, b_ref, o_ref):
        av = a_ref[...].astype(BF16)
        bv = b_ref[...].astype(BF16)
        o_ref[...] = lax.dot_general(av, bv, dims, preferred_element_type=F32).astype(o_ref.dtype)

    a_spec = pl.BlockSpec((k, tm), lambda i, j: (0, i)) if ta else pl.BlockSpec((tm, k), lambda i, j: (i, 0))
    b_spec = pl.BlockSpec((tn, k), lambda i, j: (j, 0)) if tb else pl.BlockSpec((k, tn), lambda i, j: (0, j))
    return pl.pallas_call(
        body, grid=(m // tm, n // tn), in_specs=[a_spec, b_spec],
        out_specs=pl.BlockSpec((tm, tn), lambda i, j: (i, j)),
        out_shape=jax.ShapeDtypeStruct((m, n), out_dtype), name=name,
        compiler_params=_cparams(("parallel", "parallel")))(a, b)


def _rowwise(body, rows, vecs, outs, accs, *, name, tr=None):
    rows = [r if isinstance(r, tuple) else (r, r.shape[1], 0) for r in rows]
    nrows = rows[0][0].shape[0]
    if tr is None:
        per_row = sum(w * jnp.dtype(a.dtype).itemsize for a, w, _ in rows) + sum(
            w * jnp.dtype(dt).itemsize for w, dt in outs)
        tr = next((t for t in (2048, 1024, 512, 256) if nrows % t == 0 and 2 * t * per_row <= ROW_BLOCK_BUDGET), 256)
    tr = tr if nrows % tr == 0 else nrows
    nr, nv, no = len(rows), len(vecs), len(outs)

    def kern(*refs):
        r_in, v_in = refs[:nr], refs[nr:nr + nv]
        o_refs, a_refs = refs[nr + nv:nr + nv + no], refs[nr + nv + no:]
        res_o, res_a = body([r[...] for r in r_in], [v[...] for v in v_in])
        for o, val in zip(o_refs, res_o):
            o[...] = val.astype(o.dtype)
        if a_refs:
            @pl.when(pl.program_id(0) == 0)
            def _():
                for a in a_refs:
                    a[...] = jnp.zeros(a.shape, a.dtype)
            for a, val in zip(a_refs, res_a):
                a[...] += val

    in_specs = [pl.BlockSpec((tr, w), functools.partial(lambda i, cb: (i, cb), cb=cb)) for _, w, cb in rows]
    in_specs += [pl.BlockSpec(v.shape, functools.partial(lambda i, nd: (0,) * nd, nd=v.ndim)) for v in vecs]
    out_specs = [pl.BlockSpec((tr, w), lambda i: (i, 0)) for w, _ in outs]
    out_specs += [pl.BlockSpec(s, lambda i: (0, 0)) for s in accs]
    out_shape = [jax.ShapeDtypeStruct((nrows, w), dt) for w, dt in outs]
    out_shape += [jax.ShapeDtypeStruct(s, F32) for s in accs]
    res = pl.pallas_call(
        kern, grid=(nrows // tr,), in_specs=in_specs, out_specs=out_specs, out_shape=out_shape, name=name,
        compiler_params=_cparams(("arbitrary",) if accs else ("parallel",)))(*[r[0] for r in rows], *vecs)
    return res


def _colsum(v):
    return jnp.sum(v, axis=0, keepdims=True)


def _adaln_fwd(x, g, sc, sh, name):
    def body(r, v):
        (xv,), (gv, scv, shv) = r, v
        rs = lax.rsqrt(jnp.mean(xv * xv, axis=-1, keepdims=True) + EPS)
        return [(xv * rs) * gv * (1.0 + scv) + shv], []
    return _rowwise(body, [x], [g, sc, sh], [(x.shape[1], BF16)], [], name=name)[0]


def _adaln_bwd(x, dh, dxr, g, sc, name):
    d = x.shape[1]

    def body(r, v):
        (xv, dhv, dxv), (gv, scv) = r, v
        rs = lax.rsqrt(jnp.mean(xv * xv, axis=-1, keepdims=True) + EPS)
        nv = xv * rs
        dn = dhv * (gv * (1.0 + scv))
        dx = rs * (dn - nv * jnp.mean(dn * nv, axis=-1, keepdims=True)) + dxv
        dhn = dhv * nv
        return [dx], [_colsum(dhn * (1.0 + scv)), _colsum(dhn * gv), _colsum(dhv)]
    return _rowwise(body, [x, dh, dxr], [g, sc], [(d, F32)], [(1, d)] * 3, name=name)


def _resid_fwd(x, y, gt, name):
    def body(r, v):
        return [r[0] + v[0] * r[1]], []
    return _rowwise(body, [x, y], [gt], [(x.shape[1], F32)], [], name=name)[0]


def _resid_bwd(dx, y, gt, name):
    d = dx.shape[1]

    def body(r, v):
        return [v[0] * r[0]], [_colsum(r[0] * r[1])]
    return _rowwise(body, [dx, y], [gt], [(d, BF16)], [(1, d)], name=name)


def _swiglu_fwd(u, name):
    f = u.shape[1] // 2

    def body(r, v):
        uv = r[0]
        return [_silu(uv[:, :f]) * uv[:, f:]], []
    return _rowwise(body, [u], [], [(f, BF16)], [], name=name)[0]


def _swiglu_bwd(u, da, name):
    f = u.shape[1] // 2

    def body(r, v):
        uv, dav = r
        gate, up = uv[:, :f], uv[:, f:]
        return [jnp.concatenate([dav * up * _dsilu(gate), dav * _silu(gate)], axis=1)], []
    return _rowwise(body, [u, da], [], [(2 * f, BF16)], [], name=name)[0]


def _loss_fwd_bwd(y, tgt, name):
    d = y.shape[1]

    def body(r, v):
        err = r[0] - r[1]
        part = jnp.sum(jnp.sum(err * err, axis=1, keepdims=True), axis=0, keepdims=True) * (0.5 / d)
        return [err * (1.0 / d)], [jnp.broadcast_to(part, (1, LANES))]
    dy, acc = _rowwise(body, [y, tgt], [], [(d, F32)], [(1, LANES)], name=name)
    return acc[0, 0], dy


def _head_means(v):
    row = lax.broadcasted_iota(jnp.int32, (LANES, LANES), 0)
    col = lax.broadcasted_iota(jnp.int32, (LANES, LANES), 1)
    same = ((row // SB_HEAD_DIM) == (col // SB_HEAD_DIM)).astype(F32)
    parts = [jnp.dot(v[:, g * LANES:(g + 1) * LANES], same, precision=HI, preferred_element_type=F32)
             for g in range(v.shape[1] // LANES)]
    return jnp.concatenate(parts, axis=1) * (1.0 / SB_HEAD_DIM)


def _sb_norm_fwd(qkv, gq, gk, name):
    d = qkv.shape[1] // 3

    def body(r, v):
        return [x * lax.rsqrt(_head_means(x * x) + EPS) * g for x, g in zip(r, v)], []
    return _rowwise(body, [(qkv, d, 0), (qkv, d, 1)], [gq, gk], [(d, F32), (d, F32)], [], name=name)


def _sb_norm_bwd(qkv, dqn, dkn, dv, gq, gk, name):
    d = qkv.shape[1] // 3

    def body(r, v):
        outs, accs = [], []
        for x, dy, g in ((r[0], r[2], v[0]), (r[1], r[3], v[1])):
            rs = lax.rsqrt(_head_means(x * x) + EPS)
            nv = x * rs
            dn = dy * g
            outs.append(rs * (dn - nv * _head_means(dn * nv)))
            accs.append(_colsum(dy * nv))
        return [jnp.concatenate(outs + [r[4]], axis=1)], accs
    return _rowwise(body, [(qkv, d, 0), (qkv, d, 1), dqn, dkn, dv], [gq, gk], [(3 * d, BF16)], [(1, d), (1, d)],
                    name=name)


def _head_tiles(a):
    return [a[:, h * DN_HEAD_DIM:(h + 1) * DN_HEAD_DIM] for h in range(a.shape[1] // DN_HEAD_DIM)]


def _dn_post_fwd(o, proj, g, name):
    d = o.shape[1]

    def body(r, v):
        outs = []
        for ov, zv in zip(_head_tiles(r[0]), _head_tiles(r[1])):
            rs = lax.rsqrt(jnp.mean(ov * ov, axis=-1, keepdims=True) + EPS)
            outs.append(ov * rs * v[0] * _silu(zv))
        return [jnp.concatenate(outs, axis=1)], []
    return _rowwise(body, [o, (proj, d, 3)], [g], [(d, BF16)], [], name=name)[0]


def _dn_post_bwd(o, proj, don, g, name):
    d = o.shape[1]

    def body(r, v):
        dos, dzs, dg = [], [], jnp.zeros((1, DN_HEAD_DIM), F32)
        for ov, zv, dv in zip(_head_tiles(r[0]), _head_tiles(r[1]), _head_tiles(r[2])):
            rs = lax.rsqrt(jnp.mean(ov * ov, axis=-1, keepdims=True) + EPS)
            nv = ov * rs
            s = _silu(zv)
            dn = dv * v[0] * s
            dos.append(rs * (dn - nv * jnp.mean(dn * nv, axis=-1, keepdims=True)))
            dzs.append(dv * nv * v[0] * _dsilu(zv))
            dg = dg + _colsum(dv * nv * s)
        return [jnp.concatenate(dos, axis=1), jnp.concatenate(dzs, axis=1)], [dg]
    return _rowwise(body, [o, (proj, d, 3), don], [g], [(d, F32), (d, BF16)], [(1, DN_HEAD_DIM)], name=name)


def _chunk_tri(tr, upper):
    row = lax.broadcasted_iota(jnp.int32, (tr, tr), 0)
    col = lax.broadcasted_iota(jnp.int32, (tr, tr), 1)
    same = (row // DN_CHUNK) == (col // DN_CHUNK)
    return (same & ((row <= col) if upper else (row >= col))).astype(F32)


def _dn_gate_fwd(proj, colblk, a_log, dt_bias, name):
    def body(r, v):
        ab = r[0]
        a, b = ab[:, :LANES], ab[:, LANES:]
        g = -jnp.exp(v[0]) * _softplus(a + v[1])
        big_g = jnp.dot(_chunk_tri(g.shape[0], False), g, precision=HI, preferred_element_type=F32)
        return [big_g, _sigmoid(b)], []
    return _rowwise(body, [(proj, 2 * LANES, colblk)], [a_log, dt_bias], [(LANES, F32), (LANES, F32)], [], name=name,
                    tr=DN_GATE_ROWS)


def _dn_gate_bwd(proj, colblk, d_big_g, dbeta, a_log, dt_bias, name):
    def body(r, v):
        ab, dgc, dbt = r
        a, b = ab[:, :LANES], ab[:, LANES:]
        dg = jnp.dot(_chunk_tri(dgc.shape[0], True), dgc, precision=HI, preferred_element_type=F32)
        na = -jnp.exp(v[0])
        pre = a + v[1]
        da = dg * na * _sigmoid(pre)
        beta = _sigmoid(b)
        db = dbt * beta * (1.0 - beta)
        return [jnp.concatenate([da, db], axis=1)], [_colsum(dg * na * _softplus(pre)), _colsum(da)]
    return _rowwise(body, [(proj, 2 * LANES, colblk), d_big_g, dbeta], [a_log, dt_bias],
                    [(2 * LANES, BF16)], [(1, LANES), (1, LANES)], name=name, tr=DN_GATE_ROWS)


def _shift_rows(x, s):
    if s == 0:
        return x
    t = x.shape[0]
    row = lax.broadcasted_iota(jnp.int32, x.shape, 0)
    rolled = pltpu.roll(x, s % t, axis=0)
    return jnp.where((row >= s) if s > 0 else (row < t + s), rolled, 0.0)


def _rnd(x):
    return x.astype(BF16).astype(F32)


def _conv(x, c_ref):
    c = x * _rnd(c_ref[DN_CONV - 1:DN_CONV, :])
    for s in range(1, DN_CONV):
        c = c + _shift_rows(x, s) * _rnd(c_ref[DN_CONV - 1 - s:DN_CONV - s, :])
    return c


def _dn_prep_fwd(proj, conv_w, heads, name):
    t = proj.shape[0]
    d = heads * DN_HEAD_DIM

    def body(xq, xk, xv, cq, ck, cv, q_ref, k_ref, v_ref):
        for x_ref, c_ref, o_ref, norm in ((xq, cq, q_ref, True), (xk, ck, k_ref, True), (xv, cv, v_ref, False)):
            y = _silu(_conv(_rnd(x_ref[...]), c_ref))
            if norm:
                y = y * lax.rsqrt(jnp.sum(y * y, axis=-1, keepdims=True) + EPS)
            o_ref[...] = y

    xs = [pl.BlockSpec((t, DN_HEAD_DIM), functools.partial(lambda h, o: (0, h + o), o=o * heads)) for o in range(3)]
    cs = [pl.BlockSpec((DN_CONV, DN_HEAD_DIM), functools.partial(lambda h, o: (0, h + o), o=o * heads)) for o in range(3)]
    return pl.pallas_call(
        body, grid=(heads,), in_specs=xs + cs,
        out_specs=[pl.BlockSpec((t, DN_HEAD_DIM), lambda h: (0, h))] * 3,
        out_shape=[jax.ShapeDtypeStruct((t, d), F32)] * 3, name=name,
        compiler_params=_cparams(("parallel",)))(proj, proj, proj, conv_w, conv_w, conv_w)


def _dn_prep_bwd(proj, conv_w, dq, dk, dv, heads, name):
    t = proj.shape[0]
    d = heads * DN_HEAD_DIM

    def body(xq, xk, xv, cq, ck, cv, gq, gk, gv, oq, ok, ov, wq, wk, wv):
        for x_ref, c_ref, g_ref, o_ref, w_ref, norm in ((xq, cq, gq, oq, wq, True), (xk, ck, gk, ok, wk, True),
                                                        (xv, cv, gv, ov, wv, False)):
            x, dy = _rnd(x_ref[...]), g_ref[...]
            c = _conv(x, c_ref)
            if norm:
                y = _silu(c)
                rs = lax.rsqrt(jnp.sum(y * y, axis=-1, keepdims=True) + EPS)
                yn = y * rs
                dy = rs * (dy - yn * jnp.sum(dy * yn, axis=-1, keepdims=True))
            dc = _rnd(dy * _dsilu(c))
            dx = dc * _rnd(c_ref[DN_CONV - 1:DN_CONV, :])
            w_ref[DN_CONV - 1:DN_CONV, :] = _colsum(dc * x)
            for s in range(1, DN_CONV):
                dx = dx + _shift_rows(dc, -s) * _rnd(c_ref[DN_CONV - 1 - s:DN_CONV - s, :])
                w_ref[DN_CONV - 1 - s:DN_CONV - s, :] = _colsum(dc * _shift_rows(x, s))
            o_ref[...] = dx.astype(o_ref.dtype)

    xs = [pl.BlockSpec((t, DN_HEAD_DIM), functools.partial(lambda h, o: (0, h + o), o=o * heads)) for o in range(3)]
    cs = [pl.BlockSpec((DN_CONV, DN_HEAD_DIM), functools.partial(lambda h, o: (0, h + o), o=o * heads)) for o in range(3)]
    hs = pl.BlockSpec((t, DN_HEAD_DIM), lambda h: (0, h))
    ws = pl.BlockSpec((DN_CONV, DN_HEAD_DIM), lambda h: (0, h))
    return pl.pallas_call(
        body, grid=(heads,), in_specs=xs + cs + [hs] * 3, out_specs=[hs] * 3 + [ws] * 3,
        out_shape=[jax.ShapeDtypeStruct((t, d), BF16)] * 3 + [jax.ShapeDtypeStruct((DN_CONV, d), F32)] * 3, name=name,
        compiler_params=_cparams(("parallel",)))(proj, proj, proj, conv_w, conv_w, conv_w, dq, dk, dv)


X3 = "three bf16 passes"


def _bdot(a, b, dims, prec):
    if prec != X3:
        return lax.dot_general(a, b, dims, precision=prec, preferred_element_type=F32)
    a1, b1 = a.astype(BF16), b.astype(BF16)
    a2, b2 = (a - a1.astype(F32)).astype(BF16), (b - b1.astype(F32)).astype(BF16)
    dot = lambda x, y: lax.dot_general(x, y, dims, preferred_element_type=F32)
    return dot(a1, b1) + (dot(a1, b2) + dot(a2, b1))


def _bmm(a, b, prec=None):
    return _bdot(a, b, (((2,), (1,)), ((0,), (0,))), prec)


def _bmm_nt(a, b, prec=None):
    return _bdot(a, b, (((2,), (2,)), ((0,), (0,))), prec)


def _bmm_tn(a, b, prec=None):
    return _bdot(a, b, (((1,), (1,)), ((0,), (0,))), prec)


def _dn_local(qg, kg, vg, gc, gr, bt):
    c = qg.shape[1]
    row = lax.broadcasted_iota(jnp.int32, (c, c), 0)
    col = lax.broadcasted_iota(jnp.int32, (c, c), 1)
    incl, strict = (row >= col)[None], (row > col)[None]
    decay = jnp.where(incl, jnp.exp(jnp.where(incl, gc - gr, 0.0)), 0.0)
    kb = kg * bt
    vb = vg * bt
    m = _bmm_nt(kb.astype(BF16), kg.astype(BF16))
    a = jnp.where(strict, m * decay, 0.0)
    bp = -a
    tm = jnp.where((row == col)[None], 1.0, 0.0) + bp
    steps = max(1, (c - 1).bit_length()) - 1
    for _ in range(steps):
        bp = _bmm(bp, bp, X3)
        tm = tm + _bmm(tm, bp, X3)
    eg = jnp.exp(gc)
    glast = gc[:, c - 1:c, :]
    erel = jnp.exp(glast - gc)
    kbg = kb * eg
    qm = _bmm_nt(qg.astype(BF16), kg.astype(BF16))
    return dict(decay=decay, strict=strict, kb=kb, vb=vb, m=m, tm=tm, eg=eg, erel=erel, kbg=kbg, qm=qm,
                u=_bmm(tm, vb, X3), w=_bmm(tm, kbg, X3), qd=qg * eg, kt=kg * erel, gl=jnp.exp(glast))


def _dot(a, b):
    return jnp.dot(a.astype(BF16), b.astype(BF16), preferred_element_type=F32)


def _dot_nt(a, b):
    return lax.dot_general(a.astype(BF16), b.astype(BF16), (((1,), (1,)), ((), ())), preferred_element_type=F32)


def _dot_tn(a, b):
    return lax.dot_general(a.astype(BF16), b.astype(BF16), (((0,), (0,)), ((), ())), preferred_element_type=F32)


def _dn_chunk_specs(t, heads):
    n = t // DN_CHUNK
    hs = pl.BlockSpec((t, DN_HEAD_DIM), lambda h: (0, h))
    gcs = pl.BlockSpec((1, n, DN_CHUNK, 1), lambda h: (h, 0, 0, 0))
    grs = pl.BlockSpec((1, n, 1, DN_CHUNK), lambda h: (h, 0, 0, 0))
    ss = pl.BlockSpec((1, n, DN_HEAD_DIM, DN_HEAD_DIM), lambda h: (h, 0, 0, 0))
    return n, hs, gcs, grs, ss


def _dn_chunk_fwd(q, k, v, gc, gr, bc, name):
    t, d = q.shape
    heads = d // DN_HEAD_DIM
    c, dk = DN_CHUNK, DN_HEAD_DIM
    n, hs, gcs, grs, ss = _dn_chunk_specs(t, heads)
    nb = min(DN_GROUP, n)
    scale = dk ** -0.5

    def body(q_ref, k_ref, v_ref, gc_ref, gr_ref, b_ref, o_ref, s_ref, u_s, w_s, at_s, qd_s, kt_s, gl_s):
        def group(gi, carry):
            r0 = pl.multiple_of(gi * (nb * c), nb * c)
            n0 = gi * nb
            ld = lambda ref: ref[pl.ds(r0, nb * c), :].reshape(nb, c, dk)
            loc = _dn_local(ld(q_ref) * scale, ld(k_ref), ld(v_ref), gc_ref[0, pl.ds(n0, nb)],
                            gr_ref[0, pl.ds(n0, nb)], b_ref[0, pl.ds(n0, nb)])
            u_s[pl.ds(n0, nb)] = loc["u"]
            w_s[pl.ds(n0, nb)] = loc["w"]
            at_s[pl.ds(n0, nb)] = loc["qm"] * loc["decay"]
            qd_s[pl.ds(n0, nb)] = loc["qd"]
            kt_s[pl.ds(n0, nb)] = loc["kt"]
            gl_s[pl.ds(n0, nb)] = loc["gl"]
            return carry
        lax.fori_loop(0, n // nb, group, 0)

        def chunk(i, s):
            s_ref[0, i] = s
            vnew = u_s[i] - _dot(w_s[i], s)
            o = _dot(qd_s[i], s) + _dot(at_s[i], vnew)
            o_ref[pl.ds(pl.multiple_of(i * c, c), c), :] = o
            return s * gl_s[i] + _dot_tn(kt_s[i], vnew)
        lax.fori_loop(0, n, chunk, jnp.zeros((dk, dk), F32))

    scratch = [pltpu.VMEM((n, c, dk), F32), pltpu.VMEM((n, c, dk), F32), pltpu.VMEM((n, c, c), F32),
               pltpu.VMEM((n, c, dk), F32), pltpu.VMEM((n, c, dk), F32), pltpu.VMEM((n, 1, 1), F32)]
    return pl.pallas_call(
        body, grid=(heads,), in_specs=[hs, hs, hs, gcs, grs, gcs], out_specs=[hs, ss],
        out_shape=[jax.ShapeDtypeStruct((t, d), F32), jax.ShapeDtypeStruct((heads, n, dk, dk), F32)],
        scratch_shapes=scratch, name=name, compiler_params=_cparams(("parallel",)))(q, k, v, gc, gr, bc)


def _dn_chunk_bwd(q, k, v, gc, gr, bc, s_all, do, name):
    t, d = q.shape
    heads = d // DN_HEAD_DIM
    c, dk = DN_CHUNK, DN_HEAD_DIM
    n, hs, gcs, grs, ss = _dn_chunk_specs(t, heads)
    nb = min(DN_GROUP, n)
    scale = dk ** -0.5

    def body(q_ref, k_ref, v_ref, gc_ref, gr_ref, b_ref, s_ref, do_ref,
             dq_ref, dk_ref, dv_ref, dgc_ref, dgr_ref, db_ref,
             u_s, w_s, att_s, qd_s, kt_s, gl_s, du_s, dw_s, dat_s, dqd_s, dkt_s, dgl_s):
        def load_group(gi):
            r0 = pl.multiple_of(gi * (nb * c), nb * c)
            n0 = gi * nb
            ld = lambda ref: ref[pl.ds(r0, nb * c), :].reshape(nb, c, dk)
            qg, kg, vg = ld(q_ref) * scale, ld(k_ref), ld(v_ref)
            gcv, grv, bt = gc_ref[0, pl.ds(n0, nb)], gr_ref[0, pl.ds(n0, nb)], b_ref[0, pl.ds(n0, nb)]
            return r0, n0, qg, kg, vg, gcv, grv, bt, _dn_local(qg, kg, vg, gcv, grv, bt)

        def group_a(gi, carry):
            _, n0, qg, kg, _, gcv, grv, _, loc = load_group(gi)
            row = lax.broadcasted_iota(jnp.int32, (c, c), 0)
            col = lax.broadcasted_iota(jnp.int32, (c, c), 1)
            upper = (col >= row)[None]
            decay_t = jnp.where(upper, jnp.exp(jnp.where(upper, grv - gcv, 0.0)), 0.0)
            u_s[pl.ds(n0, nb)] = loc["u"]
            w_s[pl.ds(n0, nb)] = loc["w"]
            att_s[pl.ds(n0, nb)] = _bmm_nt(kg.astype(BF16), qg.astype(BF16)) * decay_t
            qd_s[pl.ds(n0, nb)] = loc["qd"]
            kt_s[pl.ds(n0, nb)] = loc["kt"]
            gl_s[pl.ds(n0, nb)] = loc["gl"]
            return carry
        lax.fori_loop(0, n // nb, group_a, 0)

        def chunk_b(it, ds_next):
            i = n - 1 - it
            s = s_ref[0, i]
            dov = do_ref[pl.ds(pl.multiple_of(i * c, c), c), :]
            w, kt = w_s[i], kt_s[i]
            vnew = u_s[i] - _dot(w, s)
            dvnew = _dot(att_s[i], dov) + _dot(kt, ds_next)
            du_s[i] = dvnew
            dw_s[i] = -_dot_nt(dvnew, s)
            dat_s[i] = _dot_nt(dov, vnew)
            dqd_s[i] = _dot_nt(dov, s)
            dkt_s[i] = _dot_nt(vnew, ds_next)
            dgl_s[i] = jnp.sum(jnp.sum(ds_next * s, axis=1, keepdims=True), axis=0, keepdims=True)
            return ds_next * gl_s[i] + _dot_tn(qd_s[i], dov) - _dot_tn(w, dvnew)
        lax.fori_loop(0, n, chunk_b, jnp.zeros((dk, dk), F32))

        def group_c(gi, carry):
            r0, n0, qg, kg, vg, gcv, grv, bt, loc = load_group(gi)
            sl = pl.ds(n0, nb)
            du, dw, dat, dqd, dkt, dgl = du_s[sl], dw_s[sl], dat_s[sl], dqd_s[sl], dkt_s[sl], dgl_s[sl]
            tm, decay, kb, kbg = loc["tm"], loc["decay"], loc["kb"], loc["kbg"]
            dvb = _bmm_tn(tm, du, X3)
            dkbg = _bmm_tn(tm, dw, X3)
            dt = _bmm_nt(du, loc["vb"], X3) + _bmm_nt(dw, kbg, X3)
            da = jnp.where(loc["strict"], -_bmm_tn(tm, _bmm_nt(dt, tm, X3), X3), 0.0)
            dms = (da * decay).astype(BF16)
            dqs = (dat * decay).astype(BF16)
            kgb = kg.astype(BF16)
            dkb = _bmm(dms, kgb) + dkbg * loc["eg"]
            dqt = _bmm(dqs, kgb) + dqd * loc["eg"]
            dkk = _bmm_tn(dms, kb.astype(BF16)) + _bmm_tn(dqs, qg.astype(BF16)) + dkt * loc["erel"] + dkb * bt
            e = (da * loc["m"] + dat * loc["qm"]) * decay
            lsum = lambda x: jnp.sum(x, axis=2, keepdims=True)
            dkt_kt = lsum(dkt * loc["kt"])
            dgcv = lsum(e) + lsum(dqd * loc["qd"]) - dkt_kt + lsum(dkbg * kbg)
            dglast = jnp.sum(dkt_kt, axis=1, keepdims=True) + dgl * loc["gl"]
            rowc = lax.broadcasted_iota(jnp.int32, (1, c, 1), 1)
            dgc_ref[0, sl] = dgcv + jnp.where(rowc == c - 1, dglast, 0.0)
            dgr_ref[0, sl] = -jnp.sum(e, axis=1, keepdims=True)
            db_ref[0, sl] = lsum(dkb * kg) + lsum(dvb * vg)
            rows = pl.ds(r0, nb * c)
            dq_ref[rows, :] = (dqt * scale).reshape(nb * c, dk)
            dk_ref[rows, :] = dkk.reshape(nb * c, dk)
            dv_ref[rows, :] = (dvb * bt).reshape(nb * c, dk)
            return carry
        lax.fori_loop(0, n // nb, group_c, 0)

    big = lambda: pltpu.VMEM((n, c, dk), F32)
    sq = lambda: pltpu.VMEM((n, c, c), F32)
    one = lambda: pltpu.VMEM((n, 1, 1), F32)
    scratch = [big(), big(), sq(), big(), big(), one(), big(), big(), sq(), big(), big(), one()]
    return pl.pallas_call(
        body, grid=(heads,), in_specs=[hs, hs, hs, gcs, grs, gcs, ss, hs], out_specs=[hs, hs, hs, gcs, grs, gcs],
        out_shape=[jax.ShapeDtypeStruct((t, d), F32)] * 3 + [
            jax.ShapeDtypeStruct((heads, n, c, 1), F32), jax.ShapeDtypeStruct((heads, n, 1, c), F32),
            jax.ShapeDtypeStruct((heads, n, c, 1), F32)],
        scratch_shapes=scratch, name=name, compiler_params=_cparams(("parallel",)))(q, k, v, gc, gr, bc, s_all, do)


def _dot01(x, m01):
    x1 = x.astype(BF16)
    r1 = x - x1.astype(F32)
    x2 = r1.astype(BF16)
    x3 = (r1 - x2.astype(F32)).astype(BF16)
    dot = lambda a: jnp.dot(a, m01, preferred_element_type=F32)
    return (dot(x1) + dot(x2)) + dot(x3)


def _sb_tile(z, mask, r):
    rows, bk = z.shape
    row = lax.broadcasted_iota(jnp.int32, (bk, bk), 0)
    col = lax.broadcasted_iota(jnp.int32, (bk, bk), 1)
    later = (row > col).astype(BF16)
    parts = [slice(p, p + SB_PART) for p in range(0, rows, SB_PART)]
    lm, ls, cs = [], [], []
    for p in parts:
        lsm = -_softplus(z[p])
        lm.append(lsm if mask is None else jnp.where(mask[p], lsm, 0.0))
        ls.append(z[p] + lsm)
        cs.append(_dot01(lm[-1], later))
    a = []
    for i, p in enumerate(parts):
        ap = jnp.exp(ls[i] + cs[i] + r[p])
        a.append(ap if mask is None else jnp.where(mask[p], ap, 0.0))
    cat = lambda xs: jnp.concatenate(xs, axis=0)
    return cat(a), jnp.exp(cat(ls)), r + jnp.sum(cat(lm), axis=1, keepdims=True)


def _sb_sweep(bq, bk, qi, fetch, visit, carry):
    nd = bq // bk
    row = lax.broadcasted_iota(jnp.int32, (2 * bq, bk), 0)
    row = jnp.where(row >= bq, row - bq, row)
    col = lax.broadcasted_iota(jnp.int32, (2 * bq, bk), 1)
    ahead = lambda j: fetch(jnp.maximum(j - 1, 0))
    pre = fetch(qi * nd + nd - 1)
    for dd in reversed(range(nd)):
        nxt = ahead(qi * nd + dd)
        carry = visit(qi * nd + dd, (col + dd * bk) < row, carry, pre)
        pre = nxt

    def below(it, c):
        j = qi * nd - 1 - it
        nxt = ahead(j)
        return visit(j, None, c[0], c[1]), nxt
    return lax.fori_loop(0, qi * nd, below, (carry, pre))[0]


def _stack_heads(a, h0):
    return jnp.concatenate([jnp.where(h0, a, 0.0), jnp.where(h0, 0.0, a)], axis=0).astype(BF16)


def _side_by_side(a, bq):
    return jnp.concatenate([a[:bq], a[bq:]], axis=1)


def _sb_specs(t, d, bq, vcol):
    qs = pl.BlockSpec((bq, LANES), lambda g, i: (i, g))
    ks = pl.BlockSpec((t, LANES), lambda g, i: (0, g))
    vs = pl.BlockSpec((t, LANES), lambda g, i: (0, g + vcol))
    return qs, ks, vs


def _sb_fwd(qn, kn, qkv, name):
    t, d = qn.shape
    bq, bk = min(SB_QBLOCK, t), min(SB_BLOCK, t)
    scale = SB_HEAD_DIM ** -0.5
    nt = (((1,), (1,)), ((), ()))

    def body(q_ref, k_ref, v_ref, o_ref):
        qi = pl.program_id(1)
        h0 = lax.broadcasted_iota(jnp.int32, (1, LANES), 1) < SB_HEAD_DIM
        q2 = _stack_heads(q_ref[...] * scale, h0)

        tile_rows = lambda j: pl.ds(pl.multiple_of(j * bk, bk), bk)

        def fetch(j):
            return lax.dot_general(q2, k_ref[tile_rows(j), :].astype(BF16), nt, preferred_element_type=F32)

        def visit(j, mask, carry, z):
            acc, r = carry
            a, _, r = _sb_tile(z, mask, r)
            a2 = _side_by_side(a.astype(BF16), bq)
            return acc + jnp.dot(a2, _stack_heads(v_ref[tile_rows(j), :], h0), preferred_element_type=F32), r
        acc, _ = _sb_sweep(bq, bk, qi, fetch, visit, (jnp.zeros((bq, LANES), F32), jnp.zeros((2 * bq, 1), F32)))
        o_ref[...] = acc.astype(o_ref.dtype)

    qs, ks, vs = _sb_specs(t, d, bq, 2 * d // LANES)
    return pl.pallas_call(
        body, grid=(d // LANES, t // bq), in_specs=[qs, ks, vs], out_specs=qs,
        out_shape=jax.ShapeDtypeStruct((t, d), BF16), name=name,
        compiler_params=_cparams(("parallel", "parallel")))(qn, kn, qkv)


def _sb_bwd(qn, kn, qkv, do, name):
    t, d = qn.shape
    bq, bk = min(SB_QBLOCK, t), min(SB_BLOCK, t)
    scale = SB_HEAD_DIM ** -0.5
    nt = (((1,), (1,)), ((), ()))

    def body(q_ref, k_ref, v_ref, do_ref, dq_ref, dk_ref, dv_ref, p_s, sg_s):
        qi = pl.program_id(1)

        @pl.when(qi == 0)
        def _():
            dk_ref[...] = jnp.zeros(dk_ref.shape, F32)
            dv_ref[...] = jnp.zeros(dv_ref.shape, F32)

        h0 = lax.broadcasted_iota(jnp.int32, (1, LANES), 1) < SB_HEAD_DIM
        q2 = _stack_heads(q_ref[...] * scale, h0)
        do2 = _stack_heads(do_ref[...], h0)
        row = lax.broadcasted_iota(jnp.int32, (bk, bk), 0)
        col = lax.broadcasted_iota(jnp.int32, (bk, bk), 1)
        later_incl = (row >= col).astype(BF16)
        zero = jnp.zeros((2 * bq, 1), F32)
        tn = (((0,), (0,)), ((), ()))

        tile_rows = lambda j: pl.ds(pl.multiple_of(j * bk, bk), bk)

        def fetch1(j):
            rows = tile_rows(j)
            return (lax.dot_general(q2, k_ref[rows, :].astype(BF16), nt, preferred_element_type=F32),
                    lax.dot_general(do2, v_ref[rows, :].astype(BF16), nt, preferred_element_type=F32))

        def visit1(j, mask, carry, z_da):
            r, sp = carry
            a, sg, r = _sb_tile(z_da[0], mask, r)
            p = a * z_da[1]
            p_s[j] = p
            sg_s[j] = sg
            dv_ref[tile_rows(j), :] += lax.dot_general(a.astype(BF16), do2, tn, preferred_element_type=F32)
            return r, sp + jnp.sum(p, axis=1, keepdims=True)
        _, total = _sb_sweep(bq, bk, qi, fetch1, visit1, (zero, zero))

        def fetch2(j):
            return _dot01(p_s[j], later_incl)

        def visit2(j, mask, carry, later_p):
            dq, sp = carry
            rows = tile_rows(j)
            p, sg = p_s[j], sg_s[j]
            pref = total - sp - later_p
            dz = p * (1.0 - sg) - pref * sg
            if mask is not None:
                dz = jnp.where(mask, dz, 0.0)
            dz = dz.astype(BF16)
            dk_ref[rows, :] += lax.dot_general(dz, q2, tn, preferred_element_type=F32)
            dq = dq + jnp.dot(_side_by_side(dz, bq), _stack_heads(k_ref[rows, :], h0), preferred_element_type=F32)
            return dq, sp + jnp.sum(p, axis=1, keepdims=True)
        dq, _ = _sb_sweep(bq, bk, qi, fetch2, visit2, (jnp.zeros((bq, LANES), F32), zero))
        dq_ref[...] = dq * scale

    qs, ks, vs = _sb_specs(t, d, bq, 2 * d // LANES)
    shp = jax.ShapeDtypeStruct((t, d), F32)
    scratch = [pltpu.VMEM((t // bk, 2 * bq, bk), F32), pltpu.VMEM((t // bk, 2 * bq, bk), F32)]
    return pl.pallas_call(
        body, grid=(d // LANES, t // bq), in_specs=[qs, ks, vs, qs], out_specs=[qs, ks, ks],
        out_shape=[shp, shp, shp], scratch_shapes=scratch, name=name,
        compiler_params=_cparams(("parallel", "arbitrary")))(qn, kn, qkv, do)


def _adamw(w, g, m, v, name):
    shape = w.shape
    two_d = lambda a: a.reshape(-1, shape[-1])
    c1 = 1.0 - ADAM_B1 ** ADAM_STEP
    c2 = 1.0 - ADAM_B2 ** ADAM_STEP

    def body(r, _):
        wv, gv, mv, vv = r
        mn = ADAM_B1 * mv + (1.0 - ADAM_B1) * gv
        vn = ADAM_B2 * vv + (1.0 - ADAM_B2) * (gv * gv)
        delta = -ADAM_LR * ((mn / c1) / (jnp.sqrt(vn / c2) + ADAM_EPS) + ADAM_WD * wv)
        return [delta, mn, vn], []
    width = shape[-1]
    res = _rowwise(body, [two_d(w), two_d(g), two_d(m), two_d(v)], [], [(width, F32)] * 3, [], name=name)
    return [r.reshape(shape) for r in res]


def _dn_layer_fwd(h1, w, conv_w, a_log, dt_bias, onorm_g, tag):
    t, d = h1.shape
    heads = d // DN_HEAD_DIM
    n = t // DN_CHUNK
    proj = _mm(h1, w["all"], name=tag + "_proj")
    qn, kn, vv = _dn_prep_fwd(proj, conv_w, heads, tag + "_prep")
    big_g, beta = _dn_gate_fwd(proj, 4 * d // (2 * LANES), a_log, dt_bias, tag + "_gate")
    gt_ = big_g[:, :heads].T.reshape(heads, n, DN_CHUNK)
    gc, gr = gt_[..., None], gt_[:, :, None, :]
    bc = beta[:, :heads].T.reshape(heads, n, DN_CHUNK)[..., None]
    o, s_all = _dn_chunk_fwd(qn, kn, vv, gc, gr, bc, tag + "_chunk")
    on = _dn_post_fwd(o, proj, onorm_g, tag + "_post")
    y = _mm(on, w["out"], name=tag + "_out")
    return y, dict(proj=proj, qn=qn, kn=kn, v=vv, gc=gc, gr=gr, bc=bc, o=o, s_all=s_all, on=on)


def _dn_layer_bwd(dy, h1, w, conv_w, a_log, dt_bias, onorm_g, sv, tag):
    t, d = h1.shape
    heads = d // DN_HEAD_DIM
    don = _mm(dy, w["out"], tb=True, name=tag + "_dout")
    g_out = _mm(sv["on"], dy, ta=True, out_dtype=BF16, name=tag + "_gwout")
    do, dz, g_on = _dn_post_bwd(sv["o"], sv["proj"], don, onorm_g, tag + "_dpost")
    dq, dk, dv, dgc, dgr, dbc = _dn_chunk_bwd(sv["qn"], sv["kn"], sv["v"], sv["gc"], sv["gr"], sv["bc"], sv["s_all"],
                                              do, tag + "_dchunk")
    pad = lambda a: jnp.pad(a.reshape(heads, t).T, ((0, 0), (0, LANES - heads)))
    d_big_g = pad(dgc) + pad(dgr)
    dab, g_alog, g_dt = _dn_gate_bwd(sv["proj"], 4 * d // (2 * LANES), d_big_g, pad(dbc), a_log, dt_bias, tag + "_dgate")
    dxq, dxk, dxv, wq, wk, wv = _dn_prep_bwd(sv["proj"], conv_w, dq, dk, dv, heads, tag + "_dprep")
    dproj = jnp.concatenate([dxq, dxk, dxv, dz, dab], axis=1)
    dh1 = _mm(dproj, w["all"], tb=True, name=tag + "_dh")
    g_all = _mm(h1, dproj, ta=True, out_dtype=BF16, name=tag + "_gwin")
    grads = dict(w_all=g_all, w_out=g_out, conv_w=jnp.concatenate([wq, wk, wv], axis=1), a_log=g_alog, dt_bias=g_dt,
                 onorm_g=g_on)
    return dh1, grads


def _sb_layer_fwd(h1, w, q_g, k_g, tag):
    t, d = h1.shape
    heads = d // SB_HEAD_DIM
    qkv = _mm(h1, w["qkv"], name=tag + "_proj")
    gq, gk = jnp.tile(q_g, (1, heads)), jnp.tile(k_g, (1, heads))
    qn, kn = _sb_norm_fwd(qkv, gq, gk, tag + "_norm")
    o = _sb_fwd(qn, kn, qkv, tag + "_attn")
    y = _mm(o, w["out"], name=tag + "_out")
    return y, dict(qkv=qkv, qn=qn, kn=kn, o=o, gq=gq, gk=gk)


def _sb_layer_bwd(dy, h1, w, q_g, k_g, sv, tag):
    t, d = h1.shape
    heads = d // SB_HEAD_DIM
    do = _mm(dy, w["out"], tb=True, name=tag + "_dout")
    g_out = _mm(sv["o"], dy, ta=True, out_dtype=BF16, name=tag + "_gwout")
    dqn, dkn, dv = _sb_bwd(sv["qn"], sv["kn"], sv["qkv"], do, tag + "_dattn")
    dqkv, g_q, g_k = _sb_norm_bwd(sv["qkv"], dqn, dkn, dv, sv["gq"], sv["gk"], tag + "_dnorm")
    fold = lambda g: jnp.sum(g.reshape(heads, SB_HEAD_DIM), axis=0, keepdims=True)
    dh1 = _mm(dqkv, w["qkv"], tb=True, name=tag + "_dh")
    g_qkv = _mm(h1, dqkv, ta=True, out_dtype=BF16, name=tag + "_gwin")
    return dh1, dict(w_qkv=g_qkv, w_out=g_out, q_norm_g=fold(g_q), k_norm_g=fold(g_k))


def _local_step(x, tgt, mod, norm1_g, norm2_g, layer_weights, layer_grads):
    depth = mod.shape[0]
    d = x.shape[1]
    saved = []
    for i in range(depth):
        mix_w, ffn_w = layer_weights(i, x)
        mv = [mod[i:i + 1, j * d:(j + 1) * d] for j in range(N_MOD)]
        sh1, sc1, gt1, sh2, sc2, gt2 = mv
        tag = "l%d" % i
        h1 = _adaln_fwd(x, norm1_g[i:i + 1], sc1, sh1, tag + "_ln1")
        if i % 2 == 0:
            p = mix_w
            y, sv = _dn_layer_fwd(h1, p, p["conv_w"], p["a_log"], p["dt_bias"], p["onorm_g"], tag + "_dn")
        else:
            p = mix_w
            y, sv = _sb_layer_fwd(h1, p, p["q_g"], p["k_g"], tag + "_sb")
        x1 = _resid_fwd(x, y, gt1, tag + "_res1")
        h2 = _adaln_fwd(x1, norm2_g[i:i + 1], sc2, sh2, tag + "_ln2")
        u = _mm(h2, ffn_w["w_in"], name=tag + "_ffn_in")
        a = _swiglu_fwd(u, tag + "_swiglu")
        y2 = _mm(a, ffn_w["w_out"], name=tag + "_ffn_out")
        x2 = _resid_fwd(x1, y2, gt2, tag + "_res2")
        saved.append(dict(x0=x, h1=h1, y=y, mix=sv, x1=x1, h2=h2, u=u, a=a, y2=y2, mix_w=mix_w, ffn_w=ffn_w))
        x = x2

    loss, dx = _loss_fwd_bwd(x, tgt, "loss")

    dmod, dn1, dn2 = [None] * depth, [None] * depth, [None] * depth
    zero = jnp.zeros((), F32)
    for i in reversed(range(depth)):
        s = saved[i]
        mv = [mod[i:i + 1, j * d:(j + 1) * d] + zero for j in range(N_MOD)]
        sh1, sc1, gt1, sh2, sc2, gt2 = mv
        tag = "l%d" % i
        ffn_w, p = s["ffn_w"], s["mix_w"]
        dy2, dgt2 = _resid_bwd(dx, s["y2"], gt2, tag + "_dres2")
        da = _mm(dy2, ffn_w["w_out"], tb=True, name=tag + "_dffn_a")
        g_wout = _mm(s["a"], dy2, ta=True, out_dtype=BF16, name=tag + "_gffn_out")
        du = _swiglu_bwd(s["u"], da, tag + "_dswiglu")
        dh2 = _mm(du, ffn_w["w_in"], tb=True, name=tag + "_dffn_h")
        g_win = _mm(s["h2"], du, ta=True, out_dtype=BF16, name=tag + "_gffn_in")
        dx, dg2, dsc2, dsh2 = _adaln_bwd(s["x1"], dh2, dx, norm2_g[i:i + 1], sc2, tag + "_dln2")
        dy, dgt1 = _resid_bwd(dx, s["y"], gt1, tag + "_dres1")
        if i % 2 == 0:
            dh1, g_mix = _dn_layer_bwd(dy, s["h1"], p, p["conv_w"], p["a_log"], p["dt_bias"], p["onorm_g"], s["mix"],
                                       tag + "_dn")
        else:
            dh1, g_mix = _sb_layer_bwd(dy, s["h1"], p, p["q_g"], p["k_g"], s["mix"], tag + "_sb")
        dx, dg1, dsc1, dsh1 = _adaln_bwd(s["x0"], dh1, dx, norm1_g[i:i + 1], sc1, tag + "_dln1")
        dmod[i] = jnp.concatenate([dsh1, dsc1, dgt1, dsh2, dsc2, dgt2], axis=1)
        dn1[i], dn2[i] = dg1, dg2
        zero = layer_grads(i, dx, g_mix, dict(w_in=g_win, w_out=g_wout))
    return loss, dx, jnp.concatenate(dmod, axis=0), jnp.concatenate(dn1, axis=0), jnp.concatenate(dn2, axis=0)


def _axes():
    return lax.axis_index("x"), lax.axis_index("y"), lax.axis_index("c")


def _remote(src, dst, send_sem, recv_sem, dev):
    return pltpu.make_async_remote_copy(src_ref=src, dst_ref=dst, send_sem=send_sem, recv_sem=recv_sem,
                                        device_id=dev, device_id_type=MESH)


def _other_chips(x, y):
    return [(1 - x, y), (x, 1 - y), (1 - x, 1 - y)]


def _allgather_small(v, name):
    m, n = v.shape

    def body(x_ref, out_ref, send_sems, recv_sems, local_sem):
        x, y, c = _axes()
        me, sibling = (x, y, c), (x, y, 1 - c)
        chips = _other_chips(x, y)

        def rows(px, py, pc):
            return out_ref.at[pl.ds((4 * px + 2 * py + pc) * m, m), :]

        def copy(k, block, to, src=None):
            return _remote(rows(*block) if src is None else src, rows(*block), send_sems.at[k], recv_sems.at[k], to)

        mine = pltpu.make_async_copy(x_ref, rows(*me), local_sem)
        mine.start()
        first = [copy(0, me, sibling, src=x_ref)]
        first += [copy(1 + j, me, (*chip, c), src=x_ref) for j, chip in enumerate(chips)]
        for cp in first:
            cp.start()
        passed = [copy(4 + j, (*chip, c), sibling) for j, chip in enumerate(chips)]
        for j, chip in enumerate(chips):
            copy(1 + j, (*chip, c), me).wait_recv()
            passed[j].start()
        copy(0, sibling, me).wait_recv()
        for j, chip in enumerate(chips):
            copy(4 + j, (*chip, 1 - c), me).wait_recv()
        for cp in first + passed:
            cp.wait_send()
        mine.wait()

    return pl.pallas_call(
        body, out_shape=jax.ShapeDtypeStruct((8 * m, n), v.dtype),
        in_specs=[pl.BlockSpec(memory_space=pltpu.VMEM)], out_specs=pl.BlockSpec(memory_space=pltpu.VMEM),
        scratch_shapes=[pltpu.SemaphoreType.DMA((7,)), pltpu.SemaphoreType.DMA((7,)), pltpu.SemaphoreType.DMA],
        name=name, compiler_params=pltpu.CompilerParams(vmem_limit_bytes=VMEM_LIMIT))(v)


def _half(ref, h, rh):
    return ref.at[(slice(None),) * (len(ref.shape) - 2) + (pl.ds(h * rh, rh), slice(None))]


def _hbm_call(body, ins, out_shapes, n_sems, n_local, name):
    hbm = pl.BlockSpec(memory_space=pltpu.HBM)
    scratch = [pltpu.SemaphoreType.DMA((n_sems,)), pltpu.SemaphoreType.DMA((n_sems,))]
    if n_local:
        scratch.append(pltpu.SemaphoreType.DMA((n_local,)))
    return pl.pallas_call(body, out_shape=out_shapes, in_specs=[hbm] * len(ins), out_specs=[hbm] * len(out_shapes),
                          scratch_shapes=scratch, name=name)(*ins)


def _allgather_weights(shards, name):
    n = len(shards)

    def body(*refs):
        p_refs, out_refs, (send_sems, recv_sems, local_sems) = refs[:n], refs[n:2 * n], refs[2 * n:]
        x, y, c = _axes()
        sibling = (x, y, 1 - c)
        chips = _other_chips(x, y)
        rh = [s.shape[1] // 2 for s in shards]
        blk = lambda i, cx, cy, h: _half(out_refs[i].at[2 * cx + cy], h, rh[i])
        mine = [pltpu.make_async_copy(p_refs[i], out_refs[i].at[2 * x + y], local_sems.at[i]) for i in range(n)]
        for cp in mine:
            cp.start()
        first = [_remote(_half(p_refs[i], c, rh[i]), blk(i, x, y, c), send_sems.at[6 * i + j], recv_sems.at[6 * i + j],
                         (*chip, c)) for i in range(n) for j, chip in enumerate(chips)]
        for cp in first:
            cp.start()
        passed = []
        for j, chip in enumerate(chips):
            for i in range(n):
                _remote(_half(p_refs[i], c, rh[i]), blk(i, *chip, c), send_sems.at[6 * i + j], recv_sems.at[6 * i + j],
                        sibling).wait_recv()
                passed.append(_remote(blk(i, *chip, c), blk(i, *chip, c), send_sems.at[6 * i + 3 + j],
                                      recv_sems.at[6 * i + 3 + j], sibling))
                passed[-1].start()
        for j, chip in enumerate(chips):
            for i in range(n):
                _remote(_half(p_refs[i], c, rh[i]), blk(i, *chip, 1 - c), send_sems.at[6 * i + 3 + j],
                        recv_sems.at[6 * i + 3 + j], sibling).wait_recv()
        for cp in first + passed:
            cp.wait_send()
        for cp in mine:
            cp.wait()

    outs = [jax.ShapeDtypeStruct((4,) + s.shape, s.dtype) for s in shards]
    return _hbm_call(body, shards, outs, 6 * n, n, name)


def _sibling_swap_halves(arrs, name):
    n = len(arrs)

    def body(*refs):
        v_refs, out_refs, (send_sems, recv_sems) = refs[:n], refs[n:2 * n], refs[2 * n:]
        x, y, c = _axes()
        cps = [_remote(_half(v_refs[i], 1 - c, arrs[i].shape[-2] // 2), out_refs[i], send_sems.at[i], recv_sems.at[i],
                       (x, y, 1 - c)) for i in range(n)]
        for cp in cps:
            cp.start()
        for cp in cps:
            cp.wait()

    outs = [jax.ShapeDtypeStruct(a.shape[:-2] + (a.shape[-2] // 2, a.shape[-1]), a.dtype) for a in arrs]
    return _hbm_call(body, arrs, outs, n, 0, name)


def _chip_scatter(arrs, name):
    n = len(arrs)

    def body(*refs):
        v_refs, out_refs, (send_sems, recv_sems, local_sems) = refs[:n], refs[n:2 * n], refs[2 * n:]
        x, y, c = _axes()
        me = 2 * x + y
        chips = _other_chips(x, y)
        mine = [pltpu.make_async_copy(v_refs[i].at[me], out_refs[i].at[me], local_sems.at[i]) for i in range(n)]
        for cp in mine:
            cp.start()
        sends = [_remote(v_refs[i].at[2 * cx + cy], out_refs[i].at[me], send_sems.at[3 * i + j], recv_sems.at[3 * i + j],
                         (cx, cy, c)) for i in range(n) for j, (cx, cy) in enumerate(chips)]
        for cp in sends:
            cp.start()
        for i in range(n):
            for j, (cx, cy) in enumerate(chips):
                _remote(v_refs[i].at[me], out_refs[i].at[2 * cx + cy], send_sems.at[3 * i + j], recv_sems.at[3 * i + j],
                        (cx, cy, c)).wait_recv()
        for cp in sends:
            cp.wait_send()
        for cp in mine:
            cp.wait()

    outs = [jax.ShapeDtypeStruct(a.shape, a.dtype) for a in arrs]
    return _hbm_call(body, arrs, outs, 3 * n, n, name)


def _ici_src(ref, mode, slot, c):
    return _half(ref, c, ref.shape[-2] // 2) if mode == "half" else ref.at[slot]


def _ici_start(srcs, mode, name):
    n, ncp = len(srcs), 3 * len(srcs)
    lands = [jnp.zeros(((4,) + s.shape[:-2] + (s.shape[-2] // 2, s.shape[-1])) if mode == "half" else s.shape, s.dtype)
             for s in srcs]

    def body(*refs):
        src_refs, land_refs = refs[:n], refs[n:2 * n]
        send_sems, recv_sems = refs[2 * n:2 * n + ncp], refs[2 * n + ncp:2 * n + 2 * ncp]
        token = refs[-1]
        x, y, c = _axes()
        for i in range(n):
            for j, (cx, cy) in enumerate(_other_chips(x, y)):
                _remote(_ici_src(src_refs[i], mode, 2 * cx + cy, c), land_refs[i].at[2 * x + y], send_sems[3 * i + j],
                        recv_sems[3 * i + j], (cx, cy, c)).start()
        token[...] = jnp.zeros(token.shape, token.dtype)

    hbm, sem = pl.BlockSpec(memory_space=pltpu.HBM), pl.BlockSpec(memory_space=pltpu.SEMAPHORE)
    bufs = srcs + lands
    out_shape = tuple([pltpu.SemaphoreType.DMA(())] * (2 * ncp) + [pltpu.HBM(b.shape, b.dtype) for b in bufs]
                      + [jax.ShapeDtypeStruct((8, LANES), F32)])
    res = pl.pallas_call(
        body, name=name, out_shape=out_shape, in_specs=(hbm,) * (2 * n),
        out_specs=(sem,) * (2 * ncp) + (hbm,) * (2 * n) + (pl.BlockSpec(memory_space=pltpu.VMEM),),
        input_output_aliases={i: 2 * ncp + i for i in range(2 * n)},
        compiler_params=pltpu.CompilerParams(has_side_effects=pltpu.SideEffectType.DATAFLOW_SIDE_EFFECTING),
    )(*[pltpu.with_memory_space_constraint(b, pltpu.HBM) for b in bufs])
    return list(res[:2 * ncp]), list(res[2 * ncp:2 * ncp + 2 * n]), res[-1][0, 0]


def _ici_wait(sems, bufs, mode, after, name):
    n, ncp = len(bufs) // 2, len(sems) // 2

    def body(*refs):
        src_refs, land_refs = refs[:n], refs[n:2 * n]
        send_sems, recv_sems = refs[2 * n:2 * n + ncp], refs[2 * n + ncp:2 * n + 2 * ncp]
        x, y, c = _axes()
        for i in range(n):
            for j, (cx, cy) in enumerate(_other_chips(x, y)):
                cp = _remote(_ici_src(src_refs[i], mode, 2 * cx + cy, c), land_refs[i].at[2 * cx + cy],
                             send_sems[3 * i + j], recv_sems[3 * i + j], (cx, cy, c))
                cp.wait_send()
                cp.wait_recv()

    hbm, sem = pl.BlockSpec(memory_space=pltpu.HBM), pl.BlockSpec(memory_space=pltpu.SEMAPHORE)
    res = pl.pallas_call(
        body, name=name, out_shape=tuple(pltpu.HBM(b.shape, b.dtype) for b in bufs),
        in_specs=(hbm,) * (2 * n) + (sem,) * (2 * ncp) + (pl.BlockSpec(memory_space=pl.ANY),),
        out_specs=(hbm,) * (2 * n), input_output_aliases={i: i for i in range(2 * n)},
        compiler_params=pltpu.CompilerParams(has_side_effects=pltpu.SideEffectType.DATAFLOW_SIDE_EFFECTING),
    )(*bufs, *sems, after)
    return list(res[n:])


def _share_halves(arrs, name):
    n = len(arrs)

    def body(*refs):
        v_refs, out_refs, (send_sems, recv_sems) = refs[:n], refs[n:2 * n], refs[2 * n:]
        x, y, c = _axes()
        cps = [_remote(v_refs[i], out_refs[i], send_sems.at[i], recv_sems.at[i], (x, y, 1 - c)) for i in range(n)]
        for cp in cps:
            cp.start()
        for cp in cps:
            cp.wait()

    outs = [jax.ShapeDtypeStruct(a.shape, a.dtype) for a in arrs]
    return _hbm_call(body, arrs, outs, n, 0, name)


def _sum_lead(v, name):
    k, r, w = v.shape
    tr = _pick_tile(r, (256, 128, 64, 32, 16))

    def body(v_ref, o_ref):
        acc = v_ref[0].astype(F32)
        for i in range(1, k):
            acc = acc + v_ref[i].astype(F32)
        o_ref[...] = acc

    return pl.pallas_call(
        body, grid=(r // tr,), in_specs=[pl.BlockSpec((k, tr, w), lambda i: (0, i, 0))],
        out_specs=pl.BlockSpec((tr, w), lambda i: (i, 0)), out_shape=jax.ShapeDtypeStruct((r, w), F32), name=name,
        compiler_params=_cparams(("parallel",)))(v)


_BIG = (("dn_w_in", 2), ("dn_w_out", 1), ("sb_w_qkv", 2), ("sb_w_out", 1), ("ffn_w_in", 2), ("ffn_w_out", 1))


def _to_chip_major(a, axis, nchips):
    l, r, c = a.shape
    if axis == 2:
        return a.reshape(l, r, nchips, c // nchips).transpose(2, 0, 1, 3)
    return a.reshape(l, nchips, r // nchips, c).transpose(1, 0, 2, 3)


def _from_chip_major(a, axis):
    n, l, r, c = a.shape
    if axis == 2:
        return a.transpose(1, 2, 0, 3).reshape(l, r, n * c)
    return a.transpose(1, 0, 2, 3).reshape(l, n * r, c)


def kernel(x, c, ada_w, ada_b, norm1_g, norm2_g, dn_w_in, dn_conv_w, dn_a_log, dn_dt_bias, dn_onorm_g, dn_w_out, sb_w_qkv, sb_q_norm_g, sb_k_norm_g, sb_w_out, ffn_w_in, ffn_w_out, loss_target, m_ada_w, m_ada_b, m_norm1_g, m_norm2_g, m_dn_w_in, m_dn_conv_w, m_dn_a_log, m_dn_dt_bias, m_dn_onorm_g, m_dn_w_out, m_sb_w_qkv, m_sb_q_norm_g, m_sb_k_norm_g, m_sb_w_out, m_ffn_w_in, m_ffn_w_out, v_ada_w, v_ada_b, v_norm1_g, v_norm2_g, v_dn_w_in, v_dn_conv_w, v_dn_a_log, v_dn_dt_bias, v_dn_onorm_g, v_dn_w_out, v_sb_w_qkv, v_sb_q_norm_g, v_sb_k_norm_g, v_sb_w_out, v_ffn_w_in, v_ffn_w_out):
    names = ("ada_w", "ada_b", "norm1_g", "norm2_g", "dn_w_in", "dn_conv_w", "dn_a_log", "dn_dt_bias", "dn_onorm_g",
             "dn_w_out", "sb_w_qkv", "sb_q_norm_g", "sb_k_norm_g", "sb_w_out", "ffn_w_in", "ffn_w_out")
    w = dict(zip(names, (ada_w, ada_b, norm1_g, norm2_g, dn_w_in, dn_conv_w, dn_a_log, dn_dt_bias, dn_onorm_g,
                         dn_w_out, sb_w_qkv, sb_q_norm_g, sb_k_norm_g, sb_w_out, ffn_w_in, ffn_w_out)))
    mom = dict(zip(names, (m_ada_w, m_ada_b, m_norm1_g, m_norm2_g, m_dn_w_in, m_dn_conv_w, m_dn_a_log, m_dn_dt_bias,
                           m_dn_onorm_g, m_dn_w_out, m_sb_w_qkv, m_sb_q_norm_g, m_sb_k_norm_g, m_sb_w_out, m_ffn_w_in,
                           m_ffn_w_out)))
    var = dict(zip(names, (v_ada_w, v_ada_b, v_norm1_g, v_norm2_g, v_dn_w_in, v_dn_conv_w, v_dn_a_log, v_dn_dt_bias,
                           v_dn_onorm_g, v_dn_w_out, v_sb_w_qkv, v_sb_q_norm_g, v_sb_k_norm_g, v_sb_w_out, v_ffn_w_in,
                           v_ffn_w_out)))
    ax, ay, ac = _axes()
    chip = 2 * ax + ay
    dev = 2 * chip + ac
    t, d = x.shape[1], x.shape[2]
    depth, ndn, nsb = ada_w.shape[0], dn_w_in.shape[0], sb_w_qkv.shape[0]
    heads = d // DN_HEAD_DIM
    mod_cols = ada_w.shape[2]
    conv_cols = dn_conv_w.shape[2]
    nchips, ndev = 4, 8

    conv_rows = ndn * DN_CONV * conv_cols // d
    pay1 = jnp.concatenate([c, dn_conv_w.reshape(conv_rows, d), jnp.zeros((8 - 1 - conv_rows, d), F32)], axis=0)
    g1 = _allgather_small(pay1, "ag_cond").reshape(ndev, 8, d)
    c_all = g1[:, 0]
    conv_full = g1[::2, 1:1 + conv_rows].reshape(nchips, ndn, DN_CONV, conv_cols).transpose(1, 2, 0, 3)
    conv_full = conv_full.reshape(ndn, DN_CONV, nchips * conv_cols)

    c16 = jnp.pad(c_all, ((0, 16 - ndev), (0, 0)))
    cond16 = _rowwise(lambda r, v: ([_silu(r[0])], []), [c16], [], [(d, F32)], [], name="cond_silu")[0]
    pay2 = jnp.concatenate([_mm(cond16, ada_w[i], name="ada_mod%d" % i)[:ndev] for i in range(depth)], axis=0)
    g2 = _allgather_small(pay2, "ag_mod").reshape(ndev, depth, ndev, mod_cols)[::2]
    mod_raw = lax.dynamic_index_in_dim(g2, dev, axis=2, keepdims=False)
    mod_raw = mod_raw.transpose(1, 0, 2).reshape(depth, nchips * mod_cols)
    mod = _rowwise(lambda r, v: ([r[0] + r[1]], []), [mod_raw, ada_b], [], [(nchips * mod_cols, F32)], [],
                   name="ada_bias")[0]

    axis_of = dict(_BIG)
    groups = [[0], list(range(1, depth))]

    def kinds_of(layers):
        out = []
        for n, _ in _BIG:
            idx = [l // 2 for l in layers if l % 2 == (0 if n.startswith("dn_") else 1)] if n[:3] in ("dn_", "sb_") \
                else list(layers)
            if idx:
                out.append((n, min(idx), max(idx) + 1))
        return out

    two_d = lambda a: a.reshape(-1, a.shape[-1])
    both_halves = lambda mine, other, ax: jnp.where(ac == 0, jnp.concatenate([mine, other], axis=ax),
                                                    jnp.concatenate([other, mine], axis=ax))
    w16 = {n: w[n].astype(BF16) for n, _ in _BIG}
    gathers = []
    for gi, layers in enumerate(groups):
        sl = kinds_of(layers)
        shards = [w16[n][lo:hi] for n, lo, hi in sl]
        sems, bufs, zero = _ici_start(shards, "half", "ag_start%d" % gi)
        gathers.append((sl, shards, sems, bufs))
        mod = mod + zero
    full = {}

    def finish_gather(gi, after):
        sl, shards, sems, bufs = gathers[gi]
        lands = _ici_wait(sems, bufs, "half", after, "ag_wait%d" % gi)
        for (n, lo, hi), shard, mine, other in zip(sl, shards, lands, _share_halves(lands, "ag_pair%d" % gi)):
            a = lax.dynamic_update_index_in_dim(both_halves(mine, other, 2), shard, chip, 0)
            a = _from_chip_major(a, axis_of[n])
            for l in range(lo, hi):
                full[n, l] = a[l - lo]

    padl = lambda v: jnp.pad(v[None, :], ((0, 0), (0, LANES - v.shape[0])))
    padc = lambda a: jnp.pad(a, ((0, 0), (0, LANES - a.shape[1])))

    def layer_weights(i, after):
        for gi, layers in enumerate(groups):
            if layers and i == layers[0]:
                finish_gather(gi, after)
        j = i // 2
        if i % 2 == 0:
            wi = full["dn_w_in", j]
            w_all = jnp.concatenate([wi[:, :4 * d], padc(wi[:, 4 * d:4 * d + heads]), padc(wi[:, 4 * d + heads:])],
                                    axis=1)
            mix = dict(all=w_all, out=full["dn_w_out", j], conv_w=conv_full[j], a_log=padl(dn_a_log[j]),
                       dt_bias=padl(dn_dt_bias[j]), onorm_g=dn_onorm_g[j][None])
        else:
            mix = dict(qkv=full["sb_w_qkv", j], out=full["sb_w_out", j], q_g=sb_q_norm_g[j][None],
                       k_g=sb_k_norm_g[j][None])
        return mix, dict(w_in=full["ffn_w_in", i], w_out=full["ffn_w_out", i])

    g_dn, g_sb, g_ffn = [None] * ndn, [None] * nsb, [None] * depth
    reduces = {}

    def gw_in(g):
        ga = g["w_all"]
        return jnp.concatenate([ga[:, :4 * d], ga[:, 4 * d:4 * d + heads], ga[:, 4 * d + LANES:4 * d + LANES + heads]],
                               axis=1)
    layer_grad = dict(dn_w_in=lambda j: gw_in(g_dn[j]), dn_w_out=lambda j: g_dn[j]["w_out"],
                      sb_w_qkv=lambda j: g_sb[j]["w_qkv"], sb_w_out=lambda j: g_sb[j]["w_out"],
                      ffn_w_in=lambda l: g_ffn[l]["w_in"], ffn_w_out=lambda l: g_ffn[l]["w_out"])

    def layer_grads(i, after, g_mix, g_f):
        (g_dn if i % 2 == 0 else g_sb)[i // 2], g_ffn[i] = g_mix, g_f
        zero = jnp.zeros((), F32)
        for gi, layers in enumerate(groups):
            if layers and i == layers[0]:
                sl = kinds_of(layers)
                parts = [_to_chip_major(jnp.stack([layer_grad[n](j) for j in range(lo, hi)]), axis_of[n], nchips)
                         for n, lo, hi in sl]
                pairs = []
                for (n, _, _), p, fs in zip(sl, parts, _sibling_swap_halves(parts, "gr_pair%d" % gi)):
                    rh = fs.shape[2]
                    own = lax.dynamic_slice_in_dim(p, ac * rh, rh, axis=2)
                    pairs.append(_rowwise(lambda r, v: ([r[0].astype(F32) + r[1].astype(F32)], []),
                                          [two_d(own), two_d(fs)], [], [(fs.shape[-1], BF16)], [],
                                          name="gr_pair_add%d_%s" % (gi, n))[0].reshape(fs.shape))
                sems, bufs, zero = _ici_start(pairs, "slot", "gr_start%d" % gi)
                reduces[gi] = (sl, pairs, sems, bufs)
        return zero

    loss_local, grad_x, dmod, g_n1, g_n2 = _local_step(x[0], loss_target[0], mod, norm1_g, norm2_g, layer_weights,
                                                       layer_grads)
    loss = lax.psum(loss_local, ("x", "y", "c"))

    g_conv = jnp.stack([g["conv_w"] for g in g_dn])
    misc = jnp.concatenate([jnp.concatenate([g["onorm_g"] for g in g_dn], axis=1),
                            jnp.concatenate([g["a_log"] for g in g_dn], axis=1),
                            jnp.concatenate([g["dt_bias"] for g in g_dn], axis=1),
                            jnp.concatenate([g["q_norm_g"] for g in g_sb], axis=1),
                            jnp.concatenate([g["k_norm_g"] for g in g_sb], axis=1)], axis=1)
    misc = jnp.pad(misc, ((0, 0), (0, -misc.shape[1] % d))).reshape(-1, d)
    small = [dmod.reshape(-1, d), g_n1, g_n2, g_conv.reshape(-1, d), misc]
    small_rows = [s.shape[0] for s in small]
    pad_rows = -sum(small_rows) % 8
    pay3 = jnp.concatenate(small + [jnp.zeros((pad_rows, d), F32)], axis=0)
    nrow3 = pay3.shape[0]
    g3 = _allgather_small(pay3, "ag_small")
    summed = _rowwise(lambda r, v: ([], [_colsum(r[0])]), [g3.reshape(ndev, nrow3 * d)], [], [], [(1, nrow3 * d)],
                      name="small_sum")[0].reshape(nrow3, d)
    offs = [0]
    for n_ in small_rows:
        offs.append(offs[-1] + n_)
    grads = {}
    grads["ada_b"] = summed[offs[0]:offs[1]].reshape(depth, N_MOD * d)
    grads["norm1_g"] = summed[offs[1]:offs[2]]
    grads["norm2_g"] = summed[offs[2]:offs[3]]
    conv_sum = summed[offs[3]:offs[4]].reshape(ndn, DN_CONV, nchips * conv_cols)
    grads["dn_conv_w"] = lax.dynamic_slice_in_dim(conv_sum, chip * conv_cols, conv_cols, axis=2)
    mrow = summed[offs[4]:offs[5]].reshape(-1)
    o = 0
    grads["dn_onorm_g"] = mrow[o:o + ndn * DN_HEAD_DIM].reshape(ndn, DN_HEAD_DIM)
    o += ndn * DN_HEAD_DIM
    grads["dn_a_log"] = mrow[o:o + ndn * LANES].reshape(ndn, LANES)[:, :heads]
    o += ndn * LANES
    grads["dn_dt_bias"] = mrow[o:o + ndn * LANES].reshape(ndn, LANES)[:, :heads]
    o += ndn * LANES
    grads["sb_q_norm_g"] = mrow[o:o + nsb * SB_HEAD_DIM].reshape(nsb, SB_HEAD_DIM)
    o += nsb * SB_HEAD_DIM
    grads["sb_k_norm_g"] = mrow[o:o + nsb * SB_HEAD_DIM].reshape(nsb, SB_HEAD_DIM)
    dmod_all = g3.reshape(ndev, nrow3, d)[:, :small_rows[0]].reshape(ndev, depth, N_MOD * d)
    dmod_mine = lax.dynamic_slice_in_dim(dmod_all, chip * mod_cols, mod_cols, axis=2)
    dmod16 = jnp.pad(dmod_mine, ((0, 16 - ndev), (0, 0), (0, 0)))
    grads["ada_w"] = jnp.stack([_mm(cond16, dmod16[:, i], ta=True, name="ada_gw%d" % i) for i in range(depth)])

    pieces = {n: [] for n, _ in _BIG}
    for gi in reversed(range(len(groups))):
        if gi not in reduces:
            continue
        sl, pairs, sems, bufs = reduces[gi]
        lands = _ici_wait(sems, bufs, "slot", grad_x, "gr_wait%d" % gi)
        reduced = []
        for (n, _, _), land, pair in zip(sl, lands, pairs):
            own = lax.dynamic_index_in_dim(pair, chip, 0, keepdims=True)
            land = lax.dynamic_update_slice_in_dim(land, own, chip, axis=0)
            reduced.append(_sum_lead(land.reshape(nchips, -1, land.shape[-1]), "gr_chip_add%d_%s" % (gi, n)))
        for (n, lo, hi), mine, other in zip(sl, reduced, _share_halves(reduced, "gr_share%d" % gi)):
            shape = (hi - lo, -1, mine.shape[-1])
            pieces[n].append((lo, both_halves(mine.reshape(shape), other.reshape(shape), 1)))
    for n, _ in _BIG:
        grads[n] = jnp.concatenate([p for _, p in sorted(pieces[n], key=lambda t: t[0])], axis=0)

    delta, new_m, new_v = {}, {}, {}
    for n in names:
        delta[n], new_m[n], new_v[n] = _adamw(w[n], grads[n], mom[n], var[n], "adamw_" + n)
    return (loss, grad_x[None], *[grads[n] for n in names], *[delta[n] for n in names], *[new_m[n] for n in names],
            *[new_v[n] for n in names])
```

```python
import functools

import jax
import jax.numpy as jnp
from jax import lax
from jax.experimental import pallas as pl
from jax.experimental.pallas import tpu as pltpu

F32 = jnp.float32
BF16 = jnp.bfloat16
HI = lax.Precision.HIGHEST
MESH = pl.DeviceIdType.MESH

EPS = 1e-6
N_MOD = 6
DN_HEAD_DIM = 128
DN_CONV = 4
DN_CHUNK = 64
DN_GROUP = 8
DN_GATE_ROWS = 256
SB_HEAD_DIM = 64
SB_BLOCK = 128
SB_QBLOCK = 512
SB_PART = 128
LANES = 128
VMEM_LIMIT = 56 * 1024 * 1024
MM_BLOCK_BUDGET = 36 * 1024 * 1024
ROW_BLOCK_BUDGET = 20 * 1024 * 1024

ADAM_LR = 0.001
ADAM_B1 = 0.9
ADAM_B2 = 0.999
ADAM_EPS = 1e-08
ADAM_WD = 0.01
ADAM_STEP = 10


def _cparams(sem=None):
    return pltpu.CompilerParams(dimension_semantics=sem, vmem_limit_bytes=VMEM_LIMIT)


def _sigmoid(x):
    return 1.0 / (1.0 + jnp.exp(-x))


def _silu(x):
    return x * _sigmoid(x)


def _dsilu(x):
    s = _sigmoid(x)
    return s * (1.0 + x * (1.0 - s))


def _softplus(x):
    return jnp.maximum(x, 0.0) + jnp.log1p(jnp.exp(-jnp.abs(x)))


def _pick_tile(n, prefs):
    for t in prefs:
        if n % t == 0:
            return t
    return n


def _mm(a, b, *, ta=False, tb=False, out_dtype=F32, name):
    m = a.shape[1] if ta else a.shape[0]
    k = a.shape[0] if ta else a.shape[1]
    n = b.shape[0] if tb else b.shape[1]
    assert (b.shape[1] if tb else b.shape[0]) == k
    size = lambda dt: jnp.dtype(dt).itemsize
    best = None
    for tm in sorted({t for t in (m, 2048, 1024, 512, 256, 128) if m % t == 0 and (t % 128 == 0 or t == m)}):
        for tn in sorted({t for t in (n, 2816, 1408, 1024, 768, 512, 256, 128) if n % t == 0 and (t % 128 == 0 or t == n)}):
            need = 2 * (tm * k * size(a.dtype) + tn * k * size(b.dtype) + tm * tn * size(out_dtype))
            if need <= MM_BLOCK_BUDGET and (best is None or tm * tn > best[0] * best[1]):
                best = (tm, tn)
    tm, tn = best
    dims = (((0 if ta else 1,), (1 if tb else 0,)), ((), ()))

    def body(a_ref, b_ref, o_ref):
        av = a_ref[...].astype(BF16)
        bv = b_ref[...].astype(BF16)
        o_ref[...] = lax.dot_general(av, bv, dims, preferred_element_type=F32).astype(o_ref.dtype)

    a_spec = pl.BlockSpec((k, tm), lambda i, j: (0, i)) if ta else pl.BlockSpec((tm, k), lambda i, j: (i, 0))
    b_spec = pl.BlockSpec((tn, k), lambda i, j: (j, 0)) if tb else pl.BlockSpec((k, tn), lambda i, j: (0, j))
    return pl.pallas_call(
        body, grid=(m // tm, n // tn), in_specs=[a_spec, b_spec],
        out_specs=pl.BlockSpec((tm, tn), lambda i, j: (i, j)),
        out_shape=jax.ShapeDtypeStruct((m, n), out_dtype), name=name,
        compiler_params=_cparams(("parallel", "parallel")))(a, b)


def _rowwise(body, rows, vecs, outs, accs, *, name, tr=None):
    rows = [r if isinstance(r, tuple) else (r, r.shape[1], 0) for r in rows]
    nrows = rows[0][0].shape[0]
    if tr is None:
        per_row = sum(w * jnp.dtype(a.dtype).itemsize for a, w, _ in rows) + sum(
            w * jnp.dtype(dt).itemsize for w, dt in outs)
        tr = next((t for t in (2048, 1024, 512, 256) if nrows % t == 0 and 2 * t * per_row <= ROW_BLOCK_BUDGET), 256)
    tr = tr if nrows % tr == 0 else nrows
    nr, nv, no = len(rows), len(vecs), len(outs)

    def kern(*refs):
        r_in, v_in = refs[:nr], refs[nr:nr + nv]
        o_refs, a_refs = refs[nr + nv:nr + nv + no], refs[nr + nv + no:]
        res_o, res_a = body([r[...] for r in r_in], [v[...] for v in v_in])
        for o, val in zip(o_refs, res_o):
            o[...] = val.astype(o.dtype)
        if a_refs:
            @pl.when(pl.program_id(0) == 0)
            def _():
                for a in a_refs:
                    a[...] = jnp.zeros(a.shape, a.dtype)
            for a, val in zip(a_refs, res_a):
                a[...] += val

    in_specs = [pl.BlockSpec((tr, w), functools.partial(lambda i, cb: (i, cb), cb=cb)) for _, w, cb in rows]
    in_specs += [pl.BlockSpec(v.shape, functools.partial(lambda i, nd: (0,) * nd, nd=v.ndim)) for v in vecs]
    out_specs = [pl.BlockSpec((tr, w), lambda i: (i, 0)) for w, _ in outs]
    out_specs += [pl.BlockSpec(s, lambda i: (0, 0)) for s in accs]
    out_shape = [jax.ShapeDtypeStruct((nrows, w), dt) for w, dt in outs]
    out_shape += [jax.ShapeDtypeStruct(s, F32) for s in accs]
    res = pl.pallas_call(
        kern, grid=(nrows // tr,), in_specs=in_specs, out_specs=out_specs, out_shape=out_shape, name=name,
        compiler_params=_cparams(("arbitrary",) if accs else ("parallel",)))(*[r[0] for r in rows], *vecs)
    return res


def _colsum(v):
    return jnp.sum(v, axis=0, keepdims=True)


def _adaln_fwd(x, g, sc, sh, name):
    def body(r, v):
        (xv,), (gv, scv, shv) = r, v
        rs = lax.rsqrt(jnp.mean(xv * xv, axis=-1, keepdims=True) + EPS)
        return [(xv * rs) * gv * (1.0 + scv) + shv], []
    return _rowwise(body, [x], [g, sc, sh], [(x.shape[1], BF16)], [], name=name)[0]


def _adaln_bwd(x, dh, dxr, g, sc, name):
    d = x.shape[1]

    def body(r, v):
        (xv, dhv, dxv), (gv, scv) = r, v
        rs = lax.rsqrt(jnp.mean(xv * xv, axis=-1, keepdims=True) + EPS)
        nv = xv * rs
        dn = dhv * (gv * (1.0 + scv))
        dx = rs * (dn - nv * jnp.mean(dn * nv, axis=-1, keepdims=True)) + dxv
        dhn = dhv * nv
        return [dx], [_colsum(dhn * (1.0 + scv)), _colsum(dhn * gv), _colsum(dhv)]
    return _rowwise(body, [x, dh, dxr], [g, sc], [(d, F32)], [(1, d)] * 3, name=name)


def _resid_fwd(x, y, gt, name):
    def body(r, v):
        return [r[0] + v[0] * r[1]], []
    return _rowwise(body, [x, y], [gt], [(x.shape[1], F32)], [], name=name)[0]


def _resid_bwd(dx, y, gt, name):
    d = dx.shape[1]

    def body(r, v):
        return [v[0] * r[0]], [_colsum(r[0] * r[1])]
    return _rowwise(body, [dx, y], [gt], [(d, BF16)], [(1, d)], name=name)


def _swiglu_fwd(u, name):
    f = u.shape[1] // 2

    def body(r, v):
        uv = r[0]
        return [_silu(uv[:, :f]) * uv[:, f:]], []
    return _rowwise(body, [u], [], [(f, BF16)], [], name=name)[0]


def _swiglu_bwd(u, da, name):
    f = u.shape[1] // 2

    def body(r, v):
        uv, dav = r
        gate, up = uv[:, :f], uv[:, f:]
        return [jnp.concatenate([dav * up * _dsilu(gate), dav * _silu(gate)], axis=1)], []
    return _rowwise(body, [u, da], [], [(2 * f, BF16)], [], name=name)[0]


def _loss_fwd_bwd(y, tgt, name):
    d = y.shape[1]

    def body(r, v):
        err = r[0] - r[1]
        part = jnp.sum(jnp.sum(err * err, axis=1, keepdims=True), axis=0, keepdims=True) * (0.5 / d)
        return [err * (1.0 / d)], [jnp.broadcast_to(part, (1, LANES))]
    dy, acc = _rowwise(body, [y, tgt], [], [(d, F32)], [(1, LANES)], name=name)
    return acc[0, 0], dy


def _head_means(v):
    row = lax.broadcasted_iota(jnp.int32, (LANES, LANES), 0)
    col = lax.broadcasted_iota(jnp.int32, (LANES, LANES), 1)
    same = ((row // SB_HEAD_DIM) == (col // SB_HEAD_DIM)).astype(F32)
    parts = [jnp.dot(v[:, g * LANES:(g + 1) * LANES], same, precision=HI, preferred_element_type=F32)
             for g in range(v.shape[1] // LANES)]
    return jnp.concatenate(parts, axis=1) * (1.0 / SB_HEAD_DIM)


def _sb_norm_fwd(qkv, gq, gk, name):
    d = qkv.shape[1] // 3

    def body(r, v):
        return [x * lax.rsqrt(_head_means(x * x) + EPS) * g for x, g in zip(r, v)], []
    return _rowwise(body, [(qkv, d, 0), (qkv, d, 1)], [gq, gk], [(d, F32), (d, F32)], [], name=name)


def _sb_norm_bwd(qkv, dqn, dkn, dv, gq, gk, name):
    d = qkv.shape[1] // 3

    def body(r, v):
        outs, accs = [], []
        for x, dy, g in ((r[0], r[2], v[0]), (r[1], r[3], v[1])):
            rs = lax.rsqrt(_head_means(x * x) + EPS)
            nv = x * rs
            dn = dy * g
            outs.append(rs * (dn - nv * _head_means(dn * nv)))
            accs.append(_colsum(dy * nv))
        return [jnp.concatenate(outs + [r[4]], axis=1)], accs
    return _rowwise(body, [(qkv, d, 0), (qkv, d, 1), dqn, dkn, dv], [gq, gk], [(3 * d, BF16)], [(1, d), (1, d)],
                    name=name)


def _head_tiles(a):
    return [a[:, h * DN_HEAD_DIM:(h + 1) * DN_HEAD_DIM] for h in range(a.shape[1] // DN_HEAD_DIM)]


def _dn_post_fwd(o, proj, g, name):
    d = o.shape[1]

    def body(r, v):
        outs = []
        for ov, zv in zip(_head_tiles(r[0]), _head_tiles(r[1])):
            rs = lax.rsqrt(jnp.mean(ov * ov, axis=-1, keepdims=True) + EPS)
            outs.append(ov * rs * v[0] * _silu(zv))
        return [jnp.concatenate(outs, axis=1)], []
    return _rowwise(body, [o, (proj, d, 3)], [g], [(d, BF16)], [], name=name)[0]


def _dn_post_bwd(o, proj, don, g, name):
    d = o.shape[1]

    def body(r, v):
        dos, dzs, dg = [], [], jnp.zeros((1, DN_HEAD_DIM), F32)
        for ov, zv, dv in zip(_head_tiles(r[0]), _head_tiles(r[1]), _head_tiles(r[2])):
            rs = lax.rsqrt(jnp.mean(ov * ov, axis=-1, keepdims=True) + EPS)
            nv = ov * rs
            s = _silu(zv)
            dn = dv * v[0] * s
            dos.append(rs * (dn - nv * jnp.mean(dn * nv, axis=-1, keepdims=True)))
            dzs.append(dv * nv * v[0] * _dsilu(zv))
            dg = dg + _colsum(dv * nv * s)
        return [jnp.concatenate(dos, axis=1), jnp.concatenate(dzs, axis=1)], [dg]
    return _rowwise(body, [o, (proj, d, 3), don], [g], [(d, F32), (d, BF16)], [(1, DN_HEAD_DIM)], name=name)


def _chunk_tri(tr, upper):
    row = lax.broadcasted_iota(jnp.int32, (tr, tr), 0)
    col = lax.broadcasted_iota(jnp.int32, (tr, tr), 1)
    same = (row // DN_CHUNK) == (col // DN_CHUNK)
    return (same & ((row <= col) if upper else (row >= col))).astype(F32)


def _dn_gate_fwd(proj, colblk, a_log, dt_bias, name):
    def body(r, v):
        ab = r[0]
        a, b = ab[:, :LANES], ab[:, LANES:]
        g = -jnp.exp(v[0]) * _softplus(a + v[1])
        big_g = jnp.dot(_chunk_tri(g.shape[0], False), g, precision=HI, preferred_element_type=F32)
        return [big_g, _sigmoid(b)], []
    return _rowwise(body, [(proj, 2 * LANES, colblk)], [a_log, dt_bias], [(LANES, F32), (LANES, F32)], [], name=name,
                    tr=DN_GATE_ROWS)


def _dn_gate_bwd(proj, colblk, d_big_g, dbeta, a_log, dt_bias, name):
    def body(r, v):
        ab, dgc, dbt = r
        a, b = ab[:, :LANES], ab[:, LANES:]
        dg = jnp.dot(_chunk_tri(dgc.shape[0], True), dgc, precision=HI, preferred_element_type=F32)
        na = -jnp.exp(v[0])
        pre = a + v[1]
        da = dg * na * _sigmoid(pre)
        beta = _sigmoid(b)
        db = dbt * beta * (1.0 - beta)
        return [jnp.concatenate([da, db], axis=1)], [_colsum(dg * na * _softplus(pre)), _colsum(da)]
    return _rowwise(body, [(proj, 2 * LANES, colblk), d_big_g, dbeta], [a_log, dt_bias],
                    [(2 * LANES, BF16)], [(1, LANES), (1, LANES)], name=name, tr=DN_GATE_ROWS)


def _shift_rows(x, s):
    if s == 0:
        return x
    t = x.shape[0]
    row = lax.broadcasted_iota(jnp.int32, x.shape, 0)
    rolled = pltpu.roll(x, s % t, axis=0)
    return jnp.where((row >= s) if s > 0 else (row < t + s), rolled, 0.0)


def _rnd(x):
    return x.astype(BF16).astype(F32)


def _conv(x, c_ref):
    c = x * _rnd(c_ref[DN_CONV - 1:DN_CONV, :])
    for s in range(1, DN_CONV):
        c = c + _shift_rows(x, s) * _rnd(c_ref[DN_CONV - 1 - s:DN_CONV - s, :])
    return c


def _dn_prep_fwd(proj, conv_w, heads, name):
    t = proj.shape[0]
    d = heads * DN_HEAD_DIM

    def body(xq, xk, xv, cq, ck, cv, q_ref, k_ref, v_ref):
        for x_ref, c_ref, o_ref, norm in ((xq, cq, q_ref, True), (xk, ck, k_ref, True), (xv, cv, v_ref, False)):
            y = _silu(_conv(_rnd(x_ref[...]), c_ref))
            if norm:
                y = y * lax.rsqrt(jnp.sum(y * y, axis=-1, keepdims=True) + EPS)
            o_ref[...] = y

    xs = [pl.BlockSpec((t, DN_HEAD_DIM), functools.partial(lambda h, o: (0, h + o), o=o * heads)) for o in range(3)]
    cs = [pl.BlockSpec((DN_CONV, DN_HEAD_DIM), functools.partial(lambda h, o: (0, h + o), o=o * heads)) for o in range(3)]
    return pl.pallas_call(
        body, grid=(heads,), in_specs=xs + cs,
        out_specs=[pl.BlockSpec((t, DN_HEAD_DIM), lambda h: (0, h))] * 3,
        out_shape=[jax.ShapeDtypeStruct((t, d), F32)] * 3, name=name,
        compiler_params=_cparams(("parallel",)))(proj, proj, proj, conv_w, conv_w, conv_w)


def _dn_prep_bwd(proj, conv_w, dq, dk, dv, heads, name):
    t = proj.shape[0]
    d = heads * DN_HEAD_DIM

    def body(xq, xk, xv, cq, ck, cv, gq, gk, gv, oq, ok, ov, wq, wk, wv):
        for x_ref, c_ref, g_ref, o_ref, w_ref, norm in ((xq, cq, gq, oq, wq, True), (xk, ck, gk, ok, wk, True),
                                                        (xv, cv, gv, ov, wv, False)):
            x, dy = _rnd(x_ref[...]), g_ref[...]
            c = _conv(x, c_ref)
            if norm:
                y = _silu(c)
                rs = lax.rsqrt(jnp.sum(y * y, axis=-1, keepdims=True) + EPS)
                yn = y * rs
                dy = rs * (dy - yn * jnp.sum(dy * yn, axis=-1, keepdims=True))
            dc = _rnd(dy * _dsilu(c))
            dx = dc * _rnd(c_ref[DN_CONV - 1:DN_CONV, :])
            w_ref[DN_CONV - 1:DN_CONV, :] = _colsum(dc * x)
            for s in range(1, DN_CONV):
                dx = dx + _shift_rows(dc, -s) * _rnd(c_ref[DN_CONV - 1 - s:DN_CONV - s, :])
                w_ref[DN_CONV - 1 - s:DN_CONV - s, :] = _colsum(dc * _shift_rows(x, s))
            o_ref[...] = dx.astype(o_ref.dtype)

    xs = [pl.BlockSpec((t, DN_HEAD_DIM), functools.partial(lambda h, o: (0, h + o), o=o * heads)) for o in range(3)]
    cs = [pl.BlockSpec((DN_CONV, DN_HEAD_DIM), functools.partial(lambda h, o: (0, h + o), o=o * heads)) for o in range(3)]
    hs = pl.BlockSpec((t, DN_HEAD_DIM), lambda h: (0, h))
    ws = pl.BlockSpec((DN_CONV, DN_HEAD_DIM), lambda h: (0, h))
    return pl.pallas_call(
        body, grid=(heads,), in_specs=xs + cs + [hs] * 3, out_specs=[hs] * 3 + [ws] * 3,
        out_shape=[jax.ShapeDtypeStruct((t, d), BF16)] * 3 + [jax.ShapeDtypeStruct((DN_CONV, d), F32)] * 3, name=name,
        compiler_params=_cparams(("parallel",)))(proj, proj, proj, conv_w, conv_w, conv_w, dq, dk, dv)


X3 = "three bf16 passes"


def _bdot(a, b, dims, prec):
    if prec != X3:
        return lax.dot_general(a, b, dims, precision=prec, preferred_element_type=F32)
    a1, b1 = a.astype(BF16), b.astype(BF16)
    a2, b2 = (a - a1.astype(F32)).astype(BF16), (b - b1.astype(F32)).astype(BF16)
    dot = lambda x, y: lax.dot_general(x, y, dims, preferred_element_type=F32)
    return dot(a1, b1) + (dot(a1, b2) + dot(a2, b1))


def _bmm(a, b, prec=None):
    return _bdot(a, b, (((2,), (1,)), ((0,), (0,))), prec)


def _bmm_nt(a, b, prec=None):
    return _bdot(a, b, (((2,), (2,)), ((0,), (0,))), prec)


def _bmm_tn(a, b, prec=None):
    return _bdot(a, b, (((1,), (1,)), ((0,), (0,))), prec)


def _dn_local(qg, kg, vg, gc, gr, bt):
    c = qg.shape[1]
    row = lax.broadcasted_iota(jnp.int32, (c, c), 0)
    col = lax.broadcasted_iota(jnp.int32, (c, c), 1)
    incl, strict = (row >= col)[None], (row > col)[None]
    decay = jnp.where(incl, jnp.exp(jnp.where(incl, gc - gr, 0.0)), 0.0)
    kb = kg * bt
    vb = vg * bt
    m = _bmm_nt(kb.astype(BF16), kg.astype(BF16))
    a = jnp.where(strict, m * decay, 0.0)
    bp = -a
    tm = jnp.where((row == col)[None], 1.0, 0.0) + bp
    steps = max(1, (c - 1).bit_length()) - 1
    for _ in range(steps):
        bp = _bmm(bp, bp, X3)
        tm = tm + _bmm(tm, bp, X3)
    eg = jnp.exp(gc)
    glast = gc[:, c - 1:c, :]
    erel = jnp.exp(glast - gc)
    kbg = kb * eg
    qm = _bmm_nt(qg.astype(BF16), kg.astype(BF16))
    return dict(decay=decay, strict=strict, kb=kb, vb=vb, m=m, tm=tm, eg=eg, erel=erel, kbg=kbg, qm=qm,
                u=_bmm(tm, vb, X3), w=_bmm(tm, kbg, X3), qd=qg * eg, kt=kg * erel, gl=jnp.exp(glast))


def _dot(a, b):
    return jnp.dot(a.astype(BF16), b.astype(BF16), preferred_element_type=F32)


def _dot_nt(a, b):
    return lax.dot_general(a.astype(BF16), b.astype(BF16), (((1,), (1,)), ((), ())), preferred_element_type=F32)


def _dot_tn(a, b):
    return lax.dot_general(a.astype(BF16), b.astype(BF16), (((0,), (0,)), ((), ())), preferred_element_type=F32)


def _dn_chunk_specs(t, heads):
    n = t // DN_CHUNK
    hs = pl.BlockSpec((t, DN_HEAD_DIM), lambda h: (0, h))
    gcs = pl.BlockSpec((1, n, DN_CHUNK, 1), lambda h: (h, 0, 0, 0))
    grs = pl.BlockSpec((1, n, 1, DN_CHUNK), lambda h: (h, 0, 0, 0))
    ss = pl.BlockSpec((1, n, DN_HEAD_DIM, DN_HEAD_DIM), lambda h: (h, 0, 0, 0))
    return n, hs, gcs, grs, ss


def _dn_chunk_fwd(q, k, v, gc, gr, bc, name):
    t, d = q.shape
    heads = d // DN_HEAD_DIM
    c, dk = DN_CHUNK, DN_HEAD_DIM
    n, hs, gcs, grs, ss = _dn_chunk_specs(t, heads)
    nb = min(DN_GROUP, n)
    scale = dk ** -0.5

    def body(q_ref, k_ref, v_ref, gc_ref, gr_ref, b_ref, o_ref, s_ref, u_s, w_s, at_s, qd_s, kt_s, gl_s):
        def group(gi, carry):
            r0 = pl.multiple_of(gi * (nb * c), nb * c)
            n0 = gi * nb
            ld = lambda ref: ref[pl.ds(r0, nb * c), :].reshape(nb, c, dk)
            loc = _dn_local(ld(q_ref) * scale, ld(k_ref), ld(v_ref), gc_ref[0, pl.ds(n0, nb)],
                            gr_ref[0, pl.ds(n0, nb)], b_ref[0, pl.ds(n0, nb)])
            u_s[pl.ds(n0, nb)] = loc["u"]
            w_s[pl.ds(n0, nb)] = loc["w"]
            at_s[pl.ds(n0, nb)] = loc["qm"] * loc["decay"]
            qd_s[pl.ds(n0, nb)] = loc["qd"]
            kt_s[pl.ds(n0, nb)] = loc["kt"]
            gl_s[pl.ds(n0, nb)] = loc["gl"]
            return carry
        lax.fori_loop(0, n // nb, group, 0)

        def chunk(i, s):
            s_ref[0, i] = s
            vnew = u_s[i] - _dot(w_s[i], s)
            o = _dot(qd_s[i], s) + _dot(at_s[i], vnew)
            o_ref[pl.ds(pl.multiple_of(i * c, c), c), :] = o
            return s * gl_s[i] + _dot_tn(kt_s[i], vnew)
        lax.fori_loop(0, n, chunk, jnp.zeros((dk, dk), F32))

    scratch = [pltpu.VMEM((n, c, dk), F32), pltpu.VMEM((n, c, dk), F32), pltpu.VMEM((n, c, c), F32),
               pltpu.VMEM((n, c, dk), F32), pltpu.VMEM((n, c, dk), F32), pltpu.VMEM((n, 1, 1), F32)]
    return pl.pallas_call(
        body, grid=(heads,), in_specs=[hs, hs, hs, gcs, grs, gcs], out_specs=[hs, ss],
        out_shape=[jax.ShapeDtypeStruct((t, d), F32), jax.ShapeDtypeStruct((heads, n, dk, dk), F32)],
        scratch_shapes=scratch, name=name, compiler_params=_cparams(("parallel",)))(q, k, v, gc, gr, bc)


def _dn_chunk_bwd(q, k, v, gc, gr, bc, s_all, do, name):
    t, d = q.shape
    heads = d // DN_HEAD_DIM
    c, dk = DN_CHUNK, DN_HEAD_DIM
    n, hs, gcs, grs, ss = _dn_chunk_specs(t, heads)
    nb = min(DN_GROUP, n)
    scale = dk ** -0.5

    def body(q_ref, k_ref, v_ref, gc_ref, gr_ref, b_ref, s_ref, do_ref,
             dq_ref, dk_ref, dv_ref, dgc_ref, dgr_ref, db_ref,
             u_s, w_s, att_s, qd_s, kt_s, gl_s, du_s, dw_s, dat_s, dqd_s, dkt_s, dgl_s):
        def load_group(gi):
            r0 = pl.multiple_of(gi * (nb * c), nb * c)
            n0 = gi * nb
            ld = lambda ref: ref[pl.ds(r0, nb * c), :].reshape(nb, c, dk)
            qg, kg, vg = ld(q_ref) * scale, ld(k_ref), ld(v_ref)
            gcv, grv, bt = gc_ref[0, pl.ds(n0, nb)], gr_ref[0, pl.ds(n0, nb)], b_ref[0, pl.ds(n0, nb)]
            return r0, n0, qg, kg, vg, gcv, grv, bt, _dn_local(qg, kg, vg, gcv, grv, bt)

        def group_a(gi, carry):
            _, n0, qg, kg, _, gcv, grv, _, loc = load_group(gi)
            row = lax.broadcasted_iota(jnp.int32, (c, c), 0)
            col = lax.broadcasted_iota(jnp.int32, (c, c), 1)
            upper = (col >= row)[None]
            decay_t = jnp.where(upper, jnp.exp(jnp.where(upper, grv - gcv, 0.0)), 0.0)
            u_s[pl.ds(n0, nb)] = loc["u"]
            w_s[pl.ds(n0, nb)] = loc["w"]
            att_s[pl.ds(n0, nb)] = _bmm_nt(kg.astype(BF16), qg.astype(BF16)) * decay_t
            qd_s[pl.ds(n0, nb)] = loc["qd"]
            kt_s[pl.ds(n0, nb)] = loc["kt"]
            gl_s[pl.ds(n0, nb)] = loc["gl"]
            return carry
        lax.fori_loop(0, n // nb, group_a, 0)

        def chunk_b(it, ds_next):
            i = n - 1 - it
            s = s_ref[0, i]
            dov = do_ref[pl.ds(pl.multiple_of(i * c, c), c), :]
            w, kt = w_s[i], kt_s[i]
            vnew = u_s[i] - _dot(w, s)
            dvnew = _dot(att_s[i], dov) + _dot(kt, ds_next)
            du_s[i] = dvnew
            dw_s[i] = -_dot_nt(dvnew, s)
            dat_s[i] = _dot_nt(dov, vnew)
            dqd_s[i] = _dot_nt(dov, s)
            dkt_s[i] = _dot_nt(vnew, ds_next)
            dgl_s[i] = jnp.sum(jnp.sum(ds_next * s, axis=1, keepdims=True), axis=0, keepdims=True)
            return ds_next * gl_s[i] + _dot_tn(qd_s[i], dov) - _dot_tn(w, dvnew)
        lax.fori_loop(0, n, chunk_b, jnp.zeros((dk, dk), F32))

        def group_c(gi, carry):
            r0, n0, qg, kg, vg, gcv, grv, bt, loc = load_group(gi)
            sl = pl.ds(n0, nb)
            du, dw, dat, dqd, dkt, dgl = du_s[sl], dw_s[sl], dat_s[sl], dqd_s[sl], dkt_s[sl], dgl_s[sl]
            tm, decay, kb, kbg = loc["tm"], loc["decay"], loc["kb"], loc["kbg"]
            dvb = _bmm_tn(tm, du, X3)
            dkbg = _bmm_tn(tm, dw, X3)
            dt = _bmm_nt(du, loc["vb"], X3) + _bmm_nt(dw, kbg, X3)
            da = jnp.where(loc["strict"], -_bmm_tn(tm, _bmm_nt(dt, tm, X3), X3), 0.0)
            dms = (da * decay).astype(BF16)
            dqs = (dat * decay).astype(BF16)
            kgb = kg.astype(BF16)
            dkb = _bmm(dms, kgb) + dkbg * loc["eg"]
            dqt = _bmm(dqs, kgb) + dqd * loc["eg"]
            dkk = _bmm_tn(dms, kb.astype(BF16)) + _bmm_tn(dqs, qg.astype(BF16)) + dkt * loc["erel"] + dkb * bt
            e = (da * loc["m"] + dat * loc["qm"]) * decay
            lsum = lambda x: jnp.sum(x, axis=2, keepdims=True)
            dkt_kt = lsum(dkt * loc["kt"])
            dgcv = lsum(e) + lsum(dqd * loc["qd"]) - dkt_kt + lsum(dkbg * kbg)
            dglast = jnp.sum(dkt_kt, axis=1, keepdims=True) + dgl * loc["gl"]
            rowc = lax.broadcasted_iota(jnp.int32, (1, c, 1), 1)
            dgc_ref[0, sl] = dgcv + jnp.where(rowc == c - 1, dglast, 0.0)
            dgr_ref[0, sl] = -jnp.sum(e, axis=1, keepdims=True)
            db_ref[0, sl] = lsum(dkb * kg) + lsum(dvb * vg)
            rows = pl.ds(r0, nb * c)
            dq_ref[rows, :] = (dqt * scale).reshape(nb * c, dk)
            dk_ref[rows, :] = dkk.reshape(nb * c, dk)
            dv_ref[rows, :] = (dvb * bt).reshape(nb * c, dk)
            return carry
        lax.fori_loop(0, n // nb, group_c, 0)

    big = lambda: pltpu.VMEM((n, c, dk), F32)
    sq = lambda: pltpu.VMEM((n, c, c), F32)
    one = lambda: pltpu.VMEM((n, 1, 1), F32)
    scratch = [big(), big(), sq(), big(), big(), one(), big(), big(), sq(), big(), big(), one()]
    return pl.pallas_call(
        body, grid=(heads,), in_specs=[hs, hs, hs, gcs, grs, gcs, ss, hs], out_specs=[hs, hs, hs, gcs, grs, gcs],
        out_shape=[jax.ShapeDtypeStruct((t, d), F32)] * 3 + [
            jax.ShapeDtypeStruct((heads, n, c, 1), F32), jax.ShapeDtypeStruct((heads, n, 1, c), F32),
            jax.ShapeDtypeStruct((heads, n, c, 1), F32)],
        scratch_shapes=scratch, name=name, compiler_params=_cparams(("parallel",)))(q, k, v, gc, gr, bc, s_all, do)


def _dot01(x, m01):
    x1 = x.astype(BF16)
    r1 = x - x1.astype(F32)
    x2 = r1.astype(BF16)
    x3 = (r1 - x2.astype(F32)).astype(BF16)
    dot = lambda a: jnp.dot(a, m01, preferred_element_type=F32)
    return (dot(x1) + dot(x2)) + dot(x3)


def _sb_logs(z, mask):
    rows, bk = z.shape
    row = lax.broadcasted_iota(jnp.int32, (bk, bk), 0)
    col = lax.broadcasted_iota(jnp.int32, (bk, bk), 1)
    later = (row > col).astype(BF16)
    lm, ls, cs = [], [], []
    for p in [slice(p, p + SB_PART) for p in range(0, rows, SB_PART)]:
        lsm = -_softplus(z[p])
        lm.append(lsm if mask is None else jnp.where(mask[p], lsm, 0.0))
        ls.append(z[p] + lsm)
        cs.append(_dot01(lm[-1], later))
    cat = lambda xs: jnp.concatenate(xs, axis=0)
    return cat(ls), cat(cs), jnp.sum(cat(lm), axis=1, keepdims=True)


def _sb_masks(bq, bk):
    row = lax.broadcasted_iota(jnp.int32, (2 * bq, bk), 0)
    row = jnp.where(row >= bq, row - bq, row)
    col = lax.broadcasted_iota(jnp.int32, (2 * bq, bk), 1)
    return [(col + dd * bk) < row for dd in range(bq // bk)]


def _sb_sweep(bq, bk, qi, fetch, visit, carry):
    nd = bq // bk
    masks = _sb_masks(bq, bk)
    ahead = lambda j: fetch(jnp.maximum(j - 1, 0))
    pre = fetch(qi * nd + nd - 1)
    for dd in reversed(range(nd)):
        nxt = ahead(qi * nd + dd)
        carry = visit(qi * nd + dd, masks[dd], carry, pre)
        pre = nxt

    def below(it, c):
        j = qi * nd - 1 - it
        nxt = ahead(j)
        return visit(j, None, c[0], c[1]), nxt
    return lax.fori_loop(0, qi * nd, below, (carry, pre))[0]


def _sb_sweep2(bq, bk, qi, fetch, first, second, carry):
    nd = bq // bk
    masks = _sb_masks(bq, bk)
    ahead = lambda j: fetch(jnp.maximum(j - 1, 0))
    pre = fetch(qi * nd + nd - 1)
    r = jnp.zeros((2 * bq, 1), F32)
    pend = None
    for dd in reversed(range(nd)):
        j = qi * nd + dd
        nxt = ahead(j)
        ls, cs, rsum, extras = first(masks[dd], pre)
        if pend is not None:
            carry = second(pend[0], carry, *pend[1])
        pend = (j, (jnp.where(masks[dd], ls + cs + r, -1e30),) + extras)
        r, pre = r + rsum, nxt

    def below(it, c):
        carry, r, pre, pj, pargs = c
        j = qi * nd - 1 - it
        nxt = ahead(j)
        ls, cs, rsum, extras = first(None, pre)
        carry = second(pj, carry, *pargs)
        return carry, r + rsum, nxt, j, (ls + cs + r,) + extras
    carry, _, _, pj, pargs = lax.fori_loop(0, qi * nd, below, (carry, r, pre) + pend)
    return second(pj, carry, *pargs)


def _stack_heads(a, h0):
    return jnp.concatenate([jnp.where(h0, a, 0.0), jnp.where(h0, 0.0, a)], axis=0).astype(BF16)


def _side_by_side(a, bq):
    return jnp.concatenate([a[:bq], a[bq:]], axis=1)


def _sb_specs(t, d, bq, vcol):
    qs = pl.BlockSpec((bq, LANES), lambda g, i: (i, g))
    ks = pl.BlockSpec((t, LANES), lambda g, i: (0, g))
    vs = pl.BlockSpec((t, LANES), lambda g, i: (0, g + vcol))
    return qs, ks, vs


def _sb_fwd(qn, kn, qkv, name):
    t, d = qn.shape
    bq, bk = min(SB_QBLOCK, t), min(SB_BLOCK, t)
    scale = SB_HEAD_DIM ** -0.5
    nt = (((1,), (1,)), ((), ()))

    def body(q_ref, k_ref, v_ref, o_ref):
        qi = pl.program_id(1)
        h0 = lax.broadcasted_iota(jnp.int32, (1, LANES), 1) < SB_HEAD_DIM
        q2 = _stack_heads(q_ref[...] * scale, h0)

        tile_rows = lambda j: pl.ds(pl.multiple_of(j * bk, bk), bk)

        def fetch(j):
            return lax.dot_general(q2, k_ref[tile_rows(j), :].astype(BF16), nt, preferred_element_type=F32)

        def first(mask, z):
            return _sb_logs(z, mask) + ((),)

        def second(j, acc, loga):
            a2 = _side_by_side(jnp.exp(loga).astype(BF16), bq)
            return acc + jnp.dot(a2, _stack_heads(v_ref[tile_rows(j), :], h0), preferred_element_type=F32)
        acc = _sb_sweep2(bq, bk, qi, fetch, first, second, jnp.zeros((bq, LANES), F32))
        o_ref[...] = acc.astype(o_ref.dtype)

    qs, ks, vs = _sb_specs(t, d, bq, 2 * d // LANES)
    return pl.pallas_call(
        body, grid=(d // LANES, t // bq), in_specs=[qs, ks, vs], out_specs=qs,
        out_shape=jax.ShapeDtypeStruct((t, d), BF16), name=name,
        compiler_params=_cparams(("parallel", "parallel")))(qn, kn, qkv)


def _sb_bwd(qn, kn, qkv, do, name):
    t, d = qn.shape
    bq, bk = min(SB_QBLOCK, t), min(SB_BLOCK, t)
    scale = SB_HEAD_DIM ** -0.5
    nt = (((1,), (1,)), ((), ()))

    def body(q_ref, k_ref, v_ref, do_ref, dq_ref, dk_ref, dv_ref, p_s, sg_s):
        qi = pl.program_id(1)

        @pl.when(qi == 0)
        def _():
            dk_ref[...] = jnp.zeros(dk_ref.shape, F32)
            dv_ref[...] = jnp.zeros(dv_ref.shape, F32)

        h0 = lax.broadcasted_iota(jnp.int32, (1, LANES), 1) < SB_HEAD_DIM
        q2 = _stack_heads(q_ref[...] * scale, h0)
        do2 = _stack_heads(do_ref[...], h0)
        row = lax.broadcasted_iota(jnp.int32, (bk, bk), 0)
        col = lax.broadcasted_iota(jnp.int32, (bk, bk), 1)
        later_incl = (row >= col).astype(BF16)
        zero = jnp.zeros((2 * bq, 1), F32)
        tn = (((0,), (0,)), ((), ()))

        tile_rows = lambda j: pl.ds(pl.multiple_of(j * bk, bk), bk)

        def fetch1(j):
            rows = tile_rows(j)
            return (lax.dot_general(q2, k_ref[rows, :].astype(BF16), nt, preferred_element_type=F32),
                    lax.dot_general(do2, v_ref[rows, :].astype(BF16), nt, preferred_element_type=F32))

        def first1(mask, z_da):
            ls, cs, rsum = _sb_logs(z_da[0], mask)
            return ls, cs, rsum, (ls, z_da[1])

        def second1(j, sp, loga, ls, da):
            a = jnp.exp(loga)
            p = a * da
            p_s[j] = p
            sg_s[j] = jnp.exp(ls)
            dv_ref[tile_rows(j), :] += lax.dot_general(a.astype(BF16), do2, tn, preferred_element_type=F32)
            return sp + jnp.sum(p, axis=1, keepdims=True)
        total = _sb_sweep2(bq, bk, qi, fetch1, first1, second1, zero)

        def fetch2(j):
            return _dot01(p_s[j], later_incl)

        def visit2(j, mask, carry, later_p):
            dq, sp = carry
            rows = tile_rows(j)
            p, sg = p_s[j], sg_s[j]
            pref = total - sp - later_p
            dz = p * (1.0 - sg) - pref * sg
            if mask is not None:
                dz = jnp.where(mask, dz, 0.0)
            dz = dz.astype(BF16)
            dk_ref[rows, :] += lax.dot_general(dz, q2, tn, preferred_element_type=F32)
            dq = dq + jnp.dot(_side_by_side(dz, bq), _stack_heads(k_ref[rows, :], h0), preferred_element_type=F32)
            return dq, sp + jnp.sum(p, axis=1, keepdims=True)
        dq, _ = _sb_sweep(bq, bk, qi, fetch2, visit2, (jnp.zeros((bq, LANES), F32), zero))
        dq_ref[...] = dq * scale

    qs, ks, vs = _sb_specs(t, d, bq, 2 * d // LANES)
    shp = jax.ShapeDtypeStruct((t, d), F32)
    scratch = [pltpu.VMEM((t // bk, 2 * bq, bk), F32), pltpu.VMEM((t // bk, 2 * bq, bk), F32)]
    return pl.pallas_call(
        body, grid=(d // LANES, t // bq), in_specs=[qs, ks, vs, qs], out_specs=[qs, ks, ks],
        out_shape=[shp, shp, shp], scratch_shapes=scratch, name=name,
        compiler_params=_cparams(("parallel", "arbitrary")))(qn, kn, qkv, do)


def _adamw(w, g, m, v, name):
    shape = w.shape
    two_d = lambda a: a.reshape(-1, shape[-1])
    c1 = 1.0 - ADAM_B1 ** ADAM_STEP
    c2 = 1.0 - ADAM_B2 ** ADAM_STEP

    def body(r, _):
        wv, gv, mv, vv = r
        mn = ADAM_B1 * mv + (1.0 - ADAM_B1) * gv
        vn = ADAM_B2 * vv + (1.0 - ADAM_B2) * (gv * gv)
        delta = -ADAM_LR * ((mn / c1) / (jnp.sqrt(vn / c2) + ADAM_EPS) + ADAM_WD * wv)
        return [delta, mn, vn], []
    width = shape[-1]
    res = _rowwise(body, [two_d(w), two_d(g), two_d(m), two_d(v)], [], [(width, F32)] * 3, [], name=name)
    return [r.reshape(shape) for r in res]


def _dn_layer_fwd(h1, w, conv_w, a_log, dt_bias, onorm_g, tag):
    t, d = h1.shape
    heads = d // DN_HEAD_DIM
    n = t // DN_CHUNK
    proj = _mm(h1, w["all"], name=tag + "_proj")
    qn, kn, vv = _dn_prep_fwd(proj, conv_w, heads, tag + "_prep")
    big_g, beta = _dn_gate_fwd(proj, 4 * d // (2 * LANES), a_log, dt_bias, tag + "_gate")
    gt_ = big_g[:, :heads].T.reshape(heads, n, DN_CHUNK)
    gc, gr = gt_[..., None], gt_[:, :, None, :]
    bc = beta[:, :heads].T.reshape(heads, n, DN_CHUNK)[..., None]
    o, s_all = _dn_chunk_fwd(qn, kn, vv, gc, gr, bc, tag + "_chunk")
    on = _dn_post_fwd(o, proj, onorm_g, tag + "_post")
    y = _mm(on, w["out"], name=tag + "_out")
    return y, dict(proj=proj, qn=qn, kn=kn, v=vv, gc=gc, gr=gr, bc=bc, o=o, s_all=s_all, on=on)


def _dn_layer_bwd(dy, h1, w, conv_w, a_log, dt_bias, onorm_g, sv, tag):
    t, d = h1.shape
    heads = d // DN_HEAD_DIM
    don = _mm(dy, w["out"], tb=True, name=tag + "_dout")
    g_out = _mm(sv["on"], dy, ta=True, out_dtype=BF16, name=tag + "_gwout")
    do, dz, g_on = _dn_post_bwd(sv["o"], sv["proj"], don, onorm_g, tag + "_dpost")
    dq, dk, dv, dgc, dgr, dbc = _dn_chunk_bwd(sv["qn"], sv["kn"], sv["v"], sv["gc"], sv["gr"], sv["bc"], sv["s_all"],
                                              do, tag + "_dchunk")
    pad = lambda a: jnp.pad(a.reshape(heads, t).T, ((0, 0), (0, LANES - heads)))
    d_big_g = pad(dgc) + pad(dgr)
    dab, g_alog, g_dt = _dn_gate_bwd(sv["proj"], 4 * d // (2 * LANES), d_big_g, pad(dbc), a_log, dt_bias, tag + "_dgate")
    dxq, dxk, dxv, wq, wk, wv = _dn_prep_bwd(sv["proj"], conv_w, dq, dk, dv, heads, tag + "_dprep")
    dproj = jnp.concatenate([dxq, dxk, dxv, dz, dab], axis=1)
    dh1 = _mm(dproj, w["all"], tb=True, name=tag + "_dh")
    g_all = _mm(h1, dproj, ta=True, out_dtype=BF16, name=tag + "_gwin")
    grads = dict(w_all=g_all, w_out=g_out, conv_w=jnp.concatenate([wq, wk, wv], axis=1), a_log=g_alog, dt_bias=g_dt,
                 onorm_g=g_on)
    return dh1, grads


def _sb_layer_fwd(h1, w, q_g, k_g, tag):
    t, d = h1.shape
    heads = d // SB_HEAD_DIM
    qkv = _mm(h1, w["qkv"], name=tag + "_proj")
    gq, gk = jnp.tile(q_g, (1, heads)), jnp.tile(k_g, (1, heads))
    qn, kn = _sb_norm_fwd(qkv, gq, gk, tag + "_norm")
    o = _sb_fwd(qn, kn, qkv, tag + "_attn")
    y = _mm(o, w["out"], name=tag + "_out")
    return y, dict(qkv=qkv, qn=qn, kn=kn, o=o, gq=gq, gk=gk)


def _sb_layer_bwd(dy, h1, w, q_g, k_g, sv, tag):
    t, d = h1.shape
    heads = d // SB_HEAD_DIM
    do = _mm(dy, w["out"], tb=True, name=tag + "_dout")
    g_out = _mm(sv["o"], dy, ta=True, out_dtype=BF16, name=tag + "_gwout")
    dqn, dkn, dv = _sb_bwd(sv["qn"], sv["kn"], sv["qkv"], do, tag + "_dattn")
    dqkv, g_q, g_k = _sb_norm_bwd(sv["qkv"], dqn, dkn, dv, sv["gq"], sv["gk"], tag + "_dnorm")
    fold = lambda g: jnp.sum(g.reshape(heads, SB_HEAD_DIM), axis=0, keepdims=True)
    dh1 = _mm(dqkv, w["qkv"], tb=True, name=tag + "_dh")
    g_qkv = _mm(h1, dqkv, ta=True, out_dtype=BF16, name=tag + "_gwin")
    return dh1, dict(w_qkv=g_qkv, w_out=g_out, q_norm_g=fold(g_q), k_norm_g=fold(g_k))


def _local_step(x, tgt, mod, norm1_g, norm2_g, layer_weights, layer_grads):
    depth = mod.shape[0]
    d = x.shape[1]
    saved = []
    for i in range(depth):
        mix_w, ffn_w = layer_weights(i, x)
        mv = [mod[i:i + 1, j * d:(j + 1) * d] for j in range(N_MOD)]
        sh1, sc1, gt1, sh2, sc2, gt2 = mv
        tag = "l%d" % i
        h1 = _adaln_fwd(x, norm1_g[i:i + 1], sc1, sh1, tag + "_ln1")
        if i % 2 == 0:
            p = mix_w
            y, sv = _dn_layer_fwd(h1, p, p["conv_w"], p["a_log"], p["dt_bias"], p["onorm_g"], tag + "_dn")
        else:
            p = mix_w
            y, sv = _sb_layer_fwd(h1, p, p["q_g"], p["k_g"], tag + "_sb")
        x1 = _resid_fwd(x, y, gt1, tag + "_res1")
        h2 = _adaln_fwd(x1, norm2_g[i:i + 1], sc2, sh2, tag + "_ln2")
        u = _mm(h2, ffn_w["w_in"], name=tag + "_ffn_in")
        a = _swiglu_fwd(u, tag + "_swiglu")
        y2 = _mm(a, ffn_w["w_out"], name=tag + "_ffn_out")
        x2 = _resid_fwd(x1, y2, gt2, tag + "_res2")
        saved.append(dict(x0=x, h1=h1, y=y, mix=sv, x1=x1, h2=h2, u=u, a=a, y2=y2, mix_w=mix_w, ffn_w=ffn_w))
        x = x2

    loss, dx = _loss_fwd_bwd(x, tgt, "loss")

    dmod, dn1, dn2 = [None] * depth, [None] * depth, [None] * depth
    zero = jnp.zeros((), F32)
    for i in reversed(range(depth)):
        s = saved[i]
        mv = [mod[i:i + 1, j * d:(j + 1) * d] + zero for j in range(N_MOD)]
        sh1, sc1, gt1, sh2, sc2, gt2 = mv
        tag = "l%d" % i
        ffn_w, p = s["ffn_w"], s["mix_w"]
        dy2, dgt2 = _resid_bwd(dx, s["y2"], gt2, tag + "_dres2")
        da = _mm(dy2, ffn_w["w_out"], tb=True, name=tag + "_dffn_a")
        g_wout = _mm(s["a"], dy2, ta=True, out_dtype=BF16, name=tag + "_gffn_out")
        du = _swiglu_bwd(s["u"], da, tag + "_dswiglu")
        dh2 = _mm(du, ffn_w["w_in"], tb=True, name=tag + "_dffn_h")
        g_win = _mm(s["h2"], du, ta=True, out_dtype=BF16, name=tag + "_gffn_in")
        dx, dg2, dsc2, dsh2 = _adaln_bwd(s["x1"], dh2, dx, norm2_g[i:i + 1], sc2, tag + "_dln2")
        dy, dgt1 = _resid_bwd(dx, s["y"], gt1, tag + "_dres1")
        if i % 2 == 0:
            dh1, g_mix = _dn_layer_bwd(dy, s["h1"], p, p["conv_w"], p["a_log"], p["dt_bias"], p["onorm_g"], s["mix"],
                                       tag + "_dn")
        else:
            dh1, g_mix = _sb_layer_bwd(dy, s["h1"], p, p["q_g"], p["k_g"], s["mix"], tag + "_sb")
        dx, dg1, dsc1, dsh1 = _adaln_bwd(s["x0"], dh1, dx, norm1_g[i:i + 1], sc1, tag + "_dln1")
        dmod[i] = jnp.concatenate([dsh1, dsc1, dgt1, dsh2, dsc2, dgt2], axis=1)
        dn1[i], dn2[i] = dg1, dg2
        zero = layer_grads(i, dx, g_mix, dict(w_in=g_win, w_out=g_wout))
    return loss, dx, jnp.concatenate(dmod, axis=0), jnp.concatenate(dn1, axis=0), jnp.concatenate(dn2, axis=0)


def _axes():
    return lax.axis_index("x"), lax.axis_index("y"), lax.axis_index("c")


def _remote(src, dst, send_sem, recv_sem, dev):
    return pltpu.make_async_remote_copy(src_ref=src, dst_ref=dst, send_sem=send_sem, recv_sem=recv_sem,
                                        device_id=dev, device_id_type=MESH)


def _other_chips(x, y):
    return [(1 - x, y), (x, 1 - y), (1 - x, 1 - y)]


def _allgather_small(v, name):
    m, n = v.shape

    def body(x_ref, out_ref, send_sems, recv_sems, local_sem):
        x, y, c = _axes()
        me, sibling = (x, y, c), (x, y, 1 - c)
        chips = _other_chips(x, y)

        def rows(px, py, pc):
            return out_ref.at[pl.ds((4 * px + 2 * py + pc) * m, m), :]

        def copy(k, block, to, src=None):
            return _remote(rows(*block) if src is None else src, rows(*block), send_sems.at[k], recv_sems.at[k], to)

        mine = pltpu.make_async_copy(x_ref, rows(*me), local_sem)
        mine.start()
        first = [copy(0, me, sibling, src=x_ref)]
        first += [copy(1 + j, me, (*chip, c), src=x_ref) for j, chip in enumerate(chips)]
        for cp in first:
            cp.start()
        passed = [copy(4 + j, (*chip, c), sibling) for j, chip in enumerate(chips)]
        for j, chip in enumerate(chips):
            copy(1 + j, (*chip, c), me).wait_recv()
            passed[j].start()
        copy(0, sibling, me).wait_recv()
        for j, chip in enumerate(chips):
            copy(4 + j, (*chip, 1 - c), me).wait_recv()
        for cp in first + passed:
            cp.wait_send()
        mine.wait()

    return pl.pallas_call(
        body, out_shape=jax.ShapeDtypeStruct((8 * m, n), v.dtype),
        in_specs=[pl.BlockSpec(memory_space=pltpu.VMEM)], out_specs=pl.BlockSpec(memory_space=pltpu.VMEM),
        scratch_shapes=[pltpu.SemaphoreType.DMA((7,)), pltpu.SemaphoreType.DMA((7,)), pltpu.SemaphoreType.DMA],
        name=name, compiler_params=pltpu.CompilerParams(vmem_limit_bytes=VMEM_LIMIT))(v)


def _half(ref, h, rh):
    return ref.at[(slice(None),) * (len(ref.shape) - 2) + (pl.ds(h * rh, rh), slice(None))]


def _hbm_call(body, ins, out_shapes, n_sems, n_local, name):
    hbm = pl.BlockSpec(memory_space=pltpu.HBM)
    scratch = [pltpu.SemaphoreType.DMA((n_sems,)), pltpu.SemaphoreType.DMA((n_sems,))]
    if n_local:
        scratch.append(pltpu.SemaphoreType.DMA((n_local,)))
    return pl.pallas_call(body, out_shape=out_shapes, in_specs=[hbm] * len(ins), out_specs=[hbm] * len(out_shapes),
                          scratch_shapes=scratch, name=name)(*ins)


def _allgather_weights(shards, name):
    n = len(shards)

    def body(*refs):
        p_refs, out_refs, (send_sems, recv_sems, local_sems) = refs[:n], refs[n:2 * n], refs[2 * n:]
        x, y, c = _axes()
        sibling = (x, y, 1 - c)
        chips = _other_chips(x, y)
        rh = [s.shape[1] // 2 for s in shards]
        blk = lambda i, cx, cy, h: _half(out_refs[i].at[2 * cx + cy], h, rh[i])
        mine = [pltpu.make_async_copy(p_refs[i], out_refs[i].at[2 * x + y], local_sems.at[i]) for i in range(n)]
        for cp in mine:
            cp.start()
        first = [_remote(_half(p_refs[i], c, rh[i]), blk(i, x, y, c), send_sems.at[6 * i + j], recv_sems.at[6 * i + j],
                         (*chip, c)) for i in range(n) for j, chip in enumerate(chips)]
        for cp in first:
            cp.start()
        passed = []
        for j, chip in enumerate(chips):
            for i in range(n):
                _remote(_half(p_refs[i], c, rh[i]), blk(i, *chip, c), send_sems.at[6 * i + j], recv_sems.at[6 * i + j],
                        sibling).wait_recv()
                passed.append(_remote(blk(i, *chip, c), blk(i, *chip, c), send_sems.at[6 * i + 3 + j],
                                      recv_sems.at[6 * i + 3 + j], sibling))
                passed[-1].start()
        for j, chip in enumerate(chips):
            for i in range(n):
                _remote(_half(p_refs[i], c, rh[i]), blk(i, *chip, 1 - c), send_sems.at[6 * i + 3 + j],
                        recv_sems.at[6 * i + 3 + j], sibling).wait_recv()
        for cp in first + passed:
            cp.wait_send()
        for cp in mine:
            cp.wait()

    outs = [jax.ShapeDtypeStruct((4,) + s.shape, s.dtype) for s in shards]
    return _hbm_call(body, shards, outs, 6 * n, n, name)


def _sibling_swap_halves(arrs, name):
    n = len(arrs)

    def body(*refs):
        v_refs, out_refs, (send_sems, recv_sems) = refs[:n], refs[n:2 * n], refs[2 * n:]
        x, y, c = _axes()
        cps = [_remote(_half(v_refs[i], 1 - c, arrs[i].shape[-2] // 2), out_refs[i], send_sems.at[i], recv_sems.at[i],
                       (x, y, 1 - c)) for i in range(n)]
        for cp in cps:
            cp.start()
        for cp in cps:
            cp.wait()

    outs = [jax.ShapeDtypeStruct(a.shape[:-2] + (a.shape[-2] // 2, a.shape[-1]), a.dtype) for a in arrs]
    return _hbm_call(body, arrs, outs, n, 0, name)


def _chip_scatter(arrs, name):
    n = len(arrs)

    def body(*refs):
        v_refs, out_refs, (send_sems, recv_sems, local_sems) = refs[:n], refs[n:2 * n], refs[2 * n:]
        x, y, c = _axes()
        me = 2 * x + y
        chips = _other_chips(x, y)
        mine = [pltpu.make_async_copy(v_refs[i].at[me], out_refs[i].at[me], local_sems.at[i]) for i in range(n)]
        for cp in mine:
            cp.start()
        sends = [_remote(v_refs[i].at[2 * cx + cy], out_refs[i].at[me], send_sems.at[3 * i + j], recv_sems.at[3 * i + j],
                         (cx, cy, c)) for i in range(n) for j, (cx, cy) in enumerate(chips)]
        for cp in sends:
            cp.start()
        for i in range(n):
            for j, (cx, cy) in enumerate(chips):
                _remote(v_refs[i].at[me], out_refs[i].at[2 * cx + cy], send_sems.at[3 * i + j], recv_sems.at[3 * i + j],
                        (cx, cy, c)).wait_recv()
        for cp in sends:
            cp.wait_send()
        for cp in mine:
            cp.wait()

    outs = [jax.ShapeDtypeStruct(a.shape, a.dtype) for a in arrs]
    return _hbm_call(body, arrs, outs, 3 * n, n, name)


def _ici_src(ref, mode, slot, c):
    return _half(ref, c, ref.shape[-2] // 2) if mode == "half" else ref.at[slot]


def _ici_start(srcs, mode, name):
    n, ncp = len(srcs), 3 * len(srcs)
    lands = [jnp.zeros(((4,) + s.shape[:-2] + (s.shape[-2] // 2, s.shape[-1])) if mode == "half" else s.shape, s.dtype)
             for s in srcs]

    def body(*refs):
        src_refs, land_refs = refs[:n], refs[n:2 * n]
        send_sems, recv_sems = refs[2 * n:2 * n + ncp], refs[2 * n + ncp:2 * n + 2 * ncp]
        token = refs[-1]
        x, y, c = _axes()
        for i in range(n):
            for j, (cx, cy) in enumerate(_other_chips(x, y)):
                _remote(_ici_src(src_refs[i], mode, 2 * cx + cy, c), land_refs[i].at[2 * x + y], send_sems[3 * i + j],
                        recv_sems[3 * i + j], (cx, cy, c)).start()
        token[...] = jnp.zeros(token.shape, token.dtype)

    hbm, sem = pl.BlockSpec(memory_space=pltpu.HBM), pl.BlockSpec(memory_space=pltpu.SEMAPHORE)
    bufs = srcs + lands
    out_shape = tuple([pltpu.SemaphoreType.DMA(())] * (2 * ncp) + [pltpu.HBM(b.shape, b.dtype) for b in bufs]
                      + [jax.ShapeDtypeStruct((8, LANES), F32)])
    res = pl.pallas_call(
        body, name=name, out_shape=out_shape, in_specs=(hbm,) * (2 * n),
        out_specs=(sem,) * (2 * ncp) + (hbm,) * (2 * n) + (pl.BlockSpec(memory_space=pltpu.VMEM),),
        input_output_aliases={i: 2 * ncp + i for i in range(2 * n)},
        compiler_params=pltpu.CompilerParams(has_side_effects=pltpu.SideEffectType.DATAFLOW_SIDE_EFFECTING),
    )(*[pltpu.with_memory_space_constraint(b, pltpu.HBM) for b in bufs])
    return list(res[:2 * ncp]), list(res[2 * ncp:2 * ncp + 2 * n]), res[-1][0, 0]


def _ici_wait(sems, bufs, mode, after, name):
    n, ncp = len(bufs) // 2, len(sems) // 2

    def body(*refs):
        src_refs, land_refs = refs[:n], refs[n:2 * n]
        send_sems, recv_sems = refs[2 * n:2 * n + ncp], refs[2 * n + ncp:2 * n + 2 * ncp]
        x, y, c = _axes()
        for i in range(n):
            for j, (cx, cy) in enumerate(_other_chips(x, y)):
                cp = _remote(_ici_src(src_refs[i], mode, 2 * cx + cy, c), land_refs[i].at[2 * cx + cy],
                             send_sems[3 * i + j], recv_sems[3 * i + j], (cx, cy, c))
                cp.wait_send()
                cp.wait_recv()

    hbm, sem = pl.BlockSpec(memory_space=pltpu.HBM), pl.BlockSpec(memory_space=pltpu.SEMAPHORE)
    res = pl.pallas_call(
        body, name=name, out_shape=tuple(pltpu.HBM(b.shape, b.dtype) for b in bufs),
        in_specs=(hbm,) * (2 * n) + (sem,) * (2 * ncp) + (pl.BlockSpec(memory_space=pl.ANY),),
        out_specs=(hbm,) * (2 * n), input_output_aliases={i: i for i in range(2 * n)},
        compiler_params=pltpu.CompilerParams(has_side_effects=pltpu.SideEffectType.DATAFLOW_SIDE_EFFECTING),
    )(*bufs, *sems, after)
    return list(res[n:])


def _share_halves(arrs, name):
    n = len(arrs)

    def body(*refs):
        v_refs, out_refs, (send_sems, recv_sems) = refs[:n], refs[n:2 * n], refs[2 * n:]
        x, y, c = _axes()
        cps = [_remote(v_refs[i], out_refs[i], send_sems.at[i], recv_sems.at[i], (x, y, 1 - c)) for i in range(n)]
        for cp in cps:
            cp.start()
        for cp in cps:
            cp.wait()

    outs = [jax.ShapeDtypeStruct(a.shape, a.dtype) for a in arrs]
    return _hbm_call(body, arrs, outs, n, 0, name)


def _sum_lead(v, name):
    k, r, w = v.shape
    tr = _pick_tile(r, (256, 128, 64, 32, 16))

    def body(v_ref, o_ref):
        acc = v_ref[0].astype(F32)
        for i in range(1, k):
            acc = acc + v_ref[i].astype(F32)
        o_ref[...] = acc

    return pl.pallas_call(
        body, grid=(r // tr,), in_specs=[pl.BlockSpec((k, tr, w), lambda i: (0, i, 0))],
        out_specs=pl.BlockSpec((tr, w), lambda i: (i, 0)), out_shape=jax.ShapeDtypeStruct((r, w), F32), name=name,
        compiler_params=_cparams(("parallel",)))(v)


_BIG = (("dn_w_in", 2), ("dn_w_out", 1), ("sb_w_qkv", 2), ("sb_w_out", 1), ("ffn_w_in", 2), ("ffn_w_out", 1))


def _to_chip_major(a, axis, nchips):
    l, r, c = a.shape
    if axis == 2:
        return a.reshape(l, r, nchips, c // nchips).transpose(2, 0, 1, 3)
    return a.reshape(l, nchips, r // nchips, c).transpose(1, 0, 2, 3)


def _from_chip_major(a, axis):
    n, l, r, c = a.shape
    if axis == 2:
        return a.transpose(1, 2, 0, 3).reshape(l, r, n * c)
    return a.transpose(1, 0, 2, 3).reshape(l, n * r, c)


def kernel(x, c, ada_w, ada_b, norm1_g, norm2_g, dn_w_in, dn_conv_w, dn_a_log, dn_dt_bias, dn_onorm_g, dn_w_out, sb_w_qkv, sb_q_norm_g, sb_k_norm_g, sb_w_out, ffn_w_in, ffn_w_out, loss_target, m_ada_w, m_ada_b, m_norm1_g, m_norm2_g, m_dn_w_in, m_dn_conv_w, m_dn_a_log, m_dn_dt_bias, m_dn_onorm_g, m_dn_w_out, m_sb_w_qkv, m_sb_q_norm_g, m_sb_k_norm_g, m_sb_w_out, m_ffn_w_in, m_ffn_w_out, v_ada_w, v_ada_b, v_norm1_g, v_norm2_g, v_dn_w_in, v_dn_conv_w, v_dn_a_log, v_dn_dt_bias, v_dn_onorm_g, v_dn_w_out, v_sb_w_qkv, v_sb_q_norm_g, v_sb_k_norm_g, v_sb_w_out, v_ffn_w_in, v_ffn_w_out):
    names = ("ada_w", "ada_b", "norm1_g", "norm2_g", "dn_w_in", "dn_conv_w", "dn_a_log", "dn_dt_bias", "dn_onorm_g",
             "dn_w_out", "sb_w_qkv", "sb_q_norm_g", "sb_k_norm_g", "sb_w_out", "ffn_w_in", "ffn_w_out")
    w = dict(zip(names, (ada_w, ada_b, norm1_g, norm2_g, dn_w_in, dn_conv_w, dn_a_log, dn_dt_bias, dn_onorm_g,
                         dn_w_out, sb_w_qkv, sb_q_norm_g, sb_k_norm_g, sb_w_out, ffn_w_in, ffn_w_out)))
    mom = dict(zip(names, (m_ada_w, m_ada_b, m_norm1_g, m_norm2_g, m_dn_w_in, m_dn_conv_w, m_dn_a_log, m_dn_dt_bias,
                           m_dn_onorm_g, m_dn_w_out, m_sb_w_qkv, m_sb_q_norm_g, m_sb_k_norm_g, m_sb_w_out, m_ffn_w_in,
                           m_ffn_w_out)))
    var = dict(zip(names, (v_ada_w, v_ada_b, v_norm1_g, v_norm2_g, v_dn_w_in, v_dn_conv_w, v_dn_a_log, v_dn_dt_bias,
                           v_dn_onorm_g, v_dn_w_out, v_sb_w_qkv, v_sb_q_norm_g, v_sb_k_norm_g, v_sb_w_out, v_ffn_w_in,
                           v_ffn_w_out)))
    ax, ay, ac = _axes()
    chip = 2 * ax + ay
    dev = 2 * chip + ac
    t, d = x.shape[1], x.shape[2]
    depth, ndn, nsb = ada_w.shape[0], dn_w_in.shape[0], sb_w_qkv.shape[0]
    heads = d // DN_HEAD_DIM
    mod_cols = ada_w.shape[2]
    conv_cols = dn_conv_w.shape[2]
    nchips, ndev = 4, 8

    conv_rows = ndn * DN_CONV * conv_cols // d
    pay1 = jnp.concatenate([c, dn_conv_w.reshape(conv_rows, d), jnp.zeros((8 - 1 - conv_rows, d), F32)], axis=0)
    g1 = _allgather_small(pay1, "ag_cond").reshape(ndev, 8, d)
    c_all = g1[:, 0]
    conv_full = g1[::2, 1:1 + conv_rows].reshape(nchips, ndn, DN_CONV, conv_cols).transpose(1, 2, 0, 3)
    conv_full = conv_full.reshape(ndn, DN_CONV, nchips * conv_cols)

    c16 = jnp.pad(c_all, ((0, 16 - ndev), (0, 0)))
    cond16 = _rowwise(lambda r, v: ([_silu(r[0])], []), [c16], [], [(d, F32)], [], name="cond_silu")[0]
    pay2 = jnp.concatenate([_mm(cond16, ada_w[i], name="ada_mod%d" % i)[:ndev] for i in range(depth)], axis=0)
    g2 = _allgather_small(pay2, "ag_mod").reshape(ndev, depth, ndev, mod_cols)[::2]
    mod_raw = lax.dynamic_index_in_dim(g2, dev, axis=2, keepdims=False)
    mod_raw = mod_raw.transpose(1, 0, 2).reshape(depth, nchips * mod_cols)
    mod = _rowwise(lambda r, v: ([r[0] + r[1]], []), [mod_raw, ada_b], [], [(nchips * mod_cols, F32)], [],
                   name="ada_bias")[0]

    axis_of = dict(_BIG)
    groups = [[0], list(range(1, depth))]

    def kinds_of(layers):
        out = []
        for n, _ in _BIG:
            idx = [l // 2 for l in layers if l % 2 == (0 if n.startswith("dn_") else 1)] if n[:3] in ("dn_", "sb_") \
                else list(layers)
            if idx:
                out.append((n, min(idx), max(idx) + 1))
        return out

    two_d = lambda a: a.reshape(-1, a.shape[-1])
    def both_halves(mine, other, ax):
        rh = mine.shape[ax]
        buf = lax.empty(mine.shape[:ax] + (2 * rh,) + mine.shape[ax + 1:], mine.dtype)
        buf = lax.dynamic_update_slice_in_dim(buf, mine, ac * rh, axis=ax)
        return lax.dynamic_update_slice_in_dim(buf, other, (1 - ac) * rh, axis=ax)
    w16 = {n: w[n].astype(BF16) for n, _ in _BIG}
    gathers = []
    for gi, layers in enumerate(groups):
        sl = kinds_of(layers)
        shards = [w16[n][lo:hi] for n, lo, hi in sl]
        sems, bufs, zero = _ici_start(shards, "half", "ag_start%d" % gi)
        gathers.append((sl, shards, sems, bufs))
        mod = mod + zero
    full = {}

    def finish_gather(gi, after):
        sl, shards, sems, bufs = gathers[gi]
        lands = _ici_wait(sems, bufs, "half", after, "ag_wait%d" % gi)
        for (n, lo, hi), shard, mine, other in zip(sl, shards, lands, _share_halves(lands, "ag_pair%d" % gi)):
            a = lax.dynamic_update_index_in_dim(both_halves(mine, other, 2), shard, chip, 0)
            a = _from_chip_major(a, axis_of[n])
            for l in range(lo, hi):
                full[n, l] = a[l - lo]

    padl = lambda v: jnp.pad(v[None, :], ((0, 0), (0, LANES - v.shape[0])))
    padc = lambda a: jnp.pad(a, ((0, 0), (0, LANES - a.shape[1])))

    def layer_weights(i, after):
        for gi, layers in enumerate(groups):
            if layers and i == layers[0]:
                finish_gather(gi, after)
        j = i // 2
        if i % 2 == 0:
            wi = full["dn_w_in", j]
            w_all = jnp.concatenate([wi[:, :4 * d], padc(wi[:, 4 * d:4 * d + heads]), padc(wi[:, 4 * d + heads:])],
                                    axis=1)
            mix = dict(all=w_all, out=full["dn_w_out", j], conv_w=conv_full[j], a_log=padl(dn_a_log[j]),
                       dt_bias=padl(dn_dt_bias[j]), onorm_g=dn_onorm_g[j][None])
        else:
            mix = dict(qkv=full["sb_w_qkv", j], out=full["sb_w_out", j], q_g=sb_q_norm_g[j][None],
                       k_g=sb_k_norm_g[j][None])
        return mix, dict(w_in=full["ffn_w_in", i], w_out=full["ffn_w_out", i])

    g_dn, g_sb, g_ffn = [None] * ndn, [None] * nsb, [None] * depth
    reduces = {}

    def gw_in(g):
        ga = g["w_all"]
        return jnp.concatenate([ga[:, :4 * d], ga[:, 4 * d:4 * d + heads], ga[:, 4 * d + LANES:4 * d + LANES + heads]],
                               axis=1)
    layer_grad = dict(dn_w_in=lambda j: gw_in(g_dn[j]), dn_w_out=lambda j: g_dn[j]["w_out"],
                      sb_w_qkv=lambda j: g_sb[j]["w_qkv"], sb_w_out=lambda j: g_sb[j]["w_out"],
                      ffn_w_in=lambda l: g_ffn[l]["w_in"], ffn_w_out=lambda l: g_ffn[l]["w_out"])

    def layer_grads(i, after, g_mix, g_f):
        (g_dn if i % 2 == 0 else g_sb)[i // 2], g_ffn[i] = g_mix, g_f
        zero = jnp.zeros((), F32)
        for gi, layers in enumerate(groups):
            if layers and i == layers[0]:
                sl = kinds_of(layers)
                parts = [_to_chip_major(jnp.stack([layer_grad[n](j) for j in range(lo, hi)]), axis_of[n], nchips)
                         for n, lo, hi in sl]
                pairs = []
                for (n, _, _), p, fs in zip(sl, parts, _sibling_swap_halves(parts, "gr_pair%d" % gi)):
                    rh = fs.shape[2]
                    own = lax.dynamic_slice_in_dim(p, ac * rh, rh, axis=2)
                    pairs.append(_rowwise(lambda r, v: ([r[0].astype(F32) + r[1].astype(F32)], []),
                                          [two_d(own), two_d(fs)], [], [(fs.shape[-1], BF16)], [],
                                          name="gr_pair_add%d_%s" % (gi, n))[0].reshape(fs.shape))
                sems, bufs, zero = _ici_start(pairs, "slot", "gr_start%d" % gi)
                reduces[gi] = (sl, pairs, sems, bufs)
        return zero

    loss_local, grad_x, dmod, g_n1, g_n2 = _local_step(x[0], loss_target[0], mod, norm1_g, norm2_g, layer_weights,
                                                       layer_grads)
    loss = lax.psum(loss_local, ("x", "y", "c"))

    g_conv = jnp.stack([g["conv_w"] for g in g_dn])
    misc = jnp.concatenate([jnp.concatenate([g["onorm_g"] for g in g_dn], axis=1),
                            jnp.concatenate([g["a_log"] for g in g_dn], axis=1),
                            jnp.concatenate([g["dt_bias"] for g in g_dn], axis=1),
                            jnp.concatenate([g["q_norm_g"] for g in g_sb], axis=1),
                            jnp.concatenate([g["k_norm_g"] for g in g_sb], axis=1)], axis=1)
    misc = jnp.pad(misc, ((0, 0), (0, -misc.shape[1] % d))).reshape(-1, d)
    small = [dmod.reshape(-1, d), g_n1, g_n2, g_conv.reshape(-1, d), misc]
    small_rows = [s.shape[0] for s in small]
    pad_rows = -sum(small_rows) % 8
    pay3 = jnp.concatenate(small + [jnp.zeros((pad_rows, d), F32)], axis=0)
    nrow3 = pay3.shape[0]
    g3 = _allgather_small(pay3, "ag_small")
    summed = _rowwise(lambda r, v: ([], [_colsum(r[0])]), [g3.reshape(ndev, nrow3 * d)], [], [], [(1, nrow3 * d)],
                      name="small_sum")[0].reshape(nrow3, d)
    offs = [0]
    for n_ in small_rows:
        offs.append(offs[-1] + n_)
    grads = {}
    grads["ada_b"] = summed[offs[0]:offs[1]].reshape(depth, N_MOD * d)
    grads["norm1_g"] = summed[offs[1]:offs[2]]
    grads["norm2_g"] = summed[offs[2]:offs[3]]
    conv_sum = summed[offs[3]:offs[4]].reshape(ndn, DN_CONV, nchips * conv_cols)
    grads["dn_conv_w"] = lax.dynamic_slice_in_dim(conv_sum, chip * conv_cols, conv_cols, axis=2)
    mrow = summed[offs[4]:offs[5]].reshape(-1)
    o = 0
    grads["dn_onorm_g"] = mrow[o:o + ndn * DN_HEAD_DIM].reshape(ndn, DN_HEAD_DIM)
    o += ndn * DN_HEAD_DIM
    grads["dn_a_log"] = mrow[o:o + ndn * LANES].reshape(ndn, LANES)[:, :heads]
    o += ndn * LANES
    grads["dn_dt_bias"] = mrow[o:o + ndn * LANES].reshape(ndn, LANES)[:, :heads]
    o += ndn * LANES
    grads["sb_q_norm_g"] = mrow[o:o + nsb * SB_HEAD_DIM].reshape(nsb, SB_HEAD_DIM)
    o += nsb * SB_HEAD_DIM
    grads["sb_k_norm_g"] = mrow[o:o + nsb * SB_HEAD_DIM].reshape(nsb, SB_HEAD_DIM)
    dmod_all = g3.reshape(ndev, nrow3, d)[:, :small_rows[0]].reshape(ndev, depth, N_MOD * d)
    dmod_mine = lax.dynamic_slice_in_dim(dmod_all, chip * mod_cols, mod_cols, axis=2)
    dmod16 = jnp.pad(dmod_mine, ((0, 16 - ndev), (0, 0), (0, 0)))
    grads["ada_w"] = jnp.stack([_mm(cond16, dmod16[:, i], ta=True, name="ada_gw%d" % i) for i in range(depth)])

    pieces = {n: [] for n, _ in _BIG}
    for gi in reversed(range(len(groups))):
        if gi not in reduces:
            continue
        sl, pairs, sems, bufs = reduces[gi]
        lands = _ici_wait(sems, bufs, "slot", grad_x, "gr_wait%d" % gi)
        reduced = []
        for (n, _, _), land, pair in zip(sl, lands, pairs):
            own = lax.dynamic_index_in_dim(pair, chip, 0, keepdims=True)
            land = lax.dynamic_update_slice_in_dim(land, own, chip, axis=0)
            reduced.append(_sum_lead(land.reshape(nchips, -1, land.shape[-1]), "gr_chip_add%d_%s" % (gi, n)))
        for (n, lo, hi), mine, other in zip(sl, reduced, _share_halves(reduced, "gr_share%d" % gi)):
            shape = (hi - lo, -1, mine.shape[-1])
            pieces[n].append((lo, both_halves(mine.reshape(shape), other.reshape(shape), 1)))
    for n, _ in _BIG:
        grads[n] = jnp.concatenate([p for _, p in sorted(pieces[n], key=lambda t: t[0])], axis=0)

    delta, new_m, new_v = {}, {}, {}
    for n in names:
        delta[n], new_m[n], new_v[n] = _adamw(w[n], grads[n], mom[n], var[n], "adamw_" + n)
    return (loss, grad_x[None], *[grads[n] for n in names], *[delta[n] for n in names], *[new_m[n] for n in names],
            *[new_v[n] for n in names])
```

```python
import functools

import jax
import jax.numpy as jnp
from jax import lax
from jax.experimental import pallas as pl
from jax.experimental.pallas import tpu as pltpu

F32 = jnp.float32
BF16 = jnp.bfloat16
HI = lax.Precision.HIGHEST
MESH = pl.DeviceIdType.MESH

EPS = 1e-6
N_MOD = 6
DN_HEAD_DIM = 128
DN_CONV = 4
DN_CHUNK = 64
DN_GROUP = 8
DN_GATE_ROWS = 256
SB_HEAD_DIM = 64
SB_BLOCK = 128
SB_QBLOCK = 512
SB_PART = 128
LANES = 128
VMEM_LIMIT = 56 * 1024 * 1024
MM_BLOCK_BUDGET = 36 * 1024 * 1024
ROW_BLOCK_BUDGET = 20 * 1024 * 1024

ADAM_LR = 0.001
ADAM_B1 = 0.9
ADAM_B2 = 0.999
ADAM_EPS = 1e-08
ADAM_WD = 0.01
ADAM_STEP = 10


def _cparams(sem=None):
    return pltpu.CompilerParams(dimension_semantics=sem, vmem_limit_bytes=VMEM_LIMIT)


def _sigmoid(x):
    return 1.0 / (1.0 + jnp.exp(-x))


def _silu(x):
    return x * _sigmoid(x)


def _dsilu(x):
    s = _sigmoid(x)
    return s * (1.0 + x * (1.0 - s))


def _softplus(x):
    return jnp.maximum(x, 0.0) + jnp.log1p(jnp.exp(-jnp.abs(x)))


def _pick_tile(n, prefs):
    for t in prefs:
        if n % t == 0:
            return t
    return n


def _mm(a, b, *, ta=False, tb=False, out_dtype=F32, name):
    m = a.shape[1] if ta else a.shape[0]
    k = a.shape[0] if ta else a.shape[1]
    n = b.shape[0] if tb else b.shape[1]
    assert (b.shape[1] if tb else b.shape[0]) == k
    size = lambda dt: jnp.dtype(dt).itemsize
    best = None
    for tm in sorted({t for t in (m, 2048, 1024, 512, 256, 128) if m % t == 0 and (t % 128 == 0 or t == m)}):
        for tn in sorted({t for t in (n, 2816, 1408, 1024, 768, 512, 256, 128) if n % t == 0 and (t % 128 == 0 or t == n)}):
            need = 2 * (tm * k * size(a.dtype) + tn * k * size(b.dtype) + tm * tn * size(out_dtype))
            if need <= MM_BLOCK_BUDGET and (best is None or tm * tn > best[0] * best[1]):
                best = (tm, tn)
    tm, tn = best
    dims = (((0 if ta else 1,), (1 if tb else 0,)), ((), ()))

    def body(a_ref, b_ref, o_ref):
        av = a_ref[...].astype(BF16)
        bv = b_ref[...].astype(BF16)
        o_ref[...] = lax.dot_general(av, bv, dims, preferred_element_type=F32).astype(o_ref.dtype)

    a_spec = pl.BlockSpec((k, tm), lambda i, j: (0, i)) if ta else pl.BlockSpec((tm, k), lambda i, j: (i, 0))
    b_spec = pl.BlockSpec((tn, k), lambda i, j: (j, 0)) if tb else pl.BlockSpec((k, tn), lambda i, j: (0, j))
    return pl.pallas_call(
        body, grid=(m // tm, n // tn), in_specs=[a_spec, b_spec],
        out_specs=pl.BlockSpec((tm, tn), lambda i, j: (i, j)),
        out_shape=jax.ShapeDtypeStruct((m, n), out_dtype), name=name,
        compiler_params=_cparams(("parallel", "parallel")))(a, b)


def _rowwise(body, rows, vecs, outs, accs, *, name, tr=None):
    rows = [r if isinstance(r, tuple) else (r, r.shape[1], 0) for r in rows]
    nrows = rows[0][0].shape[0]
    if tr is None:
        per_row = sum(w * jnp.dtype(a.dtype).itemsize for a, w, _ in rows) + sum(
            w * jnp.dtype(dt).itemsize for w, dt in outs)
        tr = next((t for t in (2048, 1024, 512, 256) if nrows % t == 0 and 2 * t * per_row <= ROW_BLOCK_BUDGET), 256)
    tr = tr if nrows % tr == 0 else nrows
    nr, nv, no = len(rows), len(vecs), len(outs)

    def kern(*refs):
        r_in, v_in = refs[:nr], refs[nr:nr + nv]
        o_refs, a_refs = refs[nr + nv:nr + nv + no], refs[nr + nv + no:]
        res_o, res_a = body([r[...] for r in r_in], [v[...] for v in v_in])
        for o, val in zip(o_refs, res_o):
            o[...] = val.astype(o.dtype)
        if a_refs:
            @pl.when(pl.program_id(0) == 0)
            def _():
                for a in a_refs:
                    a[...] = jnp.zeros(a.shape, a.dtype)
            for a, val in zip(a_refs, res_a):
                a[...] += val

    in_specs = [pl.BlockSpec((tr, w), functools.partial(lambda i, cb: (i, cb), cb=cb)) for _, w, cb in rows]
    in_specs += [pl.BlockSpec(v.shape, functools.partial(lambda i, nd: (0,) * nd, nd=v.ndim)) for v in vecs]
    out_specs = [pl.BlockSpec((tr, w), lambda i: (i, 0)) for w, _ in outs]
    out_specs += [pl.BlockSpec(s, lambda i: (0, 0)) for s in accs]
    out_shape = [jax.ShapeDtypeStruct((nrows, w), dt) for w, dt in outs]
    out_shape += [jax.ShapeDtypeStruct(s, F32) for s in accs]
    res = pl.pallas_call(
        kern, grid=(nrows // tr,), in_specs=in_specs, out_specs=out_specs, out_shape=out_shape, name=name,
        compiler_params=_cparams(("arbitrary",) if accs else ("parallel",)))(*[r[0] for r in rows], *vecs)
    return res


def _colsum(v):
    return jnp.sum(v, axis=0, keepdims=True)


def _adaln_fwd(x, g, sc, sh, name):
    def body(r, v):
        (xv,), (gv, scv, shv) = r, v
        rs = lax.rsqrt(jnp.mean(xv * xv, axis=-1, keepdims=True) + EPS)
        return [(xv * rs) * gv * (1.0 + scv) + shv], []
    return _rowwise(body, [x], [g, sc, sh], [(x.shape[1], BF16)], [], name=name)[0]


def _adaln_bwd(x, dh, dxr, g, sc, name):
    d = x.shape[1]

    def body(r, v):
        (xv, dhv, dxv), (gv, scv) = r, v
        rs = lax.rsqrt(jnp.mean(xv * xv, axis=-1, keepdims=True) + EPS)
        nv = xv * rs
        dn = dhv * (gv * (1.0 + scv))
        dx = rs * (dn - nv * jnp.mean(dn * nv, axis=-1, keepdims=True)) + dxv
        dhn = dhv * nv
        return [dx], [_colsum(dhn * (1.0 + scv)), _colsum(dhn * gv), _colsum(dhv)]
    return _rowwise(body, [x, dh, dxr], [g, sc], [(d, F32)], [(1, d)] * 3, name=name)


def _resid_fwd(x, y, gt, name):
    def body(r, v):
        return [r[0] + v[0] * r[1]], []
    return _rowwise(body, [x, y], [gt], [(x.shape[1], F32)], [], name=name)[0]


def _resid_bwd(dx, y, gt, name):
    d = dx.shape[1]

    def body(r, v):
        return [v[0] * r[0]], [_colsum(r[0] * r[1])]
    return _rowwise(body, [dx, y], [gt], [(d, BF16)], [(1, d)], name=name)


def _swiglu_fwd(u, name):
    f = u.shape[1] // 2

    def body(r, v):
        uv = r[0]
        return [_silu(uv[:, :f]) * uv[:, f:]], []
    return _rowwise(body, [u], [], [(f, BF16)], [], name=name)[0]


def _swiglu_bwd(u, da, name):
    f = u.shape[1] // 2

    def body(r, v):
        uv, dav = r
        gate, up = uv[:, :f], uv[:, f:]
        return [jnp.concatenate([dav * up * _dsilu(gate), dav * _silu(gate)], axis=1)], []
    return _rowwise(body, [u, da], [], [(2 * f, BF16)], [], name=name)[0]


def _loss_fwd_bwd(y, tgt, name):
    d = y.shape[1]

    def body(r, v):
        err = r[0] - r[1]
        part = jnp.sum(jnp.sum(err * err, axis=1, keepdims=True), axis=0, keepdims=True) * (0.5 / d)
        return [err * (1.0 / d)], [jnp.broadcast_to(part, (1, LANES))]
    dy, acc = _rowwise(body, [y, tgt], [], [(d, F32)], [(1, LANES)], name=name)
    return acc[0, 0], dy


def _head_means(v):
    row = lax.broadcasted_iota(jnp.int32, (LANES, LANES), 0)
    col = lax.broadcasted_iota(jnp.int32, (LANES, LANES), 1)
    same = ((row // SB_HEAD_DIM) == (col // SB_HEAD_DIM)).astype(F32)
    parts = [jnp.dot(v[:, g * LANES:(g + 1) * LANES], same, precision=HI, preferred_element_type=F32)
             for g in range(v.shape[1] // LANES)]
    return jnp.concatenate(parts, axis=1) * (1.0 / SB_HEAD_DIM)


def _sb_norm_fwd(qkv, gq, gk, name):
    d = qkv.shape[1] // 3

    def body(r, v):
        return [x * lax.rsqrt(_head_means(x * x) + EPS) * g for x, g in zip(r, v)], []
    return _rowwise(body, [(qkv, d, 0), (qkv, d, 1)], [gq, gk], [(d, F32), (d, F32)], [], name=name)


def _sb_norm_bwd(qkv, dqn, dkn, dv, gq, gk, name):
    d = qkv.shape[1] // 3

    def body(r, v):
        outs, accs = [], []
        for x, dy, g in ((r[0], r[2], v[0]), (r[1], r[3], v[1])):
            rs = lax.rsqrt(_head_means(x * x) + EPS)
            nv = x * rs
            dn = dy * g
            outs.append(rs * (dn - nv * _head_means(dn * nv)))
            accs.append(_colsum(dy * nv))
        return [jnp.concatenate(outs + [r[4]], axis=1)], accs
    return _rowwise(body, [(qkv, d, 0), (qkv, d, 1), dqn, dkn, dv], [gq, gk], [(3 * d, BF16)], [(1, d), (1, d)],
                    name=name)


def _head_tiles(a):
    return [a[:, h * DN_HEAD_DIM:(h + 1) * DN_HEAD_DIM] for h in range(a.shape[1] // DN_HEAD_DIM)]


def _dn_post_fwd(o, proj, g, name):
    d = o.shape[1]

    def body(r, v):
        outs = []
        for ov, zv in zip(_head_tiles(r[0]), _head_tiles(r[1])):
            rs = lax.rsqrt(jnp.mean(ov * ov, axis=-1, keepdims=True) + EPS)
            outs.append(ov * rs * v[0] * _silu(zv))
        return [jnp.concatenate(outs, axis=1)], []
    return _rowwise(body, [o, (proj, d, 3)], [g], [(d, BF16)], [], name=name)[0]


def _dn_post_bwd(o, proj, don, g, name):
    d = o.shape[1]

    def body(r, v):
        dos, dzs, dg = [], [], jnp.zeros((1, DN_HEAD_DIM), F32)
        for ov, zv, dv in zip(_head_tiles(r[0]), _head_tiles(r[1]), _head_tiles(r[2])):
            rs = lax.rsqrt(jnp.mean(ov * ov, axis=-1, keepdims=True) + EPS)
            nv = ov * rs
            s = _silu(zv)
            dn = dv * v[0] * s
            dos.append(rs * (dn - nv * jnp.mean(dn * nv, axis=-1, keepdims=True)))
            dzs.append(dv * nv * v[0] * _dsilu(zv))
            dg = dg + _colsum(dv * nv * s)
        return [jnp.concatenate(dos, axis=1), jnp.concatenate(dzs, axis=1)], [dg]
    return _rowwise(body, [o, (proj, d, 3), don], [g], [(d, F32), (d, BF16)], [(1, DN_HEAD_DIM)], name=name)


def _chunk_tri(tr, upper):
    row = lax.broadcasted_iota(jnp.int32, (tr, tr), 0)
    col = lax.broadcasted_iota(jnp.int32, (tr, tr), 1)
    same = (row // DN_CHUNK) == (col // DN_CHUNK)
    return (same & ((row <= col) if upper else (row >= col))).astype(F32)


def _dn_gate_fwd(proj, colblk, a_log, dt_bias, name):
    def body(r, v):
        ab = r[0]
        a, b = ab[:, :LANES], ab[:, LANES:]
        g = -jnp.exp(v[0]) * _softplus(a + v[1])
        big_g = jnp.dot(_chunk_tri(g.shape[0], False), g, precision=HI, preferred_element_type=F32)
        return [big_g, _sigmoid(b)], []
    return _rowwise(body, [(proj, 2 * LANES, colblk)], [a_log, dt_bias], [(LANES, F32), (LANES, F32)], [], name=name,
                    tr=DN_GATE_ROWS)


def _dn_gate_bwd(proj, colblk, d_big_g, dbeta, a_log, dt_bias, name):
    def body(r, v):
        ab, dgc, dbt = r
        a, b = ab[:, :LANES], ab[:, LANES:]
        dg = jnp.dot(_chunk_tri(dgc.shape[0], True), dgc, precision=HI, preferred_element_type=F32)
        na = -jnp.exp(v[0])
        pre = a + v[1]
        da = dg * na * _sigmoid(pre)
        beta = _sigmoid(b)
        db = dbt * beta * (1.0 - beta)
        return [jnp.concatenate([da, db], axis=1)], [_colsum(dg * na * _softplus(pre)), _colsum(da)]
    return _rowwise(body, [(proj, 2 * LANES, colblk), d_big_g, dbeta], [a_log, dt_bias],
                    [(2 * LANES, BF16)], [(1, LANES), (1, LANES)], name=name, tr=DN_GATE_ROWS)


def _shift_rows(x, s):
    if s == 0:
        return x
    t = x.shape[0]
    row = lax.broadcasted_iota(jnp.int32, x.shape, 0)
    rolled = pltpu.roll(x, s % t, axis=0)
    return jnp.where((row >= s) if s > 0 else (row < t + s), rolled, 0.0)


def _rnd(x):
    return x.astype(BF16).astype(F32)


def _conv(x, c_ref):
    c = x * _rnd(c_ref[DN_CONV - 1:DN_CONV, :])
    for s in range(1, DN_CONV):
        c = c + _shift_rows(x, s) * _rnd(c_ref[DN_CONV - 1 - s:DN_CONV - s, :])
    return c


def _dn_prep_fwd(proj, conv_w, heads, name):
    t = proj.shape[0]
    d = heads * DN_HEAD_DIM

    def body(xq, xk, xv, cq, ck, cv, q_ref, k_ref, v_ref):
        for x_ref, c_ref, o_ref, norm in ((xq, cq, q_ref, True), (xk, ck, k_ref, True), (xv, cv, v_ref, False)):
            y = _silu(_conv(_rnd(x_ref[...]), c_ref))
            if norm:
                y = y * lax.rsqrt(jnp.sum(y * y, axis=-1, keepdims=True) + EPS)
            o_ref[...] = y

    xs = [pl.BlockSpec((t, DN_HEAD_DIM), functools.partial(lambda h, o: (0, h + o), o=o * heads)) for o in range(3)]
    cs = [pl.BlockSpec((DN_CONV, DN_HEAD_DIM), functools.partial(lambda h, o: (0, h + o), o=o * heads)) for o in range(3)]
    return pl.pallas_call(
        body, grid=(heads,), in_specs=xs + cs,
        out_specs=[pl.BlockSpec((t, DN_HEAD_DIM), lambda h: (0, h))] * 3,
        out_shape=[jax.ShapeDtypeStruct((t, d), F32)] * 3, name=name,
        compiler_params=_cparams(("parallel",)))(proj, proj, proj, conv_w, conv_w, conv_w)


def _dn_prep_bwd(proj, conv_w, dq, dk, dv, heads, name):
    t = proj.shape[0]
    d = heads * DN_HEAD_DIM

    def body(xq, xk, xv, cq, ck, cv, gq, gk, gv, oq, ok, ov, wq, wk, wv):
        for x_ref, c_ref, g_ref, o_ref, w_ref, norm in ((xq, cq, gq, oq, wq, True), (xk, ck, gk, ok, wk, True),
                                                        (xv, cv, gv, ov, wv, False)):
            x, dy = _rnd(x_ref[...]), g_ref[...]
            c = _conv(x, c_ref)
            if norm:
                y = _silu(c)
                rs = lax.rsqrt(jnp.sum(y * y, axis=-1, keepdims=True) + EPS)
                yn = y * rs
                dy = rs * (dy - yn * jnp.sum(dy * yn, axis=-1, keepdims=True))
            dc = _rnd(dy * _dsilu(c))
            dx = dc * _rnd(c_ref[DN_CONV - 1:DN_CONV, :])
            w_ref[DN_CONV - 1:DN_CONV, :] = _colsum(dc * x)
            for s in range(1, DN_CONV):
                dx = dx + _shift_rows(dc, -s) * _rnd(c_ref[DN_CONV - 1 - s:DN_CONV - s, :])
                w_ref[DN_CONV - 1 - s:DN_CONV - s, :] = _colsum(dc * _shift_rows(x, s))
            o_ref[...] = dx.astype(o_ref.dtype)

    xs = [pl.BlockSpec((t, DN_HEAD_DIM), functools.partial(lambda h, o: (0, h + o), o=o * heads)) for o in range(3)]
    cs = [pl.BlockSpec((DN_CONV, DN_HEAD_DIM), functools.partial(lambda h, o: (0, h + o), o=o * heads)) for o in range(3)]
    hs = pl.BlockSpec((t, DN_HEAD_DIM), lambda h: (0, h))
    ws = pl.BlockSpec((DN_CONV, DN_HEAD_DIM), lambda h: (0, h))
    return pl.pallas_call(
        body, grid=(heads,), in_specs=xs + cs + [hs] * 3, out_specs=[hs] * 3 + [ws] * 3,
        out_shape=[jax.ShapeDtypeStruct((t, d), BF16)] * 3 + [jax.ShapeDtypeStruct((DN_CONV, d), F32)] * 3, name=name,
        compiler_params=_cparams(("parallel",)))(proj, proj, proj, conv_w, conv_w, conv_w, dq, dk, dv)


X3 = "three bf16 passes"


def _bdot(a, b, dims, prec):
    if prec != X3:
        return lax.dot_general(a, b, dims, precision=prec, preferred_element_type=F32)
    a1, b1 = a.astype(BF16), b.astype(BF16)
    a2, b2 = (a - a1.astype(F32)).astype(BF16), (b - b1.astype(F32)).astype(BF16)
    dot = lambda x, y: lax.dot_general(x, y, dims, preferred_element_type=F32)
    return dot(a1, b1) + (dot(a1, b2) + dot(a2, b1))


def _bmm(a, b, prec=None):
    return _bdot(a, b, (((2,), (1,)), ((0,), (0,))), prec)


def _bmm_nt(a, b, prec=None):
    return _bdot(a, b, (((2,), (2,)), ((0,), (0,))), prec)


def _bmm_tn(a, b, prec=None):
    return _bdot(a, b, (((1,), (1,)), ((0,), (0,))), prec)


def _dn_local(qg, kg, vg, gc, gr, bt):
    c = qg.shape[1]
    row = lax.broadcasted_iota(jnp.int32, (c, c), 0)
    col = lax.broadcasted_iota(jnp.int32, (c, c), 1)
    incl, strict = (row >= col)[None], (row > col)[None]
    decay = jnp.where(incl, jnp.exp(jnp.where(incl, gc - gr, 0.0)), 0.0)
    kb = kg * bt
    vb = vg * bt
    m = _bmm_nt(kb.astype(BF16), kg.astype(BF16))
    a = jnp.where(strict, m * decay, 0.0)
    bp = -a
    tm = jnp.where((row == col)[None], 1.0, 0.0) + bp
    steps = max(1, (c - 1).bit_length()) - 1
    for _ in range(steps):
        bp = _bmm(bp, bp, X3)
        tm = tm + _bmm(tm, bp, X3)
    eg = jnp.exp(gc)
    glast = gc[:, c - 1:c, :]
    erel = jnp.exp(glast - gc)
    kbg = kb * eg
    qm = _bmm_nt(qg.astype(BF16), kg.astype(BF16))
    return dict(decay=decay, strict=strict, kb=kb, vb=vb, m=m, tm=tm, eg=eg, erel=erel, kbg=kbg, qm=qm,
                u=_bmm(tm, vb, X3), w=_bmm(tm, kbg, X3), qd=qg * eg, kt=kg * erel, gl=jnp.exp(glast))


def _dot(a, b):
    return jnp.dot(a.astype(BF16), b.astype(BF16), preferred_element_type=F32)


def _dot_nt(a, b):
    return lax.dot_general(a.astype(BF16), b.astype(BF16), (((1,), (1,)), ((), ())), preferred_element_type=F32)


def _dot_tn(a, b):
    return lax.dot_general(a.astype(BF16), b.astype(BF16), (((0,), (0,)), ((), ())), preferred_element_type=F32)


def _dn_chunk_specs(t, heads):
    n = t // DN_CHUNK
    hs = pl.BlockSpec((t, DN_HEAD_DIM), lambda h: (0, h))
    gcs = pl.BlockSpec((1, n, DN_CHUNK, 1), lambda h: (h, 0, 0, 0))
    grs = pl.BlockSpec((1, n, 1, DN_CHUNK), lambda h: (h, 0, 0, 0))
    ss = pl.BlockSpec((1, n, DN_HEAD_DIM, DN_HEAD_DIM), lambda h: (h, 0, 0, 0))
    return n, hs, gcs, grs, ss


def _dn_chunk_fwd(q, k, v, gc, gr, bc, name):
    t, d = q.shape
    heads = d // DN_HEAD_DIM
    c, dk = DN_CHUNK, DN_HEAD_DIM
    n, hs, gcs, grs, ss = _dn_chunk_specs(t, heads)
    nb = min(DN_GROUP, n)
    scale = dk ** -0.5

    def body(q_ref, k_ref, v_ref, gc_ref, gr_ref, b_ref, o_ref, s_ref, u_s, w_s, at_s, qd_s, kt_s, gl_s):
        def group(gi, carry):
            r0 = pl.multiple_of(gi * (nb * c), nb * c)
            n0 = gi * nb
            ld = lambda ref: ref[pl.ds(r0, nb * c), :].reshape(nb, c, dk)
            loc = _dn_local(ld(q_ref) * scale, ld(k_ref), ld(v_ref), gc_ref[0, pl.ds(n0, nb)],
                            gr_ref[0, pl.ds(n0, nb)], b_ref[0, pl.ds(n0, nb)])
            u_s[pl.ds(n0, nb)] = loc["u"]
            w_s[pl.ds(n0, nb)] = loc["w"]
            at_s[pl.ds(n0, nb)] = loc["qm"] * loc["decay"]
            qd_s[pl.ds(n0, nb)] = loc["qd"]
            kt_s[pl.ds(n0, nb)] = loc["kt"]
            gl_s[pl.ds(n0, nb)] = loc["gl"]
            return carry
        lax.fori_loop(0, n // nb, group, 0)

        def chunk(i, s):
            s_ref[0, i] = s
            vnew = u_s[i] - _dot(w_s[i], s)
            o = _dot(qd_s[i], s) + _dot(at_s[i], vnew)
            o_ref[pl.ds(pl.multiple_of(i * c, c), c), :] = o
            return s * gl_s[i] + _dot_tn(kt_s[i], vnew)
        lax.fori_loop(0, n, chunk, jnp.zeros((dk, dk), F32))

    scratch = [pltpu.VMEM((n, c, dk), F32), pltpu.VMEM((n, c, dk), F32), pltpu.VMEM((n, c, c), F32),
               pltpu.VMEM((n, c, dk), F32), pltpu.VMEM((n, c, dk), F32), pltpu.VMEM((n, 1, 1), F32)]
    return pl.pallas_call(
        body, grid=(heads,), in_specs=[hs, hs, hs, gcs, grs, gcs], out_specs=[hs, ss],
        out_shape=[jax.ShapeDtypeStruct((t, d), F32), jax.ShapeDtypeStruct((heads, n, dk, dk), F32)],
        scratch_shapes=scratch, name=name, compiler_params=_cparams(("parallel",)))(q, k, v, gc, gr, bc)


def _dn_chunk_bwd(q, k, v, gc, gr, bc, s_all, do, name):
    t, d = q.shape
    heads = d // DN_HEAD_DIM
    c, dk = DN_CHUNK, DN_HEAD_DIM
    n, hs, gcs, grs, ss = _dn_chunk_specs(t, heads)
    nb = min(DN_GROUP, n)
    scale = dk ** -0.5

    def body(q_ref, k_ref, v_ref, gc_ref, gr_ref, b_ref, s_ref, do_ref,
             dq_ref, dk_ref, dv_ref, dgc_ref, dgr_ref, db_ref,
             u_s, w_s, att_s, qd_s, kt_s, gl_s, du_s, dw_s, dat_s, dqd_s, dkt_s, dgl_s):
        def load_group(gi):
            r0 = pl.multiple_of(gi * (nb * c), nb * c)
            n0 = gi * nb
            ld = lambda ref: ref[pl.ds(r0, nb * c), :].reshape(nb, c, dk)
            qg, kg, vg = ld(q_ref) * scale, ld(k_ref), ld(v_ref)
            gcv, grv, bt = gc_ref[0, pl.ds(n0, nb)], gr_ref[0, pl.ds(n0, nb)], b_ref[0, pl.ds(n0, nb)]
            return r0, n0, qg, kg, vg, gcv, grv, bt, _dn_local(qg, kg, vg, gcv, grv, bt)

        def group_a(gi, carry):
            _, n0, qg, kg, _, gcv, grv, _, loc = load_group(gi)
            row = lax.broadcasted_iota(jnp.int32, (c, c), 0)
            col = lax.broadcasted_iota(jnp.int32, (c, c), 1)
            upper = (col >= row)[None]
            decay_t = jnp.where(upper, jnp.exp(jnp.where(upper, grv - gcv, 0.0)), 0.0)
            u_s[pl.ds(n0, nb)] = loc["u"]
            w_s[pl.ds(n0, nb)] = loc["w"]
            att_s[pl.ds(n0, nb)] = _bmm_nt(kg.astype(BF16), qg.astype(BF16)) * decay_t
            qd_s[pl.ds(n0, nb)] = loc["qd"]
            kt_s[pl.ds(n0, nb)] = loc["kt"]
            gl_s[pl.ds(n0, nb)] = loc["gl"]
            return carry
        lax.fori_loop(0, n // nb, group_a, 0)

        def chunk_b(it, ds_next):
            i = n - 1 - it
            s = s_ref[0, i]
            dov = do_ref[pl.ds(pl.multiple_of(i * c, c), c), :]
            w, kt = w_s[i], kt_s[i]
            vnew = u_s[i] - _dot(w, s)
            dvnew = _dot(att_s[i], dov) + _dot(kt, ds_next)
            du_s[i] = dvnew
            dw_s[i] = -_dot_nt(dvnew, s)
            dat_s[i] = _dot_nt(dov, vnew)
            dqd_s[i] = _dot_nt(dov, s)
            dkt_s[i] = _dot_nt(vnew, ds_next)
            dgl_s[i] = jnp.sum(jnp.sum(ds_next * s, axis=1, keepdims=True), axis=0, keepdims=True)
            return ds_next * gl_s[i] + _dot_tn(qd_s[i], dov) - _dot_tn(w, dvnew)
        lax.fori_loop(0, n, chunk_b, jnp.zeros((dk, dk), F32))

        def group_c(gi, carry):
            r0, n0, qg, kg, vg, gcv, grv, bt, loc = load_group(gi)
            sl = pl.ds(n0, nb)
            du, dw, dat, dqd, dkt, dgl = du_s[sl], dw_s[sl], dat_s[sl], dqd_s[sl], dkt_s[sl], dgl_s[sl]
            tm, decay, kb, kbg = loc["tm"], loc["decay"], loc["kb"], loc["kbg"]
            dvb = _bmm_tn(tm, du, X3)
            dkbg = _bmm_tn(tm, dw, X3)
            dt = _bmm_nt(du, loc["vb"], X3) + _bmm_nt(dw, kbg, X3)
            da = jnp.where(loc["strict"], -_bmm_tn(tm, _bmm_nt(dt, tm, X3), X3), 0.0)
            dms = (da * decay).astype(BF16)
            dqs = (dat * decay).astype(BF16)
            kgb = kg.astype(BF16)
            dkb = _bmm(dms, kgb) + dkbg * loc["eg"]
            dqt = _bmm(dqs, kgb) + dqd * loc["eg"]
            dkk = _bmm_tn(dms, kb.astype(BF16)) + _bmm_tn(dqs, qg.astype(BF16)) + dkt * loc["erel"] + dkb * bt
            e = (da * loc["m"] + dat * loc["qm"]) * decay
            lsum = lambda x: jnp.sum(x, axis=2, keepdims=True)
            dkt_kt = lsum(dkt * loc["kt"])
            dgcv = lsum(e) + lsum(dqd * loc["qd"]) - dkt_kt + lsum(dkbg * kbg)
            dglast = jnp.sum(dkt_kt, axis=1, keepdims=True) + dgl * loc["gl"]
            rowc = lax.broadcasted_iota(jnp.int32, (1, c, 1), 1)
            dgc_ref[0, sl] = dgcv + jnp.where(rowc == c - 1, dglast, 0.0)
            dgr_ref[0, sl] = -jnp.sum(e, axis=1, keepdims=True)
            db_ref[0, sl] = lsum(dkb * kg) + lsum(dvb * vg)
            rows = pl.ds(r0, nb * c)
            dq_ref[rows, :] = (dqt * scale).reshape(nb * c, dk)
            dk_ref[rows, :] = dkk.reshape(nb * c, dk)
            dv_ref[rows, :] = (dvb * bt).reshape(nb * c, dk)
            return carry
        lax.fori_loop(0, n // nb, group_c, 0)

    big = lambda: pltpu.VMEM((n, c, dk), F32)
    sq = lambda: pltpu.VMEM((n, c, c), F32)
    one = lambda: pltpu.VMEM((n, 1, 1), F32)
    scratch = [big(), big(), sq(), big(), big(), one(), big(), big(), sq(), big(), big(), one()]
    return pl.pallas_call(
        body, grid=(heads,), in_specs=[hs, hs, hs, gcs, grs, gcs, ss, hs], out_specs=[hs, hs, hs, gcs, grs, gcs],
        out_shape=[jax.ShapeDtypeStruct((t, d), F32)] * 3 + [
            jax.ShapeDtypeStruct((heads, n, c, 1), F32), jax.ShapeDtypeStruct((heads, n, 1, c), F32),
            jax.ShapeDtypeStruct((heads, n, c, 1), F32)],
        scratch_shapes=scratch, name=name, compiler_params=_cparams(("parallel",)))(q, k, v, gc, gr, bc, s_all, do)


def _dot01(x, m01):
    x1 = x.astype(BF16)
    r1 = x - x1.astype(F32)
    x2 = r1.astype(BF16)
    x3 = (r1 - x2.astype(F32)).astype(BF16)
    dot = lambda a: jnp.dot(a, m01, preferred_element_type=F32)
    return (dot(x1) + dot(x2)) + dot(x3)


def _sb_logs(z, mask):
    rows, bk = z.shape
    row = lax.broadcasted_iota(jnp.int32, (bk, bk), 0)
    col = lax.broadcasted_iota(jnp.int32, (bk, bk), 1)
    later = (row > col).astype(BF16)
    lm, ls, cs = [], [], []
    for p in [slice(p, p + SB_PART) for p in range(0, rows, SB_PART)]:
        lsm = -_softplus(z[p])
        lm.append(lsm if mask is None else jnp.where(mask[p], lsm, 0.0))
        ls.append(z[p] + lsm)
        cs.append(_dot01(lm[-1], later))
    cat = lambda xs: jnp.concatenate(xs, axis=0)
    return cat(ls), cat(cs), jnp.sum(cat(lm), axis=1, keepdims=True)


def _sb_masks(bq, bk):
    row = lax.broadcasted_iota(jnp.int32, (2 * bq, bk), 0)
    row = jnp.where(row >= bq, row - bq, row)
    col = lax.broadcasted_iota(jnp.int32, (2 * bq, bk), 1)
    return [(col + dd * bk) < row for dd in range(bq // bk)]


def _sb_sweep(bq, bk, qi, fetch, visit, carry):
    nd = bq // bk
    masks = _sb_masks(bq, bk)
    ahead = lambda j: fetch(jnp.maximum(j - 1, 0))
    pre = fetch(qi * nd + nd - 1)
    for dd in reversed(range(nd)):
        nxt = ahead(qi * nd + dd)
        carry = visit(qi * nd + dd, masks[dd], carry, pre)
        pre = nxt

    def below(it, c):
        j = qi * nd - 1 - it
        nxt = ahead(j)
        return visit(j, None, c[0], c[1]), nxt
    return lax.fori_loop(0, qi * nd, below, (carry, pre))[0]


def _sb_sweep2(bq, bk, qi, fetch, first, second, carry):
    nd = bq // bk
    masks = _sb_masks(bq, bk)
    ahead = lambda j: fetch(jnp.maximum(j - 1, 0))
    pre = fetch(qi * nd + nd - 1)
    r = jnp.zeros((2 * bq, 1), F32)
    pend = None
    for dd in reversed(range(nd)):
        j = qi * nd + dd
        nxt = ahead(j)
        ls, cs, rsum, extras = first(masks[dd], pre)
        if pend is not None:
            carry = second(pend[0], carry, *pend[1])
        pend = (j, (jnp.where(masks[dd], ls + cs + r, -1e30),) + extras)
        r, pre = r + rsum, nxt

    def below(it, c):
        carry, r, pre, pj, pargs = c
        j = qi * nd - 1 - it
        nxt = ahead(j)
        ls, cs, rsum, extras = first(None, pre)
        carry = second(pj, carry, *pargs)
        return carry, r + rsum, nxt, j, (ls + cs + r,) + extras
    carry, _, _, pj, pargs = lax.fori_loop(0, qi * nd, below, (carry, r, pre) + pend)
    return second(pj, carry, *pargs)


def _stack_heads(a, h0):
    return jnp.concatenate([jnp.where(h0, a, 0.0), jnp.where(h0, 0.0, a)], axis=0).astype(BF16)


def _side_by_side(a, bq):
    return jnp.concatenate([a[:bq], a[bq:]], axis=1)


def _sb_specs(t, d, bq, vcol):
    qs = pl.BlockSpec((bq, LANES), lambda g, i: (i, g))
    ks = pl.BlockSpec((t, LANES), lambda g, i: (0, g))
    vs = pl.BlockSpec((t, LANES), lambda g, i: (0, g + vcol))
    return qs, ks, vs


def _sb_fwd(qn, kn, qkv, name):
    t, d = qn.shape
    bq, bk = min(SB_QBLOCK, t), min(SB_BLOCK, t)
    scale = SB_HEAD_DIM ** -0.5
    nt = (((1,), (1,)), ((), ()))

    def body(q_ref, k_ref, v_ref, o_ref):
        qi = pl.program_id(1)
        h0 = lax.broadcasted_iota(jnp.int32, (1, LANES), 1) < SB_HEAD_DIM
        q2 = _stack_heads(q_ref[...] * scale, h0)

        tile_rows = lambda j: pl.ds(pl.multiple_of(j * bk, bk), bk)

        def fetch(j):
            return lax.dot_general(q2, k_ref[tile_rows(j), :].astype(BF16), nt, preferred_element_type=F32)

        def first(mask, z):
            return _sb_logs(z, mask) + ((),)

        def second(j, acc, loga):
            a2 = _side_by_side(jnp.exp(loga).astype(BF16), bq)
            return acc + jnp.dot(a2, _stack_heads(v_ref[tile_rows(j), :], h0), preferred_element_type=F32)
        acc = _sb_sweep2(bq, bk, qi, fetch, first, second, jnp.zeros((bq, LANES), F32))
        o_ref[...] = acc.astype(o_ref.dtype)

    qs, ks, vs = _sb_specs(t, d, bq, 2 * d // LANES)
    return pl.pallas_call(
        body, grid=(d // LANES, t // bq), in_specs=[qs, ks, vs], out_specs=qs,
        out_shape=jax.ShapeDtypeStruct((t, d), BF16), name=name,
        compiler_params=_cparams(("parallel", "parallel")))(qn, kn, qkv)


def _sb_bwd(qn, kn, qkv, do, name):
    t, d = qn.shape
    bq, bk = min(SB_QBLOCK, t), min(SB_BLOCK, t)
    scale = SB_HEAD_DIM ** -0.5
    nt = (((1,), (1,)), ((), ()))

    def body(q_ref, k_ref, v_ref, do_ref, dq_ref, dk_ref, dv_ref, p_s, sg_s):
        qi = pl.program_id(1)

        @pl.when(qi == 0)
        def _():
            dk_ref[...] = jnp.zeros(dk_ref.shape, F32)
            dv_ref[...] = jnp.zeros(dv_ref.shape, F32)

        h0 = lax.broadcasted_iota(jnp.int32, (1, LANES), 1) < SB_HEAD_DIM
        q2 = _stack_heads(q_ref[...] * scale, h0)
        do2 = _stack_heads(do_ref[...], h0)
        row = lax.broadcasted_iota(jnp.int32, (bk, bk), 0)
        col = lax.broadcasted_iota(jnp.int32, (bk, bk), 1)
        later_incl = (row >= col).astype(BF16)
        zero = jnp.zeros((2 * bq, 1), F32)
        tn = (((0,), (0,)), ((), ()))

        tile_rows = lambda j: pl.ds(pl.multiple_of(j * bk, bk), bk)

        def fetch1(j):
            rows = tile_rows(j)
            return (lax.dot_general(q2, k_ref[rows, :].astype(BF16), nt, preferred_element_type=F32),
                    lax.dot_general(do2, v_ref[rows, :].astype(BF16), nt, preferred_element_type=F32))

        def first1(mask, z_da):
            ls, cs, rsum = _sb_logs(z_da[0], mask)
            return ls, cs, rsum, (ls, z_da[1])

        def second1(j, sp, loga, ls, da):
            a = jnp.exp(loga)
            p = a * da
            p_s[j] = p
            sg_s[j] = jnp.exp(ls)
            dv_ref[tile_rows(j), :] += lax.dot_general(a.astype(BF16), do2, tn, preferred_element_type=F32)
            return sp + jnp.sum(p, axis=1, keepdims=True)
        total = _sb_sweep2(bq, bk, qi, fetch1, first1, second1, zero)

        def fetch2(j):
            return _dot01(p_s[j], later_incl)

        def visit2(j, mask, carry, later_p):
            dq, sp = carry
            rows = tile_rows(j)
            p, sg = p_s[j], sg_s[j]
            pref = total - sp - later_p
            dz = p * (1.0 - sg) - pref * sg
            if mask is not None:
                dz = jnp.where(mask, dz, 0.0)
            dz = dz.astype(BF16)
            dk_ref[rows, :] += lax.dot_general(dz, q2, tn, preferred_element_type=F32)
            dq = dq + jnp.dot(_side_by_side(dz, bq), _stack_heads(k_ref[rows, :], h0), preferred_element_type=F32)
            return dq, sp + jnp.sum(p, axis=1, keepdims=True)
        dq, _ = _sb_sweep(bq, bk, qi, fetch2, visit2, (jnp.zeros((bq, LANES), F32), zero))
        dq_ref[...] = dq * scale

    qs, ks, vs = _sb_specs(t, d, bq, 2 * d // LANES)
    shp = jax.ShapeDtypeStruct((t, d), F32)
    scratch = [pltpu.VMEM((t // bk, 2 * bq, bk), F32), pltpu.VMEM((t // bk, 2 * bq, bk), F32)]
    return pl.pallas_call(
        body, grid=(d // LANES, t // bq), in_specs=[qs, ks, vs, qs], out_specs=[qs, ks, ks],
        out_shape=[shp, shp, shp], scratch_shapes=scratch, name=name,
        compiler_params=_cparams(("parallel", "arbitrary")))(qn, kn, qkv, do)


def _adamw(w, g, m, v, name):
    shape = w.shape
    two_d = lambda a: a.reshape(-1, shape[-1])
    c1 = 1.0 - ADAM_B1 ** ADAM_STEP
    c2 = 1.0 - ADAM_B2 ** ADAM_STEP

    def body(r, _):
        wv, gv, mv, vv = r
        mn = ADAM_B1 * mv + (1.0 - ADAM_B1) * gv
        vn = ADAM_B2 * vv + (1.0 - ADAM_B2) * (gv * gv)
        delta = -ADAM_LR * ((mn / c1) / (jnp.sqrt(vn / c2) + ADAM_EPS) + ADAM_WD * wv)
        return [delta, mn, vn], []
    width = shape[-1]
    res = _rowwise(body, [two_d(w), two_d(g), two_d(m), two_d(v)], [], [(width, F32)] * 3, [], name=name)
    return [r.reshape(shape) for r in res]


def _dn_layer_fwd(h1, w, conv_w, a_log, dt_bias, onorm_g, tag):
    t, d = h1.shape
    heads = d // DN_HEAD_DIM
    n = t // DN_CHUNK
    proj = _mm(h1, w["all"], name=tag + "_proj")
    qn, kn, vv = _dn_prep_fwd(proj, conv_w, heads, tag + "_prep")
    big_g, beta = _dn_gate_fwd(proj, 4 * d // (2 * LANES), a_log, dt_bias, tag + "_gate")
    gt_ = big_g[:, :heads].T.reshape(heads, n, DN_CHUNK)
    gc, gr = gt_[..., None], gt_[:, :, None, :]
    bc = beta[:, :heads].T.reshape(heads, n, DN_CHUNK)[..., None]
    o, s_all = _dn_chunk_fwd(qn, kn, vv, gc, gr, bc, tag + "_chunk")
    on = _dn_post_fwd(o, proj, onorm_g, tag + "_post")
    y = _mm(on, w["out"], name=tag + "_out")
    return y, dict(proj=proj, qn=qn, kn=kn, v=vv, gc=gc, gr=gr, bc=bc, o=o, s_all=s_all, on=on)


def _dn_layer_bwd(dy, h1, w, conv_w, a_log, dt_bias, onorm_g, sv, tag):
    t, d = h1.shape
    heads = d // DN_HEAD_DIM
    don = _mm(dy, w["out"], tb=True, name=tag + "_dout")
    g_out = _mm(sv["on"], dy, ta=True, out_dtype=BF16, name=tag + "_gwout")
    do, dz, g_on = _dn_post_bwd(sv["o"], sv["proj"], don, onorm_g, tag + "_dpost")
    dq, dk, dv, dgc, dgr, dbc = _dn_chunk_bwd(sv["qn"], sv["kn"], sv["v"], sv["gc"], sv["gr"], sv["bc"], sv["s_all"],
                                              do, tag + "_dchunk")
    pad = lambda a: jnp.pad(a.reshape(heads, t).T, ((0, 0), (0, LANES - heads)))
    d_big_g = pad(dgc) + pad(dgr)
    dab, g_alog, g_dt = _dn_gate_bwd(sv["proj"], 4 * d // (2 * LANES), d_big_g, pad(dbc), a_log, dt_bias, tag + "_dgate")
    dxq, dxk, dxv, wq, wk, wv = _dn_prep_bwd(sv["proj"], conv_w, dq, dk, dv, heads, tag + "_dprep")
    dproj = jnp.concatenate([dxq, dxk, dxv, dz, dab], axis=1)
    dh1 = _mm(dproj, w["all"], tb=True, name=tag + "_dh")
    g_all = _mm(h1, dproj, ta=True, out_dtype=BF16, name=tag + "_gwin")
    grads = dict(w_all=g_all, w_out=g_out, conv_w=jnp.concatenate([wq, wk, wv], axis=1), a_log=g_alog, dt_bias=g_dt,
                 onorm_g=g_on)
    return dh1, grads


def _sb_layer_fwd(h1, w, q_g, k_g, tag):
    t, d = h1.shape
    heads = d // SB_HEAD_DIM
    qkv = _mm(h1, w["qkv"], name=tag + "_proj")
    gq, gk = jnp.tile(q_g, (1, heads)), jnp.tile(k_g, (1, heads))
    qn, kn = _sb_norm_fwd(qkv, gq, gk, tag + "_norm")
    o = _sb_fwd(qn, kn, qkv, tag + "_attn")
    y = _mm(o, w["out"], name=tag + "_out")
    return y, dict(qkv=qkv, qn=qn, kn=kn, o=o, gq=gq, gk=gk)


def _sb_layer_bwd(dy, h1, w, q_g, k_g, sv, tag):
    t, d = h1.shape
    heads = d // SB_HEAD_DIM
    do = _mm(dy, w["out"], tb=True, name=tag + "_dout")
    g_out = _mm(sv["o"], dy, ta=True, out_dtype=BF16, name=tag + "_gwout")
    dqn, dkn, dv = _sb_bwd(sv["qn"], sv["kn"], sv["qkv"], do, tag + "_dattn")
    dqkv, g_q, g_k = _sb_norm_bwd(sv["qkv"], dqn, dkn, dv, sv["gq"], sv["gk"], tag + "_dnorm")
    fold = lambda g: jnp.sum(g.reshape(heads, SB_HEAD_DIM), axis=0, keepdims=True)
    dh1 = _mm(dqkv, w["qkv"], tb=True, name=tag + "_dh")
    g_qkv = _mm(h1, dqkv, ta=True, out_dtype=BF16, name=tag + "_gwin")
    return dh1, dict(w_qkv=g_qkv, w_out=g_out, q_norm_g=fold(g_q), k_norm_g=fold(g_k))


def _local_step(x, tgt, mod, norm1_g, norm2_g, layer_weights, layer_grads):
    depth = mod.shape[0]
    d = x.shape[1]
    saved = []
    for i in range(depth):
        mix_w, ffn_w = layer_weights(i, x)
        mv = [mod[i:i + 1, j * d:(j + 1) * d] for j in range(N_MOD)]
        sh1, sc1, gt1, sh2, sc2, gt2 = mv
        tag = "l%d" % i
        h1 = _adaln_fwd(x, norm1_g[i:i + 1], sc1, sh1, tag + "_ln1")
        if i % 2 == 0:
            p = mix_w
            y, sv = _dn_layer_fwd(h1, p, p["conv_w"], p["a_log"], p["dt_bias"], p["onorm_g"], tag + "_dn")
        else:
            p = mix_w
            y, sv = _sb_layer_fwd(h1, p, p["q_g"], p["k_g"], tag + "_sb")
        x1 = _resid_fwd(x, y, gt1, tag + "_res1")
        h2 = _adaln_fwd(x1, norm2_g[i:i + 1], sc2, sh2, tag + "_ln2")
        u = _mm(h2, ffn_w["w_in"], name=tag + "_ffn_in")
        a = _swiglu_fwd(u, tag + "_swiglu")
        y2 = _mm(a, ffn_w["w_out"], name=tag + "_ffn_out")
        x2 = _resid_fwd(x1, y2, gt2, tag + "_res2")
        saved.append(dict(x0=x, h1=h1, y=y, mix=sv, x1=x1, h2=h2, u=u, a=a, y2=y2, mix_w=mix_w, ffn_w=ffn_w))
        x = x2

    loss, dx = _loss_fwd_bwd(x, tgt, "loss")

    dmod, dn1, dn2 = [None] * depth, [None] * depth, [None] * depth
    zero = jnp.zeros((), F32)
    for i in reversed(range(depth)):
        s = saved[i]
        mv = [mod[i:i + 1, j * d:(j + 1) * d] + zero for j in range(N_MOD)]
        sh1, sc1, gt1, sh2, sc2, gt2 = mv
        tag = "l%d" % i
        ffn_w, p = s["ffn_w"], s["mix_w"]
        dy2, dgt2 = _resid_bwd(dx, s["y2"], gt2, tag + "_dres2")
        da = _mm(dy2, ffn_w["w_out"], tb=True, name=tag + "_dffn_a")
        g_wout = _mm(s["a"], dy2, ta=True, out_dtype=BF16, name=tag + "_gffn_out")
        du = _swiglu_bwd(s["u"], da, tag + "_dswiglu")
        dh2 = _mm(du, ffn_w["w_in"], tb=True, name=tag + "_dffn_h")
        g_win = _mm(s["h2"], du, ta=True, out_dtype=BF16, name=tag + "_gffn_in")
        dx, dg2, dsc2, dsh2 = _adaln_bwd(s["x1"], dh2, dx, norm2_g[i:i + 1], sc2, tag + "_dln2")
        dy, dgt1 = _resid_bwd(dx, s["y"], gt1, tag + "_dres1")
        if i % 2 == 0:
            dh1, g_mix = _dn_layer_bwd(dy, s["h1"], p, p["conv_w"], p["a_log"], p["dt_bias"], p["onorm_g"], s["mix"],
                                       tag + "_dn")
        else:
            dh1, g_mix = _sb_layer_bwd(dy, s["h1"], p, p["q_g"], p["k_g"], s["mix"], tag + "_sb")
        dx, dg1, dsc1, dsh1 = _adaln_bwd(s["x0"], dh1, dx, norm1_g[i:i + 1], sc1, tag + "_dln1")
        dmod[i] = jnp.concatenate([dsh1, dsc1, dgt1, dsh2, dsc2, dgt2], axis=1)
        dn1[i], dn2[i] = dg1, dg2
        zero = layer_grads(i, dx, g_mix, dict(w_in=g_win, w_out=g_wout))
    return loss, dx, jnp.concatenate(dmod, axis=0), jnp.concatenate(dn1, axis=0), jnp.concatenate(dn2, axis=0)


def _axes():
    return lax.axis_index("x"), lax.axis_index("y"), lax.axis_index("c")


def _remote(src, dst, send_sem, recv_sem, dev):
    return pltpu.make_async_remote_copy(src_ref=src, dst_ref=dst, send_sem=send_sem, recv_sem=recv_sem,
                                        device_id=dev, device_id_type=MESH)


def _other_chips(x, y):
    return [(1 - x, y), (x, 1 - y), (1 - x, 1 - y)]


def _allgather_small(v, name):
    m, n = v.shape

    def body(x_ref, out_ref, send_sems, recv_sems, local_sem):
        x, y, c = _axes()
        me, sibling = (x, y, c), (x, y, 1 - c)
        chips = _other_chips(x, y)

        def rows(px, py, pc):
            return out_ref.at[pl.ds((4 * px + 2 * py + pc) * m, m), :]

        def copy(k, block, to, src=None):
            return _remote(rows(*block) if src is None else src, rows(*block), send_sems.at[k], recv_sems.at[k], to)

        mine = pltpu.make_async_copy(x_ref, rows(*me), local_sem)
        mine.start()
        first = [copy(0, me, sibling, src=x_ref)]
        first += [copy(1 + j, me, (*chip, c), src=x_ref) for j, chip in enumerate(chips)]
        for cp in first:
            cp.start()
        passed = [copy(4 + j, (*chip, c), sibling) for j, chip in enumerate(chips)]
        for j, chip in enumerate(chips):
            copy(1 + j, (*chip, c), me).wait_recv()
            passed[j].start()
        copy(0, sibling, me).wait_recv()
        for j, chip in enumerate(chips):
            copy(4 + j, (*chip, 1 - c), me).wait_recv()
        for cp in first + passed:
            cp.wait_send()
        mine.wait()

    return pl.pallas_call(
        body, out_shape=jax.ShapeDtypeStruct((8 * m, n), v.dtype),
        in_specs=[pl.BlockSpec(memory_space=pltpu.VMEM)], out_specs=pl.BlockSpec(memory_space=pltpu.VMEM),
        scratch_shapes=[pltpu.SemaphoreType.DMA((7,)), pltpu.SemaphoreType.DMA((7,)), pltpu.SemaphoreType.DMA],
        name=name, compiler_params=pltpu.CompilerParams(vmem_limit_bytes=VMEM_LIMIT))(v)


def _half(ref, h, rh):
    return ref.at[(slice(None),) * (len(ref.shape) - 2) + (pl.ds(h * rh, rh), slice(None))]


def _hbm_call(body, ins, out_shapes, n_sems, n_local, name):
    hbm = pl.BlockSpec(memory_space=pltpu.HBM)
    scratch = [pltpu.SemaphoreType.DMA((n_sems,)), pltpu.SemaphoreType.DMA((n_sems,))]
    if n_local:
        scratch.append(pltpu.SemaphoreType.DMA((n_local,)))
    return pl.pallas_call(body, out_shape=out_shapes, in_specs=[hbm] * len(ins), out_specs=[hbm] * len(out_shapes),
                          scratch_shapes=scratch, name=name)(*ins)


def _allgather_weights(shards, name):
    n = len(shards)

    def body(*refs):
        p_refs, out_refs, (send_sems, recv_sems, local_sems) = refs[:n], refs[n:2 * n], refs[2 * n:]
        x, y, c = _axes()
        sibling = (x, y, 1 - c)
        chips = _other_chips(x, y)
        rh = [s.shape[1] // 2 for s in shards]
        blk = lambda i, cx, cy, h: _half(out_refs[i].at[2 * cx + cy], h, rh[i])
        mine = [pltpu.make_async_copy(p_refs[i], out_refs[i].at[2 * x + y], local_sems.at[i]) for i in range(n)]
        for cp in mine:
            cp.start()
        first = [_remote(_half(p_refs[i], c, rh[i]), blk(i, x, y, c), send_sems.at[6 * i + j], recv_sems.at[6 * i + j],
                         (*chip, c)) for i in range(n) for j, chip in enumerate(chips)]
        for cp in first:
            cp.start()
        passed = []
        for j, chip in enumerate(chips):
            for i in range(n):
                _remote(_half(p_refs[i], c, rh[i]), blk(i, *chip, c), send_sems.at[6 * i + j], recv_sems.at[6 * i + j],
                        sibling).wait_recv()
                passed.append(_remote(blk(i, *chip, c), blk(i, *chip, c), send_sems.at[6 * i + 3 + j],
                                      recv_sems.at[6 * i + 3 + j], sibling))
                passed[-1].start()
        for j, chip in enumerate(chips):
            for i in range(n):
                _remote(_half(p_refs[i], c, rh[i]), blk(i, *chip, 1 - c), send_sems.at[6 * i + 3 + j],
                        recv_sems.at[6 * i + 3 + j], sibling).wait_recv()
        for cp in first + passed:
            cp.wait_send()
        for cp in mine:
            cp.wait()

    outs = [jax.ShapeDtypeStruct((4,) + s.shape, s.dtype) for s in shards]
    return _hbm_call(body, shards, outs, 6 * n, n, name)


def _sibling_swap_halves(arrs, name):
    n = len(arrs)

    def body(*refs):
        v_refs, out_refs, (send_sems, recv_sems) = refs[:n], refs[n:2 * n], refs[2 * n:]
        x, y, c = _axes()
        cps = [_remote(_half(v_refs[i], 1 - c, arrs[i].shape[-2] // 2), out_refs[i], send_sems.at[i], recv_sems.at[i],
                       (x, y, 1 - c)) for i in range(n)]
        for cp in cps:
            cp.start()
        for cp in cps:
            cp.wait()

    outs = [jax.ShapeDtypeStruct(a.shape[:-2] + (a.shape[-2] // 2, a.shape[-1]), a.dtype) for a in arrs]
    return _hbm_call(body, arrs, outs, n, 0, name)


def _chip_scatter(arrs, name):
    n = len(arrs)

    def body(*refs):
        v_refs, out_refs, (send_sems, recv_sems, local_sems) = refs[:n], refs[n:2 * n], refs[2 * n:]
        x, y, c = _axes()
        me = 2 * x + y
        chips = _other_chips(x, y)
        mine = [pltpu.make_async_copy(v_refs[i].at[me], out_refs[i].at[me], local_sems.at[i]) for i in range(n)]
        for cp in mine:
            cp.start()
        sends = [_remote(v_refs[i].at[2 * cx + cy], out_refs[i].at[me], send_sems.at[3 * i + j], recv_sems.at[3 * i + j],
                         (cx, cy, c)) for i in range(n) for j, (cx, cy) in enumerate(chips)]
        for cp in sends:
            cp.start()
        for i in range(n):
            for j, (cx, cy) in enumerate(chips):
                _remote(v_refs[i].at[me], out_refs[i].at[2 * cx + cy], send_sems.at[3 * i + j], recv_sems.at[3 * i + j],
                        (cx, cy, c)).wait_recv()
        for cp in sends:
            cp.wait_send()
        for cp in mine:
            cp.wait()

    outs = [jax.ShapeDtypeStruct(a.shape, a.dtype) for a in arrs]
    return _hbm_call(body, arrs, outs, 3 * n, n, name)


def _ici_src(ref, mode, slot, c):
    return _half(ref, c, ref.shape[-2] // 2) if mode == "half" else ref.at[slot]


def _ici_start(srcs, mode, name):
    n, ncp = len(srcs), 3 * len(srcs)
    lands = [jnp.zeros(((4,) + s.shape[:-2] + (s.shape[-2] // 2, s.shape[-1])) if mode == "half" else s.shape, s.dtype)
             for s in srcs]

    def body(*refs):
        src_refs, land_refs = refs[:n], refs[n:2 * n]
        send_sems, recv_sems = refs[2 * n:2 * n + ncp], refs[2 * n + ncp:2 * n + 2 * ncp]
        token = refs[-1]
        x, y, c = _axes()
        for i in range(n):
            for j, (cx, cy) in enumerate(_other_chips(x, y)):
                _remote(_ici_src(src_refs[i], mode, 2 * cx + cy, c), land_refs[i].at[2 * x + y], send_sems[3 * i + j],
                        recv_sems[3 * i + j], (cx, cy, c)).start()
        token[...] = jnp.zeros(token.shape, token.dtype)

    hbm, sem = pl.BlockSpec(memory_space=pltpu.HBM), pl.BlockSpec(memory_space=pltpu.SEMAPHORE)
    bufs = srcs + lands
    out_shape = tuple([pltpu.SemaphoreType.DMA(())] * (2 * ncp) + [pltpu.HBM(b.shape, b.dtype) for b in bufs]
                      + [jax.ShapeDtypeStruct((8, LANES), F32)])
    res = pl.pallas_call(
        body, name=name, out_shape=out_shape, in_specs=(hbm,) * (2 * n),
        out_specs=(sem,) * (2 * ncp) + (hbm,) * (2 * n) + (pl.BlockSpec(memory_space=pltpu.VMEM),),
        input_output_aliases={i: 2 * ncp + i for i in range(2 * n)},
        compiler_params=pltpu.CompilerParams(has_side_effects=pltpu.SideEffectType.DATAFLOW_SIDE_EFFECTING),
    )(*[pltpu.with_memory_space_constraint(b, pltpu.HBM) for b in bufs])
    return list(res[:2 * ncp]), list(res[2 * ncp:2 * ncp + 2 * n]), res[-1][0, 0]


def _ici_wait(sems, bufs, mode, after, name):
    n, ncp = len(bufs) // 2, len(sems) // 2

    def body(*refs):
        src_refs, land_refs = refs[:n], refs[n:2 * n]
        send_sems, recv_sems = refs[2 * n:2 * n + ncp], refs[2 * n + ncp:2 * n + 2 * ncp]
        x, y, c = _axes()
        for i in range(n):
            for j, (cx, cy) in enumerate(_other_chips(x, y)):
                cp = _remote(_ici_src(src_refs[i], mode, 2 * cx + cy, c), land_refs[i].at[2 * cx + cy],
                             send_sems[3 * i + j], recv_sems[3 * i + j], (cx, cy, c))
                cp.wait_send()
                cp.wait_recv()

    hbm, sem = pl.BlockSpec(memory_space=pltpu.HBM), pl.BlockSpec(memory_space=pltpu.SEMAPHORE)
    res = pl.pallas_call(
        body, name=name, out_shape=tuple(pltpu.HBM(b.shape, b.dtype) for b in bufs),
        in_specs=(hbm,) * (2 * n) + (sem,) * (2 * ncp) + (pl.BlockSpec(memory_space=pl.ANY),),
        out_specs=(hbm,) * (2 * n), input_output_aliases={i: i for i in range(2 * n)},
        compiler_params=pltpu.CompilerParams(has_side_effects=pltpu.SideEffectType.DATAFLOW_SIDE_EFFECTING),
    )(*bufs, *sems, after)
    return list(res[n:])


def _share_halves(arrs, name):
    n = len(arrs)

    def body(*refs):
        v_refs, out_refs, (send_sems, recv_sems) = refs[:n], refs[n:2 * n], refs[2 * n:]
        x, y, c = _axes()
        cps = [_remote(v_refs[i], out_refs[i], send_sems.at[i], recv_sems.at[i], (x, y, 1 - c)) for i in range(n)]
        for cp in cps:
            cp.start()
        for cp in cps:
            cp.wait()

    outs = [jax.ShapeDtypeStruct(a.shape, a.dtype) for a in arrs]
    return _hbm_call(body, arrs, outs, n, 0, name)


def _sum_lead(v, name):
    k, r, w = v.shape
    tr = _pick_tile(r, (256, 128, 64, 32, 16))

    def body(v_ref, o_ref):
        acc = v_ref[0].astype(F32)
        for i in range(1, k):
            acc = acc + v_ref[i].astype(F32)
        o_ref[...] = acc

    return pl.pallas_call(
        body, grid=(r // tr,), in_specs=[pl.BlockSpec((k, tr, w), lambda i: (0, i, 0))],
        out_specs=pl.BlockSpec((tr, w), lambda i: (i, 0)), out_shape=jax.ShapeDtypeStruct((r, w), F32), name=name,
        compiler_params=_cparams(("parallel",)))(v)


_BIG = (("dn_w_in", 2), ("dn_w_out", 1), ("sb_w_qkv", 2), ("sb_w_out", 1), ("ffn_w_in", 2), ("ffn_w_out", 1))


def _to_chip_major(a, axis, nchips):
    l, r, c = a.shape
    if axis == 2:
        return a.reshape(l, r, nchips, c // nchips).transpose(2, 0, 1, 3)
    return a.reshape(l, nchips, r // nchips, c).transpose(1, 0, 2, 3)


def _from_chip_major(a, axis):
    n, l, r, c = a.shape
    if axis == 2:
        return a.transpose(1, 2, 0, 3).reshape(l, r, n * c)
    return a.transpose(1, 0, 2, 3).reshape(l, n * r, c)


def kernel(x, c, ada_w, ada_b, norm1_g, norm2_g, dn_w_in, dn_conv_w, dn_a_log, dn_dt_bias, dn_onorm_g, dn_w_out, sb_w_qkv, sb_q_norm_g, sb_k_norm_g, sb_w_out, ffn_w_in, ffn_w_out, loss_target, m_ada_w, m_ada_b, m_norm1_g, m_norm2_g, m_dn_w_in, m_dn_conv_w, m_dn_a_log, m_dn_dt_bias, m_dn_onorm_g, m_dn_w_out, m_sb_w_qkv, m_sb_q_norm_g, m_sb_k_norm_g, m_sb_w_out, m_ffn_w_in, m_ffn_w_out, v_ada_w, v_ada_b, v_norm1_g, v_norm2_g, v_dn_w_in, v_dn_conv_w, v_dn_a_log, v_dn_dt_bias, v_dn_onorm_g, v_dn_w_out, v_sb_w_qkv, v_sb_q_norm_g, v_sb_k_norm_g, v_sb_w_out, v_ffn_w_in, v_ffn_w_out):
    names = ("ada_w", "ada_b", "norm1_g", "norm2_g", "dn_w_in", "dn_conv_w", "dn_a_log", "dn_dt_bias", "dn_onorm_g",
             "dn_w_out", "sb_w_qkv", "sb_q_norm_g", "sb_k_norm_g", "sb_w_out", "ffn_w_in", "ffn_w_out")
    w = dict(zip(names, (ada_w, ada_b, norm1_g, norm2_g, dn_w_in, dn_conv_w, dn_a_log, dn_dt_bias, dn_onorm_g,
                         dn_w_out, sb_w_qkv, sb_q_norm_g, sb_k_norm_g, sb_w_out, ffn_w_in, ffn_w_out)))
    mom = dict(zip(names, (m_ada_w, m_ada_b, m_norm1_g, m_norm2_g, m_dn_w_in, m_dn_conv_w, m_dn_a_log, m_dn_dt_bias,
                           m_dn_onorm_g, m_dn_w_out, m_sb_w_qkv, m_sb_q_norm_g, m_sb_k_norm_g, m_sb_w_out, m_ffn_w_in,
                           m_ffn_w_out)))
    var = dict(zip(names, (v_ada_w, v_ada_b, v_norm1_g, v_norm2_g, v_dn_w_in, v_dn_conv_w, v_dn_a_log, v_dn_dt_bias,
                           v_dn_onorm_g, v_dn_w_out, v_sb_w_qkv, v_sb_q_norm_g, v_sb_k_norm_g, v_sb_w_out, v_ffn_w_in,
                           v_ffn_w_out)))
    ax, ay, ac = _axes()
    chip = 2 * ax + ay
    dev = 2 * chip + ac
    t, d = x.shape[1], x.shape[2]
    depth, ndn, nsb = ada_w.shape[0], dn_w_in.shape[0], sb_w_qkv.shape[0]
    heads = d // DN_HEAD_DIM
    mod_cols = ada_w.shape[2]
    conv_cols = dn_conv_w.shape[2]
    nchips, ndev = 4, 8

    conv_rows = ndn * DN_CONV * conv_cols // d
    pay1 = jnp.concatenate([c, dn_conv_w.reshape(conv_rows, d), jnp.zeros((8 - 1 - conv_rows, d), F32)], axis=0)
    g1 = _allgather_small(pay1, "ag_cond").reshape(ndev, 8, d)
    c_all = g1[:, 0]
    conv_full = g1[::2, 1:1 + conv_rows].reshape(nchips, ndn, DN_CONV, conv_cols).transpose(1, 2, 0, 3)
    conv_full = conv_full.reshape(ndn, DN_CONV, nchips * conv_cols)

    c16 = jnp.pad(c_all, ((0, 16 - ndev), (0, 0)))
    cond16 = _rowwise(lambda r, v: ([_silu(r[0])], []), [c16], [], [(d, F32)], [], name="cond_silu")[0]
    pay2 = jnp.concatenate([_mm(cond16, ada_w[i], name="ada_mod%d" % i)[:ndev] for i in range(depth)], axis=0)
    g2 = _allgather_small(pay2, "ag_mod").reshape(ndev, depth, ndev, mod_cols)[::2]
    mod_raw = lax.dynamic_index_in_dim(g2, dev, axis=2, keepdims=False)
    mod_raw = mod_raw.transpose(1, 0, 2).reshape(depth, nchips * mod_cols)
    mod = _rowwise(lambda r, v: ([r[0] + r[1]], []), [mod_raw, ada_b], [], [(nchips * mod_cols, F32)], [],
                   name="ada_bias")[0]

    axis_of = dict(_BIG)
    groups = [[0], list(range(1, depth))]

    def kinds_of(layers):
        out = []
        for n, _ in _BIG:
            idx = [l // 2 for l in layers if l % 2 == (0 if n.startswith("dn_") else 1)] if n[:3] in ("dn_", "sb_") \
                else list(layers)
            if idx:
                out.append((n, min(idx), max(idx) + 1))
        return out

    two_d = lambda a: a.reshape(-1, a.shape[-1])
    both_halves = lambda mine, other, ax: jnp.where(ac == 0, jnp.concatenate([mine, other], axis=ax),
                                                    jnp.concatenate([other, mine], axis=ax))
    w16 = {n: w[n].astype(BF16) for n, _ in _BIG}
    gathers = []
    for gi, layers in enumerate(groups):
        sl = kinds_of(layers)
        shards = [w16[n][lo:hi] for n, lo, hi in sl]
        sems, bufs, zero = _ici_start(shards, "half", "ag_start%d" % gi)
        gathers.append((sl, shards, sems, bufs))
        mod = mod + zero
    full = {}

    def finish_gather(gi, after):
        sl, shards, sems, bufs = gathers[gi]
        lands = _ici_wait(sems, bufs, "half", after, "ag_wait%d" % gi)
        for (n, lo, hi), shard, mine, other in zip(sl, shards, lands, _share_halves(lands, "ag_pair%d" % gi)):
            a = lax.dynamic_update_index_in_dim(both_halves(mine, other, 2), shard, chip, 0)
            a = _from_chip_major(a, axis_of[n])
            for l in range(lo, hi):
                full[n, l] = a[l - lo]

    padl = lambda v: jnp.pad(v[None, :], ((0, 0), (0, LANES - v.shape[0])))
    padc = lambda a: jnp.pad(a, ((0, 0), (0, LANES - a.shape[1])))

    def layer_weights(i, after):
        for gi, layers in enumerate(groups):
            if layers and i == layers[0]:
                finish_gather(gi, after)
        j = i // 2
        if i % 2 == 0:
            wi = full["dn_w_in", j]
            w_all = jnp.concatenate([wi[:, :4 * d], padc(wi[:, 4 * d:4 * d + heads]), padc(wi[:, 4 * d + heads:])],
                                    axis=1)
            mix = dict(all=w_all, out=full["dn_w_out", j], conv_w=conv_full[j], a_log=padl(dn_a_log[j]),
                       dt_bias=padl(dn_dt_bias[j]), onorm_g=dn_onorm_g[j][None])
        else:
            mix = dict(qkv=full["sb_w_qkv", j], out=full["sb_w_out", j], q_g=sb_q_norm_g[j][None],
                       k_g=sb_k_norm_g[j][None])
        return mix, dict(w_in=full["ffn_w_in", i], w_out=full["ffn_w_out", i])

    g_dn, g_sb, g_ffn = [None] * ndn, [None] * nsb, [None] * depth
    reduces = {}

    def gw_in(g):
        ga = g["w_all"]
        return jnp.concatenate([ga[:, :4 * d], ga[:, 4 * d:4 * d + heads], ga[:, 4 * d + LANES:4 * d + LANES + heads]],
                               axis=1)
    layer_grad = dict(dn_w_in=lambda j: gw_in(g_dn[j]), dn_w_out=lambda j: g_dn[j]["w_out"],
                      sb_w_qkv=lambda j: g_sb[j]["w_qkv"], sb_w_out=lambda j: g_sb[j]["w_out"],
                      ffn_w_in=lambda l: g_ffn[l]["w_in"], ffn_w_out=lambda l: g_ffn[l]["w_out"])

    def layer_grads(i, after, g_mix, g_f):
        (g_dn if i % 2 == 0 else g_sb)[i // 2], g_ffn[i] = g_mix, g_f
        zero = jnp.zeros((), F32)
        for gi, layers in enumerate(groups):
            if layers and i == layers[0]:
                sl = kinds_of(layers)
                parts = [_to_chip_major(jnp.stack([layer_grad[n](j) for j in range(lo, hi)]), axis_of[n], nchips)
                         for n, lo, hi in sl]
                pairs = []
                for (n, _, _), p, fs in zip(sl, parts, _sibling_swap_halves(parts, "gr_pair%d" % gi)):
                    rh = fs.shape[2]
                    own = lax.dynamic_slice_in_dim(p, ac * rh, rh, axis=2)
                    pairs.append(_rowwise(lambda r, v: ([r[0].astype(F32) + r[1].astype(F32)], []),
                                          [two_d(own), two_d(fs)], [], [(fs.shape[-1], BF16)], [],
                                          name="gr_pair_add%d_%s" % (gi, n))[0].reshape(fs.shape))
                sems, bufs, zero = _ici_start(pairs, "slot", "gr_start%d" % gi)
                reduces[gi] = (sl, pairs, sems, bufs)
        return zero

    loss_local, grad_x, dmod, g_n1, g_n2 = _local_step(x[0], loss_target[0], mod, norm1_g, norm2_g, layer_weights,
                                                       layer_grads)
    loss = lax.psum(loss_local, ("x", "y", "c"))

    g_conv = jnp.stack([g["conv_w"] for g in g_dn])
    misc = jnp.concatenate([jnp.concatenate([g["onorm_g"] for g in g_dn], axis=1),
                            jnp.concatenate([g["a_log"] for g in g_dn], axis=1),
                            jnp.concatenate([g["dt_bias"] for g in g_dn], axis=1),
                            jnp.concatenate([g["q_norm_g"] for g in g_sb], axis=1),
                            jnp.concatenate([g["k_norm_g"] for g in g_sb], axis=1)], axis=1)
    misc = jnp.pad(misc, ((0, 0), (0, -misc.shape[1] % d))).reshape(-1, d)
    small = [dmod.reshape(-1, d), g_n1, g_n2, g_conv.reshape(-1, d), misc]
    small_rows = [s.shape[0] for s in small]
    pad_rows = -sum(small_rows) % 8
    pay3 = jnp.concatenate(small + [jnp.zeros((pad_rows, d), F32)], axis=0)
    nrow3 = pay3.shape[0]
    g3 = _allgather_small(pay3, "ag_small")
    summed = _rowwise(lambda r, v: ([], [_colsum(r[0])]), [g3.reshape(ndev, nrow3 * d)], [], [], [(1, nrow3 * d)],
                      name="small_sum")[0].reshape(nrow3, d)
    offs = [0]
    for n_ in small_rows:
        offs.append(offs[-1] + n_)
    grads = {}
    grads["ada_b"] = summed[offs[0]:offs[1]].reshape(depth, N_MOD * d)
    grads["norm1_g"] = summed[offs[1]:offs[2]]
    grads["norm2_g"] = summed[offs[2]:offs[3]]
    conv_sum = summed[offs[3]:offs[4]].reshape(ndn, DN_CONV, nchips * conv_cols)
    grads["dn_conv_w"] = lax.dynamic_slice_in_dim(conv_sum, chip * conv_cols, conv_cols, axis=2)
    mrow = summed[offs[4]:offs[5]].reshape(-1)
    o = 0
    grads["dn_onorm_g"] = mrow[o:o + ndn * DN_HEAD_DIM].reshape(ndn, DN_HEAD_DIM)
    o += ndn * DN_HEAD_DIM
    grads["dn_a_log"] = mrow[o:o + ndn * LANES].reshape(ndn, LANES)[:, :heads]
    o += ndn * LANES
    grads["dn_dt_bias"] = mrow[o:o + ndn * LANES].reshape(ndn, LANES)[:, :heads]
    o += ndn * LANES
    grads["sb_q_norm_g"] = mrow[o:o + nsb * SB_HEAD_DIM].reshape(nsb, SB_HEAD_DIM)
    o += nsb * SB_HEAD_DIM
    grads["sb_k_norm_g"] = mrow[o:o + nsb * SB_HEAD_DIM].reshape(nsb, SB_HEAD_DIM)
    dmod_all = g3.reshape(ndev, nrow3, d)[:, :small_rows[0]].reshape(ndev, depth, N_MOD * d)
    dmod_mine = lax.dynamic_slice_in_dim(dmod_all, chip * mod_cols, mod_cols, axis=2)
    dmod16 = jnp.pad(dmod_mine, ((0, 16 - ndev), (0, 0), (0, 0)))
    grads["ada_w"] = jnp.stack([_mm(cond16, dmod16[:, i], ta=True, name="ada_gw%d" % i) for i in range(depth)])

    pieces = {n: [] for n, _ in _BIG}
    for gi in reversed(range(len(groups))):
        if gi not in reduces:
            continue
        sl, pairs, sems, bufs = reduces[gi]
        lands = _ici_wait(sems, bufs, "slot", grad_x, "gr_wait%d" % gi)
        reduced = []
        for (n, _, _), land, pair in zip(sl, lands, pairs):
            own = lax.dynamic_index_in_dim(pair, chip, 0, keepdims=True)
            land = lax.dynamic_update_slice_in_dim(land, own, chip, axis=0)
            reduced.append(_sum_lead(land.reshape(nchips, -1, land.shape[-1]), "gr_chip_add%d_%s" % (gi, n)))
        for (n, lo, hi), mine, other in zip(sl, reduced, _share_halves(reduced, "gr_share%d" % gi)):
            shape = (hi - lo, -1, mine.shape[-1])
            pieces[n].append((lo, both_halves(mine.reshape(shape), other.reshape(shape), 1)))
    for n, _ in _BIG:
        grads[n] = jnp.concatenate([p for _, p in sorted(pieces[n], key=lambda t: t[0])], axis=0)

    delta, new_m, new_v = {}, {}, {}
    for n in names:
        delta[n], new_m[n], new_v[n] = _adamw(w[n], grads[n], mom[n], var[n], "adamw_" + n)
    return (loss, grad_x[None], *[grads[n] for n in names], *[delta[n] for n in names], *[new_m[n] for n in names],
            *[new_v[n] for n in names])
```

```python
import functools

import jax
import jax.numpy as jnp
from jax import lax
from jax.experimental import pallas as pl
from jax.experimental.pallas import tpu as pltpu

F32 = jnp.float32
BF16 = jnp.bfloat16
HI = lax.Precision.HIGHEST
MESH = pl.DeviceIdType.MESH

EPS = 1e-6
N_MOD = 6
DN_HEAD_DIM = 128
DN_CONV = 4
DN_CHUNK = 64
DN_GROUP = 8
DN_GATE_ROWS = 256
SB_HEAD_DIM = 64
SB_BLOCK = 128
SB_QBLOCK = 512
SB_PART = 128
LANES = 128
VMEM_LIMIT = 56 * 1024 * 1024
MM_BLOCK_BUDGET = 36 * 1024 * 1024
ROW_BLOCK_BUDGET = 20 * 1024 * 1024

ADAM_LR = 0.001
ADAM_B1 = 0.9
ADAM_B2 = 0.999
ADAM_EPS = 1e-08
ADAM_WD = 0.01
ADAM_STEP = 10


def _cparams(sem=None):
    return pltpu.CompilerParams(dimension_semantics=sem, vmem_limit_bytes=VMEM_LIMIT)


def _sigmoid(x):
    return 1.0 / (1.0 + jnp.exp(-x))


def _silu(x):
    return x * _sigmoid(x)


def _dsilu(x):
    s = _sigmoid(x)
    return s * (1.0 + x * (1.0 - s))


def _softplus(x):
    return jnp.maximum(x, 0.0) + jnp.log1p(jnp.exp(-jnp.abs(x)))


def _pick_tile(n, prefs):
    for t in prefs:
        if n % t == 0:
            return t
    return n


def _mm(a, b, *, ta=False, tb=False, out_dtype=F32, name):
    m = a.shape[1] if ta else a.shape[0]
    k = a.shape[0] if ta else a.shape[1]
    n = b.shape[0] if tb else b.shape[1]
    assert (b.shape[1] if tb else b.shape[0]) == k
    size = lambda dt: jnp.dtype(dt).itemsize
    best = None
    for tm in sorted({t for t in (m, 2048, 1024, 512, 256, 128) if m % t == 0 and (t % 128 == 0 or t == m)}):
        for tn in sorted({t for t in (n, 2816, 1408, 1024, 768, 512, 256, 128) if n % t == 0 and (t % 128 == 0 or t == n)}):
            need = 2 * (tm * k * size(a.dtype) + tn * k * size(b.dtype) + tm * tn * size(out_dtype))
            if need <= MM_BLOCK_BUDGET and (best is None or tm * tn > best[0] * best[1]):
                best = (tm, tn)
    tm, tn = best
    dims = (((0 if ta else 1,), (1 if tb else 0,)), ((), ()))

    def body(a_ref, b_ref, o_ref):
        av = a_ref[...].astype(BF16)
        bv = b_ref[...].astype(BF16)
        o_ref[...] = lax.dot_general(av, bv, dims, preferred_element_type=F32).astype(o_ref.dtype)

    a_spec = pl.BlockSpec((k, tm), lambda i, j: (0, i)) if ta else pl.BlockSpec((tm, k), lambda i, j: (i, 0))
    b_spec = pl.BlockSpec((tn, k), lambda i, j: (j, 0)) if tb else pl.BlockSpec((k, tn), lambda i, j: (0, j))
    return pl.pallas_call(
        body, grid=(m // tm, n // tn), in_specs=[a_spec, b_spec],
        out_specs=pl.BlockSpec((tm, tn), lambda i, j: (i, j)),
        out_shape=jax.ShapeDtypeStruct((m, n), out_dtype), name=name,
        compiler_params=_cparams(("parallel", "parallel")))(a, b)


def _rowwise(body, rows, vecs, outs, accs, *, name, tr=None):
    rows = [r if isinstance(r, tuple) else (r, r.shape[1], 0) for r in rows]
    nrows = rows[0][0].shape[0]
    if tr is None:
        per_row = sum(w * jnp.dtype(a.dtype).itemsize for a, w, _ in rows) + sum(
            w * jnp.dtype(dt).itemsize for w, dt in outs)
        tr = next((t for t in (2048, 1024, 512, 256) if nrows % t == 0 and 2 * t * per_row <= ROW_BLOCK_BUDGET), 256)
    tr = tr if nrows % tr == 0 else nrows
    nr, nv, no = len(rows), len(vecs), len(outs)

    def kern(*refs):
        r_in, v_in = refs[:nr], refs[nr:nr + nv]
        o_refs, a_refs = refs[nr + nv:nr + nv + no], refs[nr + nv + no:]
        res_o, res_a = body([r[...] for r in r_in], [v[...] for v in v_in])
        for o, val in zip(o_refs, res_o):
            o[...] = val.astype(o.dtype)
        if a_refs:
            @pl.when(pl.program_id(0) == 0)
            def _():
                for a in a_refs:
                    a[...] = jnp.zeros(a.shape, a.dtype)
            for a, val in zip(a_refs, res_a):
                a[...] += val

    in_specs = [pl.BlockSpec((tr, w), functools.partial(lambda i, cb: (i, cb), cb=cb)) for _, w, cb in rows]
    in_specs += [pl.BlockSpec(v.shape, functools.partial(lambda i, nd: (0,) * nd, nd=v.ndim)) for v in vecs]
    out_specs = [pl.BlockSpec((tr, w), lambda i: (i, 0)) for w, _ in outs]
    out_specs += [pl.BlockSpec(s, lambda i: (0, 0)) for s in accs]
    out_shape = [jax.ShapeDtypeStruct((nrows, w), dt) for w, dt in outs]
    out_shape += [jax.ShapeDtypeStruct(s, F32) for s in accs]
    res = pl.pallas_call(
        kern, grid=(nrows // tr,), in_specs=in_specs, out_specs=out_specs, out_shape=out_shape, name=name,
        compiler_params=_cparams(("arbitrary",) if accs else ("parallel",)))(*[r[0] for r in rows], *vecs)
    return res


def _colsum(v):
    return jnp.sum(v, axis=0, keepdims=True)


def _adaln_fwd(x, g, sc, sh, name):
    def body(r, v):
        (xv,), (gv, scv, shv) = r, v
        rs = lax.rsqrt(jnp.mean(xv * xv, axis=-1, keepdims=True) + EPS)
        return [(xv * rs) * gv * (1.0 + scv) + shv], []
    return _rowwise(body, [x], [g, sc, sh], [(x.shape[1], BF16)], [], name=name)[0]


def _adaln_bwd(x, dh, dxr, g, sc, name):
    d = x.shape[1]

    def body(r, v):
        (xv, dhv, dxv), (gv, scv) = r, v
        rs = lax.rsqrt(jnp.mean(xv * xv, axis=-1, keepdims=True) + EPS)
        nv = xv * rs
        dn = dhv * (gv * (1.0 + scv))
        dx = rs * (dn - nv * jnp.mean(dn * nv, axis=-1, keepdims=True)) + dxv
        dhn = dhv * nv
        return [dx], [_colsum(dhn * (1.0 + scv)), _colsum(dhn * gv), _colsum(dhv)]
    return _rowwise(body, [x, dh, dxr], [g, sc], [(d, F32)], [(1, d)] * 3, name=name)


def _resid_fwd(x, y, gt, name):
    def body(r, v):
        return [r[0] + v[0] * r[1]], []
    return _rowwise(body, [x, y], [gt], [(x.shape[1], F32)], [], name=name)[0]


def _resid_bwd(dx, y, gt, name):
    d = dx.shape[1]

    def body(r, v):
        return [v[0] * r[0]], [_colsum(r[0] * r[1])]
    return _rowwise(body, [dx, y], [gt], [(d, BF16)], [(1, d)], name=name)


def _swiglu_fwd(u, name):
    f = u.shape[1] // 2

    def body(r, v):
        uv = r[0]
        return [_silu(uv[:, :f]) * uv[:, f:]], []
    return _rowwise(body, [u], [], [(f, BF16)], [], name=name)[0]


def _swiglu_bwd(u, da, name):
    f = u.shape[1] // 2

    def body(r, v):
        uv, dav = r
        gate, up = uv[:, :f], uv[:, f:]
        return [jnp.concatenate([dav * up * _dsilu(gate), dav * _silu(gate)], axis=1)], []
    return _rowwise(body, [u, da], [], [(2 * f, BF16)], [], name=name)[0]


def _loss_fwd_bwd(y, tgt, name):
    d = y.shape[1]

    def body(r, v):
        err = r[0] - r[1]
        part = jnp.sum(jnp.sum(err * err, axis=1, keepdims=True), axis=0, keepdims=True) * (0.5 / d)
        return [err * (1.0 / d)], [jnp.broadcast_to(part, (1, LANES))]
    dy, acc = _rowwise(body, [y, tgt], [], [(d, F32)], [(1, LANES)], name=name)
    return acc[0, 0], dy


def _head_means(v):
    row = lax.broadcasted_iota(jnp.int32, (LANES, LANES), 0)
    col = lax.broadcasted_iota(jnp.int32, (LANES, LANES), 1)
    same = ((row // SB_HEAD_DIM) == (col // SB_HEAD_DIM)).astype(F32)
    parts = [jnp.dot(v[:, g * LANES:(g + 1) * LANES], same, precision=HI, preferred_element_type=F32)
             for g in range(v.shape[1] // LANES)]
    return jnp.concatenate(parts, axis=1) * (1.0 / SB_HEAD_DIM)


def _sb_norm_fwd(qkv, gq, gk, name):
    d = qkv.shape[1] // 3

    def body(r, v):
        return [x * lax.rsqrt(_head_means(x * x) + EPS) * g for x, g in zip(r, v)], []
    return _rowwise(body, [(qkv, d, 0), (qkv, d, 1)], [gq, gk], [(d, F32), (d, F32)], [], name=name)


def _sb_norm_bwd(qkv, dqn, dkn, dv, gq, gk, name):
    d = qkv.shape[1] // 3

    def body(r, v):
        outs, accs = [], []
        for x, dy, g in ((r[0], r[2], v[0]), (r[1], r[3], v[1])):
            rs = lax.rsqrt(_head_means(x * x) + EPS)
            nv = x * rs
            dn = dy * g
            outs.append(rs * (dn - nv * _head_means(dn * nv)))
            accs.append(_colsum(dy * nv))
        return [jnp.concatenate(outs + [r[4]], axis=1)], accs
    return _rowwise(body, [(qkv, d, 0), (qkv, d, 1), dqn, dkn, dv], [gq, gk], [(3 * d, BF16)], [(1, d), (1, d)],
                    name=name)


def _head_tiles(a):
    return [a[:, h * DN_HEAD_DIM:(h + 1) * DN_HEAD_DIM] for h in range(a.shape[1] // DN_HEAD_DIM)]


def _dn_post_fwd(o, proj, g, name):
    d = o.shape[1]

    def body(r, v):
        outs = []
        for ov, zv in zip(_head_tiles(r[0]), _head_tiles(r[1])):
            rs = lax.rsqrt(jnp.mean(ov * ov, axis=-1, keepdims=True) + EPS)
            outs.append(ov * rs * v[0] * _silu(zv))
        return [jnp.concatenate(outs, axis=1)], []
    return _rowwise(body, [o, (proj, d, 3)], [g], [(d, BF16)], [], name=name)[0]


def _dn_post_bwd(o, proj, don, g, name):
    d = o.shape[1]

    def body(r, v):
        dos, dzs, dg = [], [], jnp.zeros((1, DN_HEAD_DIM), F32)
        for ov, zv, dv in zip(_head_tiles(r[0]), _head_tiles(r[1]), _head_tiles(r[2])):
            rs = lax.rsqrt(jnp.mean(ov * ov, axis=-1, keepdims=True) + EPS)
            nv = ov * rs
            s = _silu(zv)
            dn = dv * v[0] * s
            dos.append(rs * (dn - nv * jnp.mean(dn * nv, axis=-1, keepdims=True)))
            dzs.append(dv * nv * v[0] * _dsilu(zv))
            dg = dg + _colsum(dv * nv * s)
        return [jnp.concatenate(dos, axis=1), jnp.concatenate(dzs, axis=1)], [dg]
    return _rowwise(body, [o, (proj, d, 3), don], [g], [(d, F32), (d, BF16)], [(1, DN_HEAD_DIM)], name=name)


def _chunk_tri(tr, upper):
    row = lax.broadcasted_iota(jnp.int32, (tr, tr), 0)
    col = lax.broadcasted_iota(jnp.int32, (tr, tr), 1)
    same = (row // DN_CHUNK) == (col // DN_CHUNK)
    return (same & ((row <= col) if upper else (row >= col))).astype(F32)


def _dn_gate_fwd(proj, colblk, a_log, dt_bias, name):
    def body(r, v):
        ab = r[0]
        a, b = ab[:, :LANES], ab[:, LANES:]
        g = -jnp.exp(v[0]) * _softplus(a + v[1])
        big_g = jnp.dot(_chunk_tri(g.shape[0], False), g, precision=HI, preferred_element_type=F32)
        return [big_g, _sigmoid(b)], []
    return _rowwise(body, [(proj, 2 * LANES, colblk)], [a_log, dt_bias], [(LANES, F32), (LANES, F32)], [], name=name,
                    tr=DN_GATE_ROWS)


def _dn_gate_bwd(proj, colblk, d_big_g, dbeta, a_log, dt_bias, name):
    def body(r, v):
        ab, dgc, dbt = r
        a, b = ab[:, :LANES], ab[:, LANES:]
        dg = jnp.dot(_chunk_tri(dgc.shape[0], True), dgc, precision=HI, preferred_element_type=F32)
        na = -jnp.exp(v[0])
        pre = a + v[1]
        da = dg * na * _sigmoid(pre)
        beta = _sigmoid(b)
        db = dbt * beta * (1.0 - beta)
        return [jnp.concatenate([da, db], axis=1)], [_colsum(dg * na * _softplus(pre)), _colsum(da)]
    return _rowwise(body, [(proj, 2 * LANES, colblk), d_big_g, dbeta], [a_log, dt_bias],
                    [(2 * LANES, BF16)], [(1, LANES), (1, LANES)], name=name, tr=DN_GATE_ROWS)


def _shift_rows(x, s):
    if s == 0:
        return x
    t = x.shape[0]
    row = lax.broadcasted_iota(jnp.int32, x.shape, 0)
    rolled = pltpu.roll(x, s % t, axis=0)
    return jnp.where((row >= s) if s > 0 else (row < t + s), rolled, 0.0)


def _rnd(x):
    return x.astype(BF16).astype(F32)


def _conv(x, c_ref):
    c = x * _rnd(c_ref[DN_CONV - 1:DN_CONV, :])
    for s in range(1, DN_CONV):
        c = c + _shift_rows(x, s) * _rnd(c_ref[DN_CONV - 1 - s:DN_CONV - s, :])
    return c


def _dn_prep_fwd(proj, conv_w, heads, name):
    t = proj.shape[0]
    d = heads * DN_HEAD_DIM

    def body(xq, xk, xv, cq, ck, cv, q_ref, k_ref, v_ref):
        for x_ref, c_ref, o_ref, norm in ((xq, cq, q_ref, True), (xk, ck, k_ref, True), (xv, cv, v_ref, False)):
            y = _silu(_conv(_rnd(x_ref[...]), c_ref))
            if norm:
                y = y * lax.rsqrt(jnp.sum(y * y, axis=-1, keepdims=True) + EPS)
            o_ref[...] = y

    xs = [pl.BlockSpec((t, DN_HEAD_DIM), functools.partial(lambda h, o: (0, h + o), o=o * heads)) for o in range(3)]
    cs = [pl.BlockSpec((DN_CONV, DN_HEAD_DIM), functools.partial(lambda h, o: (0, h + o), o=o * heads)) for o in range(3)]
    return pl.pallas_call(
        body, grid=(heads,), in_specs=xs + cs,
        out_specs=[pl.BlockSpec((t, DN_HEAD_DIM), lambda h: (0, h))] * 3,
        out_shape=[jax.ShapeDtypeStruct((t, d), F32)] * 3, name=name,
        compiler_params=_cparams(("parallel",)))(proj, proj, proj, conv_w, conv_w, conv_w)


def _dn_prep_bwd(proj, conv_w, dq, dk, dv, heads, name):
    t = proj.shape[0]
    d = heads * DN_HEAD_DIM

    def body(xq, xk, xv, cq, ck, cv, gq, gk, gv, oq, ok, ov, wq, wk, wv):
        for x_ref, c_ref, g_ref, o_ref, w_ref, norm in ((xq, cq, gq, oq, wq, True), (xk, ck, gk, ok, wk, True),
                                                        (xv, cv, gv, ov, wv, False)):
            x, dy = _rnd(x_ref[...]), g_ref[...]
            c = _conv(x, c_ref)
            if norm:
                y = _silu(c)
                rs = lax.rsqrt(jnp.sum(y * y, axis=-1, keepdims=True) + EPS)
                yn = y * rs
                dy = rs * (dy - yn * jnp.sum(dy * yn, axis=-1, keepdims=True))
            dc = _rnd(dy * _dsilu(c))
            dx = dc * _rnd(c_ref[DN_CONV - 1:DN_CONV, :])
            w_ref[DN_CONV - 1:DN_CONV, :] = _colsum(dc * x)
            for s in range(1, DN_CONV):
                dx = dx + _shift_rows(dc, -s) * _rnd(c_ref[DN_CONV - 1 - s:DN_CONV - s, :])
                w_ref[DN_CONV - 1 - s:DN_CONV - s, :] = _colsum(dc * _shift_rows(x, s))
            o_ref[...] = dx.astype(o_ref.dtype)

    xs = [pl.BlockSpec((t, DN_HEAD_DIM), functools.partial(lambda h, o: (0, h + o), o=o * heads)) for o in range(3)]
    cs = [pl.BlockSpec((DN_CONV, DN_HEAD_DIM), functools.partial(lambda h, o: (0, h + o), o=o * heads)) for o in range(3)]
    hs = pl.BlockSpec((t, DN_HEAD_DIM), lambda h: (0, h))
    ws = pl.BlockSpec((DN_CONV, DN_HEAD_DIM), lambda h: (0, h))
    return pl.pallas_call(
        body, grid=(heads,), in_specs=xs + cs + [hs] * 3, out_specs=[hs] * 3 + [ws] * 3,
        out_shape=[jax.ShapeDtypeStruct((t, d), BF16)] * 3 + [jax.ShapeDtypeStruct((DN_CONV, d), F32)] * 3, name=name,
        compiler_params=_cparams(("parallel",)))(proj, proj, proj, conv_w, conv_w, conv_w, dq, dk, dv)


X3 = "three bf16 passes"


def _bdot(a, b, dims, prec):
    if prec != X3:
        return lax.dot_general(a, b, dims, precision=prec, preferred_element_type=F32)
    a1, b1 = a.astype(BF16), b.astype(BF16)
    a2, b2 = (a - a1.astype(F32)).astype(BF16), (b - b1.astype(F32)).astype(BF16)
    dot = lambda x, y: lax.dot_general(x, y, dims, preferred_element_type=F32)
    return dot(a1, b1) + (dot(a1, b2) + dot(a2, b1))


def _bmm(a, b, prec=None):
    return _bdot(a, b, (((2,), (1,)), ((0,), (0,))), prec)


def _bmm_nt(a, b, prec=None):
    return _bdot(a, b, (((2,), (2,)), ((0,), (0,))), prec)


def _bmm_tn(a, b, prec=None):
    return _bdot(a, b, (((1,), (1,)), ((0,), (0,))), prec)


def _dn_local(qg, kg, vg, gc, gr, bt):
    c = qg.shape[1]
    row = lax.broadcasted_iota(jnp.int32, (c, c), 0)
    col = lax.broadcasted_iota(jnp.int32, (c, c), 1)
    incl, strict = (row >= col)[None], (row > col)[None]
    decay = jnp.where(incl, jnp.exp(jnp.where(incl, gc - gr, 0.0)), 0.0)
    kb = kg * bt
    vb = vg * bt
    m = _bmm_nt(kb.astype(BF16), kg.astype(BF16))
    a = jnp.where(strict, m * decay, 0.0)
    bp = -a
    tm = jnp.where((row == col)[None], 1.0, 0.0) + bp
    steps = max(1, (c - 1).bit_length()) - 1
    for _ in range(steps):
        bp = _bmm(bp, bp, X3)
        tm = tm + _bmm(tm, bp, X3)
    eg = jnp.exp(gc)
    glast = gc[:, c - 1:c, :]
    erel = jnp.exp(glast - gc)
    kbg = kb * eg
    qm = _bmm_nt(qg.astype(BF16), kg.astype(BF16))
    return dict(decay=decay, strict=strict, kb=kb, vb=vb, m=m, tm=tm, eg=eg, erel=erel, kbg=kbg, qm=qm,
                u=_bmm(tm, vb, X3), w=_bmm(tm, kbg, X3), qd=qg * eg, kt=kg * erel, gl=jnp.exp(glast))


def _dot(a, b):
    return jnp.dot(a.astype(BF16), b.astype(BF16), preferred_element_type=F32)


def _dot_nt(a, b):
    return lax.dot_general(a.astype(BF16), b.astype(BF16), (((1,), (1,)), ((), ())), preferred_element_type=F32)


def _dot_tn(a, b):
    return lax.dot_general(a.astype(BF16), b.astype(BF16), (((0,), (0,)), ((), ())), preferred_element_type=F32)


def _dn_chunk_specs(t, heads):
    n = t // DN_CHUNK
    hs = pl.BlockSpec((t, DN_HEAD_DIM), lambda h: (0, h))
    gcs = pl.BlockSpec((1, n, DN_CHUNK, 1), lambda h: (h, 0, 0, 0))
    grs = pl.BlockSpec((1, n, 1, DN_CHUNK), lambda h: (h, 0, 0, 0))
    ss = pl.BlockSpec((1, n, DN_HEAD_DIM, DN_HEAD_DIM), lambda h: (h, 0, 0, 0))
    return n, hs, gcs, grs, ss


def _dn_chunk_fwd(q, k, v, gc, gr, bc, name):
    t, d = q.shape
    heads = d // DN_HEAD_DIM
    c, dk = DN_CHUNK, DN_HEAD_DIM
    n, hs, gcs, grs, ss = _dn_chunk_specs(t, heads)
    nb = min(DN_GROUP, n)
    scale = dk ** -0.5

    def body(q_ref, k_ref, v_ref, gc_ref, gr_ref, b_ref, o_ref, s_ref, u_s, w_s, at_s, qd_s, kt_s, gl_s):
        def group(gi, carry):
            r0 = pl.multiple_of(gi * (nb * c), nb * c)
            n0 = gi * nb
            ld = lambda ref: ref[pl.ds(r0, nb * c), :].reshape(nb, c, dk)
            loc = _dn_local(ld(q_ref) * scale, ld(k_ref), ld(v_ref), gc_ref[0, pl.ds(n0, nb)],
                            gr_ref[0, pl.ds(n0, nb)], b_ref[0, pl.ds(n0, nb)])
            u_s[pl.ds(n0, nb)] = loc["u"]
            w_s[pl.ds(n0, nb)] = loc["w"]
            at_s[pl.ds(n0, nb)] = loc["qm"] * loc["decay"]
            qd_s[pl.ds(n0, nb)] = loc["qd"]
            kt_s[pl.ds(n0, nb)] = loc["kt"]
            gl_s[pl.ds(n0, nb)] = loc["gl"]
            return carry
        lax.fori_loop(0, n // nb, group, 0)

        def chunk(i, s):
            s_ref[0, i] = s
            vnew = u_s[i] - _dot(w_s[i], s)
            o = _dot(qd_s[i], s) + _dot(at_s[i], vnew)
            o_ref[pl.ds(pl.multiple_of(i * c, c), c), :] = o
            return s * gl_s[i] + _dot_tn(kt_s[i], vnew)
        lax.fori_loop(0, n, chunk, jnp.zeros((dk, dk), F32))

    scratch = [pltpu.VMEM((n, c, dk), F32), pltpu.VMEM((n, c, dk), F32), pltpu.VMEM((n, c, c), F32),
               pltpu.VMEM((n, c, dk), F32), pltpu.VMEM((n, c, dk), F32), pltpu.VMEM((n, 1, 1), F32)]
    return pl.pallas_call(
        body, grid=(heads,), in_specs=[hs, hs, hs, gcs, grs, gcs], out_specs=[hs, ss],
        out_shape=[jax.ShapeDtypeStruct((t, d), F32), jax.ShapeDtypeStruct((heads, n, dk, dk), F32)],
        scratch_shapes=scratch, name=name, compiler_params=_cparams(("parallel",)))(q, k, v, gc, gr, bc)


def _dn_chunk_bwd(q, k, v, gc, gr, bc, s_all, do, name):
    t, d = q.shape
    heads = d // DN_HEAD_DIM
    c, dk = DN_CHUNK, DN_HEAD_DIM
    n, hs, gcs, grs, ss = _dn_chunk_specs(t, heads)
    nb = min(DN_GROUP, n)
    scale = dk ** -0.5

    def body(q_ref, k_ref, v_ref, gc_ref, gr_ref, b_ref, s_ref, do_ref,
             dq_ref, dk_ref, dv_ref, dgc_ref, dgr_ref, db_ref,
             u_s, w_s, att_s, qd_s, kt_s, gl_s, du_s, dw_s, dat_s, dqd_s, dkt_s, dgl_s):
        def load_group(gi):
            r0 = pl.multiple_of(gi * (nb * c), nb * c)
            n0 = gi * nb
            ld = lambda ref: ref[pl.ds(r0, nb * c), :].reshape(nb, c, dk)
            qg, kg, vg = ld(q_ref) * scale, ld(k_ref), ld(v_ref)
            gcv, grv, bt = gc_ref[0, pl.ds(n0, nb)], gr_ref[0, pl.ds(n0, nb)], b_ref[0, pl.ds(n0, nb)]
            return r0, n0, qg, kg, vg, gcv, grv, bt, _dn_local(qg, kg, vg, gcv, grv, bt)

        def group_a(gi, carry):
            _, n0, qg, kg, _, gcv, grv, _, loc = load_group(gi)
            row = lax.broadcasted_iota(jnp.int32, (c, c), 0)
            col = lax.broadcasted_iota(jnp.int32, (c, c), 1)
            upper = (col >= row)[None]
            decay_t = jnp.where(upper, jnp.exp(jnp.where(upper, grv - gcv, 0.0)), 0.0)
            u_s[pl.ds(n0, nb)] = loc["u"]
            w_s[pl.ds(n0, nb)] = loc["w"]
            att_s[pl.ds(n0, nb)] = _bmm_nt(kg.astype(BF16), qg.astype(BF16)) * decay_t
            qd_s[pl.ds(n0, nb)] = loc["qd"]
            kt_s[pl.ds(n0, nb)] = loc["kt"]
            gl_s[pl.ds(n0, nb)] = loc["gl"]
            return carry
        lax.fori_loop(0, n // nb, group_a, 0)

        def chunk_b(it, ds_next):
            i = n - 1 - it
            s = s_ref[0, i]
            dov = do_ref[pl.ds(pl.multiple_of(i * c, c), c), :]
            w, kt = w_s[i], kt_s[i]
            vnew = u_s[i] - _dot(w, s)
            dvnew = _dot(att_s[i], dov) + _dot(kt, ds_next)
            du_s[i] = dvnew
            dw_s[i] = -_dot_nt(dvnew, s)
            dat_s[i] = _dot_nt(dov, vnew)
            dqd_s[i] = _dot_nt(dov, s)
            dkt_s[i] = _dot_nt(vnew, ds_next)
            dgl_s[i] = jnp.sum(jnp.sum(ds_next * s, axis=1, keepdims=True), axis=0, keepdims=True)
            return ds_next * gl_s[i] + _dot_tn(qd_s[i], dov) - _dot_tn(w, dvnew)
        lax.fori_loop(0, n, chunk_b, jnp.zeros((dk, dk), F32))

        def group_c(gi, carry):
            r0, n0, qg, kg, vg, gcv, grv, bt, loc = load_group(gi)
            sl = pl.ds(n0, nb)
            du, dw, dat, dqd, dkt, dgl = du_s[sl], dw_s[sl], dat_s[sl], dqd_s[sl], dkt_s[sl], dgl_s[sl]
            tm, decay, kb, kbg = loc["tm"], loc["decay"], loc["kb"], loc["kbg"]
            dvb = _bmm_tn(tm, du, X3)
            dkbg = _bmm_tn(tm, dw, X3)
            dt = _bmm_nt(du, loc["vb"], X3) + _bmm_nt(dw, kbg, X3)
            da = jnp.where(loc["strict"], -_bmm_tn(tm, _bmm_nt(dt, tm, X3), X3), 0.0)
            dms = (da * decay).astype(BF16)
            dqs = (dat * decay).astype(BF16)
            kgb = kg.astype(BF16)
            dkb = _bmm(dms, kgb) + dkbg * loc["eg"]
            dqt = _bmm(dqs, kgb) + dqd * loc["eg"]
            dkk = _bmm_tn(dms, kb.astype(BF16)) + _bmm_tn(dqs, qg.astype(BF16)) + dkt * loc["erel"] + dkb * bt
            e = (da * loc["m"] + dat * loc["qm"]) * decay
            lsum = lambda x: jnp.sum(x, axis=2, keepdims=True)
            dkt_kt = lsum(dkt * loc["kt"])
            dgcv = lsum(e) + lsum(dqd * loc["qd"]) - dkt_kt + lsum(dkbg * kbg)
            dglast = jnp.sum(dkt_kt, axis=1, keepdims=True) + dgl * loc["gl"]
            rowc = lax.broadcasted_iota(jnp.int32, (1, c, 1), 1)
            dgc_ref[0, sl] = dgcv + jnp.where(rowc == c - 1, dglast, 0.0)
            dgr_ref[0, sl] = -jnp.sum(e, axis=1, keepdims=True)
            db_ref[0, sl] = lsum(dkb * kg) + lsum(dvb * vg)
            rows = pl.ds(r0, nb * c)
            dq_ref[rows, :] = (dqt * scale).reshape(nb * c, dk)
            dk_ref[rows, :] = dkk.reshape(nb * c, dk)
            dv_ref[rows, :] = (dvb * bt).reshape(nb * c, dk)
            return carry
        lax.fori_loop(0, n // nb, group_c, 0)

    big = lambda: pltpu.VMEM((n, c, dk), F32)
    sq = lambda: pltpu.VMEM((n, c, c), F32)
    one = lambda: pltpu.VMEM((n, 1, 1), F32)
    scratch = [big(), big(), sq(), big(), big(), one(), big(), big(), sq(), big(), big(), one()]
    return pl.pallas_call(
        body, grid=(heads,), in_specs=[hs, hs, hs, gcs, grs, gcs, ss, hs], out_specs=[hs, hs, hs, gcs, grs, gcs],
        out_shape=[jax.ShapeDtypeStruct((t, d), F32)] * 3 + [
            jax.ShapeDtypeStruct((heads, n, c, 1), F32), jax.ShapeDtypeStruct((heads, n, 1, c), F32),
            jax.ShapeDtypeStruct((heads, n, c, 1), F32)],
        scratch_shapes=scratch, name=name, compiler_params=_cparams(("parallel",)))(q, k, v, gc, gr, bc, s_all, do)


def _dot01(x, m01):
    x1 = x.astype(BF16)
    r1 = x - x1.astype(F32)
    x2 = r1.astype(BF16)
    x3 = (r1 - x2.astype(F32)).astype(BF16)
    dot = lambda a: jnp.dot(a, m01, preferred_element_type=F32)
    return (dot(x1) + dot(x2)) + dot(x3)


def _sb_logs(z, mask):
    rows, bk = z.shape
    row = lax.broadcasted_iota(jnp.int32, (bk, bk), 0)
    col = lax.broadcasted_iota(jnp.int32, (bk, bk), 1)
    later = (row > col).astype(BF16)
    lm, ls, cs = [], [], []
    for p in [slice(p, p + SB_PART) for p in range(0, rows, SB_PART)]:
        lsm = -_softplus(z[p])
        lm.append(lsm if mask is None else jnp.where(mask[p], lsm, 0.0))
        ls.append(z[p] + lsm)
        cs.append(_dot01(lm[-1], later))
    cat = lambda xs: jnp.concatenate(xs, axis=0)
    return cat(ls), cat(cs), jnp.sum(cat(lm), axis=1, keepdims=True)


def _sb_masks(bq, bk):
    row = lax.broadcasted_iota(jnp.int32, (2 * bq, bk), 0)
    row = jnp.where(row >= bq, row - bq, row)
    col = lax.broadcasted_iota(jnp.int32, (2 * bq, bk), 1)
    return [(col + dd * bk) < row for dd in range(bq // bk)]


def _sb_sweep(bq, bk, qi, fetch, visit, carry):
    nd = bq // bk
    masks = _sb_masks(bq, bk)
    ahead = lambda j: fetch(jnp.maximum(j - 1, 0))
    pre = fetch(qi * nd + nd - 1)
    for dd in reversed(range(nd)):
        nxt = ahead(qi * nd + dd)
        carry = visit(qi * nd + dd, masks[dd], carry, pre)
        pre = nxt

    def below(it, c):
        j = qi * nd - 1 - it
        nxt = ahead(j)
        return visit(j, None, c[0], c[1]), nxt
    return lax.fori_loop(0, qi * nd, below, (carry, pre))[0]


def _sb_sweep2(bq, bk, qi, fetch, first, second, carry):
    nd = bq // bk
    masks = _sb_masks(bq, bk)
    ahead = lambda j: fetch(jnp.maximum(j - 1, 0))
    pre = fetch(qi * nd + nd - 1)
    r = jnp.zeros((2 * bq, 1), F32)
    pend = None
    for dd in reversed(range(nd)):
        j = qi * nd + dd
        nxt = ahead(j)
        ls, cs, rsum, extras = first(masks[dd], pre)
        if pend is not None:
            carry = second(pend[0], carry, *pend[1])
        pend = (j, (jnp.where(masks[dd], ls + cs + r, -1e30),) + extras)
        r, pre = r + rsum, nxt

    def below(it, c):
        carry, r, pre, pj, pargs = c
        j = qi * nd - 1 - it
        nxt = ahead(j)
        ls, cs, rsum, extras = first(None, pre)
        carry = second(pj, carry, *pargs)
        return carry, r + rsum, nxt, j, (ls + cs + r,) + extras
    carry, _, _, pj, pargs = lax.fori_loop(0, qi * nd, below, (carry, r, pre) + pend)
    return second(pj, carry, *pargs)


def _stack_heads(a, h0):
    return jnp.concatenate([jnp.where(h0, a, 0.0), jnp.where(h0, 0.0, a)], axis=0).astype(BF16)


def _side_by_side(a, bq):
    return jnp.concatenate([a[:bq], a[bq:]], axis=1)


def _sb_specs(t, d, bq, vcol):
    qs = pl.BlockSpec((bq, LANES), lambda g, i: (i, g))
    ks = pl.BlockSpec((t, LANES), lambda g, i: (0, g))
    vs = pl.BlockSpec((t, LANES), lambda g, i: (0, g + vcol))
    return qs, ks, vs


def _sb_fwd(qn, kn, qkv, name):
    t, d = qn.shape
    bq, bk = min(SB_QBLOCK, t), min(SB_BLOCK, t)
    scale = SB_HEAD_DIM ** -0.5
    nt = (((1,), (1,)), ((), ()))

    def body(q_ref, k_ref, v_ref, o_ref):
        qi = pl.program_id(1)
        h0 = lax.broadcasted_iota(jnp.int32, (1, LANES), 1) < SB_HEAD_DIM
        q2 = _stack_heads(q_ref[...] * scale, h0)

        tile_rows = lambda j: pl.ds(pl.multiple_of(j * bk, bk), bk)

        def fetch(j):
            return lax.dot_general(q2, k_ref[tile_rows(j), :].astype(BF16), nt, preferred_element_type=F32)

        def first(mask, z):
            return _sb_logs(z, mask) + ((),)

        def second(j, acc, loga):
            a2 = _side_by_side(jnp.exp(loga).astype(BF16), bq)
            return acc + jnp.dot(a2, _stack_heads(v_ref[tile_rows(j), :], h0), preferred_element_type=F32)
        acc = _sb_sweep2(bq, bk, qi, fetch, first, second, jnp.zeros((bq, LANES), F32))
        o_ref[...] = acc.astype(o_ref.dtype)

    qs, ks, vs = _sb_specs(t, d, bq, 2 * d // LANES)
    return pl.pallas_call(
        body, grid=(d // LANES, t // bq), in_specs=[qs, ks, vs], out_specs=qs,
        out_shape=jax.ShapeDtypeStruct((t, d), BF16), name=name,
        compiler_params=_cparams(("parallel", "parallel")))(qn, kn, qkv)


def _sb_bwd(qn, kn, qkv, do, name):
    t, d = qn.shape
    bq, bk = min(SB_QBLOCK, t), min(SB_BLOCK, t)
    scale = SB_HEAD_DIM ** -0.5
    nt = (((1,), (1,)), ((), ()))

    def body(q_ref, k_ref, v_ref, do_ref, dq_ref, dk_ref, dv_ref, p_s, sg_s):
        qi = pl.program_id(1)

        @pl.when(qi == 0)
        def _():
            dk_ref[...] = jnp.zeros(dk_ref.shape, F32)
            dv_ref[...] = jnp.zeros(dv_ref.shape, F32)

        h0 = lax.broadcasted_iota(jnp.int32, (1, LANES), 1) < SB_HEAD_DIM
        q2 = _stack_heads(q_ref[...] * scale, h0)
        do2 = _stack_heads(do_ref[...], h0)
        row = lax.broadcasted_iota(jnp.int32, (bk, bk), 0)
        col = lax.broadcasted_iota(jnp.int32, (bk, bk), 1)
        later_incl = (row >= col).astype(BF16)
        zero = jnp.zeros((2 * bq, 1), F32)
        tn = (((0,), (0,)), ((), ()))

        tile_rows = lambda j: pl.ds(pl.multiple_of(j * bk, bk), bk)

        def fetch1(j):
            rows = tile_rows(j)
            return (lax.dot_general(q2, k_ref[rows, :].astype(BF16), nt, preferred_element_type=F32),
                    lax.dot_general(do2, v_ref[rows, :].astype(BF16), nt, preferred_element_type=F32))

        def first1(mask, z_da):
            ls, cs, rsum = _sb_logs(z_da[0], mask)
            return ls, cs, rsum, (ls, z_da[1])

        def second1(j, sp, loga, ls, da):
            a = jnp.exp(loga)
            p = a * da
            p_s[j] = p
            sg_s[j] = jnp.exp(ls)
            dv_ref[tile_rows(j), :] += lax.dot_general(a.astype(BF16), do2, tn, preferred_element_type=F32)
            return sp + jnp.sum(p, axis=1, keepdims=True)
        total = _sb_sweep2(bq, bk, qi, fetch1, first1, second1, zero)

        def fetch2(j):
            return _dot01(p_s[j], later_incl)

        def visit2(j, mask, carry, later_p):
            dq, sp = carry
            rows = tile_rows(j)
            p, sg = p_s[j], sg_s[j]
            pref = total - sp - later_p
            dz = p * (1.0 - sg) - pref * sg
            if mask is not None:
                dz = jnp.where(mask, dz, 0.0)
            dz = dz.astype(BF16)
            dk_ref[rows, :] += lax.dot_general(dz, q2, tn, preferred_element_type=F32)
            dq = dq + jnp.dot(_side_by_side(dz, bq), _stack_heads(k_ref[rows, :], h0), preferred_element_type=F32)
            return dq, sp + jnp.sum(p, axis=1, keepdims=True)
        dq, _ = _sb_sweep(bq, bk, qi, fetch2, visit2, (jnp.zeros((bq, LANES), F32), zero))
        dq_ref[...] = dq * scale

    qs, ks, vs = _sb_specs(t, d, bq, 2 * d // LANES)
    shp = jax.ShapeDtypeStruct((t, d), F32)
    scratch = [pltpu.VMEM((t // bk, 2 * bq, bk), F32), pltpu.VMEM((t // bk, 2 * bq, bk), F32)]
    return pl.pallas_call(
        body, grid=(d // LANES, t // bq), in_specs=[qs, ks, vs, qs], out_specs=[qs, ks, ks],
        out_shape=[shp, shp, shp], scratch_shapes=scratch, name=name,
        compiler_params=_cparams(("parallel", "arbitrary")))(qn, kn, qkv, do)


def _adamw(w, g, m, v, name):
    shape = w.shape
    two_d = lambda a: a.reshape(-1, shape[-1])
    c1 = 1.0 - ADAM_B1 ** ADAM_STEP
    c2 = 1.0 - ADAM_B2 ** ADAM_STEP

    def body(r, _):
        wv, gv, mv, vv = r
        mn = ADAM_B1 * mv + (1.0 - ADAM_B1) * gv
        vn = ADAM_B2 * vv + (1.0 - ADAM_B2) * (gv * gv)
        delta = -ADAM_LR * ((mn / c1) / (jnp.sqrt(vn / c2) + ADAM_EPS) + ADAM_WD * wv)
        return [delta, mn, vn], []
    width = shape[-1]
    res = _rowwise(body, [two_d(w), two_d(g), two_d(m), two_d(v)], [], [(width, F32)] * 3, [], name=name)
    return [r.reshape(shape) for r in res]


def _dn_layer_fwd(h1, w, conv_w, a_log, dt_bias, onorm_g, tag):
    t, d = h1.shape
    heads = d // DN_HEAD_DIM
    n = t // DN_CHUNK
    proj = _mm(h1, w["all"], name=tag + "_proj")
    qn, kn, vv = _dn_prep_fwd(proj, conv_w, heads, tag + "_prep")
    big_g, beta = _dn_gate_fwd(proj, 4 * d // (2 * LANES), a_log, dt_bias, tag + "_gate")
    gt_ = big_g[:, :heads].T.reshape(heads, n, DN_CHUNK)
    gc, gr = gt_[..., None], gt_[:, :, None, :]
    bc = beta[:, :heads].T.reshape(heads, n, DN_CHUNK)[..., None]
    o, s_all = _dn_chunk_fwd(qn, kn, vv, gc, gr, bc, tag + "_chunk")
    on = _dn_post_fwd(o, proj, onorm_g, tag + "_post")
    y = _mm(on, w["out"], name=tag + "_out")
    return y, dict(proj=proj, qn=qn, kn=kn, v=vv, gc=gc, gr=gr, bc=bc, o=o, s_all=s_all, on=on)


def _dn_layer_bwd(dy, h1, w, conv_w, a_log, dt_bias, onorm_g, sv, tag):
    t, d = h1.shape
    heads = d // DN_HEAD_DIM
    don = _mm(dy, w["out"], tb=True, name=tag + "_dout")
    g_out = _mm(sv["on"], dy, ta=True, out_dtype=BF16, name=tag + "_gwout")
    do, dz, g_on = _dn_post_bwd(sv["o"], sv["proj"], don, onorm_g, tag + "_dpost")
    dq, dk, dv, dgc, dgr, dbc = _dn_chunk_bwd(sv["qn"], sv["kn"], sv["v"], sv["gc"], sv["gr"], sv["bc"], sv["s_all"],
                                              do, tag + "_dchunk")
    pad = lambda a: jnp.pad(a.reshape(heads, t).T, ((0, 0), (0, LANES - heads)))
    d_big_g = pad(dgc) + pad(dgr)
    dab, g_alog, g_dt = _dn_gate_bwd(sv["proj"], 4 * d // (2 * LANES), d_big_g, pad(dbc), a_log, dt_bias, tag + "_dgate")
    dxq, dxk, dxv, wq, wk, wv = _dn_prep_bwd(sv["proj"], conv_w, dq, dk, dv, heads, tag + "_dprep")
    dproj = jnp.concatenate([dxq, dxk, dxv, dz, dab], axis=1)
    dh1 = _mm(dproj, w["all"], tb=True, name=tag + "_dh")
    g_all = _mm(h1, dproj, ta=True, out_dtype=BF16, name=tag + "_gwin")
    grads = dict(w_all=g_all, w_out=g_out, conv_w=jnp.concatenate([wq, wk, wv], axis=1), a_log=g_alog, dt_bias=g_dt,
                 onorm_g=g_on)
    return dh1, grads


def _sb_layer_fwd(h1, w, q_g, k_g, tag):
    t, d = h1.shape
    heads = d // SB_HEAD_DIM
    qkv = _mm(h1, w["qkv"], name=tag + "_proj")
    gq, gk = jnp.tile(q_g, (1, heads)), jnp.tile(k_g, (1, heads))
    qn, kn = _sb_norm_fwd(qkv, gq, gk, tag + "_norm")
    o = _sb_fwd(qn, kn, qkv, tag + "_attn")
    y = _mm(o, w["out"], name=tag + "_out")
    return y, dict(qkv=qkv, qn=qn, kn=kn, o=o, gq=gq, gk=gk)


def _sb_layer_bwd(dy, h1, w, q_g, k_g, sv, tag):
    t, d = h1.shape
    heads = d // SB_HEAD_DIM
    do = _mm(dy, w["out"], tb=True, name=tag + "_dout")
    g_out = _mm(sv["o"], dy, ta=True, out_dtype=BF16, name=tag + "_gwout")
    dqn, dkn, dv = _sb_bwd(sv["qn"], sv["kn"], sv["qkv"], do, tag + "_dattn")
    dqkv, g_q, g_k = _sb_norm_bwd(sv["qkv"], dqn, dkn, dv, sv["gq"], sv["gk"], tag + "_dnorm")
    fold = lambda g: jnp.sum(g.reshape(heads, SB_HEAD_DIM), axis=0, keepdims=True)
    dh1 = _mm(dqkv, w["qkv"], tb=True, name=tag + "_dh")
    g_qkv = _mm(h1, dqkv, ta=True, out_dtype=BF16, name=tag + "_gwin")
    return dh1, dict(w_qkv=g_qkv, w_out=g_out, q_norm_g=fold(g_q), k_norm_g=fold(g_k))


def _local_step(x, tgt, mod, norm1_g, norm2_g, layer_weights, layer_grads):
    depth = mod.shape[0]
    d = x.shape[1]
    saved = []
    for i in range(depth):
        mix_w, ffn_w = layer_weights(i, x)
        mv = [mod[i:i + 1, j * d:(j + 1) * d] for j in range(N_MOD)]
        sh1, sc1, gt1, sh2, sc2, gt2 = mv
        tag = "l%d" % i
        h1 = _adaln_fwd(x, norm1_g[i:i + 1], sc1, sh1, tag + "_ln1")
        if i % 2 == 0:
            p = mix_w
            y, sv = _dn_layer_fwd(h1, p, p["conv_w"], p["a_log"], p["dt_bias"], p["onorm_g"], tag + "_dn")
        else:
            p = mix_w
            y, sv = _sb_layer_fwd(h1, p, p["q_g"], p["k_g"], tag + "_sb")
        x1 = _resid_fwd(x, y, gt1, tag + "_res1")
        h2 = _adaln_fwd(x1, norm2_g[i:i + 1], sc2, sh2, tag + "_ln2")
        u = _mm(h2, ffn_w["w_in"], name=tag + "_ffn_in")
        a = _swiglu_fwd(u, tag + "_swiglu")
        y2 = _mm(a, ffn_w["w_out"], name=tag + "_ffn_out")
        x2 = _resid_fwd(x1, y2, gt2, tag + "_res2")
        saved.append(dict(x0=x, h1=h1, y=y, mix=sv, x1=x1, h2=h2, u=u, a=a, y2=y2, mix_w=mix_w, ffn_w=ffn_w))
        x = x2

    loss, dx = _loss_fwd_bwd(x, tgt, "loss")

    dmod, dn1, dn2 = [None] * depth, [None] * depth, [None] * depth
    zero = jnp.zeros((), F32)
    for i in reversed(range(depth)):
        s = saved[i]
        mv = [mod[i:i + 1, j * d:(j + 1) * d] + zero for j in range(N_MOD)]
        sh1, sc1, gt1, sh2, sc2, gt2 = mv
        tag = "l%d" % i
        ffn_w, p = s["ffn_w"], s["mix_w"]
        dy2, dgt2 = _resid_bwd(dx, s["y2"], gt2, tag + "_dres2")
        da = _mm(dy2, ffn_w["w_out"], tb=True, name=tag + "_dffn_a")
        g_wout = _mm(s["a"], dy2, ta=True, out_dtype=BF16, name=tag + "_gffn_out")
        du = _swiglu_bwd(s["u"], da, tag + "_dswiglu")
        dh2 = _mm(du, ffn_w["w_in"], tb=True, name=tag + "_dffn_h")
        g_win = _mm(s["h2"], du, ta=True, out_dtype=BF16, name=tag + "_gffn_in")
        dx, dg2, dsc2, dsh2 = _adaln_bwd(s["x1"], dh2, dx, norm2_g[i:i + 1], sc2, tag + "_dln2")
        dy, dgt1 = _resid_bwd(dx, s["y"], gt1, tag + "_dres1")
        if i % 2 == 0:
            dh1, g_mix = _dn_layer_bwd(dy, s["h1"], p, p["conv_w"], p["a_log"], p["dt_bias"], p["onorm_g"], s["mix"],
                                       tag + "_dn")
        else:
            dh1, g_mix = _sb_layer_bwd(dy, s["h1"], p, p["q_g"], p["k_g"], s["mix"], tag + "_sb")
        dx, dg1, dsc1, dsh1 = _adaln_bwd(s["x0"], dh1, dx, norm1_g[i:i + 1], sc1, tag + "_dln1")
        dmod[i] = jnp.concatenate([dsh1, dsc1, dgt1, dsh2, dsc2, dgt2], axis=1)
        dn1[i], dn2[i] = dg1, dg2
        zero = layer_grads(i, dx, g_mix, dict(w_in=g_win, w_out=g_wout))
    return loss, dx, jnp.concatenate(dmod, axis=0), jnp.concatenate(dn1, axis=0), jnp.concatenate(dn2, axis=0)


def _axes():
    return lax.axis_index("x"), lax.axis_index("y"), lax.axis_index("c")


def _remote(src, dst, send_sem, recv_sem, dev):
    return pltpu.make_async_remote_copy(src_ref=src, dst_ref=dst, send_sem=send_sem, recv_sem=recv_sem,
                                        device_id=dev, device_id_type=MESH)


def _other_chips(x, y):
    return [(1 - x, y), (x, 1 - y), (1 - x, 1 - y)]


def _allgather_small(v, name):
    m, n = v.shape

    def body(x_ref, out_ref, send_sems, recv_sems, local_sem):
        x, y, c = _axes()
        me, sibling = (x, y, c), (x, y, 1 - c)
        chips = _other_chips(x, y)

        def rows(px, py, pc):
            return out_ref.at[pl.ds((4 * px + 2 * py + pc) * m, m), :]

        def copy(k, block, to, src=None):
            return _remote(rows(*block) if src is None else src, rows(*block), send_sems.at[k], recv_sems.at[k], to)

        mine = pltpu.make_async_copy(x_ref, rows(*me), local_sem)
        mine.start()
        first = [copy(0, me, sibling, src=x_ref)]
        first += [copy(1 + j, me, (*chip, c), src=x_ref) for j, chip in enumerate(chips)]
        for cp in first:
            cp.start()
        passed = [copy(4 + j, (*chip, c), sibling) for j, chip in enumerate(chips)]
        for j, chip in enumerate(chips):
            copy(1 + j, (*chip, c), me).wait_recv()
            passed[j].start()
        copy(0, sibling, me).wait_recv()
        for j, chip in enumerate(chips):
            copy(4 + j, (*chip, 1 - c), me).wait_recv()
        for cp in first + passed:
            cp.wait_send()
        mine.wait()

    return pl.pallas_call(
        body, out_shape=jax.ShapeDtypeStruct((8 * m, n), v.dtype),
        in_specs=[pl.BlockSpec(memory_space=pltpu.VMEM)], out_specs=pl.BlockSpec(memory_space=pltpu.VMEM),
        scratch_shapes=[pltpu.SemaphoreType.DMA((7,)), pltpu.SemaphoreType.DMA((7,)), pltpu.SemaphoreType.DMA],
        name=name, compiler_params=pltpu.CompilerParams(vmem_limit_bytes=VMEM_LIMIT))(v)


def _half(ref, h, rh):
    return ref.at[(slice(None),) * (len(ref.shape) - 2) + (pl.ds(h * rh, rh), slice(None))]


def _hbm_call(body, ins, out_shapes, n_sems, n_local, name):
    hbm = pl.BlockSpec(memory_space=pltpu.HBM)
    scratch = [pltpu.SemaphoreType.DMA((n_sems,)), pltpu.SemaphoreType.DMA((n_sems,))]
    if n_local:
        scratch.append(pltpu.SemaphoreType.DMA((n_local,)))
    return pl.pallas_call(body, out_shape=out_shapes, in_specs=[hbm] * len(ins), out_specs=[hbm] * len(out_shapes),
                          scratch_shapes=scratch, name=name)(*ins)


def _allgather_weights(shards, name):
    n = len(shards)

    def body(*refs):
        p_refs, out_refs, (send_sems, recv_sems, local_sems) = refs[:n], refs[n:2 * n], refs[2 * n:]
        x, y, c = _axes()
        sibling = (x, y, 1 - c)
        chips = _other_chips(x, y)
        rh = [s.shape[1] // 2 for s in shards]
        blk = lambda i, cx, cy, h: _half(out_refs[i].at[2 * cx + cy], h, rh[i])
        mine = [pltpu.make_async_copy(p_refs[i], out_refs[i].at[2 * x + y], local_sems.at[i]) for i in range(n)]
        for cp in mine:
            cp.start()
        first = [_remote(_half(p_refs[i], c, rh[i]), blk(i, x, y, c), send_sems.at[6 * i + j], recv_sems.at[6 * i + j],
                         (*chip, c)) for i in range(n) for j, chip in enumerate(chips)]
        for cp in first:
            cp.start()
        passed = []
        for j, chip in enumerate(chips):
            for i in range(n):
                _remote(_half(p_refs[i], c, rh[i]), blk(i, *chip, c), send_sems.at[6 * i + j], recv_sems.at[6 * i + j],
                        sibling).wait_recv()
                passed.append(_remote(blk(i, *chip, c), blk(i, *chip, c), send_sems.at[6 * i + 3 + j],
                                      recv_sems.at[6 * i + 3 + j], sibling))
                passed[-1].start()
        for j, chip in enumerate(chips):
            for i in range(n):
                _remote(_half(p_refs[i], c, rh[i]), blk(i, *chip, 1 - c), send_sems.at[6 * i + 3 + j],
                        recv_sems.at[6 * i + 3 + j], sibling).wait_recv()
        for cp in first + passed:
            cp.wait_send()
        for cp in mine:
            cp.wait()

    outs = [jax.ShapeDtypeStruct((4,) + s.shape, s.dtype) for s in shards]
    return _hbm_call(body, shards, outs, 6 * n, n, name)


def _sibling_swap_halves(arrs, name):
    n = len(arrs)

    def body(*refs):
        v_refs, out_refs, (send_sems, recv_sems) = refs[:n], refs[n:2 * n], refs[2 * n:]
        x, y, c = _axes()
        cps = [_remote(_half(v_refs[i], 1 - c, arrs[i].shape[-2] // 2), out_refs[i], send_sems.at[i], recv_sems.at[i],
                       (x, y, 1 - c)) for i in range(n)]
        for cp in cps:
            cp.start()
        for cp in cps:
            cp.wait()

    outs = [jax.ShapeDtypeStruct(a.shape[:-2] + (a.shape[-2] // 2, a.shape[-1]), a.dtype) for a in arrs]
    return _hbm_call(body, arrs, outs, n, 0, name)


def _chip_scatter(arrs, name):
    n = len(arrs)

    def body(*refs):
        v_refs, out_refs, (send_sems, recv_sems, local_sems) = refs[:n], refs[n:2 * n], refs[2 * n:]
        x, y, c = _axes()
        me = 2 * x + y
        chips = _other_chips(x, y)
        mine = [pltpu.make_async_copy(v_refs[i].at[me], out_refs[i].at[me], local_sems.at[i]) for i in range(n)]
        for cp in mine:
            cp.start()
        sends = [_remote(v_refs[i].at[2 * cx + cy], out_refs[i].at[me], send_sems.at[3 * i + j], recv_sems.at[3 * i + j],
                         (cx, cy, c)) for i in range(n) for j, (cx, cy) in enumerate(chips)]
        for cp in sends:
            cp.start()
        for i in range(n):
            for j, (cx, cy) in enumerate(chips):
                _remote(v_refs[i].at[me], out_refs[i].at[2 * cx + cy], send_sems.at[3 * i + j], recv_sems.at[3 * i + j],
                        (cx, cy, c)).wait_recv()
        for cp in sends:
            cp.wait_send()
        for cp in mine:
            cp.wait()

    outs = [jax.ShapeDtypeStruct(a.shape, a.dtype) for a in arrs]
    return _hbm_call(body, arrs, outs, 3 * n, n, name)


def _ici_src(ref, mode, slot, c):
    return _half(ref, c, ref.shape[-2] // 2) if mode == "half" else ref.at[slot]


def _ici_start(srcs, mode, name):
    n, ncp = len(srcs), 3 * len(srcs)
    lands = [jnp.zeros(((4,) + s.shape[:-2] + (s.shape[-2] // 2, s.shape[-1])) if mode == "half" else s.shape, s.dtype)
             for s in srcs]

    def body(*refs):
        src_refs, land_refs = refs[:n], refs[n:2 * n]
        send_sems, recv_sems = refs[2 * n:2 * n + ncp], refs[2 * n + ncp:2 * n + 2 * ncp]
        token = refs[-1]
        x, y, c = _axes()
        for i in range(n):
            for j, (cx, cy) in enumerate(_other_chips(x, y)):
                _remote(_ici_src(src_refs[i], mode, 2 * cx + cy, c), land_refs[i].at[2 * x + y], send_sems[3 * i + j],
                        recv_sems[3 * i + j], (cx, cy, c)).start()
        token[...] = jnp.zeros(token.shape, token.dtype)

    hbm, sem = pl.BlockSpec(memory_space=pltpu.HBM), pl.BlockSpec(memory_space=pltpu.SEMAPHORE)
    bufs = srcs + lands
    out_shape = tuple([pltpu.SemaphoreType.DMA(())] * (2 * ncp) + [pltpu.HBM(b.shape, b.dtype) for b in bufs]
                      + [jax.ShapeDtypeStruct((8, LANES), F32)])
    res = pl.pallas_call(
        body, name=name, out_shape=out_shape, in_specs=(hbm,) * (2 * n),
        out_specs=(sem,) * (2 * ncp) + (hbm,) * (2 * n) + (pl.BlockSpec(memory_space=pltpu.VMEM),),
        input_output_aliases={i: 2 * ncp + i for i in range(2 * n)},
        compiler_params=pltpu.CompilerParams(has_side_effects=pltpu.SideEffectType.DATAFLOW_SIDE_EFFECTING),
    )(*[pltpu.with_memory_space_constraint(b, pltpu.HBM) for b in bufs])
    return list(res[:2 * ncp]), list(res[2 * ncp:2 * ncp + 2 * n]), res[-1][0, 0]


def _ici_wait(sems, bufs, mode, after, name):
    n, ncp = len(bufs) // 2, len(sems) // 2

    def body(*refs):
        src_refs, land_refs = refs[:n], refs[n:2 * n]
        send_sems, recv_sems = refs[2 * n:2 * n + ncp], refs[2 * n + ncp:2 * n + 2 * ncp]
        x, y, c = _axes()
        for i in range(n):
            for j, (cx, cy) in enumerate(_other_chips(x, y)):
                cp = _remote(_ici_src(src_refs[i], mode, 2 * cx + cy, c), land_refs[i].at[2 * cx + cy],
                             send_sems[3 * i + j], recv_sems[3 * i + j], (cx, cy, c))
                cp.wait_send()
                cp.wait_recv()

    hbm, sem = pl.BlockSpec(memory_space=pltpu.HBM), pl.BlockSpec(memory_space=pltpu.SEMAPHORE)
    res = pl.pallas_call(
        body, name=name, out_shape=tuple(pltpu.HBM(b.shape, b.dtype) for b in bufs),
        in_specs=(hbm,) * (2 * n) + (sem,) * (2 * ncp) + (pl.BlockSpec(memory_space=pl.ANY),),
        out_specs=(hbm,) * (2 * n), input_output_aliases={i: i for i in range(2 * n)},
        compiler_params=pltpu.CompilerParams(has_side_effects=pltpu.SideEffectType.DATAFLOW_SIDE_EFFECTING),
    )(*bufs, *sems, after)
    return list(res[n:])


def _share_halves(arrs, name):
    n = len(arrs)

    def body(*refs):
        v_refs, out_refs, (send_sems, recv_sems) = refs[:n], refs[n:2 * n], refs[2 * n:]
        x, y, c = _axes()
        cps = [_remote(v_refs[i], out_refs[i], send_sems.at[i], recv_sems.at[i], (x, y, 1 - c)) for i in range(n)]
        for cp in cps:
            cp.start()
        for cp in cps:
            cp.wait()

    outs = [jax.ShapeDtypeStruct(a.shape, a.dtype) for a in arrs]
    return _hbm_call(body, arrs, outs, n, 0, name)


def _assemble_weights(mine, other, own, axis, name):
    nchips, l, rh, cs = mine.shape
    r = 2 * rh

    def body(m_ref, o_ref, w_ref, out_ref):
        x, y, c = _axes()
        is_own = pl.program_id(1) == 2 * x + y

        @pl.when(is_own)
        def _():
            out_ref[...] = w_ref[...]

        @pl.when(jnp.logical_not(is_own))
        def _():
            out_ref[pl.ds(pl.multiple_of(c * rh, rh), rh), :] = m_ref[...]
            out_ref[pl.ds(pl.multiple_of((1 - c) * rh, rh), rh), :] = o_ref[...]

    half = pl.BlockSpec((None, None, rh, cs), lambda i, s: (s, i, 0, 0))
    whole = pl.BlockSpec((None, r, cs), lambda i, s: (i, 0, 0))
    if axis == 2:
        out_spec, shape = pl.BlockSpec((None, r, cs), lambda i, s: (i, 0, s)), (l, r, nchips * cs)
    else:
        out_spec, shape = pl.BlockSpec((None, r, cs), lambda i, s: (i, s, 0)), (l, nchips * r, cs)
    return pl.pallas_call(
        body, grid=(l, nchips), in_specs=[half, half, whole], out_specs=out_spec,
        out_shape=jax.ShapeDtypeStruct(shape, mine.dtype), name=name,
        compiler_params=_cparams(("parallel", "arbitrary")))(mine, other, own)


def _sum_lead(v, name):
    k, r, w = v.shape
    tr = _pick_tile(r, (256, 128, 64, 32, 16))

    def body(v_ref, o_ref):
        acc = v_ref[0].astype(F32)
        for i in range(1, k):
            acc = acc + v_ref[i].astype(F32)
        o_ref[...] = acc

    return pl.pallas_call(
        body, grid=(r // tr,), in_specs=[pl.BlockSpec((k, tr, w), lambda i: (0, i, 0))],
        out_specs=pl.BlockSpec((tr, w), lambda i: (i, 0)), out_shape=jax.ShapeDtypeStruct((r, w), F32), name=name,
        compiler_params=_cparams(("parallel",)))(v)


_BIG = (("dn_w_in", 2), ("dn_w_out", 1), ("sb_w_qkv", 2), ("sb_w_out", 1), ("ffn_w_in", 2), ("ffn_w_out", 1))


def _to_chip_major(a, axis, nchips):
    l, r, c = a.shape
    if axis == 2:
        return a.reshape(l, r, nchips, c // nchips).transpose(2, 0, 1, 3)
    return a.reshape(l, nchips, r // nchips, c).transpose(1, 0, 2, 3)


def _from_chip_major(a, axis):
    n, l, r, c = a.shape
    if axis == 2:
        return a.transpose(1, 2, 0, 3).reshape(l, r, n * c)
    return a.transpose(1, 0, 2, 3).reshape(l, n * r, c)


def kernel(x, c, ada_w, ada_b, norm1_g, norm2_g, dn_w_in, dn_conv_w, dn_a_log, dn_dt_bias, dn_onorm_g, dn_w_out, sb_w_qkv, sb_q_norm_g, sb_k_norm_g, sb_w_out, ffn_w_in, ffn_w_out, loss_target, m_ada_w, m_ada_b, m_norm1_g, m_norm2_g, m_dn_w_in, m_dn_conv_w, m_dn_a_log, m_dn_dt_bias, m_dn_onorm_g, m_dn_w_out, m_sb_w_qkv, m_sb_q_norm_g, m_sb_k_norm_g, m_sb_w_out, m_ffn_w_in, m_ffn_w_out, v_ada_w, v_ada_b, v_norm1_g, v_norm2_g, v_dn_w_in, v_dn_conv_w, v_dn_a_log, v_dn_dt_bias, v_dn_onorm_g, v_dn_w_out, v_sb_w_qkv, v_sb_q_norm_g, v_sb_k_norm_g, v_sb_w_out, v_ffn_w_in, v_ffn_w_out):
    names = ("ada_w", "ada_b", "norm1_g", "norm2_g", "dn_w_in", "dn_conv_w", "dn_a_log", "dn_dt_bias", "dn_onorm_g",
             "dn_w_out", "sb_w_qkv", "sb_q_norm_g", "sb_k_norm_g", "sb_w_out", "ffn_w_in", "ffn_w_out")
    w = dict(zip(names, (ada_w, ada_b, norm1_g, norm2_g, dn_w_in, dn_conv_w, dn_a_log, dn_dt_bias, dn_onorm_g,
                         dn_w_out, sb_w_qkv, sb_q_norm_g, sb_k_norm_g, sb_w_out, ffn_w_in, ffn_w_out)))
    mom = dict(zip(names, (m_ada_w, m_ada_b, m_norm1_g, m_norm2_g, m_dn_w_in, m_dn_conv_w, m_dn_a_log, m_dn_dt_bias,
                           m_dn_onorm_g, m_dn_w_out, m_sb_w_qkv, m_sb_q_norm_g, m_sb_k_norm_g, m_sb_w_out, m_ffn_w_in,
                           m_ffn_w_out)))
    var = dict(zip(names, (v_ada_w, v_ada_b, v_norm1_g, v_norm2_g, v_dn_w_in, v_dn_conv_w, v_dn_a_log, v_dn_dt_bias,
                           v_dn_onorm_g, v_dn_w_out, v_sb_w_qkv, v_sb_q_norm_g, v_sb_k_norm_g, v_sb_w_out, v_ffn_w_in,
                           v_ffn_w_out)))
    ax, ay, ac = _axes()
    chip = 2 * ax + ay
    dev = 2 * chip + ac
    t, d = x.shape[1], x.shape[2]
    depth, ndn, nsb = ada_w.shape[0], dn_w_in.shape[0], sb_w_qkv.shape[0]
    heads = d // DN_HEAD_DIM
    mod_cols = ada_w.shape[2]
    conv_cols = dn_conv_w.shape[2]
    nchips, ndev = 4, 8

    conv_rows = ndn * DN_CONV * conv_cols // d
    pay1 = jnp.concatenate([c, dn_conv_w.reshape(conv_rows, d), jnp.zeros((8 - 1 - conv_rows, d), F32)], axis=0)
    g1 = _allgather_small(pay1, "ag_cond").reshape(ndev, 8, d)
    c_all = g1[:, 0]
    conv_full = g1[::2, 1:1 + conv_rows].reshape(nchips, ndn, DN_CONV, conv_cols).transpose(1, 2, 0, 3)
    conv_full = conv_full.reshape(ndn, DN_CONV, nchips * conv_cols)

    c16 = jnp.pad(c_all, ((0, 16 - ndev), (0, 0)))
    cond16 = _rowwise(lambda r, v: ([_silu(r[0])], []), [c16], [], [(d, F32)], [], name="cond_silu")[0]
    pay2 = jnp.concatenate([_mm(cond16, ada_w[i], name="ada_mod%d" % i)[:ndev] for i in range(depth)], axis=0)
    g2 = _allgather_small(pay2, "ag_mod").reshape(ndev, depth, ndev, mod_cols)[::2]
    mod_raw = lax.dynamic_index_in_dim(g2, dev, axis=2, keepdims=False)
    mod_raw = mod_raw.transpose(1, 0, 2).reshape(depth, nchips * mod_cols)
    mod = _rowwise(lambda r, v: ([r[0] + r[1]], []), [mod_raw, ada_b], [], [(nchips * mod_cols, F32)], [],
                   name="ada_bias")[0]

    axis_of = dict(_BIG)
    groups = [[0], list(range(1, depth))]

    def kinds_of(layers):
        out = []
        for n, _ in _BIG:
            idx = [l // 2 for l in layers if l % 2 == (0 if n.startswith("dn_") else 1)] if n[:3] in ("dn_", "sb_") \
                else list(layers)
            if idx:
                out.append((n, min(idx), max(idx) + 1))
        return out

    two_d = lambda a: a.reshape(-1, a.shape[-1])
    both_halves = lambda mine, other, ax: jnp.where(ac == 0, jnp.concatenate([mine, other], axis=ax),
                                                    jnp.concatenate([other, mine], axis=ax))
    w16 = {n: w[n].astype(BF16) for n, _ in _BIG}
    gathers = []
    for gi, layers in enumerate(groups):
        sl = kinds_of(layers)
        shards = [w16[n][lo:hi] for n, lo, hi in sl]
        sems, bufs, zero = _ici_start(shards, "half", "ag_start%d" % gi)
        gathers.append((sl, shards, sems, bufs))
        mod = mod + zero
    full = {}

    def finish_gather(gi, after):
        sl, shards, sems, bufs = gathers[gi]
        lands = _ici_wait(sems, bufs, "half", after, "ag_wait%d" % gi)
        for (n, lo, hi), shard, mine, other in zip(sl, shards, lands, _share_halves(lands, "ag_pair%d" % gi)):
            if shard.shape[-1] % LANES == 0:
                a = _assemble_weights(mine, other, shard, axis_of[n], "ag_whole%d_%s" % (gi, n))
            else:
                a = lax.dynamic_update_index_in_dim(both_halves(mine, other, 2), shard, chip, 0)
                a = _from_chip_major(a, axis_of[n])
            for l in range(lo, hi):
                full[n, l] = a[l - lo]

    padl = lambda v: jnp.pad(v[None, :], ((0, 0), (0, LANES - v.shape[0])))
    padc = lambda a: jnp.pad(a, ((0, 0), (0, LANES - a.shape[1])))

    def layer_weights(i, after):
        for gi, layers in enumerate(groups):
            if layers and i == layers[0]:
                finish_gather(gi, after)
        j = i // 2
        if i % 2 == 0:
            wi = full["dn_w_in", j]
            w_all = jnp.concatenate([wi[:, :4 * d], padc(wi[:, 4 * d:4 * d + heads]), padc(wi[:, 4 * d + heads:])],
                                    axis=1)
            mix = dict(all=w_all, out=full["dn_w_out", j], conv_w=conv_full[j], a_log=padl(dn_a_log[j]),
                       dt_bias=padl(dn_dt_bias[j]), onorm_g=dn_onorm_g[j][None])
        else:
            mix = dict(qkv=full["sb_w_qkv", j], out=full["sb_w_out", j], q_g=sb_q_norm_g[j][None],
                       k_g=sb_k_norm_g[j][None])
        return mix, dict(w_in=full["ffn_w_in", i], w_out=full["ffn_w_out", i])

    g_dn, g_sb, g_ffn = [None] * ndn, [None] * nsb, [None] * depth
    reduces = {}

    def gw_in(g):
        ga = g["w_all"]
        return jnp.concatenate([ga[:, :4 * d], ga[:, 4 * d:4 * d + heads], ga[:, 4 * d + LANES:4 * d + LANES + heads]],
                               axis=1)
    layer_grad = dict(dn_w_in=lambda j: gw_in(g_dn[j]), dn_w_out=lambda j: g_dn[j]["w_out"],
                      sb_w_qkv=lambda j: g_sb[j]["w_qkv"], sb_w_out=lambda j: g_sb[j]["w_out"],
                      ffn_w_in=lambda l: g_ffn[l]["w_in"], ffn_w_out=lambda l: g_ffn[l]["w_out"])

    def layer_grads(i, after, g_mix, g_f):
        (g_dn if i % 2 == 0 else g_sb)[i // 2], g_ffn[i] = g_mix, g_f
        zero = jnp.zeros((), F32)
        for gi, layers in enumerate(groups):
            if layers and i == layers[0]:
                sl = kinds_of(layers)
                parts = [_to_chip_major(jnp.stack([layer_grad[n](j) for j in range(lo, hi)]), axis_of[n], nchips)
                         for n, lo, hi in sl]
                pairs = []
                for (n, _, _), p, fs in zip(sl, parts, _sibling_swap_halves(parts, "gr_pair%d" % gi)):
                    rh = fs.shape[2]
                    own = lax.dynamic_slice_in_dim(p, ac * rh, rh, axis=2)
                    pairs.append(_rowwise(lambda r, v: ([r[0].astype(F32) + r[1].astype(F32)], []),
                                          [two_d(own), two_d(fs)], [], [(fs.shape[-1], BF16)], [],
                                          name="gr_pair_add%d_%s" % (gi, n))[0].reshape(fs.shape))
                sems, bufs, zero = _ici_start(pairs, "slot", "gr_start%d" % gi)
                reduces[gi] = (sl, pairs, sems, bufs)
        return zero

    loss_local, grad_x, dmod, g_n1, g_n2 = _local_step(x[0], loss_target[0], mod, norm1_g, norm2_g, layer_weights,
                                                       layer_grads)
    loss = lax.psum(loss_local, ("x", "y", "c"))

    g_conv = jnp.stack([g["conv_w"] for g in g_dn])
    misc = jnp.concatenate([jnp.concatenate([g["onorm_g"] for g in g_dn], axis=1),
                            jnp.concatenate([g["a_log"] for g in g_dn], axis=1),
                            jnp.concatenate([g["dt_bias"] for g in g_dn], axis=1),
                            jnp.concatenate([g["q_norm_g"] for g in g_sb], axis=1),
                            jnp.concatenate([g["k_norm_g"] for g in g_sb], axis=1)], axis=1)
    misc = jnp.pad(misc, ((0, 0), (0, -misc.shape[1] % d))).reshape(-1, d)
    small = [dmod.reshape(-1, d), g_n1, g_n2, g_conv.reshape(-1, d), misc]
    small_rows = [s.shape[0] for s in small]
    pad_rows = -sum(small_rows) % 8
    pay3 = jnp.concatenate(small + [jnp.zeros((pad_rows, d), F32)], axis=0)
    nrow3 = pay3.shape[0]
    g3 = _allgather_small(pay3, "ag_small")
    summed = _rowwise(lambda r, v: ([], [_colsum(r[0])]), [g3.reshape(ndev, nrow3 * d)], [], [], [(1, nrow3 * d)],
                      name="small_sum")[0].reshape(nrow3, d)
    offs = [0]
    for n_ in small_rows:
        offs.append(offs[-1] + n_)
    grads = {}
    grads["ada_b"] = summed[offs[0]:offs[1]].reshape(depth, N_MOD * d)
    grads["norm1_g"] = summed[offs[1]:offs[2]]
    grads["norm2_g"] = summed[offs[2]:offs[3]]
    conv_sum = summed[offs[3]:offs[4]].reshape(ndn, DN_CONV, nchips * conv_cols)
    grads["dn_conv_w"] = lax.dynamic_slice_in_dim(conv_sum, chip * conv_cols, conv_cols, axis=2)
    mrow = summed[offs[4]:offs[5]].reshape(-1)
    o = 0
    grads["dn_onorm_g"] = mrow[o:o + ndn * DN_HEAD_DIM].reshape(ndn, DN_HEAD_DIM)
    o += ndn * DN_HEAD_DIM
    grads["dn_a_log"] = mrow[o:o + ndn * LANES].reshape(ndn, LANES)[:, :heads]
    o += ndn * LANES
    grads["dn_dt_bias"] = mrow[o:o + ndn * LANES].reshape(ndn, LANES)[:, :heads]
    o += ndn * LANES
    grads["sb_q_norm_g"] = mrow[o:o + nsb * SB_HEAD_DIM].reshape(nsb, SB_HEAD_DIM)
    o += nsb * SB_HEAD_DIM
    grads["sb_k_norm_g"] = mrow[o:o + nsb * SB_HEAD_DIM].reshape(nsb, SB_HEAD_DIM)
    dmod_all = g3.reshape(ndev, nrow3, d)[:, :small_rows[0]].reshape(ndev, depth, N_MOD * d)
    dmod_mine = lax.dynamic_slice_in_dim(dmod_all, chip * mod_cols, mod_cols, axis=2)
    dmod16 = jnp.pad(dmod_mine, ((0, 16 - ndev), (0, 0), (0, 0)))
    grads["ada_w"] = jnp.stack([_mm(cond16, dmod16[:, i], ta=True, name="ada_gw%d" % i) for i in range(depth)])

    pieces = {n: [] for n, _ in _BIG}
    for gi in reversed(range(len(groups))):
        if gi not in reduces:
            continue
        sl, pairs, sems, bufs = reduces[gi]
        lands = _ici_wait(sems, bufs, "slot", grad_x, "gr_wait%d" % gi)
        reduced = []
        for (n, _, _), land, pair in zip(sl, lands, pairs):
            own = lax.dynamic_index_in_dim(pair, chip, 0, keepdims=True)
            land = lax.dynamic_update_slice_in_dim(land, own, chip, axis=0)
            reduced.append(_sum_lead(land.reshape(nchips, -1, land.shape[-1]), "gr_chip_add%d_%s" % (gi, n)))
        for (n, lo, hi), mine, other in zip(sl, reduced, _share_halves(reduced, "gr_share%d" % gi)):
            shape = (hi - lo, -1, mine.shape[-1])
            pieces[n].append((lo, both_halves(mine.reshape(shape), other.reshape(shape), 1)))
    for n, _ in _BIG:
        grads[n] = jnp.concatenate([p for _, p in sorted(pieces[n], key=lambda t: t[0])], axis=0)

    delta, new_m, new_v = {}, {}, {}
    for n in names:
        delta[n], new_m[n], new_v[n] = _adamw(w[n], grads[n], mom[n], var[n], "adamw_" + n)
    return (loss, grad_x[None], *[grads[n] for n in names], *[delta[n] for n in names], *[new_m[n] for n in names],
            *[new_v[n] for n in names])
```

```python
import functools

import jax
import jax.numpy as jnp
from jax import lax
from jax.experimental import pallas as pl
from jax.experimental.pallas import tpu as pltpu

F32 = jnp.float32
BF16 = jnp.bfloat16
HI = lax.Precision.HIGHEST
MESH = pl.DeviceIdType.MESH

EPS = 1e-6
N_MOD = 6
DN_HEAD_DIM = 128
DN_CONV = 4
DN_CHUNK = 64
DN_GROUP = 8
DN_GATE_ROWS = 256
SB_HEAD_DIM = 64
SB_BLOCK = 128
SB_QBLOCK = 512
SB_PART = 128
LANES = 128
VMEM_LIMIT = 56 * 1024 * 1024
MM_BLOCK_BUDGET = 36 * 1024 * 1024
ROW_BLOCK_BUDGET = 20 * 1024 * 1024

ADAM_LR = 0.001
ADAM_B1 = 0.9
ADAM_B2 = 0.999
ADAM_EPS = 1e-08
ADAM_WD = 0.01
ADAM_STEP = 10


def _cparams(sem=None):
    return pltpu.CompilerParams(dimension_semantics=sem, vmem_limit_bytes=VMEM_LIMIT)


def _sigmoid(x):
    return 1.0 / (1.0 + jnp.exp(-x))


def _silu(x):
    return x * _sigmoid(x)


def _dsilu(x):
    s = _sigmoid(x)
    return s * (1.0 + x * (1.0 - s))


def _softplus(x):
    return jnp.maximum(x, 0.0) + jnp.log1p(jnp.exp(-jnp.abs(x)))


def _pick_tile(n, prefs):
    for t in prefs:
        if n % t == 0:
            return t
    return n


def _mm(a, b, *, ta=False, tb=False, out_dtype=F32, name):
    m = a.shape[1] if ta else a.shape[0]
    k = a.shape[0] if ta else a.shape[1]
    n = b.shape[0] if tb else b.shape[1]
    assert (b.shape[1] if tb else b.shape[0]) == k
    size = lambda dt: jnp.dtype(dt).itemsize
    best = None
    for tm in sorted({t for t in (m, 2048, 1024, 512, 256, 128) if m % t == 0 and (t % 128 == 0 or t == m)}):
        for tn in sorted({t for t in (n, 2816, 1408, 1024, 768, 512, 256, 128) if n % t == 0 and (t % 128 == 0 or t == n)}):
            need = 2 * (tm * k * size(a.dtype) + tn * k * size(b.dtype) + tm * tn * size(out_dtype))
            if need <= MM_BLOCK_BUDGET and (best is None or tm * tn > best[0] * best[1]):
                best = (tm, tn)
    tm, tn = best
    dims = (((0 if ta else 1,), (1 if tb else 0,)), ((), ()))

    def body(a_ref, b_ref, o_ref):
        av = a_ref[...].astype(BF16)
        bv = b_ref[...].astype(BF16)
        o_ref[...] = lax.dot_general(av, bv, dims, preferred_element_type=F32).astype(o_ref.dtype)

    a_spec = pl.BlockSpec((k, tm), lambda i, j: (0, i)) if ta else pl.BlockSpec((tm, k), lambda i, j: (i, 0))
    b_spec = pl.BlockSpec((tn, k), lambda i, j: (j, 0)) if tb else pl.BlockSpec((k, tn), lambda i, j: (0, j))
    return pl.pallas_call(
        body, grid=(m // tm, n // tn), in_specs=[a_spec, b_spec],
        out_specs=pl.BlockSpec((tm, tn), lambda i, j: (i, j)),
        out_shape=jax.ShapeDtypeStruct((m, n), out_dtype), name=name,
        compiler_params=_cparams(("parallel", "parallel")))(a, b)


def _rowwise(body, rows, vecs, outs, accs, *, name, tr=None):
    rows = [r if isinstance(r, tuple) else (r, r.shape[1], 0) for r in rows]
    nrows = rows[0][0].shape[0]
    if tr is None:
        per_row = sum(w * jnp.dtype(a.dtype).itemsize for a, w, _ in rows) + sum(
            w * jnp.dtype(dt).itemsize for w, dt in outs)
        tr = next((t for t in (2048, 1024, 512, 256) if nrows % t == 0 and 2 * t * per_row <= ROW_BLOCK_BUDGET), 256)
    tr = tr if nrows % tr == 0 else nrows
    nr, nv, no = len(rows), len(vecs), len(outs)

    def kern(*refs):
        r_in, v_in = refs[:nr], refs[nr:nr + nv]
        o_refs, a_refs = refs[nr + nv:nr + nv + no], refs[nr + nv + no:]
        res_o, res_a = body([r[...] for r in r_in], [v[...] for v in v_in])
        for o, val in zip(o_refs, res_o):
            o[...] = val.astype(o.dtype)
        if a_refs:
            @pl.when(pl.program_id(0) == 0)
            def _():
                for a in a_refs:
                    a[...] = jnp.zeros(a.shape, a.dtype)
            for a, val in zip(a_refs, res_a):
                a[...] += val

    in_specs = [pl.BlockSpec((tr, w), functools.partial(lambda i, cb: (i, cb), cb=cb)) for _, w, cb in rows]
    in_specs += [pl.BlockSpec(v.shape, functools.partial(lambda i, nd: (0,) * nd, nd=v.ndim)) for v in vecs]
    out_specs = [pl.BlockSpec((tr, w), lambda i: (i, 0)) for w, _ in outs]
    out_specs += [pl.BlockSpec(s, lambda i: (0, 0)) for s in accs]
    out_shape = [jax.ShapeDtypeStruct((nrows, w), dt) for w, dt in outs]
    out_shape += [jax.ShapeDtypeStruct(s, F32) for s in accs]
    res = pl.pallas_call(
        kern, grid=(nrows // tr,), in_specs=in_specs, out_specs=out_specs, out_shape=out_shape, name=name,
        compiler_params=_cparams(("arbitrary",) if accs else ("parallel",)))(*[r[0] for r in rows], *vecs)
    return res


def _colsum(v):
    return jnp.sum(v, axis=0, keepdims=True)


def _adaln_fwd(x, g, sc, sh, name):
    def body(r, v):
        (xv,), (gv, scv, shv) = r, v
        rs = lax.rsqrt(jnp.mean(xv * xv, axis=-1, keepdims=True) + EPS)
        return [(xv * rs) * gv * (1.0 + scv) + shv], []
    return _rowwise(body, [x], [g, sc, sh], [(x.shape[1], BF16)], [], name=name)[0]


def _adaln_bwd(x, dh, dxr, g, sc, name):
    d = x.shape[1]

    def body(r, v):
        (xv, dhv, dxv), (gv, scv) = r, v
        rs = lax.rsqrt(jnp.mean(xv * xv, axis=-1, keepdims=True) + EPS)
        nv = xv * rs
        dn = dhv * (gv * (1.0 + scv))
        dx = rs * (dn - nv * jnp.mean(dn * nv, axis=-1, keepdims=True)) + dxv
        dhn = dhv * nv
        return [dx], [_colsum(dhn * (1.0 + scv)), _colsum(dhn * gv), _colsum(dhv)]
    return _rowwise(body, [x, dh, dxr], [g, sc], [(d, F32)], [(1, d)] * 3, name=name)


def _resid_fwd(x, y, gt, name):
    def body(r, v):
        return [r[0] + v[0] * r[1]], []
    return _rowwise(body, [x, y], [gt], [(x.shape[1], F32)], [], name=name)[0]


def _resid_bwd(dx, y, gt, name):
    d = dx.shape[1]

    def body(r, v):
        return [v[0] * r[0]], [_colsum(r[0] * r[1])]
    return _rowwise(body, [dx, y], [gt], [(d, BF16)], [(1, d)], name=name)


def _swiglu_fwd(u, name):
    f = u.shape[1] // 2

    def body(r, v):
        uv = r[0]
        return [_silu(uv[:, :f]) * uv[:, f:]], []
    return _rowwise(body, [u], [], [(f, BF16)], [], name=name)[0]


def _swiglu_bwd(u, da, name):
    f = u.shape[1] // 2

    def body(r, v):
        uv, dav = r
        gate, up = uv[:, :f], uv[:, f:]
        return [jnp.concatenate([dav * up * _dsilu(gate), dav * _silu(gate)], axis=1)], []
    return _rowwise(body, [u, da], [], [(2 * f, BF16)], [], name=name)[0]


def _loss_fwd_bwd(y, tgt, name):
    d = y.shape[1]

    def body(r, v):
        err = r[0] - r[1]
        part = jnp.sum(jnp.sum(err * err, axis=1, keepdims=True), axis=0, keepdims=True) * (0.5 / d)
        return [err * (1.0 / d)], [jnp.broadcast_to(part, (1, LANES))]
    dy, acc = _rowwise(body, [y, tgt], [], [(d, F32)], [(1, LANES)], name=name)
    return acc[0, 0], dy


def _head_means(v):
    row = lax.broadcasted_iota(jnp.int32, (LANES, LANES), 0)
    col = lax.broadcasted_iota(jnp.int32, (LANES, LANES), 1)
    same = ((row // SB_HEAD_DIM) == (col // SB_HEAD_DIM)).astype(F32)
    parts = [jnp.dot(v[:, g * LANES:(g + 1) * LANES], same, precision=HI, preferred_element_type=F32)
             for g in range(v.shape[1] // LANES)]
    return jnp.concatenate(parts, axis=1) * (1.0 / SB_HEAD_DIM)


def _sb_norm_fwd(qkv, gq, gk, name):
    d = qkv.shape[1] // 3

    def body(r, v):
        return [x * lax.rsqrt(_head_means(x * x) + EPS) * g for x, g in zip(r, v)], []
    return _rowwise(body, [(qkv, d, 0), (qkv, d, 1)], [gq, gk], [(d, F32), (d, F32)], [], name=name)


def _sb_norm_bwd(qkv, dqn, dkn, dv, gq, gk, name):
    d = qkv.shape[1] // 3

    def body(r, v):
        outs, accs = [], []
        for x, dy, g in ((r[0], r[2], v[0]), (r[1], r[3], v[1])):
            rs = lax.rsqrt(_head_means(x * x) + EPS)
            nv = x * rs
            dn = dy * g
            outs.append(rs * (dn - nv * _head_means(dn * nv)))
            accs.append(_colsum(dy * nv))
        return [jnp.concatenate(outs + [r[4]], axis=1)], accs
    return _rowwise(body, [(qkv, d, 0), (qkv, d, 1), dqn, dkn, dv], [gq, gk], [(3 * d, BF16)], [(1, d), (1, d)],
                    name=name)


def _head_tiles(a):
    return [a[:, h * DN_HEAD_DIM:(h + 1) * DN_HEAD_DIM] for h in range(a.shape[1] // DN_HEAD_DIM)]


def _dn_post_fwd(o, proj, g, name):
    d = o.shape[1]

    def body(r, v):
        outs = []
        for ov, zv in zip(_head_tiles(r[0]), _head_tiles(r[1])):
            rs = lax.rsqrt(jnp.mean(ov * ov, axis=-1, keepdims=True) + EPS)
            outs.append(ov * rs * v[0] * _silu(zv))
        return [jnp.concatenate(outs, axis=1)], []
    return _rowwise(body, [o, (proj, d, 3)], [g], [(d, BF16)], [], name=name)[0]


def _dn_post_bwd(o, proj, don, g, name):
    d = o.shape[1]

    def body(r, v):
        dos, dzs, dg = [], [], jnp.zeros((1, DN_HEAD_DIM), F32)
        for ov, zv, dv in zip(_head_tiles(r[0]), _head_tiles(r[1]), _head_tiles(r[2])):
            rs = lax.rsqrt(jnp.mean(ov * ov, axis=-1, keepdims=True) + EPS)
            nv = ov * rs
            s = _silu(zv)
            dn = dv * v[0] * s
            dos.append(rs * (dn - nv * jnp.mean(dn * nv, axis=-1, keepdims=True)))
            dzs.append(dv * nv * v[0] * _dsilu(zv))
            dg = dg + _colsum(dv * nv * s)
        return [jnp.concatenate(dos, axis=1), jnp.concatenate(dzs, axis=1)], [dg]
    return _rowwise(body, [o, (proj, d, 3), don], [g], [(d, F32), (d, BF16)], [(1, DN_HEAD_DIM)], name=name)


def _chunk_tri(tr, upper):
    row = lax.broadcasted_iota(jnp.int32, (tr, tr), 0)
    col = lax.broadcasted_iota(jnp.int32, (tr, tr), 1)
    same = (row // DN_CHUNK) == (col // DN_CHUNK)
    return (same & ((row <= col) if upper else (row >= col))).astype(F32)


def _dn_gate_fwd(proj, colblk, a_log, dt_bias, name):
    def body(r, v):
        ab = r[0]
        a, b = ab[:, :LANES], ab[:, LANES:]
        g = -jnp.exp(v[0]) * _softplus(a + v[1])
        big_g = jnp.dot(_chunk_tri(g.shape[0], False), g, precision=HI, preferred_element_type=F32)
        return [big_g, _sigmoid(b)], []
    return _rowwise(body, [(proj, 2 * LANES, colblk)], [a_log, dt_bias], [(LANES, F32), (LANES, F32)], [], name=name,
                    tr=DN_GATE_ROWS)


def _dn_gate_bwd(proj, colblk, d_big_g, dbeta, a_log, dt_bias, name):
    def body(r, v):
        ab, dgc, dbt = r
        a, b = ab[:, :LANES], ab[:, LANES:]
        dg = jnp.dot(_chunk_tri(dgc.shape[0], True), dgc, precision=HI, preferred_element_type=F32)
        na = -jnp.exp(v[0])
        pre = a + v[1]
        da = dg * na * _sigmoid(pre)
        beta = _sigmoid(b)
        db = dbt * beta * (1.0 - beta)
        return [jnp.concatenate([da, db], axis=1)], [_colsum(dg * na * _softplus(pre)), _colsum(da)]
    return _rowwise(body, [(proj, 2 * LANES, colblk), d_big_g, dbeta], [a_log, dt_bias],
                    [(2 * LANES, BF16)], [(1, LANES), (1, LANES)], name=name, tr=DN_GATE_ROWS)


def _shift_rows(x, s):
    if s == 0:
        return x
    t = x.shape[0]
    row = lax.broadcasted_iota(jnp.int32, x.shape, 0)
    rolled = pltpu.roll(x, s % t, axis=0)
    return jnp.where((row >= s) if s > 0 else (row < t + s), rolled, 0.0)


def _rnd(x):
    return x.astype(BF16).astype(F32)


def _conv(x, c_ref):
    c = x * _rnd(c_ref[DN_CONV - 1:DN_CONV, :])
    for s in range(1, DN_CONV):
        c = c + _shift_rows(x, s) * _rnd(c_ref[DN_CONV - 1 - s:DN_CONV - s, :])
    return c


def _dn_prep_fwd(proj, conv_w, heads, name):
    t = proj.shape[0]
    d = heads * DN_HEAD_DIM

    def body(xq, xk, xv, cq, ck, cv, q_ref, k_ref, v_ref):
        for x_ref, c_ref, o_ref, norm in ((xq, cq, q_ref, True), (xk, ck, k_ref, True), (xv, cv, v_ref, False)):
            y = _silu(_conv(_rnd(x_ref[...]), c_ref))
            if norm:
                y = y * lax.rsqrt(jnp.sum(y * y, axis=-1, keepdims=True) + EPS)
            o_ref[...] = y

    xs = [pl.BlockSpec((t, DN_HEAD_DIM), functools.partial(lambda h, o: (0, h + o), o=o * heads)) for o in range(3)]
    cs = [pl.BlockSpec((DN_CONV, DN_HEAD_DIM), functools.partial(lambda h, o: (0, h + o), o=o * heads)) for o in range(3)]
    return pl.pallas_call(
        body, grid=(heads,), in_specs=xs + cs,
        out_specs=[pl.BlockSpec((t, DN_HEAD_DIM), lambda h: (0, h))] * 3,
        out_shape=[jax.ShapeDtypeStruct((t, d), F32)] * 3, name=name,
        compiler_params=_cparams(("parallel",)))(proj, proj, proj, conv_w, conv_w, conv_w)


def _dn_prep_bwd(proj, conv_w, dq, dk, dv, heads, name):
    t = proj.shape[0]
    d = heads * DN_HEAD_DIM

    def body(xq, xk, xv, cq, ck, cv, gq, gk, gv, oq, ok, ov, wq, wk, wv):
        for x_ref, c_ref, g_ref, o_ref, w_ref, norm in ((xq, cq, gq, oq, wq, True), (xk, ck, gk, ok, wk, True),
                                                        (xv, cv, gv, ov, wv, False)):
            x, dy = _rnd(x_ref[...]), g_ref[...]
            c = _conv(x, c_ref)
            if norm:
                y = _silu(c)
                rs = lax.rsqrt(jnp.sum(y * y, axis=-1, keepdims=True) + EPS)
                yn = y * rs
                dy = rs * (dy - yn * jnp.sum(dy * yn, axis=-1, keepdims=True))
            dc = _rnd(dy * _dsilu(c))
            dx = dc * _rnd(c_ref[DN_CONV - 1:DN_CONV, :])
            w_ref[DN_CONV - 1:DN_CONV, :] = _colsum(dc * x)
            for s in range(1, DN_CONV):
                dx = dx + _shift_rows(dc, -s) * _rnd(c_ref[DN_CONV - 1 - s:DN_CONV - s, :])
                w_ref[DN_CONV - 1 - s:DN_CONV - s, :] = _colsum(dc * _shift_rows(x, s))
            o_ref[...] = dx.astype(o_ref.dtype)

    xs = [pl.BlockSpec((t, DN_HEAD_DIM), functools.partial(lambda h, o: (0, h + o), o=o * heads)) for o in range(3)]
    cs = [pl.BlockSpec((DN_CONV, DN_HEAD_DIM), functools.partial(lambda h, o: (0, h + o), o=o * heads)) for o in range(3)]
    hs = pl.BlockSpec((t, DN_HEAD_DIM), lambda h: (0, h))
    ws = pl.BlockSpec((DN_CONV, DN_HEAD_DIM), lambda h: (0, h))
    return pl.pallas_call(
        body, grid=(heads,), in_specs=xs + cs + [hs] * 3, out_specs=[hs] * 3 + [ws] * 3,
        out_shape=[jax.ShapeDtypeStruct((t, d), BF16)] * 3 + [jax.ShapeDtypeStruct((DN_CONV, d), F32)] * 3, name=name,
        compiler_params=_cparams(("parallel",)))(proj, proj, proj, conv_w, conv_w, conv_w, dq, dk, dv)


X3 = "three bf16 passes"


def _bdot(a, b, dims, prec):
    if prec != X3:
        return lax.dot_general(a, b, dims, precision=prec, preferred_element_type=F32)
    a1, b1 = a.astype(BF16), b.astype(BF16)
    a2, b2 = (a - a1.astype(F32)).astype(BF16), (b - b1.astype(F32)).astype(BF16)
    dot = lambda x, y: lax.dot_general(x, y, dims, preferred_element_type=F32)
    return dot(a1, b1) + (dot(a1, b2) + dot(a2, b1))


def _bmm(a, b, prec=None):
    return _bdot(a, b, (((2,), (1,)), ((0,), (0,))), prec)


def _bmm_nt(a, b, prec=None):
    return _bdot(a, b, (((2,), (2,)), ((0,), (0,))), prec)


def _bmm_tn(a, b, prec=None):
    return _bdot(a, b, (((1,), (1,)), ((0,), (0,))), prec)


def _dn_local(qg, kg, vg, gc, gr, bt):
    c = qg.shape[1]
    row = lax.broadcasted_iota(jnp.int32, (c, c), 0)
    col = lax.broadcasted_iota(jnp.int32, (c, c), 1)
    incl, strict = (row >= col)[None], (row > col)[None]
    decay = jnp.where(incl, jnp.exp(jnp.where(incl, gc - gr, 0.0)), 0.0)
    kb = kg * bt
    vb = vg * bt
    m = _bmm_nt(kb.astype(BF16), kg.astype(BF16))
    a = jnp.where(strict, m * decay, 0.0)
    bp = -a
    tm = jnp.where((row == col)[None], 1.0, 0.0) + bp
    steps = max(1, (c - 1).bit_length()) - 1
    for _ in range(steps):
        bp = _bmm(bp, bp, X3)
        tm = tm + _bmm(tm, bp, X3)
    eg = jnp.exp(gc)
    glast = gc[:, c - 1:c, :]
    erel = jnp.exp(glast - gc)
    kbg = kb * eg
    qm = _bmm_nt(qg.astype(BF16), kg.astype(BF16))
    return dict(decay=decay, strict=strict, kb=kb, vb=vb, m=m, tm=tm, eg=eg, erel=erel, kbg=kbg, qm=qm,
                u=_bmm(tm, vb, X3), w=_bmm(tm, kbg, X3), qd=qg * eg, kt=kg * erel, gl=jnp.exp(glast))


def _dot(a, b):
    return jnp.dot(a.astype(BF16), b.astype(BF16), preferred_element_type=F32)


def _dot_nt(a, b):
    return lax.dot_general(a.astype(BF16), b.astype(BF16), (((1,), (1,)), ((), ())), preferred_element_type=F32)


def _dot_tn(a, b):
    return lax.dot_general(a.astype(BF16), b.astype(BF16), (((0,), (0,)), ((), ())), preferred_element_type=F32)


def _dn_chunk_specs(t, heads):
    n = t // DN_CHUNK
    hs = pl.BlockSpec((t, DN_HEAD_DIM), lambda h: (0, h))
    gcs = pl.BlockSpec((1, n, DN_CHUNK, 1), lambda h: (h, 0, 0, 0))
    grs = pl.BlockSpec((1, n, 1, DN_CHUNK), lambda h: (h, 0, 0, 0))
    ss = pl.BlockSpec((1, n, DN_HEAD_DIM, DN_HEAD_DIM), lambda h: (h, 0, 0, 0))
    return n, hs, gcs, grs, ss


def _dn_chunk_fwd(q, k, v, gc, gr, bc, name):
    t, d = q.shape
    heads = d // DN_HEAD_DIM
    c, dk = DN_CHUNK, DN_HEAD_DIM
    n, hs, gcs, grs, ss = _dn_chunk_specs(t, heads)
    nb = min(DN_GROUP, n)
    scale = dk ** -0.5

    def body(q_ref, k_ref, v_ref, gc_ref, gr_ref, b_ref, o_ref, s_ref, u_s, w_s, at_s, qd_s, kt_s, gl_s):
        def group(gi, carry):
            r0 = pl.multiple_of(gi * (nb * c), nb * c)
            n0 = gi * nb
            ld = lambda ref: ref[pl.ds(r0, nb * c), :].reshape(nb, c, dk)
            loc = _dn_local(ld(q_ref) * scale, ld(k_ref), ld(v_ref), gc_ref[0, pl.ds(n0, nb)],
                            gr_ref[0, pl.ds(n0, nb)], b_ref[0, pl.ds(n0, nb)])
            u_s[pl.ds(n0, nb)] = loc["u"]
            w_s[pl.ds(n0, nb)] = loc["w"]
            at_s[pl.ds(n0, nb)] = loc["qm"] * loc["decay"]
            qd_s[pl.ds(n0, nb)] = loc["qd"]
            kt_s[pl.ds(n0, nb)] = loc["kt"]
            gl_s[pl.ds(n0, nb)] = loc["gl"]
            return carry
        lax.fori_loop(0, n // nb, group, 0)

        def chunk(i, s):
            s_ref[0, i] = s
            vnew = u_s[i] - _dot(w_s[i], s)
            o = _dot(qd_s[i], s) + _dot(at_s[i], vnew)
            o_ref[pl.ds(pl.multiple_of(i * c, c), c), :] = o
            return s * gl_s[i] + _dot_tn(kt_s[i], vnew)
        lax.fori_loop(0, n, chunk, jnp.zeros((dk, dk), F32))

    scratch = [pltpu.VMEM((n, c, dk), F32), pltpu.VMEM((n, c, dk), F32), pltpu.VMEM((n, c, c), F32),
               pltpu.VMEM((n, c, dk), F32), pltpu.VMEM((n, c, dk), F32), pltpu.VMEM((n, 1, 1), F32)]
    return pl.pallas_call(
        body, grid=(heads,), in_specs=[hs, hs, hs, gcs, grs, gcs], out_specs=[hs, ss],
        out_shape=[jax.ShapeDtypeStruct((t, d), F32), jax.ShapeDtypeStruct((heads, n, dk, dk), F32)],
        scratch_shapes=scratch, name=name, compiler_params=_cparams(("parallel",)))(q, k, v, gc, gr, bc)


def _dn_chunk_bwd(q, k, v, gc, gr, bc, s_all, do, name):
    t, d = q.shape
    heads = d // DN_HEAD_DIM
    c, dk = DN_CHUNK, DN_HEAD_DIM
    n, hs, gcs, grs, ss = _dn_chunk_specs(t, heads)
    nb = min(DN_GROUP, n)
    scale = dk ** -0.5

    def body(q_ref, k_ref, v_ref, gc_ref, gr_ref, b_ref, s_ref, do_ref,
             dq_ref, dk_ref, dv_ref, dgc_ref, dgr_ref, db_ref,
             u_s, w_s, att_s, qd_s, kt_s, gl_s, du_s, dw_s, dat_s, dqd_s, dkt_s, dgl_s):
        def load_group(gi):
            r0 = pl.multiple_of(gi * (nb * c), nb * c)
            n0 = gi * nb
            ld = lambda ref: ref[pl.ds(r0, nb * c), :].reshape(nb, c, dk)
            qg, kg, vg = ld(q_ref) * scale, ld(k_ref), ld(v_ref)
            gcv, grv, bt = gc_ref[0, pl.ds(n0, nb)], gr_ref[0, pl.ds(n0, nb)], b_ref[0, pl.ds(n0, nb)]
            return r0, n0, qg, kg, vg, gcv, grv, bt, _dn_local(qg, kg, vg, gcv, grv, bt)

        def group_a(gi, carry):
            _, n0, qg, kg, _, gcv, grv, _, loc = load_group(gi)
            row = lax.broadcasted_iota(jnp.int32, (c, c), 0)
            col = lax.broadcasted_iota(jnp.int32, (c, c), 1)
            upper = (col >= row)[None]
            decay_t = jnp.where(upper, jnp.exp(jnp.where(upper, grv - gcv, 0.0)), 0.0)
            u_s[pl.ds(n0, nb)] = loc["u"]
            w_s[pl.ds(n0, nb)] = loc["w"]
            att_s[pl.ds(n0, nb)] = _bmm_nt(kg.astype(BF16), qg.astype(BF16)) * decay_t
            qd_s[pl.ds(n0, nb)] = loc["qd"]
            kt_s[pl.ds(n0, nb)] = loc["kt"]
            gl_s[pl.ds(n0, nb)] = loc["gl"]
            return carry
        lax.fori_loop(0, n // nb, group_a, 0)

        def chunk_b(it, ds_next):
            i = n - 1 - it
            s = s_ref[0, i]
            dov = do_ref[pl.ds(pl.multiple_of(i * c, c), c), :]
            w, kt = w_s[i], kt_s[i]
            vnew = u_s[i] - _dot(w, s)
            dvnew = _dot(att_s[i], dov) + _dot(kt, ds_next)
            du_s[i] = dvnew
            dw_s[i] = -_dot_nt(dvnew, s)
            dat_s[i] = _dot_nt(dov, vnew)
            dqd_s[i] = _dot_nt(dov, s)
            dkt_s[i] = _dot_nt(vnew, ds_next)
            dgl_s[i] = jnp.sum(jnp.sum(ds_next * s, axis=1, keepdims=True), axis=0, keepdims=True)
            return ds_next * gl_s[i] + _dot_tn(qd_s[i], dov) - _dot_tn(w, dvnew)
        lax.fori_loop(0, n, chunk_b, jnp.zeros((dk, dk), F32))

        def group_c(gi, carry):
            r0, n0, qg, kg, vg, gcv, grv, bt, loc = load_group(gi)
            sl = pl.ds(n0, nb)
            du, dw, dat, dqd, dkt, dgl = du_s[sl], dw_s[sl], dat_s[sl], dqd_s[sl], dkt_s[sl], dgl_s[sl]
            tm, decay, kb, kbg = loc["tm"], loc["decay"], loc["kb"], loc["kbg"]
            dvb = _bmm_tn(tm, du, X3)
            dkbg = _bmm_tn(tm, dw, X3)
            dt = _bmm_nt(du, loc["vb"], X3) + _bmm_nt(dw, kbg, X3)
            da = jnp.where(loc["strict"], -_bmm_tn(tm, _bmm_nt(dt, tm, X3), X3), 0.0)
            dms = (da * decay).astype(BF16)
            dqs = (dat * decay).astype(BF16)
            kgb = kg.astype(BF16)
            dkb = _bmm(dms, kgb) + dkbg * loc["eg"]
            dqt = _bmm(dqs, kgb) + dqd * loc["eg"]
            dkk = _bmm_tn(dms, kb.astype(BF16)) + _bmm_tn(dqs, qg.astype(BF16)) + dkt * loc["erel"] + dkb * bt
            e = (da * loc["m"] + dat * loc["qm"]) * decay
            lsum = lambda x: jnp.sum(x, axis=2, keepdims=True)
            dkt_kt = lsum(dkt * loc["kt"])
            dgcv = lsum(e) + lsum(dqd * loc["qd"]) - dkt_kt + lsum(dkbg * kbg)
            dglast = jnp.sum(dkt_kt, axis=1, keepdims=True) + dgl * loc["gl"]
            rowc = lax.broadcasted_iota(jnp.int32, (1, c, 1), 1)
            dgc_ref[0, sl] = dgcv + jnp.where(rowc == c - 1, dglast, 0.0)
            dgr_ref[0, sl] = -jnp.sum(e, axis=1, keepdims=True)
            db_ref[0, sl] = lsum(dkb * kg) + lsum(dvb * vg)
            rows = pl.ds(r0, nb * c)
            dq_ref[rows, :] = (dqt * scale).reshape(nb * c, dk)
            dk_ref[rows, :] = dkk.reshape(nb * c, dk)
            dv_ref[rows, :] = (dvb * bt).reshape(nb * c, dk)
            return carry
        lax.fori_loop(0, n // nb, group_c, 0)

    big = lambda: pltpu.VMEM((n, c, dk), F32)
    sq = lambda: pltpu.VMEM((n, c, c), F32)
    one = lambda: pltpu.VMEM((n, 1, 1), F32)
    scratch = [big(), big(), sq(), big(), big(), one(), big(), big(), sq(), big(), big(), one()]
    return pl.pallas_call(
        body, grid=(heads,), in_specs=[hs, hs, hs, gcs, grs, gcs, ss, hs], out_specs=[hs, hs, hs, gcs, grs, gcs],
        out_shape=[jax.ShapeDtypeStruct((t, d), F32)] * 3 + [
            jax.ShapeDtypeStruct((heads, n, c, 1), F32), jax.ShapeDtypeStruct((heads, n, 1, c), F32),
            jax.ShapeDtypeStruct((heads, n, c, 1), F32)],
        scratch_shapes=scratch, name=name, compiler_params=_cparams(("parallel",)))(q, k, v, gc, gr, bc, s_all, do)


def _dot01(x, m01):
    x1 = x.astype(BF16)
    r1 = x - x1.astype(F32)
    x2 = r1.astype(BF16)
    x3 = (r1 - x2.astype(F32)).astype(BF16)
    dot = lambda a: jnp.dot(a, m01, preferred_element_type=F32)
    return (dot(x1) + dot(x2)) + dot(x3)


def _sb_logs(z, mask):
    rows, bk = z.shape
    row = lax.broadcasted_iota(jnp.int32, (bk, bk), 0)
    col = lax.broadcasted_iota(jnp.int32, (bk, bk), 1)
    later = (row > col).astype(BF16)
    lm, ls, cs = [], [], []
    for p in [slice(p, p + SB_PART) for p in range(0, rows, SB_PART)]:
        lsm = -_softplus(z[p])
        lm.append(lsm if mask is None else jnp.where(mask[p], lsm, 0.0))
        ls.append(z[p] + lsm)
        cs.append(_dot01(lm[-1], later))
    cat = lambda xs: jnp.concatenate(xs, axis=0)
    return cat(ls), cat(cs), jnp.sum(cat(lm), axis=1, keepdims=True)


def _sb_masks(bq, bk):
    row = lax.broadcasted_iota(jnp.int32, (2 * bq, bk), 0)
    row = jnp.where(row >= bq, row - bq, row)
    col = lax.broadcasted_iota(jnp.int32, (2 * bq, bk), 1)
    return [(col + dd * bk) < row for dd in range(bq // bk)]


def _sb_sweep(bq, bk, qi, fetch, visit, carry):
    nd = bq // bk
    masks = _sb_masks(bq, bk)
    ahead = lambda j: fetch(jnp.maximum(j - 1, 0))
    pre = fetch(qi * nd + nd - 1)
    for dd in reversed(range(nd)):
        nxt = ahead(qi * nd + dd)
        carry = visit(qi * nd + dd, masks[dd], carry, pre)
        pre = nxt

    def below(it, c):
        j = qi * nd - 1 - it
        nxt = ahead(j)
        return visit(j, None, c[0], c[1]), nxt
    return lax.fori_loop(0, qi * nd, below, (carry, pre))[0]


def _sb_sweep2(bq, bk, qi, fetch, first, second, carry):
    nd = bq // bk
    masks = _sb_masks(bq, bk)
    ahead = lambda j: fetch(jnp.maximum(j - 1, 0))
    pre = fetch(qi * nd + nd - 1)
    r = jnp.zeros((2 * bq, 1), F32)
    pend = None
    for dd in reversed(range(nd)):
        j = qi * nd + dd
        nxt = ahead(j)
        ls, cs, rsum, extras = first(j, masks[dd], pre)
        if pend is not None:
            carry = second(pend[0], carry, *pend[1])
        pend = (j, (jnp.where(masks[dd], ls + cs + r, -1e30),) + extras)
        r, pre = r + rsum, nxt

    def below(it, c):
        carry, r, pre, pj, pargs = c
        j = qi * nd - 1 - it
        nxt = ahead(j)
        ls, cs, rsum, extras = first(j, None, pre)
        carry = second(pj, carry, *pargs)
        return carry, r + rsum, nxt, j, (ls + cs + r,) + extras
    carry, _, _, pj, pargs = lax.fori_loop(0, qi * nd, below, (carry, r, pre) + pend)
    return second(pj, carry, *pargs)


def _stack_heads(a, h0):
    return jnp.concatenate([jnp.where(h0, a, 0.0), jnp.where(h0, 0.0, a)], axis=0).astype(BF16)


def _side_by_side(a, bq):
    return jnp.concatenate([a[:bq], a[bq:]], axis=1)


def _sb_specs(t, d, bq, vcol):
    qs = pl.BlockSpec((bq, LANES), lambda g, i: (i, g))
    ks = pl.BlockSpec((t, LANES), lambda g, i: (0, g))
    vs = pl.BlockSpec((t, LANES), lambda g, i: (0, g + vcol))
    return qs, ks, vs


def _sb_fwd(qn, kn, qkv, name):
    t, d = qn.shape
    bq, bk = min(SB_QBLOCK, t), min(SB_BLOCK, t)
    scale = SB_HEAD_DIM ** -0.5
    nt = (((1,), (1,)), ((), ()))

    def body(q_ref, k_ref, v_ref, o_ref):
        qi = pl.program_id(1)
        h0 = lax.broadcasted_iota(jnp.int32, (1, LANES), 1) < SB_HEAD_DIM
        q2 = _stack_heads(q_ref[...] * scale, h0)

        tile_rows = lambda j: pl.ds(pl.multiple_of(j * bk, bk), bk)

        def fetch(j):
            return lax.dot_general(q2, k_ref[tile_rows(j), :].astype(BF16), nt, preferred_element_type=F32)

        def first(j, mask, z):
            return _sb_logs(z, mask) + ((),)

        def second(j, acc, loga):
            a2 = _side_by_side(jnp.exp(loga).astype(BF16), bq)
            return acc + jnp.dot(a2, _stack_heads(v_ref[tile_rows(j), :], h0), preferred_element_type=F32)
        acc = _sb_sweep2(bq, bk, qi, fetch, first, second, jnp.zeros((bq, LANES), F32))
        o_ref[...] = acc.astype(o_ref.dtype)

    qs, ks, vs = _sb_specs(t, d, bq, 2 * d // LANES)
    return pl.pallas_call(
        body, grid=(d // LANES, t // bq), in_specs=[qs, ks, vs], out_specs=qs,
        out_shape=jax.ShapeDtypeStruct((t, d), BF16), name=name,
        compiler_params=_cparams(("parallel", "parallel")))(qn, kn, qkv)


def _sb_bwd(qn, kn, qkv, do, name):
    t, d = qn.shape
    bq, bk = min(SB_QBLOCK, t), min(SB_BLOCK, t)
    scale = SB_HEAD_DIM ** -0.5
    nt = (((1,), (1,)), ((), ()))

    def body(q_ref, k_ref, v_ref, do_ref, dq_ref, dk_ref, dv_ref, p_s, sg_s):
        qi = pl.program_id(1)

        @pl.when(qi == 0)
        def _():
            dk_ref[...] = jnp.zeros(dk_ref.shape, F32)
            dv_ref[...] = jnp.zeros(dv_ref.shape, F32)

        h0 = lax.broadcasted_iota(jnp.int32, (1, LANES), 1) < SB_HEAD_DIM
        q2 = _stack_heads(q_ref[...] * scale, h0)
        do2 = _stack_heads(do_ref[...], h0)
        row = lax.broadcasted_iota(jnp.int32, (bk, bk), 0)
        col = lax.broadcasted_iota(jnp.int32, (bk, bk), 1)
        later_incl = (row >= col).astype(BF16)
        zero = jnp.zeros((2 * bq, 1), F32)
        tn = (((0,), (0,)), ((), ()))

        tile_rows = lambda j: pl.ds(pl.multiple_of(j * bk, bk), bk)

        def fetch1(j):
            return lax.dot_general(q2, k_ref[tile_rows(j), :].astype(BF16), nt, preferred_element_type=F32)

        def first1(j, mask, z):
            da = lax.dot_general(do2, v_ref[tile_rows(j), :].astype(BF16), nt, preferred_element_type=F32)
            ls, cs, rsum = _sb_logs(z, mask)
            sg_s[j] = jnp.exp(ls)
            return ls, cs, rsum, (da,)

        def second1(j, sp, loga, da):
            a = jnp.exp(loga)
            p = a * da
            p_s[j] = p
            dv_ref[tile_rows(j), :] += lax.dot_general(a.astype(BF16), do2, tn, preferred_element_type=F32)
            return sp + jnp.sum(p, axis=1, keepdims=True)
        total = _sb_sweep2(bq, bk, qi, fetch1, first1, second1, zero)

        def fetch2(j):
            return _dot01(p_s[j], later_incl)

        def visit2(j, mask, carry, later_p):
            dq, sp = carry
            rows = tile_rows(j)
            p, sg = p_s[j], sg_s[j]
            pref = total - sp - later_p
            dz = p * (1.0 - sg) - pref * sg
            if mask is not None:
                dz = jnp.where(mask, dz, 0.0)
            dz = dz.astype(BF16)
            dk_ref[rows, :] += lax.dot_general(dz, q2, tn, preferred_element_type=F32)
            dq = dq + jnp.dot(_side_by_side(dz, bq), _stack_heads(k_ref[rows, :], h0), preferred_element_type=F32)
            return dq, sp + jnp.sum(p, axis=1, keepdims=True)
        dq, _ = _sb_sweep(bq, bk, qi, fetch2, visit2, (jnp.zeros((bq, LANES), F32), zero))
        dq_ref[...] = dq * scale

    qs, ks, vs = _sb_specs(t, d, bq, 2 * d // LANES)
    shp = jax.ShapeDtypeStruct((t, d), F32)
    scratch = [pltpu.VMEM((t // bk, 2 * bq, bk), F32), pltpu.VMEM((t // bk, 2 * bq, bk), F32)]
    return pl.pallas_call(
        body, grid=(d // LANES, t // bq), in_specs=[qs, ks, vs, qs], out_specs=[qs, ks, ks],
        out_shape=[shp, shp, shp], scratch_shapes=scratch, name=name,
        compiler_params=_cparams(("parallel", "arbitrary")))(qn, kn, qkv, do)


def _adamw(w, g, m, v, name):
    shape = w.shape
    two_d = lambda a: a.reshape(-1, shape[-1])
    c1 = 1.0 - ADAM_B1 ** ADAM_STEP
    c2 = 1.0 - ADAM_B2 ** ADAM_STEP

    def body(r, _):
        wv, gv, mv, vv = r
        mn = ADAM_B1 * mv + (1.0 - ADAM_B1) * gv
        vn = ADAM_B2 * vv + (1.0 - ADAM_B2) * (gv * gv)
        delta = -ADAM_LR * ((mn / c1) / (jnp.sqrt(vn / c2) + ADAM_EPS) + ADAM_WD * wv)
        return [delta, mn, vn], []
    width = shape[-1]
    res = _rowwise(body, [two_d(w), two_d(g), two_d(m), two_d(v)], [], [(width, F32)] * 3, [], name=name)
    return [r.reshape(shape) for r in res]


def _dn_layer_fwd(h1, w, conv_w, a_log, dt_bias, onorm_g, tag):
    t, d = h1.shape
    heads = d // DN_HEAD_DIM
    n = t // DN_CHUNK
    proj = _mm(h1, w["all"], name=tag + "_proj")
    qn, kn, vv = _dn_prep_fwd(proj, conv_w, heads, tag + "_prep")
    big_g, beta = _dn_gate_fwd(proj, 4 * d // (2 * LANES), a_log, dt_bias, tag + "_gate")
    gt_ = big_g[:, :heads].T.reshape(heads, n, DN_CHUNK)
    gc, gr = gt_[..., None], gt_[:, :, None, :]
    bc = beta[:, :heads].T.reshape(heads, n, DN_CHUNK)[..., None]
    o, s_all = _dn_chunk_fwd(qn, kn, vv, gc, gr, bc, tag + "_chunk")
    on = _dn_post_fwd(o, proj, onorm_g, tag + "_post")
    y = _mm(on, w["out"], name=tag + "_out")
    return y, dict(proj=proj, qn=qn, kn=kn, v=vv, gc=gc, gr=gr, bc=bc, o=o, s_all=s_all, on=on)


def _dn_layer_bwd(dy, h1, w, conv_w, a_log, dt_bias, onorm_g, sv, tag):
    t, d = h1.shape
    heads = d // DN_HEAD_DIM
    don = _mm(dy, w["out"], tb=True, name=tag + "_dout")
    g_out = _mm(sv["on"], dy, ta=True, out_dtype=BF16, name=tag + "_gwout")
    do, dz, g_on = _dn_post_bwd(sv["o"], sv["proj"], don, onorm_g, tag + "_dpost")
    dq, dk, dv, dgc, dgr, dbc = _dn_chunk_bwd(sv["qn"], sv["kn"], sv["v"], sv["gc"], sv["gr"], sv["bc"], sv["s_all"],
                                              do, tag + "_dchunk")
    pad = lambda a: jnp.pad(a.reshape(heads, t).T, ((0, 0), (0, LANES - heads)))
    d_big_g = pad(dgc) + pad(dgr)
    dab, g_alog, g_dt = _dn_gate_bwd(sv["proj"], 4 * d // (2 * LANES), d_big_g, pad(dbc), a_log, dt_bias, tag + "_dgate")
    dxq, dxk, dxv, wq, wk, wv = _dn_prep_bwd(sv["proj"], conv_w, dq, dk, dv, heads, tag + "_dprep")
    dproj = jnp.concatenate([dxq, dxk, dxv, dz, dab], axis=1)
    dh1 = _mm(dproj, w["all"], tb=True, name=tag + "_dh")
    g_all = _mm(h1, dproj, ta=True, out_dtype=BF16, name=tag + "_gwin")
    grads = dict(w_all=g_all, w_out=g_out, conv_w=jnp.concatenate([wq, wk, wv], axis=1), a_log=g_alog, dt_bias=g_dt,
                 onorm_g=g_on)
    return dh1, grads


def _sb_layer_fwd(h1, w, q_g, k_g, tag):
    t, d = h1.shape
    heads = d // SB_HEAD_DIM
    qkv = _mm(h1, w["qkv"], name=tag + "_proj")
    gq, gk = jnp.tile(q_g, (1, heads)), jnp.tile(k_g, (1, heads))
    qn, kn = _sb_norm_fwd(qkv, gq, gk, tag + "_norm")
    o = _sb_fwd(qn, kn, qkv, tag + "_attn")
    y = _mm(o, w["out"], name=tag + "_out")
    return y, dict(qkv=qkv, qn=qn, kn=kn, o=o, gq=gq, gk=gk)


def _sb_layer_bwd(dy, h1, w, q_g, k_g, sv, tag):
    t, d = h1.shape
    heads = d // SB_HEAD_DIM
    do = _mm(dy, w["out"], tb=True, name=tag + "_dout")
    g_out = _mm(sv["o"], dy, ta=True, out_dtype=BF16, name=tag + "_gwout")
    dqn, dkn, dv = _sb_bwd(sv["qn"], sv["kn"], sv["qkv"], do, tag + "_dattn")
    dqkv, g_q, g_k = _sb_norm_bwd(sv["qkv"], dqn, dkn, dv, sv["gq"], sv["gk"], tag + "_dnorm")
    fold = lambda g: jnp.sum(g.reshape(heads, SB_HEAD_DIM), axis=0, keepdims=True)
    dh1 = _mm(dqkv, w["qkv"], tb=True, name=tag + "_dh")
    g_qkv = _mm(h1, dqkv, ta=True, out_dtype=BF16, name=tag + "_gwin")
    return dh1, dict(w_qkv=g_qkv, w_out=g_out, q_norm_g=fold(g_q), k_norm_g=fold(g_k))


def _local_step(x, tgt, mod, norm1_g, norm2_g, layer_weights, layer_grads):
    depth = mod.shape[0]
    d = x.shape[1]
    saved = []
    for i in range(depth):
        mix_w, ffn_w = layer_weights(i, x)
        mv = [mod[i:i + 1, j * d:(j + 1) * d] for j in range(N_MOD)]
        sh1, sc1, gt1, sh2, sc2, gt2 = mv
        tag = "l%d" % i
        h1 = _adaln_fwd(x, norm1_g[i:i + 1], sc1, sh1, tag + "_ln1")
        if i % 2 == 0:
            p = mix_w
            y, sv = _dn_layer_fwd(h1, p, p["conv_w"], p["a_log"], p["dt_bias"], p["onorm_g"], tag + "_dn")
        else:
            p = mix_w
            y, sv = _sb_layer_fwd(h1, p, p["q_g"], p["k_g"], tag + "_sb")
        x1 = _resid_fwd(x, y, gt1, tag + "_res1")
        h2 = _adaln_fwd(x1, norm2_g[i:i + 1], sc2, sh2, tag + "_ln2")
        u = _mm(h2, ffn_w["w_in"], name=tag + "_ffn_in")
        a = _swiglu_fwd(u, tag + "_swiglu")
        y2 = _mm(a, ffn_w["w_out"], name=tag + "_ffn_out")
        x2 = _resid_fwd(x1, y2, gt2, tag + "_res2")
        saved.append(dict(x0=x, h1=h1, y=y, mix=sv, x1=x1, h2=h2, u=u, a=a, y2=y2, mix_w=mix_w, ffn_w=ffn_w))
        x = x2

    loss, dx = _loss_fwd_bwd(x, tgt, "loss")

    dmod, dn1, dn2 = [None] * depth, [None] * depth, [None] * depth
    zero = jnp.zeros((), F32)
    for i in reversed(range(depth)):
        s = saved[i]
        mv = [mod[i:i + 1, j * d:(j + 1) * d] + zero for j in range(N_MOD)]
        sh1, sc1, gt1, sh2, sc2, gt2 = mv
        tag = "l%d" % i
        ffn_w, p = s["ffn_w"], s["mix_w"]
        dy2, dgt2 = _resid_bwd(dx, s["y2"], gt2, tag + "_dres2")
        da = _mm(dy2, ffn_w["w_out"], tb=True, name=tag + "_dffn_a")
        g_wout = _mm(s["a"], dy2, ta=True, out_dtype=BF16, name=tag + "_gffn_out")
        du = _swiglu_bwd(s["u"], da, tag + "_dswiglu")
        dh2 = _mm(du, ffn_w["w_in"], tb=True, name=tag + "_dffn_h")
        g_win = _mm(s["h2"], du, ta=True, out_dtype=BF16, name=tag + "_gffn_in")
        dx, dg2, dsc2, dsh2 = _adaln_bwd(s["x1"], dh2, dx, norm2_g[i:i + 1], sc2, tag + "_dln2")
        dy, dgt1 = _resid_bwd(dx, s["y"], gt1, tag + "_dres1")
        if i % 2 == 0:
            dh1, g_mix = _dn_layer_bwd(dy, s["h1"], p, p["conv_w"], p["a_log"], p["dt_bias"], p["onorm_g"], s["mix"],
                                       tag + "_dn")
        else:
            dh1, g_mix = _sb_layer_bwd(dy, s["h1"], p, p["q_g"], p["k_g"], s["mix"], tag + "_sb")
        dx, dg1, dsc1, dsh1 = _adaln_bwd(s["x0"], dh1, dx, norm1_g[i:i + 1], sc1, tag + "_dln1")
        dmod[i] = jnp.concatenate([dsh1, dsc1, dgt1, dsh2, dsc2, dgt2], axis=1)
        dn1[i], dn2[i] = dg1, dg2
        zero = layer_grads(i, dx, g_mix, dict(w_in=g_win, w_out=g_wout))
    return loss, dx, jnp.concatenate(dmod, axis=0), jnp.concatenate(dn1, axis=0), jnp.concatenate(dn2, axis=0)


def _axes():
    return lax.axis_index("x"), lax.axis_index("y"), lax.axis_index("c")


def _remote(src, dst, send_sem, recv_sem, dev):
    return pltpu.make_async_remote_copy(src_ref=src, dst_ref=dst, send_sem=send_sem, recv_sem=recv_sem,
                                        device_id=dev, device_id_type=MESH)


def _other_chips(x, y):
    return [(1 - x, y), (x, 1 - y), (1 - x, 1 - y)]


def _allgather_small(v, name):
    m, n = v.shape

    def body(x_ref, out_ref, send_sems, recv_sems, local_sem):
        x, y, c = _axes()
        me, sibling = (x, y, c), (x, y, 1 - c)
        chips = _other_chips(x, y)

        def rows(px, py, pc):
            return out_ref.at[pl.ds((4 * px + 2 * py + pc) * m, m), :]

        def copy(k, block, to, src=None):
            return _remote(rows(*block) if src is None else src, rows(*block), send_sems.at[k], recv_sems.at[k], to)

        mine = pltpu.make_async_copy(x_ref, rows(*me), local_sem)
        mine.start()
        first = [copy(0, me, sibling, src=x_ref)]
        first += [copy(1 + j, me, (*chip, c), src=x_ref) for j, chip in enumerate(chips)]
        for cp in first:
            cp.start()
        passed = [copy(4 + j, (*chip, c), sibling) for j, chip in enumerate(chips)]
        for j, chip in enumerate(chips):
            copy(1 + j, (*chip, c), me).wait_recv()
            passed[j].start()
        copy(0, sibling, me).wait_recv()
        for j, chip in enumerate(chips):
            copy(4 + j, (*chip, 1 - c), me).wait_recv()
        for cp in first + passed:
            cp.wait_send()
        mine.wait()

    return pl.pallas_call(
        body, out_shape=jax.ShapeDtypeStruct((8 * m, n), v.dtype),
        in_specs=[pl.BlockSpec(memory_space=pltpu.VMEM)], out_specs=pl.BlockSpec(memory_space=pltpu.VMEM),
        scratch_shapes=[pltpu.SemaphoreType.DMA((7,)), pltpu.SemaphoreType.DMA((7,)), pltpu.SemaphoreType.DMA],
        name=name, compiler_params=pltpu.CompilerParams(vmem_limit_bytes=VMEM_LIMIT))(v)


def _half(ref, h, rh):
    return ref.at[(slice(None),) * (len(ref.shape) - 2) + (pl.ds(h * rh, rh), slice(None))]


def _hbm_call(body, ins, out_shapes, n_sems, n_local, name):
    hbm = pl.BlockSpec(memory_space=pltpu.HBM)
    scratch = [pltpu.SemaphoreType.DMA((n_sems,)), pltpu.SemaphoreType.DMA((n_sems,))]
    if n_local:
        scratch.append(pltpu.SemaphoreType.DMA((n_local,)))
    return pl.pallas_call(body, out_shape=out_shapes, in_specs=[hbm] * len(ins), out_specs=[hbm] * len(out_shapes),
                          scratch_shapes=scratch, name=name)(*ins)


def _sibling_swap_halves(arrs, name):
    n = len(arrs)

    def body(*refs):
        v_refs, out_refs, (send_sems, recv_sems) = refs[:n], refs[n:2 * n], refs[2 * n:]
        x, y, c = _axes()
        cps = [_remote(_half(v_refs[i], 1 - c, arrs[i].shape[-2] // 2), out_refs[i], send_sems.at[i], recv_sems.at[i],
                       (x, y, 1 - c)) for i in range(n)]
        for cp in cps:
            cp.start()
        for cp in cps:
            cp.wait()

    outs = [jax.ShapeDtypeStruct(a.shape[:-2] + (a.shape[-2] // 2, a.shape[-1]), a.dtype) for a in arrs]
    return _hbm_call(body, arrs, outs, n, 0, name)


def _ici_src(ref, mode, slot, c):
    return _half(ref, c, ref.shape[-2] // 2) if mode == "half" else ref.at[slot]


def _ici_start(srcs, mode, name):
    n, ncp = len(srcs), 3 * len(srcs)
    lands = [jnp.zeros(((4,) + s.shape[:-2] + (s.shape[-2] // 2, s.shape[-1])) if mode == "half" else s.shape, s.dtype)
             for s in srcs]

    def body(*refs):
        src_refs, land_refs = refs[:n], refs[n:2 * n]
        send_sems, recv_sems = refs[2 * n:2 * n + ncp], refs[2 * n + ncp:2 * n + 2 * ncp]
        token = refs[-1]
        x, y, c = _axes()
        for i in range(n):
            for j, (cx, cy) in enumerate(_other_chips(x, y)):
                _remote(_ici_src(src_refs[i], mode, 2 * cx + cy, c), land_refs[i].at[2 * x + y], send_sems[3 * i + j],
                        recv_sems[3 * i + j], (cx, cy, c)).start()
        token[...] = jnp.zeros(token.shape, token.dtype)

    hbm, sem = pl.BlockSpec(memory_space=pltpu.HBM), pl.BlockSpec(memory_space=pltpu.SEMAPHORE)
    bufs = srcs + lands
    out_shape = tuple([pltpu.SemaphoreType.DMA(())] * (2 * ncp) + [pltpu.HBM(b.shape, b.dtype) for b in bufs]
                      + [jax.ShapeDtypeStruct((8, LANES), F32)])
    res = pl.pallas_call(
        body, name=name, out_shape=out_shape, in_specs=(hbm,) * (2 * n),
        out_specs=(sem,) * (2 * ncp) + (hbm,) * (2 * n) + (pl.BlockSpec(memory_space=pltpu.VMEM),),
        input_output_aliases={i: 2 * ncp + i for i in range(2 * n)},
        compiler_params=pltpu.CompilerParams(has_side_effects=pltpu.SideEffectType.DATAFLOW_SIDE_EFFECTING),
    )(*[pltpu.with_memory_space_constraint(b, pltpu.HBM) for b in bufs])
    return list(res[:2 * ncp]), list(res[2 * ncp:2 * ncp + 2 * n]), res[-1][0, 0]


def _ici_wait(sems, bufs, mode, after, name):
    n, ncp = len(bufs) // 2, len(sems) // 2

    def body(*refs):
        src_refs, land_refs = refs[:n], refs[n:2 * n]
        send_sems, recv_sems = refs[2 * n:2 * n + ncp], refs[2 * n + ncp:2 * n + 2 * ncp]
        x, y, c = _axes()
        for i in range(n):
            for j, (cx, cy) in enumerate(_other_chips(x, y)):
                cp = _remote(_ici_src(src_refs[i], mode, 2 * cx + cy, c), land_refs[i].at[2 * cx + cy],
                             send_sems[3 * i + j], recv_sems[3 * i + j], (cx, cy, c))
                cp.wait_send()
                cp.wait_recv()

    hbm, sem = pl.BlockSpec(memory_space=pltpu.HBM), pl.BlockSpec(memory_space=pltpu.SEMAPHORE)
    res = pl.pallas_call(
        body, name=name, out_shape=tuple(pltpu.HBM(b.shape, b.dtype) for b in bufs),
        in_specs=(hbm,) * (2 * n) + (sem,) * (2 * ncp) + (pl.BlockSpec(memory_space=pl.ANY),),
        out_specs=(hbm,) * (2 * n), input_output_aliases={i: i for i in range(2 * n)},
        compiler_params=pltpu.CompilerParams(has_side_effects=pltpu.SideEffectType.DATAFLOW_SIDE_EFFECTING),
    )(*bufs, *sems, after)
    return list(res[n:])


def _share_halves(arrs, name):
    n = len(arrs)

    def body(*refs):
        v_refs, out_refs, (send_sems, recv_sems) = refs[:n], refs[n:2 * n], refs[2 * n:]
        x, y, c = _axes()
        cps = [_remote(v_refs[i], out_refs[i], send_sems.at[i], recv_sems.at[i], (x, y, 1 - c)) for i in range(n)]
        for cp in cps:
            cp.start()
        for cp in cps:
            cp.wait()

    outs = [jax.ShapeDtypeStruct(a.shape, a.dtype) for a in arrs]
    return _hbm_call(body, arrs, outs, n, 0, name)


def _assemble_weights(mine, other, own, axis, name):
    nchips, l, rh, cs = mine.shape
    r = 2 * rh

    def body(m_ref, o_ref, w_ref, out_ref):
        x, y, c = _axes()
        is_own = pl.program_id(1) == 2 * x + y

        @pl.when(is_own)
        def _():
            out_ref[...] = w_ref[...]

        @pl.when(jnp.logical_not(is_own))
        def _():
            out_ref[pl.ds(pl.multiple_of(c * rh, rh), rh), :] = m_ref[...]
            out_ref[pl.ds(pl.multiple_of((1 - c) * rh, rh), rh), :] = o_ref[...]

    half = pl.BlockSpec((None, None, rh, cs), lambda i, s: (s, i, 0, 0))
    whole = pl.BlockSpec((None, r, cs), lambda i, s: (i, 0, 0))
    if axis == 2:
        out_spec, shape = pl.BlockSpec((None, r, cs), lambda i, s: (i, 0, s)), (l, r, nchips * cs)
    else:
        out_spec, shape = pl.BlockSpec((None, r, cs), lambda i, s: (i, s, 0)), (l, nchips * r, cs)
    return pl.pallas_call(
        body, grid=(l, nchips), in_specs=[half, half, whole], out_specs=out_spec,
        out_shape=jax.ShapeDtypeStruct(shape, mine.dtype), name=name,
        compiler_params=_cparams(("parallel", "arbitrary")))(mine, other, own)


def _sum_chips(land, own, name):
    k, r, w = land.shape
    tr = _pick_tile(r, (256, 128, 64, 32, 16))

    def body(land_ref, own_ref, o_ref):
        x, y, _ = _axes()
        me = 2 * x + y
        acc = jnp.zeros((tr, w), F32)
        for s in range(k):
            acc = acc + jnp.where(me == s, own_ref[s], land_ref[s]).astype(F32)
        o_ref[...] = acc

    spec = pl.BlockSpec((k, tr, w), lambda i: (0, i, 0))
    return pl.pallas_call(
        body, grid=(r // tr,), in_specs=[spec, spec],
        out_specs=pl.BlockSpec((tr, w), lambda i: (i, 0)), out_shape=jax.ShapeDtypeStruct((r, w), F32), name=name,
        compiler_params=_cparams(("parallel",)))(land, own)


_BIG = (("dn_w_in", 2), ("dn_w_out", 1), ("sb_w_qkv", 2), ("sb_w_out", 1), ("ffn_w_in", 2), ("ffn_w_out", 1))


def _to_chip_major(a, axis, nchips):
    l, r, c = a.shape
    if axis == 2:
        return a.reshape(l, r, nchips, c // nchips).transpose(2, 0, 1, 3)
    return a.reshape(l, nchips, r // nchips, c).transpose(1, 0, 2, 3)


def _from_chip_major(a, axis):
    n, l, r, c = a.shape
    if axis == 2:
        return a.transpose(1, 2, 0, 3).reshape(l, r, n * c)
    return a.transpose(1, 0, 2, 3).reshape(l, n * r, c)


def kernel(x, c, ada_w, ada_b, norm1_g, norm2_g, dn_w_in, dn_conv_w, dn_a_log, dn_dt_bias, dn_onorm_g, dn_w_out, sb_w_qkv, sb_q_norm_g, sb_k_norm_g, sb_w_out, ffn_w_in, ffn_w_out, loss_target, m_ada_w, m_ada_b, m_norm1_g, m_norm2_g, m_dn_w_in, m_dn_conv_w, m_dn_a_log, m_dn_dt_bias, m_dn_onorm_g, m_dn_w_out, m_sb_w_qkv, m_sb_q_norm_g, m_sb_k_norm_g, m_sb_w_out, m_ffn_w_in, m_ffn_w_out, v_ada_w, v_ada_b, v_norm1_g, v_norm2_g, v_dn_w_in, v_dn_conv_w, v_dn_a_log, v_dn_dt_bias, v_dn_onorm_g, v_dn_w_out, v_sb_w_qkv, v_sb_q_norm_g, v_sb_k_norm_g, v_sb_w_out, v_ffn_w_in, v_ffn_w_out):
    names = ("ada_w", "ada_b", "norm1_g", "norm2_g", "dn_w_in", "dn_conv_w", "dn_a_log", "dn_dt_bias", "dn_onorm_g",
             "dn_w_out", "sb_w_qkv", "sb_q_norm_g", "sb_k_norm_g", "sb_w_out", "ffn_w_in", "ffn_w_out")
    w = dict(zip(names, (ada_w, ada_b, norm1_g, norm2_g, dn_w_in, dn_conv_w, dn_a_log, dn_dt_bias, dn_onorm_g,
                         dn_w_out, sb_w_qkv, sb_q_norm_g, sb_k_norm_g, sb_w_out, ffn_w_in, ffn_w_out)))
    mom = dict(zip(names, (m_ada_w, m_ada_b, m_norm1_g, m_norm2_g, m_dn_w_in, m_dn_conv_w, m_dn_a_log, m_dn_dt_bias,
                           m_dn_onorm_g, m_dn_w_out, m_sb_w_qkv, m_sb_q_norm_g, m_sb_k_norm_g, m_sb_w_out, m_ffn_w_in,
                           m_ffn_w_out)))
    var = dict(zip(names, (v_ada_w, v_ada_b, v_norm1_g, v_norm2_g, v_dn_w_in, v_dn_conv_w, v_dn_a_log, v_dn_dt_bias,
                           v_dn_onorm_g, v_dn_w_out, v_sb_w_qkv, v_sb_q_norm_g, v_sb_k_norm_g, v_sb_w_out, v_ffn_w_in,
                           v_ffn_w_out)))
    ax, ay, ac = _axes()
    chip = 2 * ax + ay
    dev = 2 * chip + ac
    t, d = x.shape[1], x.shape[2]
    depth, ndn, nsb = ada_w.shape[0], dn_w_in.shape[0], sb_w_qkv.shape[0]
    heads = d // DN_HEAD_DIM
    mod_cols = ada_w.shape[2]
    conv_cols = dn_conv_w.shape[2]
    nchips, ndev = 4, 8

    conv_rows = ndn * DN_CONV * conv_cols // d
    pay1 = jnp.concatenate([c, dn_conv_w.reshape(conv_rows, d), jnp.zeros((8 - 1 - conv_rows, d), F32)], axis=0)
    g1 = _allgather_small(pay1, "ag_cond").reshape(ndev, 8, d)
    c_all = g1[:, 0]
    conv_full = g1[::2, 1:1 + conv_rows].reshape(nchips, ndn, DN_CONV, conv_cols).transpose(1, 2, 0, 3)
    conv_full = conv_full.reshape(ndn, DN_CONV, nchips * conv_cols)

    c16 = jnp.pad(c_all, ((0, 16 - ndev), (0, 0)))
    cond16 = _rowwise(lambda r, v: ([_silu(r[0])], []), [c16], [], [(d, F32)], [], name="cond_silu")[0]
    pay2 = jnp.concatenate([_mm(cond16, ada_w[i], name="ada_mod%d" % i)[:ndev] for i in range(depth)], axis=0)
    g2 = _allgather_small(pay2, "ag_mod").reshape(ndev, depth, ndev, mod_cols)[::2]
    mod_raw = lax.dynamic_index_in_dim(g2, dev, axis=2, keepdims=False)
    mod_raw = mod_raw.transpose(1, 0, 2).reshape(depth, nchips * mod_cols)
    mod = _rowwise(lambda r, v: ([r[0] + r[1]], []), [mod_raw, ada_b], [], [(nchips * mod_cols, F32)], [],
                   name="ada_bias")[0]

    axis_of = dict(_BIG)
    groups = [[0], list(range(1, depth))]

    def kinds_of(layers):
        out = []
        for n, _ in _BIG:
            idx = [l // 2 for l in layers if l % 2 == (0 if n.startswith("dn_") else 1)] if n[:3] in ("dn_", "sb_") \
                else list(layers)
            if idx:
                out.append((n, min(idx), max(idx) + 1))
        return out

    two_d = lambda a: a.reshape(-1, a.shape[-1])
    both_halves = lambda mine, other, ax: jnp.where(ac == 0, jnp.concatenate([mine, other], axis=ax),
                                                    jnp.concatenate([other, mine], axis=ax))
    w16 = {n: w[n].astype(BF16) for n, _ in _BIG}
    gathers = []
    for gi, layers in enumerate(groups):
        sl = kinds_of(layers)
        shards = [w16[n][lo:hi] for n, lo, hi in sl]
        sems, bufs, zero = _ici_start(shards, "half", "ag_start%d" % gi)
        gathers.append((sl, shards, sems, bufs))
        mod = mod + zero
    full = {}

    def finish_gather(gi, after):
        sl, shards, sems, bufs = gathers[gi]
        lands = _ici_wait(sems, bufs, "half", after, "ag_wait%d" % gi)
        for (n, lo, hi), shard, mine, other in zip(sl, shards, lands, _share_halves(lands, "ag_pair%d" % gi)):
            if shard.shape[-1] % LANES == 0:
                a = _assemble_weights(mine, other, shard, axis_of[n], "ag_whole%d_%s" % (gi, n))
            else:
                a = lax.dynamic_update_index_in_dim(both_halves(mine, other, 2), shard, chip, 0)
                a = _from_chip_major(a, axis_of[n])
            for l in range(lo, hi):
                full[n, l] = a[l - lo]

    padl = lambda v: jnp.pad(v[None, :], ((0, 0), (0, LANES - v.shape[0])))
    padc = lambda a: jnp.pad(a, ((0, 0), (0, LANES - a.shape[1])))

    def layer_weights(i, after):
        for gi, layers in enumerate(groups):
            if layers and i == layers[0]:
                finish_gather(gi, after)
        j = i // 2
        if i % 2 == 0:
            wi = full["dn_w_in", j]
            w_all = jnp.concatenate([wi[:, :4 * d], padc(wi[:, 4 * d:4 * d + heads]), padc(wi[:, 4 * d + heads:])],
                                    axis=1)
            mix = dict(all=w_all, out=full["dn_w_out", j], conv_w=conv_full[j], a_log=padl(dn_a_log[j]),
                       dt_bias=padl(dn_dt_bias[j]), onorm_g=dn_onorm_g[j][None])
        else:
            mix = dict(qkv=full["sb_w_qkv", j], out=full["sb_w_out", j], q_g=sb_q_norm_g[j][None],
                       k_g=sb_k_norm_g[j][None])
        return mix, dict(w_in=full["ffn_w_in", i], w_out=full["ffn_w_out", i])

    g_dn, g_sb, g_ffn = [None] * ndn, [None] * nsb, [None] * depth
    reduces = {}

    def gw_in(g):
        ga = g["w_all"]
        return jnp.concatenate([ga[:, :4 * d], ga[:, 4 * d:4 * d + heads], ga[:, 4 * d + LANES:4 * d + LANES + heads]],
                               axis=1)
    layer_grad = dict(dn_w_in=lambda j: gw_in(g_dn[j]), dn_w_out=lambda j: g_dn[j]["w_out"],
                      sb_w_qkv=lambda j: g_sb[j]["w_qkv"], sb_w_out=lambda j: g_sb[j]["w_out"],
                      ffn_w_in=lambda l: g_ffn[l]["w_in"], ffn_w_out=lambda l: g_ffn[l]["w_out"])

    def layer_grads(i, after, g_mix, g_f):
        (g_dn if i % 2 == 0 else g_sb)[i // 2], g_ffn[i] = g_mix, g_f
        zero = jnp.zeros((), F32)
        for gi, layers in enumerate(groups):
            if layers and i == layers[0]:
                sl = kinds_of(layers)
                parts = [_to_chip_major(jnp.stack([layer_grad[n](j) for j in range(lo, hi)]), axis_of[n], nchips)
                         for n, lo, hi in sl]
                pairs = []
                for (n, _, _), p, fs in zip(sl, parts, _sibling_swap_halves(parts, "gr_pair%d" % gi)):
                    rh = fs.shape[2]
                    own = lax.dynamic_slice_in_dim(p, ac * rh, rh, axis=2)
                    pairs.append(_rowwise(lambda r, v: ([r[0].astype(F32) + r[1].astype(F32)], []),
                                          [two_d(own), two_d(fs)], [], [(fs.shape[-1], BF16)], [],
                                          name="gr_pair_add%d_%s" % (gi, n))[0].reshape(fs.shape))
                sems, bufs, zero = _ici_start(pairs, "slot", "gr_start%d" % gi)
                reduces[gi] = (sl, pairs, sems, bufs)
        return zero

    loss_local, grad_x, dmod, g_n1, g_n2 = _local_step(x[0], loss_target[0], mod, norm1_g, norm2_g, layer_weights,
                                                       layer_grads)
    loss = lax.psum(loss_local, ("x", "y", "c"))

    g_conv = jnp.stack([g["conv_w"] for g in g_dn])
    misc = jnp.concatenate([jnp.concatenate([g["onorm_g"] for g in g_dn], axis=1),
                            jnp.concatenate([g["a_log"] for g in g_dn], axis=1),
                            jnp.concatenate([g["dt_bias"] for g in g_dn], axis=1),
                            jnp.concatenate([g["q_norm_g"] for g in g_sb], axis=1),
                            jnp.concatenate([g["k_norm_g"] for g in g_sb], axis=1)], axis=1)
    misc = jnp.pad(misc, ((0, 0), (0, -misc.shape[1] % d))).reshape(-1, d)
    small = [dmod.reshape(-1, d), g_n1, g_n2, g_conv.reshape(-1, d), misc]
    small_rows = [s.shape[0] for s in small]
    pad_rows = -sum(small_rows) % 8
    pay3 = jnp.concatenate(small + [jnp.zeros((pad_rows, d), F32)], axis=0)
    nrow3 = pay3.shape[0]
    g3 = _allgather_small(pay3, "ag_small")
    summed = _rowwise(lambda r, v: ([], [_colsum(r[0])]), [g3.reshape(ndev, nrow3 * d)], [], [], [(1, nrow3 * d)],
                      name="small_sum")[0].reshape(nrow3, d)
    offs = [0]
    for n_ in small_rows:
        offs.append(offs[-1] + n_)
    grads = {}
    grads["ada_b"] = summed[offs[0]:offs[1]].reshape(depth, N_MOD * d)
    grads["norm1_g"] = summed[offs[1]:offs[2]]
    grads["norm2_g"] = summed[offs[2]:offs[3]]
    conv_sum = summed[offs[3]:offs[4]].reshape(ndn, DN_CONV, nchips * conv_cols)
    grads["dn_conv_w"] = lax.dynamic_slice_in_dim(conv_sum, chip * conv_cols, conv_cols, axis=2)
    mrow = summed[offs[4]:offs[5]].reshape(-1)
    o = 0
    grads["dn_onorm_g"] = mrow[o:o + ndn * DN_HEAD_DIM].reshape(ndn, DN_HEAD_DIM)
    o += ndn * DN_HEAD_DIM
    grads["dn_a_log"] = mrow[o:o + ndn * LANES].reshape(ndn, LANES)[:, :heads]
    o += ndn * LANES
    grads["dn_dt_bias"] = mrow[o:o + ndn * LANES].reshape(ndn, LANES)[:, :heads]
    o += ndn * LANES
    grads["sb_q_norm_g"] = mrow[o:o + nsb * SB_HEAD_DIM].reshape(nsb, SB_HEAD_DIM)
    o += nsb * SB_HEAD_DIM
    grads["sb_k_norm_g"] = mrow[o:o + nsb * SB_HEAD_DIM].reshape(nsb, SB_HEAD_DIM)
    dmod_all = g3.reshape(ndev, nrow3, d)[:, :small_rows[0]].reshape(ndev, depth, N_MOD * d)
    dmod_mine = lax.dynamic_slice_in_dim(dmod_all, chip * mod_cols, mod_cols, axis=2)
    dmod16 = jnp.pad(dmod_mine, ((0, 16 - ndev), (0, 0), (0, 0)))
    grads["ada_w"] = jnp.stack([_mm(cond16, dmod16[:, i], ta=True, name="ada_gw%d" % i) for i in range(depth)])

    pieces = {n: [] for n, _ in _BIG}
    for gi in reversed(range(len(groups))):
        if gi not in reduces:
            continue
        sl, pairs, sems, bufs = reduces[gi]
        lands = _ici_wait(sems, bufs, "slot", grad_x, "gr_wait%d" % gi)
        flat = lambda a: a.reshape(nchips, -1, a.shape[-1])
        reduced = [_sum_chips(flat(land), flat(pair), "gr_chip_add%d_%s" % (gi, n))
                   for (n, _, _), land, pair in zip(sl, lands, pairs)]
        for (n, lo, hi), mine, other in zip(sl, reduced, _share_halves(reduced, "gr_share%d" % gi)):
            shape = (hi - lo, -1, mine.shape[-1])
            pieces[n].append((lo, both_halves(mine.reshape(shape), other.reshape(shape), 1)))
    for n, _ in _BIG:
        grads[n] = jnp.concatenate([p for _, p in sorted(pieces[n], key=lambda t: t[0])], axis=0)

    delta, new_m, new_v = {}, {}, {}
    for n in names:
        delta[n], new_m[n], new_v[n] = _adamw(w[n], grads[n], mom[n], var[n], "adamw_" + n)
    return (loss, grad_x[None], *[grads[n] for n in names], *[delta[n] for n in names], *[new_m[n] for n in names],
            *[new_v[n] for n in names])
```

```python
import functools

import jax
import jax.numpy as jnp
from jax import lax
from jax.experimental import pallas as pl
from jax.experimental.pallas import tpu as pltpu

F32 = jnp.float32
BF16 = jnp.bfloat16
HI = lax.Precision.HIGHEST
MESH = pl.DeviceIdType.MESH

EPS = 1e-6
N_MOD = 6
DN_HEAD_DIM = 128
DN_CONV = 4
DN_CHUNK = 64
DN_GROUP = 8
DN_GATE_ROWS = 256
SB_HEAD_DIM = 64
SB_BLOCK = 128
SB_QBLOCK = 512
SB_PART = 128
LANES = 128
VMEM_LIMIT = 56 * 1024 * 1024
MM_BLOCK_BUDGET = 36 * 1024 * 1024
ROW_BLOCK_BUDGET = 20 * 1024 * 1024

ADAM_LR = 0.001
ADAM_B1 = 0.9
ADAM_B2 = 0.999
ADAM_EPS = 1e-08
ADAM_WD = 0.01
ADAM_STEP = 10


def _cparams(sem=None):
    return pltpu.CompilerParams(dimension_semantics=sem, vmem_limit_bytes=VMEM_LIMIT)


def _sigmoid(x):
    return 1.0 / (1.0 + jnp.exp(-x))


def _silu(x):
    return x * _sigmoid(x)


def _dsilu(x):
    s = _sigmoid(x)
    return s * (1.0 + x * (1.0 - s))


def _softplus(x):
    return jnp.maximum(x, 0.0) + jnp.log1p(jnp.exp(-jnp.abs(x)))


def _pick_tile(n, prefs):
    for t in prefs:
        if n % t == 0:
            return t
    return n


def _mm(a, b, *, ta=False, tb=False, out_dtype=F32, name):
    m = a.shape[1] if ta else a.shape[0]
    k = a.shape[0] if ta else a.shape[1]
    n = b.shape[0] if tb else b.shape[1]
    assert (b.shape[1] if tb else b.shape[0]) == k
    size = lambda dt: jnp.dtype(dt).itemsize
    best = None
    for tm in sorted({t for t in (m, 2048, 1024, 512, 256, 128) if m % t == 0 and (t % 128 == 0 or t == m)}):
        for tn in sorted({t for t in (n, 2816, 1408, 1024, 768, 512, 256, 128) if n % t == 0 and (t % 128 == 0 or t == n)}):
            need = 2 * (tm * k * size(a.dtype) + tn * k * size(b.dtype) + tm * tn * size(out_dtype))
            if need <= MM_BLOCK_BUDGET and (best is None or tm * tn > best[0] * best[1]):
                best = (tm, tn)
    tm, tn = best
    dims = (((0 if ta else 1,), (1 if tb else 0,)), ((), ()))

    def body(a_ref, b_ref, o_ref):
        av = a_ref[...].astype(BF16)
        bv = b_ref[...].astype(BF16)
        o_ref[...] = lax.dot_general(av, bv, dims, preferred_element_type=F32).astype(o_ref.dtype)

    a_spec = pl.BlockSpec((k, tm), lambda i, j: (0, i)) if ta else pl.BlockSpec((tm, k), lambda i, j: (i, 0))
    b_spec = pl.BlockSpec((tn, k), lambda i, j: (j, 0)) if tb else pl.BlockSpec((k, tn), lambda i, j: (0, j))
    return pl.pallas_call(
        body, grid=(m // tm, n // tn), in_specs=[a_spec, b_spec],
        out_specs=pl.BlockSpec((tm, tn), lambda i, j: (i, j)),
        out_shape=jax.ShapeDtypeStruct((m, n), out_dtype), name=name,
        compiler_params=_cparams(("parallel", "parallel")))(a, b)


def _rowwise(body, rows, vecs, outs, accs, *, name, tr=None):
    rows = [r if isinstance(r, tuple) else (r, r.shape[1], 0) for r in rows]
    nrows = rows[0][0].shape[0]
    if tr is None:
        per_row = sum(w * jnp.dtype(a.dtype).itemsize for a, w, _ in rows) + sum(
            w * jnp.dtype(dt).itemsize for w, dt in outs)
        tr = next((t for t in (2048, 1024, 512, 256) if nrows % t == 0 and 2 * t * per_row <= ROW_BLOCK_BUDGET), 256)
    tr = tr if nrows % tr == 0 else nrows
    nr, nv, no = len(rows), len(vecs), len(outs)

    def kern(*refs):
        r_in, v_in = refs[:nr], refs[nr:nr + nv]
        o_refs, a_refs = refs[nr + nv:nr + nv + no], refs[nr + nv + no:]
        res_o, res_a = body([r[...] for r in r_in], [v[...] for v in v_in])
        for o, val in zip(o_refs, res_o):
            o[...] = val.astype(o.dtype)
        if a_refs:
            @pl.when(pl.program_id(0) == 0)
            def _():
                for a in a_refs:
                    a[...] = jnp.zeros(a.shape, a.dtype)
            for a, val in zip(a_refs, res_a):
                a[...] += val

    in_specs = [pl.BlockSpec((tr, w), functools.partial(lambda i, cb: (i, cb), cb=cb)) for _, w, cb in rows]
    in_specs += [pl.BlockSpec(v.shape, functools.partial(lambda i, nd: (0,) * nd, nd=v.ndim)) for v in vecs]
    out_specs = [pl.BlockSpec((tr, w), lambda i: (i, 0)) for w, _ in outs]
    out_specs += [pl.BlockSpec(s, lambda i: (0, 0)) for s in accs]
    out_shape = [jax.ShapeDtypeStruct((nrows, w), dt) for w, dt in outs]
    out_shape += [jax.ShapeDtypeStruct(s, F32) for s in accs]
    res = pl.pallas_call(
        kern, grid=(nrows // tr,), in_specs=in_specs, out_specs=out_specs, out_shape=out_shape, name=name,
        compiler_params=_cparams(("arbitrary",) if accs else ("parallel",)))(*[r[0] for r in rows], *vecs)
    return res


def _colsum(v):
    return jnp.sum(v, axis=0, keepdims=True)


def _adaln_fwd(x, g, sc, sh, name):
    def body(r, v):
        (xv,), (gv, scv, shv) = r, v
        rs = lax.rsqrt(jnp.mean(xv * xv, axis=-1, keepdims=True) + EPS)
        return [(xv * rs) * gv * (1.0 + scv) + shv], []
    return _rowwise(body, [x], [g, sc, sh], [(x.shape[1], BF16)], [], name=name)[0]


def _adaln_bwd(x, dh, dxr, g, sc, name):
    d = x.shape[1]

    def body(r, v):
        (xv, dhv, dxv), (gv, scv) = r, v
        rs = lax.rsqrt(jnp.mean(xv * xv, axis=-1, keepdims=True) + EPS)
        nv = xv * rs
        dn = dhv * (gv * (1.0 + scv))
        dx = rs * (dn - nv * jnp.mean(dn * nv, axis=-1, keepdims=True)) + dxv
        dhn = dhv * nv
        return [dx], [_colsum(dhn * (1.0 + scv)), _colsum(dhn * gv), _colsum(dhv)]
    return _rowwise(body, [x, dh, dxr], [g, sc], [(d, F32)], [(1, d)] * 3, name=name)


def _resid_fwd(x, y, gt, name):
    def body(r, v):
        return [r[0] + v[0] * r[1]], []
    return _rowwise(body, [x, y], [gt], [(x.shape[1], F32)], [], name=name)[0]


def _resid_bwd(dx, y, gt, name):
    d = dx.shape[1]

    def body(r, v):
        return [v[0] * r[0]], [_colsum(r[0] * r[1])]
    return _rowwise(body, [dx, y], [gt], [(d, BF16)], [(1, d)], name=name)


def _swiglu_fwd(u, name):
    f = u.shape[1] // 2

    def body(r, v):
        uv = r[0]
        return [_silu(uv[:, :f]) * uv[:, f:]], []
    return _rowwise(body, [u], [], [(f, BF16)], [], name=name)[0]


def _swiglu_bwd(u, da, name):
    f = u.shape[1] // 2

    def body(r, v):
        uv, dav = r
        gate, up = uv[:, :f], uv[:, f:]
        return [jnp.concatenate([dav * up * _dsilu(gate), dav * _silu(gate)], axis=1)], []
    return _rowwise(body, [u, da], [], [(2 * f, BF16)], [], name=name)[0]


def _loss_fwd_bwd(y, tgt, name):
    d = y.shape[1]

    def body(r, v):
        err = r[0] - r[1]
        part = jnp.sum(jnp.sum(err * err, axis=1, keepdims=True), axis=0, keepdims=True) * (0.5 / d)
        return [err * (1.0 / d)], [jnp.broadcast_to(part, (1, LANES))]
    dy, acc = _rowwise(body, [y, tgt], [], [(d, F32)], [(1, LANES)], name=name)
    return acc[0, 0], dy


def _head_means(v):
    row = lax.broadcasted_iota(jnp.int32, (LANES, LANES), 0)
    col = lax.broadcasted_iota(jnp.int32, (LANES, LANES), 1)
    same = ((row // SB_HEAD_DIM) == (col // SB_HEAD_DIM)).astype(F32)
    parts = [jnp.dot(v[:, g * LANES:(g + 1) * LANES], same, precision=HI, preferred_element_type=F32)
             for g in range(v.shape[1] // LANES)]
    return jnp.concatenate(parts, axis=1) * (1.0 / SB_HEAD_DIM)


def _sb_norm_fwd(qkv, gq, gk, name):
    d = qkv.shape[1] // 3

    def body(r, v):
        return [x * lax.rsqrt(_head_means(x * x) + EPS) * g for x, g in zip(r, v)], []
    return _rowwise(body, [(qkv, d, 0), (qkv, d, 1)], [gq, gk], [(d, F32), (d, F32)], [], name=name)


def _sb_norm_bwd(qkv, dqn, dkn, dv, gq, gk, name):
    d = qkv.shape[1] // 3

    def body(r, v):
        outs, accs = [], []
        for x, dy, g in ((r[0], r[2], v[0]), (r[1], r[3], v[1])):
            rs = lax.rsqrt(_head_means(x * x) + EPS)
            nv = x * rs
            dn = dy * g
            outs.append(rs * (dn - nv * _head_means(dn * nv)))
            accs.append(_colsum(dy * nv))
        return [jnp.concatenate(outs + [r[4]], axis=1)], accs
    return _rowwise(body, [(qkv, d, 0), (qkv, d, 1), dqn, dkn, dv], [gq, gk], [(3 * d, BF16)], [(1, d), (1, d)],
                    name=name)


def _head_tiles(a):
    return [a[:, h * DN_HEAD_DIM:(h + 1) * DN_HEAD_DIM] for h in range(a.shape[1] // DN_HEAD_DIM)]


def _dn_post_fwd(o, proj, g, name):
    d = o.shape[1]

    def body(r, v):
        outs = []
        for ov, zv in zip(_head_tiles(r[0]), _head_tiles(r[1])):
            rs = lax.rsqrt(jnp.mean(ov * ov, axis=-1, keepdims=True) + EPS)
            outs.append(ov * rs * v[0] * _silu(zv))
        return [jnp.concatenate(outs, axis=1)], []
    return _rowwise(body, [o, (proj, d, 3)], [g], [(d, BF16)], [], name=name)[0]


def _dn_post_bwd(o, proj, don, g, name):
    d = o.shape[1]

    def body(r, v):
        dos, dzs, dg = [], [], jnp.zeros((1, DN_HEAD_DIM), F32)
        for ov, zv, dv in zip(_head_tiles(r[0]), _head_tiles(r[1]), _head_tiles(r[2])):
            rs = lax.rsqrt(jnp.mean(ov * ov, axis=-1, keepdims=True) + EPS)
            nv = ov * rs
            s = _silu(zv)
            dn = dv * v[0] * s
            dos.append(rs * (dn - nv * jnp.mean(dn * nv, axis=-1, keepdims=True)))
            dzs.append(dv * nv * v[0] * _dsilu(zv))
            dg = dg + _colsum(dv * nv * s)
        return [jnp.concatenate(dos, axis=1), jnp.concatenate(dzs, axis=1)], [dg]
    return _rowwise(body, [o, (proj, d, 3), don], [g], [(d, F32), (d, BF16)], [(1, DN_HEAD_DIM)], name=name)


def _chunk_tri(tr, upper):
    row = lax.broadcasted_iota(jnp.int32, (tr, tr), 0)
    col = lax.broadcasted_iota(jnp.int32, (tr, tr), 1)
    same = (row // DN_CHUNK) == (col // DN_CHUNK)
    return (same & ((row <= col) if upper else (row >= col))).astype(F32)


def _dn_gate_fwd(proj, colblk, a_log, dt_bias, name):
    def body(r, v):
        ab = r[0]
        a, b = ab[:, :LANES], ab[:, LANES:]
        g = -jnp.exp(v[0]) * _softplus(a + v[1])
        big_g = jnp.dot(_chunk_tri(g.shape[0], False), g, precision=HI, preferred_element_type=F32)
        return [big_g, _sigmoid(b)], []
    return _rowwise(body, [(proj, 2 * LANES, colblk)], [a_log, dt_bias], [(LANES, F32), (LANES, F32)], [], name=name,
                    tr=DN_GATE_ROWS)


def _dn_gate_bwd(proj, colblk, d_big_g, dbeta, a_log, dt_bias, name):
    def body(r, v):
        ab, dgc, dbt = r
        a, b = ab[:, :LANES], ab[:, LANES:]
        dg = jnp.dot(_chunk_tri(dgc.shape[0], True), dgc, precision=HI, preferred_element_type=F32)
        na = -jnp.exp(v[0])
        pre = a + v[1]
        da = dg * na * _sigmoid(pre)
        beta = _sigmoid(b)
        db = dbt * beta * (1.0 - beta)
        return [jnp.concatenate([da, db], axis=1)], [_colsum(dg * na * _softplus(pre)), _colsum(da)]
    return _rowwise(body, [(proj, 2 * LANES, colblk), d_big_g, dbeta], [a_log, dt_bias],
                    [(2 * LANES, BF16)], [(1, LANES), (1, LANES)], name=name, tr=DN_GATE_ROWS)


def _shift_rows(x, s):
    if s == 0:
        return x
    t = x.shape[0]
    row = lax.broadcasted_iota(jnp.int32, x.shape, 0)
    rolled = pltpu.roll(x, s % t, axis=0)
    return jnp.where((row >= s) if s > 0 else (row < t + s), rolled, 0.0)


def _rnd(x):
    return x.astype(BF16).astype(F32)


def _conv(x, c_ref):
    c = x * _rnd(c_ref[DN_CONV - 1:DN_CONV, :])
    for s in range(1, DN_CONV):
        c = c + _shift_rows(x, s) * _rnd(c_ref[DN_CONV - 1 - s:DN_CONV - s, :])
    return c


def _dn_prep_fwd(proj, conv_w, heads, name):
    t = proj.shape[0]
    d = heads * DN_HEAD_DIM

    def body(xq, xk, xv, cq, ck, cv, q_ref, k_ref, v_ref):
        for x_ref, c_ref, o_ref, norm in ((xq, cq, q_ref, True), (xk, ck, k_ref, True), (xv, cv, v_ref, False)):
            y = _silu(_conv(_rnd(x_ref[...]), c_ref))
            if norm:
                y = y * lax.rsqrt(jnp.sum(y * y, axis=-1, keepdims=True) + EPS)
            o_ref[...] = y

    xs = [pl.BlockSpec((t, DN_HEAD_DIM), functools.partial(lambda h, o: (0, h + o), o=o * heads)) for o in range(3)]
    cs = [pl.BlockSpec((DN_CONV, DN_HEAD_DIM), functools.partial(lambda h, o: (0, h + o), o=o * heads)) for o in range(3)]
    return pl.pallas_call(
        body, grid=(heads,), in_specs=xs + cs,
        out_specs=[pl.BlockSpec((t, DN_HEAD_DIM), lambda h: (0, h))] * 3,
        out_shape=[jax.ShapeDtypeStruct((t, d), F32)] * 3, name=name,
        compiler_params=_cparams(("parallel",)))(proj, proj, proj, conv_w, conv_w, conv_w)


def _dn_prep_bwd(proj, conv_w, dq, dk, dv, heads, name):
    t = proj.shape[0]
    d = heads * DN_HEAD_DIM

    def body(xq, xk, xv, cq, ck, cv, gq, gk, gv, oq, ok, ov, wq, wk, wv):
        for x_ref, c_ref, g_ref, o_ref, w_ref, norm in ((xq, cq, gq, oq, wq, True), (xk, ck, gk, ok, wk, True),
                                                        (xv, cv, gv, ov, wv, False)):
            x, dy = _rnd(x_ref[...]), g_ref[...]
            c = _conv(x, c_ref)
            if norm:
                y = _silu(c)
                rs = lax.rsqrt(jnp.sum(y * y, axis=-1, keepdims=True) + EPS)
                yn = y * rs
                dy = rs * (dy - yn * jnp.sum(dy * yn, axis=-1, keepdims=True))
            dc = _rnd(dy * _dsilu(c))
            dx = dc * _rnd(c_ref[DN_CONV - 1:DN_CONV, :])
            w_ref[DN_CONV - 1:DN_CONV, :] = _colsum(dc * x)
            for s in range(1, DN_CONV):
                dx = dx + _shift_rows(dc, -s) * _rnd(c_ref[DN_CONV - 1 - s:DN_CONV - s, :])
                w_ref[DN_CONV - 1 - s:DN_CONV - s, :] = _colsum(dc * _shift_rows(x, s))
            o_ref[...] = dx.astype(o_ref.dtype)

    xs = [pl.BlockSpec((t, DN_HEAD_DIM), functools.partial(lambda h, o: (0, h + o), o=o * heads)) for o in range(3)]
    cs = [pl.BlockSpec((DN_CONV, DN_HEAD_DIM), functools.partial(lambda h, o: (0, h + o), o=o * heads)) for o in range(3)]
    hs = pl.BlockSpec((t, DN_HEAD_DIM), lambda h: (0, h))
    ws = pl.BlockSpec((DN_CONV, DN_HEAD_DIM), lambda h: (0, h))
    return pl.pallas_call(
        body, grid=(heads,), in_specs=xs + cs + [hs] * 3, out_specs=[hs] * 3 + [ws] * 3,
        out_shape=[jax.ShapeDtypeStruct((t, d), BF16)] * 3 + [jax.ShapeDtypeStruct((DN_CONV, d), F32)] * 3, name=name,
        compiler_params=_cparams(("parallel",)))(proj, proj, proj, conv_w, conv_w, conv_w, dq, dk, dv)


X3 = "three bf16 passes"


def _bdot(a, b, dims, prec):
    if prec != X3:
        return lax.dot_general(a, b, dims, precision=prec, preferred_element_type=F32)
    a1, b1 = a.astype(BF16), b.astype(BF16)
    a2, b2 = (a - a1.astype(F32)).astype(BF16), (b - b1.astype(F32)).astype(BF16)
    dot = lambda x, y: lax.dot_general(x, y, dims, preferred_element_type=F32)
    return dot(a1, b1) + (dot(a1, b2) + dot(a2, b1))


def _bmm(a, b, prec=None):
    return _bdot(a, b, (((2,), (1,)), ((0,), (0,))), prec)


def _bmm_nt(a, b, prec=None):
    return _bdot(a, b, (((2,), (2,)), ((0,), (0,))), prec)


def _bmm_tn(a, b, prec=None):
    return _bdot(a, b, (((1,), (1,)), ((0,), (0,))), prec)


def _dn_local(qg, kg, vg, gc, gr, bt):
    c = qg.shape[1]
    row = lax.broadcasted_iota(jnp.int32, (c, c), 0)
    col = lax.broadcasted_iota(jnp.int32, (c, c), 1)
    incl, strict = (row >= col)[None], (row > col)[None]
    decay = jnp.where(incl, jnp.exp(jnp.where(incl, gc - gr, 0.0)), 0.0)
    kb = kg * bt
    vb = vg * bt
    m = _bmm_nt(kb.astype(BF16), kg.astype(BF16))
    a = jnp.where(strict, m * decay, 0.0)
    bp = -a
    tm = jnp.where((row == col)[None], 1.0, 0.0) + bp
    steps = max(1, (c - 1).bit_length()) - 1
    for _ in range(steps):
        bp = _bmm(bp, bp, X3)
        tm = tm + _bmm(tm, bp, X3)
    eg = jnp.exp(gc)
    glast = gc[:, c - 1:c, :]
    erel = jnp.exp(glast - gc)
    kbg = kb * eg
    qm = _bmm_nt(qg.astype(BF16), kg.astype(BF16))
    return dict(decay=decay, strict=strict, kb=kb, vb=vb, m=m, tm=tm, eg=eg, erel=erel, kbg=kbg, qm=qm,
                u=_bmm(tm, vb, X3), w=_bmm(tm, kbg, X3), qd=qg * eg, kt=kg * erel, gl=jnp.exp(glast))


def _dot(a, b):
    return jnp.dot(a.astype(BF16), b.astype(BF16), preferred_element_type=F32)


def _dot_nt(a, b):
    return lax.dot_general(a.astype(BF16), b.astype(BF16), (((1,), (1,)), ((), ())), preferred_element_type=F32)


def _dot_tn(a, b):
    return lax.dot_general(a.astype(BF16), b.astype(BF16), (((0,), (0,)), ((), ())), preferred_element_type=F32)


def _dn_chunk_specs(t, heads):
    n = t // DN_CHUNK
    hs = pl.BlockSpec((t, DN_HEAD_DIM), lambda h: (0, h))
    gcs = pl.BlockSpec((1, n, DN_CHUNK, 1), lambda h: (h, 0, 0, 0))
    grs = pl.BlockSpec((1, n, 1, DN_CHUNK), lambda h: (h, 0, 0, 0))
    ss = pl.BlockSpec((1, n, DN_HEAD_DIM, DN_HEAD_DIM), lambda h: (h, 0, 0, 0))
    return n, hs, gcs, grs, ss


def _dn_chunk_fwd(q, k, v, gc, gr, bc, name):
    t, d = q.shape
    heads = d // DN_HEAD_DIM
    c, dk = DN_CHUNK, DN_HEAD_DIM
    n, hs, gcs, grs, ss = _dn_chunk_specs(t, heads)
    nb = min(DN_GROUP, n)
    scale = dk ** -0.5

    def body(q_ref, k_ref, v_ref, gc_ref, gr_ref, b_ref, o_ref, s_ref, u_s, w_s, at_s, qd_s, kt_s, gl_s):
        def group(gi, carry):
            r0 = pl.multiple_of(gi * (nb * c), nb * c)
            n0 = gi * nb
            ld = lambda ref: ref[pl.ds(r0, nb * c), :].reshape(nb, c, dk)
            loc = _dn_local(ld(q_ref) * scale, ld(k_ref), ld(v_ref), gc_ref[0, pl.ds(n0, nb)],
                            gr_ref[0, pl.ds(n0, nb)], b_ref[0, pl.ds(n0, nb)])
            u_s[pl.ds(n0, nb)] = loc["u"]
            w_s[pl.ds(n0, nb)] = loc["w"]
            at_s[pl.ds(n0, nb)] = loc["qm"] * loc["decay"]
            qd_s[pl.ds(n0, nb)] = loc["qd"]
            kt_s[pl.ds(n0, nb)] = loc["kt"]
            gl_s[pl.ds(n0, nb)] = loc["gl"]
            return carry
        lax.fori_loop(0, n // nb, group, 0)

        def chunk(i, s):
            s_ref[0, i] = s
            vnew = u_s[i] - _dot(w_s[i], s)
            o = _dot(qd_s[i], s) + _dot(at_s[i], vnew)
            o_ref[pl.ds(pl.multiple_of(i * c, c), c), :] = o
            return s * gl_s[i] + _dot_tn(kt_s[i], vnew)
        lax.fori_loop(0, n, chunk, jnp.zeros((dk, dk), F32))

    scratch = [pltpu.VMEM((n, c, dk), F32), pltpu.VMEM((n, c, dk), F32), pltpu.VMEM((n, c, c), F32),
               pltpu.VMEM((n, c, dk), F32), pltpu.VMEM((n, c, dk), F32), pltpu.VMEM((n, 1, 1), F32)]
    return pl.pallas_call(
        body, grid=(heads,), in_specs=[hs, hs, hs, gcs, grs, gcs], out_specs=[hs, ss],
        out_shape=[jax.ShapeDtypeStruct((t, d), F32), jax.ShapeDtypeStruct((heads, n, dk, dk), F32)],
        scratch_shapes=scratch, name=name, compiler_params=_cparams(("parallel",)))(q, k, v, gc, gr, bc)


def _dn_chunk_bwd(q, k, v, gc, gr, bc, s_all, do, name):
    t, d = q.shape
    heads = d // DN_HEAD_DIM
    c, dk = DN_CHUNK, DN_HEAD_DIM
    n, hs, gcs, grs, ss = _dn_chunk_specs(t, heads)
    nb = min(DN_GROUP, n)
    scale = dk ** -0.5

    def body(q_ref, k_ref, v_ref, gc_ref, gr_ref, b_ref, s_ref, do_ref,
             dq_ref, dk_ref, dv_ref, dgc_ref, dgr_ref, db_ref,
             u_s, w_s, att_s, qd_s, kt_s, gl_s, du_s, dw_s, dat_s, dqd_s, dkt_s, dgl_s):
        def load_group(gi):
            r0 = pl.multiple_of(gi * (nb * c), nb * c)
            n0 = gi * nb
            ld = lambda ref: ref[pl.ds(r0, nb * c), :].reshape(nb, c, dk)
            qg, kg, vg = ld(q_ref) * scale, ld(k_ref), ld(v_ref)
            gcv, grv, bt = gc_ref[0, pl.ds(n0, nb)], gr_ref[0, pl.ds(n0, nb)], b_ref[0, pl.ds(n0, nb)]
            return r0, n0, qg, kg, vg, gcv, grv, bt, _dn_local(qg, kg, vg, gcv, grv, bt)

        def group_a(gi, carry):
            _, n0, qg, kg, _, gcv, grv, _, loc = load_group(gi)
            row = lax.broadcasted_iota(jnp.int32, (c, c), 0)
            col = lax.broadcasted_iota(jnp.int32, (c, c), 1)
            upper = (col >= row)[None]
            decay_t = jnp.where(upper, jnp.exp(jnp.where(upper, grv - gcv, 0.0)), 0.0)
            u_s[pl.ds(n0, nb)] = loc["u"]
            w_s[pl.ds(n0, nb)] = loc["w"]
            att_s[pl.ds(n0, nb)] = _bmm_nt(kg.astype(BF16), qg.astype(BF16)) * decay_t
            qd_s[pl.ds(n0, nb)] = loc["qd"]
            kt_s[pl.ds(n0, nb)] = loc["kt"]
            gl_s[pl.ds(n0, nb)] = loc["gl"]
            return carry
        lax.fori_loop(0, n // nb, group_a, 0)

        def chunk_b(it, ds_next):
            i = n - 1 - it
            s = s_ref[0, i]
            dov = do_ref[pl.ds(pl.multiple_of(i * c, c), c), :]
            w, kt = w_s[i], kt_s[i]
            vnew = u_s[i] - _dot(w, s)
            dvnew = _dot(att_s[i], dov) + _dot(kt, ds_next)
            du_s[i] = dvnew
            dw_s[i] = -_dot_nt(dvnew, s)
            dat_s[i] = _dot_nt(dov, vnew)
            dqd_s[i] = _dot_nt(dov, s)
            dkt_s[i] = _dot_nt(vnew, ds_next)
            dgl_s[i] = jnp.sum(jnp.sum(ds_next * s, axis=1, keepdims=True), axis=0, keepdims=True)
            return ds_next * gl_s[i] + _dot_tn(qd_s[i], dov) - _dot_tn(w, dvnew)
        lax.fori_loop(0, n, chunk_b, jnp.zeros((dk, dk), F32))

        def group_c(gi, carry):
            r0, n0, qg, kg, vg, gcv, grv, bt, loc = load_group(gi)
            sl = pl.ds(n0, nb)
            du, dw, dat, dqd, dkt, dgl = du_s[sl], dw_s[sl], dat_s[sl], dqd_s[sl], dkt_s[sl], dgl_s[sl]
            tm, decay, kb, kbg = loc["tm"], loc["decay"], loc["kb"], loc["kbg"]
            dvb = _bmm_tn(tm, du, X3)
            dkbg = _bmm_tn(tm, dw, X3)
            dt = _bmm_nt(du, loc["vb"], X3) + _bmm_nt(dw, kbg, X3)
            da = jnp.where(loc["strict"], -_bmm_tn(tm, _bmm_nt(dt, tm, X3), X3), 0.0)
            dms = (da * decay).astype(BF16)
            dqs = (dat * decay).astype(BF16)
            kgb = kg.astype(BF16)
            dkb = _bmm(dms, kgb) + dkbg * loc["eg"]
            dqt = _bmm(dqs, kgb) + dqd * loc["eg"]
            dkk = _bmm_tn(dms, kb.astype(BF16)) + _bmm_tn(dqs, qg.astype(BF16)) + dkt * loc["erel"] + dkb * bt
            e = (da * loc["m"] + dat * loc["qm"]) * decay
            lsum = lambda x: jnp.sum(x, axis=2, keepdims=True)
            dkt_kt = lsum(dkt * loc["kt"])
            dgcv = lsum(e) + lsum(dqd * loc["qd"]) - dkt_kt + lsum(dkbg * kbg)
            dglast = jnp.sum(dkt_kt, axis=1, keepdims=True) + dgl * loc["gl"]
            rowc = lax.broadcasted_iota(jnp.int32, (1, c, 1), 1)
            dgc_ref[0, sl] = dgcv + jnp.where(rowc == c - 1, dglast, 0.0)
            dgr_ref[0, sl] = -jnp.sum(e, axis=1, keepdims=True)
            db_ref[0, sl] = lsum(dkb * kg) + lsum(dvb * vg)
            rows = pl.ds(r0, nb * c)
            dq_ref[rows, :] = (dqt * scale).reshape(nb * c, dk)
            dk_ref[rows, :] = dkk.reshape(nb * c, dk)
            dv_ref[rows, :] = (dvb * bt).reshape(nb * c, dk)
            return carry
        lax.fori_loop(0, n // nb, group_c, 0)

    big = lambda: pltpu.VMEM((n, c, dk), F32)
    sq = lambda: pltpu.VMEM((n, c, c), F32)
    one = lambda: pltpu.VMEM((n, 1, 1), F32)
    scratch = [big(), big(), sq(), big(), big(), one(), big(), big(), sq(), big(), big(), one()]
    return pl.pallas_call(
        body, grid=(heads,), in_specs=[hs, hs, hs, gcs, grs, gcs, ss, hs], out_specs=[hs, hs, hs, gcs, grs, gcs],
        out_shape=[jax.ShapeDtypeStruct((t, d), F32)] * 3 + [
            jax.ShapeDtypeStruct((heads, n, c, 1), F32), jax.ShapeDtypeStruct((heads, n, 1, c), F32),
            jax.ShapeDtypeStruct((heads, n, c, 1), F32)],
        scratch_shapes=scratch, name=name, compiler_params=_cparams(("parallel",)))(q, k, v, gc, gr, bc, s_all, do)


def _dot01(x, m01):
    x1 = x.astype(BF16)
    r1 = x - x1.astype(F32)
    x2 = r1.astype(BF16)
    x3 = (r1 - x2.astype(F32)).astype(BF16)
    dot = lambda a: jnp.dot(a, m01, preferred_element_type=F32)
    return (dot(x1) + dot(x2)) + dot(x3)


def _sb_logs(z, mask):
    rows, bk = z.shape
    row = lax.broadcasted_iota(jnp.int32, (bk, bk), 0)
    col = lax.broadcasted_iota(jnp.int32, (bk, bk), 1)
    later = (row > col).astype(BF16)
    lm, ls, cs = [], [], []
    for p in [slice(p, p + SB_PART) for p in range(0, rows, SB_PART)]:
        lsm = -_softplus(z[p])
        lm.append(lsm if mask is None else jnp.where(mask[p], lsm, 0.0))
        ls.append(z[p] + lsm)
        cs.append(_dot01(lm[-1], later))
    cat = lambda xs: jnp.concatenate(xs, axis=0)
    return cat(ls), cat(cs), jnp.sum(cat(lm), axis=1, keepdims=True)


def _sb_masks(bq, bk):
    row = lax.broadcasted_iota(jnp.int32, (2 * bq, bk), 0)
    row = jnp.where(row >= bq, row - bq, row)
    col = lax.broadcasted_iota(jnp.int32, (2 * bq, bk), 1)
    return [(col + dd * bk) < row for dd in range(bq // bk)]


def _sb_sweep(bq, bk, qi, fetch, visit, carry):
    nd = bq // bk
    masks = _sb_masks(bq, bk)
    ahead = lambda j: fetch(jnp.maximum(j - 1, 0))
    pre = fetch(qi * nd + nd - 1)
    for dd in reversed(range(nd)):
        nxt = ahead(qi * nd + dd)
        carry = visit(qi * nd + dd, masks[dd], carry, pre)
        pre = nxt

    def below(it, c):
        j = qi * nd - 1 - it
        nxt = ahead(j)
        return visit(j, None, c[0], c[1]), nxt
    return lax.fori_loop(0, qi * nd, below, (carry, pre))[0]


def _sb_sweep2(bq, bk, qi, fetch, first, second, carry):
    nd = bq // bk
    masks = _sb_masks(bq, bk)
    ahead = lambda j: fetch(jnp.maximum(j - 1, 0))
    pre = fetch(qi * nd + nd - 1)
    r = jnp.zeros((2 * bq, 1), F32)
    pend = None
    for dd in reversed(range(nd)):
        j = qi * nd + dd
        nxt = ahead(j)
        ls, cs, rsum, extras = first(j, masks[dd], pre)
        if pend is not None:
            carry = second(pend[0], carry, *pend[1])
        pend = (j, (jnp.where(masks[dd], ls + cs + r, -1e30),) + extras)
        r, pre = r + rsum, nxt

    def below(it, c):
        carry, r, pre, pj, pargs = c
        j = qi * nd - 1 - it
        nxt = ahead(j)
        ls, cs, rsum, extras = first(j, None, pre)
        carry = second(pj, carry, *pargs)
        return carry, r + rsum, nxt, j, (ls + cs + r,) + extras
    carry, _, _, pj, pargs = lax.fori_loop(0, qi * nd, below, (carry, r, pre) + pend)
    return second(pj, carry, *pargs)


def _stack_heads(a, h0):
    return jnp.concatenate([jnp.where(h0, a, 0.0), jnp.where(h0, 0.0, a)], axis=0).astype(BF16)


def _side_by_side(a, bq):
    return jnp.concatenate([a[:bq], a[bq:]], axis=1)


def _sb_specs(t, d, bq, vcol):
    qs = pl.BlockSpec((bq, LANES), lambda g, i: (i, g))
    ks = pl.BlockSpec((t, LANES), lambda g, i: (0, g))
    vs = pl.BlockSpec((t, LANES), lambda g, i: (0, g + vcol))
    return qs, ks, vs


def _sb_fwd(qn, kn, qkv, name):
    t, d = qn.shape
    bq, bk = min(SB_QBLOCK, t), min(SB_BLOCK, t)
    scale = SB_HEAD_DIM ** -0.5
    nt = (((1,), (1,)), ((), ()))

    def body(q_ref, k_ref, v_ref, o_ref):
        qi = pl.program_id(1)
        h0 = lax.broadcasted_iota(jnp.int32, (1, LANES), 1) < SB_HEAD_DIM
        q2 = _stack_heads(q_ref[...] * scale, h0)

        tile_rows = lambda j: pl.ds(pl.multiple_of(j * bk, bk), bk)

        def fetch(j):
            return lax.dot_general(q2, k_ref[tile_rows(j), :].astype(BF16), nt, preferred_element_type=F32)

        def first(j, mask, z):
            return _sb_logs(z, mask) + ((),)

        def second(j, acc, loga):
            a2 = _side_by_side(jnp.exp(loga).astype(BF16), bq)
            return acc + jnp.dot(a2, _stack_heads(v_ref[tile_rows(j), :], h0), preferred_element_type=F32)
        acc = _sb_sweep2(bq, bk, qi, fetch, first, second, jnp.zeros((bq, LANES), F32))
        o_ref[...] = acc.astype(o_ref.dtype)

    qs, ks, vs = _sb_specs(t, d, bq, 2 * d // LANES)
    return pl.pallas_call(
        body, grid=(d // LANES, t // bq), in_specs=[qs, ks, vs], out_specs=qs,
        out_shape=jax.ShapeDtypeStruct((t, d), BF16), name=name,
        compiler_params=_cparams(("parallel", "parallel")))(qn, kn, qkv)


def _sb_bwd(qn, kn, qkv, do, name):
    t, d = qn.shape
    bq, bk = min(SB_QBLOCK, t), min(SB_BLOCK, t)
    scale = SB_HEAD_DIM ** -0.5
    nt = (((1,), (1,)), ((), ()))

    def body(q_ref, k_ref, v_ref, do_ref, dq_ref, dk_ref, dv_ref, p_s, sg_s):
        qi = pl.program_id(1)

        @pl.when(qi == 0)
        def _():
            dk_ref[...] = jnp.zeros(dk_ref.shape, F32)
            dv_ref[...] = jnp.zeros(dv_ref.shape, F32)

        h0 = lax.broadcasted_iota(jnp.int32, (1, LANES), 1) < SB_HEAD_DIM
        q2 = _stack_heads(q_ref[...] * scale, h0)
        do2 = _stack_heads(do_ref[...], h0)
        row = lax.broadcasted_iota(jnp.int32, (bk, bk), 0)
        col = lax.broadcasted_iota(jnp.int32, (bk, bk), 1)
        later_incl = (row >= col).astype(BF16)
        zero = jnp.zeros((2 * bq, 1), F32)
        tn = (((0,), (0,)), ((), ()))

        tile_rows = lambda j: pl.ds(pl.multiple_of(j * bk, bk), bk)

        def fetch1(j):
            return lax.dot_general(q2, k_ref[tile_rows(j), :].astype(BF16), nt, preferred_element_type=F32)

        def first1(j, mask, z):
            da = lax.dot_general(do2, v_ref[tile_rows(j), :].astype(BF16), nt, preferred_element_type=F32)
            ls, cs, rsum = _sb_logs(z, mask)
            sg_s[j] = jnp.exp(ls)
            return ls, cs, rsum, (da,)

        def second1(j, sp, loga, da):
            a = jnp.exp(loga)
            p = a * da
            p_s[j] = p
            dv_ref[tile_rows(j), :] += lax.dot_general(a.astype(BF16), do2, tn, preferred_element_type=F32)
            return sp + jnp.sum(p, axis=1, keepdims=True)
        total = _sb_sweep2(bq, bk, qi, fetch1, first1, second1, zero)

        def fetch2(j):
            return _dot01(p_s[j], later_incl)

        def visit2(j, mask, carry, later_p):
            dq, sp = carry
            rows = tile_rows(j)
            p, sg = p_s[j], sg_s[j]
            pref = total - sp - later_p
            dz = p * (1.0 - sg) - pref * sg
            if mask is not None:
                dz = jnp.where(mask, dz, 0.0)
            dz = dz.astype(BF16)
            dk_ref[rows, :] += lax.dot_general(dz, q2, tn, preferred_element_type=F32)
            dq = dq + jnp.dot(_side_by_side(dz, bq), _stack_heads(k_ref[rows, :], h0), preferred_element_type=F32)
            return dq, sp + jnp.sum(p, axis=1, keepdims=True)
        dq, _ = _sb_sweep(bq, bk, qi, fetch2, visit2, (jnp.zeros((bq, LANES), F32), zero))
        dq_ref[...] = dq * scale

    qs, ks, vs = _sb_specs(t, d, bq, 2 * d // LANES)
    shp = jax.ShapeDtypeStruct((t, d), F32)
    scratch = [pltpu.VMEM((t // bk, 2 * bq, bk), F32), pltpu.VMEM((t // bk, 2 * bq, bk), F32)]
    return pl.pallas_call(
        body, grid=(d // LANES, t // bq), in_specs=[qs, ks, vs, qs], out_specs=[qs, ks, ks],
        out_shape=[shp, shp, shp], scratch_shapes=scratch, name=name,
        compiler_params=_cparams(("parallel", "arbitrary")))(qn, kn, qkv, do)


def _adamw(w, g, m, v, name):
    shape = w.shape
    two_d = lambda a: a.reshape(-1, shape[-1])
    c1 = 1.0 - ADAM_B1 ** ADAM_STEP
    c2 = 1.0 - ADAM_B2 ** ADAM_STEP

    def body(r, _):
        wv, gv, mv, vv = r
        mn = ADAM_B1 * mv + (1.0 - ADAM_B1) * gv
        vn = ADAM_B2 * vv + (1.0 - ADAM_B2) * (gv * gv)
        delta = -ADAM_LR * ((mn / c1) / (jnp.sqrt(vn / c2) + ADAM_EPS) + ADAM_WD * wv)
        return [delta, mn, vn], []
    width = shape[-1]
    res = _rowwise(body, [two_d(w), two_d(g), two_d(m), two_d(v)], [], [(width, F32)] * 3, [], name=name)
    return [r.reshape(shape) for r in res]


def _dn_layer_fwd(h1, w, conv_w, a_log, dt_bias, onorm_g, tag):
    t, d = h1.shape
    heads = d // DN_HEAD_DIM
    n = t // DN_CHUNK
    proj = _mm(h1, w["all"], name=tag + "_proj")
    qn, kn, vv = _dn_prep_fwd(proj, conv_w, heads, tag + "_prep")
    big_g, beta = _dn_gate_fwd(proj, 4 * d // (2 * LANES), a_log, dt_bias, tag + "_gate")
    gt_ = big_g[:, :heads].T.reshape(heads, n, DN_CHUNK)
    gc, gr = gt_[..., None], gt_[:, :, None, :]
    bc = beta[:, :heads].T.reshape(heads, n, DN_CHUNK)[..., None]
    o, s_all = _dn_chunk_fwd(qn, kn, vv, gc, gr, bc, tag + "_chunk")
    on = _dn_post_fwd(o, proj, onorm_g, tag + "_post")
    y = _mm(on, w["out"], name=tag + "_out")
    return y, dict(proj=proj, qn=qn, kn=kn, v=vv, gc=gc, gr=gr, bc=bc, o=o, s_all=s_all, on=on)


def _dn_layer_bwd(dy, h1, w, conv_w, a_log, dt_bias, onorm_g, sv, tag):
    t, d = h1.shape
    heads = d // DN_HEAD_DIM
    don = _mm(dy, w["out"], tb=True, name=tag + "_dout")
    g_out = _mm(sv["on"], dy, ta=True, out_dtype=BF16, name=tag + "_gwout")
    do, dz, g_on = _dn_post_bwd(sv["o"], sv["proj"], don, onorm_g, tag + "_dpost")
    dq, dk, dv, dgc, dgr, dbc = _dn_chunk_bwd(sv["qn"], sv["kn"], sv["v"], sv["gc"], sv["gr"], sv["bc"], sv["s_all"],
                                              do, tag + "_dchunk")
    pad = lambda a: jnp.pad(a.reshape(heads, t).T, ((0, 0), (0, LANES - heads)))
    d_big_g = pad(dgc) + pad(dgr)
    dab, g_alog, g_dt = _dn_gate_bwd(sv["proj"], 4 * d // (2 * LANES), d_big_g, pad(dbc), a_log, dt_bias, tag + "_dgate")
    dxq, dxk, dxv, wq, wk, wv = _dn_prep_bwd(sv["proj"], conv_w, dq, dk, dv, heads, tag + "_dprep")
    dproj = jnp.concatenate([dxq, dxk, dxv, dz, dab], axis=1)
    dh1 = _mm(dproj, w["all"], tb=True, name=tag + "_dh")
    g_all = _mm(h1, dproj, ta=True, out_dtype=BF16, name=tag + "_gwin")
    grads = dict(w_all=g_all, w_out=g_out, conv_w=jnp.concatenate([wq, wk, wv], axis=1), a_log=g_alog, dt_bias=g_dt,
                 onorm_g=g_on)
    return dh1, grads


def _sb_layer_fwd(h1, w, q_g, k_g, tag):
    t, d = h1.shape
    heads = d // SB_HEAD_DIM
    qkv = _mm(h1, w["qkv"], name=tag + "_proj")
    gq, gk = jnp.tile(q_g, (1, heads)), jnp.tile(k_g, (1, heads))
    qn, kn = _sb_norm_fwd(qkv, gq, gk, tag + "_norm")
    o = _sb_fwd(qn, kn, qkv, tag + "_attn")
    y = _mm(o, w["out"], name=tag + "_out")
    return y, dict(qkv=qkv, qn=qn, kn=kn, o=o, gq=gq, gk=gk)


def _sb_layer_bwd(dy, h1, w, q_g, k_g, sv, tag):
    t, d = h1.shape
    heads = d // SB_HEAD_DIM
    do = _mm(dy, w["out"], tb=True, name=tag + "_dout")
    g_out = _mm(sv["o"], dy, ta=True, out_dtype=BF16, name=tag + "_gwout")
    dqn, dkn, dv = _sb_bwd(sv["qn"], sv["kn"], sv["qkv"], do, tag + "_dattn")
    dqkv, g_q, g_k = _sb_norm_bwd(sv["qkv"], dqn, dkn, dv, sv["gq"], sv["gk"], tag + "_dnorm")
    fold = lambda g: jnp.sum(g.reshape(heads, SB_HEAD_DIM), axis=0, keepdims=True)
    dh1 = _mm(dqkv, w["qkv"], tb=True, name=tag + "_dh")
    g_qkv = _mm(h1, dqkv, ta=True, out_dtype=BF16, name=tag + "_gwin")
    return dh1, dict(w_qkv=g_qkv, w_out=g_out, q_norm_g=fold(g_q), k_norm_g=fold(g_k))


def _local_step(x, tgt, mod, norm1_g, norm2_g, layer_weights, layer_grads):
    depth = mod.shape[0]
    d = x.shape[1]
    saved = []
    for i in range(depth):
        mix_w, ffn_of = layer_weights(i, x)
        mv = [mod[i:i + 1, j * d:(j + 1) * d] for j in range(N_MOD)]
        sh1, sc1, gt1, sh2, sc2, gt2 = mv
        tag = "l%d" % i
        h1 = _adaln_fwd(x, norm1_g[i:i + 1], sc1, sh1, tag + "_ln1")
        if i % 2 == 0:
            p = mix_w
            y, sv = _dn_layer_fwd(h1, p, p["conv_w"], p["a_log"], p["dt_bias"], p["onorm_g"], tag + "_dn")
        else:
            p = mix_w
            y, sv = _sb_layer_fwd(h1, p, p["q_g"], p["k_g"], tag + "_sb")
        x1 = _resid_fwd(x, y, gt1, tag + "_res1")
        h2 = _adaln_fwd(x1, norm2_g[i:i + 1], sc2, sh2, tag + "_ln2")
        ffn_w = ffn_of(x1)
        u = _mm(h2, ffn_w["w_in"], name=tag + "_ffn_in")
        a = _swiglu_fwd(u, tag + "_swiglu")
        y2 = _mm(a, ffn_w["w_out"], name=tag + "_ffn_out")
        x2 = _resid_fwd(x1, y2, gt2, tag + "_res2")
        saved.append(dict(x0=x, h1=h1, y=y, mix=sv, x1=x1, h2=h2, u=u, a=a, y2=y2, mix_w=mix_w, ffn_w=ffn_w))
        x = x2

    loss, dx = _loss_fwd_bwd(x, tgt, "loss")

    dmod, dn1, dn2 = [None] * depth, [None] * depth, [None] * depth
    zero = jnp.zeros((), F32)
    for i in reversed(range(depth)):
        s = saved[i]
        mv = [mod[i:i + 1, j * d:(j + 1) * d] + zero for j in range(N_MOD)]
        sh1, sc1, gt1, sh2, sc2, gt2 = mv
        tag = "l%d" % i
        ffn_w, p = s["ffn_w"], s["mix_w"]
        dy2, dgt2 = _resid_bwd(dx, s["y2"], gt2, tag + "_dres2")
        da = _mm(dy2, ffn_w["w_out"], tb=True, name=tag + "_dffn_a")
        g_wout = _mm(s["a"], dy2, ta=True, out_dtype=BF16, name=tag + "_gffn_out")
        du = _swiglu_bwd(s["u"], da, tag + "_dswiglu")
        dh2 = _mm(du, ffn_w["w_in"], tb=True, name=tag + "_dffn_h")
        g_win = _mm(s["h2"], du, ta=True, out_dtype=BF16, name=tag + "_gffn_in")
        dx, dg2, dsc2, dsh2 = _adaln_bwd(s["x1"], dh2, dx, norm2_g[i:i + 1], sc2, tag + "_dln2")
        gt1 = gt1 + layer_grads(i, "ffn", dx, dict(w_in=g_win, w_out=g_wout))
        dy, dgt1 = _resid_bwd(dx, s["y"], gt1, tag + "_dres1")
        if i % 2 == 0:
            dh1, g_mix = _dn_layer_bwd(dy, s["h1"], p, p["conv_w"], p["a_log"], p["dt_bias"], p["onorm_g"], s["mix"],
                                       tag + "_dn")
        else:
            dh1, g_mix = _sb_layer_bwd(dy, s["h1"], p, p["q_g"], p["k_g"], s["mix"], tag + "_sb")
        dx, dg1, dsc1, dsh1 = _adaln_bwd(s["x0"], dh1, dx, norm1_g[i:i + 1], sc1, tag + "_dln1")
        dmod[i] = jnp.concatenate([dsh1, dsc1, dgt1, dsh2, dsc2, dgt2], axis=1)
        dn1[i], dn2[i] = dg1, dg2
        zero = layer_grads(i, "mix", dx, g_mix)
    return loss, dx, jnp.concatenate(dmod, axis=0), jnp.concatenate(dn1, axis=0), jnp.concatenate(dn2, axis=0)


def _axes():
    return lax.axis_index("x"), lax.axis_index("y"), lax.axis_index("c")


def _remote(src, dst, send_sem, recv_sem, dev):
    return pltpu.make_async_remote_copy(src_ref=src, dst_ref=dst, send_sem=send_sem, recv_sem=recv_sem,
                                        device_id=dev, device_id_type=MESH)


def _other_chips(x, y):
    return [(1 - x, y), (x, 1 - y), (1 - x, 1 - y)]


def _allgather_small(v, name):
    m, n = v.shape

    def body(x_ref, out_ref, send_sems, recv_sems, local_sem):
        x, y, c = _axes()
        me, sibling = (x, y, c), (x, y, 1 - c)
        chips = _other_chips(x, y)

        def rows(px, py, pc):
            return out_ref.at[pl.ds((4 * px + 2 * py + pc) * m, m), :]

        def copy(k, block, to, src=None):
            return _remote(rows(*block) if src is None else src, rows(*block), send_sems.at[k], recv_sems.at[k], to)

        mine = pltpu.make_async_copy(x_ref, rows(*me), local_sem)
        mine.start()
        first = [copy(0, me, sibling, src=x_ref)]
        first += [copy(1 + j, me, (*chip, c), src=x_ref) for j, chip in enumerate(chips)]
        for cp in first:
            cp.start()
        passed = [copy(4 + j, (*chip, c), sibling) for j, chip in enumerate(chips)]
        for j, chip in enumerate(chips):
            copy(1 + j, (*chip, c), me).wait_recv()
            passed[j].start()
        copy(0, sibling, me).wait_recv()
        for j, chip in enumerate(chips):
            copy(4 + j, (*chip, 1 - c), me).wait_recv()
        for cp in first + passed:
            cp.wait_send()
        mine.wait()

    return pl.pallas_call(
        body, out_shape=jax.ShapeDtypeStruct((8 * m, n), v.dtype),
        in_specs=[pl.BlockSpec(memory_space=pltpu.VMEM)], out_specs=pl.BlockSpec(memory_space=pltpu.VMEM),
        scratch_shapes=[pltpu.SemaphoreType.DMA((7,)), pltpu.SemaphoreType.DMA((7,)), pltpu.SemaphoreType.DMA],
        name=name, compiler_params=pltpu.CompilerParams(vmem_limit_bytes=VMEM_LIMIT))(v)


def _half(ref, h, rh):
    return ref.at[(slice(None),) * (len(ref.shape) - 2) + (pl.ds(h * rh, rh), slice(None))]


def _hbm_call(body, ins, out_shapes, n_sems, n_local, name):
    hbm = pl.BlockSpec(memory_space=pltpu.HBM)
    scratch = [pltpu.SemaphoreType.DMA((n_sems,)), pltpu.SemaphoreType.DMA((n_sems,))]
    if n_local:
        scratch.append(pltpu.SemaphoreType.DMA((n_local,)))
    return pl.pallas_call(body, out_shape=out_shapes, in_specs=[hbm] * len(ins), out_specs=[hbm] * len(out_shapes),
                          scratch_shapes=scratch, name=name)(*ins)


def _sibling_swap_halves(arrs, name):
    n = len(arrs)

    def body(*refs):
        v_refs, out_refs, (send_sems, recv_sems) = refs[:n], refs[n:2 * n], refs[2 * n:]
        x, y, c = _axes()
        cps = [_remote(_half(v_refs[i], 1 - c, arrs[i].shape[-2] // 2), out_refs[i], send_sems.at[i], recv_sems.at[i],
                       (x, y, 1 - c)) for i in range(n)]
        for cp in cps:
            cp.start()
        for cp in cps:
            cp.wait()

    outs = [jax.ShapeDtypeStruct(a.shape[:-2] + (a.shape[-2] // 2, a.shape[-1]), a.dtype) for a in arrs]
    return _hbm_call(body, arrs, outs, n, 0, name)


def _ici_src(ref, mode, slot, c):
    return _half(ref, c, ref.shape[-2] // 2) if mode == "half" else ref.at[slot]


def _ici_start(srcs, mode, name):
    n, ncp = len(srcs), 3 * len(srcs)
    lands = [jnp.zeros(((4,) + s.shape[:-2] + (s.shape[-2] // 2, s.shape[-1])) if mode == "half" else s.shape, s.dtype)
             for s in srcs]

    def body(*refs):
        src_refs, land_refs = refs[:n], refs[n:2 * n]
        send_sems, recv_sems = refs[2 * n:2 * n + ncp], refs[2 * n + ncp:2 * n + 2 * ncp]
        token = refs[-1]
        x, y, c = _axes()
        for i in range(n):
            for j, (cx, cy) in enumerate(_other_chips(x, y)):
                _remote(_ici_src(src_refs[i], mode, 2 * cx + cy, c), land_refs[i].at[2 * x + y], send_sems[3 * i + j],
                        recv_sems[3 * i + j], (cx, cy, c)).start()
        token[...] = jnp.zeros(token.shape, token.dtype)

    hbm, sem = pl.BlockSpec(memory_space=pltpu.HBM), pl.BlockSpec(memory_space=pltpu.SEMAPHORE)
    bufs = srcs + lands
    out_shape = tuple([pltpu.SemaphoreType.DMA(())] * (2 * ncp) + [pltpu.HBM(b.shape, b.dtype) for b in bufs]
                      + [jax.ShapeDtypeStruct((8, LANES), F32)])
    res = pl.pallas_call(
        body, name=name, out_shape=out_shape, in_specs=(hbm,) * (2 * n),
        out_specs=(sem,) * (2 * ncp) + (hbm,) * (2 * n) + (pl.BlockSpec(memory_space=pltpu.VMEM),),
        input_output_aliases={i: 2 * ncp + i for i in range(2 * n)},
        compiler_params=pltpu.CompilerParams(has_side_effects=pltpu.SideEffectType.DATAFLOW_SIDE_EFFECTING),
    )(*[pltpu.with_memory_space_constraint(b, pltpu.HBM) for b in bufs])
    return list(res[:2 * ncp]), list(res[2 * ncp:2 * ncp + 2 * n]), res[-1][0, 0]


def _ici_wait(sems, bufs, mode, after, name):
    n, ncp = len(bufs) // 2, len(sems) // 2

    def body(*refs):
        src_refs, land_refs = refs[:n], refs[n:2 * n]
        send_sems, recv_sems = refs[2 * n:2 * n + ncp], refs[2 * n + ncp:2 * n + 2 * ncp]
        x, y, c = _axes()
        for i in range(n):
            for j, (cx, cy) in enumerate(_other_chips(x, y)):
                cp = _remote(_ici_src(src_refs[i], mode, 2 * cx + cy, c), land_refs[i].at[2 * cx + cy],
                             send_sems[3 * i + j], recv_sems[3 * i + j], (cx, cy, c))
                cp.wait_send()
                cp.wait_recv()

    hbm, sem = pl.BlockSpec(memory_space=pltpu.HBM), pl.BlockSpec(memory_space=pltpu.SEMAPHORE)
    res = pl.pallas_call(
        body, name=name, out_shape=tuple(pltpu.HBM(b.shape, b.dtype) for b in bufs),
        in_specs=(hbm,) * (2 * n) + (sem,) * (2 * ncp) + (pl.BlockSpec(memory_space=pl.ANY),),
        out_specs=(hbm,) * (2 * n), input_output_aliases={i: i for i in range(2 * n)},
        compiler_params=pltpu.CompilerParams(has_side_effects=pltpu.SideEffectType.DATAFLOW_SIDE_EFFECTING),
    )(*bufs, *sems, after)
    return list(res[n:])


def _share_halves(arrs, name):
    n = len(arrs)

    def body(*refs):
        v_refs, out_refs, (send_sems, recv_sems) = refs[:n], refs[n:2 * n], refs[2 * n:]
        x, y, c = _axes()
        cps = [_remote(v_refs[i], out_refs[i], send_sems.at[i], recv_sems.at[i], (x, y, 1 - c)) for i in range(n)]
        for cp in cps:
            cp.start()
        for cp in cps:
            cp.wait()

    outs = [jax.ShapeDtypeStruct(a.shape, a.dtype) for a in arrs]
    return _hbm_call(body, arrs, outs, n, 0, name)


def _assemble_weights(mine, other, own, axis, name):
    nchips, l, rh, cs = mine.shape
    r = 2 * rh

    def body(m_ref, o_ref, w_ref, out_ref):
        x, y, c = _axes()
        is_own = pl.program_id(1) == 2 * x + y

        @pl.when(is_own)
        def _():
            out_ref[...] = w_ref[...]

        @pl.when(jnp.logical_not(is_own))
        def _():
            out_ref[pl.ds(pl.multiple_of(c * rh, rh), rh), :] = m_ref[...]
            out_ref[pl.ds(pl.multiple_of((1 - c) * rh, rh), rh), :] = o_ref[...]

    half = pl.BlockSpec((None, None, rh, cs), lambda i, s: (s, i, 0, 0))
    whole = pl.BlockSpec((None, r, cs), lambda i, s: (i, 0, 0))
    if axis == 2:
        out_spec, shape = pl.BlockSpec((None, r, cs), lambda i, s: (i, 0, s)), (l, r, nchips * cs)
    else:
        out_spec, shape = pl.BlockSpec((None, r, cs), lambda i, s: (i, s, 0)), (l, nchips * r, cs)
    return pl.pallas_call(
        body, grid=(l, nchips), in_specs=[half, half, whole], out_specs=out_spec,
        out_shape=jax.ShapeDtypeStruct(shape, mine.dtype), name=name,
        compiler_params=_cparams(("parallel", "arbitrary")))(mine, other, own)


def _sum_chips(land, own, name):
    k, r, w = land.shape
    tr = _pick_tile(r, (256, 128, 64, 32, 16))

    def body(land_ref, own_ref, o_ref):
        x, y, _ = _axes()
        me = 2 * x + y
        acc = jnp.zeros((tr, w), F32)
        for s in range(k):
            acc = acc + jnp.where(me == s, own_ref[s], land_ref[s]).astype(F32)
        o_ref[...] = acc

    spec = pl.BlockSpec((k, tr, w), lambda i: (0, i, 0))
    return pl.pallas_call(
        body, grid=(r // tr,), in_specs=[spec, spec],
        out_specs=pl.BlockSpec((tr, w), lambda i: (i, 0)), out_shape=jax.ShapeDtypeStruct((r, w), F32), name=name,
        compiler_params=_cparams(("parallel",)))(land, own)


_BIG = (("dn_w_in", 2), ("dn_w_out", 1), ("sb_w_qkv", 2), ("sb_w_out", 1), ("ffn_w_in", 2), ("ffn_w_out", 1))


def _to_chip_major(a, axis, nchips):
    l, r, c = a.shape
    if axis == 2:
        return a.reshape(l, r, nchips, c // nchips).transpose(2, 0, 1, 3)
    return a.reshape(l, nchips, r // nchips, c).transpose(1, 0, 2, 3)


def _from_chip_major(a, axis):
    n, l, r, c = a.shape
    if axis == 2:
        return a.transpose(1, 2, 0, 3).reshape(l, r, n * c)
    return a.transpose(1, 0, 2, 3).reshape(l, n * r, c)


def kernel(x, c, ada_w, ada_b, norm1_g, norm2_g, dn_w_in, dn_conv_w, dn_a_log, dn_dt_bias, dn_onorm_g, dn_w_out, sb_w_qkv, sb_q_norm_g, sb_k_norm_g, sb_w_out, ffn_w_in, ffn_w_out, loss_target, m_ada_w, m_ada_b, m_norm1_g, m_norm2_g, m_dn_w_in, m_dn_conv_w, m_dn_a_log, m_dn_dt_bias, m_dn_onorm_g, m_dn_w_out, m_sb_w_qkv, m_sb_q_norm_g, m_sb_k_norm_g, m_sb_w_out, m_ffn_w_in, m_ffn_w_out, v_ada_w, v_ada_b, v_norm1_g, v_norm2_g, v_dn_w_in, v_dn_conv_w, v_dn_a_log, v_dn_dt_bias, v_dn_onorm_g, v_dn_w_out, v_sb_w_qkv, v_sb_q_norm_g, v_sb_k_norm_g, v_sb_w_out, v_ffn_w_in, v_ffn_w_out):
    names = ("ada_w", "ada_b", "norm1_g", "norm2_g", "dn_w_in", "dn_conv_w", "dn_a_log", "dn_dt_bias", "dn_onorm_g",
             "dn_w_out", "sb_w_qkv", "sb_q_norm_g", "sb_k_norm_g", "sb_w_out", "ffn_w_in", "ffn_w_out")
    w = dict(zip(names, (ada_w, ada_b, norm1_g, norm2_g, dn_w_in, dn_conv_w, dn_a_log, dn_dt_bias, dn_onorm_g,
                         dn_w_out, sb_w_qkv, sb_q_norm_g, sb_k_norm_g, sb_w_out, ffn_w_in, ffn_w_out)))
    mom = dict(zip(names, (m_ada_w, m_ada_b, m_norm1_g, m_norm2_g, m_dn_w_in, m_dn_conv_w, m_dn_a_log, m_dn_dt_bias,
                           m_dn_onorm_g, m_dn_w_out, m_sb_w_qkv, m_sb_q_norm_g, m_sb_k_norm_g, m_sb_w_out, m_ffn_w_in,
                           m_ffn_w_out)))
    var = dict(zip(names, (v_ada_w, v_ada_b, v_norm1_g, v_norm2_g, v_dn_w_in, v_dn_conv_w, v_dn_a_log, v_dn_dt_bias,
                           v_dn_onorm_g, v_dn_w_out, v_sb_w_qkv, v_sb_q_norm_g, v_sb_k_norm_g, v_sb_w_out, v_ffn_w_in,
                           v_ffn_w_out)))
    ax, ay, ac = _axes()
    chip = 2 * ax + ay
    dev = 2 * chip + ac
    t, d = x.shape[1], x.shape[2]
    depth, ndn, nsb = ada_w.shape[0], dn_w_in.shape[0], sb_w_qkv.shape[0]
    heads = d // DN_HEAD_DIM
    mod_cols = ada_w.shape[2]
    conv_cols = dn_conv_w.shape[2]
    nchips, ndev = 4, 8

    conv_rows = ndn * DN_CONV * conv_cols // d
    pay1 = jnp.concatenate([c, dn_conv_w.reshape(conv_rows, d), jnp.zeros((8 - 1 - conv_rows, d), F32)], axis=0)
    g1 = _allgather_small(pay1, "ag_cond").reshape(ndev, 8, d)
    c_all = g1[:, 0]
    conv_full = g1[::2, 1:1 + conv_rows].reshape(nchips, ndn, DN_CONV, conv_cols).transpose(1, 2, 0, 3)
    conv_full = conv_full.reshape(ndn, DN_CONV, nchips * conv_cols)

    c16 = jnp.pad(c_all, ((0, 16 - ndev), (0, 0)))
    cond16 = _rowwise(lambda r, v: ([_silu(r[0])], []), [c16], [], [(d, F32)], [], name="cond_silu")[0]
    pay2 = jnp.concatenate([_mm(cond16, ada_w[i], name="ada_mod%d" % i)[:ndev] for i in range(depth)], axis=0)
    g2 = _allgather_small(pay2, "ag_mod").reshape(ndev, depth, ndev, mod_cols)[::2]
    mod_raw = lax.dynamic_index_in_dim(g2, dev, axis=2, keepdims=False)
    mod_raw = mod_raw.transpose(1, 0, 2).reshape(depth, nchips * mod_cols)
    mod = _rowwise(lambda r, v: ([r[0] + r[1]], []), [mod_raw, ada_b], [], [(nchips * mod_cols, F32)], [],
                   name="ada_bias")[0]

    axis_of = dict(_BIG)

    def kinds_of(layers):
        out = []
        for n, _ in _BIG:
            idx = [l // 2 for l in layers if l % 2 == (0 if n.startswith("dn_") else 1)] if n[:3] in ("dn_", "sb_") \
                else list(layers)
            if idx:
                out.append((n, min(idx), max(idx) + 1))
        return out

    two_d = lambda a: a.reshape(-1, a.shape[-1])
    both_halves = lambda mine, other, ax: jnp.where(ac == 0, jnp.concatenate([mine, other], axis=ax),
                                                    jnp.concatenate([other, mine], axis=ax))
    is_ffn = lambda k: k[0].startswith("ffn")
    groups = [([k for k in kinds_of([0]) if not is_ffn(k)], ("mix", 0)),
              ([k for k in kinds_of([0]) if is_ffn(k)], ("ffn", 0)),
              (kinds_of(list(range(1, depth))), ("mix", 1))]
    groups = [g for g in groups if g[0]]
    w16 = {n: w[n].astype(BF16) for n, _ in _BIG}
    gathers = []
    for gi, (sl, _) in enumerate(groups):
        shards = [w16[n][lo:hi] for n, lo, hi in sl]
        sems, bufs, zero = _ici_start(shards, "half", "ag_start%d" % gi)
        gathers.append((sl, shards, sems, bufs))
        mod = mod + zero
    full = {}

    def finish_gather(gi, after):
        sl, shards, sems, bufs = gathers[gi]
        lands = _ici_wait(sems, bufs, "half", after, "ag_wait%d" % gi)
        for (n, lo, hi), shard, mine, other in zip(sl, shards, lands, _share_halves(lands, "ag_pair%d" % gi)):
            if shard.shape[-1] % LANES == 0:
                a = _assemble_weights(mine, other, shard, axis_of[n], "ag_whole%d_%s" % (gi, n))
            else:
                a = lax.dynamic_update_index_in_dim(both_halves(mine, other, 2), shard, chip, 0)
                a = _from_chip_major(a, axis_of[n])
            for l in range(lo, hi):
                full[n, l] = a[l - lo]

    padl = lambda v: jnp.pad(v[None, :], ((0, 0), (0, LANES - v.shape[0])))
    padc = lambda a: jnp.pad(a, ((0, 0), (0, LANES - a.shape[1])))

    def layer_weights(i, after):
        for gi, (_, trigger) in enumerate(groups):
            if trigger == ("mix", i):
                finish_gather(gi, after)

        def ffn_of(after_mixer):
            for gi, (_, trigger) in enumerate(groups):
                if trigger == ("ffn", i):
                    finish_gather(gi, after_mixer)
            return dict(w_in=full["ffn_w_in", i], w_out=full["ffn_w_out", i])
        j = i // 2
        if i % 2 == 0:
            wi = full["dn_w_in", j]
            w_all = jnp.concatenate([wi[:, :4 * d], padc(wi[:, 4 * d:4 * d + heads]), padc(wi[:, 4 * d + heads:])],
                                    axis=1)
            mix = dict(all=w_all, out=full["dn_w_out", j], conv_w=conv_full[j], a_log=padl(dn_a_log[j]),
                       dt_bias=padl(dn_dt_bias[j]), onorm_g=dn_onorm_g[j][None])
        else:
            mix = dict(qkv=full["sb_w_qkv", j], out=full["sb_w_out", j], q_g=sb_q_norm_g[j][None],
                       k_g=sb_k_norm_g[j][None])
        return mix, ffn_of

    g_dn, g_sb, g_ffn = [None] * ndn, [None] * nsb, [None] * depth
    reduces = {}

    def gw_in(g):
        ga = g["w_all"]
        return jnp.concatenate([ga[:, :4 * d], ga[:, 4 * d:4 * d + heads], ga[:, 4 * d + LANES:4 * d + LANES + heads]],
                               axis=1)
    layer_grad = dict(dn_w_in=lambda j: gw_in(g_dn[j]), dn_w_out=lambda j: g_dn[j]["w_out"],
                      sb_w_qkv=lambda j: g_sb[j]["w_qkv"], sb_w_out=lambda j: g_sb[j]["w_out"],
                      ffn_w_in=lambda l: g_ffn[l]["w_in"], ffn_w_out=lambda l: g_ffn[l]["w_out"])

    def layer_grads(i, part, after, g):
        if part == "ffn":
            g_ffn[i] = g
        else:
            (g_dn if i % 2 == 0 else g_sb)[i // 2] = g
        zero = jnp.zeros((), F32)
        for gi, (sl, trigger) in enumerate(groups):
            if trigger == (part, i):
                parts = [_to_chip_major(jnp.stack([layer_grad[n](j) for j in range(lo, hi)]), axis_of[n], nchips)
                         for n, lo, hi in sl]
                pairs = []
                for (n, _, _), p, fs in zip(sl, parts, _sibling_swap_halves(parts, "gr_pair%d" % gi)):
                    rh = fs.shape[2]
                    own = lax.dynamic_slice_in_dim(p, ac * rh, rh, axis=2)
                    pairs.append(_rowwise(lambda r, v: ([r[0].astype(F32) + r[1].astype(F32)], []),
                                          [two_d(own), two_d(fs)], [], [(fs.shape[-1], BF16)], [],
                                          name="gr_pair_add%d_%s" % (gi, n))[0].reshape(fs.shape))
                sems, bufs, zero = _ici_start(pairs, "slot", "gr_start%d" % gi)
                reduces[gi] = (sl, pairs, sems, bufs)
        return zero

    loss_local, grad_x, dmod, g_n1, g_n2 = _local_step(x[0], loss_target[0], mod, norm1_g, norm2_g, layer_weights,
                                                       layer_grads)
    loss = lax.psum(loss_local, ("x", "y", "c"))

    g_conv = jnp.stack([g["conv_w"] for g in g_dn])
    misc = jnp.concatenate([jnp.concatenate([g["onorm_g"] for g in g_dn], axis=1),
                            jnp.concatenate([g["a_log"] for g in g_dn], axis=1),
                            jnp.concatenate([g["dt_bias"] for g in g_dn], axis=1),
                            jnp.concatenate([g["q_norm_g"] for g in g_sb], axis=1),
                            jnp.concatenate([g["k_norm_g"] for g in g_sb], axis=1)], axis=1)
    misc = jnp.pad(misc, ((0, 0), (0, -misc.shape[1] % d))).reshape(-1, d)
    small = [dmod.reshape(-1, d), g_n1, g_n2, g_conv.reshape(-1, d), misc]
    small_rows = [s.shape[0] for s in small]
    pad_rows = -sum(small_rows) % 8
    pay3 = jnp.concatenate(small + [jnp.zeros((pad_rows, d), F32)], axis=0)
    nrow3 = pay3.shape[0]
    g3 = _allgather_small(pay3, "ag_small")
    summed = _rowwise(lambda r, v: ([], [_colsum(r[0])]), [g3.reshape(ndev, nrow3 * d)], [], [], [(1, nrow3 * d)],
                      name="small_sum")[0].reshape(nrow3, d)
    offs = [0]
    for n_ in small_rows:
        offs.append(offs[-1] + n_)
    grads = {}
    grads["ada_b"] = summed[offs[0]:offs[1]].reshape(depth, N_MOD * d)
    grads["norm1_g"] = summed[offs[1]:offs[2]]
    grads["norm2_g"] = summed[offs[2]:offs[3]]
    conv_sum = summed[offs[3]:offs[4]].reshape(ndn, DN_CONV, nchips * conv_cols)
    grads["dn_conv_w"] = lax.dynamic_slice_in_dim(conv_sum, chip * conv_cols, conv_cols, axis=2)
    mrow = summed[offs[4]:offs[5]].reshape(-1)
    o = 0
    grads["dn_onorm_g"] = mrow[o:o + ndn * DN_HEAD_DIM].reshape(ndn, DN_HEAD_DIM)
    o += ndn * DN_HEAD_DIM
    grads["dn_a_log"] = mrow[o:o + ndn * LANES].reshape(ndn, LANES)[:, :heads]
    o += ndn * LANES
    grads["dn_dt_bias"] = mrow[o:o + ndn * LANES].reshape(ndn, LANES)[:, :heads]
    o += ndn * LANES
    grads["sb_q_norm_g"] = mrow[o:o + nsb * SB_HEAD_DIM].reshape(nsb, SB_HEAD_DIM)
    o += nsb * SB_HEAD_DIM
    grads["sb_k_norm_g"] = mrow[o:o + nsb * SB_HEAD_DIM].reshape(nsb, SB_HEAD_DIM)
    dmod_all = g3.reshape(ndev, nrow3, d)[:, :small_rows[0]].reshape(ndev, depth, N_MOD * d)
    dmod_mine = lax.dynamic_slice_in_dim(dmod_all, chip * mod_cols, mod_cols, axis=2)
    dmod16 = jnp.pad(dmod_mine, ((0, 16 - ndev), (0, 0), (0, 0)))
    grads["ada_w"] = jnp.stack([_mm(cond16, dmod16[:, i], ta=True, name="ada_gw%d" % i) for i in range(depth)])

    pieces = {n: [] for n, _ in _BIG}
    for gi in reversed(range(len(groups))):
        if gi not in reduces:
            continue
        sl, pairs, sems, bufs = reduces[gi]
        lands = _ici_wait(sems, bufs, "slot", grad_x, "gr_wait%d" % gi)
        flat = lambda a: a.reshape(nchips, -1, a.shape[-1])
        reduced = [_sum_chips(flat(land), flat(pair), "gr_chip_add%d_%s" % (gi, n))
                   for (n, _, _), land, pair in zip(sl, lands, pairs)]
        for (n, lo, hi), mine, other in zip(sl, reduced, _share_halves(reduced, "gr_share%d" % gi)):
            shape = (hi - lo, -1, mine.shape[-1])
            pieces[n].append((lo, both_halves(mine.reshape(shape), other.reshape(shape), 1)))
    for n, _ in _BIG:
        grads[n] = jnp.concatenate([p for _, p in sorted(pieces[n], key=lambda t: t[0])], axis=0)

    delta, new_m, new_v = {}, {}, {}
    for n in names:
        delta[n], new_m[n], new_v[n] = _adamw(w[n], grads[n], mom[n], var[n], "adamw_" + n)
    return (loss, grad_x[None], *[grads[n] for n in names], *[delta[n] for n in names], *[new_m[n] for n in names],
            *[new_v[n] for n in names])
```

```python
import functools

import jax
import jax.numpy as jnp
from jax import lax
from jax.experimental import pallas as pl
from jax.experimental.pallas import tpu as pltpu

F32 = jnp.float32
BF16 = jnp.bfloat16
HI = lax.Precision.HIGHEST
MESH = pl.DeviceIdType.MESH

EPS = 1e-6
N_MOD = 6
DN_HEAD_DIM = 128
DN_CONV = 4
DN_CHUNK = 64
DN_GROUP = 8
DN_GATE_ROWS = 256
SB_HEAD_DIM = 64
SB_BLOCK = 128
SB_QBLOCK = 512
SB_PART = 128
LANES = 128
VMEM_LIMIT = 56 * 1024 * 1024
MM_BLOCK_BUDGET = 36 * 1024 * 1024
ROW_BLOCK_BUDGET = 20 * 1024 * 1024

ADAM_LR = 0.001
ADAM_B1 = 0.9
ADAM_B2 = 0.999
ADAM_EPS = 1e-08
ADAM_WD = 0.01
ADAM_STEP = 10


def _cparams(sem=None):
    return pltpu.CompilerParams(dimension_semantics=sem, vmem_limit_bytes=VMEM_LIMIT)


def _sigmoid(x):
    return 1.0 / (1.0 + jnp.exp(-x))


def _silu(x):
    return x * _sigmoid(x)


def _dsilu(x):
    s = _sigmoid(x)
    return s * (1.0 + x * (1.0 - s))


def _softplus(x):
    return jnp.maximum(x, 0.0) + jnp.log1p(jnp.exp(-jnp.abs(x)))


def _pick_tile(n, prefs):
    for t in prefs:
        if n % t == 0:
            return t
    return n


def _mm(a, b, *, ta=False, tb=False, out_dtype=F32, name):
    m = a.shape[1] if ta else a.shape[0]
    k = a.shape[0] if ta else a.shape[1]
    n = b.shape[0] if tb else b.shape[1]
    assert (b.shape[1] if tb else b.shape[0]) == k
    size = lambda dt: jnp.dtype(dt).itemsize
    best = None
    for tm in sorted({t for t in (m, 2048, 1024, 512, 256, 128) if m % t == 0 and (t % 128 == 0 or t == m)}):
        for tn in sorted({t for t in (n, 2816, 1408, 1024, 768, 512, 256, 128) if n % t == 0 and (t % 128 == 0 or t == n)}):
            need = 2 * (tm * k * size(a.dtype) + tn * k * size(b.dtype) + tm * tn * size(out_dtype))
            if need <= MM_BLOCK_BUDGET and (best is None or tm * tn > best[0] * best[1]):
                best = (tm, tn)
    tm, tn = best
    dims = (((0 if ta else 1,), (1 if tb else 0,)), ((), ()))

    def body(a_ref, b_ref, o_ref):
        av = a_ref[...].astype(BF16)
        bv = b_ref[...].astype(BF16)
        o_ref[...] = lax.dot_general(av, bv, dims, preferred_element_type=F32).astype(o_ref.dtype)

    a_spec = pl.BlockSpec((k, tm), lambda i, j: (0, i)) if ta else pl.BlockSpec((tm, k), lambda i, j: (i, 0))
    b_spec = pl.BlockSpec((tn, k), lambda i, j: (j, 0)) if tb else pl.BlockSpec((k, tn), lambda i, j: (0, j))
    return pl.pallas_call(
        body, grid=(m // tm, n // tn), in_specs=[a_spec, b_spec],
        out_specs=pl.BlockSpec((tm, tn), lambda i, j: (i, j)),
        out_shape=jax.ShapeDtypeStruct((m, n), out_dtype), name=name,
        compiler_params=_cparams(("parallel", "parallel")))(a, b)


def _rowwise(body, rows, vecs, outs, accs, *, name, tr=None):
    rows = [r if isinstance(r, tuple) else (r, r.shape[1], 0) for r in rows]
    nrows = rows[0][0].shape[0]
    if tr is None:
        per_row = sum(w * jnp.dtype(a.dtype).itemsize for a, w, _ in rows) + sum(
            w * jnp.dtype(dt).itemsize for w, dt in outs)
        tr = next((t for t in (2048, 1024, 512, 256) if nrows % t == 0 and 2 * t * per_row <= ROW_BLOCK_BUDGET), 256)
    tr = tr if nrows % tr == 0 else nrows
    nr, nv, no = len(rows), len(vecs), len(outs)

    def kern(*refs):
        r_in, v_in = refs[:nr], refs[nr:nr + nv]
        o_refs, a_refs = refs[nr + nv:nr + nv + no], refs[nr + nv + no:]
        res_o, res_a = body([r[...] for r in r_in], [v[...] for v in v_in])
        for o, val in zip(o_refs, res_o):
            o[...] = val.astype(o.dtype)
        if a_refs:
            @pl.when(pl.program_id(0) == 0)
            def _():
                for a in a_refs:
                    a[...] = jnp.zeros(a.shape, a.dtype)
            for a, val in zip(a_refs, res_a):
                a[...] += val

    in_specs = [pl.BlockSpec((tr, w), functools.partial(lambda i, cb: (i, cb), cb=cb)) for _, w, cb in rows]
    in_specs += [pl.BlockSpec(v.shape, functools.partial(lambda i, nd: (0,) * nd, nd=v.ndim)) for v in vecs]
    out_specs = [pl.BlockSpec((tr, w), lambda i: (i, 0)) for w, _ in outs]
    out_specs += [pl.BlockSpec(s, lambda i: (0, 0)) for s in accs]
    out_shape = [jax.ShapeDtypeStruct((nrows, w), dt) for w, dt in outs]
    out_shape += [jax.ShapeDtypeStruct(s, F32) for s in accs]
    res = pl.pallas_call(
        kern, grid=(nrows // tr,), in_specs=in_specs, out_specs=out_specs, out_shape=out_shape, name=name,
        compiler_params=_cparams(("arbitrary",) if accs else ("parallel",)))(*[r[0] for r in rows], *vecs)
    return res


def _colsum(v):
    return jnp.sum(v, axis=0, keepdims=True)


def _adaln_fwd(x, g, sc, sh, name):
    def body(r, v):
        (xv,), (gv, scv, shv) = r, v
        rs = lax.rsqrt(jnp.mean(xv * xv, axis=-1, keepdims=True) + EPS)
        return [(xv * rs) * gv * (1.0 + scv) + shv], []
    return _rowwise(body, [x], [g, sc, sh], [(x.shape[1], BF16)], [], name=name)[0]


def _adaln_bwd(x, dh, dxr, g, sc, name):
    d = x.shape[1]

    def body(r, v):
        (xv, dhv, dxv), (gv, scv) = r, v
        rs = lax.rsqrt(jnp.mean(xv * xv, axis=-1, keepdims=True) + EPS)
        nv = xv * rs
        dn = dhv * (gv * (1.0 + scv))
        dx = rs * (dn - nv * jnp.mean(dn * nv, axis=-1, keepdims=True)) + dxv
        dhn = dhv * nv
        return [dx], [_colsum(dhn * (1.0 + scv)), _colsum(dhn * gv), _colsum(dhv)]
    return _rowwise(body, [x, dh, dxr], [g, sc], [(d, F32)], [(1, d)] * 3, name=name)


def _resid_fwd(x, y, gt, name):
    def body(r, v):
        return [r[0] + v[0] * r[1]], []
    return _rowwise(body, [x, y], [gt], [(x.shape[1], F32)], [], name=name)[0]


def _resid_bwd(dx, y, gt, name):
    d = dx.shape[1]

    def body(r, v):
        return [v[0] * r[0]], [_colsum(r[0] * r[1])]
    return _rowwise(body, [dx, y], [gt], [(d, BF16)], [(1, d)], name=name)


def _swiglu_fwd(u, name):
    f = u.shape[1] // 2

    def body(r, v):
        uv = r[0]
        return [_silu(uv[:, :f]) * uv[:, f:]], []
    return _rowwise(body, [u], [], [(f, BF16)], [], name=name)[0]


def _swiglu_bwd(u, da, name):
    f = u.shape[1] // 2

    def body(r, v):
        uv, dav = r
        gate, up = uv[:, :f], uv[:, f:]
        return [jnp.concatenate([dav * up * _dsilu(gate), dav * _silu(gate)], axis=1)], []
    return _rowwise(body, [u, da], [], [(2 * f, BF16)], [], name=name)[0]


def _loss_fwd_bwd(y, tgt, name):
    d = y.shape[1]

    def body(r, v):
        err = r[0] - r[1]
        part = jnp.sum(jnp.sum(err * err, axis=1, keepdims=True), axis=0, keepdims=True) * (0.5 / d)
        return [err * (1.0 / d)], [jnp.broadcast_to(part, (1, LANES))]
    dy, acc = _rowwise(body, [y, tgt], [], [(d, F32)], [(1, LANES)], name=name)
    return acc[0, 0], dy


def _head_means(v):
    row = lax.broadcasted_iota(jnp.int32, (LANES, LANES), 0)
    col = lax.broadcasted_iota(jnp.int32, (LANES, LANES), 1)
    same = ((row // SB_HEAD_DIM) == (col // SB_HEAD_DIM)).astype(F32)
    parts = [jnp.dot(v[:, g * LANES:(g + 1) * LANES], same, precision=HI, preferred_element_type=F32)
             for g in range(v.shape[1] // LANES)]
    return jnp.concatenate(parts, axis=1) * (1.0 / SB_HEAD_DIM)


def _sb_norm_fwd(qkv, gq, gk, name):
    d = qkv.shape[1] // 3

    def body(r, v):
        return [x * lax.rsqrt(_head_means(x * x) + EPS) * g for x, g in zip(r, v)], []
    return _rowwise(body, [(qkv, d, 0), (qkv, d, 1)], [gq, gk], [(d, F32), (d, F32)], [], name=name)


def _sb_norm_bwd(qkv, dqn, dkn, dv, gq, gk, name):
    d = qkv.shape[1] // 3

    def body(r, v):
        outs, accs = [], []
        for x, dy, g in ((r[0], r[2], v[0]), (r[1], r[3], v[1])):
            rs = lax.rsqrt(_head_means(x * x) + EPS)
            nv = x * rs
            dn = dy * g
            outs.append(rs * (dn - nv * _head_means(dn * nv)))
            accs.append(_colsum(dy * nv))
        return [jnp.concatenate(outs + [r[4]], axis=1)], accs
    return _rowwise(body, [(qkv, d, 0), (qkv, d, 1), dqn, dkn, dv], [gq, gk], [(3 * d, BF16)], [(1, d), (1, d)],
                    name=name)


def _head_tiles(a):
    return [a[:, h * DN_HEAD_DIM:(h + 1) * DN_HEAD_DIM] for h in range(a.shape[1] // DN_HEAD_DIM)]


def _dn_post_fwd(o, proj, g, name):
    d = o.shape[1]

    def body(r, v):
        outs = []
        for ov, zv in zip(_head_tiles(r[0]), _head_tiles(r[1])):
            rs = lax.rsqrt(jnp.mean(ov * ov, axis=-1, keepdims=True) + EPS)
            outs.append(ov * rs * v[0] * _silu(zv))
        return [jnp.concatenate(outs, axis=1)], []
    return _rowwise(body, [o, (proj, d, 3)], [g], [(d, BF16)], [], name=name)[0]


def _dn_post_bwd(o, proj, don, g, name):
    d = o.shape[1]

    def body(r, v):
        dos, dzs, dg = [], [], jnp.zeros((1, DN_HEAD_DIM), F32)
        for ov, zv, dv in zip(_head_tiles(r[0]), _head_tiles(r[1]), _head_tiles(r[2])):
            rs = lax.rsqrt(jnp.mean(ov * ov, axis=-1, keepdims=True) + EPS)
            nv = ov * rs
            s = _silu(zv)
            dn = dv * v[0] * s
            dos.append(rs * (dn - nv * jnp.mean(dn * nv, axis=-1, keepdims=True)))
            dzs.append(dv * nv * v[0] * _dsilu(zv))
            dg = dg + _colsum(dv * nv * s)
        return [jnp.concatenate(dos, axis=1), jnp.concatenate(dzs, axis=1)], [dg]
    return _rowwise(body, [o, (proj, d, 3), don], [g], [(d, F32), (d, BF16)], [(1, DN_HEAD_DIM)], name=name)


def _chunk_tri(tr, upper):
    row = lax.broadcasted_iota(jnp.int32, (tr, tr), 0)
    col = lax.broadcasted_iota(jnp.int32, (tr, tr), 1)
    same = (row // DN_CHUNK) == (col // DN_CHUNK)
    return (same & ((row <= col) if upper else (row >= col))).astype(F32)


def _dn_gate_fwd(proj, colblk, a_log, dt_bias, name):
    def body(r, v):
        ab = r[0]
        a, b = ab[:, :LANES], ab[:, LANES:]
        g = -jnp.exp(v[0]) * _softplus(a + v[1])
        big_g = jnp.dot(_chunk_tri(g.shape[0], False), g, precision=HI, preferred_element_type=F32)
        return [big_g, _sigmoid(b)], []
    return _rowwise(body, [(proj, 2 * LANES, colblk)], [a_log, dt_bias], [(LANES, F32), (LANES, F32)], [], name=name,
                    tr=DN_GATE_ROWS)


def _dn_gate_bwd(proj, colblk, d_big_g, dbeta, a_log, dt_bias, name):
    def body(r, v):
        ab, dgc, dbt = r
        a, b = ab[:, :LANES], ab[:, LANES:]
        dg = jnp.dot(_chunk_tri(dgc.shape[0], True), dgc, precision=HI, preferred_element_type=F32)
        na = -jnp.exp(v[0])
        pre = a + v[1]
        da = dg * na * _sigmoid(pre)
        beta = _sigmoid(b)
        db = dbt * beta * (1.0 - beta)
        return [jnp.concatenate([da, db], axis=1)], [_colsum(dg * na * _softplus(pre)), _colsum(da)]
    return _rowwise(body, [(proj, 2 * LANES, colblk), d_big_g, dbeta], [a_log, dt_bias],
                    [(2 * LANES, BF16)], [(1, LANES), (1, LANES)], name=name, tr=DN_GATE_ROWS)


def _shift_rows(x, s):
    if s == 0:
        return x
    t = x.shape[0]
    row = lax.broadcasted_iota(jnp.int32, x.shape, 0)
    rolled = pltpu.roll(x, s % t, axis=0)
    return jnp.where((row >= s) if s > 0 else (row < t + s), rolled, 0.0)


def _rnd(x):
    return x.astype(BF16).astype(F32)


def _conv(x, c_ref):
    c = x * _rnd(c_ref[DN_CONV - 1:DN_CONV, :])
    for s in range(1, DN_CONV):
        c = c + _shift_rows(x, s) * _rnd(c_ref[DN_CONV - 1 - s:DN_CONV - s, :])
    return c


def _dn_prep_fwd(proj, conv_w, heads, name):
    t = proj.shape[0]
    d = heads * DN_HEAD_DIM

    def body(xq, xk, xv, cq, ck, cv, q_ref, k_ref, v_ref):
        for x_ref, c_ref, o_ref, norm in ((xq, cq, q_ref, True), (xk, ck, k_ref, True), (xv, cv, v_ref, False)):
            y = _silu(_conv(_rnd(x_ref[...]), c_ref))
            if norm:
                y = y * lax.rsqrt(jnp.sum(y * y, axis=-1, keepdims=True) + EPS)
            o_ref[...] = y

    xs = [pl.BlockSpec((t, DN_HEAD_DIM), functools.partial(lambda h, o: (0, h + o), o=o * heads)) for o in range(3)]
    cs = [pl.BlockSpec((DN_CONV, DN_HEAD_DIM), functools.partial(lambda h, o: (0, h + o), o=o * heads)) for o in range(3)]
    return pl.pallas_call(
        body, grid=(heads,), in_specs=xs + cs,
        out_specs=[pl.BlockSpec((t, DN_HEAD_DIM), lambda h: (0, h))] * 3,
        out_shape=[jax.ShapeDtypeStruct((t, d), F32)] * 3, name=name,
        compiler_params=_cparams(("parallel",)))(proj, proj, proj, conv_w, conv_w, conv_w)


def _dn_prep_bwd(proj, conv_w, dq, dk, dv, heads, name):
    t = proj.shape[0]
    d = heads * DN_HEAD_DIM

    def body(xq, xk, xv, cq, ck, cv, gq, gk, gv, oq, ok, ov, wq, wk, wv):
        for x_ref, c_ref, g_ref, o_ref, w_ref, norm in ((xq, cq, gq, oq, wq, True), (xk, ck, gk, ok, wk, True),
                                                        (xv, cv, gv, ov, wv, False)):
            x, dy = _rnd(x_ref[...]), g_ref[...]
            c = _conv(x, c_ref)
            if norm:
                y = _silu(c)
                rs = lax.rsqrt(jnp.sum(y * y, axis=-1, keepdims=True) + EPS)
                yn = y * rs
                dy = rs * (dy - yn * jnp.sum(dy * yn, axis=-1, keepdims=True))
            dc = _rnd(dy * _dsilu(c))
            dx = dc * _rnd(c_ref[DN_CONV - 1:DN_CONV, :])
            w_ref[DN_CONV - 1:DN_CONV, :] = _colsum(dc * x)
            for s in range(1, DN_CONV):
                dx = dx + _shift_rows(dc, -s) * _rnd(c_ref[DN_CONV - 1 - s:DN_CONV - s, :])
                w_ref[DN_CONV - 1 - s:DN_CONV - s, :] = _colsum(dc * _shift_rows(x, s))
            o_ref[...] = dx.astype(o_ref.dtype)

    xs = [pl.BlockSpec((t, DN_HEAD_DIM), functools.partial(lambda h, o: (0, h + o), o=o * heads)) for o in range(3)]
    cs = [pl.BlockSpec((DN_CONV, DN_HEAD_DIM), functools.partial(lambda h, o: (0, h + o), o=o * heads)) for o in range(3)]
    hs = pl.BlockSpec((t, DN_HEAD_DIM), lambda h: (0, h))
    ws = pl.BlockSpec((DN_CONV, DN_HEAD_DIM), lambda h: (0, h))
    return pl.pallas_call(
        body, grid=(heads,), in_specs=xs + cs + [hs] * 3, out_specs=[hs] * 3 + [ws] * 3,
        out_shape=[jax.ShapeDtypeStruct((t, d), BF16)] * 3 + [jax.ShapeDtypeStruct((DN_CONV, d), F32)] * 3, name=name,
        compiler_params=_cparams(("parallel",)))(proj, proj, proj, conv_w, conv_w, conv_w, dq, dk, dv)


X3 = "three bf16 passes"


def _bdot(a, b, dims, prec):
    if prec != X3:
        return lax.dot_general(a, b, dims, precision=prec, preferred_element_type=F32)
    a1, b1 = a.astype(BF16), b.astype(BF16)
    a2, b2 = (a - a1.astype(F32)).astype(BF16), (b - b1.astype(F32)).astype(BF16)
    dot = lambda x, y: lax.dot_general(x, y, dims, preferred_element_type=F32)
    return dot(a1, b1) + (dot(a1, b2) + dot(a2, b1))


def _bmm(a, b, prec=None):
    return _bdot(a, b, (((2,), (1,)), ((0,), (0,))), prec)


def _bmm_nt(a, b, prec=None):
    return _bdot(a, b, (((2,), (2,)), ((0,), (0,))), prec)


def _bmm_tn(a, b, prec=None):
    return _bdot(a, b, (((1,), (1,)), ((0,), (0,))), prec)


def _dn_local(qg, kg, vg, gc, gr, bt):
    c = qg.shape[1]
    row = lax.broadcasted_iota(jnp.int32, (c, c), 0)
    col = lax.broadcasted_iota(jnp.int32, (c, c), 1)
    incl, strict = (row >= col)[None], (row > col)[None]
    decay = jnp.where(incl, jnp.exp(jnp.where(incl, gc - gr, 0.0)), 0.0)
    kb = kg * bt
    vb = vg * bt
    m = _bmm_nt(kb.astype(BF16), kg.astype(BF16))
    a = jnp.where(strict, m * decay, 0.0)
    bp = -a
    tm = jnp.where((row == col)[None], 1.0, 0.0) + bp
    steps = max(1, (c - 1).bit_length()) - 1
    for _ in range(steps):
        bp = _bmm(bp, bp, X3)
        tm = tm + _bmm(tm, bp, X3)
    eg = jnp.exp(gc)
    glast = gc[:, c - 1:c, :]
    erel = jnp.exp(glast - gc)
    kbg = kb * eg
    qm = _bmm_nt(qg.astype(BF16), kg.astype(BF16))
    return dict(decay=decay, strict=strict, kb=kb, vb=vb, m=m, tm=tm, eg=eg, erel=erel, kbg=kbg, qm=qm,
                u=_bmm(tm, vb, X3), w=_bmm(tm, kbg, X3), qd=qg * eg, kt=kg * erel, gl=jnp.exp(glast))


def _dot(a, b):
    return jnp.dot(a.astype(BF16), b.astype(BF16), preferred_element_type=F32)


def _dot_nt(a, b):
    return lax.dot_general(a.astype(BF16), b.astype(BF16), (((1,), (1,)), ((), ())), preferred_element_type=F32)


def _dot_tn(a, b):
    return lax.dot_general(a.astype(BF16), b.astype(BF16), (((0,), (0,)), ((), ())), preferred_element_type=F32)


def _dn_chunk_specs(t, heads):
    n = t // DN_CHUNK
    hs = pl.BlockSpec((t, DN_HEAD_DIM), lambda h: (0, h))
    gcs = pl.BlockSpec((1, n, DN_CHUNK, 1), lambda h: (h, 0, 0, 0))
    grs = pl.BlockSpec((1, n, 1, DN_CHUNK), lambda h: (h, 0, 0, 0))
    ss = pl.BlockSpec((1, n, DN_HEAD_DIM, DN_HEAD_DIM), lambda h: (h, 0, 0, 0))
    return n, hs, gcs, grs, ss


def _dn_chunk_fwd(q, k, v, gc, gr, bc, name):
    t, d = q.shape
    heads = d // DN_HEAD_DIM
    c, dk = DN_CHUNK, DN_HEAD_DIM
    n, hs, gcs, grs, ss = _dn_chunk_specs(t, heads)
    nb = min(DN_GROUP, n)
    scale = dk ** -0.5

    def body(q_ref, k_ref, v_ref, gc_ref, gr_ref, b_ref, o_ref, s_ref, u_s, w_s, at_s, qd_s, kt_s, gl_s):
        def group(gi, carry):
            r0 = pl.multiple_of(gi * (nb * c), nb * c)
            n0 = gi * nb
            ld = lambda ref: ref[pl.ds(r0, nb * c), :].reshape(nb, c, dk)
            loc = _dn_local(ld(q_ref) * scale, ld(k_ref), ld(v_ref), gc_ref[0, pl.ds(n0, nb)],
                            gr_ref[0, pl.ds(n0, nb)], b_ref[0, pl.ds(n0, nb)])
            u_s[pl.ds(n0, nb)] = loc["u"]
            w_s[pl.ds(n0, nb)] = loc["w"]
            at_s[pl.ds(n0, nb)] = loc["qm"] * loc["decay"]
            qd_s[pl.ds(n0, nb)] = loc["qd"]
            kt_s[pl.ds(n0, nb)] = loc["kt"]
            gl_s[pl.ds(n0, nb)] = loc["gl"]
            return carry
        lax.fori_loop(0, n // nb, group, 0)

        def chunk(i, s):
            s_ref[0, i] = s
            vnew = u_s[i] - _dot(w_s[i], s)
            o = _dot(qd_s[i], s) + _dot(at_s[i], vnew)
            o_ref[pl.ds(pl.multiple_of(i * c, c), c), :] = o
            return s * gl_s[i] + _dot_tn(kt_s[i], vnew)
        lax.fori_loop(0, n, chunk, jnp.zeros((dk, dk), F32))

    scratch = [pltpu.VMEM((n, c, dk), F32), pltpu.VMEM((n, c, dk), F32), pltpu.VMEM((n, c, c), F32),
               pltpu.VMEM((n, c, dk), F32), pltpu.VMEM((n, c, dk), F32), pltpu.VMEM((n, 1, 1), F32)]
    return pl.pallas_call(
        body, grid=(heads,), in_specs=[hs, hs, hs, gcs, grs, gcs], out_specs=[hs, ss],
        out_shape=[jax.ShapeDtypeStruct((t, d), F32), jax.ShapeDtypeStruct((heads, n, dk, dk), F32)],
        scratch_shapes=scratch, name=name, compiler_params=_cparams(("parallel",)))(q, k, v, gc, gr, bc)


def _dn_chunk_bwd(q, k, v, gc, gr, bc, s_all, do, name):
    t, d = q.shape
    heads = d // DN_HEAD_DIM
    c, dk = DN_CHUNK, DN_HEAD_DIM
    n, hs, gcs, grs, ss = _dn_chunk_specs(t, heads)
    nb = min(DN_GROUP, n)
    scale = dk ** -0.5

    def body(q_ref, k_ref, v_ref, gc_ref, gr_ref, b_ref, s_ref, do_ref,
             dq_ref, dk_ref, dv_ref, dgc_ref, dgr_ref, db_ref,
             u_s, w_s, att_s, qd_s, kt_s, gl_s, du_s, dw_s, dat_s, dqd_s, dkt_s, dgl_s):
        def load_group(gi):
            r0 = pl.multiple_of(gi * (nb * c), nb * c)
            n0 = gi * nb
            ld = lambda ref: ref[pl.ds(r0, nb * c), :].reshape(nb, c, dk)
            qg, kg, vg = ld(q_ref) * scale, ld(k_ref), ld(v_ref)
            gcv, grv, bt = gc_ref[0, pl.ds(n0, nb)], gr_ref[0, pl.ds(n0, nb)], b_ref[0, pl.ds(n0, nb)]
            return r0, n0, qg, kg, vg, gcv, grv, bt, _dn_local(qg, kg, vg, gcv, grv, bt)

        def group_a(gi, carry):
            _, n0, qg, kg, _, gcv, grv, _, loc = load_group(gi)
            row = lax.broadcasted_iota(jnp.int32, (c, c), 0)
            col = lax.broadcasted_iota(jnp.int32, (c, c), 1)
            upper = (col >= row)[None]
            decay_t = jnp.where(upper, jnp.exp(jnp.where(upper, grv - gcv, 0.0)), 0.0)
            u_s[pl.ds(n0, nb)] = loc["u"]
            w_s[pl.ds(n0, nb)] = loc["w"]
            att_s[pl.ds(n0, nb)] = _bmm_nt(kg.astype(BF16), qg.astype(BF16)) * decay_t
            qd_s[pl.ds(n0, nb)] = loc["qd"]
            kt_s[pl.ds(n0, nb)] = loc["kt"]
            gl_s[pl.ds(n0, nb)] = loc["gl"]
            return carry
        lax.fori_loop(0, n // nb, group_a, 0)

        def chunk_b(it, ds_next):
            i = n - 1 - it
            s = s_ref[0, i]
            dov = do_ref[pl.ds(pl.multiple_of(i * c, c), c), :]
            w, kt = w_s[i], kt_s[i]
            vnew = u_s[i] - _dot(w, s)
            dvnew = _dot(att_s[i], dov) + _dot(kt, ds_next)
            du_s[i] = dvnew
            dw_s[i] = -_dot_nt(dvnew, s)
            dat_s[i] = _dot_nt(dov, vnew)
            dqd_s[i] = _dot_nt(dov, s)
            dkt_s[i] = _dot_nt(vnew, ds_next)
            dgl_s[i] = jnp.sum(jnp.sum(ds_next * s, axis=1, keepdims=True), axis=0, keepdims=True)
            return ds_next * gl_s[i] + _dot_tn(qd_s[i], dov) - _dot_tn(w, dvnew)
        lax.fori_loop(0, n, chunk_b, jnp.zeros((dk, dk), F32))

        def group_c(gi, carry):
            r0, n0, qg, kg, vg, gcv, grv, bt, loc = load_group(gi)
            sl = pl.ds(n0, nb)
            du, dw, dat, dqd, dkt, dgl = du_s[sl], dw_s[sl], dat_s[sl], dqd_s[sl], dkt_s[sl], dgl_s[sl]
            tm, decay, kb, kbg = loc["tm"], loc["decay"], loc["kb"], loc["kbg"]
            dvb = _bmm_tn(tm, du, X3)
            dkbg = _bmm_tn(tm, dw, X3)
            dt = _bmm_nt(du, loc["vb"], X3) + _bmm_nt(dw, kbg, X3)
            da = jnp.where(loc["strict"], -_bmm_tn(tm, _bmm_nt(dt, tm, X3), X3), 0.0)
            dms = (da * decay).astype(BF16)
            dqs = (dat * decay).astype(BF16)
            kgb = kg.astype(BF16)
            dkb = _bmm(dms, kgb) + dkbg * loc["eg"]
            dqt = _bmm(dqs, kgb) + dqd * loc["eg"]
            dkk = _bmm_tn(dms, kb.astype(BF16)) + _bmm_tn(dqs, qg.astype(BF16)) + dkt * loc["erel"] + dkb * bt
            e = (da * loc["m"] + dat * loc["qm"]) * decay
            lsum = lambda x: jnp.sum(x, axis=2, keepdims=True)
            dkt_kt = lsum(dkt * loc["kt"])
            dgcv = lsum(e) + lsum(dqd * loc["qd"]) - dkt_kt + lsum(dkbg * kbg)
            dglast = jnp.sum(dkt_kt, axis=1, keepdims=True) + dgl * loc["gl"]
            rowc = lax.broadcasted_iota(jnp.int32, (1, c, 1), 1)
            dgc_ref[0, sl] = dgcv + jnp.where(rowc == c - 1, dglast, 0.0)
            dgr_ref[0, sl] = -jnp.sum(e, axis=1, keepdims=True)
            db_ref[0, sl] = lsum(dkb * kg) + lsum(dvb * vg)
            rows = pl.ds(r0, nb * c)
            dq_ref[rows, :] = (dqt * scale).reshape(nb * c, dk)
            dk_ref[rows, :] = dkk.reshape(nb * c, dk)
            dv_ref[rows, :] = (dvb * bt).reshape(nb * c, dk)
            return carry
        lax.fori_loop(0, n // nb, group_c, 0)

    big = lambda: pltpu.VMEM((n, c, dk), F32)
    sq = lambda: pltpu.VMEM((n, c, c), F32)
    one = lambda: pltpu.VMEM((n, 1, 1), F32)
    scratch = [big(), big(), sq(), big(), big(), one(), big(), big(), sq(), big(), big(), one()]
    return pl.pallas_call(
        body, grid=(heads,), in_specs=[hs, hs, hs, gcs, grs, gcs, ss, hs], out_specs=[hs, hs, hs, gcs, grs, gcs],
        out_shape=[jax.ShapeDtypeStruct((t, d), F32)] * 3 + [
            jax.ShapeDtypeStruct((heads, n, c, 1), F32), jax.ShapeDtypeStruct((heads, n, 1, c), F32),
            jax.ShapeDtypeStruct((heads, n, c, 1), F32)],
        scratch_shapes=scratch, name=name, compiler_params=_cparams(("parallel",)))(q, k, v, gc, gr, bc, s_all, do)


def _dot01(x, m01):
    x1 = x.astype(BF16)
    r1 = x - x1.astype(F32)
    x2 = r1.astype(BF16)
    x3 = (r1 - x2.astype(F32)).astype(BF16)
    dot = lambda a: jnp.dot(a, m01, preferred_element_type=F32)
    return (dot(x1) + dot(x2)) + dot(x3)


def _sb_logs(z, mask):
    rows, bk = z.shape
    row = lax.broadcasted_iota(jnp.int32, (bk, bk), 0)
    col = lax.broadcasted_iota(jnp.int32, (bk, bk), 1)
    later = (row > col).astype(BF16)
    lm, ls, cs = [], [], []
    for p in [slice(p, p + SB_PART) for p in range(0, rows, SB_PART)]:
        lsm = -_softplus(z[p])
        lm.append(lsm if mask is None else jnp.where(mask[p], lsm, 0.0))
        ls.append(z[p] + lsm)
        cs.append(_dot01(lm[-1], later))
    cat = lambda xs: jnp.concatenate(xs, axis=0)
    return cat(ls), cat(cs), jnp.sum(cat(lm), axis=1, keepdims=True)


def _sb_masks(bq, bk):
    row = lax.broadcasted_iota(jnp.int32, (2 * bq, bk), 0)
    row = jnp.where(row >= bq, row - bq, row)
    col = lax.broadcasted_iota(jnp.int32, (2 * bq, bk), 1)
    return [(col + dd * bk) < row for dd in range(bq // bk)]


def _sb_sweep(bq, bk, qi, fetch, visit, carry):
    nd = bq // bk
    masks = _sb_masks(bq, bk)
    ahead = lambda j: fetch(jnp.maximum(j - 1, 0))
    pre = fetch(qi * nd + nd - 1)
    for dd in reversed(range(nd)):
        nxt = ahead(qi * nd + dd)
        carry = visit(qi * nd + dd, masks[dd], carry, pre)
        pre = nxt

    def below(it, c):
        j = qi * nd - 1 - it
        nxt = ahead(j)
        return visit(j, None, c[0], c[1]), nxt
    return lax.fori_loop(0, qi * nd, below, (carry, pre))[0]


def _sb_sweep2(bq, bk, qi, fetch, first, second, carry):
    nd = bq // bk
    masks = _sb_masks(bq, bk)
    ahead = lambda j: fetch(jnp.maximum(j - 1, 0))
    pre = fetch(qi * nd + nd - 1)
    r = jnp.zeros((2 * bq, 1), F32)
    pend = None
    for dd in reversed(range(nd)):
        j = qi * nd + dd
        nxt = ahead(j)
        ls, cs, rsum, extras = first(j, masks[dd], pre)
        if pend is not None:
            carry = second(pend[0], carry, *pend[1])
        pend = (j, (jnp.where(masks[dd], ls + cs + r, -1e30),) + extras)
        r, pre = r + rsum, nxt

    def below(it, c):
        carry, r, pre, pj, pargs = c
        j = qi * nd - 1 - it
        nxt = ahead(j)
        ls, cs, rsum, extras = first(j, None, pre)
        carry = second(pj, carry, *pargs)
        return carry, r + rsum, nxt, j, (ls + cs + r,) + extras
    carry, _, _, pj, pargs = lax.fori_loop(0, qi * nd, below, (carry, r, pre) + pend)
    return second(pj, carry, *pargs)


def _stack_heads(a, h0):
    return jnp.concatenate([jnp.where(h0, a, 0.0), jnp.where(h0, 0.0, a)], axis=0).astype(BF16)


def _side_by_side(a, bq):
    return jnp.concatenate([a[:bq], a[bq:]], axis=1)


def _sb_specs(t, d, bq, vcol):
    qs = pl.BlockSpec((bq, LANES), lambda g, i: (i, g))
    ks = pl.BlockSpec((t, LANES), lambda g, i: (0, g))
    vs = pl.BlockSpec((t, LANES), lambda g, i: (0, g + vcol))
    return qs, ks, vs


def _sb_fwd(qn, kn, qkv, name):
    t, d = qn.shape
    bq, bk = min(SB_QBLOCK, t), min(SB_BLOCK, t)
    scale = SB_HEAD_DIM ** -0.5
    nt = (((1,), (1,)), ((), ()))

    def body(q_ref, k_ref, v_ref, o_ref):
        qi = pl.program_id(1)
        h0 = lax.broadcasted_iota(jnp.int32, (1, LANES), 1) < SB_HEAD_DIM
        q2 = _stack_heads(q_ref[...] * scale, h0)

        tile_rows = lambda j: pl.ds(pl.multiple_of(j * bk, bk), bk)

        def fetch(j):
            return lax.dot_general(q2, k_ref[tile_rows(j), :].astype(BF16), nt, preferred_element_type=F32)

        def first(j, mask, z):
            return _sb_logs(z, mask) + ((),)

        def second(j, acc, loga):
            a2 = _side_by_side(jnp.exp(loga).astype(BF16), bq)
            return acc + jnp.dot(a2, _stack_heads(v_ref[tile_rows(j), :], h0), preferred_element_type=F32)
        acc = _sb_sweep2(bq, bk, qi, fetch, first, second, jnp.zeros((bq, LANES), F32))
        o_ref[...] = acc.astype(o_ref.dtype)

    qs, ks, vs = _sb_specs(t, d, bq, 2 * d // LANES)
    return pl.pallas_call(
        body, grid=(d // LANES, t // bq), in_specs=[qs, ks, vs], out_specs=qs,
        out_shape=jax.ShapeDtypeStruct((t, d), BF16), name=name,
        compiler_params=_cparams(("parallel", "parallel")))(qn, kn, qkv)


def _sb_bwd(qn, kn, qkv, do, name):
    t, d = qn.shape
    bq, bk = min(SB_QBLOCK, t), min(SB_BLOCK, t)
    scale = SB_HEAD_DIM ** -0.5
    nt = (((1,), (1,)), ((), ()))

    def body(q_ref, k_ref, v_ref, do_ref, dq_ref, dk_ref, dv_ref, p_s, sg_s):
        qi = pl.program_id(1)

        @pl.when(qi == 0)
        def _():
            dk_ref[...] = jnp.zeros(dk_ref.shape, F32)
            dv_ref[...] = jnp.zeros(dv_ref.shape, F32)

        h0 = lax.broadcasted_iota(jnp.int32, (1, LANES), 1) < SB_HEAD_DIM
        q2 = _stack_heads(q_ref[...] * scale, h0)
        do2 = _stack_heads(do_ref[...], h0)
        row = lax.broadcasted_iota(jnp.int32, (bk, bk), 0)
        col = lax.broadcasted_iota(jnp.int32, (bk, bk), 1)
        later_incl = (row >= col).astype(BF16)
        zero = jnp.zeros((2 * bq, 1), F32)
        tn = (((0,), (0,)), ((), ()))

        tile_rows = lambda j: pl.ds(pl.multiple_of(j * bk, bk), bk)

        def fetch1(j):
            return lax.dot_general(q2, k_ref[tile_rows(j), :].astype(BF16), nt, preferred_element_type=F32)

        def first1(j, mask, z):
            da = lax.dot_general(do2, v_ref[tile_rows(j), :].astype(BF16), nt, preferred_element_type=F32)
            ls, cs, rsum = _sb_logs(z, mask)
            sg_s[j] = jnp.exp(ls)
            return ls, cs, rsum, (da,)

        def second1(j, sp, loga, da):
            a = jnp.exp(loga)
            p = a * da
            p_s[j] = p
            dv_ref[tile_rows(j), :] += lax.dot_general(a.astype(BF16), do2, tn, preferred_element_type=F32)
            return sp + jnp.sum(p, axis=1, keepdims=True)
        total = _sb_sweep2(bq, bk, qi, fetch1, first1, second1, zero)

        def fetch2(j):
            return _dot01(p_s[j], later_incl)

        def visit2(j, mask, carry, later_p):
            dq, sp = carry
            rows = tile_rows(j)
            p, sg = p_s[j], sg_s[j]
            pref = total - sp - later_p
            dz = p * (1.0 - sg) - pref * sg
            if mask is not None:
                dz = jnp.where(mask, dz, 0.0)
            dz = dz.astype(BF16)
            dk_ref[rows, :] += lax.dot_general(dz, q2, tn, preferred_element_type=F32)
            dq = dq + jnp.dot(_side_by_side(dz, bq), _stack_heads(k_ref[rows, :], h0), preferred_element_type=F32)
            return dq, sp + jnp.sum(p, axis=1, keepdims=True)
        dq, _ = _sb_sweep(bq, bk, qi, fetch2, visit2, (jnp.zeros((bq, LANES), F32), zero))
        dq_ref[...] = dq * scale

    qs, ks, vs = _sb_specs(t, d, bq, 2 * d // LANES)
    shp = jax.ShapeDtypeStruct((t, d), F32)
    scratch = [pltpu.VMEM((t // bk, 2 * bq, bk), F32), pltpu.VMEM((t // bk, 2 * bq, bk), F32)]
    return pl.pallas_call(
        body, grid=(d // LANES, t // bq), in_specs=[qs, ks, vs, qs], out_specs=[qs, ks, ks],
        out_shape=[shp, shp, shp], scratch_shapes=scratch, name=name,
        compiler_params=_cparams(("parallel", "arbitrary")))(qn, kn, qkv, do)


def _adamw(w, g, m, v, name):
    shape = w.shape
    two_d = lambda a: a.reshape(-1, shape[-1])
    c1 = 1.0 - ADAM_B1 ** ADAM_STEP
    c2 = 1.0 - ADAM_B2 ** ADAM_STEP

    def body(r, _):
        wv, gv, mv, vv = r
        mn = ADAM_B1 * mv + (1.0 - ADAM_B1) * gv
        vn = ADAM_B2 * vv + (1.0 - ADAM_B2) * (gv * gv)
        delta = -ADAM_LR * ((mn / c1) / (jnp.sqrt(vn / c2) + ADAM_EPS) + ADAM_WD * wv)
        return [delta, mn, vn], []
    width = shape[-1]
    res = _rowwise(body, [two_d(w), two_d(g), two_d(m), two_d(v)], [], [(width, F32)] * 3, [], name=name)
    return [r.reshape(shape) for r in res]


def _dn_layer_fwd(h1, w, conv_w, a_log, dt_bias, onorm_g, tag):
    t, d = h1.shape
    heads = d // DN_HEAD_DIM
    n = t // DN_CHUNK
    proj = _mm(h1, w["all"], name=tag + "_proj")
    qn, kn, vv = _dn_prep_fwd(proj, conv_w, heads, tag + "_prep")
    big_g, beta = _dn_gate_fwd(proj, 4 * d // (2 * LANES), a_log, dt_bias, tag + "_gate")
    gt_ = big_g[:, :heads].T.reshape(heads, n, DN_CHUNK)
    gc, gr = gt_[..., None], gt_[:, :, None, :]
    bc = beta[:, :heads].T.reshape(heads, n, DN_CHUNK)[..., None]
    o, s_all = _dn_chunk_fwd(qn, kn, vv, gc, gr, bc, tag + "_chunk")
    on = _dn_post_fwd(o, proj, onorm_g, tag + "_post")
    y = _mm(on, w["out"], name=tag + "_out")
    return y, dict(proj=proj, qn=qn, kn=kn, v=vv, gc=gc, gr=gr, bc=bc, o=o, s_all=s_all, on=on)


def _dn_layer_bwd(dy, h1, w, conv_w, a_log, dt_bias, onorm_g, sv, tag):
    t, d = h1.shape
    heads = d // DN_HEAD_DIM
    don = _mm(dy, w["out"], tb=True, name=tag + "_dout")
    g_out = _mm(sv["on"], dy, ta=True, out_dtype=BF16, name=tag + "_gwout")
    do, dz, g_on = _dn_post_bwd(sv["o"], sv["proj"], don, onorm_g, tag + "_dpost")
    dq, dk, dv, dgc, dgr, dbc = _dn_chunk_bwd(sv["qn"], sv["kn"], sv["v"], sv["gc"], sv["gr"], sv["bc"], sv["s_all"],
                                              do, tag + "_dchunk")
    pad = lambda a: jnp.pad(a.reshape(heads, t).T, ((0, 0), (0, LANES - heads)))
    d_big_g = pad(dgc) + pad(dgr)
    dab, g_alog, g_dt = _dn_gate_bwd(sv["proj"], 4 * d // (2 * LANES), d_big_g, pad(dbc), a_log, dt_bias, tag + "_dgate")
    dxq, dxk, dxv, wq, wk, wv = _dn_prep_bwd(sv["proj"], conv_w, dq, dk, dv, heads, tag + "_dprep")
    dproj = jnp.concatenate([dxq, dxk, dxv, dz, dab], axis=1)
    dh1 = _mm(dproj, w["all"], tb=True, name=tag + "_dh")
    g_all = _mm(h1, dproj, ta=True, out_dtype=BF16, name=tag + "_gwin")
    grads = dict(w_all=g_all, w_out=g_out, conv_w=jnp.concatenate([wq, wk, wv], axis=1), a_log=g_alog, dt_bias=g_dt,
                 onorm_g=g_on)
    return dh1, grads


def _sb_layer_fwd(h1, w, q_g, k_g, tag):
    t, d = h1.shape
    heads = d // SB_HEAD_DIM
    qkv = _mm(h1, w["qkv"], name=tag + "_proj")
    gq, gk = jnp.tile(q_g, (1, heads)), jnp.tile(k_g, (1, heads))
    qn, kn = _sb_norm_fwd(qkv, gq, gk, tag + "_norm")
    o = _sb_fwd(qn, kn, qkv, tag + "_attn")
    y = _mm(o, w["out"], name=tag + "_out")
    return y, dict(qkv=qkv, qn=qn, kn=kn, o=o, gq=gq, gk=gk)


def _sb_layer_bwd(dy, h1, w, q_g, k_g, sv, tag):
    t, d = h1.shape
    heads = d // SB_HEAD_DIM
    do = _mm(dy, w["out"], tb=True, name=tag + "_dout")
    g_out = _mm(sv["o"], dy, ta=True, out_dtype=BF16, name=tag + "_gwout")
    dqn, dkn, dv = _sb_bwd(sv["qn"], sv["kn"], sv["qkv"], do, tag + "_dattn")
    dqkv, g_q, g_k = _sb_norm_bwd(sv["qkv"], dqn, dkn, dv, sv["gq"], sv["gk"], tag + "_dnorm")
    fold = lambda g: jnp.sum(g.reshape(heads, SB_HEAD_DIM), axis=0, keepdims=True)
    dh1 = _mm(dqkv, w["qkv"], tb=True, name=tag + "_dh")
    g_qkv = _mm(h1, dqkv, ta=True, out_dtype=BF16, name=tag + "_gwin")
    return dh1, dict(w_qkv=g_qkv, w_out=g_out, q_norm_g=fold(g_q), k_norm_g=fold(g_k))


def _local_step(x, tgt, mod, norm1_g, norm2_g, layer_weights, layer_grads):
    depth = mod.shape[0]
    d = x.shape[1]
    saved = []
    for i in range(depth):
        mix_w, ffn_of = layer_weights(i, x)
        mv = [mod[i:i + 1, j * d:(j + 1) * d] for j in range(N_MOD)]
        sh1, sc1, gt1, sh2, sc2, gt2 = mv
        tag = "l%d" % i
        h1 = _adaln_fwd(x, norm1_g[i:i + 1], sc1, sh1, tag + "_ln1")
        if i % 2 == 0:
            p = mix_w
            y, sv = _dn_layer_fwd(h1, p, p["conv_w"], p["a_log"], p["dt_bias"], p["onorm_g"], tag + "_dn")
        else:
            p = mix_w
            y, sv = _sb_layer_fwd(h1, p, p["q_g"], p["k_g"], tag + "_sb")
        x1 = _resid_fwd(x, y, gt1, tag + "_res1")
        h2 = _adaln_fwd(x1, norm2_g[i:i + 1], sc2, sh2, tag + "_ln2")
        ffn_w = ffn_of(x1)
        u = _mm(h2, ffn_w["w_in"], name=tag + "_ffn_in")
        a = _swiglu_fwd(u, tag + "_swiglu")
        y2 = _mm(a, ffn_w["w_out"], name=tag + "_ffn_out")
        x2 = _resid_fwd(x1, y2, gt2, tag + "_res2")
        saved.append(dict(x0=x, h1=h1, y=y, mix=sv, x1=x1, h2=h2, u=u, a=a, y2=y2, mix_w=mix_w, ffn_w=ffn_w))
        x = x2

    loss, dx = _loss_fwd_bwd(x, tgt, "loss")

    dmod, dn1, dn2 = [None] * depth, [None] * depth, [None] * depth
    zero = jnp.zeros((), F32)
    for i in reversed(range(depth)):
        s = saved[i]
        mv = [mod[i:i + 1, j * d:(j + 1) * d] + zero for j in range(N_MOD)]
        sh1, sc1, gt1, sh2, sc2, gt2 = mv
        tag = "l%d" % i
        ffn_w, p = s["ffn_w"], s["mix_w"]
        dy2, dgt2 = _resid_bwd(dx, s["y2"], gt2, tag + "_dres2")
        da = _mm(dy2, ffn_w["w_out"], tb=True, name=tag + "_dffn_a")
        g_wout = _mm(s["a"], dy2, ta=True, out_dtype=BF16, name=tag + "_gffn_out")
        du = _swiglu_bwd(s["u"], da, tag + "_dswiglu")
        dh2 = _mm(du, ffn_w["w_in"], tb=True, name=tag + "_dffn_h")
        g_win = _mm(s["h2"], du, ta=True, out_dtype=BF16, name=tag + "_gffn_in")
        dx, dg2, dsc2, dsh2 = _adaln_bwd(s["x1"], dh2, dx, norm2_g[i:i + 1], sc2, tag + "_dln2")
        gt1 = gt1 + layer_grads(i, "ffn", dx, dict(w_in=g_win, w_out=g_wout))
        dy, dgt1 = _resid_bwd(dx, s["y"], gt1, tag + "_dres1")
        if i % 2 == 0:
            dh1, g_mix = _dn_layer_bwd(dy, s["h1"], p, p["conv_w"], p["a_log"], p["dt_bias"], p["onorm_g"], s["mix"],
                                       tag + "_dn")
        else:
            dh1, g_mix = _sb_layer_bwd(dy, s["h1"], p, p["q_g"], p["k_g"], s["mix"], tag + "_sb")
        dx, dg1, dsc1, dsh1 = _adaln_bwd(s["x0"], dh1, dx, norm1_g[i:i + 1], sc1, tag + "_dln1")
        dmod[i] = jnp.concatenate([dsh1, dsc1, dgt1, dsh2, dsc2, dgt2], axis=1)
        dn1[i], dn2[i] = dg1, dg2
        zero = layer_grads(i, "mix", dx, g_mix)
    return loss, dx, jnp.concatenate(dmod, axis=0), jnp.concatenate(dn1, axis=0), jnp.concatenate(dn2, axis=0)


def _axes():
    return lax.axis_index("x"), lax.axis_index("y"), lax.axis_index("c")


def _remote(src, dst, send_sem, recv_sem, dev):
    return pltpu.make_async_remote_copy(src_ref=src, dst_ref=dst, send_sem=send_sem, recv_sem=recv_sem,
                                        device_id=dev, device_id_type=MESH)


def _other_chips(x, y):
    return [(1 - x, y), (x, 1 - y), (1 - x, 1 - y)]


def _allgather_small(v, name):
    m, n = v.shape

    def body(x_ref, out_ref, send_sems, recv_sems, local_sem):
        x, y, c = _axes()
        me, sibling = (x, y, c), (x, y, 1 - c)
        chips = _other_chips(x, y)

        def rows(px, py, pc):
            return out_ref.at[pl.ds((4 * px + 2 * py + pc) * m, m), :]

        def copy(k, block, to, src=None):
            return _remote(rows(*block) if src is None else src, rows(*block), send_sems.at[k], recv_sems.at[k], to)

        mine = pltpu.make_async_copy(x_ref, rows(*me), local_sem)
        mine.start()
        first = [copy(0, me, sibling, src=x_ref)]
        first += [copy(1 + j, me, (*chip, c), src=x_ref) for j, chip in enumerate(chips)]
        for cp in first:
            cp.start()
        passed = [copy(4 + j, (*chip, c), sibling) for j, chip in enumerate(chips)]
        for j, chip in enumerate(chips):
            copy(1 + j, (*chip, c), me).wait_recv()
            passed[j].start()
        copy(0, sibling, me).wait_recv()
        for j, chip in enumerate(chips):
            copy(4 + j, (*chip, 1 - c), me).wait_recv()
        for cp in first + passed:
            cp.wait_send()
        mine.wait()

    return pl.pallas_call(
        body, out_shape=jax.ShapeDtypeStruct((8 * m, n), v.dtype),
        in_specs=[pl.BlockSpec(memory_space=pltpu.VMEM)], out_specs=pl.BlockSpec(memory_space=pltpu.VMEM),
        scratch_shapes=[pltpu.SemaphoreType.DMA((7,)), pltpu.SemaphoreType.DMA((7,)), pltpu.SemaphoreType.DMA],
        name=name, compiler_params=pltpu.CompilerParams(vmem_limit_bytes=VMEM_LIMIT))(v)


def _half(ref, h, rh):
    return ref.at[(slice(None),) * (len(ref.shape) - 2) + (pl.ds(h * rh, rh), slice(None))]


def _hbm_call(body, ins, out_shapes, n_sems, n_local, name):
    hbm = pl.BlockSpec(memory_space=pltpu.HBM)
    scratch = [pltpu.SemaphoreType.DMA((n_sems,)), pltpu.SemaphoreType.DMA((n_sems,))]
    if n_local:
        scratch.append(pltpu.SemaphoreType.DMA((n_local,)))
    return pl.pallas_call(body, out_shape=out_shapes, in_specs=[hbm] * len(ins), out_specs=[hbm] * len(out_shapes),
                          scratch_shapes=scratch, name=name)(*ins)


def _sibling_swap_halves(arrs, name):
    n = len(arrs)

    def body(*refs):
        v_refs, out_refs, (send_sems, recv_sems) = refs[:n], refs[n:2 * n], refs[2 * n:]
        x, y, c = _axes()
        cps = [_remote(_half(v_refs[i], 1 - c, arrs[i].shape[-2] // 2), out_refs[i], send_sems.at[i], recv_sems.at[i],
                       (x, y, 1 - c)) for i in range(n)]
        for cp in cps:
            cp.start()
        for cp in cps:
            cp.wait()

    outs = [jax.ShapeDtypeStruct(a.shape[:-2] + (a.shape[-2] // 2, a.shape[-1]), a.dtype) for a in arrs]
    return _hbm_call(body, arrs, outs, n, 0, name)


def _ici_src(ref, mode, slot, c):
    return _half(ref, c, ref.shape[-2] // 2) if mode == "half" else ref.at[slot]


def _ici_start(srcs, mode, name):
    n, ncp = len(srcs), 3 * len(srcs)
    lands = [jnp.zeros(((4,) + s.shape[:-2] + (s.shape[-2] // 2, s.shape[-1])) if mode == "half" else s.shape, s.dtype)
             for s in srcs]

    def body(*refs):
        src_refs, land_refs = refs[:n], refs[n:2 * n]
        send_sems, recv_sems = refs[2 * n:2 * n + ncp], refs[2 * n + ncp:2 * n + 2 * ncp]
        token = refs[-1]
        x, y, c = _axes()
        for i in range(n):
            for j, (cx, cy) in enumerate(_other_chips(x, y)):
                _remote(_ici_src(src_refs[i], mode, 2 * cx + cy, c), land_refs[i].at[2 * x + y], send_sems[3 * i + j],
                        recv_sems[3 * i + j], (cx, cy, c)).start()
        token[...] = jnp.zeros(token.shape, token.dtype)

    hbm, sem = pl.BlockSpec(memory_space=pltpu.HBM), pl.BlockSpec(memory_space=pltpu.SEMAPHORE)
    bufs = srcs + lands
    out_shape = tuple([pltpu.SemaphoreType.DMA(())] * (2 * ncp) + [pltpu.HBM(b.shape, b.dtype) for b in bufs]
                      + [jax.ShapeDtypeStruct((8, LANES), F32)])
    res = pl.pallas_call(
        body, name=name, out_shape=out_shape, in_specs=(hbm,) * (2 * n),
        out_specs=(sem,) * (2 * ncp) + (hbm,) * (2 * n) + (pl.BlockSpec(memory_space=pltpu.VMEM),),
        input_output_aliases={i: 2 * ncp + i for i in range(2 * n)},
        compiler_params=pltpu.CompilerParams(has_side_effects=pltpu.SideEffectType.DATAFLOW_SIDE_EFFECTING),
    )(*[pltpu.with_memory_space_constraint(b, pltpu.HBM) for b in bufs])
    return list(res[:2 * ncp]), list(res[2 * ncp:2 * ncp + 2 * n]), res[-1][0, 0]


def _ici_wait(sems, bufs, mode, after, name):
    n, ncp = len(bufs) // 2, len(sems) // 2

    def body(*refs):
        src_refs, land_refs = refs[:n], refs[n:2 * n]
        send_sems, recv_sems = refs[2 * n:2 * n + ncp], refs[2 * n + ncp:2 * n + 2 * ncp]
        x, y, c = _axes()
        for i in range(n):
            for j, (cx, cy) in enumerate(_other_chips(x, y)):
                cp = _remote(_ici_src(src_refs[i], mode, 2 * cx + cy, c), land_refs[i].at[2 * cx + cy],
                             send_sems[3 * i + j], recv_sems[3 * i + j], (cx, cy, c))
                cp.wait_send()
                cp.wait_recv()

    hbm, sem = pl.BlockSpec(memory_space=pltpu.HBM), pl.BlockSpec(memory_space=pltpu.SEMAPHORE)
    res = pl.pallas_call(
        body, name=name, out_shape=tuple(pltpu.HBM(b.shape, b.dtype) for b in bufs),
        in_specs=(hbm,) * (2 * n) + (sem,) * (2 * ncp) + (pl.BlockSpec(memory_space=pl.ANY),),
        out_specs=(hbm,) * (2 * n), input_output_aliases={i: i for i in range(2 * n)},
        compiler_params=pltpu.CompilerParams(has_side_effects=pltpu.SideEffectType.DATAFLOW_SIDE_EFFECTING),
    )(*bufs, *sems, after)
    return list(res[n:])


def _share_halves(arrs, name):
    n = len(arrs)

    def body(*refs):
        v_refs, out_refs, (send_sems, recv_sems) = refs[:n], refs[n:2 * n], refs[2 * n:]
        x, y, c = _axes()
        cps = [_remote(v_refs[i], out_refs[i], send_sems.at[i], recv_sems.at[i], (x, y, 1 - c)) for i in range(n)]
        for cp in cps:
            cp.start()
        for cp in cps:
            cp.wait()

    outs = [jax.ShapeDtypeStruct(a.shape, a.dtype) for a in arrs]
    return _hbm_call(body, arrs, outs, n, 0, name)


def _assemble_weights(mine, other, own, axis, name):
    nchips, l, rh, cs = mine.shape
    r = 2 * rh

    def body(m_ref, o_ref, w_ref, out_ref):
        x, y, c = _axes()
        is_own = pl.program_id(1) == 2 * x + y

        @pl.when(is_own)
        def _():
            out_ref[...] = w_ref[...]

        @pl.when(jnp.logical_not(is_own))
        def _():
            out_ref[pl.ds(pl.multiple_of(c * rh, rh), rh), :] = m_ref[...]
            out_ref[pl.ds(pl.multiple_of((1 - c) * rh, rh), rh), :] = o_ref[...]

    half = pl.BlockSpec((None, None, rh, cs), lambda i, s: (s, i, 0, 0))
    whole = pl.BlockSpec((None, r, cs), lambda i, s: (i, 0, 0))
    if axis == 2:
        out_spec, shape = pl.BlockSpec((None, r, cs), lambda i, s: (i, 0, s)), (l, r, nchips * cs)
    else:
        out_spec, shape = pl.BlockSpec((None, r, cs), lambda i, s: (i, s, 0)), (l, nchips * r, cs)
    return pl.pallas_call(
        body, grid=(l, nchips), in_specs=[half, half, whole], out_specs=out_spec,
        out_shape=jax.ShapeDtypeStruct(shape, mine.dtype), name=name,
        compiler_params=_cparams(("parallel", "arbitrary")))(mine, other, own)


def _sum_chips(land, own, name):
    k, r, w = land.shape
    tr = _pick_tile(r, (256, 128, 64, 32, 16))

    def body(land_ref, own_ref, o_ref):
        x, y, _ = _axes()
        me = 2 * x + y
        acc = jnp.zeros((tr, w), F32)
        for s in range(k):
            acc = acc + jnp.where(me == s, own_ref[s], land_ref[s]).astype(F32)
        o_ref[...] = acc

    spec = pl.BlockSpec((k, tr, w), lambda i: (0, i, 0))
    return pl.pallas_call(
        body, grid=(r // tr,), in_specs=[spec, spec],
        out_specs=pl.BlockSpec((tr, w), lambda i: (i, 0)), out_shape=jax.ShapeDtypeStruct((r, w), F32), name=name,
        compiler_params=_cparams(("parallel",)))(land, own)


_BIG = (("dn_w_in", 2), ("dn_w_out", 1), ("sb_w_qkv", 2), ("sb_w_out", 1), ("ffn_w_in", 2), ("ffn_w_out", 1))


def _to_chip_major(a, axis, nchips):
    l, r, c = a.shape
    if axis == 2:
        return a.reshape(l, r, nchips, c // nchips).transpose(2, 0, 1, 3)
    return a.reshape(l, nchips, r // nchips, c).transpose(1, 0, 2, 3)


def _from_chip_major(a, axis):
    n, l, r, c = a.shape
    if axis == 2:
        return a.transpose(1, 2, 0, 3).reshape(l, r, n * c)
    return a.transpose(1, 0, 2, 3).reshape(l, n * r, c)


def kernel(x, c, ada_w, ada_b, norm1_g, norm2_g, dn_w_in, dn_conv_w, dn_a_log, dn_dt_bias, dn_onorm_g, dn_w_out, sb_w_qkv, sb_q_norm_g, sb_k_norm_g, sb_w_out, ffn_w_in, ffn_w_out, loss_target, m_ada_w, m_ada_b, m_norm1_g, m_norm2_g, m_dn_w_in, m_dn_conv_w, m_dn_a_log, m_dn_dt_bias, m_dn_onorm_g, m_dn_w_out, m_sb_w_qkv, m_sb_q_norm_g, m_sb_k_norm_g, m_sb_w_out, m_ffn_w_in, m_ffn_w_out, v_ada_w, v_ada_b, v_norm1_g, v_norm2_g, v_dn_w_in, v_dn_conv_w, v_dn_a_log, v_dn_dt_bias, v_dn_onorm_g, v_dn_w_out, v_sb_w_qkv, v_sb_q_norm_g, v_sb_k_norm_g, v_sb_w_out, v_ffn_w_in, v_ffn_w_out):
    names = ("ada_w", "ada_b", "norm1_g", "norm2_g", "dn_w_in", "dn_conv_w", "dn_a_log", "dn_dt_bias", "dn_onorm_g",
             "dn_w_out", "sb_w_qkv", "sb_q_norm_g", "sb_k_norm_g", "sb_w_out", "ffn_w_in", "ffn_w_out")
    w = dict(zip(names, (ada_w, ada_b, norm1_g, norm2_g, dn_w_in, dn_conv_w, dn_a_log, dn_dt_bias, dn_onorm_g,
                         dn_w_out, sb_w_qkv, sb_q_norm_g, sb_k_norm_g, sb_w_out, ffn_w_in, ffn_w_out)))
    mom = dict(zip(names, (m_ada_w, m_ada_b, m_norm1_g, m_norm2_g, m_dn_w_in, m_dn_conv_w, m_dn_a_log, m_dn_dt_bias,
                           m_dn_onorm_g, m_dn_w_out, m_sb_w_qkv, m_sb_q_norm_g, m_sb_k_norm_g, m_sb_w_out, m_ffn_w_in,
                           m_ffn_w_out)))
    var = dict(zip(names, (v_ada_w, v_ada_b, v_norm1_g, v_norm2_g, v_dn_w_in, v_dn_conv_w, v_dn_a_log, v_dn_dt_bias,
                           v_dn_onorm_g, v_dn_w_out, v_sb_w_qkv, v_sb_q_norm_g, v_sb_k_norm_g, v_sb_w_out, v_ffn_w_in,
                           v_ffn_w_out)))
    ax, ay, ac = _axes()
    chip = 2 * ax + ay
    dev = 2 * chip + ac
    t, d = x.shape[1], x.shape[2]
    depth, ndn, nsb = ada_w.shape[0], dn_w_in.shape[0], sb_w_qkv.shape[0]
    heads = d // DN_HEAD_DIM
    mod_cols = ada_w.shape[2]
    conv_cols = dn_conv_w.shape[2]
    nchips, ndev = 4, 8

    conv_rows = ndn * DN_CONV * conv_cols // d
    pay1 = jnp.concatenate([c, dn_conv_w.reshape(conv_rows, d), jnp.zeros((8 - 1 - conv_rows, d), F32)], axis=0)
    g1 = _allgather_small(pay1, "ag_cond").reshape(ndev, 8, d)
    c_all = g1[:, 0]
    conv_full = g1[::2, 1:1 + conv_rows].reshape(nchips, ndn, DN_CONV, conv_cols).transpose(1, 2, 0, 3)
    conv_full = conv_full.reshape(ndn, DN_CONV, nchips * conv_cols)

    c16 = jnp.pad(c_all, ((0, 16 - ndev), (0, 0)))
    cond16 = _rowwise(lambda r, v: ([_silu(r[0])], []), [c16], [], [(d, F32)], [], name="cond_silu")[0]
    pay2 = jnp.concatenate([_mm(cond16, ada_w[i], name="ada_mod%d" % i)[:ndev] for i in range(depth)], axis=0)
    g2 = _allgather_small(pay2, "ag_mod").reshape(ndev, depth, ndev, mod_cols)[::2]
    mod_raw = lax.dynamic_index_in_dim(g2, dev, axis=2, keepdims=False)
    mod_raw = mod_raw.transpose(1, 0, 2).reshape(depth, nchips * mod_cols)
    mod = _rowwise(lambda r, v: ([r[0] + r[1]], []), [mod_raw, ada_b], [], [(nchips * mod_cols, F32)], [],
                   name="ada_bias")[0]

    axis_of = dict(_BIG)

    def kinds_of(layers):
        out = []
        for n, _ in _BIG:
            idx = [l // 2 for l in layers if l % 2 == (0 if n.startswith("dn_") else 1)] if n[:3] in ("dn_", "sb_") \
                else list(layers)
            if idx:
                out.append((n, min(idx), max(idx) + 1))
        return out

    two_d = lambda a: a.reshape(-1, a.shape[-1])
    both_halves = lambda mine, other, ax: jnp.where(ac == 0, jnp.concatenate([mine, other], axis=ax),
                                                    jnp.concatenate([other, mine], axis=ax))
    is_ffn = lambda k: k[0].startswith("ffn")
    rest = (kinds_of(list(range(1, depth))), ("mix", 1))
    gather_groups = [g for g in [(kinds_of([0]), ("mix", 0)), rest] if g[0]]
    reduce_groups = [g for g in [([k for k in kinds_of([0]) if not is_ffn(k)], ("mix", 0)),
                                 ([k for k in kinds_of([0]) if is_ffn(k)], ("ffn", 0)), rest] if g[0]]
    w16 = {n: w[n].astype(BF16) for n, _ in _BIG}
    gathers = []
    for gi, (sl, _) in enumerate(gather_groups):
        shards = [w16[n][lo:hi] for n, lo, hi in sl]
        sems, bufs, zero = _ici_start(shards, "half", "ag_start%d" % gi)
        gathers.append((sl, shards, sems, bufs))
        mod = mod + zero
    full = {}

    def finish_gather(gi, after):
        sl, shards, sems, bufs = gathers[gi]
        lands = _ici_wait(sems, bufs, "half", after, "ag_wait%d" % gi)
        for (n, lo, hi), shard, mine, other in zip(sl, shards, lands, _share_halves(lands, "ag_pair%d" % gi)):
            if shard.shape[-1] % LANES == 0:
                a = _assemble_weights(mine, other, shard, axis_of[n], "ag_whole%d_%s" % (gi, n))
            else:
                a = lax.dynamic_update_index_in_dim(both_halves(mine, other, 2), shard, chip, 0)
                a = _from_chip_major(a, axis_of[n])
            for l in range(lo, hi):
                full[n, l] = a[l - lo]

    padl = lambda v: jnp.pad(v[None, :], ((0, 0), (0, LANES - v.shape[0])))
    padc = lambda a: jnp.pad(a, ((0, 0), (0, LANES - a.shape[1])))

    def layer_weights(i, after):
        for gi, (_, trigger) in enumerate(gather_groups):
            if trigger == ("mix", i):
                finish_gather(gi, after)

        def ffn_of(after_mixer):
            for gi, (_, trigger) in enumerate(gather_groups):
                if trigger == ("ffn", i):
                    finish_gather(gi, after_mixer)
            return dict(w_in=full["ffn_w_in", i], w_out=full["ffn_w_out", i])
        j = i // 2
        if i % 2 == 0:
            wi = full["dn_w_in", j]
            w_all = jnp.concatenate([wi[:, :4 * d], padc(wi[:, 4 * d:4 * d + heads]), padc(wi[:, 4 * d + heads:])],
                                    axis=1)
            mix = dict(all=w_all, out=full["dn_w_out", j], conv_w=conv_full[j], a_log=padl(dn_a_log[j]),
                       dt_bias=padl(dn_dt_bias[j]), onorm_g=dn_onorm_g[j][None])
        else:
            mix = dict(qkv=full["sb_w_qkv", j], out=full["sb_w_out", j], q_g=sb_q_norm_g[j][None],
                       k_g=sb_k_norm_g[j][None])
        return mix, ffn_of

    g_dn, g_sb, g_ffn = [None] * ndn, [None] * nsb, [None] * depth
    reduces = {}

    def gw_in(g):
        ga = g["w_all"]
        return jnp.concatenate([ga[:, :4 * d], ga[:, 4 * d:4 * d + heads], ga[:, 4 * d + LANES:4 * d + LANES + heads]],
                               axis=1)
    layer_grad = dict(dn_w_in=lambda j: gw_in(g_dn[j]), dn_w_out=lambda j: g_dn[j]["w_out"],
                      sb_w_qkv=lambda j: g_sb[j]["w_qkv"], sb_w_out=lambda j: g_sb[j]["w_out"],
                      ffn_w_in=lambda l: g_ffn[l]["w_in"], ffn_w_out=lambda l: g_ffn[l]["w_out"])

    def layer_grads(i, part, after, g):
        if part == "ffn":
            g_ffn[i] = g
        else:
            (g_dn if i % 2 == 0 else g_sb)[i // 2] = g
        zero = jnp.zeros((), F32)
        for gi, (sl, trigger) in enumerate(reduce_groups):
            if trigger == (part, i):
                parts = [_to_chip_major(jnp.stack([layer_grad[n](j) for j in range(lo, hi)]), axis_of[n], nchips)
                         for n, lo, hi in sl]
                pairs = []
                for (n, _, _), p, fs in zip(sl, parts, _sibling_swap_halves(parts, "gr_pair%d" % gi)):
                    rh = fs.shape[2]
                    own = lax.dynamic_slice_in_dim(p, ac * rh, rh, axis=2)
                    pairs.append(_rowwise(lambda r, v: ([r[0].astype(F32) + r[1].astype(F32)], []),
                                          [two_d(own), two_d(fs)], [], [(fs.shape[-1], BF16)], [],
                                          name="gr_pair_add%d_%s" % (gi, n))[0].reshape(fs.shape))
                sems, bufs, zero = _ici_start(pairs, "slot", "gr_start%d" % gi)
                reduces[gi] = (sl, pairs, sems, bufs)
        return zero

    loss_local, grad_x, dmod, g_n1, g_n2 = _local_step(x[0], loss_target[0], mod, norm1_g, norm2_g, layer_weights,
                                                       layer_grads)
    loss = lax.psum(loss_local, ("x", "y", "c"))

    g_conv = jnp.stack([g["conv_w"] for g in g_dn])
    misc = jnp.concatenate([jnp.concatenate([g["onorm_g"] for g in g_dn], axis=1),
                            jnp.concatenate([g["a_log"] for g in g_dn], axis=1),
                            jnp.concatenate([g["dt_bias"] for g in g_dn], axis=1),
                            jnp.concatenate([g["q_norm_g"] for g in g_sb], axis=1),
                            jnp.concatenate([g["k_norm_g"] for g in g_sb], axis=1)], axis=1)
    misc = jnp.pad(misc, ((0, 0), (0, -misc.shape[1] % d))).reshape(-1, d)
    small = [dmod.reshape(-1, d), g_n1, g_n2, g_conv.reshape(-1, d), misc]
    small_rows = [s.shape[0] for s in small]
    pad_rows = -sum(small_rows) % 8
    pay3 = jnp.concatenate(small + [jnp.zeros((pad_rows, d), F32)], axis=0)
    nrow3 = pay3.shape[0]
    g3 = _allgather_small(pay3, "ag_small")
    summed = _rowwise(lambda r, v: ([], [_colsum(r[0])]), [g3.reshape(ndev, nrow3 * d)], [], [], [(1, nrow3 * d)],
                      name="small_sum")[0].reshape(nrow3, d)
    offs = [0]
    for n_ in small_rows:
        offs.append(offs[-1] + n_)
    grads = {}
    grads["ada_b"] = summed[offs[0]:offs[1]].reshape(depth, N_MOD * d)
    grads["norm1_g"] = summed[offs[1]:offs[2]]
    grads["norm2_g"] = summed[offs[2]:offs[3]]
    conv_sum = summed[offs[3]:offs[4]].reshape(ndn, DN_CONV, nchips * conv_cols)
    grads["dn_conv_w"] = lax.dynamic_slice_in_dim(conv_sum, chip * conv_cols, conv_cols, axis=2)
    mrow = summed[offs[4]:offs[5]].reshape(-1)
    o = 0
    grads["dn_onorm_g"] = mrow[o:o + ndn * DN_HEAD_DIM].reshape(ndn, DN_HEAD_DIM)
    o += ndn * DN_HEAD_DIM
    grads["dn_a_log"] = mrow[o:o + ndn * LANES].reshape(ndn, LANES)[:, :heads]
    o += ndn * LANES
    grads["dn_dt_bias"] = mrow[o:o + ndn * LANES].reshape(ndn, LANES)[:, :heads]
    o += ndn * LANES
    grads["sb_q_norm_g"] = mrow[o:o + nsb * SB_HEAD_DIM].reshape(nsb, SB_HEAD_DIM)
    o += nsb * SB_HEAD_DIM
    grads["sb_k_norm_g"] = mrow[o:o + nsb * SB_HEAD_DIM].reshape(nsb, SB_HEAD_DIM)
    dmod_all = g3.reshape(ndev, nrow3, d)[:, :small_rows[0]].reshape(ndev, depth, N_MOD * d)
    dmod_mine = lax.dynamic_slice_in_dim(dmod_all, chip * mod_cols, mod_cols, axis=2)
    dmod16 = jnp.pad(dmod_mine, ((0, 16 - ndev), (0, 0), (0, 0)))
    grads["ada_w"] = jnp.stack([_mm(cond16, dmod16[:, i], ta=True, name="ada_gw%d" % i) for i in range(depth)])

    pieces = {n: [] for n, _ in _BIG}
    for gi in reversed(range(len(reduce_groups))):
        if gi not in reduces:
            continue
        sl, pairs, sems, bufs = reduces[gi]
        lands = _ici_wait(sems, bufs, "slot", grad_x, "gr_wait%d" % gi)
        flat = lambda a: a.reshape(nchips, -1, a.shape[-1])
        reduced = [_sum_chips(flat(land), flat(pair), "gr_chip_add%d_%s" % (gi, n))
                   for (n, _, _), land, pair in zip(sl, lands, pairs)]
        for (n, lo, hi), mine, other in zip(sl, reduced, _share_halves(reduced, "gr_share%d" % gi)):
            shape = (hi - lo, -1, mine.shape[-1])
            pieces[n].append((lo, both_halves(mine.reshape(shape), other.reshape(shape), 1)))
    for n, _ in _BIG:
        grads[n] = jnp.concatenate([p for _, p in sorted(pieces[n], key=lambda t: t[0])], axis=0)

    delta, new_m, new_v = {}, {}, {}
    for n in names:
        delta[n], new_m[n], new_v[n] = _adamw(w[n], grads[n], mom[n], var[n], "adamw_" + n)
    return (loss, grad_x[None], *[grads[n] for n in names], *[delta[n] for n in names], *[new_m[n] for n in names],
            *[new_v[n] for n in names])
```

```python
import functools

import jax
import jax.numpy as jnp
from jax import lax
from jax.experimental import pallas as pl
from jax.experimental.pallas import tpu as pltpu

F32 = jnp.float32
BF16 = jnp.bfloat16
HI = lax.Precision.HIGHEST
MESH = pl.DeviceIdType.MESH

EPS = 1e-6
N_MOD = 6
DN_HEAD_DIM = 128
DN_CONV = 4
DN_CHUNK = 64
DN_GROUP = 8
DN_GATE_ROWS = 256
SB_HEAD_DIM = 64
SB_BLOCK = 128
SB_QBLOCK = 512
SB_PART = 128
LANES = 128
VMEM_LIMIT = 56 * 1024 * 1024
MM_BLOCK_BUDGET = 36 * 1024 * 1024
ROW_BLOCK_BUDGET = 20 * 1024 * 1024

ADAM_LR = 0.001
ADAM_B1 = 0.9
ADAM_B2 = 0.999
ADAM_EPS = 1e-08
ADAM_WD = 0.01
ADAM_STEP = 10


def _cparams(sem=None):
    return pltpu.CompilerParams(dimension_semantics=sem, vmem_limit_bytes=VMEM_LIMIT)


def _sigmoid(x):
    return 1.0 / (1.0 + jnp.exp(-x))


def _silu(x):
    return x * _sigmoid(x)


def _dsilu(x):
    s = _sigmoid(x)
    return s * (1.0 + x * (1.0 - s))


def _softplus(x):
    return jnp.maximum(x, 0.0) + jnp.log1p(jnp.exp(-jnp.abs(x)))


def _pick_tile(n, prefs):
    for t in prefs:
        if n % t == 0:
            return t
    return n


def _mm(a, b, *, ta=False, tb=False, out_dtype=F32, name):
    m = a.shape[1] if ta else a.shape[0]
    k = a.shape[0] if ta else a.shape[1]
    n = b.shape[0] if tb else b.shape[1]
    assert (b.shape[1] if tb else b.shape[0]) == k
    size = lambda dt: jnp.dtype(dt).itemsize
    best = None
    for tm in sorted({t for t in (m, 2048, 1024, 512, 256, 128) if m % t == 0 and (t % 128 == 0 or t == m)}):
        for tn in sorted({t for t in (n, 2816, 1408, 1024, 768, 512, 256, 128) if n % t == 0 and (t % 128 == 0 or t == n)}):
            need = 2 * (tm * k * size(a.dtype) + tn * k * size(b.dtype) + tm * tn * size(out_dtype))
            if need <= MM_BLOCK_BUDGET and (best is None or tm * tn > best[0] * best[1]):
                best = (tm, tn)
    tm, tn = best
    dims = (((0 if ta else 1,), (1 if tb else 0,)), ((), ()))

    def body(a_ref, b_ref, o_ref):
        av = a_ref[...].astype(BF16)
        bv = b_ref[...].astype(BF16)
        o_ref[...] = lax.dot_general(av, bv, dims, preferred_element_type=F32).astype(o_ref.dtype)

    a_spec = pl.BlockSpec((k, tm), lambda i, j: (0, i)) if ta else pl.BlockSpec((tm, k), lambda i, j: (i, 0))
    b_spec = pl.BlockSpec((tn, k), lambda i, j: (j, 0)) if tb else pl.BlockSpec((k, tn), lambda i, j: (0, j))
    return pl.pallas_call(
        body, grid=(m // tm, n // tn), in_specs=[a_spec, b_spec],
        out_specs=pl.BlockSpec((tm, tn), lambda i, j: (i, j)),
        out_shape=jax.ShapeDtypeStruct((m, n), out_dtype), name=name,
        compiler_params=_cparams(("parallel", "parallel")))(a, b)


def _rowwise(body, rows, vecs, outs, accs, *, name, tr=None):
    rows = [r if isinstance(r, tuple) else (r, r.shape[1], 0) for r in rows]
    nrows = rows[0][0].shape[0]
    if tr is None:
        per_row = sum(w * jnp.dtype(a.dtype).itemsize for a, w, _ in rows) + sum(
            w * jnp.dtype(dt).itemsize for w, dt in outs)
        tr = next((t for t in (2048, 1024, 512, 256) if nrows % t == 0 and 2 * t * per_row <= ROW_BLOCK_BUDGET), 256)
    tr = tr if nrows % tr == 0 else nrows
    nr, nv, no = len(rows), len(vecs), len(outs)

    def kern(*refs):
        r_in, v_in = refs[:nr], refs[nr:nr + nv]
        o_refs, a_refs = refs[nr + nv:nr + nv + no], refs[nr + nv + no:]
        res_o, res_a = body([r[...] for r in r_in], [v[...] for v in v_in])
        for o, val in zip(o_refs, res_o):
            o[...] = val.astype(o.dtype)
        if a_refs:
            @pl.when(pl.program_id(0) == 0)
            def _():
                for a in a_refs:
                    a[...] = jnp.zeros(a.shape, a.dtype)
            for a, val in zip(a_refs, res_a):
                a[...] += val

    in_specs = [pl.BlockSpec((tr, w), functools.partial(lambda i, cb: (i, cb), cb=cb)) for _, w, cb in rows]
    in_specs += [pl.BlockSpec(v.shape, functools.partial(lambda i, nd: (0,) * nd, nd=v.ndim)) for v in vecs]
    out_specs = [pl.BlockSpec((tr, w), lambda i: (i, 0)) for w, _ in outs]
    out_specs += [pl.BlockSpec(s, lambda i: (0, 0)) for s in accs]
    out_shape = [jax.ShapeDtypeStruct((nrows, w), dt) for w, dt in outs]
    out_shape += [jax.ShapeDtypeStruct(s, F32) for s in accs]
    res = pl.pallas_call(
        kern, grid=(nrows // tr,), in_specs=in_specs, out_specs=out_specs, out_shape=out_shape, name=name,
        compiler_params=_cparams(("arbitrary",) if accs else ("parallel",)))(*[r[0] for r in rows], *vecs)
    return res


def _colsum(v):
    return jnp.sum(v, axis=0, keepdims=True)


def _adaln_fwd(x, g, sc, sh, name):
    def body(r, v):
        (xv,), (gv, scv, shv) = r, v
        rs = lax.rsqrt(jnp.mean(xv * xv, axis=-1, keepdims=True) + EPS)
        return [(xv * rs) * gv * (1.0 + scv) + shv], []
    return _rowwise(body, [x], [g, sc, sh], [(x.shape[1], BF16)], [], name=name)[0]


def _adaln_bwd(x, dh, dxr, g, sc, name):
    d = x.shape[1]

    def body(r, v):
        (xv, dhv, dxv), (gv, scv) = r, v
        rs = lax.rsqrt(jnp.mean(xv * xv, axis=-1, keepdims=True) + EPS)
        nv = xv * rs
        dn = dhv * (gv * (1.0 + scv))
        dx = rs * (dn - nv * jnp.mean(dn * nv, axis=-1, keepdims=True)) + dxv
        dhn = dhv * nv
        return [dx], [_colsum(dhn * (1.0 + scv)), _colsum(dhn * gv), _colsum(dhv)]
    return _rowwise(body, [x, dh, dxr], [g, sc], [(d, F32)], [(1, d)] * 3, name=name)


def _resid_fwd(x, y, gt, name):
    def body(r, v):
        return [r[0] + v[0] * r[1]], []
    return _rowwise(body, [x, y], [gt], [(x.shape[1], F32)], [], name=name)[0]


def _resid_bwd(dx, y, gt, name):
    d = dx.shape[1]

    def body(r, v):
        return [v[0] * r[0]], [_colsum(r[0] * r[1])]
    return _rowwise(body, [dx, y], [gt], [(d, BF16)], [(1, d)], name=name)


def _swiglu_fwd(u, name):
    f = u.shape[1] // 2

    def body(r, v):
        uv = r[0]
        return [_silu(uv[:, :f]) * uv[:, f:]], []
    return _rowwise(body, [u], [], [(f, BF16)], [], name=name)[0]


def _swiglu_bwd(u, da, name):
    f = u.shape[1] // 2

    def body(r, v):
        uv, dav = r
        gate, up = uv[:, :f], uv[:, f:]
        return [jnp.concatenate([dav * up * _dsilu(gate), dav * _silu(gate)], axis=1)], []
    return _rowwise(body, [u, da], [], [(2 * f, BF16)], [], name=name)[0]


def _loss_fwd_bwd(y, tgt, name):
    d = y.shape[1]

    def body(r, v):
        err = r[0] - r[1]
        part = jnp.sum(jnp.sum(err * err, axis=1, keepdims=True), axis=0, keepdims=True) * (0.5 / d)
        return [err * (1.0 / d)], [jnp.broadcast_to(part, (1, LANES))]
    dy, acc = _rowwise(body, [y, tgt], [], [(d, F32)], [(1, LANES)], name=name)
    return acc[0, 0], dy


def _head_means(v):
    row = lax.broadcasted_iota(jnp.int32, (LANES, LANES), 0)
    col = lax.broadcasted_iota(jnp.int32, (LANES, LANES), 1)
    same = ((row // SB_HEAD_DIM) == (col // SB_HEAD_DIM)).astype(F32)
    parts = [jnp.dot(v[:, g * LANES:(g + 1) * LANES], same, precision=HI, preferred_element_type=F32)
             for g in range(v.shape[1] // LANES)]
    return jnp.concatenate(parts, axis=1) * (1.0 / SB_HEAD_DIM)


def _sb_norm_fwd(qkv, gq, gk, name):
    d = qkv.shape[1] // 3

    def body(r, v):
        return [x * lax.rsqrt(_head_means(x * x) + EPS) * g for x, g in zip(r, v)], []
    return _rowwise(body, [(qkv, d, 0), (qkv, d, 1)], [gq, gk], [(d, F32), (d, F32)], [], name=name)


def _sb_norm_bwd(qkv, dqn, dkn, dv, gq, gk, name):
    d = qkv.shape[1] // 3

    def body(r, v):
        outs, accs = [], []
        for x, dy, g in ((r[0], r[2], v[0]), (r[1], r[3], v[1])):
            rs = lax.rsqrt(_head_means(x * x) + EPS)
            nv = x * rs
            dn = dy * g
            outs.append(rs * (dn - nv * _head_means(dn * nv)))
            accs.append(_colsum(dy * nv))
        return [jnp.concatenate(outs + [r[4]], axis=1)], accs
    return _rowwise(body, [(qkv, d, 0), (qkv, d, 1), dqn, dkn, dv], [gq, gk], [(3 * d, BF16)], [(1, d), (1, d)],
                    name=name)


def _head_tiles(a):
    return [a[:, h * DN_HEAD_DIM:(h + 1) * DN_HEAD_DIM] for h in range(a.shape[1] // DN_HEAD_DIM)]


def _dn_post_fwd(o, proj, g, name):
    d = o.shape[1]

    def body(r, v):
        outs = []
        for ov, zv in zip(_head_tiles(r[0]), _head_tiles(r[1])):
            rs = lax.rsqrt(jnp.mean(ov * ov, axis=-1, keepdims=True) + EPS)
            outs.append(ov * rs * v[0] * _silu(zv))
        return [jnp.concatenate(outs, axis=1)], []
    return _rowwise(body, [o, (proj, d, 3)], [g], [(d, BF16)], [], name=name)[0]


def _dn_post_bwd(o, proj, don, g, name):
    d = o.shape[1]

    def body(r, v):
        dos, dzs, dg = [], [], jnp.zeros((1, DN_HEAD_DIM), F32)
        for ov, zv, dv in zip(_head_tiles(r[0]), _head_tiles(r[1]), _head_tiles(r[2])):
            rs = lax.rsqrt(jnp.mean(ov * ov, axis=-1, keepdims=True) + EPS)
            nv = ov * rs
            s = _silu(zv)
            dn = dv * v[0] * s
            dos.append(rs * (dn - nv * jnp.mean(dn * nv, axis=-1, keepdims=True)))
            dzs.append(dv * nv * v[0] * _dsilu(zv))
            dg = dg + _colsum(dv * nv * s)
        return [jnp.concatenate(dos, axis=1), jnp.concatenate(dzs, axis=1)], [dg]
    return _rowwise(body, [o, (proj, d, 3), don], [g], [(d, F32), (d, BF16)], [(1, DN_HEAD_DIM)], name=name)


def _chunk_tri(tr, upper):
    row = lax.broadcasted_iota(jnp.int32, (tr, tr), 0)
    col = lax.broadcasted_iota(jnp.int32, (tr, tr), 1)
    same = (row // DN_CHUNK) == (col // DN_CHUNK)
    return (same & ((row <= col) if upper else (row >= col))).astype(F32)


def _dn_gate_fwd(proj, colblk, a_log, dt_bias, name):
    def body(r, v):
        ab = r[0]
        a, b = ab[:, :LANES], ab[:, LANES:]
        g = -jnp.exp(v[0]) * _softplus(a + v[1])
        big_g = jnp.dot(_chunk_tri(g.shape[0], False), g, precision=HI, preferred_element_type=F32)
        return [big_g, _sigmoid(b)], []
    return _rowwise(body, [(proj, 2 * LANES, colblk)], [a_log, dt_bias], [(LANES, F32), (LANES, F32)], [], name=name,
                    tr=DN_GATE_ROWS)


def _dn_gate_bwd(proj, colblk, d_big_g, dbeta, a_log, dt_bias, name):
    def body(r, v):
        ab, dgc, dbt = r
        a, b = ab[:, :LANES], ab[:, LANES:]
        dg = jnp.dot(_chunk_tri(dgc.shape[0], True), dgc, precision=HI, preferred_element_type=F32)
        na = -jnp.exp(v[0])
        pre = a + v[1]
        da = dg * na * _sigmoid(pre)
        beta = _sigmoid(b)
        db = dbt * beta * (1.0 - beta)
        return [jnp.concatenate([da, db], axis=1)], [_colsum(dg * na * _softplus(pre)), _colsum(da)]
    return _rowwise(body, [(proj, 2 * LANES, colblk), d_big_g, dbeta], [a_log, dt_bias],
                    [(2 * LANES, BF16)], [(1, LANES), (1, LANES)], name=name, tr=DN_GATE_ROWS)


def _shift_rows(x, s):
    if s == 0:
        return x
    t = x.shape[0]
    row = lax.broadcasted_iota(jnp.int32, x.shape, 0)
    rolled = pltpu.roll(x, s % t, axis=0)
    return jnp.where((row >= s) if s > 0 else (row < t + s), rolled, 0.0)


def _rnd(x):
    return x.astype(BF16).astype(F32)


def _conv(x, c_ref):
    c = x * _rnd(c_ref[DN_CONV - 1:DN_CONV, :])
    for s in range(1, DN_CONV):
        c = c + _shift_rows(x, s) * _rnd(c_ref[DN_CONV - 1 - s:DN_CONV - s, :])
    return c


def _dn_prep_fwd(proj, conv_w, heads, name):
    t = proj.shape[0]
    d = heads * DN_HEAD_DIM

    def body(xq, xk, xv, cq, ck, cv, q_ref, k_ref, v_ref):
        for x_ref, c_ref, o_ref, norm in ((xq, cq, q_ref, True), (xk, ck, k_ref, True), (xv, cv, v_ref, False)):
            y = _silu(_conv(_rnd(x_ref[...]), c_ref))
            if norm:
                y = y * lax.rsqrt(jnp.sum(y * y, axis=-1, keepdims=True) + EPS)
            o_ref[...] = y

    xs = [pl.BlockSpec((t, DN_HEAD_DIM), functools.partial(lambda h, o: (0, h + o), o=o * heads)) for o in range(3)]
    cs = [pl.BlockSpec((DN_CONV, DN_HEAD_DIM), functools.partial(lambda h, o: (0, h + o), o=o * heads)) for o in range(3)]
    return pl.pallas_call(
        body, grid=(heads,), in_specs=xs + cs,
        out_specs=[pl.BlockSpec((t, DN_HEAD_DIM), lambda h: (0, h))] * 3,
        out_shape=[jax.ShapeDtypeStruct((t, d), F32)] * 3, name=name,
        compiler_params=_cparams(("parallel",)))(proj, proj, proj, conv_w, conv_w, conv_w)


def _dn_prep_bwd(proj, conv_w, dq, dk, dv, heads, name):
    t = proj.shape[0]
    d = heads * DN_HEAD_DIM

    def body(xq, xk, xv, cq, ck, cv, gq, gk, gv, oq, ok, ov, wq, wk, wv):
        for x_ref, c_ref, g_ref, o_ref, w_ref, norm in ((xq, cq, gq, oq, wq, True), (xk, ck, gk, ok, wk, True),
                                                        (xv, cv, gv, ov, wv, False)):
            x, dy = _rnd(x_ref[...]), g_ref[...]
            c = _conv(x, c_ref)
            if norm:
                y = _silu(c)
                rs = lax.rsqrt(jnp.sum(y * y, axis=-1, keepdims=True) + EPS)
                yn = y * rs
                dy = rs * (dy - yn * jnp.sum(dy * yn, axis=-1, keepdims=True))
            dc = _rnd(dy * _dsilu(c))
            dx = dc * _rnd(c_ref[DN_CONV - 1:DN_CONV, :])
            w_ref[DN_CONV - 1:DN_CONV, :] = _colsum(dc * x)
            for s in range(1, DN_CONV):
                dx = dx + _shift_rows(dc, -s) * _rnd(c_ref[DN_CONV - 1 - s:DN_CONV - s, :])
                w_ref[DN_CONV - 1 - s:DN_CONV - s, :] = _colsum(dc * _shift_rows(x, s))
            o_ref[...] = dx.astype(o_ref.dtype)

    xs = [pl.BlockSpec((t, DN_HEAD_DIM), functools.partial(lambda h, o: (0, h + o), o=o * heads)) for o in range(3)]
    cs = [pl.BlockSpec((DN_CONV, DN_HEAD_DIM), functools.partial(lambda h, o: (0, h + o), o=o * heads)) for o in range(3)]
    hs = pl.BlockSpec((t, DN_HEAD_DIM), lambda h: (0, h))
    ws = pl.BlockSpec((DN_CONV, DN_HEAD_DIM), lambda h: (0, h))
    return pl.pallas_call(
        body, grid=(heads,), in_specs=xs + cs + [hs] * 3, out_specs=[hs] * 3 + [ws] * 3,
        out_shape=[jax.ShapeDtypeStruct((t, d), BF16)] * 3 + [jax.ShapeDtypeStruct((DN_CONV, d), F32)] * 3, name=name,
        compiler_params=_cparams(("parallel",)))(proj, proj, proj, conv_w, conv_w, conv_w, dq, dk, dv)


X3 = "three bf16 passes"


def _bdot(a, b, dims, prec):
    if prec != X3:
        return lax.dot_general(a, b, dims, precision=prec, preferred_element_type=F32)
    a1, b1 = a.astype(BF16), b.astype(BF16)
    a2, b2 = (a - a1.astype(F32)).astype(BF16), (b - b1.astype(F32)).astype(BF16)
    dot = lambda x, y: lax.dot_general(x, y, dims, preferred_element_type=F32)
    return dot(a1, b1) + (dot(a1, b2) + dot(a2, b1))


def _bmm(a, b, prec=None):
    return _bdot(a, b, (((2,), (1,)), ((0,), (0,))), prec)


def _bmm_nt(a, b, prec=None):
    return _bdot(a, b, (((2,), (2,)), ((0,), (0,))), prec)


def _bmm_tn(a, b, prec=None):
    return _bdot(a, b, (((1,), (1,)), ((0,), (0,))), prec)


def _dn_local(qg, kg, vg, gc, gr, bt):
    c = qg.shape[1]
    row = lax.broadcasted_iota(jnp.int32, (c, c), 0)
    col = lax.broadcasted_iota(jnp.int32, (c, c), 1)
    incl, strict = (row >= col)[None], (row > col)[None]
    decay = jnp.where(incl, jnp.exp(jnp.where(incl, gc - gr, 0.0)), 0.0)
    kb = kg * bt
    vb = vg * bt
    m = _bmm_nt(kb.astype(BF16), kg.astype(BF16))
    a = jnp.where(strict, m * decay, 0.0)
    bp = -a
    tm = jnp.where((row == col)[None], 1.0, 0.0) + bp
    steps = max(1, (c - 1).bit_length()) - 1
    for _ in range(steps):
        bp = _bmm(bp, bp, X3)
        tm = tm + _bmm(tm, bp, X3)
    eg = jnp.exp(gc)
    glast = gc[:, c - 1:c, :]
    erel = jnp.exp(glast - gc)
    kbg = kb * eg
    qm = _bmm_nt(qg.astype(BF16), kg.astype(BF16))
    return dict(decay=decay, strict=strict, kb=kb, vb=vb, m=m, tm=tm, eg=eg, erel=erel, kbg=kbg, qm=qm,
                u=_bmm(tm, vb, X3), w=_bmm(tm, kbg, X3), qd=qg * eg, kt=kg * erel, gl=jnp.exp(glast))


def _dot(a, b):
    return jnp.dot(a.astype(BF16), b.astype(BF16), preferred_element_type=F32)


def _dot_nt(a, b):
    return lax.dot_general(a.astype(BF16), b.astype(BF16), (((1,), (1,)), ((), ())), preferred_element_type=F32)


def _dot_tn(a, b):
    return lax.dot_general(a.astype(BF16), b.astype(BF16), (((0,), (0,)), ((), ())), preferred_element_type=F32)


def _dn_chunk_specs(t, heads):
    n = t // DN_CHUNK
    hs = pl.BlockSpec((t, DN_HEAD_DIM), lambda h: (0, h))
    gcs = pl.BlockSpec((1, n, DN_CHUNK, 1), lambda h: (h, 0, 0, 0))
    grs = pl.BlockSpec((1, n, 1, DN_CHUNK), lambda h: (h, 0, 0, 0))
    ss = pl.BlockSpec((1, n, DN_HEAD_DIM, DN_HEAD_DIM), lambda h: (h, 0, 0, 0))
    return n, hs, gcs, grs, ss


def _dn_chunk_fwd(q, k, v, gc, gr, bc, name):
    t, d = q.shape
    heads = d // DN_HEAD_DIM
    c, dk = DN_CHUNK, DN_HEAD_DIM
    n, hs, gcs, grs, ss = _dn_chunk_specs(t, heads)
    nb = min(DN_GROUP, n)
    scale = dk ** -0.5

    def body(q_ref, k_ref, v_ref, gc_ref, gr_ref, b_ref, o_ref, s_ref, u_s, w_s, at_s, qd_s, kt_s, gl_s):
        def group(gi, carry):
            r0 = pl.multiple_of(gi * (nb * c), nb * c)
            n0 = gi * nb
            ld = lambda ref: ref[pl.ds(r0, nb * c), :].reshape(nb, c, dk)
            loc = _dn_local(ld(q_ref) * scale, ld(k_ref), ld(v_ref), gc_ref[0, pl.ds(n0, nb)],
                            gr_ref[0, pl.ds(n0, nb)], b_ref[0, pl.ds(n0, nb)])
            u_s[pl.ds(n0, nb)] = loc["u"]
            w_s[pl.ds(n0, nb)] = loc["w"]
            at_s[pl.ds(n0, nb)] = loc["qm"] * loc["decay"]
            qd_s[pl.ds(n0, nb)] = loc["qd"]
            kt_s[pl.ds(n0, nb)] = loc["kt"]
            gl_s[pl.ds(n0, nb)] = loc["gl"]
            return carry
        lax.fori_loop(0, n // nb, group, 0)

        def chunk(i, s):
            s_ref[0, i] = s
            vnew = u_s[i] - _dot(w_s[i], s)
            o = _dot(qd_s[i], s) + _dot(at_s[i], vnew)
            o_ref[pl.ds(pl.multiple_of(i * c, c), c), :] = o
            return s * gl_s[i] + _dot_tn(kt_s[i], vnew)
        lax.fori_loop(0, n, chunk, jnp.zeros((dk, dk), F32))

    scratch = [pltpu.VMEM((n, c, dk), F32), pltpu.VMEM((n, c, dk), F32), pltpu.VMEM((n, c, c), F32),
               pltpu.VMEM((n, c, dk), F32), pltpu.VMEM((n, c, dk), F32), pltpu.VMEM((n, 1, 1), F32)]
    return pl.pallas_call(
        body, grid=(heads,), in_specs=[hs, hs, hs, gcs, grs, gcs], out_specs=[hs, ss],
        out_shape=[jax.ShapeDtypeStruct((t, d), F32), jax.ShapeDtypeStruct((heads, n, dk, dk), F32)],
        scratch_shapes=scratch, name=name, compiler_params=_cparams(("parallel",)))(q, k, v, gc, gr, bc)


def _dn_chunk_bwd(q, k, v, gc, gr, bc, s_all, do, name):
    t, d = q.shape
    heads = d // DN_HEAD_DIM
    c, dk = DN_CHUNK, DN_HEAD_DIM
    n, hs, gcs, grs, ss = _dn_chunk_specs(t, heads)
    nb = min(DN_GROUP, n)
    scale = dk ** -0.5

    def body(q_ref, k_ref, v_ref, gc_ref, gr_ref, b_ref, s_ref, do_ref,
             dq_ref, dk_ref, dv_ref, dgc_ref, dgr_ref, db_ref,
             u_s, w_s, att_s, qd_s, kt_s, gl_s, du_s, dw_s, dat_s, dqd_s, dkt_s, dgl_s):
        def load_group(gi):
            r0 = pl.multiple_of(gi * (nb * c), nb * c)
            n0 = gi * nb
            ld = lambda ref: ref[pl.ds(r0, nb * c), :].reshape(nb, c, dk)
            qg, kg, vg = ld(q_ref) * scale, ld(k_ref), ld(v_ref)
            gcv, grv, bt = gc_ref[0, pl.ds(n0, nb)], gr_ref[0, pl.ds(n0, nb)], b_ref[0, pl.ds(n0, nb)]
            return r0, n0, qg, kg, vg, gcv, grv, bt, _dn_local(qg, kg, vg, gcv, grv, bt)

        def group_a(gi, carry):
            _, n0, qg, kg, _, gcv, grv, _, loc = load_group(gi)
            row = lax.broadcasted_iota(jnp.int32, (c, c), 0)
            col = lax.broadcasted_iota(jnp.int32, (c, c), 1)
            upper = (col >= row)[None]
            decay_t = jnp.where(upper, jnp.exp(jnp.where(upper, grv - gcv, 0.0)), 0.0)
            u_s[pl.ds(n0, nb)] = loc["u"]
            w_s[pl.ds(n0, nb)] = loc["w"]
            att_s[pl.ds(n0, nb)] = _bmm_nt(kg.astype(BF16), qg.astype(BF16)) * decay_t
            qd_s[pl.ds(n0, nb)] = loc["qd"]
            kt_s[pl.ds(n0, nb)] = loc["kt"]
            gl_s[pl.ds(n0, nb)] = loc["gl"]
            return carry
        lax.fori_loop(0, n // nb, group_a, 0)

        def chunk_b(it, ds_next):
            i = n - 1 - it
            s = s_ref[0, i]
            dov = do_ref[pl.ds(pl.multiple_of(i * c, c), c), :]
            w, kt = w_s[i], kt_s[i]
            vnew = u_s[i] - _dot(w, s)
            dvnew = _dot(att_s[i], dov) + _dot(kt, ds_next)
            du_s[i] = dvnew
            dw_s[i] = -_dot_nt(dvnew, s)
            dat_s[i] = _dot_nt(dov, vnew)
            dqd_s[i] = _dot_nt(dov, s)
            dkt_s[i] = _dot_nt(vnew, ds_next)
            dgl_s[i] = jnp.sum(jnp.sum(ds_next * s, axis=1, keepdims=True), axis=0, keepdims=True)
            return ds_next * gl_s[i] + _dot_tn(qd_s[i], dov) - _dot_tn(w, dvnew)
        lax.fori_loop(0, n, chunk_b, jnp.zeros((dk, dk), F32))

        def group_c(gi, carry):
            r0, n0, qg, kg, vg, gcv, grv, bt, loc = load_group(gi)
            sl = pl.ds(n0, nb)
            du, dw, dat, dqd, dkt, dgl = du_s[sl], dw_s[sl], dat_s[sl], dqd_s[sl], dkt_s[sl], dgl_s[sl]
            tm, decay, kb, kbg = loc["tm"], loc["decay"], loc["kb"], loc["kbg"]
            dvb = _bmm_tn(tm, du, X3)
            dkbg = _bmm_tn(tm, dw, X3)
            dt = _bmm_nt(du, loc["vb"], X3) + _bmm_nt(dw, kbg, X3)
            da = jnp.where(loc["strict"], -_bmm_tn(tm, _bmm_nt(dt, tm, X3), X3), 0.0)
            dms = (da * decay).astype(BF16)
            dqs = (dat * decay).astype(BF16)
            kgb = kg.astype(BF16)
            dkb = _bmm(dms, kgb) + dkbg * loc["eg"]
            dqt = _bmm(dqs, kgb) + dqd * loc["eg"]
            dkk = _bmm_tn(dms, kb.astype(BF16)) + _bmm_tn(dqs, qg.astype(BF16)) + dkt * loc["erel"] + dkb * bt
            e = (da * loc["m"] + dat * loc["qm"]) * decay
            lsum = lambda x: jnp.sum(x, axis=2, keepdims=True)
            dkt_kt = lsum(dkt * loc["kt"])
            dgcv = lsum(e) + lsum(dqd * loc["qd"]) - dkt_kt + lsum(dkbg * kbg)
            dglast = jnp.sum(dkt_kt, axis=1, keepdims=True) + dgl * loc["gl"]
            rowc = lax.broadcasted_iota(jnp.int32, (1, c, 1), 1)
            dgc_ref[0, sl] = dgcv + jnp.where(rowc == c - 1, dglast, 0.0)
            dgr_ref[0, sl] = -jnp.sum(e, axis=1, keepdims=True)
            db_ref[0, sl] = lsum(dkb * kg) + lsum(dvb * vg)
            rows = pl.ds(r0, nb * c)
            dq_ref[rows, :] = (dqt * scale).reshape(nb * c, dk)
            dk_ref[rows, :] = dkk.reshape(nb * c, dk)
            dv_ref[rows, :] = (dvb * bt).reshape(nb * c, dk)
            return carry
        lax.fori_loop(0, n // nb, group_c, 0)

    big = lambda: pltpu.VMEM((n, c, dk), F32)
    sq = lambda: pltpu.VMEM((n, c, c), F32)
    one = lambda: pltpu.VMEM((n, 1, 1), F32)
    scratch = [big(), big(), sq(), big(), big(), one(), big(), big(), sq(), big(), big(), one()]
    return pl.pallas_call(
        body, grid=(heads,), in_specs=[hs, hs, hs, gcs, grs, gcs, ss, hs], out_specs=[hs, hs, hs, gcs, grs, gcs],
        out_shape=[jax.ShapeDtypeStruct((t, d), F32)] * 3 + [
            jax.ShapeDtypeStruct((heads, n, c, 1), F32), jax.ShapeDtypeStruct((heads, n, 1, c), F32),
            jax.ShapeDtypeStruct((heads, n, c, 1), F32)],
        scratch_shapes=scratch, name=name, compiler_params=_cparams(("parallel",)))(q, k, v, gc, gr, bc, s_all, do)


def _dot01(x, m01):
    x1 = x.astype(BF16)
    r1 = x - x1.astype(F32)
    x2 = r1.astype(BF16)
    x3 = (r1 - x2.astype(F32)).astype(BF16)
    dot = lambda a: jnp.dot(a, m01, preferred_element_type=F32)
    return (dot(x1) + dot(x2)) + dot(x3)


def _sb_logs(z, mask):
    rows, bk = z.shape
    row = lax.broadcasted_iota(jnp.int32, (bk, bk), 0)
    col = lax.broadcasted_iota(jnp.int32, (bk, bk), 1)
    later = (row > col).astype(BF16)
    lm, ls, cs = [], [], []
    for p in [slice(p, p + SB_PART) for p in range(0, rows, SB_PART)]:
        lsm = -_softplus(z[p])
        lm.append(lsm if mask is None else jnp.where(mask[p], lsm, 0.0))
        ls.append(z[p] + lsm)
        cs.append(_dot01(lm[-1], later))
    cat = lambda xs: jnp.concatenate(xs, axis=0)
    return cat(ls), cat(cs), jnp.sum(cat(lm), axis=1, keepdims=True)


def _sb_masks(bq, bk):
    row = lax.broadcasted_iota(jnp.int32, (2 * bq, bk), 0)
    row = jnp.where(row >= bq, row - bq, row)
    col = lax.broadcasted_iota(jnp.int32, (2 * bq, bk), 1)
    return [(col + dd * bk) < row for dd in range(bq // bk)]


def _sb_sweep(bq, bk, qi, fetch, visit, carry):
    nd = bq // bk
    masks = _sb_masks(bq, bk)
    ahead = lambda j: fetch(jnp.maximum(j - 1, 0))
    pre = fetch(qi * nd + nd - 1)
    for dd in reversed(range(nd)):
        nxt = ahead(qi * nd + dd)
        carry = visit(qi * nd + dd, masks[dd], carry, pre)
        pre = nxt

    def below(it, c):
        j = qi * nd - 1 - it
        nxt = ahead(j)
        return visit(j, None, c[0], c[1]), nxt
    return lax.fori_loop(0, qi * nd, below, (carry, pre))[0]


def _sb_sweep2(bq, bk, qi, fetch, first, second, carry):
    nd = bq // bk
    masks = _sb_masks(bq, bk)
    ahead = lambda j: fetch(jnp.maximum(j - 1, 0))
    pre = fetch(qi * nd + nd - 1)
    r = jnp.zeros((2 * bq, 1), F32)
    pend = None
    for dd in reversed(range(nd)):
        j = qi * nd + dd
        nxt = ahead(j)
        ls, cs, rsum, extras = first(j, masks[dd], pre)
        if pend is not None:
            carry = second(pend[0], carry, *pend[1])
        pend = (j, (jnp.where(masks[dd], ls + cs + r, -1e30),) + extras)
        r, pre = r + rsum, nxt

    def below(it, c):
        carry, r, pre, pj, pargs = c
        j = qi * nd - 1 - it
        nxt = ahead(j)
        ls, cs, rsum, extras = first(j, None, pre)
        carry = second(pj, carry, *pargs)
        return carry, r + rsum, nxt, j, (ls + cs + r,) + extras
    carry, _, _, pj, pargs = lax.fori_loop(0, qi * nd, below, (carry, r, pre) + pend)
    return second(pj, carry, *pargs)


def _stack_heads(a, h0):
    return jnp.concatenate([jnp.where(h0, a, 0.0), jnp.where(h0, 0.0, a)], axis=0).astype(BF16)


def _side_by_side(a, bq):
    return jnp.concatenate([a[:bq], a[bq:]], axis=1)


def _sb_specs(t, d, bq, vcol):
    qs = pl.BlockSpec((bq, LANES), lambda g, i: (i, g))
    ks = pl.BlockSpec((t, LANES), lambda g, i: (0, g))
    vs = pl.BlockSpec((t, LANES), lambda g, i: (0, g + vcol))
    return qs, ks, vs


def _sb_fwd(qn, kn, qkv, name):
    t, d = qn.shape
    bq, bk = min(SB_QBLOCK, t), min(SB_BLOCK, t)
    scale = SB_HEAD_DIM ** -0.5
    nt = (((1,), (1,)), ((), ()))

    def body(q_ref, k_ref, v_ref, o_ref):
        qi = pl.program_id(1)
        h0 = lax.broadcasted_iota(jnp.int32, (1, LANES), 1) < SB_HEAD_DIM
        q2 = _stack_heads(q_ref[...] * scale, h0)

        tile_rows = lambda j: pl.ds(pl.multiple_of(j * bk, bk), bk)

        def fetch(j):
            return lax.dot_general(q2, k_ref[tile_rows(j), :].astype(BF16), nt, preferred_element_type=F32)

        def first(j, mask, z):
            return _sb_logs(z, mask) + ((),)

        def second(j, acc, loga):
            a2 = _side_by_side(jnp.exp(loga).astype(BF16), bq)
            return acc + jnp.dot(a2, _stack_heads(v_ref[tile_rows(j), :], h0), preferred_element_type=F32)
        acc = _sb_sweep2(bq, bk, qi, fetch, first, second, jnp.zeros((bq, LANES), F32))
        o_ref[...] = acc.astype(o_ref.dtype)

    qs, ks, vs = _sb_specs(t, d, bq, 2 * d // LANES)
    return pl.pallas_call(
        body, grid=(d // LANES, t // bq), in_specs=[qs, ks, vs], out_specs=qs,
        out_shape=jax.ShapeDtypeStruct((t, d), BF16), name=name,
        compiler_params=_cparams(("parallel", "parallel")))(qn, kn, qkv)


def _sb_bwd(qn, kn, qkv, do, name):
    t, d = qn.shape
    bq, bk = min(SB_QBLOCK, t), min(SB_BLOCK, t)
    scale = SB_HEAD_DIM ** -0.5
    nt = (((1,), (1,)), ((), ()))

    def body(q_ref, k_ref, v_ref, do_ref, dq_ref, dk_ref, dv_ref, p_s, sg_s):
        qi = pl.program_id(1)

        @pl.when(qi == 0)
        def _():
            dk_ref[...] = jnp.zeros(dk_ref.shape, F32)
            dv_ref[...] = jnp.zeros(dv_ref.shape, F32)

        h0 = lax.broadcasted_iota(jnp.int32, (1, LANES), 1) < SB_HEAD_DIM
        q2 = _stack_heads(q_ref[...] * scale, h0)
        do2 = _stack_heads(do_ref[...], h0)
        row = lax.broadcasted_iota(jnp.int32, (bk, bk), 0)
        col = lax.broadcasted_iota(jnp.int32, (bk, bk), 1)
        later_incl = (row >= col).astype(BF16)
        zero = jnp.zeros((2 * bq, 1), F32)
        tn = (((0,), (0,)), ((), ()))

        tile_rows = lambda j: pl.ds(pl.multiple_of(j * bk, bk), bk)

        def fetch1(j):
            return lax.dot_general(q2, k_ref[tile_rows(j), :].astype(BF16), nt, preferred_element_type=F32)

        def first1(j, mask, z):
            da = lax.dot_general(do2, v_ref[tile_rows(j), :].astype(BF16), nt, preferred_element_type=F32)
            ls, cs, rsum = _sb_logs(z, mask)
            sg_s[j] = jnp.exp(ls)
            return ls, cs, rsum, (da,)

        def second1(j, sp, loga, da):
            a = jnp.exp(loga)
            p = a * da
            p_s[j] = p
            dv_ref[tile_rows(j), :] += lax.dot_general(a.astype(BF16), do2, tn, preferred_element_type=F32)
            return sp + jnp.sum(p, axis=1, keepdims=True)
        total = _sb_sweep2(bq, bk, qi, fetch1, first1, second1, zero)

        def fetch2(j):
            return _dot01(p_s[j], later_incl)

        def visit2(j, mask, carry, later_p):
            dq, sp = carry
            rows = tile_rows(j)
            p, sg = p_s[j], sg_s[j]
            pref = total - sp - later_p
            dz = p * (1.0 - sg) - pref * sg
            if mask is not None:
                dz = jnp.where(mask, dz, 0.0)
            dz = dz.astype(BF16)
            dk_ref[rows, :] += lax.dot_general(dz, q2, tn, preferred_element_type=F32)
            dq = dq + jnp.dot(_side_by_side(dz, bq), _stack_heads(k_ref[rows, :], h0), preferred_element_type=F32)
            return dq, sp + jnp.sum(p, axis=1, keepdims=True)
        dq, _ = _sb_sweep(bq, bk, qi, fetch2, visit2, (jnp.zeros((bq, LANES), F32), zero))
        dq_ref[...] = dq * scale

    qs, ks, vs = _sb_specs(t, d, bq, 2 * d // LANES)
    shp = jax.ShapeDtypeStruct((t, d), F32)
    scratch = [pltpu.VMEM((t // bk, 2 * bq, bk), F32), pltpu.VMEM((t // bk, 2 * bq, bk), F32)]
    return pl.pallas_call(
        body, grid=(d // LANES, t // bq), in_specs=[qs, ks, vs, qs], out_specs=[qs, ks, ks],
        out_shape=[shp, shp, shp], scratch_shapes=scratch, name=name,
        compiler_params=_cparams(("parallel", "arbitrary")))(qn, kn, qkv, do)


def _adamw(w, g, m, v, name):
    shape = w.shape
    two_d = lambda a: a.reshape(-1, shape[-1])
    c1 = 1.0 - ADAM_B1 ** ADAM_STEP
    c2 = 1.0 - ADAM_B2 ** ADAM_STEP

    def body(r, _):
        wv, gv, mv, vv = r
        mn = ADAM_B1 * mv + (1.0 - ADAM_B1) * gv
        vn = ADAM_B2 * vv + (1.0 - ADAM_B2) * (gv * gv)
        delta = -ADAM_LR * ((mn / c1) / (jnp.sqrt(vn / c2) + ADAM_EPS) + ADAM_WD * wv)
        return [delta, mn, vn], []
    width = shape[-1]
    res = _rowwise(body, [two_d(w), two_d(g), two_d(m), two_d(v)], [], [(width, F32)] * 3, [], name=name)
    return [r.reshape(shape) for r in res]


def _dn_layer_fwd(h1, w, conv_w, a_log, dt_bias, onorm_g, tag):
    t, d = h1.shape
    heads = d // DN_HEAD_DIM
    n = t // DN_CHUNK
    proj = _mm(h1, w["all"], name=tag + "_proj")
    qn, kn, vv = _dn_prep_fwd(proj, conv_w, heads, tag + "_prep")
    big_g, beta = _dn_gate_fwd(proj, 4 * d // (2 * LANES), a_log, dt_bias, tag + "_gate")
    gt_ = big_g[:, :heads].T.reshape(heads, n, DN_CHUNK)
    gc, gr = gt_[..., None], gt_[:, :, None, :]
    bc = beta[:, :heads].T.reshape(heads, n, DN_CHUNK)[..., None]
    o, s_all = _dn_chunk_fwd(qn, kn, vv, gc, gr, bc, tag + "_chunk")
    on = _dn_post_fwd(o, proj, onorm_g, tag + "_post")
    y = _mm(on, w["out"], name=tag + "_out")
    return y, dict(proj=proj, qn=qn, kn=kn, v=vv, gc=gc, gr=gr, bc=bc, o=o, s_all=s_all, on=on)


def _dn_layer_bwd(dy, h1, w, conv_w, a_log, dt_bias, onorm_g, sv, tag):
    t, d = h1.shape
    heads = d // DN_HEAD_DIM
    don = _mm(dy, w["out"], tb=True, name=tag + "_dout")
    g_out = _mm(sv["on"], dy, ta=True, out_dtype=BF16, name=tag + "_gwout")
    do, dz, g_on = _dn_post_bwd(sv["o"], sv["proj"], don, onorm_g, tag + "_dpost")
    dq, dk, dv, dgc, dgr, dbc = _dn_chunk_bwd(sv["qn"], sv["kn"], sv["v"], sv["gc"], sv["gr"], sv["bc"], sv["s_all"],
                                              do, tag + "_dchunk")
    pad = lambda a: jnp.pad(a.reshape(heads, t).T, ((0, 0), (0, LANES - heads)))
    d_big_g = pad(dgc) + pad(dgr)
    dab, g_alog, g_dt = _dn_gate_bwd(sv["proj"], 4 * d // (2 * LANES), d_big_g, pad(dbc), a_log, dt_bias, tag + "_dgate")
    dxq, dxk, dxv, wq, wk, wv = _dn_prep_bwd(sv["proj"], conv_w, dq, dk, dv, heads, tag + "_dprep")
    dproj = jnp.concatenate([dxq, dxk, dxv, dz, dab], axis=1)
    dh1 = _mm(dproj, w["all"], tb=True, name=tag + "_dh")
    g_all = _mm(h1, dproj, ta=True, out_dtype=BF16, name=tag + "_gwin")
    grads = dict(w_all=g_all, w_out=g_out, conv_w=jnp.concatenate([wq, wk, wv], axis=1), a_log=g_alog, dt_bias=g_dt,
                 onorm_g=g_on)
    return dh1, grads


def _sb_layer_fwd(h1, w, q_g, k_g, tag):
    t, d = h1.shape
    heads = d // SB_HEAD_DIM
    qkv = _mm(h1, w["qkv"], name=tag + "_proj")
    gq, gk = jnp.tile(q_g, (1, heads)), jnp.tile(k_g, (1, heads))
    qn, kn = _sb_norm_fwd(qkv, gq, gk, tag + "_norm")
    o = _sb_fwd(qn, kn, qkv, tag + "_attn")
    y = _mm(o, w["out"], name=tag + "_out")
    return y, dict(qkv=qkv, qn=qn, kn=kn, o=o, gq=gq, gk=gk)


def _sb_layer_bwd(dy, h1, w, q_g, k_g, sv, tag):
    t, d = h1.shape
    heads = d // SB_HEAD_DIM
    do = _mm(dy, w["out"], tb=True, name=tag + "_dout")
    g_out = _mm(sv["o"], dy, ta=True, out_dtype=BF16, name=tag + "_gwout")
    dqn, dkn, dv = _sb_bwd(sv["qn"], sv["kn"], sv["qkv"], do, tag + "_dattn")
    dqkv, g_q, g_k = _sb_norm_bwd(sv["qkv"], dqn, dkn, dv, sv["gq"], sv["gk"], tag + "_dnorm")
    fold = lambda g: jnp.sum(g.reshape(heads, SB_HEAD_DIM), axis=0, keepdims=True)
    dh1 = _mm(dqkv, w["qkv"], tb=True, name=tag + "_dh")
    g_qkv = _mm(h1, dqkv, ta=True, out_dtype=BF16, name=tag + "_gwin")
    return dh1, dict(w_qkv=g_qkv, w_out=g_out, q_norm_g=fold(g_q), k_norm_g=fold(g_k))


def _local_step(x, tgt, mod, norm1_g, norm2_g, layer_weights, layer_grads):
    depth = mod.shape[0]
    d = x.shape[1]
    saved = []
    for i in range(depth):
        mix_w, ffn_of = layer_weights(i, x)
        mv = [mod[i:i + 1, j * d:(j + 1) * d] for j in range(N_MOD)]
        sh1, sc1, gt1, sh2, sc2, gt2 = mv
        tag = "l%d" % i
        h1 = _adaln_fwd(x, norm1_g[i:i + 1], sc1, sh1, tag + "_ln1")
        if i % 2 == 0:
            p = mix_w
            y, sv = _dn_layer_fwd(h1, p, p["conv_w"], p["a_log"], p["dt_bias"], p["onorm_g"], tag + "_dn")
        else:
            p = mix_w
            y, sv = _sb_layer_fwd(h1, p, p["q_g"], p["k_g"], tag + "_sb")
        x1 = _resid_fwd(x, y, gt1, tag + "_res1")
        h2 = _adaln_fwd(x1, norm2_g[i:i + 1], sc2, sh2, tag + "_ln2")
        ffn_w = ffn_of(x1)
        u = _mm(h2, ffn_w["w_in"], name=tag + "_ffn_in")
        a = _swiglu_fwd(u, tag + "_swiglu")
        y2 = _mm(a, ffn_w["w_out"], name=tag + "_ffn_out")
        x2 = _resid_fwd(x1, y2, gt2, tag + "_res2")
        saved.append(dict(x0=x, h1=h1, y=y, mix=sv, x1=x1, h2=h2, u=u, a=a, y2=y2, mix_w=mix_w, ffn_w=ffn_w))
        x = x2

    loss, dx = _loss_fwd_bwd(x, tgt, "loss")

    dmod, dn1, dn2 = [None] * depth, [None] * depth, [None] * depth
    zero = jnp.zeros((), F32)
    for i in reversed(range(depth)):
        s = saved[i]
        mv = [mod[i:i + 1, j * d:(j + 1) * d] + zero for j in range(N_MOD)]
        sh1, sc1, gt1, sh2, sc2, gt2 = mv
        tag = "l%d" % i
        ffn_w, p = s["ffn_w"], s["mix_w"]
        dy2, dgt2 = _resid_bwd(dx, s["y2"], gt2, tag + "_dres2")
        da = _mm(dy2, ffn_w["w_out"], tb=True, name=tag + "_dffn_a")
        g_wout = _mm(s["a"], dy2, ta=True, out_dtype=BF16, name=tag + "_gffn_out")
        du = _swiglu_bwd(s["u"], da, tag + "_dswiglu")
        dh2 = _mm(du, ffn_w["w_in"], tb=True, name=tag + "_dffn_h")
        g_win = _mm(s["h2"], du, ta=True, out_dtype=BF16, name=tag + "_gffn_in")
        dx, dg2, dsc2, dsh2 = _adaln_bwd(s["x1"], dh2, dx, norm2_g[i:i + 1], sc2, tag + "_dln2")
        gt1 = gt1 + layer_grads(i, "ffn", dx, dict(w_in=g_win, w_out=g_wout))
        dy, dgt1 = _resid_bwd(dx, s["y"], gt1, tag + "_dres1")
        if i % 2 == 0:
            dh1, g_mix = _dn_layer_bwd(dy, s["h1"], p, p["conv_w"], p["a_log"], p["dt_bias"], p["onorm_g"], s["mix"],
                                       tag + "_dn")
        else:
            dh1, g_mix = _sb_layer_bwd(dy, s["h1"], p, p["q_g"], p["k_g"], s["mix"], tag + "_sb")
        dx, dg1, dsc1, dsh1 = _adaln_bwd(s["x0"], dh1, dx, norm1_g[i:i + 1], sc1, tag + "_dln1")
        dmod[i] = jnp.concatenate([dsh1, dsc1, dgt1, dsh2, dsc2, dgt2], axis=1)
        dn1[i], dn2[i] = dg1, dg2
        zero = layer_grads(i, "mix", dx, g_mix)
    return loss, dx, jnp.concatenate(dmod, axis=0), jnp.concatenate(dn1, axis=0), jnp.concatenate(dn2, axis=0)


def _axes():
    return lax.axis_index("x"), lax.axis_index("y"), lax.axis_index("c")


def _remote(src, dst, send_sem, recv_sem, dev):
    return pltpu.make_async_remote_copy(src_ref=src, dst_ref=dst, send_sem=send_sem, recv_sem=recv_sem,
                                        device_id=dev, device_id_type=MESH)


def _other_chips(x, y):
    return [(1 - x, y), (x, 1 - y), (1 - x, 1 - y)]


def _allgather_small(v, name):
    m, n = v.shape

    def body(x_ref, out_ref, send_sems, recv_sems, local_sem):
        x, y, c = _axes()
        me, sibling = (x, y, c), (x, y, 1 - c)
        chips = _other_chips(x, y)

        def rows(px, py, pc):
            return out_ref.at[pl.ds((4 * px + 2 * py + pc) * m, m), :]

        def copy(k, block, to, src=None):
            return _remote(rows(*block) if src is None else src, rows(*block), send_sems.at[k], recv_sems.at[k], to)

        mine = pltpu.make_async_copy(x_ref, rows(*me), local_sem)
        mine.start()
        first = [copy(0, me, sibling, src=x_ref)]
        first += [copy(1 + j, me, (*chip, c), src=x_ref) for j, chip in enumerate(chips)]
        for cp in first:
            cp.start()
        passed = [copy(4 + j, (*chip, c), sibling) for j, chip in enumerate(chips)]
        for j, chip in enumerate(chips):
            copy(1 + j, (*chip, c), me).wait_recv()
            passed[j].start()
        copy(0, sibling, me).wait_recv()
        for j, chip in enumerate(chips):
            copy(4 + j, (*chip, 1 - c), me).wait_recv()
        for cp in first + passed:
            cp.wait_send()
        mine.wait()

    return pl.pallas_call(
        body, out_shape=jax.ShapeDtypeStruct((8 * m, n), v.dtype),
        in_specs=[pl.BlockSpec(memory_space=pltpu.VMEM)], out_specs=pl.BlockSpec(memory_space=pltpu.VMEM),
        scratch_shapes=[pltpu.SemaphoreType.DMA((7,)), pltpu.SemaphoreType.DMA((7,)), pltpu.SemaphoreType.DMA],
        name=name, compiler_params=pltpu.CompilerParams(vmem_limit_bytes=VMEM_LIMIT))(v)


def _half(ref, h, rh):
    return ref.at[(slice(None),) * (len(ref.shape) - 2) + (pl.ds(h * rh, rh), slice(None))]


def _hbm_call(body, ins, out_shapes, n_sems, n_local, name):
    hbm = pl.BlockSpec(memory_space=pltpu.HBM)
    scratch = [pltpu.SemaphoreType.DMA((n_sems,)), pltpu.SemaphoreType.DMA((n_sems,))]
    if n_local:
        scratch.append(pltpu.SemaphoreType.DMA((n_local,)))
    return pl.pallas_call(body, out_shape=out_shapes, in_specs=[hbm] * len(ins), out_specs=[hbm] * len(out_shapes),
                          scratch_shapes=scratch, name=name)(*ins)


def _sibling_swap_halves(arrs, name):
    n = len(arrs)

    def body(*refs):
        v_refs, out_refs, (send_sems, recv_sems) = refs[:n], refs[n:2 * n], refs[2 * n:]
        x, y, c = _axes()
        cps = [_remote(_half(v_refs[i], 1 - c, arrs[i].shape[-2] // 2), out_refs[i], send_sems.at[i], recv_sems.at[i],
                       (x, y, 1 - c)) for i in range(n)]
        for cp in cps:
            cp.start()
        for cp in cps:
            cp.wait()

    outs = [jax.ShapeDtypeStruct(a.shape[:-2] + (a.shape[-2] // 2, a.shape[-1]), a.dtype) for a in arrs]
    return _hbm_call(body, arrs, outs, n, 0, name)


def _ici_src(ref, mode, slot, c):
    return _half(ref, c, ref.shape[-2] // 2) if mode == "half" else ref.at[slot]


def _ici_start(srcs, mode, name):
    n, ncp = len(srcs), 3 * len(srcs)
    lands = [lax.empty(((4,) + s.shape[:-2] + (s.shape[-2] // 2, s.shape[-1])) if mode == "half" else s.shape, s.dtype)
             for s in srcs]

    def body(*refs):
        src_refs, land_refs = refs[:n], refs[n:2 * n]
        send_sems, recv_sems = refs[2 * n:2 * n + ncp], refs[2 * n + ncp:2 * n + 2 * ncp]
        token = refs[-1]
        x, y, c = _axes()
        for i in range(n):
            for j, (cx, cy) in enumerate(_other_chips(x, y)):
                _remote(_ici_src(src_refs[i], mode, 2 * cx + cy, c), land_refs[i].at[2 * x + y], send_sems[3 * i + j],
                        recv_sems[3 * i + j], (cx, cy, c)).start()
        token[...] = jnp.zeros(token.shape, token.dtype)

    hbm, sem = pl.BlockSpec(memory_space=pltpu.HBM), pl.BlockSpec(memory_space=pltpu.SEMAPHORE)
    bufs = srcs + lands
    out_shape = tuple([pltpu.SemaphoreType.DMA(())] * (2 * ncp) + [pltpu.HBM(b.shape, b.dtype) for b in bufs]
                      + [jax.ShapeDtypeStruct((8, LANES), F32)])
    res = pl.pallas_call(
        body, name=name, out_shape=out_shape, in_specs=(hbm,) * (2 * n),
        out_specs=(sem,) * (2 * ncp) + (hbm,) * (2 * n) + (pl.BlockSpec(memory_space=pltpu.VMEM),),
        input_output_aliases={i: 2 * ncp + i for i in range(2 * n)},
        compiler_params=pltpu.CompilerParams(has_side_effects=pltpu.SideEffectType.DATAFLOW_SIDE_EFFECTING),
    )(*[pltpu.with_memory_space_constraint(b, pltpu.HBM) for b in bufs])
    return list(res[:2 * ncp]), list(res[2 * ncp:2 * ncp + 2 * n]), res[-1][0, 0]


def _ici_wait(sems, bufs, mode, after, name):
    n, ncp = len(bufs) // 2, len(sems) // 2

    def body(*refs):
        src_refs, land_refs = refs[:n], refs[n:2 * n]
        send_sems, recv_sems = refs[2 * n:2 * n + ncp], refs[2 * n + ncp:2 * n + 2 * ncp]
        x, y, c = _axes()
        for i in range(n):
            for j, (cx, cy) in enumerate(_other_chips(x, y)):
                cp = _remote(_ici_src(src_refs[i], mode, 2 * cx + cy, c), land_refs[i].at[2 * cx + cy],
                             send_sems[3 * i + j], recv_sems[3 * i + j], (cx, cy, c))
                cp.wait_send()
                cp.wait_recv()

    hbm, sem = pl.BlockSpec(memory_space=pltpu.HBM), pl.BlockSpec(memory_space=pltpu.SEMAPHORE)
    res = pl.pallas_call(
        body, name=name, out_shape=tuple(pltpu.HBM(b.shape, b.dtype) for b in bufs),
        in_specs=(hbm,) * (2 * n) + (sem,) * (2 * ncp) + (pl.BlockSpec(memory_space=pl.ANY),),
        out_specs=(hbm,) * (2 * n), input_output_aliases={i: i for i in range(2 * n)},
        compiler_params=pltpu.CompilerParams(has_side_effects=pltpu.SideEffectType.DATAFLOW_SIDE_EFFECTING),
    )(*bufs, *sems, after)
    return list(res[n:])


def _share_halves(arrs, name):
    n = len(arrs)

    def body(*refs):
        v_refs, out_refs, (send_sems, recv_sems) = refs[:n], refs[n:2 * n], refs[2 * n:]
        x, y, c = _axes()
        cps = [_remote(v_refs[i], out_refs[i], send_sems.at[i], recv_sems.at[i], (x, y, 1 - c)) for i in range(n)]
        for cp in cps:
            cp.start()
        for cp in cps:
            cp.wait()

    outs = [jax.ShapeDtypeStruct(a.shape, a.dtype) for a in arrs]
    return _hbm_call(body, arrs, outs, n, 0, name)


def _assemble_weights(mine, other, own, axis, name):
    nchips, l, rh, cs = mine.shape
    r = 2 * rh

    def body(m_ref, o_ref, w_ref, out_ref):
        x, y, c = _axes()
        is_own = pl.program_id(1) == 2 * x + y

        @pl.when(is_own)
        def _():
            out_ref[...] = w_ref[...]

        @pl.when(jnp.logical_not(is_own))
        def _():
            out_ref[pl.ds(pl.multiple_of(c * rh, rh), rh), :] = m_ref[...]
            out_ref[pl.ds(pl.multiple_of((1 - c) * rh, rh), rh), :] = o_ref[...]

    half = pl.BlockSpec((None, None, rh, cs), lambda i, s: (s, i, 0, 0))
    whole = pl.BlockSpec((None, r, cs), lambda i, s: (i, 0, 0))
    if axis == 2:
        out_spec, shape = pl.BlockSpec((None, r, cs), lambda i, s: (i, 0, s)), (l, r, nchips * cs)
    else:
        out_spec, shape = pl.BlockSpec((None, r, cs), lambda i, s: (i, s, 0)), (l, nchips * r, cs)
    return pl.pallas_call(
        body, grid=(l, nchips), in_specs=[half, half, whole], out_specs=out_spec,
        out_shape=jax.ShapeDtypeStruct(shape, mine.dtype), name=name,
        compiler_params=_cparams(("parallel", "arbitrary")))(mine, other, own)


def _sum_chips(land, own, name):
    k, r, w = land.shape
    tr = _pick_tile(r, (256, 128, 64, 32, 16))

    def body(land_ref, own_ref, o_ref):
        x, y, _ = _axes()
        me = 2 * x + y
        acc = jnp.zeros((tr, w), F32)
        for s in range(k):
            acc = acc + jnp.where(me == s, own_ref[s], land_ref[s]).astype(F32)
        o_ref[...] = acc

    spec = pl.BlockSpec((k, tr, w), lambda i: (0, i, 0))
    return pl.pallas_call(
        body, grid=(r // tr,), in_specs=[spec, spec],
        out_specs=pl.BlockSpec((tr, w), lambda i: (i, 0)), out_shape=jax.ShapeDtypeStruct((r, w), F32), name=name,
        compiler_params=_cparams(("parallel",)))(land, own)


_BIG = (("dn_w_in", 2), ("dn_w_out", 1), ("sb_w_qkv", 2), ("sb_w_out", 1), ("ffn_w_in", 2), ("ffn_w_out", 1))


def _to_chip_major(a, axis, nchips):
    l, r, c = a.shape
    if axis == 2:
        return a.reshape(l, r, nchips, c // nchips).transpose(2, 0, 1, 3)
    return a.reshape(l, nchips, r // nchips, c).transpose(1, 0, 2, 3)


def _from_chip_major(a, axis):
    n, l, r, c = a.shape
    if axis == 2:
        return a.transpose(1, 2, 0, 3).reshape(l, r, n * c)
    return a.transpose(1, 0, 2, 3).reshape(l, n * r, c)


def kernel(x, c, ada_w, ada_b, norm1_g, norm2_g, dn_w_in, dn_conv_w, dn_a_log, dn_dt_bias, dn_onorm_g, dn_w_out, sb_w_qkv, sb_q_norm_g, sb_k_norm_g, sb_w_out, ffn_w_in, ffn_w_out, loss_target, m_ada_w, m_ada_b, m_norm1_g, m_norm2_g, m_dn_w_in, m_dn_conv_w, m_dn_a_log, m_dn_dt_bias, m_dn_onorm_g, m_dn_w_out, m_sb_w_qkv, m_sb_q_norm_g, m_sb_k_norm_g, m_sb_w_out, m_ffn_w_in, m_ffn_w_out, v_ada_w, v_ada_b, v_norm1_g, v_norm2_g, v_dn_w_in, v_dn_conv_w, v_dn_a_log, v_dn_dt_bias, v_dn_onorm_g, v_dn_w_out, v_sb_w_qkv, v_sb_q_norm_g, v_sb_k_norm_g, v_sb_w_out, v_ffn_w_in, v_ffn_w_out):
    names = ("ada_w", "ada_b", "norm1_g", "norm2_g", "dn_w_in", "dn_conv_w", "dn_a_log", "dn_dt_bias", "dn_onorm_g",
             "dn_w_out", "sb_w_qkv", "sb_q_norm_g", "sb_k_norm_g", "sb_w_out", "ffn_w_in", "ffn_w_out")
    w = dict(zip(names, (ada_w, ada_b, norm1_g, norm2_g, dn_w_in, dn_conv_w, dn_a_log, dn_dt_bias, dn_onorm_g,
                         dn_w_out, sb_w_qkv, sb_q_norm_g, sb_k_norm_g, sb_w_out, ffn_w_in, ffn_w_out)))
    mom = dict(zip(names, (m_ada_w, m_ada_b, m_norm1_g, m_norm2_g, m_dn_w_in, m_dn_conv_w, m_dn_a_log, m_dn_dt_bias,
                           m_dn_onorm_g, m_dn_w_out, m_sb_w_qkv, m_sb_q_norm_g, m_sb_k_norm_g, m_sb_w_out, m_ffn_w_in,
                           m_ffn_w_out)))
    var = dict(zip(names, (v_ada_w, v_ada_b, v_norm1_g, v_norm2_g, v_dn_w_in, v_dn_conv_w, v_dn_a_log, v_dn_dt_bias,
                           v_dn_onorm_g, v_dn_w_out, v_sb_w_qkv, v_sb_q_norm_g, v_sb_k_norm_g, v_sb_w_out, v_ffn_w_in,
                           v_ffn_w_out)))
    ax, ay, ac = _axes()
    chip = 2 * ax + ay
    dev = 2 * chip + ac
    t, d = x.shape[1], x.shape[2]
    depth, ndn, nsb = ada_w.shape[0], dn_w_in.shape[0], sb_w_qkv.shape[0]
    heads = d // DN_HEAD_DIM
    mod_cols = ada_w.shape[2]
    conv_cols = dn_conv_w.shape[2]
    nchips, ndev = 4, 8

    conv_rows = ndn * DN_CONV * conv_cols // d
    pay1 = jnp.concatenate([c, dn_conv_w.reshape(conv_rows, d), jnp.zeros((8 - 1 - conv_rows, d), F32)], axis=0)
    g1 = _allgather_small(pay1, "ag_cond").reshape(ndev, 8, d)
    c_all = g1[:, 0]
    conv_full = g1[::2, 1:1 + conv_rows].reshape(nchips, ndn, DN_CONV, conv_cols).transpose(1, 2, 0, 3)
    conv_full = conv_full.reshape(ndn, DN_CONV, nchips * conv_cols)

    c16 = jnp.pad(c_all, ((0, 16 - ndev), (0, 0)))
    cond16 = _rowwise(lambda r, v: ([_silu(r[0])], []), [c16], [], [(d, F32)], [], name="cond_silu")[0]
    pay2 = jnp.concatenate([_mm(cond16, ada_w[i], name="ada_mod%d" % i)[:ndev] for i in range(depth)], axis=0)
    g2 = _allgather_small(pay2, "ag_mod").reshape(ndev, depth, ndev, mod_cols)[::2]
    mod_raw = lax.dynamic_index_in_dim(g2, dev, axis=2, keepdims=False)
    mod_raw = mod_raw.transpose(1, 0, 2).reshape(depth, nchips * mod_cols)
    mod = _rowwise(lambda r, v: ([r[0] + r[1]], []), [mod_raw, ada_b], [], [(nchips * mod_cols, F32)], [],
                   name="ada_bias")[0]

    axis_of = dict(_BIG)

    def kinds_of(layers):
        out = []
        for n, _ in _BIG:
            idx = [l // 2 for l in layers if l % 2 == (0 if n.startswith("dn_") else 1)] if n[:3] in ("dn_", "sb_") \
                else list(layers)
            if idx:
                out.append((n, min(idx), max(idx) + 1))
        return out

    two_d = lambda a: a.reshape(-1, a.shape[-1])
    both_halves = lambda mine, other, ax: jnp.where(ac == 0, jnp.concatenate([mine, other], axis=ax),
                                                    jnp.concatenate([other, mine], axis=ax))
    is_ffn = lambda k: k[0].startswith("ffn")
    rest = (kinds_of(list(range(1, depth))), ("mix", 1))
    gather_groups = [g for g in [(kinds_of([0]), ("mix", 0)), rest] if g[0]]
    reduce_groups = [g for g in [([k for k in kinds_of([0]) if not is_ffn(k)], ("mix", 0)),
                                 ([k for k in kinds_of([0]) if is_ffn(k)], ("ffn", 0)), rest] if g[0]]
    w16 = {n: w[n].astype(BF16) for n, _ in _BIG}
    gathers = []
    for gi, (sl, _) in enumerate(gather_groups):
        shards = [w16[n][lo:hi] for n, lo, hi in sl]
        sems, bufs, zero = _ici_start(shards, "half", "ag_start%d" % gi)
        gathers.append((sl, shards, sems, bufs))
        mod = mod + zero
    full = {}

    def finish_gather(gi, after):
        sl, shards, sems, bufs = gathers[gi]
        lands = _ici_wait(sems, bufs, "half", after, "ag_wait%d" % gi)
        for (n, lo, hi), shard, mine, other in zip(sl, shards, lands, _share_halves(lands, "ag_pair%d" % gi)):
            if shard.shape[-1] % LANES == 0:
                a = _assemble_weights(mine, other, shard, axis_of[n], "ag_whole%d_%s" % (gi, n))
            else:
                a = lax.dynamic_update_index_in_dim(both_halves(mine, other, 2), shard, chip, 0)
                a = _from_chip_major(a, axis_of[n])
            for l in range(lo, hi):
                full[n, l] = a[l - lo]

    padl = lambda v: jnp.pad(v[None, :], ((0, 0), (0, LANES - v.shape[0])))
    padc = lambda a: jnp.pad(a, ((0, 0), (0, LANES - a.shape[1])))

    def layer_weights(i, after):
        for gi, (_, trigger) in enumerate(gather_groups):
            if trigger == ("mix", i):
                finish_gather(gi, after)

        def ffn_of(after_mixer):
            for gi, (_, trigger) in enumerate(gather_groups):
                if trigger == ("ffn", i):
                    finish_gather(gi, after_mixer)
            return dict(w_in=full["ffn_w_in", i], w_out=full["ffn_w_out", i])
        j = i // 2
        if i % 2 == 0:
            wi = full["dn_w_in", j]
            w_all = jnp.concatenate([wi[:, :4 * d], padc(wi[:, 4 * d:4 * d + heads]), padc(wi[:, 4 * d + heads:])],
                                    axis=1)
            mix = dict(all=w_all, out=full["dn_w_out", j], conv_w=conv_full[j], a_log=padl(dn_a_log[j]),
                       dt_bias=padl(dn_dt_bias[j]), onorm_g=dn_onorm_g[j][None])
        else:
            mix = dict(qkv=full["sb_w_qkv", j], out=full["sb_w_out", j], q_g=sb_q_norm_g[j][None],
                       k_g=sb_k_norm_g[j][None])
        return mix, ffn_of

    g_dn, g_sb, g_ffn = [None] * ndn, [None] * nsb, [None] * depth
    reduces = {}

    def gw_in(g):
        ga = g["w_all"]
        return jnp.concatenate([ga[:, :4 * d], ga[:, 4 * d:4 * d + heads], ga[:, 4 * d + LANES:4 * d + LANES + heads]],
                               axis=1)
    layer_grad = dict(dn_w_in=lambda j: gw_in(g_dn[j]), dn_w_out=lambda j: g_dn[j]["w_out"],
                      sb_w_qkv=lambda j: g_sb[j]["w_qkv"], sb_w_out=lambda j: g_sb[j]["w_out"],
                      ffn_w_in=lambda l: g_ffn[l]["w_in"], ffn_w_out=lambda l: g_ffn[l]["w_out"])

    def layer_grads(i, part, after, g):
        if part == "ffn":
            g_ffn[i] = g
        else:
            (g_dn if i % 2 == 0 else g_sb)[i // 2] = g
        zero = jnp.zeros((), F32)
        for gi, (sl, trigger) in enumerate(reduce_groups):
            if trigger == (part, i):
                parts = [_to_chip_major(jnp.stack([layer_grad[n](j) for j in range(lo, hi)]), axis_of[n], nchips)
                         for n, lo, hi in sl]
                pairs = []
                for (n, _, _), p, fs in zip(sl, parts, _sibling_swap_halves(parts, "gr_pair%d" % gi)):
                    rh = fs.shape[2]
                    own = lax.dynamic_slice_in_dim(p, ac * rh, rh, axis=2)
                    pairs.append(_rowwise(lambda r, v: ([r[0].astype(F32) + r[1].astype(F32)], []),
                                          [two_d(own), two_d(fs)], [], [(fs.shape[-1], BF16)], [],
                                          name="gr_pair_add%d_%s" % (gi, n))[0].reshape(fs.shape))
                sems, bufs, zero = _ici_start(pairs, "slot", "gr_start%d" % gi)
                reduces[gi] = (sl, pairs, sems, bufs)
        return zero

    loss_local, grad_x, dmod, g_n1, g_n2 = _local_step(x[0], loss_target[0], mod, norm1_g, norm2_g, layer_weights,
                                                       layer_grads)
    loss = lax.psum(loss_local, ("x", "y", "c"))

    g_conv = jnp.stack([g["conv_w"] for g in g_dn])
    misc = jnp.concatenate([jnp.concatenate([g["onorm_g"] for g in g_dn], axis=1),
                            jnp.concatenate([g["a_log"] for g in g_dn], axis=1),
                            jnp.concatenate([g["dt_bias"] for g in g_dn], axis=1),
                            jnp.concatenate([g["q_norm_g"] for g in g_sb], axis=1),
                            jnp.concatenate([g["k_norm_g"] for g in g_sb], axis=1)], axis=1)
    misc = jnp.pad(misc, ((0, 0), (0, -misc.shape[1] % d))).reshape(-1, d)
    small = [dmod.reshape(-1, d), g_n1, g_n2, g_conv.reshape(-1, d), misc]
    small_rows = [s.shape[0] for s in small]
    pad_rows = -sum(small_rows) % 8
    pay3 = jnp.concatenate(small + [jnp.zeros((pad_rows, d), F32)], axis=0)
    nrow3 = pay3.shape[0]
    g3 = _allgather_small(pay3, "ag_small")
    summed = _rowwise(lambda r, v: ([], [_colsum(r[0])]), [g3.reshape(ndev, nrow3 * d)], [], [], [(1, nrow3 * d)],
                      name="small_sum")[0].reshape(nrow3, d)
    offs = [0]
    for n_ in small_rows:
        offs.append(offs[-1] + n_)
    grads = {}
    grads["ada_b"] = summed[offs[0]:offs[1]].reshape(depth, N_MOD * d)
    grads["norm1_g"] = summed[offs[1]:offs[2]]
    grads["norm2_g"] = summed[offs[2]:offs[3]]
    conv_sum = summed[offs[3]:offs[4]].reshape(ndn, DN_CONV, nchips * conv_cols)
    grads["dn_conv_w"] = lax.dynamic_slice_in_dim(conv_sum, chip * conv_cols, conv_cols, axis=2)
    mrow = summed[offs[4]:offs[5]].reshape(-1)
    o = 0
    grads["dn_onorm_g"] = mrow[o:o + ndn * DN_HEAD_DIM].reshape(ndn, DN_HEAD_DIM)
    o += ndn * DN_HEAD_DIM
    grads["dn_a_log"] = mrow[o:o + ndn * LANES].reshape(ndn, LANES)[:, :heads]
    o += ndn * LANES
    grads["dn_dt_bias"] = mrow[o:o + ndn * LANES].reshape(ndn, LANES)[:, :heads]
    o += ndn * LANES
    grads["sb_q_norm_g"] = mrow[o:o + nsb * SB_HEAD_DIM].reshape(nsb, SB_HEAD_DIM)
    o += nsb * SB_HEAD_DIM
    grads["sb_k_norm_g"] = mrow[o:o + nsb * SB_HEAD_DIM].reshape(nsb, SB_HEAD_DIM)
    dmod_all = g3.reshape(ndev, nrow3, d)[:, :small_rows[0]].reshape(ndev, depth, N_MOD * d)
    dmod_mine = lax.dynamic_slice_in_dim(dmod_all, chip * mod_cols, mod_cols, axis=2)
    dmod16 = jnp.pad(dmod_mine, ((0, 16 - ndev), (0, 0), (0, 0)))
    grads["ada_w"] = jnp.stack([_mm(cond16, dmod16[:, i], ta=True, name="ada_gw%d" % i) for i in range(depth)])

    pieces = {n: [] for n, _ in _BIG}
    for gi in reversed(range(len(reduce_groups))):
        if gi not in reduces:
            continue
        sl, pairs, sems, bufs = reduces[gi]
        lands = _ici_wait(sems, bufs, "slot", grad_x, "gr_wait%d" % gi)
        flat = lambda a: a.reshape(nchips, -1, a.shape[-1])
        reduced = [_sum_chips(flat(land), flat(pair), "gr_chip_add%d_%s" % (gi, n))
                   for (n, _, _), land, pair in zip(sl, lands, pairs)]
        for (n, lo, hi), mine, other in zip(sl, reduced, _share_halves(reduced, "gr_share%d" % gi)):
            shape = (hi - lo, -1, mine.shape[-1])
            pieces[n].append((lo, both_halves(mine.reshape(shape), other.reshape(shape), 1)))
    for n, _ in _BIG:
        grads[n] = jnp.concatenate([p for _, p in sorted(pieces[n], key=lambda t: t[0])], axis=0)

    delta, new_m, new_v = {}, {}, {}
    for n in names:
        delta[n], new_m[n], new_v[n] = _adamw(w[n], grads[n], mom[n], var[n], "adamw_" + n)
    return (loss, grad_x[None], *[grads[n] for n in names], *[delta[n] for n in names], *[new_m[n] for n in names],
            *[new_v[n] for n in names])
```

```python
import functools

import jax
import jax.numpy as jnp
from jax import lax
from jax.experimental import pallas as pl
from jax.experimental.pallas import tpu as pltpu

F32 = jnp.float32
BF16 = jnp.bfloat16
HI = lax.Precision.HIGHEST
MESH = pl.DeviceIdType.MESH

EPS = 1e-6
N_MOD = 6
DN_HEAD_DIM = 128
DN_CONV = 4
DN_CHUNK = 64
DN_GROUP = 8
DN_GATE_ROWS = 256
SB_HEAD_DIM = 64
SB_BLOCK = 128
SB_QBLOCK = 512
SB_PART = 128
LANES = 128
VMEM_LIMIT = 56 * 1024 * 1024
MM_BLOCK_BUDGET = 38 * 1024 * 1024
ROW_BLOCK_BUDGET = 20 * 1024 * 1024

ADAM_LR = 0.001
ADAM_B1 = 0.9
ADAM_B2 = 0.999
ADAM_EPS = 1e-08
ADAM_WD = 0.01
ADAM_STEP = 10


def _cparams(sem=None):
    return pltpu.CompilerParams(dimension_semantics=sem, vmem_limit_bytes=VMEM_LIMIT)


def _sigmoid(x):
    return 1.0 / (1.0 + jnp.exp(-x))


def _silu(x):
    return x * _sigmoid(x)


def _dsilu(x):
    s = _sigmoid(x)
    return s * (1.0 + x * (1.0 - s))


def _softplus(x):
    return jnp.maximum(x, 0.0) + jnp.log1p(jnp.exp(-jnp.abs(x)))


def _pick_tile(n, prefs):
    for t in prefs:
        if n % t == 0:
            return t
    return n


def _mm(a, b, *, ta=False, tb=False, out_dtype=F32, name):
    m = a.shape[1] if ta else a.shape[0]
    k = a.shape[0] if ta else a.shape[1]
    n = b.shape[0] if tb else b.shape[1]
    assert (b.shape[1] if tb else b.shape[0]) == k
    size = lambda dt: jnp.dtype(dt).itemsize
    best = None
    for tm in sorted({t for t in (m, 2048, 1024, 512, 256, 128) if m % t == 0 and (t % 128 == 0 or t == m)}):
        for tn in sorted({t for t in (n, 2816, 1408, 1024, 768, 512, 256, 128) if n % t == 0 and (t % 128 == 0 or t == n)}):
            need = 2 * (tm * k * size(a.dtype) + tn * k * size(b.dtype) + tm * tn * size(out_dtype))
            if need <= MM_BLOCK_BUDGET and (best is None or tm * tn > best[0] * best[1]):
                best = (tm, tn)
    tm, tn = best
    dims = (((0 if ta else 1,), (1 if tb else 0,)), ((), ()))

    def body(a_ref, b_ref, o_ref):
        av = a_ref[...].astype(BF16)
        bv = b_ref[...].astype(BF16)
        o_ref[...] = lax.dot_general(av, bv, dims, preferred_element_type=F32).astype(o_ref.dtype)

    a_spec = pl.BlockSpec((k, tm), lambda i, j: (0, i)) if ta else pl.BlockSpec((tm, k), lambda i, j: (i, 0))
    b_spec = pl.BlockSpec((tn, k), lambda i, j: (j, 0)) if tb else pl.BlockSpec((k, tn), lambda i, j: (0, j))
    return pl.pallas_call(
        body, grid=(m // tm, n // tn), in_specs=[a_spec, b_spec],
        out_specs=pl.BlockSpec((tm, tn), lambda i, j: (i, j)),
        out_shape=jax.ShapeDtypeStruct((m, n), out_dtype), name=name,
        compiler_params=_cparams(("parallel", "parallel")))(a, b)


def _rowwise(body, rows, vecs, outs, accs, *, name, tr=None):
    rows = [r if isinstance(r, tuple) else (r, r.shape[1], 0) for r in rows]
    nrows = rows[0][0].shape[0]
    if tr is None:
        per_row = sum(w * jnp.dtype(a.dtype).itemsize for a, w, _ in rows) + sum(
            w * jnp.dtype(dt).itemsize for w, dt in outs)
        tr = next((t for t in (2048, 1024, 512, 256) if nrows % t == 0 and 2 * t * per_row <= ROW_BLOCK_BUDGET), 256)
    tr = tr if nrows % tr == 0 else nrows
    nr, nv, no = len(rows), len(vecs), len(outs)

    def kern(*refs):
        r_in, v_in = refs[:nr], refs[nr:nr + nv]
        o_refs, a_refs = refs[nr + nv:nr + nv + no], refs[nr + nv + no:]
        res_o, res_a = body([r[...] for r in r_in], [v[...] for v in v_in])
        for o, val in zip(o_refs, res_o):
            o[...] = val.astype(o.dtype)
        if a_refs:
            @pl.when(pl.program_id(0) == 0)
            def _():
                for a in a_refs:
                    a[...] = jnp.zeros(a.shape, a.dtype)
            for a, val in zip(a_refs, res_a):
                a[...] += val

    in_specs = [pl.BlockSpec((tr, w), functools.partial(lambda i, cb: (i, cb), cb=cb)) for _, w, cb in rows]
    in_specs += [pl.BlockSpec(v.shape, functools.partial(lambda i, nd: (0,) * nd, nd=v.ndim)) for v in vecs]
    out_specs = [pl.BlockSpec((tr, w), lambda i: (i, 0)) for w, _ in outs]
    out_specs += [pl.BlockSpec(s, lambda i: (0, 0)) for s in accs]
    out_shape = [jax.ShapeDtypeStruct((nrows, w), dt) for w, dt in outs]
    out_shape += [jax.ShapeDtypeStruct(s, F32) for s in accs]
    res = pl.pallas_call(
        kern, grid=(nrows // tr,), in_specs=in_specs, out_specs=out_specs, out_shape=out_shape, name=name,
        compiler_params=_cparams(("arbitrary",) if accs else ("parallel",)))(*[r[0] for r in rows], *vecs)
    return res


def _colsum(v):
    return jnp.sum(v, axis=0, keepdims=True)


def _adaln_fwd(x, g, sc, sh, name):
    def body(r, v):
        (xv,), (gv, scv, shv) = r, v
        rs = lax.rsqrt(jnp.mean(xv * xv, axis=-1, keepdims=True) + EPS)
        return [(xv * rs) * gv * (1.0 + scv) + shv], []
    return _rowwise(body, [x], [g, sc, sh], [(x.shape[1], BF16)], [], name=name)[0]


def _adaln_bwd(x, dh, dxr, g, sc, name):
    d = x.shape[1]

    def body(r, v):
        (xv, dhv, dxv), (gv, scv) = r, v
        rs = lax.rsqrt(jnp.mean(xv * xv, axis=-1, keepdims=True) + EPS)
        nv = xv * rs
        dn = dhv * (gv * (1.0 + scv))
        dx = rs * (dn - nv * jnp.mean(dn * nv, axis=-1, keepdims=True)) + dxv
        dhn = dhv * nv
        return [dx], [_colsum(dhn * (1.0 + scv)), _colsum(dhn * gv), _colsum(dhv)]
    return _rowwise(body, [x, dh, dxr], [g, sc], [(d, F32)], [(1, d)] * 3, name=name)


def _resid_fwd(x, y, gt, name):
    def body(r, v):
        return [r[0] + v[0] * r[1]], []
    return _rowwise(body, [x, y], [gt], [(x.shape[1], F32)], [], name=name)[0]


def _resid_bwd(dx, y, gt, name):
    d = dx.shape[1]

    def body(r, v):
        return [v[0] * r[0]], [_colsum(r[0] * r[1])]
    return _rowwise(body, [dx, y], [gt], [(d, BF16)], [(1, d)], name=name)


def _swiglu_fwd(u, name):
    f = u.shape[1] // 2

    def body(r, v):
        uv = r[0]
        return [_silu(uv[:, :f]) * uv[:, f:]], []
    return _rowwise(body, [u], [], [(f, BF16)], [], name=name)[0]


def _swiglu_bwd(u, da, name):
    f = u.shape[1] // 2

    def body(r, v):
        uv, dav = r
        gate, up = uv[:, :f], uv[:, f:]
        return [jnp.concatenate([dav * up * _dsilu(gate), dav * _silu(gate)], axis=1)], []
    return _rowwise(body, [u, da], [], [(2 * f, BF16)], [], name=name)[0]


def _loss_fwd_bwd(y, tgt, name):
    d = y.shape[1]

    def body(r, v):
        err = r[0] - r[1]
        part = jnp.sum(jnp.sum(err * err, axis=1, keepdims=True), axis=0, keepdims=True) * (0.5 / d)
        return [err * (1.0 / d)], [jnp.broadcast_to(part, (1, LANES))]
    dy, acc = _rowwise(body, [y, tgt], [], [(d, F32)], [(1, LANES)], name=name)
    return acc[0, 0], dy


def _head_means(v):
    row = lax.broadcasted_iota(jnp.int32, (LANES, LANES), 0)
    col = lax.broadcasted_iota(jnp.int32, (LANES, LANES), 1)
    same = ((row // SB_HEAD_DIM) == (col // SB_HEAD_DIM)).astype(F32)
    parts = [jnp.dot(v[:, g * LANES:(g + 1) * LANES], same, precision=HI, preferred_element_type=F32)
             for g in range(v.shape[1] // LANES)]
    return jnp.concatenate(parts, axis=1) * (1.0 / SB_HEAD_DIM)


def _sb_norm_fwd(qkv, gq, gk, name):
    d = qkv.shape[1] // 3

    def body(r, v):
        return [x * lax.rsqrt(_head_means(x * x) + EPS) * g for x, g in zip(r, v)], []
    return _rowwise(body, [(qkv, d, 0), (qkv, d, 1)], [gq, gk], [(d, F32), (d, F32)], [], name=name)


def _sb_norm_bwd(qkv, dqn, dkn, dv, gq, gk, name):
    d = qkv.shape[1] // 3

    def body(r, v):
        outs, accs = [], []
        for x, dy, g in ((r[0], r[2], v[0]), (r[1], r[3], v[1])):
            rs = lax.rsqrt(_head_means(x * x) + EPS)
            nv = x * rs
            dn = dy * g
            outs.append(rs * (dn - nv * _head_means(dn * nv)))
            accs.append(_colsum(dy * nv))
        return [jnp.concatenate(outs + [r[4]], axis=1)], accs
    return _rowwise(body, [(qkv, d, 0), (qkv, d, 1), dqn, dkn, dv], [gq, gk], [(3 * d, BF16)], [(1, d), (1, d)],
                    name=name)


def _head_tiles(a):
    return [a[:, h * DN_HEAD_DIM:(h + 1) * DN_HEAD_DIM] for h in range(a.shape[1] // DN_HEAD_DIM)]


def _dn_post_fwd(o, proj, g, name):
    d = o.shape[1]

    def body(r, v):
        outs = []
        for ov, zv in zip(_head_tiles(r[0]), _head_tiles(r[1])):
            rs = lax.rsqrt(jnp.mean(ov * ov, axis=-1, keepdims=True) + EPS)
            outs.append(ov * rs * v[0] * _silu(zv))
        return [jnp.concatenate(outs, axis=1)], []
    return _rowwise(body, [o, (proj, d, 3)], [g], [(d, BF16)], [], name=name)[0]


def _dn_post_bwd(o, proj, don, g, name):
    d = o.shape[1]

    def body(r, v):
        dos, dzs, dg = [], [], jnp.zeros((1, DN_HEAD_DIM), F32)
        for ov, zv, dv in zip(_head_tiles(r[0]), _head_tiles(r[1]), _head_tiles(r[2])):
            rs = lax.rsqrt(jnp.mean(ov * ov, axis=-1, keepdims=True) + EPS)
            nv = ov * rs
            s = _silu(zv)
            dn = dv * v[0] * s
            dos.append(rs * (dn - nv * jnp.mean(dn * nv, axis=-1, keepdims=True)))
            dzs.append(dv * nv * v[0] * _dsilu(zv))
            dg = dg + _colsum(dv * nv * s)
        return [jnp.concatenate(dos, axis=1), jnp.concatenate(dzs, axis=1)], [dg]
    return _rowwise(body, [o, (proj, d, 3), don], [g], [(d, F32), (d, BF16)], [(1, DN_HEAD_DIM)], name=name)


def _chunk_tri(tr, upper):
    row = lax.broadcasted_iota(jnp.int32, (tr, tr), 0)
    col = lax.broadcasted_iota(jnp.int32, (tr, tr), 1)
    same = (row // DN_CHUNK) == (col // DN_CHUNK)
    return (same & ((row <= col) if upper else (row >= col))).astype(F32)


def _dn_gate_fwd(proj, colblk, a_log, dt_bias, name):
    def body(r, v):
        ab = r[0]
        a, b = ab[:, :LANES], ab[:, LANES:]
        g = -jnp.exp(v[0]) * _softplus(a + v[1])
        big_g = jnp.dot(_chunk_tri(g.shape[0], False), g, precision=HI, preferred_element_type=F32)
        return [big_g, _sigmoid(b)], []
    return _rowwise(body, [(proj, 2 * LANES, colblk)], [a_log, dt_bias], [(LANES, F32), (LANES, F32)], [], name=name,
                    tr=DN_GATE_ROWS)


def _dn_gate_bwd(proj, colblk, d_big_g, dbeta, a_log, dt_bias, name):
    def body(r, v):
        ab, dgc, dbt = r
        a, b = ab[:, :LANES], ab[:, LANES:]
        dg = jnp.dot(_chunk_tri(dgc.shape[0], True), dgc, precision=HI, preferred_element_type=F32)
        na = -jnp.exp(v[0])
        pre = a + v[1]
        da = dg * na * _sigmoid(pre)
        beta = _sigmoid(b)
        db = dbt * beta * (1.0 - beta)
        return [jnp.concatenate([da, db], axis=1)], [_colsum(dg * na * _softplus(pre)), _colsum(da)]
    return _rowwise(body, [(proj, 2 * LANES, colblk), d_big_g, dbeta], [a_log, dt_bias],
                    [(2 * LANES, BF16)], [(1, LANES), (1, LANES)], name=name, tr=DN_GATE_ROWS)


def _shift_rows(x, s):
    if s == 0:
        return x
    t = x.shape[0]
    row = lax.broadcasted_iota(jnp.int32, x.shape, 0)
    rolled = pltpu.roll(x, s % t, axis=0)
    return jnp.where((row >= s) if s > 0 else (row < t + s), rolled, 0.0)


def _rnd(x):
    return x.astype(BF16).astype(F32)


def _conv(x, c_ref):
    c = x * _rnd(c_ref[DN_CONV - 1:DN_CONV, :])
    for s in range(1, DN_CONV):
        c = c + _shift_rows(x, s) * _rnd(c_ref[DN_CONV - 1 - s:DN_CONV - s, :])
    return c


def _dn_prep_fwd(proj, conv_w, heads, name):
    t = proj.shape[0]
    d = heads * DN_HEAD_DIM

    def body(xq, xk, xv, cq, ck, cv, q_ref, k_ref, v_ref):
        for x_ref, c_ref, o_ref, norm in ((xq, cq, q_ref, True), (xk, ck, k_ref, True), (xv, cv, v_ref, False)):
            y = _silu(_conv(_rnd(x_ref[...]), c_ref))
            if norm:
                y = y * lax.rsqrt(jnp.sum(y * y, axis=-1, keepdims=True) + EPS)
            o_ref[...] = y

    xs = [pl.BlockSpec((t, DN_HEAD_DIM), functools.partial(lambda h, o: (0, h + o), o=o * heads)) for o in range(3)]
    cs = [pl.BlockSpec((DN_CONV, DN_HEAD_DIM), functools.partial(lambda h, o: (0, h + o), o=o * heads)) for o in range(3)]
    return pl.pallas_call(
        body, grid=(heads,), in_specs=xs + cs,
        out_specs=[pl.BlockSpec((t, DN_HEAD_DIM), lambda h: (0, h))] * 3,
        out_shape=[jax.ShapeDtypeStruct((t, d), F32)] * 3, name=name,
        compiler_params=_cparams(("parallel",)))(proj, proj, proj, conv_w, conv_w, conv_w)


def _dn_prep_bwd(proj, conv_w, dq, dk, dv, heads, name):
    t = proj.shape[0]
    d = heads * DN_HEAD_DIM

    def body(xq, xk, xv, cq, ck, cv, gq, gk, gv, oq, ok, ov, wq, wk, wv):
        for x_ref, c_ref, g_ref, o_ref, w_ref, norm in ((xq, cq, gq, oq, wq, True), (xk, ck, gk, ok, wk, True),
                                                        (xv, cv, gv, ov, wv, False)):
            x, dy = _rnd(x_ref[...]), g_ref[...]
            c = _conv(x, c_ref)
            if norm:
                y = _silu(c)
                rs = lax.rsqrt(jnp.sum(y * y, axis=-1, keepdims=True) + EPS)
                yn = y * rs
                dy = rs * (dy - yn * jnp.sum(dy * yn, axis=-1, keepdims=True))
            dc = _rnd(dy * _dsilu(c))
            dx = dc * _rnd(c_ref[DN_CONV - 1:DN_CONV, :])
            w_ref[DN_CONV - 1:DN_CONV, :] = _colsum(dc * x)
            for s in range(1, DN_CONV):
                dx = dx + _shift_rows(dc, -s) * _rnd(c_ref[DN_CONV - 1 - s:DN_CONV - s, :])
                w_ref[DN_CONV - 1 - s:DN_CONV - s, :] = _colsum(dc * _shift_rows(x, s))
            o_ref[...] = dx.astype(o_ref.dtype)

    xs = [pl.BlockSpec((t, DN_HEAD_DIM), functools.partial(lambda h, o: (0, h + o), o=o * heads)) for o in range(3)]
    cs = [pl.BlockSpec((DN_CONV, DN_HEAD_DIM), functools.partial(lambda h, o: (0, h + o), o=o * heads)) for o in range(3)]
    hs = pl.BlockSpec((t, DN_HEAD_DIM), lambda h: (0, h))
    ws = pl.BlockSpec((DN_CONV, DN_HEAD_DIM), lambda h: (0, h))
    return pl.pallas_call(
        body, grid=(heads,), in_specs=xs + cs + [hs] * 3, out_specs=[hs] * 3 + [ws] * 3,
        out_shape=[jax.ShapeDtypeStruct((t, d), BF16)] * 3 + [jax.ShapeDtypeStruct((DN_CONV, d), F32)] * 3, name=name,
        compiler_params=_cparams(("parallel",)))(proj, proj, proj, conv_w, conv_w, conv_w, dq, dk, dv)


X3 = "three bf16 passes"


def _bdot(a, b, dims, prec):
    if prec != X3:
        return lax.dot_general(a, b, dims, precision=prec, preferred_element_type=F32)
    a1, b1 = a.astype(BF16), b.astype(BF16)
    a2, b2 = (a - a1.astype(F32)).astype(BF16), (b - b1.astype(F32)).astype(BF16)
    dot = lambda x, y: lax.dot_general(x, y, dims, preferred_element_type=F32)
    return dot(a1, b1) + (dot(a1, b2) + dot(a2, b1))


def _bmm(a, b, prec=None):
    return _bdot(a, b, (((2,), (1,)), ((0,), (0,))), prec)


def _bmm_nt(a, b, prec=None):
    return _bdot(a, b, (((2,), (2,)), ((0,), (0,))), prec)


def _bmm_tn(a, b, prec=None):
    return _bdot(a, b, (((1,), (1,)), ((0,), (0,))), prec)


def _dn_local(qg, kg, vg, gc, gr, bt):
    c = qg.shape[1]
    row = lax.broadcasted_iota(jnp.int32, (c, c), 0)
    col = lax.broadcasted_iota(jnp.int32, (c, c), 1)
    incl, strict = (row >= col)[None], (row > col)[None]
    decay = jnp.where(incl, jnp.exp(jnp.where(incl, gc - gr, 0.0)), 0.0)
    kb = kg * bt
    vb = vg * bt
    m = _bmm_nt(kb.astype(BF16), kg.astype(BF16))
    a = jnp.where(strict, m * decay, 0.0)
    bp = -a
    tm = jnp.where((row == col)[None], 1.0, 0.0) + bp
    steps = max(1, (c - 1).bit_length()) - 1
    for _ in range(steps):
        bp = _bmm(bp, bp, X3)
        tm = tm + _bmm(tm, bp, X3)
    eg = jnp.exp(gc)
    glast = gc[:, c - 1:c, :]
    erel = jnp.exp(glast - gc)
    kbg = kb * eg
    qm = _bmm_nt(qg.astype(BF16), kg.astype(BF16))
    return dict(decay=decay, strict=strict, kb=kb, vb=vb, m=m, tm=tm, eg=eg, erel=erel, kbg=kbg, qm=qm,
                u=_bmm(tm, vb, X3), w=_bmm(tm, kbg, X3), qd=qg * eg, kt=kg * erel, gl=jnp.exp(glast))


def _dot(a, b):
    return jnp.dot(a.astype(BF16), b.astype(BF16), preferred_element_type=F32)


def _dot_nt(a, b):
    return lax.dot_general(a.astype(BF16), b.astype(BF16), (((1,), (1,)), ((), ())), preferred_element_type=F32)


def _dot_tn(a, b):
    return lax.dot_general(a.astype(BF16), b.astype(BF16), (((0,), (0,)), ((), ())), preferred_element_type=F32)


def _dn_chunk_specs(t, heads):
    n = t // DN_CHUNK
    hs = pl.BlockSpec((t, DN_HEAD_DIM), lambda h: (0, h))
    gcs = pl.BlockSpec((1, n, DN_CHUNK, 1), lambda h: (h, 0, 0, 0))
    grs = pl.BlockSpec((1, n, 1, DN_CHUNK), lambda h: (h, 0, 0, 0))
    ss = pl.BlockSpec((1, n, DN_HEAD_DIM, DN_HEAD_DIM), lambda h: (h, 0, 0, 0))
    return n, hs, gcs, grs, ss


def _dn_chunk_fwd(q, k, v, gc, gr, bc, name):
    t, d = q.shape
    heads = d // DN_HEAD_DIM
    c, dk = DN_CHUNK, DN_HEAD_DIM
    n, hs, gcs, grs, ss = _dn_chunk_specs(t, heads)
    nb = min(DN_GROUP, n)
    scale = dk ** -0.5

    def body(q_ref, k_ref, v_ref, gc_ref, gr_ref, b_ref, o_ref, s_ref, u_s, w_s, at_s, qd_s, kt_s, gl_s):
        def group(gi, carry):
            r0 = pl.multiple_of(gi * (nb * c), nb * c)
            n0 = gi * nb
            ld = lambda ref: ref[pl.ds(r0, nb * c), :].reshape(nb, c, dk)
            loc = _dn_local(ld(q_ref) * scale, ld(k_ref), ld(v_ref), gc_ref[0, pl.ds(n0, nb)],
                            gr_ref[0, pl.ds(n0, nb)], b_ref[0, pl.ds(n0, nb)])
            u_s[pl.ds(n0, nb)] = loc["u"]
            w_s[pl.ds(n0, nb)] = loc["w"]
            at_s[pl.ds(n0, nb)] = loc["qm"] * loc["decay"]
            qd_s[pl.ds(n0, nb)] = loc["qd"]
            kt_s[pl.ds(n0, nb)] = loc["kt"]
            gl_s[pl.ds(n0, nb)] = loc["gl"]
            return carry
        lax.fori_loop(0, n // nb, group, 0)

        def chunk(i, s):
            s_ref[0, i] = s
            vnew = u_s[i] - _dot(w_s[i], s)
            o = _dot(qd_s[i], s) + _dot(at_s[i], vnew)
            o_ref[pl.ds(pl.multiple_of(i * c, c), c), :] = o
            return s * gl_s[i] + _dot_tn(kt_s[i], vnew)
        lax.fori_loop(0, n, chunk, jnp.zeros((dk, dk), F32))

    scratch = [pltpu.VMEM((n, c, dk), F32), pltpu.VMEM((n, c, dk), F32), pltpu.VMEM((n, c, c), F32),
               pltpu.VMEM((n, c, dk), F32), pltpu.VMEM((n, c, dk), F32), pltpu.VMEM((n, 1, 1), F32)]
    return pl.pallas_call(
        body, grid=(heads,), in_specs=[hs, hs, hs, gcs, grs, gcs], out_specs=[hs, ss],
        out_shape=[jax.ShapeDtypeStruct((t, d), F32), jax.ShapeDtypeStruct((heads, n, dk, dk), F32)],
        scratch_shapes=scratch, name=name, compiler_params=_cparams(("parallel",)))(q, k, v, gc, gr, bc)


def _dn_chunk_bwd(q, k, v, gc, gr, bc, s_all, do, name):
    t, d = q.shape
    heads = d // DN_HEAD_DIM
    c, dk = DN_CHUNK, DN_HEAD_DIM
    n, hs, gcs, grs, ss = _dn_chunk_specs(t, heads)
    nb = min(DN_GROUP, n)
    scale = dk ** -0.5

    def body(q_ref, k_ref, v_ref, gc_ref, gr_ref, b_ref, s_ref, do_ref,
             dq_ref, dk_ref, dv_ref, dgc_ref, dgr_ref, db_ref,
             u_s, w_s, att_s, qd_s, kt_s, gl_s, du_s, dw_s, dat_s, dqd_s, dkt_s, dgl_s):
        def load_group(gi):
            r0 = pl.multiple_of(gi * (nb * c), nb * c)
            n0 = gi * nb
            ld = lambda ref: ref[pl.ds(r0, nb * c), :].reshape(nb, c, dk)
            qg, kg, vg = ld(q_ref) * scale, ld(k_ref), ld(v_ref)
            gcv, grv, bt = gc_ref[0, pl.ds(n0, nb)], gr_ref[0, pl.ds(n0, nb)], b_ref[0, pl.ds(n0, nb)]
            return r0, n0, qg, kg, vg, gcv, grv, bt, _dn_local(qg, kg, vg, gcv, grv, bt)

        def group_a(gi, carry):
            _, n0, qg, kg, _, gcv, grv, _, loc = load_group(gi)
            row = lax.broadcasted_iota(jnp.int32, (c, c), 0)
            col = lax.broadcasted_iota(jnp.int32, (c, c), 1)
            upper = (col >= row)[None]
            decay_t = jnp.where(upper, jnp.exp(jnp.where(upper, grv - gcv, 0.0)), 0.0)
            u_s[pl.ds(n0, nb)] = loc["u"]
            w_s[pl.ds(n0, nb)] = loc["w"]
            att_s[pl.ds(n0, nb)] = _bmm_nt(kg.astype(BF16), qg.astype(BF16)) * decay_t
            qd_s[pl.ds(n0, nb)] = loc["qd"]
            kt_s[pl.ds(n0, nb)] = loc["kt"]
            gl_s[pl.ds(n0, nb)] = loc["gl"]
            return carry
        lax.fori_loop(0, n // nb, group_a, 0)

        def chunk_b(it, ds_next):
            i = n - 1 - it
            s = s_ref[0, i]
            dov = do_ref[pl.ds(pl.multiple_of(i * c, c), c), :]
            w, kt = w_s[i], kt_s[i]
            vnew = u_s[i] - _dot(w, s)
            dvnew = _dot(att_s[i], dov) + _dot(kt, ds_next)
            du_s[i] = dvnew
            dw_s[i] = -_dot_nt(dvnew, s)
            dat_s[i] = _dot_nt(dov, vnew)
            dqd_s[i] = _dot_nt(dov, s)
            dkt_s[i] = _dot_nt(vnew, ds_next)
            dgl_s[i] = jnp.sum(jnp.sum(ds_next * s, axis=1, keepdims=True), axis=0, keepdims=True)
            return ds_next * gl_s[i] + _dot_tn(qd_s[i], dov) - _dot_tn(w, dvnew)
        lax.fori_loop(0, n, chunk_b, jnp.zeros((dk, dk), F32))

        def group_c(gi, carry):
            r0, n0, qg, kg, vg, gcv, grv, bt, loc = load_group(gi)
            sl = pl.ds(n0, nb)
            du, dw, dat, dqd, dkt, dgl = du_s[sl], dw_s[sl], dat_s[sl], dqd_s[sl], dkt_s[sl], dgl_s[sl]
            tm, decay, kb, kbg = loc["tm"], loc["decay"], loc["kb"], loc["kbg"]
            dvb = _bmm_tn(tm, du, X3)
            dkbg = _bmm_tn(tm, dw, X3)
            dt = _bmm_nt(du, loc["vb"], X3) + _bmm_nt(dw, kbg, X3)
            da = jnp.where(loc["strict"], -_bmm_tn(tm, _bmm_nt(dt, tm, X3), X3), 0.0)
            dms = (da * decay).astype(BF16)
            dqs = (dat * decay).astype(BF16)
            kgb = kg.astype(BF16)
            dkb = _bmm(dms, kgb) + dkbg * loc["eg"]
            dqt = _bmm(dqs, kgb) + dqd * loc["eg"]
            dkk = _bmm_tn(dms, kb.astype(BF16)) + _bmm_tn(dqs, qg.astype(BF16)) + dkt * loc["erel"] + dkb * bt
            e = (da * loc["m"] + dat * loc["qm"]) * decay
            lsum = lambda x: jnp.sum(x, axis=2, keepdims=True)
            dkt_kt = lsum(dkt * loc["kt"])
            dgcv = lsum(e) + lsum(dqd * loc["qd"]) - dkt_kt + lsum(dkbg * kbg)
            dglast = jnp.sum(dkt_kt, axis=1, keepdims=True) + dgl * loc["gl"]
            rowc = lax.broadcasted_iota(jnp.int32, (1, c, 1), 1)
            dgc_ref[0, sl] = dgcv + jnp.where(rowc == c - 1, dglast, 0.0)
            dgr_ref[0, sl] = -jnp.sum(e, axis=1, keepdims=True)
            db_ref[0, sl] = lsum(dkb * kg) + lsum(dvb * vg)
            rows = pl.ds(r0, nb * c)
            dq_ref[rows, :] = (dqt * scale).reshape(nb * c, dk)
            dk_ref[rows, :] = dkk.reshape(nb * c, dk)
            dv_ref[rows, :] = (dvb * bt).reshape(nb * c, dk)
            return carry
        lax.fori_loop(0, n // nb, group_c, 0)

    big = lambda: pltpu.VMEM((n, c, dk), F32)
    sq = lambda: pltpu.VMEM((n, c, c), F32)
    one = lambda: pltpu.VMEM((n, 1, 1), F32)
    scratch = [big(), big(), sq(), big(), big(), one(), big(), big(), sq(), big(), big(), one()]
    return pl.pallas_call(
        body, grid=(heads,), in_specs=[hs, hs, hs, gcs, grs, gcs, ss, hs], out_specs=[hs, hs, hs, gcs, grs, gcs],
        out_shape=[jax.ShapeDtypeStruct((t, d), F32)] * 3 + [
            jax.ShapeDtypeStruct((heads, n, c, 1), F32), jax.ShapeDtypeStruct((heads, n, 1, c), F32),
            jax.ShapeDtypeStruct((heads, n, c, 1), F32)],
        scratch_shapes=scratch, name=name, compiler_params=_cparams(("parallel",)))(q, k, v, gc, gr, bc, s_all, do)


def _dot01(x, m01):
    x1 = x.astype(BF16)
    r1 = x - x1.astype(F32)
    x2 = r1.astype(BF16)
    x3 = (r1 - x2.astype(F32)).astype(BF16)
    dot = lambda a: jnp.dot(a, m01, preferred_element_type=F32)
    return (dot(x1) + dot(x2)) + dot(x3)


def _sb_logs(z, mask):
    rows, bk = z.shape
    row = lax.broadcasted_iota(jnp.int32, (bk, bk), 0)
    col = lax.broadcasted_iota(jnp.int32, (bk, bk), 1)
    later = (row > col).astype(BF16)
    lm, ls, cs = [], [], []
    for p in [slice(p, p + SB_PART) for p in range(0, rows, SB_PART)]:
        lsm = -_softplus(z[p])
        lm.append(lsm if mask is None else jnp.where(mask[p], lsm, 0.0))
        ls.append(z[p] + lsm)
        cs.append(_dot01(lm[-1], later))
    cat = lambda xs: jnp.concatenate(xs, axis=0)
    return cat(ls), cat(cs), jnp.sum(cat(lm), axis=1, keepdims=True)


def _sb_masks(bq, bk):
    row = lax.broadcasted_iota(jnp.int32, (2 * bq, bk), 0)
    row = jnp.where(row >= bq, row - bq, row)
    col = lax.broadcasted_iota(jnp.int32, (2 * bq, bk), 1)
    return [(col + dd * bk) < row for dd in range(bq // bk)]


def _sb_sweep(bq, bk, qi, fetch, visit, carry):
    nd = bq // bk
    masks = _sb_masks(bq, bk)
    ahead = lambda j: fetch(jnp.maximum(j - 1, 0))
    pre = fetch(qi * nd + nd - 1)
    for dd in reversed(range(nd)):
        nxt = ahead(qi * nd + dd)
        carry = visit(qi * nd + dd, masks[dd], carry, pre)
        pre = nxt

    def below(it, c):
        j = qi * nd - 1 - it
        nxt = ahead(j)
        return visit(j, None, c[0], c[1]), nxt
    return lax.fori_loop(0, qi * nd, below, (carry, pre))[0]


def _sb_sweep2(bq, bk, qi, fetch, first, second, carry):
    nd = bq // bk
    masks = _sb_masks(bq, bk)
    ahead = lambda j: fetch(jnp.maximum(j - 1, 0))
    pre = fetch(qi * nd + nd - 1)
    r = jnp.zeros((2 * bq, 1), F32)
    pend = None
    for dd in reversed(range(nd)):
        j = qi * nd + dd
        nxt = ahead(j)
        ls, cs, rsum, extras = first(j, masks[dd], pre)
        if pend is not None:
            carry = second(pend[0], carry, *pend[1])
        pend = (j, (jnp.where(masks[dd], ls + cs + r, -1e30),) + extras)
        r, pre = r + rsum, nxt

    def below(it, c):
        carry, r, pre, pj, pargs = c
        j = qi * nd - 1 - it
        nxt = ahead(j)
        ls, cs, rsum, extras = first(j, None, pre)
        carry = second(pj, carry, *pargs)
        return carry, r + rsum, nxt, j, (ls + cs + r,) + extras
    carry, _, _, pj, pargs = lax.fori_loop(0, qi * nd, below, (carry, r, pre) + pend)
    return second(pj, carry, *pargs)


def _stack_heads(a, h0):
    return jnp.concatenate([jnp.where(h0, a, 0.0), jnp.where(h0, 0.0, a)], axis=0).astype(BF16)


def _side_by_side(a, bq):
    return jnp.concatenate([a[:bq], a[bq:]], axis=1)


def _sb_specs(t, d, bq, vcol):
    qs = pl.BlockSpec((bq, LANES), lambda g, i: (i, g))
    ks = pl.BlockSpec((t, LANES), lambda g, i: (0, g))
    vs = pl.BlockSpec((t, LANES), lambda g, i: (0, g + vcol))
    return qs, ks, vs


def _sb_fwd(qn, kn, qkv, name):
    t, d = qn.shape
    bq, bk = min(SB_QBLOCK, t), min(SB_BLOCK, t)
    scale = SB_HEAD_DIM ** -0.5
    nt = (((1,), (1,)), ((), ()))

    def body(q_ref, k_ref, v_ref, o_ref):
        qi = pl.program_id(1)
        h0 = lax.broadcasted_iota(jnp.int32, (1, LANES), 1) < SB_HEAD_DIM
        q2 = _stack_heads(q_ref[...] * scale, h0)

        tile_rows = lambda j: pl.ds(pl.multiple_of(j * bk, bk), bk)

        def fetch(j):
            return lax.dot_general(q2, k_ref[tile_rows(j), :].astype(BF16), nt, preferred_element_type=F32)

        def first(j, mask, z):
            return _sb_logs(z, mask) + ((),)

        def second(j, acc, loga):
            a2 = _side_by_side(jnp.exp(loga).astype(BF16), bq)
            return acc + jnp.dot(a2, _stack_heads(v_ref[tile_rows(j), :], h0), preferred_element_type=F32)
        acc = _sb_sweep2(bq, bk, qi, fetch, first, second, jnp.zeros((bq, LANES), F32))
        o_ref[...] = acc.astype(o_ref.dtype)

    qs, ks, vs = _sb_specs(t, d, bq, 2 * d // LANES)
    return pl.pallas_call(
        body, grid=(d // LANES, t // bq), in_specs=[qs, ks, vs], out_specs=qs,
        out_shape=jax.ShapeDtypeStruct((t, d), BF16), name=name,
        compiler_params=_cparams(("parallel", "parallel")))(qn, kn, qkv)


def _sb_bwd(qn, kn, qkv, do, name):
    t, d = qn.shape
    bq, bk = min(SB_QBLOCK, t), min(SB_BLOCK, t)
    scale = SB_HEAD_DIM ** -0.5
    nt = (((1,), (1,)), ((), ()))

    def body(q_ref, k_ref, v_ref, do_ref, dq_ref, dk_ref, dv_ref, p_s, sg_s):
        qi = pl.program_id(1)

        @pl.when(qi == 0)
        def _():
            dk_ref[...] = jnp.zeros(dk_ref.shape, F32)
            dv_ref[...] = jnp.zeros(dv_ref.shape, F32)

        h0 = lax.broadcasted_iota(jnp.int32, (1, LANES), 1) < SB_HEAD_DIM
        q2 = _stack_heads(q_ref[...] * scale, h0)
        do2 = _stack_heads(do_ref[...], h0)
        row = lax.broadcasted_iota(jnp.int32, (bk, bk), 0)
        col = lax.broadcasted_iota(jnp.int32, (bk, bk), 1)
        later_incl = (row >= col).astype(BF16)
        zero = jnp.zeros((2 * bq, 1), F32)
        tn = (((0,), (0,)), ((), ()))

        tile_rows = lambda j: pl.ds(pl.multiple_of(j * bk, bk), bk)

        def fetch1(j):
            return lax.dot_general(q2, k_ref[tile_rows(j), :].astype(BF16), nt, preferred_element_type=F32)

        def first1(j, mask, z):
            da = lax.dot_general(do2, v_ref[tile_rows(j), :].astype(BF16), nt, preferred_element_type=F32)
            ls, cs, rsum = _sb_logs(z, mask)
            sg_s[j] = jnp.exp(ls)
            return ls, cs, rsum, (da,)

        def second1(j, sp, loga, da):
            a = jnp.exp(loga)
            p = a * da
            p_s[j] = p
            dv_ref[tile_rows(j), :] += lax.dot_general(a.astype(BF16), do2, tn, preferred_element_type=F32)
            return sp + jnp.sum(p, axis=1, keepdims=True)
        total = _sb_sweep2(bq, bk, qi, fetch1, first1, second1, zero)

        def fetch2(j):
            return _dot01(p_s[j], later_incl)

        def visit2(j, mask, carry, later_p):
            dq, sp = carry
            rows = tile_rows(j)
            p, sg = p_s[j], sg_s[j]
            pref = total - sp - later_p
            dz = p * (1.0 - sg) - pref * sg
            if mask is not None:
                dz = jnp.where(mask, dz, 0.0)
            dz = dz.astype(BF16)
            dk_ref[rows, :] += lax.dot_general(dz, q2, tn, preferred_element_type=F32)
            dq = dq + jnp.dot(_side_by_side(dz, bq), _stack_heads(k_ref[rows, :], h0), preferred_element_type=F32)
            return dq, sp + jnp.sum(p, axis=1, keepdims=True)
        dq, _ = _sb_sweep(bq, bk, qi, fetch2, visit2, (jnp.zeros((bq, LANES), F32), zero))
        dq_ref[...] = dq * scale

    qs, ks, vs = _sb_specs(t, d, bq, 2 * d // LANES)
    shp = jax.ShapeDtypeStruct((t, d), F32)
    scratch = [pltpu.VMEM((t // bk, 2 * bq, bk), F32), pltpu.VMEM((t // bk, 2 * bq, bk), F32)]
    return pl.pallas_call(
        body, grid=(d // LANES, t // bq), in_specs=[qs, ks, vs, qs], out_specs=[qs, ks, ks],
        out_shape=[shp, shp, shp], scratch_shapes=scratch, name=name,
        compiler_params=_cparams(("parallel", "arbitrary")))(qn, kn, qkv, do)


def _adamw(w, g, m, v, name):
    shape = w.shape
    two_d = lambda a: a.reshape(-1, shape[-1])
    c1 = 1.0 - ADAM_B1 ** ADAM_STEP
    c2 = 1.0 - ADAM_B2 ** ADAM_STEP

    def body(r, _):
        wv, gv, mv, vv = r
        mn = ADAM_B1 * mv + (1.0 - ADAM_B1) * gv
        vn = ADAM_B2 * vv + (1.0 - ADAM_B2) * (gv * gv)
        delta = -ADAM_LR * ((mn / c1) / (jnp.sqrt(vn / c2) + ADAM_EPS) + ADAM_WD * wv)
        return [delta, mn, vn], []
    width = shape[-1]
    res = _rowwise(body, [two_d(w), two_d(g), two_d(m), two_d(v)], [], [(width, F32)] * 3, [], name=name)
    return [r.reshape(shape) for r in res]


def _dn_layer_fwd(h1, w, conv_w, a_log, dt_bias, onorm_g, tag):
    t, d = h1.shape
    heads = d // DN_HEAD_DIM
    n = t // DN_CHUNK
    proj = _mm(h1, w["all"], name=tag + "_proj")
    qn, kn, vv = _dn_prep_fwd(proj, conv_w, heads, tag + "_prep")
    big_g, beta = _dn_gate_fwd(proj, 4 * d // (2 * LANES), a_log, dt_bias, tag + "_gate")
    gt_ = big_g[:, :heads].T.reshape(heads, n, DN_CHUNK)
    gc, gr = gt_[..., None], gt_[:, :, None, :]
    bc = beta[:, :heads].T.reshape(heads, n, DN_CHUNK)[..., None]
    o, s_all = _dn_chunk_fwd(qn, kn, vv, gc, gr, bc, tag + "_chunk")
    on = _dn_post_fwd(o, proj, onorm_g, tag + "_post")
    y = _mm(on, w["out"], name=tag + "_out")
    return y, dict(proj=proj, qn=qn, kn=kn, v=vv, gc=gc, gr=gr, bc=bc, o=o, s_all=s_all, on=on)


def _dn_layer_bwd(dy, h1, w, conv_w, a_log, dt_bias, onorm_g, sv, tag):
    t, d = h1.shape
    heads = d // DN_HEAD_DIM
    don = _mm(dy, w["out"], tb=True, name=tag + "_dout")
    g_out = _mm(sv["on"], dy, ta=True, out_dtype=BF16, name=tag + "_gwout")
    do, dz, g_on = _dn_post_bwd(sv["o"], sv["proj"], don, onorm_g, tag + "_dpost")
    dq, dk, dv, dgc, dgr, dbc = _dn_chunk_bwd(sv["qn"], sv["kn"], sv["v"], sv["gc"], sv["gr"], sv["bc"], sv["s_all"],
                                              do, tag + "_dchunk")
    pad = lambda a: jnp.pad(a.reshape(heads, t).T, ((0, 0), (0, LANES - heads)))
    d_big_g = pad(dgc) + pad(dgr)
    dab, g_alog, g_dt = _dn_gate_bwd(sv["proj"], 4 * d // (2 * LANES), d_big_g, pad(dbc), a_log, dt_bias, tag + "_dgate")
    dxq, dxk, dxv, wq, wk, wv = _dn_prep_bwd(sv["proj"], conv_w, dq, dk, dv, heads, tag + "_dprep")
    dproj = jnp.concatenate([dxq, dxk, dxv, dz, dab], axis=1)
    dh1 = _mm(dproj, w["all"], tb=True, name=tag + "_dh")
    g_all = _mm(h1, dproj, ta=True, out_dtype=BF16, name=tag + "_gwin")
    grads = dict(w_all=g_all, w_out=g_out, conv_w=jnp.concatenate([wq, wk, wv], axis=1), a_log=g_alog, dt_bias=g_dt,
                 onorm_g=g_on)
    return dh1, grads


def _sb_layer_fwd(h1, w, q_g, k_g, tag):
    t, d = h1.shape
    heads = d // SB_HEAD_DIM
    qkv = _mm(h1, w["qkv"], name=tag + "_proj")
    gq, gk = jnp.tile(q_g, (1, heads)), jnp.tile(k_g, (1, heads))
    qn, kn = _sb_norm_fwd(qkv, gq, gk, tag + "_norm")
    o = _sb_fwd(qn, kn, qkv, tag + "_attn")
    y = _mm(o, w["out"], name=tag + "_out")
    return y, dict(qkv=qkv, qn=qn, kn=kn, o=o, gq=gq, gk=gk)


def _sb_layer_bwd(dy, h1, w, q_g, k_g, sv, tag):
    t, d = h1.shape
    heads = d // SB_HEAD_DIM
    do = _mm(dy, w["out"], tb=True, name=tag + "_dout")
    g_out = _mm(sv["o"], dy, ta=True, out_dtype=BF16, name=tag + "_gwout")
    dqn, dkn, dv = _sb_bwd(sv["qn"], sv["kn"], sv["qkv"], do, tag + "_dattn")
    dqkv, g_q, g_k = _sb_norm_bwd(sv["qkv"], dqn, dkn, dv, sv["gq"], sv["gk"], tag + "_dnorm")
    fold = lambda g: jnp.sum(g.reshape(heads, SB_HEAD_DIM), axis=0, keepdims=True)
    dh1 = _mm(dqkv, w["qkv"], tb=True, name=tag + "_dh")
    g_qkv = _mm(h1, dqkv, ta=True, out_dtype=BF16, name=tag + "_gwin")
    return dh1, dict(w_qkv=g_qkv, w_out=g_out, q_norm_g=fold(g_q), k_norm_g=fold(g_k))


def _local_step(x, tgt, mod, norm1_g, norm2_g, layer_weights, layer_grads):
    depth = mod.shape[0]
    d = x.shape[1]
    saved = []
    for i in range(depth):
        mix_w, ffn_of = layer_weights(i, x)
        mv = [mod[i:i + 1, j * d:(j + 1) * d] for j in range(N_MOD)]
        sh1, sc1, gt1, sh2, sc2, gt2 = mv
        tag = "l%d" % i
        h1 = _adaln_fwd(x, norm1_g[i:i + 1], sc1, sh1, tag + "_ln1")
        if i % 2 == 0:
            p = mix_w
            y, sv = _dn_layer_fwd(h1, p, p["conv_w"], p["a_log"], p["dt_bias"], p["onorm_g"], tag + "_dn")
        else:
            p = mix_w
            y, sv = _sb_layer_fwd(h1, p, p["q_g"], p["k_g"], tag + "_sb")
        x1 = _resid_fwd(x, y, gt1, tag + "_res1")
        h2 = _adaln_fwd(x1, norm2_g[i:i + 1], sc2, sh2, tag + "_ln2")
        ffn_w = ffn_of(x1)
        u = _mm(h2, ffn_w["w_in"], name=tag + "_ffn_in")
        a = _swiglu_fwd(u, tag + "_swiglu")
        y2 = _mm(a, ffn_w["w_out"], name=tag + "_ffn_out")
        x2 = _resid_fwd(x1, y2, gt2, tag + "_res2")
        saved.append(dict(x0=x, h1=h1, y=y, mix=sv, x1=x1, h2=h2, u=u, a=a, y2=y2, mix_w=mix_w, ffn_w=ffn_w))
        x = x2

    loss, dx = _loss_fwd_bwd(x, tgt, "loss")

    dmod, dn1, dn2 = [None] * depth, [None] * depth, [None] * depth
    zero = jnp.zeros((), F32)
    for i in reversed(range(depth)):
        s = saved[i]
        mv = [mod[i:i + 1, j * d:(j + 1) * d] + zero for j in range(N_MOD)]
        sh1, sc1, gt1, sh2, sc2, gt2 = mv
        tag = "l%d" % i
        ffn_w, p = s["ffn_w"], s["mix_w"]
        dy2, dgt2 = _resid_bwd(dx, s["y2"], gt2, tag + "_dres2")
        da = _mm(dy2, ffn_w["w_out"], tb=True, name=tag + "_dffn_a")
        g_wout = _mm(s["a"], dy2, ta=True, out_dtype=BF16, name=tag + "_gffn_out")
        du = _swiglu_bwd(s["u"], da, tag + "_dswiglu")
        dh2 = _mm(du, ffn_w["w_in"], tb=True, name=tag + "_dffn_h")
        g_win = _mm(s["h2"], du, ta=True, out_dtype=BF16, name=tag + "_gffn_in")
        dx, dg2, dsc2, dsh2 = _adaln_bwd(s["x1"], dh2, dx, norm2_g[i:i + 1], sc2, tag + "_dln2")
        gt1 = gt1 + layer_grads(i, "ffn", dx, dict(w_in=g_win, w_out=g_wout))
        dy, dgt1 = _resid_bwd(dx, s["y"], gt1, tag + "_dres1")
        if i % 2 == 0:
            dh1, g_mix = _dn_layer_bwd(dy, s["h1"], p, p["conv_w"], p["a_log"], p["dt_bias"], p["onorm_g"], s["mix"],
                                       tag + "_dn")
        else:
            dh1, g_mix = _sb_layer_bwd(dy, s["h1"], p, p["q_g"], p["k_g"], s["mix"], tag + "_sb")
        dx, dg1, dsc1, dsh1 = _adaln_bwd(s["x0"], dh1, dx, norm1_g[i:i + 1], sc1, tag + "_dln1")
        dmod[i] = jnp.concatenate([dsh1, dsc1, dgt1, dsh2, dsc2, dgt2], axis=1)
        dn1[i], dn2[i] = dg1, dg2
        zero = layer_grads(i, "mix", dx, g_mix)
    return loss, dx, jnp.concatenate(dmod, axis=0), jnp.concatenate(dn1, axis=0), jnp.concatenate(dn2, axis=0)


def _axes():
    return lax.axis_index("x"), lax.axis_index("y"), lax.axis_index("c")


def _remote(src, dst, send_sem, recv_sem, dev):
    return pltpu.make_async_remote_copy(src_ref=src, dst_ref=dst, send_sem=send_sem, recv_sem=recv_sem,
                                        device_id=dev, device_id_type=MESH)


def _other_chips(x, y):
    return [(1 - x, y), (x, 1 - y), (1 - x, 1 - y)]


def _allgather_small(v, name):
    m, n = v.shape

    def body(x_ref, out_ref, send_sems, recv_sems, local_sem):
        x, y, c = _axes()
        me, sibling = (x, y, c), (x, y, 1 - c)
        chips = _other_chips(x, y)

        def rows(px, py, pc):
            return out_ref.at[pl.ds((4 * px + 2 * py + pc) * m, m), :]

        def copy(k, block, to, src=None):
            return _remote(rows(*block) if src is None else src, rows(*block), send_sems.at[k], recv_sems.at[k], to)

        mine = pltpu.make_async_copy(x_ref, rows(*me), local_sem)
        mine.start()
        first = [copy(0, me, sibling, src=x_ref)]
        first += [copy(1 + j, me, (*chip, c), src=x_ref) for j, chip in enumerate(chips)]
        for cp in first:
            cp.start()
        passed = [copy(4 + j, (*chip, c), sibling) for j, chip in enumerate(chips)]
        for j, chip in enumerate(chips):
            copy(1 + j, (*chip, c), me).wait_recv()
            passed[j].start()
        copy(0, sibling, me).wait_recv()
        for j, chip in enumerate(chips):
            copy(4 + j, (*chip, 1 - c), me).wait_recv()
        for cp in first + passed:
            cp.wait_send()
        mine.wait()

    return pl.pallas_call(
        body, out_shape=jax.ShapeDtypeStruct((8 * m, n), v.dtype),
        in_specs=[pl.BlockSpec(memory_space=pltpu.VMEM)], out_specs=pl.BlockSpec(memory_space=pltpu.VMEM),
        scratch_shapes=[pltpu.SemaphoreType.DMA((7,)), pltpu.SemaphoreType.DMA((7,)), pltpu.SemaphoreType.DMA],
        name=name, compiler_params=pltpu.CompilerParams(vmem_limit_bytes=VMEM_LIMIT))(v)


def _half(ref, h, rh):
    return ref.at[(slice(None),) * (len(ref.shape) - 2) + (pl.ds(h * rh, rh), slice(None))]


def _hbm_call(body, ins, out_shapes, n_sems, n_local, name):
    hbm = pl.BlockSpec(memory_space=pltpu.HBM)
    scratch = [pltpu.SemaphoreType.DMA((n_sems,)), pltpu.SemaphoreType.DMA((n_sems,))]
    if n_local:
        scratch.append(pltpu.SemaphoreType.DMA((n_local,)))
    return pl.pallas_call(body, out_shape=out_shapes, in_specs=[hbm] * len(ins), out_specs=[hbm] * len(out_shapes),
                          scratch_shapes=scratch, name=name)(*ins)


def _sibling_swap_halves(arrs, name):
    n = len(arrs)

    def body(*refs):
        v_refs, out_refs, (send_sems, recv_sems) = refs[:n], refs[n:2 * n], refs[2 * n:]
        x, y, c = _axes()
        cps = [_remote(_half(v_refs[i], 1 - c, arrs[i].shape[-2] // 2), out_refs[i], send_sems.at[i], recv_sems.at[i],
                       (x, y, 1 - c)) for i in range(n)]
        for cp in cps:
            cp.start()
        for cp in cps:
            cp.wait()

    outs = [jax.ShapeDtypeStruct(a.shape[:-2] + (a.shape[-2] // 2, a.shape[-1]), a.dtype) for a in arrs]
    return _hbm_call(body, arrs, outs, n, 0, name)


def _ici_src(ref, mode, slot, c):
    return _half(ref, c, ref.shape[-2] // 2) if mode == "half" else ref.at[slot]


def _ici_start(srcs, mode, name):
    n, ncp = len(srcs), 3 * len(srcs)
    lands = [lax.empty(((4,) + s.shape[:-2] + (s.shape[-2] // 2, s.shape[-1])) if mode == "half" else s.shape, s.dtype)
             for s in srcs]

    def body(*refs):
        src_refs, land_refs = refs[:n], refs[n:2 * n]
        send_sems, recv_sems = refs[2 * n:2 * n + ncp], refs[2 * n + ncp:2 * n + 2 * ncp]
        token = refs[-1]
        x, y, c = _axes()
        for i in range(n):
            for j, (cx, cy) in enumerate(_other_chips(x, y)):
                _remote(_ici_src(src_refs[i], mode, 2 * cx + cy, c), land_refs[i].at[2 * x + y], send_sems[3 * i + j],
                        recv_sems[3 * i + j], (cx, cy, c)).start()
        token[...] = jnp.zeros(token.shape, token.dtype)

    hbm, sem = pl.BlockSpec(memory_space=pltpu.HBM), pl.BlockSpec(memory_space=pltpu.SEMAPHORE)
    bufs = srcs + lands
    out_shape = tuple([pltpu.SemaphoreType.DMA(())] * (2 * ncp) + [pltpu.HBM(b.shape, b.dtype) for b in bufs]
                      + [jax.ShapeDtypeStruct((8, LANES), F32)])
    res = pl.pallas_call(
        body, name=name, out_shape=out_shape, in_specs=(hbm,) * (2 * n),
        out_specs=(sem,) * (2 * ncp) + (hbm,) * (2 * n) + (pl.BlockSpec(memory_space=pltpu.VMEM),),
        input_output_aliases={i: 2 * ncp + i for i in range(2 * n)},
        compiler_params=pltpu.CompilerParams(has_side_effects=pltpu.SideEffectType.DATAFLOW_SIDE_EFFECTING),
    )(*[pltpu.with_memory_space_constraint(b, pltpu.HBM) for b in bufs])
    return list(res[:2 * ncp]), list(res[2 * ncp:2 * ncp + 2 * n]), res[-1][0, 0]


def _ici_wait(sems, bufs, mode, after, name):
    n, ncp = len(bufs) // 2, len(sems) // 2

    def body(*refs):
        src_refs, land_refs = refs[:n], refs[n:2 * n]
        send_sems, recv_sems = refs[2 * n:2 * n + ncp], refs[2 * n + ncp:2 * n + 2 * ncp]
        x, y, c = _axes()
        for i in range(n):
            for j, (cx, cy) in enumerate(_other_chips(x, y)):
                cp = _remote(_ici_src(src_refs[i], mode, 2 * cx + cy, c), land_refs[i].at[2 * cx + cy],
                             send_sems[3 * i + j], recv_sems[3 * i + j], (cx, cy, c))
                cp.wait_send()
                cp.wait_recv()

    hbm, sem = pl.BlockSpec(memory_space=pltpu.HBM), pl.BlockSpec(memory_space=pltpu.SEMAPHORE)
    res = pl.pallas_call(
        body, name=name, out_shape=tuple(pltpu.HBM(b.shape, b.dtype) for b in bufs),
        in_specs=(hbm,) * (2 * n) + (sem,) * (2 * ncp) + (pl.BlockSpec(memory_space=pl.ANY),),
        out_specs=(hbm,) * (2 * n), input_output_aliases={i: i for i in range(2 * n)},
        compiler_params=pltpu.CompilerParams(has_side_effects=pltpu.SideEffectType.DATAFLOW_SIDE_EFFECTING),
    )(*bufs, *sems, after)
    return list(res[n:])


def _share_halves(arrs, name):
    n = len(arrs)

    def body(*refs):
        v_refs, out_refs, (send_sems, recv_sems) = refs[:n], refs[n:2 * n], refs[2 * n:]
        x, y, c = _axes()
        cps = [_remote(v_refs[i], out_refs[i], send_sems.at[i], recv_sems.at[i], (x, y, 1 - c)) for i in range(n)]
        for cp in cps:
            cp.start()
        for cp in cps:
            cp.wait()

    outs = [jax.ShapeDtypeStruct(a.shape, a.dtype) for a in arrs]
    return _hbm_call(body, arrs, outs, n, 0, name)


def _assemble_weights(mine, other, own, axis, name):
    nchips, l, rh, cs = mine.shape
    r = 2 * rh

    def body(m_ref, o_ref, w_ref, out_ref):
        x, y, c = _axes()
        is_own = pl.program_id(1) == 2 * x + y

        @pl.when(is_own)
        def _():
            out_ref[...] = w_ref[...]

        @pl.when(jnp.logical_not(is_own))
        def _():
            out_ref[pl.ds(pl.multiple_of(c * rh, rh), rh), :] = m_ref[...]
            out_ref[pl.ds(pl.multiple_of((1 - c) * rh, rh), rh), :] = o_ref[...]

    half = pl.BlockSpec((None, None, rh, cs), lambda i, s: (s, i, 0, 0))
    whole = pl.BlockSpec((None, r, cs), lambda i, s: (i, 0, 0))
    if axis == 2:
        out_spec, shape = pl.BlockSpec((None, r, cs), lambda i, s: (i, 0, s)), (l, r, nchips * cs)
    else:
        out_spec, shape = pl.BlockSpec((None, r, cs), lambda i, s: (i, s, 0)), (l, nchips * r, cs)
    return pl.pallas_call(
        body, grid=(l, nchips), in_specs=[half, half, whole], out_specs=out_spec,
        out_shape=jax.ShapeDtypeStruct(shape, mine.dtype), name=name,
        compiler_params=_cparams(("parallel", "arbitrary")))(mine, other, own)


def _sum_chips(land, own, name):
    k, r, w = land.shape
    tr = _pick_tile(r, (256, 128, 64, 32, 16))

    def body(land_ref, own_ref, o_ref):
        x, y, _ = _axes()
        me = 2 * x + y
        acc = jnp.zeros((tr, w), F32)
        for s in range(k):
            acc = acc + jnp.where(me == s, own_ref[s], land_ref[s]).astype(F32)
        o_ref[...] = acc

    spec = pl.BlockSpec((k, tr, w), lambda i: (0, i, 0))
    return pl.pallas_call(
        body, grid=(r // tr,), in_specs=[spec, spec],
        out_specs=pl.BlockSpec((tr, w), lambda i: (i, 0)), out_shape=jax.ShapeDtypeStruct((r, w), F32), name=name,
        compiler_params=_cparams(("parallel",)))(land, own)


_BIG = (("dn_w_in", 2), ("dn_w_out", 1), ("sb_w_qkv", 2), ("sb_w_out", 1), ("ffn_w_in", 2), ("ffn_w_out", 1))


def _to_chip_major(a, axis, nchips):
    l, r, c = a.shape
    if axis == 2:
        return a.reshape(l, r, nchips, c // nchips).transpose(2, 0, 1, 3)
    return a.reshape(l, nchips, r // nchips, c).transpose(1, 0, 2, 3)


def _from_chip_major(a, axis):
    n, l, r, c = a.shape
    if axis == 2:
        return a.transpose(1, 2, 0, 3).reshape(l, r, n * c)
    return a.transpose(1, 0, 2, 3).reshape(l, n * r, c)


def kernel(x, c, ada_w, ada_b, norm1_g, norm2_g, dn_w_in, dn_conv_w, dn_a_log, dn_dt_bias, dn_onorm_g, dn_w_out, sb_w_qkv, sb_q_norm_g, sb_k_norm_g, sb_w_out, ffn_w_in, ffn_w_out, loss_target, m_ada_w, m_ada_b, m_norm1_g, m_norm2_g, m_dn_w_in, m_dn_conv_w, m_dn_a_log, m_dn_dt_bias, m_dn_onorm_g, m_dn_w_out, m_sb_w_qkv, m_sb_q_norm_g, m_sb_k_norm_g, m_sb_w_out, m_ffn_w_in, m_ffn_w_out, v_ada_w, v_ada_b, v_norm1_g, v_norm2_g, v_dn_w_in, v_dn_conv_w, v_dn_a_log, v_dn_dt_bias, v_dn_onorm_g, v_dn_w_out, v_sb_w_qkv, v_sb_q_norm_g, v_sb_k_norm_g, v_sb_w_out, v_ffn_w_in, v_ffn_w_out):
    names = ("ada_w", "ada_b", "norm1_g", "norm2_g", "dn_w_in", "dn_conv_w", "dn_a_log", "dn_dt_bias", "dn_onorm_g",
             "dn_w_out", "sb_w_qkv", "sb_q_norm_g", "sb_k_norm_g", "sb_w_out", "ffn_w_in", "ffn_w_out")
    w = dict(zip(names, (ada_w, ada_b, norm1_g, norm2_g, dn_w_in, dn_conv_w, dn_a_log, dn_dt_bias, dn_onorm_g,
                         dn_w_out, sb_w_qkv, sb_q_norm_g, sb_k_norm_g, sb_w_out, ffn_w_in, ffn_w_out)))
    mom = dict(zip(names, (m_ada_w, m_ada_b, m_norm1_g, m_norm2_g, m_dn_w_in, m_dn_conv_w, m_dn_a_log, m_dn_dt_bias,
                           m_dn_onorm_g, m_dn_w_out, m_sb_w_qkv, m_sb_q_norm_g, m_sb_k_norm_g, m_sb_w_out, m_ffn_w_in,
                           m_ffn_w_out)))
    var = dict(zip(names, (v_ada_w, v_ada_b, v_norm1_g, v_norm2_g, v_dn_w_in, v_dn_conv_w, v_dn_a_log, v_dn_dt_bias,
                           v_dn_onorm_g, v_dn_w_out, v_sb_w_qkv, v_sb_q_norm_g, v_sb_k_norm_g, v_sb_w_out, v_ffn_w_in,
                           v_ffn_w_out)))
    ax, ay, ac = _axes()
    chip = 2 * ax + ay
    dev = 2 * chip + ac
    t, d = x.shape[1], x.shape[2]
    depth, ndn, nsb = ada_w.shape[0], dn_w_in.shape[0], sb_w_qkv.shape[0]
    heads = d // DN_HEAD_DIM
    mod_cols = ada_w.shape[2]
    conv_cols = dn_conv_w.shape[2]
    nchips, ndev = 4, 8

    conv_rows = ndn * DN_CONV * conv_cols // d
    pay1 = jnp.concatenate([c, dn_conv_w.reshape(conv_rows, d), jnp.zeros((8 - 1 - conv_rows, d), F32)], axis=0)
    g1 = _allgather_small(pay1, "ag_cond").reshape(ndev, 8, d)
    c_all = g1[:, 0]
    conv_full = g1[::2, 1:1 + conv_rows].reshape(nchips, ndn, DN_CONV, conv_cols).transpose(1, 2, 0, 3)
    conv_full = conv_full.reshape(ndn, DN_CONV, nchips * conv_cols)

    c16 = jnp.pad(c_all, ((0, 16 - ndev), (0, 0)))
    cond16 = _rowwise(lambda r, v: ([_silu(r[0])], []), [c16], [], [(d, F32)], [], name="cond_silu")[0]
    pay2 = jnp.concatenate([_mm(cond16, ada_w[i], name="ada_mod%d" % i)[:ndev] for i in range(depth)], axis=0)
    g2 = _allgather_small(pay2, "ag_mod").reshape(ndev, depth, ndev, mod_cols)[::2]
    mod_raw = lax.dynamic_index_in_dim(g2, dev, axis=2, keepdims=False)
    mod_raw = mod_raw.transpose(1, 0, 2).reshape(depth, nchips * mod_cols)
    mod = _rowwise(lambda r, v: ([r[0] + r[1]], []), [mod_raw, ada_b], [], [(nchips * mod_cols, F32)], [],
                   name="ada_bias")[0]

    axis_of = dict(_BIG)

    def kinds_of(layers):
        out = []
        for n, _ in _BIG:
            idx = [l // 2 for l in layers if l % 2 == (0 if n.startswith("dn_") else 1)] if n[:3] in ("dn_", "sb_") \
                else list(layers)
            if idx:
                out.append((n, min(idx), max(idx) + 1))
        return out

    two_d = lambda a: a.reshape(-1, a.shape[-1])
    both_halves = lambda mine, other, ax: jnp.where(ac == 0, jnp.concatenate([mine, other], axis=ax),
                                                    jnp.concatenate([other, mine], axis=ax))
    is_ffn = lambda k: k[0].startswith("ffn")
    rest = (kinds_of(list(range(1, depth))), ("mix", 1))
    gather_groups = [g for g in [(kinds_of([0]), ("mix", 0)), rest] if g[0]]
    reduce_groups = [g for g in [([k for k in kinds_of([0]) if not is_ffn(k)], ("mix", 0)),
                                 ([k for k in kinds_of([0]) if is_ffn(k)], ("ffn", 0)), rest] if g[0]]
    w16 = {n: w[n].astype(BF16) for n, _ in _BIG}
    gathers = []
    for gi, (sl, _) in enumerate(gather_groups):
        shards = [w16[n][lo:hi] for n, lo, hi in sl]
        sems, bufs, zero = _ici_start(shards, "half", "ag_start%d" % gi)
        gathers.append((sl, shards, sems, bufs))
        mod = mod + zero
    full = {}

    def finish_gather(gi, after):
        sl, shards, sems, bufs = gathers[gi]
        lands = _ici_wait(sems, bufs, "half", after, "ag_wait%d" % gi)
        for (n, lo, hi), shard, mine, other in zip(sl, shards, lands, _share_halves(lands, "ag_pair%d" % gi)):
            if shard.shape[-1] % LANES == 0:
                a = _assemble_weights(mine, other, shard, axis_of[n], "ag_whole%d_%s" % (gi, n))
            else:
                a = lax.dynamic_update_index_in_dim(both_halves(mine, other, 2), shard, chip, 0)
                a = _from_chip_major(a, axis_of[n])
            for l in range(lo, hi):
                full[n, l] = a[l - lo]

    padl = lambda v: jnp.pad(v[None, :], ((0, 0), (0, LANES - v.shape[0])))
    padc = lambda a: jnp.pad(a, ((0, 0), (0, LANES - a.shape[1])))

    def layer_weights(i, after):
        for gi, (_, trigger) in enumerate(gather_groups):
            if trigger == ("mix", i):
                finish_gather(gi, after)

        def ffn_of(after_mixer):
            for gi, (_, trigger) in enumerate(gather_groups):
                if trigger == ("ffn", i):
                    finish_gather(gi, after_mixer)
            return dict(w_in=full["ffn_w_in", i], w_out=full["ffn_w_out", i])
        j = i // 2
        if i % 2 == 0:
            wi = full["dn_w_in", j]
            w_all = jnp.concatenate([wi[:, :4 * d], padc(wi[:, 4 * d:4 * d + heads]), padc(wi[:, 4 * d + heads:])],
                                    axis=1)
            mix = dict(all=w_all, out=full["dn_w_out", j], conv_w=conv_full[j], a_log=padl(dn_a_log[j]),
                       dt_bias=padl(dn_dt_bias[j]), onorm_g=dn_onorm_g[j][None])
        else:
            mix = dict(qkv=full["sb_w_qkv", j], out=full["sb_w_out", j], q_g=sb_q_norm_g[j][None],
                       k_g=sb_k_norm_g[j][None])
        return mix, ffn_of

    g_dn, g_sb, g_ffn = [None] * ndn, [None] * nsb, [None] * depth
    reduces = {}

    def gw_in(g):
        ga = g["w_all"]
        return jnp.concatenate([ga[:, :4 * d], ga[:, 4 * d:4 * d + heads], ga[:, 4 * d + LANES:4 * d + LANES + heads]],
                               axis=1)
    layer_grad = dict(dn_w_in=lambda j: gw_in(g_dn[j]), dn_w_out=lambda j: g_dn[j]["w_out"],
                      sb_w_qkv=lambda j: g_sb[j]["w_qkv"], sb_w_out=lambda j: g_sb[j]["w_out"],
                      ffn_w_in=lambda l: g_ffn[l]["w_in"], ffn_w_out=lambda l: g_ffn[l]["w_out"])

    def layer_grads(i, part, after, g):
        if part == "ffn":
            g_ffn[i] = g
        else:
            (g_dn if i % 2 == 0 else g_sb)[i // 2] = g
        zero = jnp.zeros((), F32)
        for gi, (sl, trigger) in enumerate(reduce_groups):
            if trigger == (part, i):
                parts = [_to_chip_major(jnp.stack([layer_grad[n](j) for j in range(lo, hi)]), axis_of[n], nchips)
                         for n, lo, hi in sl]
                pairs = []
                for (n, _, _), p, fs in zip(sl, parts, _sibling_swap_halves(parts, "gr_pair%d" % gi)):
                    rh = fs.shape[2]
                    own = lax.dynamic_slice_in_dim(p, ac * rh, rh, axis=2)
                    pairs.append(_rowwise(lambda r, v: ([r[0].astype(F32) + r[1].astype(F32)], []),
                                          [two_d(own), two_d(fs)], [], [(fs.shape[-1], BF16)], [],
                                          name="gr_pair_add%d_%s" % (gi, n))[0].reshape(fs.shape))
                sems, bufs, zero = _ici_start(pairs, "slot", "gr_start%d" % gi)
                reduces[gi] = (sl, pairs, sems, bufs)
        return zero

    loss_local, grad_x, dmod, g_n1, g_n2 = _local_step(x[0], loss_target[0], mod, norm1_g, norm2_g, layer_weights,
                                                       layer_grads)
    loss = lax.psum(loss_local, ("x", "y", "c"))

    g_conv = jnp.stack([g["conv_w"] for g in g_dn])
    misc = jnp.concatenate([jnp.concatenate([g["onorm_g"] for g in g_dn], axis=1),
                            jnp.concatenate([g["a_log"] for g in g_dn], axis=1),
                            jnp.concatenate([g["dt_bias"] for g in g_dn], axis=1),
                            jnp.concatenate([g["q_norm_g"] for g in g_sb], axis=1),
                            jnp.concatenate([g["k_norm_g"] for g in g_sb], axis=1)], axis=1)
    misc = jnp.pad(misc, ((0, 0), (0, -misc.shape[1] % d))).reshape(-1, d)
    small = [dmod.reshape(-1, d), g_n1, g_n2, g_conv.reshape(-1, d), misc]
    small_rows = [s.shape[0] for s in small]
    pad_rows = -sum(small_rows) % 8
    pay3 = jnp.concatenate(small + [jnp.zeros((pad_rows, d), F32)], axis=0)
    nrow3 = pay3.shape[0]
    g3 = _allgather_small(pay3, "ag_small")
    summed = _rowwise(lambda r, v: ([], [_colsum(r[0])]), [g3.reshape(ndev, nrow3 * d)], [], [], [(1, nrow3 * d)],
                      name="small_sum")[0].reshape(nrow3, d)
    offs = [0]
    for n_ in small_rows:
        offs.append(offs[-1] + n_)
    grads = {}
    grads["ada_b"] = summed[offs[0]:offs[1]].reshape(depth, N_MOD * d)
    grads["norm1_g"] = summed[offs[1]:offs[2]]
    grads["norm2_g"] = summed[offs[2]:offs[3]]
    conv_sum = summed[offs[3]:offs[4]].reshape(ndn, DN_CONV, nchips * conv_cols)
    grads["dn_conv_w"] = lax.dynamic_slice_in_dim(conv_sum, chip * conv_cols, conv_cols, axis=2)
    mrow = summed[offs[4]:offs[5]].reshape(-1)
    o = 0
    grads["dn_onorm_g"] = mrow[o:o + ndn * DN_HEAD_DIM].reshape(ndn, DN_HEAD_DIM)
    o += ndn * DN_HEAD_DIM
    grads["dn_a_log"] = mrow[o:o + ndn * LANES].reshape(ndn, LANES)[:, :heads]
    o += ndn * LANES
    grads["dn_dt_bias"] = mrow[o:o + ndn * LANES].reshape(ndn, LANES)[:, :heads]
    o += ndn * LANES
    grads["sb_q_norm_g"] = mrow[o:o + nsb * SB_HEAD_DIM].reshape(nsb, SB_HEAD_DIM)
    o += nsb * SB_HEAD_DIM
    grads["sb_k_norm_g"] = mrow[o:o + nsb * SB_HEAD_DIM].reshape(nsb, SB_HEAD_DIM)
    dmod_all = g3.reshape(ndev, nrow3, d)[:, :small_rows[0]].reshape(ndev, depth, N_MOD * d)
    dmod_mine = lax.dynamic_slice_in_dim(dmod_all, chip * mod_cols, mod_cols, axis=2)
    dmod16 = jnp.pad(dmod_mine, ((0, 16 - ndev), (0, 0), (0, 0)))
    grads["ada_w"] = jnp.stack([_mm(cond16, dmod16[:, i], ta=True, name="ada_gw%d" % i) for i in range(depth)])

    pieces = {n: [] for n, _ in _BIG}
    for gi in reversed(range(len(reduce_groups))):
        if gi not in reduces:
            continue
        sl, pairs, sems, bufs = reduces[gi]
        lands = _ici_wait(sems, bufs, "slot", grad_x, "gr_wait%d" % gi)
        flat = lambda a: a.reshape(nchips, -1, a.shape[-1])
        reduced = [_sum_chips(flat(land), flat(pair), "gr_chip_add%d_%s" % (gi, n))
                   for (n, _, _), land, pair in zip(sl, lands, pairs)]
        for (n, lo, hi), mine, other in zip(sl, reduced, _share_halves(reduced, "gr_share%d" % gi)):
            shape = (hi - lo, -1, mine.shape[-1])
            pieces[n].append((lo, both_halves(mine.reshape(shape), other.reshape(shape), 1)))
    for n, _ in _BIG:
        grads[n] = jnp.concatenate([p for _, p in sorted(pieces[n], key=lambda t: t[0])], axis=0)

    delta, new_m, new_v = {}, {}, {}
    for n in names:
        delta[n], new_m[n], new_v[n] = _adamw(w[n], grads[n], mom[n], var[n], "adamw_" + n)
    return (loss, grad_x[None], *[grads[n] for n in names], *[delta[n] for n in names], *[new_m[n] for n in names],
            *[new_v[n] for n in names])
```
